```python
import jax, jax.numpy as jnp
from jax import lax
import numpy as np

D_MODEL = 1024
BATCH = 32
SEQ = 2048
DEPTH = 1

HEAD_DIM = 64
N_ATTN_HEADS = 12
D_ATTN = N_ATTN_HEADS * HEAD_DIM
DILATED_BRANCHES = ((128, 1), (512, 4), (2048, 16))
ATTN_BLOCK = 128
ROPE_THETA = 500000.0
ROPE_DIM = HEAD_DIM // 4
N_SSD_HEADS = 12
SSD_HEAD_DIM = 64
D_SSD = N_SSD_HEADS * SSD_HEAD_DIM
SSD_GROUPS = 4
SSD_HEADS_PER_GROUP = N_SSD_HEADS // SSD_GROUPS
SSD_STATE = 128
CONV_WIDTH = 4
SSD_CHUNK = 128
D_CONV = D_SSD + 2 * SSD_GROUPS * SSD_STATE
D_MIX = D_ATTN + D_SSD
D_IN_PROJ = 3 * D_ATTN + D_SSD + D_CONV + N_SSD_HEADS
D_FF = ((8 * D_MODEL // 3 + 255) // 256) * 256
ALPHA = (2.0 * DEPTH) ** 0.25
BETA = (8.0 * DEPTH) ** -0.25
LN_EPS = 1e-5
RMS_EPS = 1e-6

kernel_name = 'hybrid_ssd_dilated_attn_macaron_deepnorm'


def layer_norm(t, g, b):
    tf = t.astype(jnp.float32)
    mu = jnp.mean(tf, axis=-1, keepdims=True)
    var = jnp.mean(jnp.square(tf - mu), axis=-1, keepdims=True)
    return ((tf - mu) * lax.rsqrt(var + LN_EPS) * g + b).astype(t.dtype)


def rms_norm(t, w):
    tf = t.astype(jnp.float32)
    return (tf * lax.rsqrt(jnp.mean(tf * tf, axis=-1, keepdims=True) + RMS_EPS) * w).astype(t.dtype)


def swiglu(t, w_gate, w_up, w_down):
    return (jax.nn.silu(t @ w_gate) * (t @ w_up)) @ w_down


def rotary_tables(positions):
    inv_freq = ROPE_THETA ** (-jnp.arange(0, ROPE_DIM, 2, dtype=jnp.float32) / ROPE_DIM)
    ang = positions.astype(jnp.float32)[..., None] * inv_freq
    return jnp.cos(ang)[:, :, None, :], jnp.sin(ang)[:, :, None, :]


def partial_rope(t, cos, sin):
    half = ROPE_DIM // 2
    cos = cos.astype(t.dtype)
    sin = sin.astype(t.dtype)
    t1 = t[..., :half]
    t2 = t[..., half:ROPE_DIM]
    return jnp.concatenate([t1 * cos - t2 * sin, t2 * cos + t1 * sin, t[..., ROPE_DIM:]], axis=-1)


def banded_causal_attention(q, k, v, wr):
    n, L, h, dh = q.shape
    nblk = -(-L // ATTN_BLOCK)
    lp = nblk * ATTN_BLOCK
    qp = jnp.pad(q, ((0, 0), (0, lp - L), (0, 0), (0, 0)))
    kp = jnp.pad(k, ((0, 0), (wr, lp - L), (0, 0), (0, 0)))
    vp = jnp.pad(v, ((0, 0), (wr, lp - L), (0, 0), (0, 0)))
    scale = dh ** -0.5
    q_off = jnp.arange(ATTN_BLOCK)
    k_off = jnp.arange(ATTN_BLOCK + wr)
    rel = q_off[:, None] + wr - k_off[None, :]
    band = (rel >= 0) & (rel <= wr)

    def one_block(i):
        start = i * ATTN_BLOCK
        qb = lax.dynamic_slice_in_dim(qp, start, ATTN_BLOCK, axis=1)
        kb = lax.dynamic_slice_in_dim(kp, start, ATTN_BLOCK + wr, axis=1)
        vb = lax.dynamic_slice_in_dim(vp, start, ATTN_BLOCK + wr, axis=1)
        m_k = start - wr + k_off
        mask = band & (m_k >= 0)[None, :]
        s = jnp.einsum('nqhd,nkhd->nhqk', qb, kb).astype(jnp.float32) * scale
        s = jnp.where(mask, s, -jnp.inf)
        lse = jax.nn.logsumexp(s, axis=-1)
        p = jnp.exp(s - lse[..., None])
        o = jnp.einsum('nhqk,nkhd->nqhd', p, vb.astype(jnp.float32))
        return o, jnp.transpose(lse, (0, 2, 1))

    o, lse = lax.map(one_block, jnp.arange(nblk))
    o = jnp.transpose(o, (1, 0, 2, 3, 4)).reshape(n, lp, h, dh)[:, :L]
    lse = jnp.transpose(lse, (1, 0, 2, 3)).reshape(n, lp, h)[:, :L]
    return o, lse


def dilated_branch(q, k, v, window, dilation):
    b, s, h, dh = q.shape
    L = s // dilation

    def to_residue(t):
        return jnp.transpose(t.reshape(b, L, dilation, h, dh), (0, 2, 1, 3, 4)).reshape(b * dilation, L, h, dh)

    o, lse = banded_causal_attention(to_residue(q), to_residue(k), to_residue(v), window // dilation)
    o = jnp.transpose(o.reshape(b, dilation, L, h, dh), (0, 2, 1, 3, 4)).reshape(b, s, h, dh)
    lse = jnp.transpose(lse.reshape(b, dilation, L, h), (0, 2, 1, 3)).reshape(b, s, h)
    return o, lse


def dilated_attention_mixture(q, k, v):
    outs, lses = [], []
    for window, dilation in DILATED_BRANCHES:
        o, lse = dilated_branch(q, k, v, window, dilation)
        outs.append(o)
        lses.append(lse)
    w = jax.nn.softmax(jnp.stack(lses, axis=0), axis=0)
    return jnp.einsum('gbsh,gbshd->bshd', w, jnp.stack(outs, axis=0))


def causal_depthwise_conv(u, w, bias):
    c = u.shape[-1]
    out = lax.conv_general_dilated(u, w[:, None, :].astype(u.dtype), window_strides=(1,),
                                   padding=((CONV_WIDTH - 1, 0),),
                                   dimension_numbers=('NWC', 'WIO', 'NWC'),
                                   feature_group_count=c)
    return out + bias


def ssd_chunked(xdt, dA, Bm, Cm):
    b, s, g, j, p = xdt.shape
    n = Bm.shape[-1]
    nc = s // SSD_CHUNK
    cl = SSD_CHUNK
    xdt = xdt.astype(jnp.float32).reshape(b, nc, cl, g, j, p)
    Bc = Bm.astype(jnp.float32).reshape(b, nc, cl, g, n)
    Cc = Cm.astype(jnp.float32).reshape(b, nc, cl, g, n)
    a = jnp.transpose(dA.reshape(b, nc, cl, g, j), (0, 3, 4, 1, 2))
    a_cum = jnp.cumsum(a, axis=-1)
    tri = jnp.tril(jnp.ones((cl, cl), dtype=bool))
    seg = a_cum[..., :, None] - a_cum[..., None, :]
    Lmat = jnp.exp(jnp.where(tri, seg, -jnp.inf))
    CB = jnp.einsum('bclgn,bcsgn->bgcls', Cc, Bc)
    y_diag = jnp.einsum('bgcls,bgjcls,bcsgjp->bclgjp', CB, Lmat, xdt)
    decay_states = jnp.exp(a_cum[..., -1:] - a_cum)
    states = jnp.einsum('bclgn,bgjcl,bclgjp->bcgjpn', Bc, decay_states, xdt)
    chunk_decay = jnp.exp(a_cum[..., -1])

    def step(h, inp):
        st, dec = inp
        return dec[..., None, None] * h + st, h

    h0 = jnp.zeros((b, g, j, p, n), jnp.float32)
    _, prev = lax.scan(step, h0, (jnp.moveaxis(states, 1, 0), jnp.moveaxis(chunk_decay, -1, 0)))
    prev = jnp.moveaxis(prev, 0, 1)
    y_off = jnp.einsum('bclgn,bcgjpn,bgjcl->bclgjp', Cc, prev, jnp.exp(a_cum))
    return (y_diag + y_off).reshape(b, s, g, j, p)


def hybrid_mixer(h, cos, sin, w_in, conv_w, conv_b, dt_bias, a_log, d_skip, attn_norm_w, ssd_norm_w, w_out):
    b, s, _ = h.shape
    proj = h @ w_in
    cuts = [D_ATTN, 2 * D_ATTN, 3 * D_ATTN, 3 * D_ATTN + D_SSD, 3 * D_ATTN + D_SSD + D_CONV]
    q, k, v, z, xbc, dt = jnp.split(proj, cuts, axis=-1)
    q = partial_rope(q.reshape(b, s, N_ATTN_HEADS, HEAD_DIM), cos, sin)
    k = partial_rope(k.reshape(b, s, N_ATTN_HEADS, HEAD_DIM), cos, sin)
    v = v.reshape(b, s, N_ATTN_HEADS, HEAD_DIM)
    attn = dilated_attention_mixture(q, k, v).astype(h.dtype).reshape(b, s, D_ATTN)
    attn = rms_norm(attn, attn_norm_w)
    xbc = jax.nn.silu(causal_depthwise_conv(xbc, conv_w, conv_b))
    xs, Bm, Cm = jnp.split(xbc, [D_SSD, D_SSD + SSD_GROUPS * SSD_STATE], axis=-1)
    xs = xs.reshape(b, s, SSD_GROUPS, SSD_HEADS_PER_GROUP, SSD_HEAD_DIM)
    Bm = Bm.reshape(b, s, SSD_GROUPS, SSD_STATE)
    Cm = Cm.reshape(b, s, SSD_GROUPS, SSD_STATE)
    dt = jax.nn.softplus(dt.astype(jnp.float32) + dt_bias).reshape(b, s, SSD_GROUPS, SSD_HEADS_PER_GROUP)
    A = -jnp.exp(a_log.astype(jnp.float32)).reshape(SSD_GROUPS, SSD_HEADS_PER_GROUP)
    y = ssd_chunked(xs.astype(jnp.float32) * dt[..., None], dt * A, Bm, Cm)
    y = y + d_skip.reshape(SSD_GROUPS, SSD_HEADS_PER_GROUP)[..., None] * xs
    y = y.astype(h.dtype).reshape(b, s, D_SSD)
    y = rms_norm(y * jax.nn.silu(z), ssd_norm_w)
    return jnp.concatenate([attn, y], axis=-1) @ w_out


def _fwd_setup_inputs(seed: int = 0) -> dict:
    key = jax.random.key(seed)
    ks = jax.random.split(key, 32)
    f32 = jnp.float32
    nrm = lambda k, shape, std: jax.random.normal(k, shape, f32) * std
    x = jax.random.normal(ks[0], (BATCH, SEQ, D_MODEL), f32)
    positions = (jnp.arange(SEQ, dtype=jnp.int32)[None, :]
                 + jax.random.randint(ks[1], (BATCH, 1), 0, 4096, dtype=jnp.int32))
    col_scale = jnp.concatenate([jnp.ones((2 * D_ATTN,), f32), jnp.full((D_ATTN,), BETA, f32),
                                 jnp.ones((D_SSD + D_CONV,), f32), jnp.full((N_SSD_HEADS,), 0.1, f32)])
    w_in = nrm(ks[2], (DEPTH, D_MODEL, D_IN_PROJ), D_MODEL ** -0.5) * col_scale
    conv_w = nrm(ks[3], (DEPTH, CONV_WIDTH, D_CONV), CONV_WIDTH ** -0.5)
    conv_b = nrm(ks[4], (DEPTH, D_CONV), 0.01)
    dt0 = jnp.exp(jax.random.uniform(ks[5], (DEPTH, N_SSD_HEADS), f32, np.log(1e-3), np.log(1e-1)))
    dt_bias = dt0 + jnp.log(-jnp.expm1(-dt0))
    a_log = jnp.log(jax.random.uniform(ks[6], (DEPTH, N_SSD_HEADS), f32, 1.0, 16.0))
    d_skip = 1.0 + nrm(ks[7], (DEPTH, N_SSD_HEADS), 0.01)
    attn_norm_w = 1.0 + nrm(ks[8], (DEPTH, D_ATTN), 0.01)
    ssd_norm_w = 1.0 + nrm(ks[9], (DEPTH, D_SSD), 0.01)
    w_out = nrm(ks[10], (DEPTH, D_MIX, D_MODEL), BETA * D_MIX ** -0.5)

    def ffn(k0, k1, k2):
        return (nrm(k0, (DEPTH, D_MODEL, D_FF), D_MODEL ** -0.5),
                nrm(k1, (DEPTH, D_MODEL, D_FF), BETA * D_MODEL ** -0.5),
                nrm(k2, (DEPTH, D_FF, D_MODEL), BETA * D_FF ** -0.5))

    ffn1_gate, ffn1_up, ffn1_down = ffn(ks[11], ks[12], ks[13])
    ffn2_gate, ffn2_up, ffn2_down = ffn(ks[14], ks[15], ks[16])
    gain = lambda k: 1.0 + nrm(k, (DEPTH, D_MODEL), 0.01)
    bias = lambda k: nrm(k, (DEPTH, D_MODEL), 0.01)
    return {'x': x, 'positions': positions,
            'ln1_g': gain(ks[17]), 'ln1_b': bias(ks[18]),
            'ffn1_gate': ffn1_gate, 'ffn1_up': ffn1_up, 'ffn1_down': ffn1_down,
            'w_in': w_in, 'conv_w': conv_w, 'conv_b': conv_b, 'dt_bias': dt_bias, 'a_log': a_log,
            'd_skip': d_skip, 'attn_norm_w': attn_norm_w, 'ssd_norm_w': ssd_norm_w, 'w_out': w_out,
            'ln2_g': gain(ks[19]), 'ln2_b': bias(ks[20]),
            'ffn2_gate': ffn2_gate, 'ffn2_up': ffn2_up, 'ffn2_down': ffn2_down,
            'ln3_g': gain(ks[21]), 'ln3_b': bias(ks[22])}


def _fwd_reference(x, positions, ln1_g, ln1_b, ffn1_gate, ffn1_up, ffn1_down, w_in, conv_w, conv_b,
              dt_bias, a_log, d_skip, attn_norm_w, ssd_norm_w, w_out, ln2_g, ln2_b,
              ffn2_gate, ffn2_up, ffn2_down, ln3_g, ln3_b):
    cos, sin = rotary_tables(positions)
    h = x
    for l in range(DEPTH):
        h = layer_norm(ALPHA * h + 0.5 * swiglu(h, ffn1_gate[l], ffn1_up[l], ffn1_down[l]), ln1_g[l], ln1_b[l])
        mix = hybrid_mixer(h, cos, sin, w_in[l], conv_w[l], conv_b[l], dt_bias[l], a_log[l], d_skip[l],
                           attn_norm_w[l], ssd_norm_w[l], w_out[l])
        h = layer_norm(ALPHA * h + mix, ln2_g[l], ln2_b[l])
        h = layer_norm(ALPHA * h + 0.5 * swiglu(h, ffn2_gate[l], ffn2_up[l], ffn2_down[l]), ln3_g[l], ln3_b[l])
    return h


import jax as _jax
import jax.numpy as _jnp

TWIN_FORMAT = 'train_step'
FWD_PARAMS = ['x', 'positions', 'ln1_g', 'ln1_b', 'ffn1_gate', 'ffn1_up', 'ffn1_down', 'w_in', 'conv_w', 'conv_b', 'dt_bias', 'a_log', 'd_skip', 'attn_norm_w', 'ssd_norm_w', 'w_out', 'ln2_g', 'ln2_b', 'ffn2_gate', 'ffn2_up', 'ffn2_down', 'ln3_g', 'ln3_b']
TWIN_WEIGHTS = ['ln1_g', 'ln1_b', 'ffn1_gate', 'ffn1_up', 'ffn1_down', 'w_in', 'conv_w', 'conv_b', 'dt_bias', 'a_log', 'd_skip', 'attn_norm_w', 'ssd_norm_w', 'w_out', 'ln2_g', 'ln2_b', 'ffn2_gate', 'ffn2_up', 'ffn2_down', 'ln3_g', 'ln3_b']
TWIN_DIFF_INPUT = 'x'
TWIN_INPUTS = ['x', 'positions', 'ln1_g', 'ln1_b', 'ffn1_gate', 'ffn1_up', 'ffn1_down', 'w_in', 'conv_w', 'conv_b', 'dt_bias', 'a_log', 'd_skip', 'attn_norm_w', 'ssd_norm_w', 'w_out', 'ln2_g', 'ln2_b', 'ffn2_gate', 'ffn2_up', 'ffn2_down', 'ln3_g', 'ln3_b', 'loss_target', 'm_ln1_g', 'm_ln1_b', 'm_ffn1_gate', 'm_ffn1_up', 'm_ffn1_down', 'm_w_in', 'm_conv_w', 'm_conv_b', 'm_dt_bias', 'm_a_log', 'm_d_skip', 'm_attn_norm_w', 'm_ssd_norm_w', 'm_w_out', 'm_ln2_g', 'm_ln2_b', 'm_ffn2_gate', 'm_ffn2_up', 'm_ffn2_down', 'm_ln3_g', 'm_ln3_b', 'v_ln1_g', 'v_ln1_b', 'v_ffn1_gate', 'v_ffn1_up', 'v_ffn1_down', 'v_w_in', 'v_conv_w', 'v_conv_b', 'v_dt_bias', 'v_a_log', 'v_d_skip', 'v_attn_norm_w', 'v_ssd_norm_w', 'v_w_out', 'v_ln2_g', 'v_ln2_b', 'v_ffn2_gate', 'v_ffn2_up', 'v_ffn2_down', 'v_ln3_g', 'v_ln3_b']
TWIN_OUTPUTS = ['loss', 'grad_x', 'grad_ln1_g', 'grad_ln1_b', 'grad_ffn1_gate', 'grad_ffn1_up', 'grad_ffn1_down', 'grad_w_in', 'grad_conv_w', 'grad_conv_b', 'grad_dt_bias', 'grad_a_log', 'grad_d_skip', 'grad_attn_norm_w', 'grad_ssd_norm_w', 'grad_w_out', 'grad_ln2_g', 'grad_ln2_b', 'grad_ffn2_gate', 'grad_ffn2_up', 'grad_ffn2_down', 'grad_ln3_g', 'grad_ln3_b', 'delta_ln1_g', 'delta_ln1_b', 'delta_ffn1_gate', 'delta_ffn1_up', 'delta_ffn1_down', 'delta_w_in', 'delta_conv_w', 'delta_conv_b', 'delta_dt_bias', 'delta_a_log', 'delta_d_skip', 'delta_attn_norm_w', 'delta_ssd_norm_w', 'delta_w_out', 'delta_ln2_g', 'delta_ln2_b', 'delta_ffn2_gate', 'delta_ffn2_up', 'delta_ffn2_down', 'delta_ln3_g', 'delta_ln3_b', 'new_m_ln1_g', 'new_m_ln1_b', 'new_m_ffn1_gate', 'new_m_ffn1_up', 'new_m_ffn1_down', 'new_m_w_in', 'new_m_conv_w', 'new_m_conv_b', 'new_m_dt_bias', 'new_m_a_log', 'new_m_d_skip', 'new_m_attn_norm_w', 'new_m_ssd_norm_w', 'new_m_w_out', 'new_m_ln2_g', 'new_m_ln2_b', 'new_m_ffn2_gate', 'new_m_ffn2_up', 'new_m_ffn2_down', 'new_m_ln3_g', 'new_m_ln3_b', 'new_v_ln1_g', 'new_v_ln1_b', 'new_v_ffn1_gate', 'new_v_ffn1_up', 'new_v_ffn1_down', 'new_v_w_in', 'new_v_conv_w', 'new_v_conv_b', 'new_v_dt_bias', 'new_v_a_log', 'new_v_d_skip', 'new_v_attn_norm_w', 'new_v_ssd_norm_w', 'new_v_w_out', 'new_v_ln2_g', 'new_v_ln2_b', 'new_v_ffn2_gate', 'new_v_ffn2_up', 'new_v_ffn2_down', 'new_v_ln3_g', 'new_v_ln3_b']
TWIN_LEAF_KINDS = {'loss': 'loss', 'grad_x': 'grad_x', 'grad_ln1_g': 'grad_w', 'grad_ln1_b': 'grad_w', 'grad_ffn1_gate': 'grad_w', 'grad_ffn1_up': 'grad_w', 'grad_ffn1_down': 'grad_w', 'grad_w_in': 'grad_w', 'grad_conv_w': 'grad_w', 'grad_conv_b': 'grad_w', 'grad_dt_bias': 'grad_w', 'grad_a_log': 'grad_w', 'grad_d_skip': 'grad_w', 'grad_attn_norm_w': 'grad_w', 'grad_ssd_norm_w': 'grad_w', 'grad_w_out': 'grad_w', 'grad_ln2_g': 'grad_w', 'grad_ln2_b': 'grad_w', 'grad_ffn2_gate': 'grad_w', 'grad_ffn2_up': 'grad_w', 'grad_ffn2_down': 'grad_w', 'grad_ln3_g': 'grad_w', 'grad_ln3_b': 'grad_w', 'delta_ln1_g': 'delta_w', 'delta_ln1_b': 'delta_w', 'delta_ffn1_gate': 'delta_w', 'delta_ffn1_up': 'delta_w', 'delta_ffn1_down': 'delta_w', 'delta_w_in': 'delta_w', 'delta_conv_w': 'delta_w', 'delta_conv_b': 'delta_w', 'delta_dt_bias': 'delta_w', 'delta_a_log': 'delta_w', 'delta_d_skip': 'delta_w', 'delta_attn_norm_w': 'delta_w', 'delta_ssd_norm_w': 'delta_w', 'delta_w_out': 'delta_w', 'delta_ln2_g': 'delta_w', 'delta_ln2_b': 'delta_w', 'delta_ffn2_gate': 'delta_w', 'delta_ffn2_up': 'delta_w', 'delta_ffn2_down': 'delta_w', 'delta_ln3_g': 'delta_w', 'delta_ln3_b': 'delta_w', 'new_m_ln1_g': 'new_m', 'new_m_ln1_b': 'new_m', 'new_m_ffn1_gate': 'new_m', 'new_m_ffn1_up': 'new_m', 'new_m_ffn1_down': 'new_m', 'new_m_w_in': 'new_m', 'new_m_conv_w': 'new_m', 'new_m_conv_b': 'new_m', 'new_m_dt_bias': 'new_m', 'new_m_a_log': 'new_m', 'new_m_d_skip': 'new_m', 'new_m_attn_norm_w': 'new_m', 'new_m_ssd_norm_w': 'new_m', 'new_m_w_out': 'new_m', 'new_m_ln2_g': 'new_m', 'new_m_ln2_b': 'new_m', 'new_m_ffn2_gate': 'new_m', 'new_m_ffn2_up': 'new_m', 'new_m_ffn2_down': 'new_m', 'new_m_ln3_g': 'new_m', 'new_m_ln3_b': 'new_m', 'new_v_ln1_g': 'new_v', 'new_v_ln1_b': 'new_v', 'new_v_ffn1_gate': 'new_v', 'new_v_ffn1_up': 'new_v', 'new_v_ffn1_down': 'new_v', 'new_v_w_in': 'new_v', 'new_v_conv_w': 'new_v', 'new_v_conv_b': 'new_v', 'new_v_dt_bias': 'new_v', 'new_v_a_log': 'new_v', 'new_v_d_skip': 'new_v', 'new_v_attn_norm_w': 'new_v', 'new_v_ssd_norm_w': 'new_v', 'new_v_w_out': 'new_v', 'new_v_ln2_g': 'new_v', 'new_v_ln2_b': 'new_v', 'new_v_ffn2_gate': 'new_v', 'new_v_ffn2_up': 'new_v', 'new_v_ffn2_down': 'new_v', 'new_v_ln3_g': 'new_v', 'new_v_ln3_b': 'new_v'}


def _forward(args):
    return _fwd_reference(*[args[k] for k in FWD_PARAMS])


def _output_shape():
    out = _jax.eval_shape(lambda: _forward(_fwd_setup_inputs(0)))
    return out.shape, out.dtype

N_MICROBATCH = 1
ADAM_LR = 0.001
ADAM_B1 = 0.9
ADAM_B2 = 0.999
ADAM_EPS = 1e-08
ADAM_WD = 0.01
ADAM_STEP = 10
PER_EXAMPLE_BATCH_AXIS = {'x': 0, 'positions': 0, 'loss_target': 0}
SHARED_INPUTS = []
_WEIGHT_DTYPES = {'ln1_g': _jnp.float32, 'ln1_b': _jnp.float32, 'ffn1_gate': _jnp.float32, 'ffn1_up': _jnp.float32, 'ffn1_down': _jnp.float32, 'w_in': _jnp.float32, 'conv_w': _jnp.float32, 'conv_b': _jnp.float32, 'dt_bias': _jnp.float32, 'a_log': _jnp.float32, 'd_skip': _jnp.float32, 'attn_norm_w': _jnp.float32, 'ssd_norm_w': _jnp.float32, 'w_out': _jnp.float32, 'ln2_g': _jnp.float32, 'ln2_b': _jnp.float32, 'ffn2_gate': _jnp.float32, 'ffn2_up': _jnp.float32, 'ffn2_down': _jnp.float32, 'ln3_g': _jnp.float32, 'ln3_b': _jnp.float32}
MOMENT_SCALE = {'ln1_g': 5.419799e-01, 'ln1_b': 1.875846e+00, 'ffn1_gate': 1.593024e-02, 'ffn1_up': 2.595454e-02, 'ffn1_down': 4.274518e-02, 'w_in': 9.715231e-02, 'conv_w': 6.639054e-02, 'conv_b': 8.843053e-02, 'dt_bias': 2.548795e-01, 'a_log': 2.844928e-01, 'd_skip': 1.358744e+00, 'attn_norm_w': 9.457569e-02, 'ssd_norm_w': 9.467614e-02, 'w_out': 1.904289e-01, 'ln2_g': 7.402710e-01, 'ln2_b': 3.716784e-01, 'ffn2_gate': 1.379109e-02, 'ffn2_up': 2.241335e-02, 'ffn2_down': 3.722948e-02, 'ln3_g': 6.393834e+01, 'ln3_b': 2.966930e+00}


def _to_microbatches(a, axis):
    t = _jnp.moveaxis(a, axis, 0)
    t = t.reshape((N_MICROBATCH, t.shape[0] // N_MICROBATCH) + t.shape[1:])
    return _jnp.moveaxis(t, 1, axis + 1)


def setup_inputs(seed: int = 0) -> dict:
    inp = _fwd_setup_inputs(seed)
    key = _jax.random.fold_in(_jax.random.key(seed), 7919)
    shape, _ = _output_shape()
    out = dict(inp)
    out["loss_target"] = _jax.random.normal(_jax.random.fold_in(key, 0), shape, _jnp.float32)
    for i, name in enumerate(TWIN_WEIGHTS):
        w = inp[name].astype(_jnp.float32)
        if MOMENT_SCALE is None:
            s = _jnp.sqrt(_jnp.mean(_jnp.square(w)) + 1e-30)
        else:
            s = MOMENT_SCALE[name]
        km, kv = _jax.random.split(_jax.random.fold_in(key, i + 1))
        out[name] = w
        out["m_" + name] = s * _jax.random.normal(km, w.shape, _jnp.float32)
        out["v_" + name] = (s * s) * _jax.random.uniform(kv, w.shape, _jnp.float32, 0.5, 1.5)
    if N_MICROBATCH > 1:
        for name, axis in PER_EXAMPLE_BATCH_AXIS.items():
            out[name] = _to_microbatches(out[name], axis)
    return {'x': out['x'], 'positions': out['positions'], 'ln1_g': out['ln1_g'], 'ln1_b': out['ln1_b'], 'ffn1_gate': out['ffn1_gate'], 'ffn1_up': out['ffn1_up'], 'ffn1_down': out['ffn1_down'], 'w_in': out['w_in'], 'conv_w': out['conv_w'], 'conv_b': out['conv_b'], 'dt_bias': out['dt_bias'], 'a_log': out['a_log'], 'd_skip': out['d_skip'], 'attn_norm_w': out['attn_norm_w'], 'ssd_norm_w': out['ssd_norm_w'], 'w_out': out['w_out'], 'ln2_g': out['ln2_g'], 'ln2_b': out['ln2_b'], 'ffn2_gate': out['ffn2_gate'], 'ffn2_up': out['ffn2_up'], 'ffn2_down': out['ffn2_down'], 'ln3_g': out['ln3_g'], 'ln3_b': out['ln3_b'], 'loss_target': out['loss_target'], 'm_ln1_g': out['m_ln1_g'], 'm_ln1_b': out['m_ln1_b'], 'm_ffn1_gate': out['m_ffn1_gate'], 'm_ffn1_up': out['m_ffn1_up'], 'm_ffn1_down': out['m_ffn1_down'], 'm_w_in': out['m_w_in'], 'm_conv_w': out['m_conv_w'], 'm_conv_b': out['m_conv_b'], 'm_dt_bias': out['m_dt_bias'], 'm_a_log': out['m_a_log'], 'm_d_skip': out['m_d_skip'], 'm_attn_norm_w': out['m_attn_norm_w'], 'm_ssd_norm_w': out['m_ssd_norm_w'], 'm_w_out': out['m_w_out'], 'm_ln2_g': out['m_ln2_g'], 'm_ln2_b': out['m_ln2_b'], 'm_ffn2_gate': out['m_ffn2_gate'], 'm_ffn2_up': out['m_ffn2_up'], 'm_ffn2_down': out['m_ffn2_down'], 'm_ln3_g': out['m_ln3_g'], 'm_ln3_b': out['m_ln3_b'], 'v_ln1_g': out['v_ln1_g'], 'v_ln1_b': out['v_ln1_b'], 'v_ffn1_gate': out['v_ffn1_gate'], 'v_ffn1_up': out['v_ffn1_up'], 'v_ffn1_down': out['v_ffn1_down'], 'v_w_in': out['v_w_in'], 'v_conv_w': out['v_conv_w'], 'v_conv_b': out['v_conv_b'], 'v_dt_bias': out['v_dt_bias'], 'v_a_log': out['v_a_log'], 'v_d_skip': out['v_d_skip'], 'v_attn_norm_w': out['v_attn_norm_w'], 'v_ssd_norm_w': out['v_ssd_norm_w'], 'v_w_out': out['v_w_out'], 'v_ln2_g': out['v_ln2_g'], 'v_ln2_b': out['v_ln2_b'], 'v_ffn2_gate': out['v_ffn2_gate'], 'v_ffn2_up': out['v_ffn2_up'], 'v_ffn2_down': out['v_ffn2_down'], 'v_ln3_g': out['v_ln3_g'], 'v_ln3_b': out['v_ln3_b']}


def _loss(weights, diff, rest, loss_target):
    with _jax.named_scope("forward"):
        args = {**rest, TWIN_DIFF_INPUT: diff, **{k: w.astype(_WEIGHT_DTYPES[k]) for k, w in weights.items()}}
        y = _forward(args)
    with _jax.named_scope("loss_head"):
        err = _jnp.square(y.astype(_jnp.float32) - loss_target)
        return 0.5 * _jnp.sum(_jnp.mean(err, axis=-1)) if err.ndim else 0.5 * err


def _adamw(w, g, m, v):
    m = ADAM_B1 * m + (1.0 - ADAM_B1) * g
    v = ADAM_B2 * v + (1.0 - ADAM_B2) * _jnp.square(g)
    m_hat = m / (1.0 - ADAM_B1 ** ADAM_STEP)
    v_hat = v / (1.0 - ADAM_B2 ** ADAM_STEP)
    delta = -ADAM_LR * (m_hat / (_jnp.sqrt(v_hat) + ADAM_EPS) + ADAM_WD * w)
    return delta, m, v


def reference(x, positions, ln1_g, ln1_b, ffn1_gate, ffn1_up, ffn1_down, w_in, conv_w, conv_b, dt_bias, a_log, d_skip, attn_norm_w, ssd_norm_w, w_out, ln2_g, ln2_b, ffn2_gate, ffn2_up, ffn2_down, ln3_g, ln3_b, loss_target, m_ln1_g, m_ln1_b, m_ffn1_gate, m_ffn1_up, m_ffn1_down, m_w_in, m_conv_w, m_conv_b, m_dt_bias, m_a_log, m_d_skip, m_attn_norm_w, m_ssd_norm_w, m_w_out, m_ln2_g, m_ln2_b, m_ffn2_gate, m_ffn2_up, m_ffn2_down, m_ln3_g, m_ln3_b, v_ln1_g, v_ln1_b, v_ffn1_gate, v_ffn1_up, v_ffn1_down, v_w_in, v_conv_w, v_conv_b, v_dt_bias, v_a_log, v_d_skip, v_attn_norm_w, v_ssd_norm_w, v_w_out, v_ln2_g, v_ln2_b, v_ffn2_gate, v_ffn2_up, v_ffn2_down, v_ln3_g, v_ln3_b):
    given = dict(x=x, positions=positions, ln1_g=ln1_g, ln1_b=ln1_b, ffn1_gate=ffn1_gate, ffn1_up=ffn1_up, ffn1_down=ffn1_down, w_in=w_in, conv_w=conv_w, conv_b=conv_b, dt_bias=dt_bias, a_log=a_log, d_skip=d_skip, attn_norm_w=attn_norm_w, ssd_norm_w=ssd_norm_w, w_out=w_out, ln2_g=ln2_g, ln2_b=ln2_b, ffn2_gate=ffn2_gate, ffn2_up=ffn2_up, ffn2_down=ffn2_down, ln3_g=ln3_g, ln3_b=ln3_b, loss_target=loss_target, m_ln1_g=m_ln1_g, m_ln1_b=m_ln1_b, m_ffn1_gate=m_ffn1_gate, m_ffn1_up=m_ffn1_up, m_ffn1_down=m_ffn1_down, m_w_in=m_w_in, m_conv_w=m_conv_w, m_conv_b=m_conv_b, m_dt_bias=m_dt_bias, m_a_log=m_a_log, m_d_skip=m_d_skip, m_attn_norm_w=m_attn_norm_w, m_ssd_norm_w=m_ssd_norm_w, m_w_out=m_w_out, m_ln2_g=m_ln2_g, m_ln2_b=m_ln2_b, m_ffn2_gate=m_ffn2_gate, m_ffn2_up=m_ffn2_up, m_ffn2_down=m_ffn2_down, m_ln3_g=m_ln3_g, m_ln3_b=m_ln3_b, v_ln1_g=v_ln1_g, v_ln1_b=v_ln1_b, v_ffn1_gate=v_ffn1_gate, v_ffn1_up=v_ffn1_up, v_ffn1_down=v_ffn1_down, v_w_in=v_w_in, v_conv_w=v_conv_w, v_conv_b=v_conv_b, v_dt_bias=v_dt_bias, v_a_log=v_a_log, v_d_skip=v_d_skip, v_attn_norm_w=v_attn_norm_w, v_ssd_norm_w=v_ssd_norm_w, v_w_out=v_w_out, v_ln2_g=v_ln2_g, v_ln2_b=v_ln2_b, v_ffn2_gate=v_ffn2_gate, v_ffn2_up=v_ffn2_up, v_ffn2_down=v_ffn2_down, v_ln3_g=v_ln3_g, v_ln3_b=v_ln3_b)
    weights = {n: given[n] for n in TWIN_WEIGHTS}
    shared = {n: given[n] for n in SHARED_INPUTS}
    per_example = {n: given[n] for n in ['x', 'positions']}
    grad_fn = _jax.value_and_grad(_loss, argnums=(0, 1))

    def one_microbatch(ex, loss_target):
        ex = dict(ex)
        diff = ex.pop(TWIN_DIFF_INPUT)
        return grad_fn(weights, diff, {**shared, **ex}, loss_target)

    if N_MICROBATCH == 1:
        loss, (grad_w, grad_x) = one_microbatch(per_example, given["loss_target"])
    else:
        def body(carry, xs):
            loss_sum, grad_sum = carry
            l_k, (gw_k, gx_k) = one_microbatch(xs[0], xs[1])
            with _jax.named_scope("update"):
                return (loss_sum + l_k, _jax.tree.map(_jnp.add, grad_sum, gw_k)), gx_k

        init = (_jnp.zeros((), _jnp.float32), _jax.tree.map(_jnp.zeros_like, weights))
        (loss, grad_w), grad_x = _jax.lax.scan(body, init, (per_example, given["loss_target"]))
    with _jax.named_scope("update"):
        delta_w, new_m, new_v = {}, {}, {}
        for n in TWIN_WEIGHTS:
            delta_w[n], new_m[n], new_v[n] = _adamw(weights[n], grad_w[n], given["m_" + n], given["v_" + n])
    return (loss, grad_x, *[grad_w[n] for n in TWIN_WEIGHTS], *[delta_w[n] for n in TWIN_WEIGHTS],
            *[new_m[n] for n in TWIN_WEIGHTS], *[new_v[n] for n in TWIN_WEIGHTS])
```

```python
import functools

import numpy as np
import jax
import jax.numpy as jnp
from jax import lax
from jax.experimental import pallas as pl
from jax.experimental.pallas import tpu as pltpu

F32, BF16 = jnp.float32, jnp.bfloat16

D_MODEL = 1024
D_FF = 2816
N_HEADS = 12
HEAD_DIM = 64
D_ATTN = 768
D_SSD = 768
N_GROUPS = 4
HEADS_PER_GROUP = 3
D_STATE = 128
D_CONV = 1792
CONV_WIDTH = 4
ROPE_DIM = 16
ROPE_THETA = 500000.0
ALPHA = 2.0 ** 0.25
LN_EPS = 1e-5
RMS_EPS = 1e-6
ADAM_LR, ADAM_B1, ADAM_B2, ADAM_EPS, ADAM_WD, ADAM_STEP = 0.001, 0.9, 0.999, 1e-08, 0.01, 10

LANES = 128
GATE_UP_INTERLEAVE = 256
SEQ_BLOCK = 256
GROUP_LANES = 256
VMEM_LIMIT = 56 * 1024 * 1024
NEG = -1e30
MESH = pl.DeviceIdType.MESH
HIGHEST = lax.Precision.HIGHEST

_NT = (((1,), (1,)), ((), ()))
_TN = (((0,), (0,)), ((), ()))


def _params(*sem):
    return pltpu.CompilerParams(dimension_semantics=sem, vmem_limit_bytes=VMEM_LIMIT)


def _bf(v):
    return v.astype(BF16)


def _mm(name, pairs, *, scale=1.0, res=None, res_scale=1.0, out_dtype=F32, tm=512, tn=512):
    m, n = pairs[0][0].shape[0], pairs[0][1].shape[1]
    tm, tn = min(tm, m), min(tn, n)
    assert m % tm == 0 and n % tn == 0, (name, m, n, tm, tn)
    npair = len(pairs)

    def body(*refs):
        acc = None
        for a_ref, b_ref in zip(refs[:npair], refs[npair:2 * npair]):
            d = jnp.dot(_bf(a_ref[...]), b_ref[...], preferred_element_type=F32)
            acc = d if acc is None else acc + d
        if scale != 1.0:
            acc = acc * scale
        if res is not None:
            acc = acc + res_scale * refs[2 * npair][...]
        refs[-1][...] = acc.astype(out_dtype)

    in_specs = [pl.BlockSpec((tm, a.shape[1]), lambda i, j: (i, 0)) for a, _ in pairs]
    in_specs += [pl.BlockSpec((b.shape[0], tn), lambda i, j: (0, j)) for _, b in pairs]
    args = [a for a, _ in pairs] + [b for _, b in pairs]
    if res is not None:
        in_specs.append(pl.BlockSpec((tm, tn), lambda i, j: (i, j)))
        args.append(res)
    return pl.pallas_call(
        body, name=name, grid=(m // tm, n // tn), in_specs=in_specs,
        out_specs=pl.BlockSpec((tm, tn), lambda i, j: (i, j)),
        out_shape=jax.ShapeDtypeStruct((m, n), out_dtype),
        compiler_params=_params("parallel", "parallel"),
    )(*args)


def _mm_tn(name, x, dy, *, scale=1.0, tk=512, tn=512, tt=1024):
    t, k = x.shape
    n = dy.shape[1]
    tk, tn, tt = min(tk, k), min(tn, n), min(tt, t)
    assert k % tk == 0 and n % tn == 0 and t % tt == 0, (name, k, n, t)
    nt = t // tt

    def body(x_ref, dy_ref, o_ref):
        step = pl.program_id(2)
        d = lax.dot_general(_bf(x_ref[...]), _bf(dy_ref[...]), _TN, preferred_element_type=F32)

        @pl.when(step == 0)
        def _():
            o_ref[...] = d

        @pl.when(step > 0)
        def _():
            o_ref[...] += d

        if scale != 1.0:
            @pl.when(step == nt - 1)
            def _():
                o_ref[...] = o_ref[...] * scale

    return pl.pallas_call(
        body, name=name, grid=(k // tk, n // tn, nt),
        in_specs=[pl.BlockSpec((tt, tk), lambda i, j, s: (s, i)), pl.BlockSpec((tt, tn), lambda i, j, s: (s, j))],
        out_specs=pl.BlockSpec((tk, tn), lambda i, j, s: (i, j)),
        out_shape=jax.ShapeDtypeStruct((k, n), F32),
        compiler_params=_params("parallel", "parallel", "arbitrary"),
    )(x, dy)


def _mm_swiglu(name, x, wgu, *, tm=512):
    t, k = x.shape
    gi = GATE_UP_INTERLEAVE
    nj = wgu.shape[1] // (2 * gi)

    def body(x_ref, w_ref, au_ref, hm_ref):
        au = jnp.dot(_bf(x_ref[...]), w_ref[...], preferred_element_type=F32)
        a, u = au[:, :gi], au[:, gi:]
        au_ref[...] = _bf(au)
        hm_ref[...] = _bf(a * jax.nn.sigmoid(a) * u)

    return pl.pallas_call(
        body, name=name, grid=(t // tm, nj),
        in_specs=[pl.BlockSpec((tm, k), lambda i, j: (i, 0)), pl.BlockSpec((k, 2 * gi), lambda i, j: (0, j))],
        out_specs=[pl.BlockSpec((tm, 2 * gi), lambda i, j: (i, j)), pl.BlockSpec((tm, gi), lambda i, j: (i, j))],
        out_shape=[jax.ShapeDtypeStruct((t, 2 * gi * nj), BF16), jax.ShapeDtypeStruct((t, gi * nj), BF16)],
        compiler_params=_params("parallel", "parallel"),
    )(x, wgu)


def _mm_swiglu_bwd(name, dr, wdt, au, *, scale, tm=512):
    t, k = dr.shape
    gi = GATE_UP_INTERLEAVE
    nj = wdt.shape[1] // gi

    def body(dr_ref, w_ref, au_ref, o_ref):
        dhm = jnp.dot(_bf(dr_ref[...]), w_ref[...], preferred_element_type=F32) * scale
        au_v = au_ref[...].astype(F32)
        a, u = au_v[:, :gi], au_v[:, gi:]
        sig = jax.nn.sigmoid(a)
        da = dhm * u * (sig * (1.0 + a * (1.0 - sig)))
        du = dhm * (a * sig)
        o_ref[:, :gi] = _bf(da)
        o_ref[:, gi:] = _bf(du)

    return pl.pallas_call(
        body, name=name, grid=(t // tm, nj),
        in_specs=[pl.BlockSpec((tm, k), lambda i, j: (i, 0)), pl.BlockSpec((k, gi), lambda i, j: (0, j)),
                  pl.BlockSpec((tm, 2 * gi), lambda i, j: (i, j))],
        out_specs=pl.BlockSpec((tm, 2 * gi), lambda i, j: (i, j)),
        out_shape=jax.ShapeDtypeStruct((t, 2 * gi * nj), BF16),
        compiler_params=_params("parallel", "parallel"),
    )(dr, wdt, au)


def _layer_norm(r, g, b):
    mu = jnp.mean(r, axis=-1, keepdims=True)
    var = jnp.mean(jnp.square(r - mu), axis=-1, keepdims=True)
    return (r - mu) * lax.rsqrt(var + LN_EPS) * g + b


def _mm_res_ln(name, a, w, res, g, b, *, scale, tm=256):
    t, k = a.shape
    n = w.shape[1]

    def body(a_ref, w_ref, res_ref, g_ref, b_ref, y_ref, r_ref):
        r = ALPHA * res_ref[...] + scale * jnp.dot(_bf(a_ref[...]), w_ref[...], preferred_element_type=F32)
        r_ref[...] = r
        y_ref[...] = _layer_norm(r, g_ref[...], b_ref[...])

    row = lambda c: pl.BlockSpec((tm, c), lambda i: (i, 0))
    const = lambda shape: pl.BlockSpec(shape, lambda i: (0, 0))
    return pl.pallas_call(
        body, name=name, grid=(t // tm,),
        in_specs=[row(k), const((k, n)), row(n), const((1, n)), const((1, n))],
        out_specs=[row(n), row(n)],
        out_shape=[jax.ShapeDtypeStruct((t, n), F32), jax.ShapeDtypeStruct((t, n), F32)],
        compiler_params=_params("parallel"),
    )(a, w, res, g, b)


def _rowwise(name, fn, rows, consts, row_outs, acc_outs=(), tm=256):
    rows = [r if isinstance(r, tuple) else (r, r.shape[1]) for r in rows]
    t = rows[0][0].shape[0]
    tm = min(tm, t)
    assert t % tm == 0
    nr, nc, no, na = len(rows), len(consts), len(row_outs), len(acc_outs)

    def body(*refs):
        vals = [r[...] for r in refs[:nr + nc]]
        outs, accs = fn(*vals)
        for o_ref, o in zip(refs[nr + nc:nr + nc + no], outs):
            o_ref[...] = o.astype(o_ref.dtype)
        if na:
            step = pl.program_id(0)
            acc_refs = refs[nr + nc + no:]

            @pl.when(step == 0)
            def _():
                for a_ref, a in zip(acc_refs, accs):
                    a_ref[...] = a

            @pl.when(step > 0)
            def _():
                for a_ref, a in zip(acc_refs, accs):
                    a_ref[...] += a

    in_specs = [pl.BlockSpec((tm, w), lambda i: (i, 0)) for _, w in rows]
    in_specs += [pl.BlockSpec(c.shape, lambda i, nd=c.ndim: (0,) * nd) for c in consts]
    out_specs = [pl.BlockSpec((tm, c), lambda i: (i, 0)) for c, _ in row_outs]
    out_specs += [pl.BlockSpec(s, lambda i: (0, 0)) for s in acc_outs]
    out_shape = [jax.ShapeDtypeStruct((t, c), dt) for c, dt in row_outs]
    out_shape += [jax.ShapeDtypeStruct(s, F32) for s in acc_outs]
    res = pl.pallas_call(
        body, name=name, grid=(t // tm,), in_specs=in_specs, out_specs=out_specs, out_shape=out_shape,
        compiler_params=_params("arbitrary" if na else "parallel"),
    )(*[r for r, _ in rows], *consts)
    return res


def _ln_bwd(name, r, g, b, dy):
    def fn(r_v, dy_v, g_v, b_v):
        _, vjp = jax.vjp(_layer_norm, r_v, g_v, b_v)
        dr, dg, db = vjp(dy_v)
        return [dr], [dg, db]
    return _rowwise(name, fn, [r, dy], [g, b], [(r.shape[1], F32)], [(1, r.shape[1])] * 2)


def _ln_loss_bwd(name, r, g, b, target):
    def fn(r_v, t_v, g_v, b_v):
        def loss_fn(rr, gg, bb):
            err = jnp.square(_layer_norm(rr, gg, bb) - t_v)
            return 0.5 * jnp.sum(jnp.mean(err, axis=-1, keepdims=True), axis=0, keepdims=True)
        loss, vjp = jax.vjp(loss_fn, r_v, g_v, b_v)
        dr, dg, db = vjp(jnp.ones((1, 1), F32))
        return [dr], [dg, db, jnp.broadcast_to(loss, (1, LANES))]
    return _rowwise(name, fn, [r, target], [g, b], [(r.shape[1], F32)], [(1, r.shape[1])] * 2 + [(1, LANES)])


def _rope_tables(posf, invf, sgn):
    ang = posf * invf
    return jnp.cos(ang), jnp.sin(ang) * sgn


def _rope_apply(tv, cos, sin):
    lane = lax.broadcasted_iota(jnp.int32, cos.shape, 1)
    first = (lane % HEAD_DIM) < (ROPE_DIM // 2)
    outs = []
    for gidx in range(tv.shape[1] // LANES):
        tg = tv[:, LANES * gidx:LANES * (gidx + 1)]
        sw = jnp.where(first, pltpu.roll(tg, LANES - ROPE_DIM // 2, 1), pltpu.roll(tg, ROPE_DIM // 2, 1))
        outs.append(tg * cos + sw * sin)
    return jnp.concatenate(outs, axis=1)


def _rope_fwd(qk, posf, invf, sgn):
    def fn(qk_v, pos_v, invf_v, sgn_v):
        cos, sin = _rope_tables(pos_v, invf_v, sgn_v)
        q = _rope_apply(qk_v[:, :D_ATTN], cos, sin) * (HEAD_DIM ** -0.5)
        k = _rope_apply(qk_v[:, D_ATTN:], cos, sin)
        return [q, k, jnp.concatenate([cos, sin], axis=1)], []
    return _rowwise("rope_fwd", fn, [qk, posf], [invf, sgn], [(D_ATTN, BF16), (D_ATTN, BF16), (2 * LANES, F32)])


def _rope_bwd(dq, dk, cs):
    def fn(dq_v, dk_v, cs_v):
        cos, sin = cs_v[:, :LANES], -cs_v[:, LANES:]
        gq = _rope_apply(dq_v * (HEAD_DIM ** -0.5), cos, sin)
        gk = _rope_apply(dk_v, cos, sin)
        return [jnp.concatenate([gq, gk], axis=1)], []
    return _rowwise("rope_bwd", fn, [dq, dk, cs], [], [(2 * D_ATTN, BF16)])[0]


def _rms(v, w):
    return v * lax.rsqrt(jnp.mean(v * v, axis=-1, keepdims=True) + RMS_EPS) * w


def _ungroup(yg):
    w = HEADS_PER_GROUP * HEAD_DIM
    return jnp.concatenate([yg[:, GROUP_LANES * g:GROUP_LANES * g + w] for g in range(N_GROUPS)], axis=1)


def _group(xs):
    w = HEADS_PER_GROUP * HEAD_DIM
    parts = []
    for g in range(N_GROUPS):
        parts += [xs[:, w * g:w * (g + 1)], jnp.zeros((xs.shape[0], GROUP_LANES - w), xs.dtype)]
    return jnp.concatenate(parts, axis=1)


def _norms_fn(attn, yg, xs, z, w_attn, w_ssd, dskip):
    a_n = _rms(attn, w_attn)
    y = _ungroup(yg) + dskip * xs
    y_n = _rms(y * (z * jax.nn.sigmoid(z)), w_ssd)
    return jnp.concatenate([a_n, y_n], axis=1)


def _norms_fwd(attn, yg, xbc, z, w_attn, w_ssd, dskip):
    def fn(*v):
        return [_norms_fn(*v)], []
    return _rowwise("norms_fwd", fn, [attn, yg, (xbc, D_SSD), z], [w_attn, w_ssd, dskip], [(D_ATTN + D_SSD, BF16)])[0]


def _norms_bwd(attn, yg, xbc, z, w_attn, w_ssd, dskip, dcat):
    def fn(attn_v, yg_v, xs_v, z_v, dcat_v, wa_v, ws_v, dk_v):
        _, vjp = jax.vjp(_norms_fn, attn_v, yg_v, xs_v, z_v, wa_v, ws_v, dk_v)
        d_attn, d_yg, d_xs, d_z, d_wa, d_ws, d_dk = vjp(dcat_v)
        return [d_attn, d_yg, d_xs, d_z], [d_wa, d_ws, d_dk]
    return _rowwise("norms_bwd", fn, [attn, yg, (xbc, D_SSD), z, dcat], [w_attn, w_ssd, dskip],
                    [(D_ATTN, F32), (N_GROUPS * GROUP_LANES, F32), (D_SSD, F32), (D_SSD, BF16)], [(1, D_SSD)] * 3)


def _ssd_prep_fn(xs, dtp, dtb, alog, e_x, e_a):
    dt = jax.nn.softplus(dtp + dtb)
    a = -jnp.exp(alog)
    dtg = jnp.dot(dt, e_x, precision=HIGHEST, preferred_element_type=F32)
    xdtg = _group(xs) * dtg
    dag = jnp.dot(dt * a, e_a, precision=HIGHEST, preferred_element_type=F32)
    return xdtg, dag


def _ssd_prep_fwd(xbc, dtp, dtb, alog, e_x, e_a):
    def fn(xbc_v, dtp_v, dtb_v, alog_v, ex_v, ea_v):
        xdtg, dag = _ssd_prep_fn(xbc_v[:, :D_SSD], dtp_v, dtb_v, alog_v, ex_v, ea_v)
        return [xdtg, xbc_v[:, D_SSD:], dag], []
    return _rowwise("ssd_prep_fwd", fn, [xbc, dtp], [dtb, alog, e_x, e_a],
                    [(N_GROUPS * GROUP_LANES, BF16), (D_CONV - D_SSD, BF16), (N_GROUPS * LANES, F32)])


def _ssd_prep_bwd(xbc, dtp, dtb, alog, e_x, e_a, dxdtg, ddag, dxs_a, db, dc):
    def fn(xs_v, dtp_v, dxdtg_v, ddag_v, dxs_a_v, db_v, dc_v, dtb_v, alog_v, ex_v, ea_v):
        _, vjp = jax.vjp(lambda a, b, c, d: _ssd_prep_fn(a, b, c, d, ex_v, ea_v), xs_v, dtp_v, dtb_v, alog_v)
        dxs, ddtp, ddtb, dalog = vjp((dxdtg_v, ddag_v))
        return [jnp.concatenate([dxs + dxs_a_v, db_v, dc_v], axis=1), ddtp], [ddtb, dalog]
    return _rowwise("ssd_prep_bwd", fn, [(xbc, D_SSD), dtp, dxdtg, ddag, dxs_a, db, dc], [dtb, alog, e_x, e_a],
                    [(D_CONV, F32), (LANES, BF16)], [(1, LANES)] * 2)


def _shift_down(u, d):
    if d == 0:
        return u
    row = lax.broadcasted_iota(jnp.int32, u.shape, 0)
    return jnp.where(row >= d, pltpu.roll(u, d, 0), 0.0)


def _shift_up(u, d):
    if d == 0:
        return u
    s = u.shape[0]
    row = lax.broadcasted_iota(jnp.int32, u.shape, 0)
    return jnp.where(row < s - d, pltpu.roll(u, s - d, 0), 0.0)


def _conv_pre(u, w, b):
    acc = b
    for k in range(CONV_WIDTH):
        acc = acc + w[k:k + 1, :] * _shift_down(u, CONV_WIDTH - 1 - k)
    return acc


def _conv_fwd(u, w, b, *, tc=256):
    nb, s, c = u.shape

    def body(u_ref, w_ref, b_ref, o_ref):
        pre = _conv_pre(u_ref[0], w_ref[...], b_ref[...])
        o_ref[0] = pre * jax.nn.sigmoid(pre)

    return pl.pallas_call(
        body, name="conv_fwd", grid=(c // tc, nb),
        in_specs=[pl.BlockSpec((1, s, tc), lambda j, i: (i, 0, j)), pl.BlockSpec((CONV_WIDTH, tc), lambda j, i: (0, j)),
                  pl.BlockSpec((1, tc), lambda j, i: (0, j))],
        out_specs=pl.BlockSpec((1, s, tc), lambda j, i: (i, 0, j)),
        out_shape=jax.ShapeDtypeStruct((nb, s, c), F32),
        compiler_params=_params("parallel", "parallel"),
    )(u, w, b)


def _conv_bwd(u, w, b, dout, *, tc=256):
    nb, s, c = u.shape

    def body(u_ref, w_ref, b_ref, d_ref, du_ref, dw_ref, db_ref):
        uv, wv = u_ref[0], w_ref[...]
        pre = _conv_pre(uv, wv, b_ref[...])
        sig = jax.nn.sigmoid(pre)
        dpre = d_ref[0] * (sig * (1.0 + pre * (1.0 - sig)))
        du = jnp.zeros_like(uv)
        dws = []
        for k in range(CONV_WIDTH):
            du = du + wv[k:k + 1, :] * _shift_up(dpre, CONV_WIDTH - 1 - k)
            dws.append(jnp.sum(dpre * _shift_down(uv, CONV_WIDTH - 1 - k), axis=0, keepdims=True))
        du_ref[0] = _bf(du)
        dwv = jnp.concatenate(dws + [jnp.zeros((8 - CONV_WIDTH, tc), F32)], axis=0)
        dbv = jnp.sum(dpre, axis=0, keepdims=True)
        first = pl.program_id(1) == 0

        @pl.when(first)
        def _():
            dw_ref[...] = dwv
            db_ref[...] = dbv

        @pl.when(jnp.logical_not(first))
        def _():
            dw_ref[...] += dwv
            db_ref[...] += dbv

    blk = pl.BlockSpec((1, s, tc), lambda j, i: (i, 0, j))
    return pl.pallas_call(
        body, name="conv_bwd", grid=(c // tc, nb),
        in_specs=[blk, pl.BlockSpec((CONV_WIDTH, tc), lambda j, i: (0, j)), pl.BlockSpec((1, tc), lambda j, i: (0, j)), blk],
        out_specs=[blk, pl.BlockSpec((8, tc), lambda j, i: (0, j)), pl.BlockSpec((1, tc), lambda j, i: (0, j))],
        out_shape=[jax.ShapeDtypeStruct((nb, s, c), BF16), jax.ShapeDtypeStruct((8, c), F32), jax.ShapeDtypeStruct((1, c), F32)],
        compiler_params=_params("parallel", "arbitrary"),
    )(u, w, b, dout)


def _branch_count_table(seq):
    nb = seq // SEQ_BLOCK
    r = np.arange(SEQ_BLOCK)[None, :, None]
    c = np.arange(SEQ_BLOCK)[None, None, :]
    delta = np.arange(nb)[:, None, None] * SEQ_BLOCK + r - c
    cnt = np.zeros(delta.shape, np.float32)
    for window, dilation in ((128, 1), (512, 4), (2048, 16)):
        cnt += (delta >= 0) & (delta % dilation == 0) & (delta <= window)
    return jnp.asarray(cnt)


def _attn_fwd(q, k, v, ctab):
    nb_, s, _ = q.shape
    ab = SEQ_BLOCK
    nblk = s // ab

    def body(q_ref, k_ref, v_ref, c_ref, o_ref, lse_ref):
        i = pl.program_id(2)
        outs, lses = [], []
        for h in range(LANES // HEAD_DIM):
            hs = slice(HEAD_DIM * h, HEAD_DIM * (h + 1))
            qh = q_ref[0, :, hs]

            def step(j, carry, qh=qh, hs=hs):
                m, l, acc = carry
                ks = pl.ds(pl.multiple_of(j * ab, ab), ab)
                cm = c_ref[i - j]
                sc = lax.dot_general(qh, k_ref[0, ks, hs], _NT, preferred_element_type=F32)
                sc = jnp.where(cm > 0, sc, NEG)
                m_new = jnp.maximum(m, jnp.max(sc, axis=1, keepdims=True))
                p = jnp.exp(sc - m_new) * cm
                a = jnp.exp(m - m_new)
                l = a * l + jnp.sum(p, axis=1, keepdims=True)
                acc = a * acc + jnp.dot(_bf(p), v_ref[0, ks, hs], preferred_element_type=F32)
                return m_new, l, acc

            init = (jnp.full((ab, 1), NEG, F32), jnp.zeros((ab, 1), F32), jnp.zeros((ab, HEAD_DIM), F32))
            m, l, acc = lax.fori_loop(0, i + 1, step, init)
            outs.append(acc / l)
            lses.append(jnp.broadcast_to(m + jnp.log(l), (ab, HEAD_DIM)))
        o_ref[0] = jnp.concatenate(outs, axis=1)
        lse_ref[0] = jnp.concatenate(lses, axis=1)

    qblk = pl.BlockSpec((1, ab, LANES), lambda b, hp, i: (b, i, hp))
    full = pl.BlockSpec((1, s, LANES), lambda b, hp, i: (b, 0, hp))
    return pl.pallas_call(
        body, name="attn_fwd", grid=(nb_, D_ATTN // LANES, nblk),
        in_specs=[qblk, full, full, pl.BlockSpec((nblk, ab, ab), lambda b, hp, i: (0, 0, 0))],
        out_specs=[qblk, qblk],
        out_shape=[jax.ShapeDtypeStruct((nb_, s, D_ATTN), F32), jax.ShapeDtypeStruct((nb_, s, D_ATTN), F32)],
        compiler_params=_params("parallel", "parallel", "parallel"),
    )(q, k, v, ctab)


def _attn_bwd(q, k, v, o, do, lse, ctab):
    nb_, s, _ = q.shape
    ab = SEQ_BLOCK
    nblk = s // ab

    def body(q_ref, k_ref, v_ref, o_ref, do_ref, lse_ref, c_ref, dq_ref, dk_ref, dv_ref, dv_acc):
        dk_ref[...] = jnp.zeros_like(dk_ref)
        dv_acc[...] = jnp.zeros_like(dv_acc)
        for h in range(LANES // HEAD_DIM):
            hs = slice(HEAD_DIM * h, HEAD_DIM * (h + 1))

            def outer(i, carry, hs=hs, h=h):
                qs = pl.ds(pl.multiple_of(i * ab, ab), ab)
                qi = q_ref[0, qs, hs]
                doi = do_ref[0, qs, hs]
                di = jnp.sum(doi * o_ref[0, qs, hs], axis=1, keepdims=True)
                lse_i = lse_ref[0, qs, HEAD_DIM * h:HEAD_DIM * h + 1]
                doi16 = _bf(doi)

                def inner(j, dq_acc):
                    ks = pl.ds(pl.multiple_of(j * ab, ab), ab)
                    kj, vj = k_ref[0, ks, hs], v_ref[0, ks, hs]
                    cm = c_ref[i - j]
                    sc = lax.dot_general(qi, kj, _NT, preferred_element_type=F32)
                    p = jnp.exp(jnp.where(cm > 0, sc - lse_i, NEG)) * cm
                    dp = lax.dot_general(doi16, vj, _NT, preferred_element_type=F32)
                    ds16 = _bf(p * (dp - di))
                    dv_acc[ks, hs] += lax.dot_general(_bf(p), doi16, _TN, preferred_element_type=F32)
                    dk_ref[0, ks, hs] += lax.dot_general(ds16, qi, _TN, preferred_element_type=F32)
                    return dq_acc + jnp.dot(ds16, kj, preferred_element_type=F32)

                dq_ref[0, qs, hs] = lax.fori_loop(0, i + 1, inner, jnp.zeros((ab, HEAD_DIM), F32))
                return carry

            lax.fori_loop(0, nblk, outer, 0)
        dv_ref[0] = _bf(dv_acc[...])

    full = pl.BlockSpec((1, s, LANES), lambda b, hp: (b, 0, hp))
    return pl.pallas_call(
        body, name="attn_bwd", grid=(nb_, D_ATTN // LANES),
        in_specs=[full] * 6 + [pl.BlockSpec((nblk, ab, ab), lambda b, hp: (0, 0, 0))],
        out_specs=[full, full, full],
        out_shape=[jax.ShapeDtypeStruct((nb_, s, D_ATTN), F32), jax.ShapeDtypeStruct((nb_, s, D_ATTN), F32),
                   jax.ShapeDtypeStruct((nb_, s, D_ATTN), BF16)],
        scratch_shapes=[pltpu.VMEM((s, LANES), F32)],
        compiler_params=_params("parallel", "parallel"),
    )(q, k, v, o, do, lse, ctab)


def _cumsum_fwd(dag):
    nb_, s, c = dag.shape
    ab = SEQ_BLOCK

    def body(a_ref, o_ref, ot_ref):
        r = lax.broadcasted_iota(jnp.int32, (ab, ab), 0)
        cc = lax.broadcasted_iota(jnp.int32, (ab, ab), 1)
        tri = (r >= cc).astype(F32)
        carry = jnp.zeros((1, c), F32)
        for i in range(s // ab):
            loc = jnp.dot(tri, a_ref[0, ab * i:ab * (i + 1), :], precision=HIGHEST, preferred_element_type=F32) + carry
            o_ref[0, ab * i:ab * (i + 1), :] = loc
            ot_ref[0, :, ab * i:ab * (i + 1)] = loc.T
            carry = loc[ab - 1:ab, :]

    return pl.pallas_call(
        body, name="ssd_cumsum", grid=(nb_,),
        in_specs=[pl.BlockSpec((1, s, c), lambda b: (b, 0, 0))],
        out_specs=[pl.BlockSpec((1, s, c), lambda b: (b, 0, 0)), pl.BlockSpec((1, c, s), lambda b: (b, 0, 0))],
        out_shape=[jax.ShapeDtypeStruct((nb_, s, c), F32), jax.ShapeDtypeStruct((nb_, c, s), F32)],
        compiler_params=_params("parallel"),
    )(dag)


def _cumsum_bwd(dcol, drow):
    nb_, s, c = dcol.shape
    ab = SEQ_BLOCK

    def body(c_ref, r_ref, o_ref):
        r = lax.broadcasted_iota(jnp.int32, (ab, ab), 0)
        cc = lax.broadcasted_iota(jnp.int32, (ab, ab), 1)
        tri = (r <= cc).astype(F32)
        carry = jnp.zeros((1, c), F32)
        for i in reversed(range(s // ab)):
            rows = r_ref[0, :, ab * i:ab * (i + 1)].T
            parts = []
            for g in range(N_GROUPS):
                parts += [rows[:, 8 * g:8 * (g + 1)], jnp.zeros((ab, LANES - 8), F32)]
            blk = c_ref[0, ab * i:ab * (i + 1), :] + jnp.concatenate(parts, axis=1)
            loc = jnp.dot(tri, blk, precision=HIGHEST, preferred_element_type=F32) + carry
            o_ref[0, ab * i:ab * (i + 1), :] = loc
            carry = loc[0:1, :]

    return pl.pallas_call(
        body, name="ssd_cumsum_bwd", grid=(nb_,),
        in_specs=[pl.BlockSpec((1, s, c), lambda b: (b, 0, 0)), pl.BlockSpec((1, N_GROUPS * 8, s), lambda b: (b, 0, 0))],
        out_specs=pl.BlockSpec((1, s, c), lambda b: (b, 0, 0)),
        out_shape=jax.ShapeDtypeStruct((nb_, s, c), F32),
        compiler_params=_params("parallel"),
    )(dcol, drow)


def _causal_ok(i, j):
    ab = SEQ_BLOCK
    r = lax.broadcasted_iota(jnp.int32, (ab, ab), 0)
    c = lax.broadcasted_iota(jnp.int32, (ab, ab), 1)
    return (r + (i - j) * ab) >= c


def _ssd_fwd(xdtg, bc, acum, acum_t):
    nb_, s, _ = xdtg.shape
    ab = SEQ_BLOCK

    def body(x_ref, b_ref, c_ref, ac_ref, at_ref, y_ref):
        i = pl.program_id(2)
        ci = c_ref[0]
        acol = [ac_ref[0, :, j:j + 1] for j in range(HEADS_PER_GROUP)]

        def step(jb, accs):
            ks = pl.ds(pl.multiple_of(jb * ab, ab), ab)
            cb = lax.dot_general(ci, b_ref[0, ks, :], _NT, preferred_element_type=F32)
            ok = _causal_ok(i, jb)
            new = []
            for j in range(HEADS_PER_GROUP):
                decay = jnp.exp(jnp.where(ok, acol[j] - at_ref[0, j:j + 1, ks], NEG))
                g = _bf(cb * decay)
                new.append(accs[j] + jnp.dot(g, x_ref[0, ks, HEAD_DIM * j:HEAD_DIM * (j + 1)], preferred_element_type=F32))
            return tuple(new)

        accs = lax.fori_loop(0, i + 1, step, tuple(jnp.zeros((ab, HEAD_DIM), F32) for _ in range(HEADS_PER_GROUP)))
        y_ref[0] = jnp.concatenate(list(accs) + [jnp.zeros((ab, GROUP_LANES - HEADS_PER_GROUP * HEAD_DIM), F32)], axis=1)

    return pl.pallas_call(
        body, name="ssd_fwd", grid=(nb_, N_GROUPS, s // ab),
        in_specs=[pl.BlockSpec((1, s, GROUP_LANES), lambda b, g, i: (b, 0, g)),
                  pl.BlockSpec((1, s, D_STATE), lambda b, g, i: (b, 0, g)),
                  pl.BlockSpec((1, ab, D_STATE), lambda b, g, i: (b, i, N_GROUPS + g)),
                  pl.BlockSpec((1, ab, LANES), lambda b, g, i: (b, i, g)),
                  pl.BlockSpec((1, 8, s), lambda b, g, i: (b, (LANES // 8) * g, 0))],
        out_specs=pl.BlockSpec((1, ab, GROUP_LANES), lambda b, g, i: (b, i, g)),
        out_shape=jax.ShapeDtypeStruct((nb_, s, N_GROUPS * GROUP_LANES), F32),
        compiler_params=_params("parallel", "parallel", "parallel"),
    )(xdtg, bc, bc, acum, acum_t)


def _ssd_bwd(xdtg, bc, acum, acum_t, dyg):
    nb_, s, _ = xdtg.shape
    ab = SEQ_BLOCK
    nblk = s // ab
    hpg = HEADS_PER_GROUP

    def body(x_ref, b_ref, c_ref, ac_ref, at_ref, dy_ref, dx_ref, db_ref, dc_ref, dac_ref, dar_ref):
        dx_ref[...] = jnp.zeros_like(dx_ref)
        db_ref[...] = jnp.zeros_like(db_ref)
        dac_ref[...] = jnp.zeros_like(dac_ref)
        dar_ref[...] = jnp.zeros_like(dar_ref)

        def outer(i, carry):
            qs = pl.ds(pl.multiple_of(i * ab, ab), ab)
            ci = c_ref[0, qs, :]
            dyi = [_bf(dy_ref[0, qs, HEAD_DIM * j:HEAD_DIM * (j + 1)]) for j in range(hpg)]
            acol = [ac_ref[0, qs, j:j + 1] for j in range(hpg)]

            def inner(jb, st):
                dc_acc, rs = st[0], list(st[1:])
                ks = pl.ds(pl.multiple_of(jb * ab, ab), ab)
                bj = b_ref[0, ks, :]
                cb = lax.dot_general(ci, bj, _NT, preferred_element_type=F32)
                ok = _causal_ok(i, jb)
                dcb = jnp.zeros((ab, ab), F32)
                for j in range(hpg):
                    hs = slice(HEAD_DIM * j, HEAD_DIM * (j + 1))
                    decay = jnp.exp(jnp.where(ok, acol[j] - at_ref[0, j:j + 1, ks], NEG))
                    g = cb * decay
                    dg = lax.dot_general(dyi[j], x_ref[0, ks, hs], _NT, preferred_element_type=F32)
                    dx_ref[0, ks, hs] += lax.dot_general(_bf(g), dyi[j], _TN, preferred_element_type=F32)
                    dcb = dcb + dg * decay
                    mm = dg * g
                    rs[j] = rs[j] + jnp.sum(mm, axis=1, keepdims=True)
                    dar_ref[0, j:j + 1, ks] -= jnp.sum(mm, axis=0, keepdims=True)
                dcb16 = _bf(dcb)
                db_ref[0, ks, :] += lax.dot_general(dcb16, ci, _TN, preferred_element_type=F32)
                return (dc_acc + jnp.dot(dcb16, bj, preferred_element_type=F32), *rs)

            init = (jnp.zeros((ab, D_STATE), F32),) + tuple(jnp.zeros((ab, 1), F32) for _ in range(hpg))
            st = lax.fori_loop(0, i + 1, inner, init)
            dc_ref[0, qs, :] = st[0]
            for j in range(hpg):
                dac_ref[0, qs, j:j + 1] = st[1 + j]
            return carry

        lax.fori_loop(0, nblk, outer, 0)

    xblk = pl.BlockSpec((1, s, GROUP_LANES), lambda b, g: (b, 0, g))
    sblk = pl.BlockSpec((1, s, D_STATE), lambda b, g: (b, 0, g))
    tblk = pl.BlockSpec((1, 8, s), lambda b, g: (b, (LANES // 8) * g, 0))
    return pl.pallas_call(
        body, name="ssd_bwd", grid=(nb_, N_GROUPS),
        in_specs=[xblk, sblk, pl.BlockSpec((1, s, D_STATE), lambda b, g: (b, 0, N_GROUPS + g)), sblk, tblk, xblk],
        out_specs=[xblk, sblk, sblk, sblk, pl.BlockSpec((1, 8, s), lambda b, g: (b, g, 0))],
        out_shape=[jax.ShapeDtypeStruct((nb_, s, N_GROUPS * GROUP_LANES), F32),
                   jax.ShapeDtypeStruct((nb_, s, N_GROUPS * D_STATE), F32),
                   jax.ShapeDtypeStruct((nb_, s, N_GROUPS * D_STATE), F32),
                   jax.ShapeDtypeStruct((nb_, s, N_GROUPS * LANES), F32),
                   jax.ShapeDtypeStruct((nb_, N_GROUPS * 8, s), F32)],
        compiler_params=_params("parallel", "parallel"),
    )(xdtg, bc, bc, acum, acum_t, dyg)


def _interleave(wg, wu):
    k, f = wg.shape
    gi = GATE_UP_INTERLEAVE
    return jnp.stack([wg.reshape(k, f // gi, gi), wu.reshape(k, f // gi, gi)], axis=2).reshape(k, 2 * f)


def _deinterleave(wgu):
    k = wgu.shape[0]
    gi = GATE_UP_INTERLEAVE
    w = wgu.reshape(k, -1, 2, gi)
    return w[:, :, 0, :].reshape(k, -1), w[:, :, 1, :].reshape(k, -1)


def _head_expanders():
    e_x = np.zeros((LANES, N_GROUPS * GROUP_LANES), np.float32)
    e_a = np.zeros((LANES, N_GROUPS * LANES), np.float32)
    for h in range(N_HEADS):
        g, j = divmod(h, HEADS_PER_GROUP)
        e_x[h, GROUP_LANES * g + HEAD_DIM * j:GROUP_LANES * g + HEAD_DIM * (j + 1)] = 1.0
        e_a[h, LANES * g + j] = 1.0
    return jnp.asarray(e_x), jnp.asarray(e_a)


def _pad_lanes(v, n=LANES):
    return jnp.pad(v, ((0, 0), (0, n - v.shape[1])))


def _local_step(x, positions, target, w):
    nb, s, d = x.shape
    t = nb * s
    x2 = x.reshape(t, d)
    tgt2 = target.reshape(t, d)

    wgu1, wgu2 = _interleave(w["ffn1_gate"], w["ffn1_up"]), _interleave(w["ffn2_gate"], w["ffn2_up"])
    w_in = w["w_in"]
    wqk, wv, wz = w_in[:, :2 * D_ATTN], w_in[:, 2 * D_ATTN:3 * D_ATTN], w_in[:, 3 * D_ATTN:3 * D_ATTN + D_SSD]
    wxbc = w_in[:, 3 * D_ATTN + D_SSD:3 * D_ATTN + D_SSD + D_CONV]
    wdt = _pad_lanes(w_in[:, 3 * D_ATTN + D_SSD + D_CONV:])

    inv_freq = ROPE_THETA ** (-jnp.arange(0, ROPE_DIM, 2, dtype=F32) / ROPE_DIM)
    half = ROPE_DIM // 2
    head_invf = jnp.concatenate([inv_freq, inv_freq, jnp.zeros((HEAD_DIM - ROPE_DIM,), F32)])
    head_sgn = jnp.concatenate([-jnp.ones((half,), F32), jnp.ones((half,), F32), jnp.zeros((HEAD_DIM - ROPE_DIM,), F32)])
    invf = jnp.tile(head_invf, LANES // HEAD_DIM)[None, :]
    sgn = jnp.tile(head_sgn, LANES // HEAD_DIM)[None, :]
    posf = positions.astype(F32).reshape(t, 1)
    ctab = _branch_count_table(s)
    e_x, e_a = _head_expanders()
    dtb, alog = _pad_lanes(w["dt_bias"]), _pad_lanes(w["a_log"])
    dskip = jnp.repeat(w["d_skip"], HEAD_DIM, axis=1)

    au1, hm1 = _mm_swiglu("ffn1_up", x2, wgu1)
    h1, r1 = _mm_res_ln("ffn1_down_ln1", hm1, w["ffn1_down"], x2, w["ln1_g"], w["ln1_b"], scale=0.5)

    qk = _mm("proj_qk", [(h1, wqk)], tn=768)
    v16 = _mm("proj_v", [(h1, wv)], tn=768, out_dtype=BF16)
    z = _mm("proj_z", [(h1, wz)], tn=768)
    xbc_pre = _mm("proj_xbc", [(h1, wxbc)], tn=896)
    dtp = _mm("proj_dt", [(h1, wdt)], tn=LANES)

    q16, k16, cs = _rope_fwd(qk, posf, invf, sgn)
    to3 = lambda a: a.reshape(nb, s, a.shape[-1])
    attn_o, lse = _attn_fwd(to3(q16), to3(k16), to3(v16), ctab)

    xbc = _conv_fwd(to3(xbc_pre), w["conv_w"], w["conv_b"]).reshape(t, D_CONV)
    xdtg, bc16, dag = _ssd_prep_fwd(xbc, dtp, dtb, alog, e_x, e_a)
    acum, acum_t = _cumsum_fwd(to3(dag))
    yg = _ssd_fwd(to3(xdtg), to3(bc16), acum, acum_t)

    cat = _norms_fwd(attn_o.reshape(t, D_ATTN), yg.reshape(t, -1), xbc, z, w["attn_norm_w"], w["ssd_norm_w"], dskip)
    h2, r2 = _mm_res_ln("w_out_ln2", cat, w["w_out"], h1, w["ln2_g"], w["ln2_b"], scale=1.0)

    au2, hm2 = _mm_swiglu("ffn2_up", h2, wgu2)
    _, r3 = _mm_res_ln("ffn2_down_ln3", hm2, w["ffn2_down"], h2, w["ln3_g"], w["ln3_b"], scale=0.5)

    g = {}
    dr3, g["ln3_g"], g["ln3_b"], loss = _ln_loss_bwd("loss_ln3_bwd", r3, w["ln3_g"], w["ln3_b"], tgt2)

    dau2 = _mm_swiglu_bwd("ffn2_act_bwd", dr3, w["ffn2_down"].T, au2, scale=0.5)
    g["ffn2_down"] = _mm_tn("ffn2_down_dw", hm2, dr3, scale=0.5, tk=256, tn=1024)
    g["ffn2_gate"], g["ffn2_up"] = _deinterleave(_mm_tn("ffn2_up_dw", h2, dau2, tk=1024, tn=512))
    dh2 = _mm("ffn2_dx", [(dau2, wgu2.T)], res=dr3, res_scale=ALPHA)

    dr2, g["ln2_g"], g["ln2_b"] = _ln_bwd("ln2_bwd", r2, w["ln2_g"], w["ln2_b"], dh2)
    dcat = _mm("w_out_dx", [(dr2, w["w_out"].T)], tn=768)
    g["w_out"] = _mm_tn("w_out_dw", cat, dr2, tk=768, tn=1024)

    d_attn, dyg, dxs_a, dz16, g["attn_norm_w"], g["ssd_norm_w"], ddskip = _norms_bwd(
        attn_o.reshape(t, D_ATTN), yg.reshape(t, -1), xbc, z, w["attn_norm_w"], w["ssd_norm_w"], dskip, dcat)
    g["d_skip"] = ddskip.reshape(N_HEADS, HEAD_DIM).sum(axis=1)[None, :]

    dq, dk, dv16 = _attn_bwd(to3(q16), to3(k16), to3(v16), attn_o, to3(d_attn), lse, ctab)
    dqk16 = _rope_bwd(dq.reshape(t, D_ATTN), dk.reshape(t, D_ATTN), cs)

    dxdtg, dbm, dcm, dacol, darow = _ssd_bwd(to3(xdtg), to3(bc16), acum, acum_t, to3(dyg))
    ddag = _cumsum_bwd(dacol, darow)
    dxbc, ddtp16, ddtb, dalog = _ssd_prep_bwd(xbc, dtp, dtb, alog, e_x, e_a, dxdtg.reshape(t, -1), ddag.reshape(t, -1),
                                               dxs_a, dbm.reshape(t, -1), dcm.reshape(t, -1))
    g["dt_bias"], g["a_log"] = ddtb[:, :N_HEADS], dalog[:, :N_HEADS]
    dxbc_pre16, dconv_w, g["conv_b"] = _conv_bwd(to3(xbc_pre), w["conv_w"], w["conv_b"], to3(dxbc))
    g["conv_w"] = dconv_w[:CONV_WIDTH]
    dxbc_pre16 = dxbc_pre16.reshape(t, D_CONV)
    dv16 = dv16.reshape(t, D_ATTN)

    dh1 = _mm("w_in_dx", [(dqk16, wqk.T), (dv16, wv.T), (dz16, wz.T), (dxbc_pre16, wxbc.T), (ddtp16, wdt.T)],
              res=dr2, res_scale=ALPHA)
    g["w_in"] = jnp.concatenate([
        _mm_tn("w_in_dw_qk", h1, dqk16, tk=1024, tn=512),
        _mm_tn("w_in_dw_v", h1, dv16, tk=1024, tn=768),
        _mm_tn("w_in_dw_z", h1, dz16, tk=1024, tn=768),
        _mm_tn("w_in_dw_xbc", h1, dxbc_pre16, tk=1024, tn=896),
        _mm_tn("w_in_dw_dt", h1, ddtp16, tk=1024, tn=LANES)[:, :N_HEADS],
    ], axis=1)

    dr1, g["ln1_g"], g["ln1_b"] = _ln_bwd("ln1_bwd", r1, w["ln1_g"], w["ln1_b"], dh1)
    dau1 = _mm_swiglu_bwd("ffn1_act_bwd", dr1, w["ffn1_down"].T, au1, scale=0.5)
    g["ffn1_down"] = _mm_tn("ffn1_down_dw", hm1, dr1, scale=0.5, tk=256, tn=1024)
    g["ffn1_gate"], g["ffn1_up"] = _deinterleave(_mm_tn("ffn1_up_dw", x2, dau1, tk=1024, tn=512))
    dx = _mm("ffn1_dx", [(dau1, wgu1.T)], res=dr1, res_scale=ALPHA)
    return loss, dx.reshape(nb, s, d), g


_HBM = pl.BlockSpec(memory_space=pltpu.HBM)
N_CHIPS = 4
N_DEVICES = 8


def _place():
    return lax.axis_index("x"), lax.axis_index("y"), lax.axis_index("c")


def _other_chips(x, y):
    return [(1 - x, y), (x, 1 - y), (1 - x, 1 - y)]


def _gather_shards(shards):
    n = len(shards)

    def body(*refs):
        ins, outs = refs[:n], refs[n:2 * n]
        send_sems, recv_sems, loc_sems = refs[2 * n:]
        x, y, c = _place()
        me = 2 * x + y
        peers = _other_chips(x, y)
        locs, sends = [], []
        for t in range(n):
            loc = pltpu.make_async_copy(ins[t], outs[t].at[me], loc_sems.at[t])
            loc.start()
            locs.append(loc)
            for p, (px, py) in enumerate(peers):
                cp = pltpu.make_async_remote_copy(ins[t], outs[t].at[me], send_sems.at[t, p], recv_sems.at[t, p],
                                                  device_id=(px, py, c), device_id_type=MESH)
                cp.start()
                sends.append(cp)
        for t in range(n):
            for p, (px, py) in enumerate(peers):
                pltpu.make_async_remote_copy(ins[t], outs[t].at[2 * px + py], send_sems.at[t, p], recv_sems.at[t, p],
                                             device_id=(px, py, c), device_id_type=MESH).wait_recv()
        for cp in sends:
            cp.wait_send()
        for loc in locs:
            loc.wait()

    return pl.pallas_call(
        body, name="gather_weights",
        in_specs=[_HBM] * n, out_specs=[_HBM] * n,
        out_shape=[jax.ShapeDtypeStruct((N_CHIPS,) + a.shape, a.dtype) for a in shards],
        scratch_shapes=[pltpu.SemaphoreType.DMA((n, N_CHIPS - 1)), pltpu.SemaphoreType.DMA((n, N_CHIPS - 1)),
                        pltpu.SemaphoreType.DMA((n,))],
    )(*shards)


def _exchange_partials(stacks):
    n = len(stacks)

    def body(*refs):
        ins, outs = refs[:n], refs[n:2 * n]
        send_sems, recv_sems, loc_sems = refs[2 * n:]
        x, y, c = _place()
        me = 2 * x + y
        peers = _other_chips(x, y)
        locs, sends = [], []
        for t in range(n):
            loc = pltpu.make_async_copy(ins[t].at[me], outs[t].at[me], loc_sems.at[t])
            loc.start()
            locs.append(loc)
            for p, (px, py) in enumerate(peers):
                cp = pltpu.make_async_remote_copy(ins[t].at[2 * px + py], outs[t].at[me], send_sems.at[t, p],
                                                  recv_sems.at[t, p], device_id=(px, py, c), device_id_type=MESH)
                cp.start()
                sends.append(cp)
        for t in range(n):
            for p, (px, py) in enumerate(peers):
                pltpu.make_async_remote_copy(ins[t].at[me], outs[t].at[2 * px + py], send_sems.at[t, p],
                                             recv_sems.at[t, p], device_id=(px, py, c), device_id_type=MESH).wait_recv()
        for cp in sends:
            cp.wait_send()
        for loc in locs:
            loc.wait()

    return pl.pallas_call(
        body, name="exchange_partials",
        in_specs=[_HBM] * n, out_specs=[_HBM] * n,
        out_shape=[jax.ShapeDtypeStruct(a.shape, a.dtype) for a in stacks],
        scratch_shapes=[pltpu.SemaphoreType.DMA((n, N_CHIPS - 1)), pltpu.SemaphoreType.DMA((n, N_CHIPS - 1)),
                        pltpu.SemaphoreType.DMA((n,))],
    )(*stacks)


def _sibling_swap(arrs):
    n = len(arrs)

    def body(*refs):
        ins, outs = refs[:n], refs[n:2 * n]
        send_sems, recv_sems = refs[2 * n:]
        x, y, c = _place()
        cps = []
        for t in range(n):
            cp = pltpu.make_async_remote_copy(ins[t], outs[t], send_sems.at[t], recv_sems.at[t],
                                              device_id=(x, y, 1 - c), device_id_type=MESH)
            cp.start()
            cps.append(cp)
        for cp in cps:
            cp.wait()

    return pl.pallas_call(
        body, name="sibling_swap",
        in_specs=[_HBM] * n, out_specs=[_HBM] * n,
        out_shape=[jax.ShapeDtypeStruct(a.shape, a.dtype) for a in arrs],
        scratch_shapes=[pltpu.SemaphoreType.DMA((n,)), pltpu.SemaphoreType.DMA((n,))],
    )(*arrs)


def _small_allreduce(v):
    r = v.shape[0]

    def body(v_ref, tot_ref, slots, send_sems, recv_sems):
        x, y, c = _place()
        me = 4 * x + 2 * y + c
        slots[me] = v_ref[...]
        cps, peers = [], []
        for k in range(1, N_DEVICES):
            px = 1 - x if (k >> 2) & 1 else x
            py = 1 - y if (k >> 1) & 1 else y
            pc = 1 - c if k & 1 else c
            cp = pltpu.make_async_remote_copy(v_ref, slots.at[me], send_sems.at[k - 1], recv_sems.at[k - 1],
                                              device_id=(px, py, pc), device_id_type=MESH)
            cp.start()
            cps.append(cp)
            peers.append((px, py, pc))
        for k, (px, py, pc) in enumerate(peers):
            pltpu.make_async_remote_copy(v_ref, slots.at[4 * px + 2 * py + pc], send_sems.at[k], recv_sems.at[k],
                                         device_id=(px, py, pc), device_id_type=MESH).wait_recv()
        for cp in cps:
            cp.wait_send()
        acc = slots[0]
        for s in range(1, N_DEVICES):
            acc = acc + slots[s]
        tot_ref[...] = acc

    return pl.pallas_call(
        body, name="small_allreduce",
        in_specs=[pl.BlockSpec(memory_space=pltpu.VMEM)], out_specs=pl.BlockSpec(memory_space=pltpu.VMEM),
        out_shape=jax.ShapeDtypeStruct((r, LANES), F32),
        scratch_shapes=[pltpu.VMEM((N_DEVICES, r, LANES), F32), pltpu.SemaphoreType.DMA((N_DEVICES - 1,)),
                        pltpu.SemaphoreType.DMA((N_DEVICES - 1,))],
    )(v)


def _elementwise(name, fn, ins, out_dtypes):
    r, c = ins[0].shape[-2:]
    tr = next((cand for cand in (256, 176, 128, 64, 32, 16) if r % cand == 0), r)
    nin = len(ins)

    def body(*refs):
        outs = fn(*[ref[...] for ref in refs[:nin]])
        for o_ref, o in zip(refs[nin:], outs):
            o_ref[...] = o.astype(o_ref.dtype)

    in_specs = [pl.BlockSpec((tr, c), lambda i: (i, 0)) if a.ndim == 2 else pl.BlockSpec((a.shape[0], tr, c), lambda i: (0, i, 0))
                for a in ins]
    return pl.pallas_call(
        body, name=name, grid=(r // tr,), in_specs=in_specs,
        out_specs=[pl.BlockSpec((tr, c), lambda i: (i, 0)) for _ in out_dtypes],
        out_shape=[jax.ShapeDtypeStruct((r, c), dt) for dt in out_dtypes],
        compiler_params=_params("parallel"),
    )(*ins)


def _sum_slots(name, stack):
    def fn(v):
        acc = v[0].astype(F32)
        for s in range(1, v.shape[0]):
            acc = acc + v[s].astype(F32)
        return [acc]
    return _elementwise(name, fn, [stack], [F32])[0]


def _adamw(name, grads, w, m, v):
    ng = len(grads)

    def fn(*vals):
        g = vals[0] if ng == 1 else vals[0] + vals[1]
        w_v, m_v, v_v = vals[ng:]
        m2 = ADAM_B1 * m_v + (1.0 - ADAM_B1) * g
        v2 = ADAM_B2 * v_v + (1.0 - ADAM_B2) * jnp.square(g)
        m_hat = m2 / (1.0 - ADAM_B1 ** ADAM_STEP)
        v_hat = v2 / (1.0 - ADAM_B2 ** ADAM_STEP)
        delta = -ADAM_LR * (m_hat / (jnp.sqrt(v_hat) + ADAM_EPS) + ADAM_WD * w_v)
        return [g, delta, m2, v2]

    return _elementwise(name, fn, list(grads) + [w, m, v], [F32] * 4)


_MATRICES = (("ffn1_gate", 1), ("ffn1_up", 1), ("ffn1_down", 0), ("w_in", 1), ("w_out", 0),
             ("ffn2_gate", 1), ("ffn2_up", 1), ("ffn2_down", 0))
_VECTORS = ("ln1_g", "ln1_b", "conv_b", "dt_bias", "a_log", "d_skip", "attn_norm_w", "ssd_norm_w",
            "ln2_g", "ln2_b", "ln3_g", "ln3_b")
_WEIGHT_ORDER = ("ln1_g", "ln1_b", "ffn1_gate", "ffn1_up", "ffn1_down", "w_in", "conv_w", "conv_b", "dt_bias", "a_log",
                 "d_skip", "attn_norm_w", "ssd_norm_w", "w_out", "ln2_g", "ln2_b", "ffn2_gate", "ffn2_up", "ffn2_down",
                 "ln3_g", "ln3_b")


def _pack_rows(vectors):
    parts = []
    for vec in vectors:
        flat = vec.reshape(-1)
        parts.append(jnp.pad(flat, (0, (-flat.shape[0]) % LANES)))
    flat = jnp.concatenate(parts)
    flat = jnp.pad(flat, (0, (-flat.shape[0]) % (8 * LANES)))
    return flat.reshape(-1, LANES)


def _unpack_rows(packed, shapes):
    flat = packed.reshape(-1)
    out, off = [], 0
    for shape in shapes:
        size = int(np.prod(shape))
        out.append(flat[off:off + size].reshape(shape))
        off += size + (-size) % LANES
    return out


def _assemble(stack, axis):
    if axis == 0:
        return stack.reshape(-1, stack.shape[2])
    return jnp.concatenate([stack[s] for s in range(N_CHIPS)], axis=1)


def _split(full, axis):
    if axis == 0:
        return full.reshape(N_CHIPS, -1, full.shape[1])
    cols = full.shape[1] // N_CHIPS
    return jnp.stack([full[:, cols * s:cols * (s + 1)] for s in range(N_CHIPS)])


def kernel(x, positions, ln1_g, ln1_b, ffn1_gate, ffn1_up, ffn1_down, w_in, conv_w, conv_b, dt_bias, a_log, d_skip, attn_norm_w, ssd_norm_w, w_out, ln2_g, ln2_b, ffn2_gate, ffn2_up, ffn2_down, ln3_g, ln3_b, loss_target, m_ln1_g, m_ln1_b, m_ffn1_gate, m_ffn1_up, m_ffn1_down, m_w_in, m_conv_w, m_conv_b, m_dt_bias, m_a_log, m_d_skip, m_attn_norm_w, m_ssd_norm_w, m_w_out, m_ln2_g, m_ln2_b, m_ffn2_gate, m_ffn2_up, m_ffn2_down, m_ln3_g, m_ln3_b, v_ln1_g, v_ln1_b, v_ffn1_gate, v_ffn1_up, v_ffn1_down, v_w_in, v_conv_w, v_conv_b, v_dt_bias, v_a_log, v_d_skip, v_attn_norm_w, v_ssd_norm_w, v_w_out, v_ln2_g, v_ln2_b, v_ffn2_gate, v_ffn2_up, v_ffn2_down, v_ln3_g, v_ln3_b):
    given = dict(locals())
    wts = {n: given[n] for n in _WEIGHT_ORDER}
    mom_m = {n: given["m_" + n] for n in _WEIGHT_ORDER}
    mom_v = {n: given["v_" + n] for n in _WEIGHT_ORDER}
    chip = 2 * lax.axis_index("x") + lax.axis_index("y")

    gathered = _gather_shards([wts[n][0].astype(BF16) for n, _ in _MATRICES] + [wts["conv_w"][0]])
    full = {n: _assemble(st, axis) for (n, axis), st in zip(_MATRICES, gathered)}
    full["conv_w"] = _assemble(gathered[-1], 1)
    for n in _VECTORS:
        full[n] = wts[n]

    loss, grad_x, g = _local_step(x, positions, loss_target, full)

    received = _exchange_partials([_split(g[n], axis).astype(BF16) for n, axis in _MATRICES])
    mine = [_sum_slots("sum_partials_" + n, st) for (n, _), st in zip(_MATRICES, received)]
    theirs = _sibling_swap(mine)

    small_shapes = [g[n].shape for n in _VECTORS] + [g["conv_w"].shape, (1,)]
    total = _small_allreduce(_pack_rows([g[n] for n in _VECTORS] + [g["conv_w"], loss[0, :1]]))
    small = _unpack_rows(total, small_shapes)
    loss_out = small[-1].reshape(())

    grads, deltas, new_m, new_v = {}, {}, {}, {}
    for (n, _), g_a, g_b in zip(_MATRICES, mine, theirs):
        res = _adamw("adamw_" + n, [g_a, g_b], wts[n][0], mom_m[n][0], mom_v[n][0])
        grads[n], deltas[n], new_m[n], new_v[n] = [r[None] for r in res]

    vec_shapes = [wts[n].shape for n in _VECTORS]
    res = _adamw("adamw_vectors", [_pack_rows(small[:len(_VECTORS)])], _pack_rows([wts[n] for n in _VECTORS]),
                 _pack_rows([mom_m[n] for n in _VECTORS]), _pack_rows([mom_v[n] for n in _VECTORS]))
    for dst, packed in zip((grads, deltas, new_m, new_v), res):
        for n, val in zip(_VECTORS, _unpack_rows(packed, vec_shapes)):
            dst[n] = val

    cols = conv_w.shape[2]
    g_conv = lax.dynamic_slice_in_dim(small[len(_VECTORS)], chip * cols, cols, axis=1)
    res = _adamw("adamw_conv_w", [g_conv], wts["conv_w"][0], mom_m["conv_w"][0], mom_v["conv_w"][0])
    grads["conv_w"], deltas["conv_w"], new_m["conv_w"], new_v["conv_w"] = [r[None] for r in res]

    return (loss_out, grad_x, *[grads[n] for n in _WEIGHT_ORDER], *[deltas[n] for n in _WEIGHT_ORDER],
            *[new_m[n] for n in _WEIGHT_ORDER], *[new_v[n] for n in _WEIGHT_ORDER])
```

```python
import functools

import numpy as np
import jax
import jax.numpy as jnp
from jax import lax
from jax.experimental import pallas as pl
from jax.experimental.pallas import tpu as pltpu

F32, BF16 = jnp.float32, jnp.bfloat16

D_MODEL = 1024
D_FF = 2816
N_HEADS = 12
HEAD_DIM = 64
D_ATTN = 768
D_SSD = 768
N_GROUPS = 4
HEADS_PER_GROUP = 3
D_STATE = 128
D_CONV = 1792
CONV_WIDTH = 4
ROPE_DIM = 16
ROPE_THETA = 500000.0
ALPHA = 2.0 ** 0.25
LN_EPS = 1e-5
RMS_EPS = 1e-6
ADAM_LR, ADAM_B1, ADAM_B2, ADAM_EPS, ADAM_WD, ADAM_STEP = 0.001, 0.9, 0.999, 1e-08, 0.01, 10

LANES = 128
GATE_UP_INTERLEAVE = 256
SEQ_BLOCK = 256
GROUP_LANES = 256
VMEM_LIMIT = 56 * 1024 * 1024
NEG = -1e30
MESH = pl.DeviceIdType.MESH
HIGHEST = lax.Precision.HIGHEST

_NT = (((1,), (1,)), ((), ()))
_TN = (((0,), (0,)), ((), ()))


def _params(*sem):
    return pltpu.CompilerParams(dimension_semantics=sem, vmem_limit_bytes=VMEM_LIMIT)


def _bf(v):
    return v.astype(BF16)


def _mm(name, pairs, *, scale=1.0, res=None, res_scale=1.0, out_dtype=F32, tm=512, tn=512):
    m, n = pairs[0][0].shape[0], pairs[0][1].shape[1]
    tm, tn = min(tm, m), min(tn, n)
    assert m % tm == 0 and n % tn == 0, (name, m, n, tm, tn)
    npair = len(pairs)

    def body(*refs):
        acc = None
        for a_ref, b_ref in zip(refs[:npair], refs[npair:2 * npair]):
            d = jnp.dot(_bf(a_ref[...]), b_ref[...], preferred_element_type=F32)
            acc = d if acc is None else acc + d
        if scale != 1.0:
            acc = acc * scale
        if res is not None:
            acc = acc + res_scale * refs[2 * npair][...]
        refs[-1][...] = acc.astype(out_dtype)

    in_specs = [pl.BlockSpec((tm, a.shape[1]), lambda i, j: (i, 0)) for a, _ in pairs]
    in_specs += [pl.BlockSpec((b.shape[0], tn), lambda i, j: (0, j)) for _, b in pairs]
    args = [a for a, _ in pairs] + [b for _, b in pairs]
    if res is not None:
        in_specs.append(pl.BlockSpec((tm, tn), lambda i, j: (i, j)))
        args.append(res)
    return pl.pallas_call(
        body, name=name, grid=(m // tm, n // tn), in_specs=in_specs,
        out_specs=pl.BlockSpec((tm, tn), lambda i, j: (i, j)),
        out_shape=jax.ShapeDtypeStruct((m, n), out_dtype),
        compiler_params=_params("parallel", "parallel"),
    )(*args)


def _mm_tn(name, x, dy, *, scale=1.0, tk=512, tn=512, tt=1024):
    t, k = x.shape
    n = dy.shape[1]
    tk, tn, tt = min(tk, k), min(tn, n), min(tt, t)
    assert k % tk == 0 and n % tn == 0 and t % tt == 0, (name, k, n, t)
    nt = t // tt

    def body(x_ref, dy_ref, o_ref):
        step = pl.program_id(2)
        d = lax.dot_general(_bf(x_ref[...]), _bf(dy_ref[...]), _TN, preferred_element_type=F32)

        @pl.when(step == 0)
        def _():
            o_ref[...] = d

        @pl.when(step > 0)
        def _():
            o_ref[...] += d

        if scale != 1.0:
            @pl.when(step == nt - 1)
            def _():
                o_ref[...] = o_ref[...] * scale

    return pl.pallas_call(
        body, name=name, grid=(k // tk, n // tn, nt),
        in_specs=[pl.BlockSpec((tt, tk), lambda i, j, s: (s, i)), pl.BlockSpec((tt, tn), lambda i, j, s: (s, j))],
        out_specs=pl.BlockSpec((tk, tn), lambda i, j, s: (i, j)),
        out_shape=jax.ShapeDtypeStruct((k, n), F32),
        compiler_params=_params("parallel", "parallel", "arbitrary"),
    )(x, dy)


def _mm_swiglu(name, x, wgu, *, tm=512):
    t, k = x.shape
    gi = GATE_UP_INTERLEAVE
    nj = wgu.shape[1] // (2 * gi)

    def body(x_ref, w_ref, au_ref, hm_ref):
        au = jnp.dot(_bf(x_ref[...]), w_ref[...], preferred_element_type=F32)
        a, u = au[:, :gi], au[:, gi:]
        au_ref[...] = _bf(au)
        hm_ref[...] = _bf(a * jax.nn.sigmoid(a) * u)

    return pl.pallas_call(
        body, name=name, grid=(t // tm, nj),
        in_specs=[pl.BlockSpec((tm, k), lambda i, j: (i, 0)), pl.BlockSpec((k, 2 * gi), lambda i, j: (0, j))],
        out_specs=[pl.BlockSpec((tm, 2 * gi), lambda i, j: (i, j)), pl.BlockSpec((tm, gi), lambda i, j: (i, j))],
        out_shape=[jax.ShapeDtypeStruct((t, 2 * gi * nj), BF16), jax.ShapeDtypeStruct((t, gi * nj), BF16)],
        compiler_params=_params("parallel", "parallel"),
    )(x, wgu)


def _mm_swiglu_bwd(name, dr, wdt, au, *, scale, tm=512):
    t, k = dr.shape
    gi = GATE_UP_INTERLEAVE
    nj = wdt.shape[1] // gi

    def body(dr_ref, w_ref, au_ref, o_ref):
        dhm = jnp.dot(_bf(dr_ref[...]), w_ref[...], preferred_element_type=F32) * scale
        au_v = au_ref[...].astype(F32)
        a, u = au_v[:, :gi], au_v[:, gi:]
        sig = jax.nn.sigmoid(a)
        da = dhm * u * (sig * (1.0 + a * (1.0 - sig)))
        du = dhm * (a * sig)
        o_ref[:, :gi] = _bf(da)
        o_ref[:, gi:] = _bf(du)

    return pl.pallas_call(
        body, name=name, grid=(t // tm, nj),
        in_specs=[pl.BlockSpec((tm, k), lambda i, j: (i, 0)), pl.BlockSpec((k, gi), lambda i, j: (0, j)),
                  pl.BlockSpec((tm, 2 * gi), lambda i, j: (i, j))],
        out_specs=pl.BlockSpec((tm, 2 * gi), lambda i, j: (i, j)),
        out_shape=jax.ShapeDtypeStruct((t, 2 * gi * nj), BF16),
        compiler_params=_params("parallel", "parallel"),
    )(dr, wdt, au)


def _layer_norm(r, g, b):
    mu = jnp.mean(r, axis=-1, keepdims=True)
    var = jnp.mean(jnp.square(r - mu), axis=-1, keepdims=True)
    return (r - mu) * lax.rsqrt(var + LN_EPS) * g + b


def _mm_res_ln(name, a, w, res, g, b, *, scale, tm=256):
    t, k = a.shape
    n = w.shape[1]

    def body(a_ref, w_ref, res_ref, g_ref, b_ref, y_ref, r_ref):
        r = ALPHA * res_ref[...] + scale * jnp.dot(_bf(a_ref[...]), w_ref[...], preferred_element_type=F32)
        r_ref[...] = r
        y_ref[...] = _layer_norm(r, g_ref[...], b_ref[...])

    row = lambda c: pl.BlockSpec((tm, c), lambda i: (i, 0))
    const = lambda shape: pl.BlockSpec(shape, lambda i: (0, 0))
    return pl.pallas_call(
        body, name=name, grid=(t // tm,),
        in_specs=[row(k), const((k, n)), row(n), const((1, n)), const((1, n))],
        out_specs=[row(n), row(n)],
        out_shape=[jax.ShapeDtypeStruct((t, n), F32), jax.ShapeDtypeStruct((t, n), F32)],
        compiler_params=_params("parallel"),
    )(a, w, res, g, b)


def _rowwise(name, fn, rows, consts, row_outs, acc_outs=(), tm=256):
    rows = [r if isinstance(r, tuple) else (r, r.shape[1]) for r in rows]
    t = rows[0][0].shape[0]
    tm = min(tm, t)
    assert t % tm == 0
    nr, nc, no, na = len(rows), len(consts), len(row_outs), len(acc_outs)

    def body(*refs):
        vals = [r[...] for r in refs[:nr + nc]]
        outs, accs = fn(*vals)
        for o_ref, o in zip(refs[nr + nc:nr + nc + no], outs):
            o_ref[...] = o.astype(o_ref.dtype)
        if na:
            step = pl.program_id(0)
            acc_refs = refs[nr + nc + no:]

            @pl.when(step == 0)
            def _():
                for a_ref, a in zip(acc_refs, accs):
                    a_ref[...] = a

            @pl.when(step > 0)
            def _():
                for a_ref, a in zip(acc_refs, accs):
                    a_ref[...] += a

    in_specs = [pl.BlockSpec((tm, w), lambda i: (i, 0)) for _, w in rows]
    in_specs += [pl.BlockSpec(c.shape, lambda i, nd=c.ndim: (0,) * nd) for c in consts]
    out_specs = [pl.BlockSpec((tm, c), lambda i: (i, 0)) for c, _ in row_outs]
    out_specs += [pl.BlockSpec(s, lambda i: (0, 0)) for s in acc_outs]
    out_shape = [jax.ShapeDtypeStruct((t, c), dt) for c, dt in row_outs]
    out_shape += [jax.ShapeDtypeStruct(s, F32) for s in acc_outs]
    res = pl.pallas_call(
        body, name=name, grid=(t // tm,), in_specs=in_specs, out_specs=out_specs, out_shape=out_shape,
        compiler_params=_params("arbitrary" if na else "parallel"),
    )(*[r for r, _ in rows], *consts)
    return res


def _ln_bwd(name, r, g, b, dy):
    def fn(r_v, dy_v, g_v, b_v):
        _, vjp = jax.vjp(_layer_norm, r_v, g_v, b_v)
        dr, dg, db = vjp(dy_v)
        return [dr], [dg, db]
    return _rowwise(name, fn, [r, dy], [g, b], [(r.shape[1], F32)], [(1, r.shape[1])] * 2)


def _ln_loss_bwd(name, r, g, b, target):
    def fn(r_v, t_v, g_v, b_v):
        def loss_fn(rr, gg, bb):
            err = jnp.square(_layer_norm(rr, gg, bb) - t_v)
            return 0.5 * jnp.sum(jnp.mean(err, axis=-1, keepdims=True), axis=0, keepdims=True)
        loss, vjp = jax.vjp(loss_fn, r_v, g_v, b_v)
        dr, dg, db = vjp(jnp.ones((1, 1), F32))
        return [dr], [dg, db, jnp.broadcast_to(loss, (1, LANES))]
    return _rowwise(name, fn, [r, target], [g, b], [(r.shape[1], F32)], [(1, r.shape[1])] * 2 + [(1, LANES)])


def _rope_tables(posf, invf, sgn):
    ang = posf * invf
    return jnp.cos(ang), jnp.sin(ang) * sgn


def _rope_apply(tv, cos, sin):
    lane = lax.broadcasted_iota(jnp.int32, cos.shape, 1)
    first = (lane % HEAD_DIM) < (ROPE_DIM // 2)
    outs = []
    for gidx in range(tv.shape[1] // LANES):
        tg = tv[:, LANES * gidx:LANES * (gidx + 1)]
        sw = jnp.where(first, pltpu.roll(tg, LANES - ROPE_DIM // 2, 1), pltpu.roll(tg, ROPE_DIM // 2, 1))
        outs.append(tg * cos + sw * sin)
    return jnp.concatenate(outs, axis=1)


def _rope_fwd(qk, posf, invf, sgn):
    def fn(qk_v, pos_v, invf_v, sgn_v):
        cos, sin = _rope_tables(pos_v, invf_v, sgn_v)
        q = _rope_apply(qk_v[:, :D_ATTN], cos, sin) * (HEAD_DIM ** -0.5)
        k = _rope_apply(qk_v[:, D_ATTN:], cos, sin)
        return [q, k, jnp.concatenate([cos, sin], axis=1)], []
    return _rowwise("rope_fwd", fn, [qk, posf], [invf, sgn], [(D_ATTN, BF16), (D_ATTN, BF16), (2 * LANES, F32)])


def _rope_bwd(dq, dk, cs):
    def fn(dq_v, dk_v, cs_v):
        cos, sin = cs_v[:, :LANES], -cs_v[:, LANES:]
        gq = _rope_apply(dq_v * (HEAD_DIM ** -0.5), cos, sin)
        gk = _rope_apply(dk_v, cos, sin)
        return [jnp.concatenate([gq, gk], axis=1)], []
    return _rowwise("rope_bwd", fn, [dq, dk, cs], [], [(2 * D_ATTN, BF16)])[0]


def _rms(v, w):
    return v * lax.rsqrt(jnp.mean(v * v, axis=-1, keepdims=True) + RMS_EPS) * w


def _ungroup(yg):
    w = HEADS_PER_GROUP * HEAD_DIM
    return jnp.concatenate([yg[:, GROUP_LANES * g:GROUP_LANES * g + w] for g in range(N_GROUPS)], axis=1)


def _group(xs):
    w = HEADS_PER_GROUP * HEAD_DIM
    parts = []
    for g in range(N_GROUPS):
        parts += [xs[:, w * g:w * (g + 1)], jnp.zeros((xs.shape[0], GROUP_LANES - w), xs.dtype)]
    return jnp.concatenate(parts, axis=1)


def _norms_fn(attn, yg, xs, z, w_attn, w_ssd, dskip):
    a_n = _rms(attn, w_attn)
    y = _ungroup(yg) + dskip * xs
    y_n = _rms(y * (z * jax.nn.sigmoid(z)), w_ssd)
    return jnp.concatenate([a_n, y_n], axis=1)


def _norms_fwd(attn, yg, xbc, z, w_attn, w_ssd, dskip):
    def fn(*v):
        return [_norms_fn(*v)], []
    return _rowwise("norms_fwd", fn, [attn, yg, (xbc, D_SSD), z], [w_attn, w_ssd, dskip], [(D_ATTN + D_SSD, BF16)])[0]


def _norms_bwd(attn, yg, xbc, z, w_attn, w_ssd, dskip, dcat):
    def fn(attn_v, yg_v, xs_v, z_v, dcat_v, wa_v, ws_v, dk_v):
        _, vjp = jax.vjp(_norms_fn, attn_v, yg_v, xs_v, z_v, wa_v, ws_v, dk_v)
        d_attn, d_yg, d_xs, d_z, d_wa, d_ws, d_dk = vjp(dcat_v)
        return [d_attn, d_yg, d_xs, d_z], [d_wa, d_ws, d_dk]
    return _rowwise("norms_bwd", fn, [attn, yg, (xbc, D_SSD), z, dcat], [w_attn, w_ssd, dskip],
                    [(D_ATTN, F32), (N_GROUPS * GROUP_LANES, F32), (D_SSD, F32), (D_SSD, BF16)], [(1, D_SSD)] * 3)


def _ssd_prep_fn(xs, dtp, dtb, alog, e_x, e_a):
    dt = jax.nn.softplus(dtp + dtb)
    a = -jnp.exp(alog)
    dtg = jnp.dot(dt, e_x, precision=HIGHEST, preferred_element_type=F32)
    xdtg = _group(xs) * dtg
    dag = jnp.dot(dt * a, e_a, precision=HIGHEST, preferred_element_type=F32)
    return xdtg, dag


def _ssd_prep_fwd(xbc, dtp, dtb, alog, e_x, e_a):
    def fn(xbc_v, dtp_v, dtb_v, alog_v, ex_v, ea_v):
        xdtg, dag = _ssd_prep_fn(xbc_v[:, :D_SSD], dtp_v, dtb_v, alog_v, ex_v, ea_v)
        return [xdtg, xbc_v[:, D_SSD:], dag], []
    return _rowwise("ssd_prep_fwd", fn, [xbc, dtp], [dtb, alog, e_x, e_a],
                    [(N_GROUPS * GROUP_LANES, BF16), (D_CONV - D_SSD, BF16), (N_GROUPS * LANES, F32)])


def _ssd_prep_bwd(xbc, dtp, dtb, alog, e_x, e_a, dxdtg, ddag, dxs_a, db, dc):
    def fn(xs_v, dtp_v, dxdtg_v, ddag_v, dxs_a_v, db_v, dc_v, dtb_v, alog_v, ex_v, ea_v):
        _, vjp = jax.vjp(lambda a, b, c, d: _ssd_prep_fn(a, b, c, d, ex_v, ea_v), xs_v, dtp_v, dtb_v, alog_v)
        dxs, ddtp, ddtb, dalog = vjp((dxdtg_v, ddag_v))
        return [jnp.concatenate([dxs + dxs_a_v, db_v, dc_v], axis=1), ddtp], [ddtb, dalog]
    return _rowwise("ssd_prep_bwd", fn, [(xbc, D_SSD), dtp, dxdtg, ddag, dxs_a, db, dc], [dtb, alog, e_x, e_a],
                    [(D_CONV, F32), (LANES, BF16)], [(1, LANES)] * 2)


def _shift_down(u, d):
    if d == 0:
        return u
    row = lax.broadcasted_iota(jnp.int32, u.shape, 0)
    return jnp.where(row >= d, pltpu.roll(u, d, 0), 0.0)


def _shift_up(u, d):
    if d == 0:
        return u
    s = u.shape[0]
    row = lax.broadcasted_iota(jnp.int32, u.shape, 0)
    return jnp.where(row < s - d, pltpu.roll(u, s - d, 0), 0.0)


def _conv_pre(u, w, b):
    acc = b
    for k in range(CONV_WIDTH):
        acc = acc + w[k:k + 1, :] * _shift_down(u, CONV_WIDTH - 1 - k)
    return acc


def _conv_fwd(u, w, b, *, tc=256):
    nb, s, c = u.shape

    def body(u_ref, w_ref, b_ref, o_ref):
        pre = _conv_pre(u_ref[0], w_ref[...], b_ref[...])
        o_ref[0] = pre * jax.nn.sigmoid(pre)

    return pl.pallas_call(
        body, name="conv_fwd", grid=(c // tc, nb),
        in_specs=[pl.BlockSpec((1, s, tc), lambda j, i: (i, 0, j)), pl.BlockSpec((CONV_WIDTH, tc), lambda j, i: (0, j)),
                  pl.BlockSpec((1, tc), lambda j, i: (0, j))],
        out_specs=pl.BlockSpec((1, s, tc), lambda j, i: (i, 0, j)),
        out_shape=jax.ShapeDtypeStruct((nb, s, c), F32),
        compiler_params=_params("parallel", "parallel"),
    )(u, w, b)


def _conv_bwd(u, w, b, dout, *, tc=256):
    nb, s, c = u.shape

    def body(u_ref, w_ref, b_ref, d_ref, du_ref, dw_ref, db_ref):
        uv, wv = u_ref[0], w_ref[...]
        pre = _conv_pre(uv, wv, b_ref[...])
        sig = jax.nn.sigmoid(pre)
        dpre = d_ref[0] * (sig * (1.0 + pre * (1.0 - sig)))
        du = jnp.zeros_like(uv)
        dws = []
        for k in range(CONV_WIDTH):
            du = du + wv[k:k + 1, :] * _shift_up(dpre, CONV_WIDTH - 1 - k)
            dws.append(jnp.sum(dpre * _shift_down(uv, CONV_WIDTH - 1 - k), axis=0, keepdims=True))
        du_ref[0] = _bf(du)
        dwv = jnp.concatenate(dws + [jnp.zeros((8 - CONV_WIDTH, tc), F32)], axis=0)
        dbv = jnp.sum(dpre, axis=0, keepdims=True)
        first = pl.program_id(1) == 0

        @pl.when(first)
        def _():
            dw_ref[...] = dwv
            db_ref[...] = dbv

        @pl.when(jnp.logical_not(first))
        def _():
            dw_ref[...] += dwv
            db_ref[...] += dbv

    blk = pl.BlockSpec((1, s, tc), lambda j, i: (i, 0, j))
    return pl.pallas_call(
        body, name="conv_bwd", grid=(c // tc, nb),
        in_specs=[blk, pl.BlockSpec((CONV_WIDTH, tc), lambda j, i: (0, j)), pl.BlockSpec((1, tc), lambda j, i: (0, j)), blk],
        out_specs=[blk, pl.BlockSpec((8, tc), lambda j, i: (0, j)), pl.BlockSpec((1, tc), lambda j, i: (0, j))],
        out_shape=[jax.ShapeDtypeStruct((nb, s, c), BF16), jax.ShapeDtypeStruct((8, c), F32), jax.ShapeDtypeStruct((1, c), F32)],
        compiler_params=_params("parallel", "arbitrary"),
    )(u, w, b, dout)


FWD_KEY_BLOCK = 256


def _branch_bias_table(seq, kb):
    ratio = SEQ_BLOCK // kb
    key = np.arange(kb)[None, :, None]
    query = np.arange(SEQ_BLOCK)[None, None, :]
    delta = (np.arange(seq // kb)[:, None, None] - (ratio - 1)) * kb + query - key
    cnt = np.zeros(delta.shape, np.float64)
    for window, dilation in ((128, 1), (512, 4), (2048, 16)):
        cnt += (delta >= 0) & (delta % dilation == 0) & (delta <= window)
    return jnp.asarray(np.where(cnt > 0, np.log(np.maximum(cnt, 1.0)), NEG).astype(np.float32))


HEADS_PER_BLOCK = LANES // HEAD_DIM


def _head_rows(v, h):
    row = lax.broadcasted_iota(jnp.int32, v.shape, 0)
    return jnp.where((row >= HEAD_DIM * h) & (row < HEAD_DIM * (h + 1)), v, jnp.zeros_like(v))


def _attn_fwd(q, k, v, bias):
    nb_, s, _ = q.shape
    ab, kb = SEQ_BLOCK, FWD_KEY_BLOCK
    nblk, nkb, ratio = s // ab, s // kb, ab // kb

    def body(q_ref, k_ref, v_ref, b_ref, o_ref, lse_ref, vt_s):
        i = pl.program_id(2)

        @pl.when(i == 0)
        def _():
            for jb in range(nkb):
                vt_s[jb] = v_ref[0, kb * jb:kb * (jb + 1), :].T

        qt = q_ref[0].T
        qts = [_head_rows(qt, h) for h in range(HEADS_PER_BLOCK)]

        def step(j, carry):
            ks = pl.ds(pl.multiple_of(j * kb, kb), kb)
            kj = k_ref[0, ks, :]
            lb = b_ref[ratio * i - j + (ratio - 1)]
            out = []
            for h in range(HEADS_PER_BLOCK):
                m, l, acc = carry[3 * h:3 * h + 3]
                st = jnp.dot(kj, qts[h], preferred_element_type=F32) + lb
                m_new = jnp.maximum(m, jnp.max(st, axis=0, keepdims=True))
                p = jnp.exp(st - m_new)
                a = jnp.exp(m - m_new)
                l = a * l + jnp.sum(p, axis=0, keepdims=True)
                vt = vt_s[j, HEAD_DIM * h:HEAD_DIM * (h + 1), :]
                acc = a * acc + jnp.dot(vt, _bf(p), preferred_element_type=F32)
                out += [m_new, l, acc]
            return tuple(out)

        init = (jnp.full((1, ab), NEG, F32), jnp.zeros((1, ab), F32), jnp.zeros((HEAD_DIM, ab), F32)) * HEADS_PER_BLOCK
        res = lax.fori_loop(0, ratio * (i + 1), step, init)
        ot = jnp.concatenate([res[3 * h + 2] / res[3 * h + 1] for h in range(HEADS_PER_BLOCK)], axis=0)
        o_ref[0] = ot.T
        rows = [res[3 * h] + jnp.log(res[3 * h + 1]) for h in range(HEADS_PER_BLOCK)]
        lse_ref[0, 0, 0] = jnp.concatenate(rows + [jnp.zeros((8 - HEADS_PER_BLOCK, ab), F32)], axis=0)

    qblk = pl.BlockSpec((1, ab, LANES), lambda b, hp, i: (b, i, hp))
    full = pl.BlockSpec((1, s, LANES), lambda b, hp, i: (b, 0, hp))
    return pl.pallas_call(
        body, name="attn_fwd", grid=(nb_, D_ATTN // LANES, nblk),
        in_specs=[qblk, full, full, pl.BlockSpec((nkb, kb, ab), lambda b, hp, i: (0, 0, 0))],
        out_specs=[qblk, pl.BlockSpec((1, 1, 1, 8, ab), lambda b, hp, i: (b, hp, i, 0, 0))],
        out_shape=[jax.ShapeDtypeStruct((nb_, s, D_ATTN), F32),
                   jax.ShapeDtypeStruct((nb_, D_ATTN // LANES, nblk, 8, ab), F32)],
        scratch_shapes=[pltpu.VMEM((nkb, LANES, kb), BF16)],
        compiler_params=_params("parallel", "parallel", "arbitrary"),
    )(q, k, v, bias)


def _attn_bwd(q, k, v, o, do, lse, bias):
    nb_, s, _ = q.shape
    ab = SEQ_BLOCK
    nblk = s // ab

    nh = HEADS_PER_BLOCK

    def body(q_ref, k_ref, v_ref, o_ref, do_ref, lse_ref, b_ref, dq_ref, dk_ref, dv_ref,
             qt_s, dot_s, kt_s, dqt_s, do16_s, d_s, dk_acc, dv_acc):
        for jb in range(nblk):
            sl = slice(ab * jb, ab * (jb + 1))
            qt, kt = q_ref[0, sl, :].T, k_ref[0, sl, :].T
            do = do_ref[0, sl, :]
            dot = do.T
            prod = dot * o_ref[0, sl, :].T
            do16_s[sl, :] = _bf(do)
            for h in range(nh):
                qt_s[nh * jb + h] = _head_rows(qt, h)
                kt_s[nh * jb + h] = _head_rows(kt, h)
                dot_s[nh * jb + h] = _head_rows(_bf(dot), h)
            d_s[jb] = jnp.concatenate(
                [jnp.sum(prod[HEAD_DIM * h:HEAD_DIM * (h + 1)], axis=0, keepdims=True) for h in range(nh)]
                + [jnp.zeros((8 - nh, ab), F32)], axis=0)
            dqt_s[jb] = jnp.zeros((LANES, ab), F32)

        def outer(j, carry):
            ks = pl.ds(pl.multiple_of(j * ab, ab), ab)
            kj, vj = k_ref[0, ks, :], v_ref[0, ks, :]
            dk_acc[...] = jnp.zeros_like(dk_acc)
            dv_acc[...] = jnp.zeros_like(dv_acc)

            def inner(i, c2):
                qs = pl.ds(pl.multiple_of(i * ab, ab), ab)
                qi, doi = q_ref[0, qs, :], do16_s[qs, :]
                lb = b_ref[i - j]
                for h in range(nh):
                    st = jnp.dot(kj, qt_s[nh * i + h], preferred_element_type=F32) + lb
                    pt = jnp.exp(st - lse_ref[0, 0, i, h:h + 1, :])
                    dpt = jnp.dot(vj, dot_s[nh * i + h], preferred_element_type=F32)
                    dst16 = _bf(pt * (dpt - d_s[i, h:h + 1, :]))
                    dv_acc[h] += jnp.dot(_bf(pt), doi, preferred_element_type=F32)
                    dk_acc[h] += jnp.dot(dst16, qi, preferred_element_type=F32)
                    dqt_s[i] += jnp.dot(kt_s[nh * j + h], dst16, preferred_element_type=F32)
                return c2

            lax.fori_loop(j, nblk, inner, 0)
            lane = lax.broadcasted_iota(jnp.int32, (ab, LANES), 1)
            dk_ref[0, ks, :] = jnp.where(lane < HEAD_DIM, dk_acc[0], dk_acc[1])
            dv_ref[0, ks, :] = _bf(jnp.where(lane < HEAD_DIM, dv_acc[0], dv_acc[1]))
            return carry

        lax.fori_loop(0, nblk, outer, 0)
        for jb in range(nblk):
            dq_ref[0, ab * jb:ab * (jb + 1), :] = dqt_s[jb].T

    assert nh == 2
    full = pl.BlockSpec((1, s, LANES), lambda b, hp: (b, 0, hp))
    return pl.pallas_call(
        body, name="attn_bwd", grid=(nb_, D_ATTN // LANES),
        in_specs=[full] * 5 + [pl.BlockSpec((1, 1, nblk, 8, ab), lambda b, hp: (b, hp, 0, 0, 0)),
                               pl.BlockSpec((nblk, ab, ab), lambda b, hp: (0, 0, 0))],
        out_specs=[full, full, full],
        out_shape=[jax.ShapeDtypeStruct((nb_, s, D_ATTN), F32), jax.ShapeDtypeStruct((nb_, s, D_ATTN), F32),
                   jax.ShapeDtypeStruct((nb_, s, D_ATTN), BF16)],
        scratch_shapes=[pltpu.VMEM((nh * nblk, LANES, ab), BF16), pltpu.VMEM((nh * nblk, LANES, ab), BF16),
                        pltpu.VMEM((nh * nblk, LANES, ab), BF16), pltpu.VMEM((nblk, LANES, ab), F32),
                        pltpu.VMEM((s, LANES), BF16), pltpu.VMEM((nblk, 8, ab), F32),
                        pltpu.VMEM((nh, ab, LANES), F32), pltpu.VMEM((nh, ab, LANES), F32)],
        compiler_params=_params("parallel", "parallel"),
    )(q, k, v, o, do, lse, bias)


def _cumsum_fwd(dag):
    nb_, s, c = dag.shape
    ab = SEQ_BLOCK

    def body(a_ref, o_ref, ot_ref):
        r = lax.broadcasted_iota(jnp.int32, (ab, ab), 0)
        cc = lax.broadcasted_iota(jnp.int32, (ab, ab), 1)
        tri = (r >= cc).astype(F32)
        carry = jnp.zeros((1, c), F32)
        for i in range(s // ab):
            loc = jnp.dot(tri, a_ref[0, ab * i:ab * (i + 1), :], precision=HIGHEST, preferred_element_type=F32) + carry
            o_ref[0, ab * i:ab * (i + 1), :] = loc
            ot_ref[0, :, ab * i:ab * (i + 1)] = loc.T
            carry = loc[ab - 1:ab, :]

    return pl.pallas_call(
        body, name="ssd_cumsum", grid=(nb_,),
        in_specs=[pl.BlockSpec((1, s, c), lambda b: (b, 0, 0))],
        out_specs=[pl.BlockSpec((1, s, c), lambda b: (b, 0, 0)), pl.BlockSpec((1, c, s), lambda b: (b, 0, 0))],
        out_shape=[jax.ShapeDtypeStruct((nb_, s, c), F32), jax.ShapeDtypeStruct((nb_, c, s), F32)],
        compiler_params=_params("parallel"),
    )(dag)


def _cumsum_bwd(dcol, drow):
    nb_, s, c = dcol.shape
    ab = SEQ_BLOCK

    def body(c_ref, r_ref, o_ref):
        r = lax.broadcasted_iota(jnp.int32, (ab, ab), 0)
        cc = lax.broadcasted_iota(jnp.int32, (ab, ab), 1)
        tri = (r <= cc).astype(F32)
        carry = jnp.zeros((1, c), F32)
        for i in reversed(range(s // ab)):
            rows = r_ref[0, :, ab * i:ab * (i + 1)].T
            parts = []
            for g in range(N_GROUPS):
                parts += [rows[:, 8 * g:8 * (g + 1)], jnp.zeros((ab, LANES - 8), F32)]
            blk = c_ref[0, ab * i:ab * (i + 1), :] + jnp.concatenate(parts, axis=1)
            loc = jnp.dot(tri, blk, precision=HIGHEST, preferred_element_type=F32) + carry
            o_ref[0, ab * i:ab * (i + 1), :] = loc
            carry = loc[0:1, :]

    return pl.pallas_call(
        body, name="ssd_cumsum_bwd", grid=(nb_,),
        in_specs=[pl.BlockSpec((1, s, c), lambda b: (b, 0, 0)), pl.BlockSpec((1, N_GROUPS * 8, s), lambda b: (b, 0, 0))],
        out_specs=pl.BlockSpec((1, s, c), lambda b: (b, 0, 0)),
        out_shape=jax.ShapeDtypeStruct((nb_, s, c), F32),
        compiler_params=_params("parallel"),
    )(dcol, drow)


def _causal_ok(i, j):
    ab = SEQ_BLOCK
    r = lax.broadcasted_iota(jnp.int32, (ab, ab), 0)
    c = lax.broadcasted_iota(jnp.int32, (ab, ab), 1)
    return (r + (i - j) * ab) >= c


def _ssd_fwd(xdtg, bc, acum, acum_t):
    nb_, s, _ = xdtg.shape
    ab = SEQ_BLOCK

    def body(x_ref, b_ref, c_ref, ac_ref, at_ref, y_ref):
        i = pl.program_id(2)
        ci = c_ref[0]
        acol = [ac_ref[0, :, j:j + 1] for j in range(HEADS_PER_GROUP)]

        def step(jb, accs):
            ks = pl.ds(pl.multiple_of(jb * ab, ab), ab)
            cb = lax.dot_general(ci, b_ref[0, ks, :], _NT, preferred_element_type=F32)
            ok = _causal_ok(i, jb)
            new = []
            for j in range(HEADS_PER_GROUP):
                decay = jnp.exp(jnp.where(ok, acol[j] - at_ref[0, j:j + 1, ks], NEG))
                g = _bf(cb * decay)
                new.append(accs[j] + jnp.dot(g, x_ref[0, ks, HEAD_DIM * j:HEAD_DIM * (j + 1)], preferred_element_type=F32))
            return tuple(new)

        accs = lax.fori_loop(0, i + 1, step, tuple(jnp.zeros((ab, HEAD_DIM), F32) for _ in range(HEADS_PER_GROUP)))
        y_ref[0] = jnp.concatenate(list(accs) + [jnp.zeros((ab, GROUP_LANES - HEADS_PER_GROUP * HEAD_DIM), F32)], axis=1)

    return pl.pallas_call(
        body, name="ssd_fwd", grid=(nb_, N_GROUPS, s // ab),
        in_specs=[pl.BlockSpec((1, s, GROUP_LANES), lambda b, g, i: (b, 0, g)),
                  pl.BlockSpec((1, s, D_STATE), lambda b, g, i: (b, 0, g)),
                  pl.BlockSpec((1, ab, D_STATE), lambda b, g, i: (b, i, N_GROUPS + g)),
                  pl.BlockSpec((1, ab, LANES), lambda b, g, i: (b, i, g)),
                  pl.BlockSpec((1, 8, s), lambda b, g, i: (b, (LANES // 8) * g, 0))],
        out_specs=pl.BlockSpec((1, ab, GROUP_LANES), lambda b, g, i: (b, i, g)),
        out_shape=jax.ShapeDtypeStruct((nb_, s, N_GROUPS * GROUP_LANES), F32),
        compiler_params=_params("parallel", "parallel", "parallel"),
    )(xdtg, bc, bc, acum, acum_t)


def _ssd_bwd(xdtg, bc, acum, acum_t, dyg):
    nb_, s, _ = xdtg.shape
    ab = SEQ_BLOCK
    nblk = s // ab
    hpg = HEADS_PER_GROUP

    def body(x_ref, b_ref, c_ref, ac_ref, at_ref, dy_ref, dx_ref, db_ref, dc_ref, dac_ref, dar_ref):
        dx_ref[...] = jnp.zeros_like(dx_ref)
        db_ref[...] = jnp.zeros_like(db_ref)
        dac_ref[...] = jnp.zeros_like(dac_ref)
        dar_ref[...] = jnp.zeros_like(dar_ref)

        def outer(i, carry):
            qs = pl.ds(pl.multiple_of(i * ab, ab), ab)
            ci = c_ref[0, qs, :]
            dyi = [_bf(dy_ref[0, qs, HEAD_DIM * j:HEAD_DIM * (j + 1)]) for j in range(hpg)]
            acol = [ac_ref[0, qs, j:j + 1] for j in range(hpg)]

            def inner(jb, st):
                dc_acc, rs = st[0], list(st[1:])
                ks = pl.ds(pl.multiple_of(jb * ab, ab), ab)
                bj = b_ref[0, ks, :]
                cb = lax.dot_general(ci, bj, _NT, preferred_element_type=F32)
                ok = _causal_ok(i, jb)
                dcb = jnp.zeros((ab, ab), F32)
                for j in range(hpg):
                    hs = slice(HEAD_DIM * j, HEAD_DIM * (j + 1))
                    decay = jnp.exp(jnp.where(ok, acol[j] - at_ref[0, j:j + 1, ks], NEG))
                    g = cb * decay
                    dg = lax.dot_general(dyi[j], x_ref[0, ks, hs], _NT, preferred_element_type=F32)
                    dx_ref[0, ks, hs] += lax.dot_general(_bf(g), dyi[j], _TN, preferred_element_type=F32)
                    dcb = dcb + dg * decay
                    mm = dg * g
                    rs[j] = rs[j] + jnp.sum(mm, axis=1, keepdims=True)
                    dar_ref[0, j:j + 1, ks] -= jnp.sum(mm, axis=0, keepdims=True)
                dcb16 = _bf(dcb)
                db_ref[0, ks, :] += lax.dot_general(dcb16, ci, _TN, preferred_element_type=F32)
                return (dc_acc + jnp.dot(dcb16, bj, preferred_element_type=F32), *rs)

            init = (jnp.zeros((ab, D_STATE), F32),) + tuple(jnp.zeros((ab, 1), F32) for _ in range(hpg))
            st = lax.fori_loop(0, i + 1, inner, init)
            dc_ref[0, qs, :] = st[0]
            for j in range(hpg):
                dac_ref[0, qs, j:j + 1] = st[1 + j]
            return carry

        lax.fori_loop(0, nblk, outer, 0)

    xblk = pl.BlockSpec((1, s, GROUP_LANES), lambda b, g: (b, 0, g))
    sblk = pl.BlockSpec((1, s, D_STATE), lambda b, g: (b, 0, g))
    tblk = pl.BlockSpec((1, 8, s), lambda b, g: (b, (LANES // 8) * g, 0))
    return pl.pallas_call(
        body, name="ssd_bwd", grid=(nb_, N_GROUPS),
        in_specs=[xblk, sblk, pl.BlockSpec((1, s, D_STATE), lambda b, g: (b, 0, N_GROUPS + g)), sblk, tblk, xblk],
        out_specs=[xblk, sblk, sblk, sblk, pl.BlockSpec((1, 8, s), lambda b, g: (b, g, 0))],
        out_shape=[jax.ShapeDtypeStruct((nb_, s, N_GROUPS * GROUP_LANES), F32),
                   jax.ShapeDtypeStruct((nb_, s, N_GROUPS * D_STATE), F32),
                   jax.ShapeDtypeStruct((nb_, s, N_GROUPS * D_STATE), F32),
                   jax.ShapeDtypeStruct((nb_, s, N_GROUPS * LANES), F32),
                   jax.ShapeDtypeStruct((nb_, N_GROUPS * 8, s), F32)],
        compiler_params=_params("parallel", "parallel"),
    )(xdtg, bc, bc, acum, acum_t, dyg)


def _interleave(wg, wu):
    k, f = wg.shape
    gi = GATE_UP_INTERLEAVE
    return jnp.stack([wg.reshape(k, f // gi, gi), wu.reshape(k, f // gi, gi)], axis=2).reshape(k, 2 * f)


def _deinterleave(wgu):
    k = wgu.shape[0]
    gi = GATE_UP_INTERLEAVE
    w = wgu.reshape(k, -1, 2, gi)
    return w[:, :, 0, :].reshape(k, -1), w[:, :, 1, :].reshape(k, -1)


def _head_expanders():
    e_x = np.zeros((LANES, N_GROUPS * GROUP_LANES), np.float32)
    e_a = np.zeros((LANES, N_GROUPS * LANES), np.float32)
    for h in range(N_HEADS):
        g, j = divmod(h, HEADS_PER_GROUP)
        e_x[h, GROUP_LANES * g + HEAD_DIM * j:GROUP_LANES * g + HEAD_DIM * (j + 1)] = 1.0
        e_a[h, LANES * g + j] = 1.0
    return jnp.asarray(e_x), jnp.asarray(e_a)


def _pad_lanes(v, n=LANES):
    return jnp.pad(v, ((0, 0), (0, n - v.shape[1])))


def _local_step(x, positions, target, w):
    nb, s, d = x.shape
    t = nb * s
    x2 = x.reshape(t, d)
    tgt2 = target.reshape(t, d)

    wgu1, wgu2 = _interleave(w["ffn1_gate"], w["ffn1_up"]), _interleave(w["ffn2_gate"], w["ffn2_up"])
    w_in = w["w_in"]
    wqk, wv, wz = w_in[:, :2 * D_ATTN], w_in[:, 2 * D_ATTN:3 * D_ATTN], w_in[:, 3 * D_ATTN:3 * D_ATTN + D_SSD]
    wxbc = w_in[:, 3 * D_ATTN + D_SSD:3 * D_ATTN + D_SSD + D_CONV]
    wdt = _pad_lanes(w_in[:, 3 * D_ATTN + D_SSD + D_CONV:])

    inv_freq = ROPE_THETA ** (-jnp.arange(0, ROPE_DIM, 2, dtype=F32) / ROPE_DIM)
    half = ROPE_DIM // 2
    head_invf = jnp.concatenate([inv_freq, inv_freq, jnp.zeros((HEAD_DIM - ROPE_DIM,), F32)])
    head_sgn = jnp.concatenate([-jnp.ones((half,), F32), jnp.ones((half,), F32), jnp.zeros((HEAD_DIM - ROPE_DIM,), F32)])
    invf = jnp.tile(head_invf, LANES // HEAD_DIM)[None, :]
    sgn = jnp.tile(head_sgn, LANES // HEAD_DIM)[None, :]
    posf = positions.astype(F32).reshape(t, 1)
    bias_fwd, bias_bwd = _branch_bias_table(s, FWD_KEY_BLOCK), _branch_bias_table(s, SEQ_BLOCK)
    e_x, e_a = _head_expanders()
    dtb, alog = _pad_lanes(w["dt_bias"]), _pad_lanes(w["a_log"])
    dskip = jnp.repeat(w["d_skip"], HEAD_DIM, axis=1)

    au1, hm1 = _mm_swiglu("ffn1_up", x2, wgu1)
    h1, r1 = _mm_res_ln("ffn1_down_ln1", hm1, w["ffn1_down"], x2, w["ln1_g"], w["ln1_b"], scale=0.5)

    qk = _mm("proj_qk", [(h1, wqk)], tn=768)
    v16 = _mm("proj_v", [(h1, wv)], tn=768, out_dtype=BF16)
    z = _mm("proj_z", [(h1, wz)], tn=768)
    xbc_pre = _mm("proj_xbc", [(h1, wxbc)], tn=896)
    dtp = _mm("proj_dt", [(h1, wdt)], tn=LANES)

    q16, k16, cs = _rope_fwd(qk, posf, invf, sgn)
    to3 = lambda a: a.reshape(nb, s, a.shape[-1])
    attn_o, lse = _attn_fwd(to3(q16), to3(k16), to3(v16), bias_fwd)

    xbc = _conv_fwd(to3(xbc_pre), w["conv_w"], w["conv_b"]).reshape(t, D_CONV)
    xdtg, bc16, dag = _ssd_prep_fwd(xbc, dtp, dtb, alog, e_x, e_a)
    acum, acum_t = _cumsum_fwd(to3(dag))
    yg = _ssd_fwd(to3(xdtg), to3(bc16), acum, acum_t)

    cat = _norms_fwd(attn_o.reshape(t, D_ATTN), yg.reshape(t, -1), xbc, z, w["attn_norm_w"], w["ssd_norm_w"], dskip)
    h2, r2 = _mm_res_ln("w_out_ln2", cat, w["w_out"], h1, w["ln2_g"], w["ln2_b"], scale=1.0)

    au2, hm2 = _mm_swiglu("ffn2_up", h2, wgu2)
    _, r3 = _mm_res_ln("ffn2_down_ln3", hm2, w["ffn2_down"], h2, w["ln3_g"], w["ln3_b"], scale=0.5)

    g = {}
    dr3, g["ln3_g"], g["ln3_b"], loss = _ln_loss_bwd("loss_ln3_bwd", r3, w["ln3_g"], w["ln3_b"], tgt2)

    dau2 = _mm_swiglu_bwd("ffn2_act_bwd", dr3, w["ffn2_down"].T, au2, scale=0.5)
    g["ffn2_down"] = _mm_tn("ffn2_down_dw", hm2, dr3, scale=0.5, tk=256, tn=1024)
    g["ffn2_gate"], g["ffn2_up"] = _deinterleave(_mm_tn("ffn2_up_dw", h2, dau2, tk=1024, tn=512))
    dh2 = _mm("ffn2_dx", [(dau2, wgu2.T)], res=dr3, res_scale=ALPHA)

    dr2, g["ln2_g"], g["ln2_b"] = _ln_bwd("ln2_bwd", r2, w["ln2_g"], w["ln2_b"], dh2)
    dcat = _mm("w_out_dx", [(dr2, w["w_out"].T)], tn=768)
    g["w_out"] = _mm_tn("w_out_dw", cat, dr2, tk=768, tn=1024)

    d_attn, dyg, dxs_a, dz16, g["attn_norm_w"], g["ssd_norm_w"], ddskip = _norms_bwd(
        attn_o.reshape(t, D_ATTN), yg.reshape(t, -1), xbc, z, w["attn_norm_w"], w["ssd_norm_w"], dskip, dcat)
    g["d_skip"] = ddskip.reshape(N_HEADS, HEAD_DIM).sum(axis=1)[None, :]

    dq, dk, dv16 = _attn_bwd(to3(q16), to3(k16), to3(v16), attn_o, to3(d_attn), lse, bias_bwd)
    dqk16 = _rope_bwd(dq.reshape(t, D_ATTN), dk.reshape(t, D_ATTN), cs)

    dxdtg, dbm, dcm, dacol, darow = _ssd_bwd(to3(xdtg), to3(bc16), acum, acum_t, to3(dyg))
    ddag = _cumsum_bwd(dacol, darow)
    dxbc, ddtp16, ddtb, dalog = _ssd_prep_bwd(xbc, dtp, dtb, alog, e_x, e_a, dxdtg.reshape(t, -1), ddag.reshape(t, -1),
                                               dxs_a, dbm.reshape(t, -1), dcm.reshape(t, -1))
    g["dt_bias"], g["a_log"] = ddtb[:, :N_HEADS], dalog[:, :N_HEADS]
    dxbc_pre16, dconv_w, g["conv_b"] = _conv_bwd(to3(xbc_pre), w["conv_w"], w["conv_b"], to3(dxbc))
    g["conv_w"] = dconv_w[:CONV_WIDTH]
    dxbc_pre16 = dxbc_pre16.reshape(t, D_CONV)
    dv16 = dv16.reshape(t, D_ATTN)

    dh1 = _mm("w_in_dx", [(dqk16, wqk.T), (dv16, wv.T), (dz16, wz.T), (dxbc_pre16, wxbc.T), (ddtp16, wdt.T)],
              res=dr2, res_scale=ALPHA)
    g["w_in"] = jnp.concatenate([
        _mm_tn("w_in_dw_qk", h1, dqk16, tk=1024, tn=512),
        _mm_tn("w_in_dw_v", h1, dv16, tk=1024, tn=768),
        _mm_tn("w_in_dw_z", h1, dz16, tk=1024, tn=768),
        _mm_tn("w_in_dw_xbc", h1, dxbc_pre16, tk=1024, tn=896),
        _mm_tn("w_in_dw_dt", h1, ddtp16, tk=1024, tn=LANES)[:, :N_HEADS],
    ], axis=1)

    dr1, g["ln1_g"], g["ln1_b"] = _ln_bwd("ln1_bwd", r1, w["ln1_g"], w["ln1_b"], dh1)
    dau1 = _mm_swiglu_bwd("ffn1_act_bwd", dr1, w["ffn1_down"].T, au1, scale=0.5)
    g["ffn1_down"] = _mm_tn("ffn1_down_dw", hm1, dr1, scale=0.5, tk=256, tn=1024)
    g["ffn1_gate"], g["ffn1_up"] = _deinterleave(_mm_tn("ffn1_up_dw", x2, dau1, tk=1024, tn=512))
    dx = _mm("ffn1_dx", [(dau1, wgu1.T)], res=dr1, res_scale=ALPHA)
    return loss, dx.reshape(nb, s, d), g


_HBM = pl.BlockSpec(memory_space=pltpu.HBM)
N_CHIPS = 4
N_DEVICES = 8


def _place():
    return lax.axis_index("x"), lax.axis_index("y"), lax.axis_index("c")


def _other_chips(x, y):
    return [(1 - x, y), (x, 1 - y), (1 - x, 1 - y)]


def _gather_shards(shards):
    n = len(shards)

    def body(*refs):
        ins, outs = refs[:n], refs[n:2 * n]
        send_sems, recv_sems, loc_sems = refs[2 * n:]
        x, y, c = _place()
        me = 2 * x + y
        peers = _other_chips(x, y)
        locs, sends = [], []
        for t in range(n):
            loc = pltpu.make_async_copy(ins[t], outs[t].at[me], loc_sems.at[t])
            loc.start()
            locs.append(loc)
            for p, (px, py) in enumerate(peers):
                cp = pltpu.make_async_remote_copy(ins[t], outs[t].at[me], send_sems.at[t, p], recv_sems.at[t, p],
                                                  device_id=(px, py, c), device_id_type=MESH)
                cp.start()
                sends.append(cp)
        for t in range(n):
            for p, (px, py) in enumerate(peers):
                pltpu.make_async_remote_copy(ins[t], outs[t].at[2 * px + py], send_sems.at[t, p], recv_sems.at[t, p],
                                             device_id=(px, py, c), device_id_type=MESH).wait_recv()
        for cp in sends:
            cp.wait_send()
        for loc in locs:
            loc.wait()

    return pl.pallas_call(
        body, name="gather_weights",
        in_specs=[_HBM] * n, out_specs=[_HBM] * n,
        out_shape=[jax.ShapeDtypeStruct((N_CHIPS,) + a.shape, a.dtype) for a in shards],
        scratch_shapes=[pltpu.SemaphoreType.DMA((n, N_CHIPS - 1)), pltpu.SemaphoreType.DMA((n, N_CHIPS - 1)),
                        pltpu.SemaphoreType.DMA((n,))],
    )(*shards)


def _exchange_partials(stacks):
    n = len(stacks)

    def body(*refs):
        ins, outs = refs[:n], refs[n:2 * n]
        send_sems, recv_sems, loc_sems = refs[2 * n:]
        x, y, c = _place()
        me = 2 * x + y
        peers = _other_chips(x, y)
        locs, sends = [], []
        for t in range(n):
            loc = pltpu.make_async_copy(ins[t].at[me], outs[t].at[me], loc_sems.at[t])
            loc.start()
            locs.append(loc)
            for p, (px, py) in enumerate(peers):
                cp = pltpu.make_async_remote_copy(ins[t].at[2 * px + py], outs[t].at[me], send_sems.at[t, p],
                                                  recv_sems.at[t, p], device_id=(px, py, c), device_id_type=MESH)
                cp.start()
                sends.append(cp)
        for t in range(n):
            for p, (px, py) in enumerate(peers):
                pltpu.make_async_remote_copy(ins[t].at[me], outs[t].at[2 * px + py], send_sems.at[t, p],
                                             recv_sems.at[t, p], device_id=(px, py, c), device_id_type=MESH).wait_recv()
        for cp in sends:
            cp.wait_send()
        for loc in locs:
            loc.wait()

    return pl.pallas_call(
        body, name="exchange_partials",
        in_specs=[_HBM] * n, out_specs=[_HBM] * n,
        out_shape=[jax.ShapeDtypeStruct(a.shape, a.dtype) for a in stacks],
        scratch_shapes=[pltpu.SemaphoreType.DMA((n, N_CHIPS - 1)), pltpu.SemaphoreType.DMA((n, N_CHIPS - 1)),
                        pltpu.SemaphoreType.DMA((n,))],
    )(*stacks)


def _sibling_swap(arrs):
    n = len(arrs)

    def body(*refs):
        ins, outs = refs[:n], refs[n:2 * n]
        send_sems, recv_sems = refs[2 * n:]
        x, y, c = _place()
        cps = []
        for t in range(n):
            cp = pltpu.make_async_remote_copy(ins[t], outs[t], send_sems.at[t], recv_sems.at[t],
                                              device_id=(x, y, 1 - c), device_id_type=MESH)
            cp.start()
            cps.append(cp)
        for cp in cps:
            cp.wait()

    return pl.pallas_call(
        body, name="sibling_swap",
        in_specs=[_HBM] * n, out_specs=[_HBM] * n,
        out_shape=[jax.ShapeDtypeStruct(a.shape, a.dtype) for a in arrs],
        scratch_shapes=[pltpu.SemaphoreType.DMA((n,)), pltpu.SemaphoreType.DMA((n,))],
    )(*arrs)


def _small_allreduce(v):
    r = v.shape[0]

    def body(v_ref, tot_ref, slots, send_sems, recv_sems):
        x, y, c = _place()
        me = 4 * x + 2 * y + c
        slots[me] = v_ref[...]
        cps, peers = [], []
        for k in range(1, N_DEVICES):
            px = 1 - x if (k >> 2) & 1 else x
            py = 1 - y if (k >> 1) & 1 else y
            pc = 1 - c if k & 1 else c
            cp = pltpu.make_async_remote_copy(v_ref, slots.at[me], send_sems.at[k - 1], recv_sems.at[k - 1],
                                              device_id=(px, py, pc), device_id_type=MESH)
            cp.start()
            cps.append(cp)
            peers.append((px, py, pc))
        for k, (px, py, pc) in enumerate(peers):
            pltpu.make_async_remote_copy(v_ref, slots.at[4 * px + 2 * py + pc], send_sems.at[k], recv_sems.at[k],
                                         device_id=(px, py, pc), device_id_type=MESH).wait_recv()
        for cp in cps:
            cp.wait_send()
        acc = slots[0]
        for s in range(1, N_DEVICES):
            acc = acc + slots[s]
        tot_ref[...] = acc

    return pl.pallas_call(
        body, name="small_allreduce",
        in_specs=[pl.BlockSpec(memory_space=pltpu.VMEM)], out_specs=pl.BlockSpec(memory_space=pltpu.VMEM),
        out_shape=jax.ShapeDtypeStruct((r, LANES), F32),
        scratch_shapes=[pltpu.VMEM((N_DEVICES, r, LANES), F32), pltpu.SemaphoreType.DMA((N_DEVICES - 1,)),
                        pltpu.SemaphoreType.DMA((N_DEVICES - 1,))],
    )(v)


def _elementwise(name, fn, ins, out_dtypes):
    r, c = ins[0].shape[-2:]
    tr = next((cand for cand in (256, 176, 128, 64, 32, 16) if r % cand == 0), r)
    nin = len(ins)

    def body(*refs):
        outs = fn(*[ref[...] for ref in refs[:nin]])
        for o_ref, o in zip(refs[nin:], outs):
            o_ref[...] = o.astype(o_ref.dtype)

    in_specs = [pl.BlockSpec((tr, c), lambda i: (i, 0)) if a.ndim == 2 else pl.BlockSpec((a.shape[0], tr, c), lambda i: (0, i, 0))
                for a in ins]
    return pl.pallas_call(
        body, name=name, grid=(r // tr,), in_specs=in_specs,
        out_specs=[pl.BlockSpec((tr, c), lambda i: (i, 0)) for _ in out_dtypes],
        out_shape=[jax.ShapeDtypeStruct((r, c), dt) for dt in out_dtypes],
        compiler_params=_params("parallel"),
    )(*ins)


def _sum_slots(name, stack):
    def fn(v):
        acc = v[0].astype(F32)
        for s in range(1, v.shape[0]):
            acc = acc + v[s].astype(F32)
        return [acc]
    return _elementwise(name, fn, [stack], [F32])[0]


def _adamw(name, grads, w, m, v):
    ng = len(grads)

    def fn(*vals):
        g = vals[0] if ng == 1 else vals[0] + vals[1]
        w_v, m_v, v_v = vals[ng:]
        m2 = ADAM_B1 * m_v + (1.0 - ADAM_B1) * g
        v2 = ADAM_B2 * v_v + (1.0 - ADAM_B2) * jnp.square(g)
        m_hat = m2 / (1.0 - ADAM_B1 ** ADAM_STEP)
        v_hat = v2 / (1.0 - ADAM_B2 ** ADAM_STEP)
        delta = -ADAM_LR * (m_hat / (jnp.sqrt(v_hat) + ADAM_EPS) + ADAM_WD * w_v)
        return [g, delta, m2, v2]

    return _elementwise(name, fn, list(grads) + [w, m, v], [F32] * 4)


_MATRICES = (("ffn1_gate", 1), ("ffn1_up", 1), ("ffn1_down", 0), ("w_in", 1), ("w_out", 0),
             ("ffn2_gate", 1), ("ffn2_up", 1), ("ffn2_down", 0))
_VECTORS = ("ln1_g", "ln1_b", "conv_b", "dt_bias", "a_log", "d_skip", "attn_norm_w", "ssd_norm_w",
            "ln2_g", "ln2_b", "ln3_g", "ln3_b")
_WEIGHT_ORDER = ("ln1_g", "ln1_b", "ffn1_gate", "ffn1_up", "ffn1_down", "w_in", "conv_w", "conv_b", "dt_bias", "a_log",
                 "d_skip", "attn_norm_w", "ssd_norm_w", "w_out", "ln2_g", "ln2_b", "ffn2_gate", "ffn2_up", "ffn2_down",
                 "ln3_g", "ln3_b")


def _pack_rows(vectors):
    parts = []
    for vec in vectors:
        flat = vec.reshape(-1)
        parts.append(jnp.pad(flat, (0, (-flat.shape[0]) % LANES)))
    flat = jnp.concatenate(parts)
    flat = jnp.pad(flat, (0, (-flat.shape[0]) % (8 * LANES)))
    return flat.reshape(-1, LANES)


def _unpack_rows(packed, shapes):
    flat = packed.reshape(-1)
    out, off = [], 0
    for shape in shapes:
        size = int(np.prod(shape))
        out.append(flat[off:off + size].reshape(shape))
        off += size + (-size) % LANES
    return out


def _assemble(stack, axis):
    if axis == 0:
        return stack.reshape(-1, stack.shape[2])
    return jnp.concatenate([stack[s] for s in range(N_CHIPS)], axis=1)


def _split(full, axis):
    if axis == 0:
        return full.reshape(N_CHIPS, -1, full.shape[1])
    cols = full.shape[1] // N_CHIPS
    return jnp.stack([full[:, cols * s:cols * (s + 1)] for s in range(N_CHIPS)])


def kernel(x, positions, ln1_g, ln1_b, ffn1_gate, ffn1_up, ffn1_down, w_in, conv_w, conv_b, dt_bias, a_log, d_skip, attn_norm_w, ssd_norm_w, w_out, ln2_g, ln2_b, ffn2_gate, ffn2_up, ffn2_down, ln3_g, ln3_b, loss_target, m_ln1_g, m_ln1_b, m_ffn1_gate, m_ffn1_up, m_ffn1_down, m_w_in, m_conv_w, m_conv_b, m_dt_bias, m_a_log, m_d_skip, m_attn_norm_w, m_ssd_norm_w, m_w_out, m_ln2_g, m_ln2_b, m_ffn2_gate, m_ffn2_up, m_ffn2_down, m_ln3_g, m_ln3_b, v_ln1_g, v_ln1_b, v_ffn1_gate, v_ffn1_up, v_ffn1_down, v_w_in, v_conv_w, v_conv_b, v_dt_bias, v_a_log, v_d_skip, v_attn_norm_w, v_ssd_norm_w, v_w_out, v_ln2_g, v_ln2_b, v_ffn2_gate, v_ffn2_up, v_ffn2_down, v_ln3_g, v_ln3_b):
    given = dict(locals())
    wts = {n: given[n] for n in _WEIGHT_ORDER}
    mom_m = {n: given["m_" + n] for n in _WEIGHT_ORDER}
    mom_v = {n: given["v_" + n] for n in _WEIGHT_ORDER}
    chip = 2 * lax.axis_index("x") + lax.axis_index("y")

    gathered = _gather_shards([wts[n][0].astype(BF16) for n, _ in _MATRICES] + [wts["conv_w"][0]])
    full = {n: _assemble(st, axis) for (n, axis), st in zip(_MATRICES, gathered)}
    full["conv_w"] = _assemble(gathered[-1], 1)
    for n in _VECTORS:
        full[n] = wts[n]

    loss, grad_x, g = _local_step(x, positions, loss_target, full)

    received = _exchange_partials([_split(g[n], axis).astype(BF16) for n, axis in _MATRICES])
    mine = [_sum_slots("sum_partials_" + n, st) for (n, _), st in zip(_MATRICES, received)]
    theirs = _sibling_swap(mine)

    small_shapes = [g[n].shape for n in _VECTORS] + [g["conv_w"].shape, (1,)]
    total = _small_allreduce(_pack_rows([g[n] for n in _VECTORS] + [g["conv_w"], loss[0, :1]]))
    small = _unpack_rows(total, small_shapes)
    loss_out = small[-1].reshape(())

    grads, deltas, new_m, new_v = {}, {}, {}, {}
    for (n, _), g_a, g_b in zip(_MATRICES, mine, theirs):
        res = _adamw("adamw_" + n, [g_a, g_b], wts[n][0], mom_m[n][0], mom_v[n][0])
        grads[n], deltas[n], new_m[n], new_v[n] = [r[None] for r in res]

    vec_shapes = [wts[n].shape for n in _VECTORS]
    res = _adamw("adamw_vectors", [_pack_rows(small[:len(_VECTORS)])], _pack_rows([wts[n] for n in _VECTORS]),
                 _pack_rows([mom_m[n] for n in _VECTORS]), _pack_rows([mom_v[n] for n in _VECTORS]))
    for dst, packed in zip((grads, deltas, new_m, new_v), res):
        for n, val in zip(_VECTORS, _unpack_rows(packed, vec_shapes)):
            dst[n] = val

    cols = conv_w.shape[2]
    g_conv = lax.dynamic_slice_in_dim(small[len(_VECTORS)], chip * cols, cols, axis=1)
    res = _adamw("adamw_conv_w", [g_conv], wts["conv_w"][0], mom_m["conv_w"][0], mom_v["conv_w"][0])
    grads["conv_w"], deltas["conv_w"], new_m["conv_w"], new_v["conv_w"] = [r[None] for r in res]

    return (loss_out, grad_x, *[grads[n] for n in _WEIGHT_ORDER], *[deltas[n] for n in _WEIGHT_ORDER],
            *[new_m[n] for n in _WEIGHT_ORDER], *[new_v[n] for n in _WEIGHT_ORDER])
```

```python
import functools

import numpy as np
import jax
import jax.numpy as jnp
from jax import lax
from jax.experimental import pallas as pl
from jax.experimental.pallas import tpu as pltpu

F32, BF16 = jnp.float32, jnp.bfloat16

D_MODEL = 1024
D_FF = 2816
N_HEADS = 12
HEAD_DIM = 64
D_ATTN = 768
D_SSD = 768
N_GROUPS = 4
HEADS_PER_GROUP = 3
D_STATE = 128
D_CONV = 1792
CONV_WIDTH = 4
ROPE_DIM = 16
ROPE_THETA = 500000.0
ALPHA = 2.0 ** 0.25
LN_EPS = 1e-5
RMS_EPS = 1e-6
ADAM_LR, ADAM_B1, ADAM_B2, ADAM_EPS, ADAM_WD, ADAM_STEP = 0.001, 0.9, 0.999, 1e-08, 0.01, 10

LANES = 128
GATE_UP_INTERLEAVE = 256
SEQ_BLOCK = 256
GROUP_LANES = 256
VMEM_LIMIT = 56 * 1024 * 1024
NEG = -1e30
MESH = pl.DeviceIdType.MESH
HIGHEST = lax.Precision.HIGHEST

_NT = (((1,), (1,)), ((), ()))
_TN = (((0,), (0,)), ((), ()))


def _params(*sem):
    return pltpu.CompilerParams(dimension_semantics=sem, vmem_limit_bytes=VMEM_LIMIT)


def _bf(v):
    return v.astype(BF16)


def _mm(name, pairs, *, scale=1.0, res=None, res_scale=1.0, out_dtype=F32, tm=512, tn=512):
    m, n = pairs[0][0].shape[0], pairs[0][1].shape[1]
    tm, tn = min(tm, m), min(tn, n)
    assert m % tm == 0 and n % tn == 0, (name, m, n, tm, tn)
    npair = len(pairs)

    def body(*refs):
        acc = None
        for a_ref, b_ref in zip(refs[:npair], refs[npair:2 * npair]):
            d = jnp.dot(_bf(a_ref[...]), b_ref[...], preferred_element_type=F32)
            acc = d if acc is None else acc + d
        if scale != 1.0:
            acc = acc * scale
        if res is not None:
            acc = acc + res_scale * refs[2 * npair][...]
        refs[-1][...] = acc.astype(out_dtype)

    in_specs = [pl.BlockSpec((tm, a.shape[1]), lambda i, j: (i, 0)) for a, _ in pairs]
    in_specs += [pl.BlockSpec((b.shape[0], tn), lambda i, j: (0, j)) for _, b in pairs]
    args = [a for a, _ in pairs] + [b for _, b in pairs]
    if res is not None:
        in_specs.append(pl.BlockSpec((tm, tn), lambda i, j: (i, j)))
        args.append(res)
    return pl.pallas_call(
        body, name=name, grid=(m // tm, n // tn), in_specs=in_specs,
        out_specs=pl.BlockSpec((tm, tn), lambda i, j: (i, j)),
        out_shape=jax.ShapeDtypeStruct((m, n), out_dtype),
        compiler_params=_params("parallel", "parallel"),
    )(*args)


def _mm_tn(name, x, dy, *, scale=1.0, tk=512, tn=512, tt=1024):
    t, k = x.shape
    n = dy.shape[1]
    tk, tn, tt = min(tk, k), min(tn, n), min(tt, t)
    assert k % tk == 0 and n % tn == 0 and t % tt == 0, (name, k, n, t)
    nt = t // tt

    def body(x_ref, dy_ref, o_ref):
        step = pl.program_id(2)
        d = lax.dot_general(_bf(x_ref[...]), _bf(dy_ref[...]), _TN, preferred_element_type=F32)

        @pl.when(step == 0)
        def _():
            o_ref[...] = d

        @pl.when(step > 0)
        def _():
            o_ref[...] += d

        if scale != 1.0:
            @pl.when(step == nt - 1)
            def _():
                o_ref[...] = o_ref[...] * scale

    return pl.pallas_call(
        body, name=name, grid=(k // tk, n // tn, nt),
        in_specs=[pl.BlockSpec((tt, tk), lambda i, j, s: (s, i)), pl.BlockSpec((tt, tn), lambda i, j, s: (s, j))],
        out_specs=pl.BlockSpec((tk, tn), lambda i, j, s: (i, j)),
        out_shape=jax.ShapeDtypeStruct((k, n), F32),
        compiler_params=_params("parallel", "parallel", "arbitrary"),
    )(x, dy)


def _mm_tn_gate_up(name, x, dau, *, tt=1024):
    t, k = x.shape
    gi = GATE_UP_INTERLEAVE
    nj = dau.shape[1] // (2 * gi)
    tt = min(tt, t)
    nt = t // tt

    def body(x_ref, dy_ref, g_ref, u_ref):
        step = pl.program_id(1)
        d = lax.dot_general(_bf(x_ref[...]), dy_ref[...], _TN, preferred_element_type=F32)

        @pl.when(step == 0)
        def _():
            g_ref[...] = d[:, :gi]
            u_ref[...] = d[:, gi:]

        @pl.when(step > 0)
        def _():
            g_ref[...] += d[:, :gi]
            u_ref[...] += d[:, gi:]

    out = pl.BlockSpec((k, gi), lambda j, s: (0, j))
    return pl.pallas_call(
        body, name=name, grid=(nj, nt),
        in_specs=[pl.BlockSpec((tt, k), lambda j, s: (s, 0)), pl.BlockSpec((tt, 2 * gi), lambda j, s: (s, j))],
        out_specs=[out, out],
        out_shape=[jax.ShapeDtypeStruct((k, gi * nj), F32)] * 2,
        compiler_params=_params("parallel", "arbitrary"),
    )(x, dau)


def _mm_swiglu(name, x, wgu, *, tm=512):
    t, k = x.shape
    gi = GATE_UP_INTERLEAVE
    nj = wgu.shape[1] // (2 * gi)

    def body(x_ref, w_ref, au_ref, hm_ref):
        au = jnp.dot(_bf(x_ref[...]), w_ref[...], preferred_element_type=F32)
        a, u = au[:, :gi], au[:, gi:]
        au_ref[...] = _bf(au)
        hm_ref[...] = _bf(a * jax.nn.sigmoid(a) * u)

    return pl.pallas_call(
        body, name=name, grid=(t // tm, nj),
        in_specs=[pl.BlockSpec((tm, k), lambda i, j: (i, 0)), pl.BlockSpec((k, 2 * gi), lambda i, j: (0, j))],
        out_specs=[pl.BlockSpec((tm, 2 * gi), lambda i, j: (i, j)), pl.BlockSpec((tm, gi), lambda i, j: (i, j))],
        out_shape=[jax.ShapeDtypeStruct((t, 2 * gi * nj), BF16), jax.ShapeDtypeStruct((t, gi * nj), BF16)],
        compiler_params=_params("parallel", "parallel"),
    )(x, wgu)


def _mm_swiglu_bwd(name, dr, wdt, au, *, scale, tm=512):
    t, k = dr.shape
    gi = GATE_UP_INTERLEAVE
    nj = wdt.shape[1] // gi

    def body(dr_ref, w_ref, au_ref, o_ref):
        dhm = jnp.dot(_bf(dr_ref[...]), w_ref[...], preferred_element_type=F32) * scale
        au_v = au_ref[...].astype(F32)
        a, u = au_v[:, :gi], au_v[:, gi:]
        sig = jax.nn.sigmoid(a)
        da = dhm * u * (sig * (1.0 + a * (1.0 - sig)))
        du = dhm * (a * sig)
        o_ref[:, :gi] = _bf(da)
        o_ref[:, gi:] = _bf(du)

    return pl.pallas_call(
        body, name=name, grid=(t // tm, nj),
        in_specs=[pl.BlockSpec((tm, k), lambda i, j: (i, 0)), pl.BlockSpec((k, gi), lambda i, j: (0, j)),
                  pl.BlockSpec((tm, 2 * gi), lambda i, j: (i, j))],
        out_specs=pl.BlockSpec((tm, 2 * gi), lambda i, j: (i, j)),
        out_shape=jax.ShapeDtypeStruct((t, 2 * gi * nj), BF16),
        compiler_params=_params("parallel", "parallel"),
    )(dr, wdt, au)


def _layer_norm(r, g, b):
    mu = jnp.mean(r, axis=-1, keepdims=True)
    var = jnp.mean(jnp.square(r - mu), axis=-1, keepdims=True)
    return (r - mu) * lax.rsqrt(var + LN_EPS) * g + b


def _mm_res_ln(name, a, w, res, g, b, *, scale, tm=256):
    t, k = a.shape
    n = w.shape[1]

    def body(a_ref, w_ref, res_ref, g_ref, b_ref, y_ref, r_ref):
        r = ALPHA * res_ref[...] + scale * jnp.dot(_bf(a_ref[...]), w_ref[...], preferred_element_type=F32)
        r_ref[...] = r
        y_ref[...] = _layer_norm(r, g_ref[...], b_ref[...])

    row = lambda c: pl.BlockSpec((tm, c), lambda i: (i, 0))
    const = lambda shape: pl.BlockSpec(shape, lambda i: (0, 0))
    return pl.pallas_call(
        body, name=name, grid=(t // tm,),
        in_specs=[row(k), const((k, n)), row(n), const((1, n)), const((1, n))],
        out_specs=[row(n), row(n)],
        out_shape=[jax.ShapeDtypeStruct((t, n), F32), jax.ShapeDtypeStruct((t, n), F32)],
        compiler_params=_params("parallel"),
    )(a, w, res, g, b)


def _rowwise(name, fn, rows, consts, row_outs, acc_outs=(), tm=256):
    rows = [r if isinstance(r, tuple) else (r, r.shape[1]) for r in rows]
    t = rows[0][0].shape[0]
    tm = min(tm, t)
    assert t % tm == 0
    nr, nc, no, na = len(rows), len(consts), len(row_outs), len(acc_outs)

    def body(*refs):
        vals = [r[...] for r in refs[:nr + nc]]
        outs, accs = fn(*vals)
        for o_ref, o in zip(refs[nr + nc:nr + nc + no], outs):
            o_ref[...] = o.astype(o_ref.dtype)
        if na:
            step = pl.program_id(0)
            acc_refs = refs[nr + nc + no:]

            @pl.when(step == 0)
            def _():
                for a_ref, a in zip(acc_refs, accs):
                    a_ref[...] = a

            @pl.when(step > 0)
            def _():
                for a_ref, a in zip(acc_refs, accs):
                    a_ref[...] += a

    in_specs = [pl.BlockSpec((tm, w), lambda i: (i, 0)) for _, w in rows]
    in_specs += [pl.BlockSpec(c.shape, lambda i, nd=c.ndim: (0,) * nd) for c in consts]
    out_specs = [pl.BlockSpec((tm, c), lambda i: (i, 0)) for c, _ in row_outs]
    out_specs += [pl.BlockSpec(s, lambda i: (0, 0)) for s in acc_outs]
    out_shape = [jax.ShapeDtypeStruct((t, c), dt) for c, dt in row_outs]
    out_shape += [jax.ShapeDtypeStruct(s, F32) for s in acc_outs]
    res = pl.pallas_call(
        body, name=name, grid=(t // tm,), in_specs=in_specs, out_specs=out_specs, out_shape=out_shape,
        compiler_params=_params("arbitrary" if na else "parallel"),
    )(*[r for r, _ in rows], *consts)
    return res


def _ln_bwd(name, r, g, b, dy):
    def fn(r_v, dy_v, g_v, b_v):
        _, vjp = jax.vjp(_layer_norm, r_v, g_v, b_v)
        dr, dg, db = vjp(dy_v)
        return [dr], [dg, db]
    return _rowwise(name, fn, [r, dy], [g, b], [(r.shape[1], F32)], [(1, r.shape[1])] * 2)


def _ln_loss_bwd(name, r, g, b, target):
    def fn(r_v, t_v, g_v, b_v):
        def loss_fn(rr, gg, bb):
            err = jnp.square(_layer_norm(rr, gg, bb) - t_v)
            return 0.5 * jnp.sum(jnp.mean(err, axis=-1, keepdims=True), axis=0, keepdims=True)
        loss, vjp = jax.vjp(loss_fn, r_v, g_v, b_v)
        dr, dg, db = vjp(jnp.ones((1, 1), F32))
        return [dr], [dg, db, jnp.broadcast_to(loss, (1, LANES))]
    return _rowwise(name, fn, [r, target], [g, b], [(r.shape[1], F32)], [(1, r.shape[1])] * 2 + [(1, LANES)])


def _rope_tables(posf, invf, sgn):
    ang = posf * invf
    return jnp.cos(ang), jnp.sin(ang) * sgn


def _rope_apply(tv, cos, sin):
    lane = lax.broadcasted_iota(jnp.int32, cos.shape, 1)
    first = (lane % HEAD_DIM) < (ROPE_DIM // 2)
    outs = []
    for gidx in range(tv.shape[1] // LANES):
        tg = tv[:, LANES * gidx:LANES * (gidx + 1)]
        sw = jnp.where(first, pltpu.roll(tg, LANES - ROPE_DIM // 2, 1), pltpu.roll(tg, ROPE_DIM // 2, 1))
        outs.append(tg * cos + sw * sin)
    return jnp.concatenate(outs, axis=1)


def _rope_fwd(qk, posf, invf, sgn):
    def fn(qk_v, pos_v, invf_v, sgn_v):
        cos, sin = _rope_tables(pos_v, invf_v, sgn_v)
        q = _rope_apply(qk_v[:, :D_ATTN], cos, sin) * (HEAD_DIM ** -0.5)
        k = _rope_apply(qk_v[:, D_ATTN:], cos, sin)
        return [q, k, jnp.concatenate([cos, sin], axis=1)], []
    return _rowwise("rope_fwd", fn, [qk, posf], [invf, sgn], [(D_ATTN, BF16), (D_ATTN, BF16), (2 * LANES, F32)])


def _rope_bwd(dq, dk, cs):
    def fn(dq_v, dk_v, cs_v):
        cos, sin = cs_v[:, :LANES], -cs_v[:, LANES:]
        gq = _rope_apply(dq_v * (HEAD_DIM ** -0.5), cos, sin)
        gk = _rope_apply(dk_v, cos, sin)
        return [jnp.concatenate([gq, gk], axis=1)], []
    return _rowwise("rope_bwd", fn, [dq, dk, cs], [], [(2 * D_ATTN, BF16)])[0]


def _rms(v, w):
    return v * lax.rsqrt(jnp.mean(v * v, axis=-1, keepdims=True) + RMS_EPS) * w


def _ungroup(yg):
    w = HEADS_PER_GROUP * HEAD_DIM
    return jnp.concatenate([yg[:, GROUP_LANES * g:GROUP_LANES * g + w] for g in range(N_GROUPS)], axis=1)


def _group(xs):
    w = HEADS_PER_GROUP * HEAD_DIM
    parts = []
    for g in range(N_GROUPS):
        parts += [xs[:, w * g:w * (g + 1)], jnp.zeros((xs.shape[0], GROUP_LANES - w), xs.dtype)]
    return jnp.concatenate(parts, axis=1)


def _norms_fn(attn, yg, xs, z, w_attn, w_ssd, dskip):
    a_n = _rms(attn, w_attn)
    y = _ungroup(yg) + dskip * xs
    y_n = _rms(y * (z * jax.nn.sigmoid(z)), w_ssd)
    return jnp.concatenate([a_n, y_n], axis=1)


def _norms_fwd(attn, yg, xbc, z, w_attn, w_ssd, dskip):
    def fn(*v):
        return [_norms_fn(*v)], []
    return _rowwise("norms_fwd", fn, [attn, yg, (xbc, D_SSD), z], [w_attn, w_ssd, dskip], [(D_ATTN + D_SSD, BF16)])[0]


def _norms_bwd(attn, yg, xbc, z, w_attn, w_ssd, dskip, dcat):
    def fn(attn_v, yg_v, xs_v, z_v, dcat_v, wa_v, ws_v, dk_v):
        _, vjp = jax.vjp(_norms_fn, attn_v, yg_v, xs_v, z_v, wa_v, ws_v, dk_v)
        d_attn, d_yg, d_xs, d_z, d_wa, d_ws, d_dk = vjp(dcat_v)
        return [d_attn, d_yg, d_xs, d_z], [d_wa, d_ws, d_dk]
    return _rowwise("norms_bwd", fn, [attn, yg, (xbc, D_SSD), z, dcat], [w_attn, w_ssd, dskip],
                    [(D_ATTN, F32), (N_GROUPS * GROUP_LANES, F32), (D_SSD, F32), (D_SSD, BF16)], [(1, D_SSD)] * 3)


def _ssd_prep_fn(xs, dtp, dtb, alog, e_x, e_a):
    dt = jax.nn.softplus(dtp + dtb)
    a = -jnp.exp(alog)
    dtg = jnp.dot(dt, e_x, precision=HIGHEST, preferred_element_type=F32)
    xdtg = _group(xs) * dtg
    dag = jnp.dot(dt * a, e_a, precision=HIGHEST, preferred_element_type=F32)
    return xdtg, dag


def _ssd_prep_fwd(xbc, dtp, dtb, alog, e_x, e_a):
    def fn(xbc_v, dtp_v, dtb_v, alog_v, ex_v, ea_v):
        xdtg, dag = _ssd_prep_fn(xbc_v[:, :D_SSD], dtp_v, dtb_v, alog_v, ex_v, ea_v)
        return [xdtg, xbc_v[:, D_SSD:], dag], []
    return _rowwise("ssd_prep_fwd", fn, [xbc, dtp], [dtb, alog, e_x, e_a],
                    [(N_GROUPS * GROUP_LANES, BF16), (D_CONV - D_SSD, BF16), (N_GROUPS * LANES, F32)])


def _ssd_prep_bwd(xbc, dtp, dtb, alog, e_x, e_a, dxdtg, ddag, dxs_a, db, dc):
    def fn(xs_v, dtp_v, dxdtg_v, ddag_v, dxs_a_v, db_v, dc_v, dtb_v, alog_v, ex_v, ea_v):
        _, vjp = jax.vjp(lambda a, b, c, d: _ssd_prep_fn(a, b, c, d, ex_v, ea_v), xs_v, dtp_v, dtb_v, alog_v)
        dxs, ddtp, ddtb, dalog = vjp((dxdtg_v, ddag_v))
        return [jnp.concatenate([dxs + dxs_a_v, db_v, dc_v], axis=1), ddtp], [ddtb, dalog]
    return _rowwise("ssd_prep_bwd", fn, [(xbc, D_SSD), dtp, dxdtg, ddag, dxs_a, db, dc], [dtb, alog, e_x, e_a],
                    [(D_CONV, F32), (LANES, BF16)], [(1, LANES)] * 2)


def _shift_down(u, d):
    if d == 0:
        return u
    row = lax.broadcasted_iota(jnp.int32, u.shape, 0)
    return jnp.where(row >= d, pltpu.roll(u, d, 0), 0.0)


def _shift_up(u, d):
    if d == 0:
        return u
    s = u.shape[0]
    row = lax.broadcasted_iota(jnp.int32, u.shape, 0)
    return jnp.where(row < s - d, pltpu.roll(u, s - d, 0), 0.0)


def _conv_pre(u, w, b):
    acc = b
    for k in range(CONV_WIDTH):
        acc = acc + w[k:k + 1, :] * _shift_down(u, CONV_WIDTH - 1 - k)
    return acc


def _conv_fwd(u, w, b, *, tc=256):
    nb, s, c = u.shape

    def body(u_ref, w_ref, b_ref, o_ref):
        pre = _conv_pre(u_ref[0], w_ref[...], b_ref[...])
        o_ref[0] = pre * jax.nn.sigmoid(pre)

    return pl.pallas_call(
        body, name="conv_fwd", grid=(c // tc, nb),
        in_specs=[pl.BlockSpec((1, s, tc), lambda j, i: (i, 0, j)), pl.BlockSpec((CONV_WIDTH, tc), lambda j, i: (0, j)),
                  pl.BlockSpec((1, tc), lambda j, i: (0, j))],
        out_specs=pl.BlockSpec((1, s, tc), lambda j, i: (i, 0, j)),
        out_shape=jax.ShapeDtypeStruct((nb, s, c), F32),
        compiler_params=_params("parallel", "parallel"),
    )(u, w, b)


def _conv_bwd(u, w, b, dout, *, tc=256):
    nb, s, c = u.shape

    def body(u_ref, w_ref, b_ref, d_ref, du_ref, dw_ref, db_ref):
        uv, wv = u_ref[0], w_ref[...]
        pre = _conv_pre(uv, wv, b_ref[...])
        sig = jax.nn.sigmoid(pre)
        dpre = d_ref[0] * (sig * (1.0 + pre * (1.0 - sig)))
        du = jnp.zeros_like(uv)
        dws = []
        for k in range(CONV_WIDTH):
            du = du + wv[k:k + 1, :] * _shift_up(dpre, CONV_WIDTH - 1 - k)
            dws.append(jnp.sum(dpre * _shift_down(uv, CONV_WIDTH - 1 - k), axis=0, keepdims=True))
        du_ref[0] = _bf(du)
        dwv = jnp.concatenate(dws + [jnp.zeros((8 - CONV_WIDTH, tc), F32)], axis=0)
        dbv = jnp.sum(dpre, axis=0, keepdims=True)
        first = pl.program_id(1) == 0

        @pl.when(first)
        def _():
            dw_ref[...] = dwv
            db_ref[...] = dbv

        @pl.when(jnp.logical_not(first))
        def _():
            dw_ref[...] += dwv
            db_ref[...] += dbv

    blk = pl.BlockSpec((1, s, tc), lambda j, i: (i, 0, j))
    return pl.pallas_call(
        body, name="conv_bwd", grid=(c // tc, nb),
        in_specs=[blk, pl.BlockSpec((CONV_WIDTH, tc), lambda j, i: (0, j)), pl.BlockSpec((1, tc), lambda j, i: (0, j)), blk],
        out_specs=[blk, pl.BlockSpec((8, tc), lambda j, i: (0, j)), pl.BlockSpec((1, tc), lambda j, i: (0, j))],
        out_shape=[jax.ShapeDtypeStruct((nb, s, c), BF16), jax.ShapeDtypeStruct((8, c), F32), jax.ShapeDtypeStruct((1, c), F32)],
        compiler_params=_params("parallel", "arbitrary"),
    )(u, w, b, dout)


FWD_KEY_BLOCK = 256


def _branch_bias_table(seq, kb):
    ratio = SEQ_BLOCK // kb
    key = np.arange(kb)[None, :, None]
    query = np.arange(SEQ_BLOCK)[None, None, :]
    delta = (np.arange(seq // kb)[:, None, None] - (ratio - 1)) * kb + query - key
    cnt = np.zeros(delta.shape, np.float64)
    for window, dilation in ((128, 1), (512, 4), (2048, 16)):
        cnt += (delta >= 0) & (delta % dilation == 0) & (delta <= window)
    return jnp.asarray(np.where(cnt > 0, np.log(np.maximum(cnt, 1.0)), NEG).astype(np.float32))


HEADS_PER_BLOCK = LANES // HEAD_DIM


def _head_rows(v, h):
    row = lax.broadcasted_iota(jnp.int32, v.shape, 0)
    return jnp.where((row >= HEAD_DIM * h) & (row < HEAD_DIM * (h + 1)), v, jnp.zeros_like(v))


def _attn_fwd(q, k, v, bias):
    nb_, s, _ = q.shape
    ab, kb = SEQ_BLOCK, FWD_KEY_BLOCK
    nblk, nkb, ratio = s // ab, s // kb, ab // kb

    def body(q_ref, k_ref, v_ref, b_ref, o_ref, lse_ref, vt_s):
        i = pl.program_id(2)

        @pl.when(i == 0)
        def _():
            for jb in range(nkb):
                vt_s[jb] = v_ref[0, kb * jb:kb * (jb + 1), :].T

        qt = q_ref[0].T
        qts = [_head_rows(qt, h) for h in range(HEADS_PER_BLOCK)]

        def step(j, carry):
            ks = pl.ds(pl.multiple_of(j * kb, kb), kb)
            kj = k_ref[0, ks, :]
            lb = b_ref[ratio * i - j + (ratio - 1)]
            out = []
            for h in range(HEADS_PER_BLOCK):
                m, l, acc = carry[3 * h:3 * h + 3]
                st = jnp.dot(kj, qts[h], preferred_element_type=F32) + lb
                m_new = jnp.maximum(m, jnp.max(st, axis=0, keepdims=True))
                p = jnp.exp(st - m_new)
                a = jnp.exp(m - m_new)
                l = a * l + jnp.sum(p, axis=0, keepdims=True)
                vt = vt_s[j, HEAD_DIM * h:HEAD_DIM * (h + 1), :]
                acc = a * acc + jnp.dot(vt, _bf(p), preferred_element_type=F32)
                out += [m_new, l, acc]
            return tuple(out)

        init = (jnp.full((1, ab), NEG, F32), jnp.zeros((1, ab), F32), jnp.zeros((HEAD_DIM, ab), F32)) * HEADS_PER_BLOCK
        res = lax.fori_loop(0, ratio * (i + 1), step, init)
        ot = jnp.concatenate([res[3 * h + 2] / res[3 * h + 1] for h in range(HEADS_PER_BLOCK)], axis=0)
        o_ref[0] = ot.T
        rows = [res[3 * h] + jnp.log(res[3 * h + 1]) for h in range(HEADS_PER_BLOCK)]
        lse_ref[0, 0, 0] = jnp.concatenate(rows + [jnp.zeros((8 - HEADS_PER_BLOCK, ab), F32)], axis=0)

    qblk = pl.BlockSpec((1, ab, LANES), lambda b, hp, i: (b, i, hp))
    full = pl.BlockSpec((1, s, LANES), lambda b, hp, i: (b, 0, hp))
    return pl.pallas_call(
        body, name="attn_fwd", grid=(nb_, D_ATTN // LANES, nblk),
        in_specs=[qblk, full, full, pl.BlockSpec((nkb, kb, ab), lambda b, hp, i: (0, 0, 0))],
        out_specs=[qblk, pl.BlockSpec((1, 1, 1, 8, ab), lambda b, hp, i: (b, hp, i, 0, 0))],
        out_shape=[jax.ShapeDtypeStruct((nb_, s, D_ATTN), F32),
                   jax.ShapeDtypeStruct((nb_, D_ATTN // LANES, nblk, 8, ab), F32)],
        scratch_shapes=[pltpu.VMEM((nkb, LANES, kb), BF16)],
        compiler_params=_params("parallel", "parallel", "arbitrary"),
    )(q, k, v, bias)


def _attn_bwd(q, k, v, o, do, lse, bias):
    nb_, s, _ = q.shape
    ab = SEQ_BLOCK
    nblk = s // ab

    nh = HEADS_PER_BLOCK

    def body(q_ref, k_ref, v_ref, o_ref, do_ref, lse_ref, b_ref, dq_ref, dk_ref, dv_ref,
             qt_s, dot_s, kt_s, dqt_s, do16_s, d_s, dk_acc, dv_acc):
        for jb in range(nblk):
            sl = slice(ab * jb, ab * (jb + 1))
            qt, kt = q_ref[0, sl, :].T, k_ref[0, sl, :].T
            do = do_ref[0, sl, :]
            dot = do.T
            prod = dot * o_ref[0, sl, :].T
            do16_s[sl, :] = _bf(do)
            for h in range(nh):
                qt_s[nh * jb + h] = _head_rows(qt, h)
                kt_s[nh * jb + h] = _head_rows(kt, h)
                dot_s[nh * jb + h] = _head_rows(_bf(dot), h)
            d_s[jb] = jnp.concatenate(
                [jnp.sum(prod[HEAD_DIM * h:HEAD_DIM * (h + 1)], axis=0, keepdims=True) for h in range(nh)]
                + [jnp.zeros((8 - nh, ab), F32)], axis=0)
            dqt_s[jb] = jnp.zeros((LANES, ab), F32)

        def outer(j, carry):
            ks = pl.ds(pl.multiple_of(j * ab, ab), ab)
            kj, vj = k_ref[0, ks, :], v_ref[0, ks, :]
            dk_acc[...] = jnp.zeros_like(dk_acc)
            dv_acc[...] = jnp.zeros_like(dv_acc)

            def inner(i, c2):
                qs = pl.ds(pl.multiple_of(i * ab, ab), ab)
                qi, doi = q_ref[0, qs, :], do16_s[qs, :]
                lb = b_ref[i - j]
                for h in range(nh):
                    st = jnp.dot(kj, qt_s[nh * i + h], preferred_element_type=F32) + lb
                    pt = jnp.exp(st - lse_ref[0, 0, i, h:h + 1, :])
                    dpt = jnp.dot(vj, dot_s[nh * i + h], preferred_element_type=F32)
                    dst16 = _bf(pt * (dpt - d_s[i, h:h + 1, :]))
                    dv_acc[h] += jnp.dot(_bf(pt), doi, preferred_element_type=F32)
                    dk_acc[h] += jnp.dot(dst16, qi, preferred_element_type=F32)
                    dqt_s[i] += jnp.dot(kt_s[nh * j + h], dst16, preferred_element_type=F32)
                return c2

            lax.fori_loop(j, nblk, inner, 0)
            lane = lax.broadcasted_iota(jnp.int32, (ab, LANES), 1)
            dk_ref[0, ks, :] = jnp.where(lane < HEAD_DIM, dk_acc[0], dk_acc[1])
            dv_ref[0, ks, :] = _bf(jnp.where(lane < HEAD_DIM, dv_acc[0], dv_acc[1]))
            return carry

        lax.fori_loop(0, nblk, outer, 0)
        for jb in range(nblk):
            dq_ref[0, ab * jb:ab * (jb + 1), :] = dqt_s[jb].T

    assert nh == 2
    full = pl.BlockSpec((1, s, LANES), lambda b, hp: (b, 0, hp))
    return pl.pallas_call(
        body, name="attn_bwd", grid=(nb_, D_ATTN // LANES),
        in_specs=[full] * 5 + [pl.BlockSpec((1, 1, nblk, 8, ab), lambda b, hp: (b, hp, 0, 0, 0)),
                               pl.BlockSpec((nblk, ab, ab), lambda b, hp: (0, 0, 0))],
        out_specs=[full, full, full],
        out_shape=[jax.ShapeDtypeStruct((nb_, s, D_ATTN), F32), jax.ShapeDtypeStruct((nb_, s, D_ATTN), F32),
                   jax.ShapeDtypeStruct((nb_, s, D_ATTN), BF16)],
        scratch_shapes=[pltpu.VMEM((nh * nblk, LANES, ab), BF16), pltpu.VMEM((nh * nblk, LANES, ab), BF16),
                        pltpu.VMEM((nh * nblk, LANES, ab), BF16), pltpu.VMEM((nblk, LANES, ab), F32),
                        pltpu.VMEM((s, LANES), BF16), pltpu.VMEM((nblk, 8, ab), F32),
                        pltpu.VMEM((nh, ab, LANES), F32), pltpu.VMEM((nh, ab, LANES), F32)],
        compiler_params=_params("parallel", "parallel"),
    )(q, k, v, o, do, lse, bias)


def _cumsum_fwd(dag):
    nb_, s, c = dag.shape
    ab = SEQ_BLOCK

    def body(a_ref, o_ref, ot_ref):
        r = lax.broadcasted_iota(jnp.int32, (ab, ab), 0)
        cc = lax.broadcasted_iota(jnp.int32, (ab, ab), 1)
        tri = (r >= cc).astype(F32)
        carry = jnp.zeros((1, c), F32)
        for i in range(s // ab):
            loc = jnp.dot(tri, a_ref[0, ab * i:ab * (i + 1), :], precision=HIGHEST, preferred_element_type=F32) + carry
            o_ref[0, ab * i:ab * (i + 1), :] = loc
            ot_ref[0, :, ab * i:ab * (i + 1)] = loc.T
            carry = loc[ab - 1:ab, :]

    return pl.pallas_call(
        body, name="ssd_cumsum", grid=(nb_,),
        in_specs=[pl.BlockSpec((1, s, c), lambda b: (b, 0, 0))],
        out_specs=[pl.BlockSpec((1, s, c), lambda b: (b, 0, 0)), pl.BlockSpec((1, c, s), lambda b: (b, 0, 0))],
        out_shape=[jax.ShapeDtypeStruct((nb_, s, c), F32), jax.ShapeDtypeStruct((nb_, c, s), F32)],
        compiler_params=_params("parallel"),
    )(dag)


def _cumsum_bwd(dcol, drow):
    nb_, s, c = dcol.shape
    ab = SEQ_BLOCK

    def body(c_ref, r_ref, o_ref):
        r = lax.broadcasted_iota(jnp.int32, (ab, ab), 0)
        cc = lax.broadcasted_iota(jnp.int32, (ab, ab), 1)
        tri = (r <= cc).astype(F32)
        carry = jnp.zeros((1, c), F32)
        for i in reversed(range(s // ab)):
            rows = r_ref[0, :, ab * i:ab * (i + 1)].T
            parts = []
            for g in range(N_GROUPS):
                parts += [rows[:, 8 * g:8 * (g + 1)], jnp.zeros((ab, LANES - 8), F32)]
            blk = c_ref[0, ab * i:ab * (i + 1), :] + jnp.concatenate(parts, axis=1)
            loc = jnp.dot(tri, blk, precision=HIGHEST, preferred_element_type=F32) + carry
            o_ref[0, ab * i:ab * (i + 1), :] = loc
            carry = loc[0:1, :]

    return pl.pallas_call(
        body, name="ssd_cumsum_bwd", grid=(nb_,),
        in_specs=[pl.BlockSpec((1, s, c), lambda b: (b, 0, 0)), pl.BlockSpec((1, N_GROUPS * 8, s), lambda b: (b, 0, 0))],
        out_specs=pl.BlockSpec((1, s, c), lambda b: (b, 0, 0)),
        out_shape=jax.ShapeDtypeStruct((nb_, s, c), F32),
        compiler_params=_params("parallel"),
    )(dcol, drow)


def _causal_ok(i, j):
    ab = SEQ_BLOCK
    r = lax.broadcasted_iota(jnp.int32, (ab, ab), 0)
    c = lax.broadcasted_iota(jnp.int32, (ab, ab), 1)
    return (r + (i - j) * ab) >= c


def _ssd_fwd(xdtg, bc, acum, acum_t):
    nb_, s, _ = xdtg.shape
    ab = SEQ_BLOCK

    def body(x_ref, b_ref, c_ref, ac_ref, at_ref, y_ref):
        i = pl.program_id(2)
        ci = c_ref[0]
        acol = [ac_ref[0, :, j:j + 1] for j in range(HEADS_PER_GROUP)]

        def step(jb, accs):
            ks = pl.ds(pl.multiple_of(jb * ab, ab), ab)
            cb = lax.dot_general(ci, b_ref[0, ks, :], _NT, preferred_element_type=F32)
            ok = _causal_ok(i, jb)
            new = []
            for j in range(HEADS_PER_GROUP):
                decay = jnp.exp(jnp.where(ok, acol[j] - at_ref[0, j:j + 1, ks], NEG))
                g = _bf(cb * decay)
                new.append(accs[j] + jnp.dot(g, x_ref[0, ks, HEAD_DIM * j:HEAD_DIM * (j + 1)], preferred_element_type=F32))
            return tuple(new)

        accs = lax.fori_loop(0, i + 1, step, tuple(jnp.zeros((ab, HEAD_DIM), F32) for _ in range(HEADS_PER_GROUP)))
        y_ref[0] = jnp.concatenate(list(accs) + [jnp.zeros((ab, GROUP_LANES - HEADS_PER_GROUP * HEAD_DIM), F32)], axis=1)

    return pl.pallas_call(
        body, name="ssd_fwd", grid=(nb_, N_GROUPS, s // ab),
        in_specs=[pl.BlockSpec((1, s, GROUP_LANES), lambda b, g, i: (b, 0, g)),
                  pl.BlockSpec((1, s, D_STATE), lambda b, g, i: (b, 0, g)),
                  pl.BlockSpec((1, ab, D_STATE), lambda b, g, i: (b, i, N_GROUPS + g)),
                  pl.BlockSpec((1, ab, LANES), lambda b, g, i: (b, i, g)),
                  pl.BlockSpec((1, 8, s), lambda b, g, i: (b, (LANES // 8) * g, 0))],
        out_specs=pl.BlockSpec((1, ab, GROUP_LANES), lambda b, g, i: (b, i, g)),
        out_shape=jax.ShapeDtypeStruct((nb_, s, N_GROUPS * GROUP_LANES), F32),
        compiler_params=_params("parallel", "parallel", "parallel"),
    )(xdtg, bc, bc, acum, acum_t)


def _ssd_bwd(xdtg, bc, acum, acum_t, dyg):
    nb_, s, _ = xdtg.shape
    ab = SEQ_BLOCK
    nblk = s // ab
    hpg = HEADS_PER_GROUP

    def body(x_ref, b_ref, c_ref, ac_ref, at_ref, dy_ref, dx_ref, db_ref, dc_ref, dac_ref, dar_ref):
        dx_ref[...] = jnp.zeros_like(dx_ref)
        db_ref[...] = jnp.zeros_like(db_ref)
        dac_ref[...] = jnp.zeros_like(dac_ref)
        dar_ref[...] = jnp.zeros_like(dar_ref)

        def outer(i, carry):
            qs = pl.ds(pl.multiple_of(i * ab, ab), ab)
            ci = c_ref[0, qs, :]
            dyi = [_bf(dy_ref[0, qs, HEAD_DIM * j:HEAD_DIM * (j + 1)]) for j in range(hpg)]
            acol = [ac_ref[0, qs, j:j + 1] for j in range(hpg)]

            def inner(jb, st):
                dc_acc, rs = st[0], list(st[1:])
                ks = pl.ds(pl.multiple_of(jb * ab, ab), ab)
                bj = b_ref[0, ks, :]
                cb = lax.dot_general(ci, bj, _NT, preferred_element_type=F32)
                ok = _causal_ok(i, jb)
                dcb = jnp.zeros((ab, ab), F32)
                for j in range(hpg):
                    hs = slice(HEAD_DIM * j, HEAD_DIM * (j + 1))
                    decay = jnp.exp(jnp.where(ok, acol[j] - at_ref[0, j:j + 1, ks], NEG))
                    g = cb * decay
                    dg = lax.dot_general(dyi[j], x_ref[0, ks, hs], _NT, preferred_element_type=F32)
                    dx_ref[0, ks, hs] += lax.dot_general(_bf(g), dyi[j], _TN, preferred_element_type=F32)
                    dcb = dcb + dg * decay
                    mm = dg * g
                    rs[j] = rs[j] + jnp.sum(mm, axis=1, keepdims=True)
                    dar_ref[0, j:j + 1, ks] -= jnp.sum(mm, axis=0, keepdims=True)
                dcb16 = _bf(dcb)
                db_ref[0, ks, :] += lax.dot_general(dcb16, ci, _TN, preferred_element_type=F32)
                return (dc_acc + jnp.dot(dcb16, bj, preferred_element_type=F32), *rs)

            init = (jnp.zeros((ab, D_STATE), F32),) + tuple(jnp.zeros((ab, 1), F32) for _ in range(hpg))
            st = lax.fori_loop(0, i + 1, inner, init)
            dc_ref[0, qs, :] = st[0]
            for j in range(hpg):
                dac_ref[0, qs, j:j + 1] = st[1 + j]
            return carry

        lax.fori_loop(0, nblk, outer, 0)

    xblk = pl.BlockSpec((1, s, GROUP_LANES), lambda b, g: (b, 0, g))
    sblk = pl.BlockSpec((1, s, D_STATE), lambda b, g: (b, 0, g))
    tblk = pl.BlockSpec((1, 8, s), lambda b, g: (b, (LANES // 8) * g, 0))
    return pl.pallas_call(
        body, name="ssd_bwd", grid=(nb_, N_GROUPS),
        in_specs=[xblk, sblk, pl.BlockSpec((1, s, D_STATE), lambda b, g: (b, 0, N_GROUPS + g)), sblk, tblk, xblk],
        out_specs=[xblk, sblk, sblk, sblk, pl.BlockSpec((1, 8, s), lambda b, g: (b, g, 0))],
        out_shape=[jax.ShapeDtypeStruct((nb_, s, N_GROUPS * GROUP_LANES), F32),
                   jax.ShapeDtypeStruct((nb_, s, N_GROUPS * D_STATE), F32),
                   jax.ShapeDtypeStruct((nb_, s, N_GROUPS * D_STATE), F32),
                   jax.ShapeDtypeStruct((nb_, s, N_GROUPS * LANES), F32),
                   jax.ShapeDtypeStruct((nb_, N_GROUPS * 8, s), F32)],
        compiler_params=_params("parallel", "parallel"),
    )(xdtg, bc, bc, acum, acum_t, dyg)


def _interleave(wg, wu):
    k, f = wg.shape
    gi = GATE_UP_INTERLEAVE
    return jnp.stack([wg.reshape(k, f // gi, gi), wu.reshape(k, f // gi, gi)], axis=2).reshape(k, 2 * f)


def _head_expanders():
    e_x = np.zeros((LANES, N_GROUPS * GROUP_LANES), np.float32)
    e_a = np.zeros((LANES, N_GROUPS * LANES), np.float32)
    for h in range(N_HEADS):
        g, j = divmod(h, HEADS_PER_GROUP)
        e_x[h, GROUP_LANES * g + HEAD_DIM * j:GROUP_LANES * g + HEAD_DIM * (j + 1)] = 1.0
        e_a[h, LANES * g + j] = 1.0
    return jnp.asarray(e_x), jnp.asarray(e_a)


def _pad_lanes(v, n=LANES):
    return jnp.pad(v, ((0, 0), (0, n - v.shape[1])))


def _local_step(x, positions, target, w):
    nb, s, d = x.shape
    t = nb * s
    x2 = x.reshape(t, d)
    tgt2 = target.reshape(t, d)

    wgu1, wgu2 = _interleave(w["ffn1_gate"], w["ffn1_up"]), _interleave(w["ffn2_gate"], w["ffn2_up"])
    w_in = w["w_in"]
    wqk, wv, wz = w_in[:, :2 * D_ATTN], w_in[:, 2 * D_ATTN:3 * D_ATTN], w_in[:, 3 * D_ATTN:3 * D_ATTN + D_SSD]
    wxbc = w_in[:, 3 * D_ATTN + D_SSD:3 * D_ATTN + D_SSD + D_CONV]
    wdt = _pad_lanes(w_in[:, 3 * D_ATTN + D_SSD + D_CONV:])

    inv_freq = ROPE_THETA ** (-jnp.arange(0, ROPE_DIM, 2, dtype=F32) / ROPE_DIM)
    half = ROPE_DIM // 2
    head_invf = jnp.concatenate([inv_freq, inv_freq, jnp.zeros((HEAD_DIM - ROPE_DIM,), F32)])
    head_sgn = jnp.concatenate([-jnp.ones((half,), F32), jnp.ones((half,), F32), jnp.zeros((HEAD_DIM - ROPE_DIM,), F32)])
    invf = jnp.tile(head_invf, LANES // HEAD_DIM)[None, :]
    sgn = jnp.tile(head_sgn, LANES // HEAD_DIM)[None, :]
    posf = positions.astype(F32).reshape(t, 1)
    bias_fwd, bias_bwd = _branch_bias_table(s, FWD_KEY_BLOCK), _branch_bias_table(s, SEQ_BLOCK)
    e_x, e_a = _head_expanders()
    dtb, alog = _pad_lanes(w["dt_bias"]), _pad_lanes(w["a_log"])
    dskip = jnp.repeat(w["d_skip"], HEAD_DIM, axis=1)

    au1, hm1 = _mm_swiglu("ffn1_up", x2, wgu1)
    h1, r1 = _mm_res_ln("ffn1_down_ln1", hm1, w["ffn1_down"], x2, w["ln1_g"], w["ln1_b"], scale=0.5)

    qk = _mm("proj_qk", [(h1, wqk)], tn=768)
    v16 = _mm("proj_v", [(h1, wv)], tn=768, out_dtype=BF16)
    z = _mm("proj_z", [(h1, wz)], tn=768)
    xbc_pre = _mm("proj_xbc", [(h1, wxbc)], tn=896)
    dtp = _mm("proj_dt", [(h1, wdt)], tn=LANES)

    q16, k16, cs = _rope_fwd(qk, posf, invf, sgn)
    to3 = lambda a: a.reshape(nb, s, a.shape[-1])
    attn_o, lse = _attn_fwd(to3(q16), to3(k16), to3(v16), bias_fwd)

    xbc = _conv_fwd(to3(xbc_pre), w["conv_w"], w["conv_b"]).reshape(t, D_CONV)
    xdtg, bc16, dag = _ssd_prep_fwd(xbc, dtp, dtb, alog, e_x, e_a)
    acum, acum_t = _cumsum_fwd(to3(dag))
    yg = _ssd_fwd(to3(xdtg), to3(bc16), acum, acum_t)

    cat = _norms_fwd(attn_o.reshape(t, D_ATTN), yg.reshape(t, -1), xbc, z, w["attn_norm_w"], w["ssd_norm_w"], dskip)
    h2, r2 = _mm_res_ln("w_out_ln2", cat, w["w_out"], h1, w["ln2_g"], w["ln2_b"], scale=1.0)

    au2, hm2 = _mm_swiglu("ffn2_up", h2, wgu2)
    _, r3 = _mm_res_ln("ffn2_down_ln3", hm2, w["ffn2_down"], h2, w["ln3_g"], w["ln3_b"], scale=0.5)

    g = {}
    dr3, g["ln3_g"], g["ln3_b"], loss = _ln_loss_bwd("loss_ln3_bwd", r3, w["ln3_g"], w["ln3_b"], tgt2)

    dau2 = _mm_swiglu_bwd("ffn2_act_bwd", dr3, w["ffn2_down"].T, au2, scale=0.5)
    g["ffn2_down"] = _mm_tn("ffn2_down_dw", hm2, dr3, scale=0.5, tk=D_FF // 2, tn=512)
    g["ffn2_gate"], g["ffn2_up"] = _mm_tn_gate_up("ffn2_up_dw", h2, dau2)
    dh2 = _mm("ffn2_dx", [(dau2, wgu2.T)], res=dr3, res_scale=ALPHA)

    dr2, g["ln2_g"], g["ln2_b"] = _ln_bwd("ln2_bwd", r2, w["ln2_g"], w["ln2_b"], dh2)
    dcat = _mm("w_out_dx", [(dr2, w["w_out"].T)], tn=768)
    g["w_out"] = _mm_tn("w_out_dw", cat, dr2, tk=768, tn=1024)

    d_attn, dyg, dxs_a, dz16, g["attn_norm_w"], g["ssd_norm_w"], ddskip = _norms_bwd(
        attn_o.reshape(t, D_ATTN), yg.reshape(t, -1), xbc, z, w["attn_norm_w"], w["ssd_norm_w"], dskip, dcat)
    g["d_skip"] = ddskip.reshape(N_HEADS, HEAD_DIM).sum(axis=1)[None, :]

    dq, dk, dv16 = _attn_bwd(to3(q16), to3(k16), to3(v16), attn_o, to3(d_attn), lse, bias_bwd)
    dqk16 = _rope_bwd(dq.reshape(t, D_ATTN), dk.reshape(t, D_ATTN), cs)

    dxdtg, dbm, dcm, dacol, darow = _ssd_bwd(to3(xdtg), to3(bc16), acum, acum_t, to3(dyg))
    ddag = _cumsum_bwd(dacol, darow)
    dxbc, ddtp16, ddtb, dalog = _ssd_prep_bwd(xbc, dtp, dtb, alog, e_x, e_a, dxdtg.reshape(t, -1), ddag.reshape(t, -1),
                                               dxs_a, dbm.reshape(t, -1), dcm.reshape(t, -1))
    g["dt_bias"], g["a_log"] = ddtb[:, :N_HEADS], dalog[:, :N_HEADS]
    dxbc_pre16, dconv_w, g["conv_b"] = _conv_bwd(to3(xbc_pre), w["conv_w"], w["conv_b"], to3(dxbc))
    g["conv_w"] = dconv_w[:CONV_WIDTH]
    dxbc_pre16 = dxbc_pre16.reshape(t, D_CONV)
    dv16 = dv16.reshape(t, D_ATTN)

    dh1 = _mm("w_in_dx", [(dqk16, wqk.T), (dv16, wv.T), (dz16, wz.T), (dxbc_pre16, wxbc.T), (ddtp16, wdt.T)],
              res=dr2, res_scale=ALPHA)
    g["w_in"] = jnp.concatenate([
        _mm_tn("w_in_dw_qk", h1, dqk16, tk=1024, tn=512),
        _mm_tn("w_in_dw_v", h1, dv16, tk=1024, tn=768),
        _mm_tn("w_in_dw_z", h1, dz16, tk=1024, tn=768),
        _mm_tn("w_in_dw_xbc", h1, dxbc_pre16, tk=1024, tn=896),
        _mm_tn("w_in_dw_dt", h1, ddtp16, tk=1024, tn=LANES)[:, :N_HEADS],
    ], axis=1)

    dr1, g["ln1_g"], g["ln1_b"] = _ln_bwd("ln1_bwd", r1, w["ln1_g"], w["ln1_b"], dh1)
    dau1 = _mm_swiglu_bwd("ffn1_act_bwd", dr1, w["ffn1_down"].T, au1, scale=0.5)
    g["ffn1_down"] = _mm_tn("ffn1_down_dw", hm1, dr1, scale=0.5, tk=D_FF // 2, tn=512)
    g["ffn1_gate"], g["ffn1_up"] = _mm_tn_gate_up("ffn1_up_dw", x2, dau1)
    dx = _mm("ffn1_dx", [(dau1, wgu1.T)], res=dr1, res_scale=ALPHA)
    return loss, dx.reshape(nb, s, d), g


_HBM = pl.BlockSpec(memory_space=pltpu.HBM)
N_CHIPS = 4
N_DEVICES = 8


def _place():
    return lax.axis_index("x"), lax.axis_index("y"), lax.axis_index("c")


def _other_chips(x, y):
    return [(1 - x, y), (x, 1 - y), (1 - x, 1 - y)]


def _gather_shards(shards, name="gather_weights"):
    n = len(shards)
    halves = [a.shape[0] // 2 for a in shards]
    assert all(h % 16 == 0 for h in halves)

    def body(*refs):
        ins, outs = refs[:n], refs[n:2 * n]
        send_sems, recv_sems, fwd_send_sems, fwd_recv_sems, loc_sems = refs[2 * n:]
        x, y, c = _place()
        me = 2 * x + y
        peers = _other_chips(x, y)

        def piece(t, ref, core):
            return ref.at[pl.ds(pl.multiple_of(core * halves[t], 16), halves[t])]

        def ici(t, p, src_chip, dst):
            px, py = peers[p]
            return pltpu.make_async_remote_copy(
                piece(t, ins[t], c) if src_chip is None else piece(t, outs[t].at[src_chip], c),
                piece(t, outs[t].at[me if src_chip is None else src_chip], c),
                send_sems.at[t, p], recv_sems.at[t, p], device_id=dst, device_id_type=MESH)

        def d2d(t, p, chip, core):
            return pltpu.make_async_remote_copy(
                piece(t, outs[t].at[chip], core), piece(t, outs[t].at[chip], core),
                fwd_send_sems.at[t, p], fwd_recv_sems.at[t, p], device_id=(x, y, 1 - c), device_id_type=MESH)

        locs, sends = [], []
        for t in range(n):
            loc = pltpu.make_async_copy(ins[t], outs[t].at[me], loc_sems.at[t])
            loc.start()
            locs.append(loc)
            for p, (px, py) in enumerate(peers):
                cp = ici(t, p, None, (px, py, c))
                cp.start()
                sends.append(cp)
        for t in range(n):
            for p, (px, py) in enumerate(peers):
                ici(t, p, 2 * px + py, (px, py, c)).wait_recv()
                fwd = d2d(t, p, 2 * px + py, c)
                fwd.start()
                sends.append(fwd)
        for t in range(n):
            for p, (px, py) in enumerate(peers):
                d2d(t, p, 2 * px + py, 1 - c).wait_recv()
        for cp in sends:
            cp.wait_send()
        for loc in locs:
            loc.wait()

    pair = pltpu.SemaphoreType.DMA((n, N_CHIPS - 1))
    return pl.pallas_call(
        body, name=name,
        in_specs=[_HBM] * n, out_specs=[_HBM] * n,
        out_shape=[jax.ShapeDtypeStruct((N_CHIPS,) + a.shape, a.dtype) for a in shards],
        scratch_shapes=[pair, pair, pair, pair, pltpu.SemaphoreType.DMA((n,))],
    )(*shards)


def _exchange_partials(stacks):
    n = len(stacks)

    def body(*refs):
        ins, outs = refs[:n], refs[n:2 * n]
        send_sems, recv_sems, loc_sems = refs[2 * n:]
        x, y, c = _place()
        me = 2 * x + y
        peers = _other_chips(x, y)
        locs, sends = [], []
        for t in range(n):
            loc = pltpu.make_async_copy(ins[t].at[me], outs[t].at[me], loc_sems.at[t])
            loc.start()
            locs.append(loc)
            for p, (px, py) in enumerate(peers):
                cp = pltpu.make_async_remote_copy(ins[t].at[2 * px + py], outs[t].at[me], send_sems.at[t, p],
                                                  recv_sems.at[t, p], device_id=(px, py, c), device_id_type=MESH)
                cp.start()
                sends.append(cp)
        for t in range(n):
            for p, (px, py) in enumerate(peers):
                pltpu.make_async_remote_copy(ins[t].at[me], outs[t].at[2 * px + py], send_sems.at[t, p],
                                             recv_sems.at[t, p], device_id=(px, py, c), device_id_type=MESH).wait_recv()
        for cp in sends:
            cp.wait_send()
        for loc in locs:
            loc.wait()

    return pl.pallas_call(
        body, name="exchange_partials",
        in_specs=[_HBM] * n, out_specs=[_HBM] * n,
        out_shape=[jax.ShapeDtypeStruct(a.shape, a.dtype) for a in stacks],
        scratch_shapes=[pltpu.SemaphoreType.DMA((n, N_CHIPS - 1)), pltpu.SemaphoreType.DMA((n, N_CHIPS - 1)),
                        pltpu.SemaphoreType.DMA((n,))],
    )(*stacks)


def _sibling_halves(stacks):
    n = len(stacks)
    halves = [a.shape[1] // 2 for a in stacks]

    def body(*refs):
        ins, outs = refs[:n], refs[n:2 * n]
        send_sems, recv_sems = refs[2 * n:]
        x, y, c = _place()
        cps = []
        for t in range(n):
            src = ins[t].at[:, pl.ds(pl.multiple_of((1 - c) * halves[t], 16), halves[t])]
            cp = pltpu.make_async_remote_copy(src, outs[t], send_sems.at[t], recv_sems.at[t],
                                              device_id=(x, y, 1 - c), device_id_type=MESH)
            cp.start()
            cps.append(cp)
        for cp in cps:
            cp.wait()

    return pl.pallas_call(
        body, name="sibling_halves",
        in_specs=[_HBM] * n, out_specs=[_HBM] * n,
        out_shape=[jax.ShapeDtypeStruct((a.shape[0], h, a.shape[2]), a.dtype) for a, h in zip(stacks, halves)],
        scratch_shapes=[pltpu.SemaphoreType.DMA((n,)), pltpu.SemaphoreType.DMA((n,))],
    )(*stacks)


def _sibling_fill(halves):
    n = len(halves)
    hs = [a.shape[0] for a in halves]

    def body(*refs):
        ins, outs = refs[:n], refs[n:2 * n]
        send_sems, recv_sems, loc_sems = refs[2 * n:]
        x, y, c = _place()

        def rows(t, core):
            return outs[t].at[pl.ds(pl.multiple_of(core * hs[t], 16), hs[t])]

        locs, cps = [], []
        for t in range(n):
            loc = pltpu.make_async_copy(ins[t], rows(t, c), loc_sems.at[t])
            loc.start()
            locs.append(loc)
            cp = pltpu.make_async_remote_copy(ins[t], rows(t, c), send_sems.at[t], recv_sems.at[t],
                                              device_id=(x, y, 1 - c), device_id_type=MESH)
            cp.start()
            cps.append(cp)
        for t in range(n):
            pltpu.make_async_remote_copy(ins[t], rows(t, 1 - c), send_sems.at[t], recv_sems.at[t],
                                         device_id=(x, y, 1 - c), device_id_type=MESH).wait_recv()
        for cp in cps:
            cp.wait_send()
        for loc in locs:
            loc.wait()

    return pl.pallas_call(
        body, name="sibling_fill",
        in_specs=[_HBM] * n, out_specs=[_HBM] * n,
        out_shape=[jax.ShapeDtypeStruct((2 * a.shape[0], a.shape[1]), a.dtype) for a in halves],
        scratch_shapes=[pltpu.SemaphoreType.DMA((n,)), pltpu.SemaphoreType.DMA((n,)), pltpu.SemaphoreType.DMA((n,))],
    )(*halves)


def _half_sum(name, own, other, core):
    k, r, cols = own.shape
    h = r // 2
    tr = next(cand for cand in (128, 176, 64, 32, 16) if h % cand == 0)
    nblk = h // tr

    def body(core_ref, own_ref, other_ref, o_ref):
        o_ref[...] = _bf(own_ref[...] + other_ref[...].astype(F32))

    grid_spec = pltpu.PrefetchScalarGridSpec(
        num_scalar_prefetch=1, grid=(nblk,),
        in_specs=[pl.BlockSpec((k, tr, cols), lambda i, core_ref: (0, i + core_ref[0] * nblk, 0)),
                  pl.BlockSpec((k, tr, cols), lambda i, core_ref: (0, i, 0))],
        out_specs=pl.BlockSpec((k, tr, cols), lambda i, core_ref: (0, i, 0)))
    return pl.pallas_call(
        body, name=name, grid_spec=grid_spec, out_shape=jax.ShapeDtypeStruct((k, h, cols), BF16),
        compiler_params=_params("parallel"),
    )(core.reshape(1).astype(jnp.int32), own, other)


def _small_allreduce(v):
    r = v.shape[0]

    def body(v_ref, tot_ref, slots, send_sems, recv_sems):
        x, y, c = _place()
        me = 4 * x + 2 * y + c
        slots[me] = v_ref[...]
        cps, peers = [], []
        for k in range(1, N_DEVICES):
            px = 1 - x if (k >> 2) & 1 else x
            py = 1 - y if (k >> 1) & 1 else y
            pc = 1 - c if k & 1 else c
            cp = pltpu.make_async_remote_copy(v_ref, slots.at[me], send_sems.at[k - 1], recv_sems.at[k - 1],
                                              device_id=(px, py, pc), device_id_type=MESH)
            cp.start()
            cps.append(cp)
            peers.append((px, py, pc))
        for k, (px, py, pc) in enumerate(peers):
            pltpu.make_async_remote_copy(v_ref, slots.at[4 * px + 2 * py + pc], send_sems.at[k], recv_sems.at[k],
                                         device_id=(px, py, pc), device_id_type=MESH).wait_recv()
        for cp in cps:
            cp.wait_send()
        acc = slots[0]
        for s in range(1, N_DEVICES):
            acc = acc + slots[s]
        tot_ref[...] = acc

    return pl.pallas_call(
        body, name="small_allreduce",
        in_specs=[pl.BlockSpec(memory_space=pltpu.VMEM)], out_specs=pl.BlockSpec(memory_space=pltpu.VMEM),
        out_shape=jax.ShapeDtypeStruct((r, LANES), F32),
        scratch_shapes=[pltpu.VMEM((N_DEVICES, r, LANES), F32), pltpu.SemaphoreType.DMA((N_DEVICES - 1,)),
                        pltpu.SemaphoreType.DMA((N_DEVICES - 1,))],
    )(v)


def _elementwise(name, fn, ins, out_dtypes):
    r, c = ins[0].shape[-2:]
    tr = next((cand for cand in (256, 176, 128, 64, 32, 16) if r % cand == 0), r)
    nin = len(ins)

    def body(*refs):
        outs = fn(*[ref[...] for ref in refs[:nin]])
        for o_ref, o in zip(refs[nin:], outs):
            o_ref[...] = o.astype(o_ref.dtype)

    in_specs = [pl.BlockSpec((tr, c), lambda i: (i, 0)) if a.ndim == 2 else pl.BlockSpec((a.shape[0], tr, c), lambda i: (0, i, 0))
                for a in ins]
    return pl.pallas_call(
        body, name=name, grid=(r // tr,), in_specs=in_specs,
        out_specs=[pl.BlockSpec((tr, c), lambda i: (i, 0)) for _ in out_dtypes],
        out_shape=[jax.ShapeDtypeStruct((r, c), dt) for dt in out_dtypes],
        compiler_params=_params("parallel"),
    )(*ins)


def _sum_slots(name, stack):
    def fn(v):
        acc = v[0].astype(F32)
        for s in range(1, v.shape[0]):
            acc = acc + v[s].astype(F32)
        return [acc]
    return _elementwise(name, fn, [stack], [F32])[0]


def _adamw(name, grads, w, m, v):
    ng = len(grads)

    def fn(*vals):
        g = vals[0] if ng == 1 else vals[0] + vals[1]
        w_v, m_v, v_v = vals[ng:]
        m2 = ADAM_B1 * m_v + (1.0 - ADAM_B1) * g
        v2 = ADAM_B2 * v_v + (1.0 - ADAM_B2) * jnp.square(g)
        m_hat = m2 / (1.0 - ADAM_B1 ** ADAM_STEP)
        v_hat = v2 / (1.0 - ADAM_B2 ** ADAM_STEP)
        delta = -ADAM_LR * (m_hat / (jnp.sqrt(v_hat) + ADAM_EPS) + ADAM_WD * w_v)
        return [g, delta, m2, v2]

    return _elementwise(name, fn, list(grads) + [w, m, v], [F32] * 4)


_MATRICES = (("ffn1_gate", 1), ("ffn1_up", 1), ("ffn1_down", 0), ("w_in", 1), ("w_out", 0),
             ("ffn2_gate", 1), ("ffn2_up", 1), ("ffn2_down", 0))
_VECTORS = ("ln1_g", "ln1_b", "conv_b", "dt_bias", "a_log", "d_skip", "attn_norm_w", "ssd_norm_w",
            "ln2_g", "ln2_b", "ln3_g", "ln3_b")
_WEIGHT_ORDER = ("ln1_g", "ln1_b", "ffn1_gate", "ffn1_up", "ffn1_down", "w_in", "conv_w", "conv_b", "dt_bias", "a_log",
                 "d_skip", "attn_norm_w", "ssd_norm_w", "w_out", "ln2_g", "ln2_b", "ffn2_gate", "ffn2_up", "ffn2_down",
                 "ln3_g", "ln3_b")


def _pack_rows(vectors):
    parts = []
    for vec in vectors:
        flat = vec.reshape(-1)
        parts.append(jnp.pad(flat, (0, (-flat.shape[0]) % LANES)))
    flat = jnp.concatenate(parts)
    flat = jnp.pad(flat, (0, (-flat.shape[0]) % (8 * LANES)))
    return flat.reshape(-1, LANES)


def _unpack_rows(packed, shapes):
    flat = packed.reshape(-1)
    out, off = [], 0
    for shape in shapes:
        size = int(np.prod(shape))
        out.append(flat[off:off + size].reshape(shape))
        off += size + (-size) % LANES
    return out


def _assemble(stack, axis):
    if axis == 0:
        return stack.reshape(-1, stack.shape[2])
    return jnp.concatenate([stack[s] for s in range(N_CHIPS)], axis=1)


def _split(full, axis):
    if axis == 0:
        return full.reshape(N_CHIPS, -1, full.shape[1])
    cols = full.shape[1] // N_CHIPS
    return jnp.stack([full[:, cols * s:cols * (s + 1)] for s in range(N_CHIPS)])


def kernel(x, positions, ln1_g, ln1_b, ffn1_gate, ffn1_up, ffn1_down, w_in, conv_w, conv_b, dt_bias, a_log, d_skip, attn_norm_w, ssd_norm_w, w_out, ln2_g, ln2_b, ffn2_gate, ffn2_up, ffn2_down, ln3_g, ln3_b, loss_target, m_ln1_g, m_ln1_b, m_ffn1_gate, m_ffn1_up, m_ffn1_down, m_w_in, m_conv_w, m_conv_b, m_dt_bias, m_a_log, m_d_skip, m_attn_norm_w, m_ssd_norm_w, m_w_out, m_ln2_g, m_ln2_b, m_ffn2_gate, m_ffn2_up, m_ffn2_down, m_ln3_g, m_ln3_b, v_ln1_g, v_ln1_b, v_ffn1_gate, v_ffn1_up, v_ffn1_down, v_w_in, v_conv_w, v_conv_b, v_dt_bias, v_a_log, v_d_skip, v_attn_norm_w, v_ssd_norm_w, v_w_out, v_ln2_g, v_ln2_b, v_ffn2_gate, v_ffn2_up, v_ffn2_down, v_ln3_g, v_ln3_b):
    given = dict(locals())
    wts = {n: given[n] for n in _WEIGHT_ORDER}
    mom_m = {n: given["m_" + n] for n in _WEIGHT_ORDER}
    mom_v = {n: given["v_" + n] for n in _WEIGHT_ORDER}
    chip = 2 * lax.axis_index("x") + lax.axis_index("y")

    conv_rows = jnp.pad(wts["conv_w"][0], ((0, 32 - CONV_WIDTH), (0, 0)))
    gathered = _gather_shards([wts[n][0].astype(BF16) for n, _ in _MATRICES] + [conv_rows])
    full = {n: _assemble(st, axis) for (n, axis), st in zip(_MATRICES, gathered)}
    full["conv_w"] = _assemble(gathered[-1][:, :CONV_WIDTH], 1)
    for n in _VECTORS:
        full[n] = wts[n]

    loss, grad_x, g = _local_step(x, positions, loss_target, full)

    core = lax.axis_index("c")
    partials = [_split(g[n], axis) for n, axis in _MATRICES]
    from_sibling = _sibling_halves([p.astype(BF16) for p in partials])
    chip_sums = [_half_sum("core_sum_" + n, p, o, core) for (n, _), p, o in zip(_MATRICES, partials, from_sibling)]
    received = _exchange_partials(chip_sums)
    half_totals = [_sum_slots("sum_partials_" + n, st) for (n, _), st in zip(_MATRICES, received)]
    totals = _sibling_fill(half_totals)

    small_shapes = [g[n].shape for n in _VECTORS] + [g["conv_w"].shape, (1,)]
    total = _small_allreduce(_pack_rows([g[n] for n in _VECTORS] + [g["conv_w"], loss[0, :1]]))
    small = _unpack_rows(total, small_shapes)
    loss_out = small[-1].reshape(())

    grads, deltas, new_m, new_v = {}, {}, {}, {}
    for (n, _), g_tot in zip(_MATRICES, totals):
        res = _adamw("adamw_" + n, [g_tot], wts[n][0], mom_m[n][0], mom_v[n][0])
        grads[n], deltas[n], new_m[n], new_v[n] = [r[None] for r in res]

    vec_shapes = [wts[n].shape for n in _VECTORS]
    res = _adamw("adamw_vectors", [_pack_rows(small[:len(_VECTORS)])], _pack_rows([wts[n] for n in _VECTORS]),
                 _pack_rows([mom_m[n] for n in _VECTORS]), _pack_rows([mom_v[n] for n in _VECTORS]))
    for dst, packed in zip((grads, deltas, new_m, new_v), res):
        for n, val in zip(_VECTORS, _unpack_rows(packed, vec_shapes)):
            dst[n] = val

    cols = conv_w.shape[2]
    g_conv = lax.dynamic_slice_in_dim(small[len(_VECTORS)], chip * cols, cols, axis=1)
    res = _adamw("adamw_conv_w", [g_conv], wts["conv_w"][0], mom_m["conv_w"][0], mom_v["conv_w"][0])
    grads["conv_w"], deltas["conv_w"], new_m["conv_w"], new_v["conv_w"] = [r[None] for r in res]

    return (loss_out, grad_x, *[grads[n] for n in _WEIGHT_ORDER], *[deltas[n] for n in _WEIGHT_ORDER],
            *[new_m[n] for n in _WEIGHT_ORDER], *[new_v[n] for n in _WEIGHT_ORDER])
```

```python
import functools

import numpy as np
import jax
import jax.numpy as jnp
from jax import lax
from jax.experimental import pallas as pl
from jax.experimental.pallas import tpu as pltpu

F32, BF16 = jnp.float32, jnp.bfloat16

D_MODEL = 1024
D_FF = 2816
N_HEADS = 12
HEAD_DIM = 64
D_ATTN = 768
D_SSD = 768
N_GROUPS = 4
HEADS_PER_GROUP = 3
D_STATE = 128
D_CONV = 1792
CONV_WIDTH = 4
ROPE_DIM = 16
ROPE_THETA = 500000.0
ALPHA = 2.0 ** 0.25
LN_EPS = 1e-5
RMS_EPS = 1e-6
ADAM_LR, ADAM_B1, ADAM_B2, ADAM_EPS, ADAM_WD, ADAM_STEP = 0.001, 0.9, 0.999, 1e-08, 0.01, 10

LANES = 128
GATE_UP_INTERLEAVE = 256
SEQ_BLOCK = 256
GROUP_LANES = 256
VMEM_LIMIT = 56 * 1024 * 1024
NEG = -1e30
MESH = pl.DeviceIdType.MESH
HIGHEST = lax.Precision.HIGHEST

_NT = (((1,), (1,)), ((), ()))
_TN = (((0,), (0,)), ((), ()))


def _params(*sem):
    return pltpu.CompilerParams(dimension_semantics=sem, vmem_limit_bytes=VMEM_LIMIT)


def _bf(v):
    return v.astype(BF16)


def _mm(name, pairs, *, scale=1.0, res=None, res_scale=1.0, out_dtype=F32, tm=512, tn=512):
    m, n = pairs[0][0].shape[0], pairs[0][1].shape[1]
    tm, tn = min(tm, m), min(tn, n)
    assert m % tm == 0 and n % tn == 0, (name, m, n, tm, tn)
    npair = len(pairs)

    def body(*refs):
        acc = None
        for a_ref, b_ref in zip(refs[:npair], refs[npair:2 * npair]):
            d = jnp.dot(_bf(a_ref[...]), b_ref[...], preferred_element_type=F32)
            acc = d if acc is None else acc + d
        if scale != 1.0:
            acc = acc * scale
        if res is not None:
            acc = acc + res_scale * refs[2 * npair][...]
        refs[-1][...] = acc.astype(out_dtype)

    in_specs = [pl.BlockSpec((tm, a.shape[1]), lambda i, j: (i, 0)) for a, _ in pairs]
    in_specs += [pl.BlockSpec((b.shape[0], tn), lambda i, j: (0, j)) for _, b in pairs]
    args = [a for a, _ in pairs] + [b for _, b in pairs]
    if res is not None:
        in_specs.append(pl.BlockSpec((tm, tn), lambda i, j: (i, j)))
        args.append(res)
    return pl.pallas_call(
        body, name=name, grid=(m // tm, n // tn), in_specs=in_specs,
        out_specs=pl.BlockSpec((tm, tn), lambda i, j: (i, j)),
        out_shape=jax.ShapeDtypeStruct((m, n), out_dtype),
        compiler_params=_params("parallel", "parallel"),
    )(*args)


def _mm_tn(name, x, dy, *, scale=1.0, tk=512, tn=512, tt=1024):
    t, k = x.shape
    n = dy.shape[1]
    tk, tn, tt = min(tk, k), min(tn, n), min(tt, t)
    assert k % tk == 0 and n % tn == 0 and t % tt == 0, (name, k, n, t)
    nt = t // tt

    def body(x_ref, dy_ref, o_ref):
        step = pl.program_id(2)
        d = lax.dot_general(_bf(x_ref[...]), _bf(dy_ref[...]), _TN, preferred_element_type=F32)

        @pl.when(step == 0)
        def _():
            o_ref[...] = d

        @pl.when(step > 0)
        def _():
            o_ref[...] += d

        if scale != 1.0:
            @pl.when(step == nt - 1)
            def _():
                o_ref[...] = o_ref[...] * scale

    return pl.pallas_call(
        body, name=name, grid=(k // tk, n // tn, nt),
        in_specs=[pl.BlockSpec((tt, tk), lambda i, j, s: (s, i)), pl.BlockSpec((tt, tn), lambda i, j, s: (s, j))],
        out_specs=pl.BlockSpec((tk, tn), lambda i, j, s: (i, j)),
        out_shape=jax.ShapeDtypeStruct((k, n), F32),
        compiler_params=_params("parallel", "parallel", "arbitrary"),
    )(x, dy)


def _mm_tn_gate_up(name, x, dau, *, tt=1024):
    t, k = x.shape
    gi = GATE_UP_INTERLEAVE
    nj = dau.shape[1] // (2 * gi)
    tt = min(tt, t)
    nt = t // tt

    def body(x_ref, dy_ref, g_ref, u_ref):
        step = pl.program_id(1)
        d = lax.dot_general(_bf(x_ref[...]), dy_ref[...], _TN, preferred_element_type=F32)

        @pl.when(step == 0)
        def _():
            g_ref[...] = d[:, :gi]
            u_ref[...] = d[:, gi:]

        @pl.when(step > 0)
        def _():
            g_ref[...] += d[:, :gi]
            u_ref[...] += d[:, gi:]

    out = pl.BlockSpec((k, gi), lambda j, s: (0, j))
    return pl.pallas_call(
        body, name=name, grid=(nj, nt),
        in_specs=[pl.BlockSpec((tt, k), lambda j, s: (s, 0)), pl.BlockSpec((tt, 2 * gi), lambda j, s: (s, j))],
        out_specs=[out, out],
        out_shape=[jax.ShapeDtypeStruct((k, gi * nj), F32)] * 2,
        compiler_params=_params("parallel", "arbitrary"),
    )(x, dau)


def _mm_swiglu(name, x, wgu, *, tm=512, carry=None):
    t, k = x.shape
    gi = GATE_UP_INTERLEAVE
    ni, nj = t // tm, wgu.shape[1] // (2 * gi)
    nc = carry.n if carry is not None else 0

    def body(*refs):
        x_ref, w_ref = refs[:2]
        au_ref, hm_ref = refs[2 + nc:4 + nc]
        if carry is not None:
            step = pl.program_id(0) * nj + pl.program_id(1)
            start, forward, finish = carry.phases(refs[2:2 + nc], refs[4 + nc:4 + 2 * nc], refs[4 + 2 * nc:])
            pl.when(step == 0)(start)
            pl.when(step == (3 * ni * nj) // 4)(forward)
        au = jnp.dot(_bf(x_ref[...]), w_ref[...], preferred_element_type=F32)
        a, u = au[:, :gi], au[:, gi:]
        au_ref[...] = _bf(au)
        hm_ref[...] = _bf(a * jax.nn.sigmoid(a) * u)
        if carry is not None:
            pl.when(step == ni * nj - 1)(finish)

    hbm = pl.BlockSpec(memory_space=pltpu.HBM)
    res = pl.pallas_call(
        body, name=name, grid=(ni, nj),
        in_specs=[pl.BlockSpec((tm, k), lambda i, j: (i, 0)), pl.BlockSpec((k, 2 * gi), lambda i, j: (0, j))] + [hbm] * nc,
        out_specs=[pl.BlockSpec((tm, 2 * gi), lambda i, j: (i, j)), pl.BlockSpec((tm, gi), lambda i, j: (i, j))] + [hbm] * nc,
        out_shape=[jax.ShapeDtypeStruct((t, 2 * gi * nj), BF16), jax.ShapeDtypeStruct((t, gi * nj), BF16)]
        + (carry.out_shape if carry is not None else []),
        scratch_shapes=carry.scratch_shapes if carry is not None else [],
        compiler_params=_params(*(("arbitrary", "arbitrary") if carry is not None else ("parallel", "parallel"))),
    )(x, wgu, *(carry.operands if carry is not None else []))
    return res if carry is None else (res[0], res[1], carry.results(res[2:]))


def _mm_swiglu_bwd(name, dr, wdt, au, *, scale, tm=512):
    t, k = dr.shape
    gi = GATE_UP_INTERLEAVE
    nj = wdt.shape[1] // gi

    def body(dr_ref, w_ref, au_ref, o_ref):
        dhm = jnp.dot(_bf(dr_ref[...]), w_ref[...], preferred_element_type=F32) * scale
        au_v = au_ref[...].astype(F32)
        a, u = au_v[:, :gi], au_v[:, gi:]
        sig = jax.nn.sigmoid(a)
        da = dhm * u * (sig * (1.0 + a * (1.0 - sig)))
        du = dhm * (a * sig)
        o_ref[:, :gi] = _bf(da)
        o_ref[:, gi:] = _bf(du)

    return pl.pallas_call(
        body, name=name, grid=(t // tm, nj),
        in_specs=[pl.BlockSpec((tm, k), lambda i, j: (i, 0)), pl.BlockSpec((k, gi), lambda i, j: (0, j)),
                  pl.BlockSpec((tm, 2 * gi), lambda i, j: (i, j))],
        out_specs=pl.BlockSpec((tm, 2 * gi), lambda i, j: (i, j)),
        out_shape=jax.ShapeDtypeStruct((t, 2 * gi * nj), BF16),
        compiler_params=_params("parallel", "parallel"),
    )(dr, wdt, au)


def _layer_norm(r, g, b):
    mu = jnp.mean(r, axis=-1, keepdims=True)
    var = jnp.mean(jnp.square(r - mu), axis=-1, keepdims=True)
    return (r - mu) * lax.rsqrt(var + LN_EPS) * g + b


def _mm_res_ln(name, a, w, res, g, b, *, scale, tm=256):
    t, k = a.shape
    n = w.shape[1]

    def body(a_ref, w_ref, res_ref, g_ref, b_ref, y_ref, r_ref):
        r = ALPHA * res_ref[...] + scale * jnp.dot(_bf(a_ref[...]), w_ref[...], preferred_element_type=F32)
        r_ref[...] = r
        y_ref[...] = _layer_norm(r, g_ref[...], b_ref[...])

    row = lambda c: pl.BlockSpec((tm, c), lambda i: (i, 0))
    const = lambda shape: pl.BlockSpec(shape, lambda i: (0, 0))
    return pl.pallas_call(
        body, name=name, grid=(t // tm,),
        in_specs=[row(k), const((k, n)), row(n), const((1, n)), const((1, n))],
        out_specs=[row(n), row(n)],
        out_shape=[jax.ShapeDtypeStruct((t, n), F32), jax.ShapeDtypeStruct((t, n), F32)],
        compiler_params=_params("parallel"),
    )(a, w, res, g, b)


def _rowwise(name, fn, rows, consts, row_outs, acc_outs=(), tm=256):
    rows = [r if isinstance(r, tuple) else (r, r.shape[1]) for r in rows]
    t = rows[0][0].shape[0]
    tm = min(tm, t)
    assert t % tm == 0
    nr, nc, no, na = len(rows), len(consts), len(row_outs), len(acc_outs)

    def body(*refs):
        vals = [r[...] for r in refs[:nr + nc]]
        outs, accs = fn(*vals)
        for o_ref, o in zip(refs[nr + nc:nr + nc + no], outs):
            o_ref[...] = o.astype(o_ref.dtype)
        if na:
            step = pl.program_id(0)
            acc_refs = refs[nr + nc + no:]

            @pl.when(step == 0)
            def _():
                for a_ref, a in zip(acc_refs, accs):
                    a_ref[...] = a

            @pl.when(step > 0)
            def _():
                for a_ref, a in zip(acc_refs, accs):
                    a_ref[...] += a

    in_specs = [pl.BlockSpec((tm, w), lambda i: (i, 0)) for _, w in rows]
    in_specs += [pl.BlockSpec(c.shape, lambda i, nd=c.ndim: (0,) * nd) for c in consts]
    out_specs = [pl.BlockSpec((tm, c), lambda i: (i, 0)) for c, _ in row_outs]
    out_specs += [pl.BlockSpec(s, lambda i: (0, 0)) for s in acc_outs]
    out_shape = [jax.ShapeDtypeStruct((t, c), dt) for c, dt in row_outs]
    out_shape += [jax.ShapeDtypeStruct(s, F32) for s in acc_outs]
    res = pl.pallas_call(
        body, name=name, grid=(t // tm,), in_specs=in_specs, out_specs=out_specs, out_shape=out_shape,
        compiler_params=_params("arbitrary" if na else "parallel"),
    )(*[r for r, _ in rows], *consts)
    return res


def _ln_bwd(name, r, g, b, dy):
    def fn(r_v, dy_v, g_v, b_v):
        _, vjp = jax.vjp(_layer_norm, r_v, g_v, b_v)
        dr, dg, db = vjp(dy_v)
        return [dr], [dg, db]
    return _rowwise(name, fn, [r, dy], [g, b], [(r.shape[1], F32)], [(1, r.shape[1])] * 2)


def _ln_loss_bwd(name, r, g, b, target):
    def fn(r_v, t_v, g_v, b_v):
        def loss_fn(rr, gg, bb):
            err = jnp.square(_layer_norm(rr, gg, bb) - t_v)
            return 0.5 * jnp.sum(jnp.mean(err, axis=-1, keepdims=True), axis=0, keepdims=True)
        loss, vjp = jax.vjp(loss_fn, r_v, g_v, b_v)
        dr, dg, db = vjp(jnp.ones((1, 1), F32))
        return [dr], [dg, db, jnp.broadcast_to(loss, (1, LANES))]
    return _rowwise(name, fn, [r, target], [g, b], [(r.shape[1], F32)], [(1, r.shape[1])] * 2 + [(1, LANES)])


def _rope_tables(posf, invf, sgn):
    ang = posf * invf
    return jnp.cos(ang), jnp.sin(ang) * sgn


def _rope_apply(tv, cos, sin):
    lane = lax.broadcasted_iota(jnp.int32, cos.shape, 1)
    first = (lane % HEAD_DIM) < (ROPE_DIM // 2)
    outs = []
    for gidx in range(tv.shape[1] // LANES):
        tg = tv[:, LANES * gidx:LANES * (gidx + 1)]
        sw = jnp.where(first, pltpu.roll(tg, LANES - ROPE_DIM // 2, 1), pltpu.roll(tg, ROPE_DIM // 2, 1))
        outs.append(tg * cos + sw * sin)
    return jnp.concatenate(outs, axis=1)


def _rope_fwd(qk, posf, invf, sgn):
    def fn(qk_v, pos_v, invf_v, sgn_v):
        cos, sin = _rope_tables(pos_v, invf_v, sgn_v)
        q = _rope_apply(qk_v[:, :D_ATTN], cos, sin) * (HEAD_DIM ** -0.5)
        k = _rope_apply(qk_v[:, D_ATTN:], cos, sin)
        return [q, k, jnp.concatenate([cos, sin], axis=1)], []
    return _rowwise("rope_fwd", fn, [qk, posf], [invf, sgn], [(D_ATTN, BF16), (D_ATTN, BF16), (2 * LANES, F32)])


def _rope_bwd(dq, dk, cs):
    def fn(dq_v, dk_v, cs_v):
        cos, sin = cs_v[:, :LANES], -cs_v[:, LANES:]
        gq = _rope_apply(dq_v * (HEAD_DIM ** -0.5), cos, sin)
        gk = _rope_apply(dk_v, cos, sin)
        return [jnp.concatenate([gq, gk], axis=1)], []
    return _rowwise("rope_bwd", fn, [dq, dk, cs], [], [(2 * D_ATTN, BF16)])[0]


def _rms(v, w):
    return v * lax.rsqrt(jnp.mean(v * v, axis=-1, keepdims=True) + RMS_EPS) * w


def _ungroup(yg):
    w = HEADS_PER_GROUP * HEAD_DIM
    return jnp.concatenate([yg[:, GROUP_LANES * g:GROUP_LANES * g + w] for g in range(N_GROUPS)], axis=1)


def _group(xs):
    w = HEADS_PER_GROUP * HEAD_DIM
    parts = []
    for g in range(N_GROUPS):
        parts += [xs[:, w * g:w * (g + 1)], jnp.zeros((xs.shape[0], GROUP_LANES - w), xs.dtype)]
    return jnp.concatenate(parts, axis=1)


def _norms_fn(attn, yg, xs, z, w_attn, w_ssd, dskip):
    a_n = _rms(attn, w_attn)
    y = _ungroup(yg) + dskip * xs
    y_n = _rms(y * (z * jax.nn.sigmoid(z)), w_ssd)
    return jnp.concatenate([a_n, y_n], axis=1)


def _norms_fwd(attn, yg, xbc, z, w_attn, w_ssd, dskip):
    def fn(*v):
        return [_norms_fn(*v)], []
    return _rowwise("norms_fwd", fn, [attn, yg, (xbc, D_SSD), z], [w_attn, w_ssd, dskip], [(D_ATTN + D_SSD, BF16)])[0]


def _norms_bwd(attn, yg, xbc, z, w_attn, w_ssd, dskip, dcat):
    def fn(attn_v, yg_v, xs_v, z_v, dcat_v, wa_v, ws_v, dk_v):
        _, vjp = jax.vjp(_norms_fn, attn_v, yg_v, xs_v, z_v, wa_v, ws_v, dk_v)
        d_attn, d_yg, d_xs, d_z, d_wa, d_ws, d_dk = vjp(dcat_v)
        return [d_attn, d_yg, d_xs, d_z], [d_wa, d_ws, d_dk]
    return _rowwise("norms_bwd", fn, [attn, yg, (xbc, D_SSD), z, dcat], [w_attn, w_ssd, dskip],
                    [(D_ATTN, F32), (N_GROUPS * GROUP_LANES, F32), (D_SSD, F32), (D_SSD, BF16)], [(1, D_SSD)] * 3)


def _ssd_prep_fn(xs, dtp, dtb, alog, e_x, e_a):
    dt = jax.nn.softplus(dtp + dtb)
    a = -jnp.exp(alog)
    dtg = jnp.dot(dt, e_x, precision=HIGHEST, preferred_element_type=F32)
    xdtg = _group(xs) * dtg
    dag = jnp.dot(dt * a, e_a, precision=HIGHEST, preferred_element_type=F32)
    return xdtg, dag


def _ssd_prep_fwd(xbc, dtp, dtb, alog, e_x, e_a):
    def fn(xbc_v, dtp_v, dtb_v, alog_v, ex_v, ea_v):
        xdtg, dag = _ssd_prep_fn(xbc_v[:, :D_SSD], dtp_v, dtb_v, alog_v, ex_v, ea_v)
        return [xdtg, xbc_v[:, D_SSD:], dag], []
    return _rowwise("ssd_prep_fwd", fn, [xbc, dtp], [dtb, alog, e_x, e_a],
                    [(N_GROUPS * GROUP_LANES, BF16), (D_CONV - D_SSD, BF16), (N_GROUPS * LANES, F32)])


def _ssd_prep_bwd(xbc, dtp, dtb, alog, e_x, e_a, dxdtg, ddag, dxs_a, db, dc):
    def fn(xs_v, dtp_v, dxdtg_v, ddag_v, dxs_a_v, db_v, dc_v, dtb_v, alog_v, ex_v, ea_v):
        _, vjp = jax.vjp(lambda a, b, c, d: _ssd_prep_fn(a, b, c, d, ex_v, ea_v), xs_v, dtp_v, dtb_v, alog_v)
        dxs, ddtp, ddtb, dalog = vjp((dxdtg_v, ddag_v))
        return [jnp.concatenate([dxs + dxs_a_v, db_v, dc_v], axis=1), ddtp], [ddtb, dalog]
    return _rowwise("ssd_prep_bwd", fn, [(xbc, D_SSD), dtp, dxdtg, ddag, dxs_a, db, dc], [dtb, alog, e_x, e_a],
                    [(D_CONV, F32), (LANES, BF16)], [(1, LANES)] * 2)


def _shift_down(u, d):
    if d == 0:
        return u
    row = lax.broadcasted_iota(jnp.int32, u.shape, 0)
    return jnp.where(row >= d, pltpu.roll(u, d, 0), 0.0)


def _shift_up(u, d):
    if d == 0:
        return u
    s = u.shape[0]
    row = lax.broadcasted_iota(jnp.int32, u.shape, 0)
    return jnp.where(row < s - d, pltpu.roll(u, s - d, 0), 0.0)


def _conv_pre(u, w, b):
    acc = b
    for k in range(CONV_WIDTH):
        acc = acc + w[k:k + 1, :] * _shift_down(u, CONV_WIDTH - 1 - k)
    return acc


def _conv_fwd(u, w, b, *, tc=256):
    nb, s, c = u.shape

    def body(u_ref, w_ref, b_ref, o_ref):
        pre = _conv_pre(u_ref[0], w_ref[...], b_ref[...])
        o_ref[0] = pre * jax.nn.sigmoid(pre)

    return pl.pallas_call(
        body, name="conv_fwd", grid=(c // tc, nb),
        in_specs=[pl.BlockSpec((1, s, tc), lambda j, i: (i, 0, j)), pl.BlockSpec((CONV_WIDTH, tc), lambda j, i: (0, j)),
                  pl.BlockSpec((1, tc), lambda j, i: (0, j))],
        out_specs=pl.BlockSpec((1, s, tc), lambda j, i: (i, 0, j)),
        out_shape=jax.ShapeDtypeStruct((nb, s, c), F32),
        compiler_params=_params("parallel", "parallel"),
    )(u, w, b)


def _conv_bwd(u, w, b, dout, *, tc=256):
    nb, s, c = u.shape

    def body(u_ref, w_ref, b_ref, d_ref, du_ref, dw_ref, db_ref):
        uv, wv = u_ref[0], w_ref[...]
        pre = _conv_pre(uv, wv, b_ref[...])
        sig = jax.nn.sigmoid(pre)
        dpre = d_ref[0] * (sig * (1.0 + pre * (1.0 - sig)))
        du = jnp.zeros_like(uv)
        dws = []
        for k in range(CONV_WIDTH):
            du = du + wv[k:k + 1, :] * _shift_up(dpre, CONV_WIDTH - 1 - k)
            dws.append(jnp.sum(dpre * _shift_down(uv, CONV_WIDTH - 1 - k), axis=0, keepdims=True))
        du_ref[0] = _bf(du)
        dwv = jnp.concatenate(dws + [jnp.zeros((8 - CONV_WIDTH, tc), F32)], axis=0)
        dbv = jnp.sum(dpre, axis=0, keepdims=True)
        first = pl.program_id(1) == 0

        @pl.when(first)
        def _():
            dw_ref[...] = dwv
            db_ref[...] = dbv

        @pl.when(jnp.logical_not(first))
        def _():
            dw_ref[...] += dwv
            db_ref[...] += dbv

    blk = pl.BlockSpec((1, s, tc), lambda j, i: (i, 0, j))
    return pl.pallas_call(
        body, name="conv_bwd", grid=(c // tc, nb),
        in_specs=[blk, pl.BlockSpec((CONV_WIDTH, tc), lambda j, i: (0, j)), pl.BlockSpec((1, tc), lambda j, i: (0, j)), blk],
        out_specs=[blk, pl.BlockSpec((8, tc), lambda j, i: (0, j)), pl.BlockSpec((1, tc), lambda j, i: (0, j))],
        out_shape=[jax.ShapeDtypeStruct((nb, s, c), BF16), jax.ShapeDtypeStruct((8, c), F32), jax.ShapeDtypeStruct((1, c), F32)],
        compiler_params=_params("parallel", "arbitrary"),
    )(u, w, b, dout)


FWD_KEY_BLOCK = 256


def _branch_bias_table(seq, kb):
    ratio = SEQ_BLOCK // kb
    key = np.arange(kb)[None, :, None]
    query = np.arange(SEQ_BLOCK)[None, None, :]
    delta = (np.arange(seq // kb)[:, None, None] - (ratio - 1)) * kb + query - key
    cnt = np.zeros(delta.shape, np.float64)
    for window, dilation in ((128, 1), (512, 4), (2048, 16)):
        cnt += (delta >= 0) & (delta % dilation == 0) & (delta <= window)
    return jnp.asarray(np.where(cnt > 0, np.log(np.maximum(cnt, 1.0)), NEG).astype(np.float32))


HEADS_PER_BLOCK = LANES // HEAD_DIM


def _head_rows(v, h):
    row = lax.broadcasted_iota(jnp.int32, v.shape, 0)
    return jnp.where((row >= HEAD_DIM * h) & (row < HEAD_DIM * (h + 1)), v, jnp.zeros_like(v))


def _attn_fwd(q, k, v, bias):
    nb_, s, _ = q.shape
    ab, kb = SEQ_BLOCK, FWD_KEY_BLOCK
    nblk, nkb, ratio = s // ab, s // kb, ab // kb

    def body(q_ref, k_ref, v_ref, b_ref, o_ref, lse_ref, vt_s):
        i = pl.program_id(2)

        @pl.when(i == 0)
        def _():
            for jb in range(nkb):
                vt_s[jb] = v_ref[0, kb * jb:kb * (jb + 1), :].T

        qt = q_ref[0].T
        qts = [_head_rows(qt, h) for h in range(HEADS_PER_BLOCK)]

        def step(j, carry):
            ks = pl.ds(pl.multiple_of(j * kb, kb), kb)
            kj = k_ref[0, ks, :]
            lb = b_ref[ratio * i - j + (ratio - 1)]
            out = []
            for h in range(HEADS_PER_BLOCK):
                m, l, acc = carry[3 * h:3 * h + 3]
                st = jnp.dot(kj, qts[h], preferred_element_type=F32) + lb
                m_new = jnp.maximum(m, jnp.max(st, axis=0, keepdims=True))
                p = jnp.exp(st - m_new)
                a = jnp.exp(m - m_new)
                l = a * l + jnp.sum(p, axis=0, keepdims=True)
                vt = vt_s[j, HEAD_DIM * h:HEAD_DIM * (h + 1), :]
                acc = a * acc + jnp.dot(vt, _bf(p), preferred_element_type=F32)
                out += [m_new, l, acc]
            return tuple(out)

        init = (jnp.full((1, ab), NEG, F32), jnp.zeros((1, ab), F32), jnp.zeros((HEAD_DIM, ab), F32)) * HEADS_PER_BLOCK
        res = lax.fori_loop(0, ratio * (i + 1), step, init)
        ot = jnp.concatenate([res[3 * h + 2] / res[3 * h + 1] for h in range(HEADS_PER_BLOCK)], axis=0)
        o_ref[0] = ot.T
        rows = [res[3 * h] + jnp.log(res[3 * h + 1]) for h in range(HEADS_PER_BLOCK)]
        lse_ref[0, 0, 0] = jnp.concatenate(rows + [jnp.zeros((8 - HEADS_PER_BLOCK, ab), F32)], axis=0)

    qblk = pl.BlockSpec((1, ab, LANES), lambda b, hp, i: (b, i, hp))
    full = pl.BlockSpec((1, s, LANES), lambda b, hp, i: (b, 0, hp))
    return pl.pallas_call(
        body, name="attn_fwd", grid=(nb_, D_ATTN // LANES, nblk),
        in_specs=[qblk, full, full, pl.BlockSpec((nkb, kb, ab), lambda b, hp, i: (0, 0, 0))],
        out_specs=[qblk, pl.BlockSpec((1, 1, 1, 8, ab), lambda b, hp, i: (b, hp, i, 0, 0))],
        out_shape=[jax.ShapeDtypeStruct((nb_, s, D_ATTN), F32),
                   jax.ShapeDtypeStruct((nb_, D_ATTN // LANES, nblk, 8, ab), F32)],
        scratch_shapes=[pltpu.VMEM((nkb, LANES, kb), BF16)],
        compiler_params=_params("parallel", "parallel", "arbitrary"),
    )(q, k, v, bias)


def _attn_bwd(q, k, v, o, do, lse, bias):
    nb_, s, _ = q.shape
    ab = SEQ_BLOCK
    nblk = s // ab

    nh = HEADS_PER_BLOCK

    def body(q_ref, k_ref, v_ref, o_ref, do_ref, lse_ref, b_ref, dq_ref, dk_ref, dv_ref,
             qt_s, dot_s, kt_s, dqt_s, do16_s, d_s, dk_acc, dv_acc):
        for jb in range(nblk):
            sl = slice(ab * jb, ab * (jb + 1))
            qt, kt = q_ref[0, sl, :].T, k_ref[0, sl, :].T
            do = do_ref[0, sl, :]
            dot = do.T
            prod = dot * o_ref[0, sl, :].T
            do16_s[sl, :] = _bf(do)
            for h in range(nh):
                qt_s[nh * jb + h] = _head_rows(qt, h)
                kt_s[nh * jb + h] = _head_rows(kt, h)
                dot_s[nh * jb + h] = _head_rows(_bf(dot), h)
            d_s[jb] = jnp.concatenate(
                [jnp.sum(prod[HEAD_DIM * h:HEAD_DIM * (h + 1)], axis=0, keepdims=True) for h in range(nh)]
                + [jnp.zeros((8 - nh, ab), F32)], axis=0)
            dqt_s[jb] = jnp.zeros((LANES, ab), F32)

        def outer(j, carry):
            ks = pl.ds(pl.multiple_of(j * ab, ab), ab)
            kj, vj = k_ref[0, ks, :], v_ref[0, ks, :]
            dk_acc[...] = jnp.zeros_like(dk_acc)
            dv_acc[...] = jnp.zeros_like(dv_acc)

            def inner(i, c2):
                qs = pl.ds(pl.multiple_of(i * ab, ab), ab)
                qi, doi = q_ref[0, qs, :], do16_s[qs, :]
                lb = b_ref[i - j]
                for h in range(nh):
                    st = jnp.dot(kj, qt_s[nh * i + h], preferred_element_type=F32) + lb
                    pt = jnp.exp(st - lse_ref[0, 0, i, h:h + 1, :])
                    dpt = jnp.dot(vj, dot_s[nh * i + h], preferred_element_type=F32)
                    dst16 = _bf(pt * (dpt - d_s[i, h:h + 1, :]))
                    dv_acc[h] += jnp.dot(_bf(pt), doi, preferred_element_type=F32)
                    dk_acc[h] += jnp.dot(dst16, qi, preferred_element_type=F32)
                    dqt_s[i] += jnp.dot(kt_s[nh * j + h], dst16, preferred_element_type=F32)
                return c2

            lax.fori_loop(j, nblk, inner, 0)
            lane = lax.broadcasted_iota(jnp.int32, (ab, LANES), 1)
            dk_ref[0, ks, :] = jnp.where(lane < HEAD_DIM, dk_acc[0], dk_acc[1])
            dv_ref[0, ks, :] = _bf(jnp.where(lane < HEAD_DIM, dv_acc[0], dv_acc[1]))
            return carry

        lax.fori_loop(0, nblk, outer, 0)
        for jb in range(nblk):
            dq_ref[0, ab * jb:ab * (jb + 1), :] = dqt_s[jb].T

    assert nh == 2
    full = pl.BlockSpec((1, s, LANES), lambda b, hp: (b, 0, hp))
    return pl.pallas_call(
        body, name="attn_bwd", grid=(nb_, D_ATTN // LANES),
        in_specs=[full] * 5 + [pl.BlockSpec((1, 1, nblk, 8, ab), lambda b, hp: (b, hp, 0, 0, 0)),
                               pl.BlockSpec((nblk, ab, ab), lambda b, hp: (0, 0, 0))],
        out_specs=[full, full, full],
        out_shape=[jax.ShapeDtypeStruct((nb_, s, D_ATTN), F32), jax.ShapeDtypeStruct((nb_, s, D_ATTN), F32),
                   jax.ShapeDtypeStruct((nb_, s, D_ATTN), BF16)],
        scratch_shapes=[pltpu.VMEM((nh * nblk, LANES, ab), BF16), pltpu.VMEM((nh * nblk, LANES, ab), BF16),
                        pltpu.VMEM((nh * nblk, LANES, ab), BF16), pltpu.VMEM((nblk, LANES, ab), F32),
                        pltpu.VMEM((s, LANES), BF16), pltpu.VMEM((nblk, 8, ab), F32),
                        pltpu.VMEM((nh, ab, LANES), F32), pltpu.VMEM((nh, ab, LANES), F32)],
        compiler_params=_params("parallel", "parallel"),
    )(q, k, v, o, do, lse, bias)


def _cumsum_fwd(dag):
    nb_, s, c = dag.shape
    ab = SEQ_BLOCK

    def body(a_ref, o_ref, ot_ref):
        r = lax.broadcasted_iota(jnp.int32, (ab, ab), 0)
        cc = lax.broadcasted_iota(jnp.int32, (ab, ab), 1)
        tri = (r >= cc).astype(F32)
        carry = jnp.zeros((1, c), F32)
        for i in range(s // ab):
            loc = jnp.dot(tri, a_ref[0, ab * i:ab * (i + 1), :], precision=HIGHEST, preferred_element_type=F32) + carry
            o_ref[0, ab * i:ab * (i + 1), :] = loc
            ot_ref[0, :, ab * i:ab * (i + 1)] = loc.T
            carry = loc[ab - 1:ab, :]

    return pl.pallas_call(
        body, name="ssd_cumsum", grid=(nb_,),
        in_specs=[pl.BlockSpec((1, s, c), lambda b: (b, 0, 0))],
        out_specs=[pl.BlockSpec((1, s, c), lambda b: (b, 0, 0)), pl.BlockSpec((1, c, s), lambda b: (b, 0, 0))],
        out_shape=[jax.ShapeDtypeStruct((nb_, s, c), F32), jax.ShapeDtypeStruct((nb_, c, s), F32)],
        compiler_params=_params("parallel"),
    )(dag)


def _cumsum_bwd(dcol, drow):
    nb_, s, c = dcol.shape
    ab = SEQ_BLOCK

    def body(c_ref, r_ref, o_ref):
        r = lax.broadcasted_iota(jnp.int32, (ab, ab), 0)
        cc = lax.broadcasted_iota(jnp.int32, (ab, ab), 1)
        tri = (r <= cc).astype(F32)
        carry = jnp.zeros((1, c), F32)
        for i in reversed(range(s // ab)):
            rows = r_ref[0, :, ab * i:ab * (i + 1)].T
            parts = []
            for g in range(N_GROUPS):
                parts += [rows[:, 8 * g:8 * (g + 1)], jnp.zeros((ab, LANES - 8), F32)]
            blk = c_ref[0, ab * i:ab * (i + 1), :] + jnp.concatenate(parts, axis=1)
            loc = jnp.dot(tri, blk, precision=HIGHEST, preferred_element_type=F32) + carry
            o_ref[0, ab * i:ab * (i + 1), :] = loc
            carry = loc[0:1, :]

    return pl.pallas_call(
        body, name="ssd_cumsum_bwd", grid=(nb_,),
        in_specs=[pl.BlockSpec((1, s, c), lambda b: (b, 0, 0)), pl.BlockSpec((1, N_GROUPS * 8, s), lambda b: (b, 0, 0))],
        out_specs=pl.BlockSpec((1, s, c), lambda b: (b, 0, 0)),
        out_shape=jax.ShapeDtypeStruct((nb_, s, c), F32),
        compiler_params=_params("parallel"),
    )(dcol, drow)


def _causal_ok(i, j):
    ab = SEQ_BLOCK
    r = lax.broadcasted_iota(jnp.int32, (ab, ab), 0)
    c = lax.broadcasted_iota(jnp.int32, (ab, ab), 1)
    return (r + (i - j) * ab) >= c


def _ssd_fwd(xdtg, bc, acum, acum_t):
    nb_, s, _ = xdtg.shape
    ab = SEQ_BLOCK

    def body(x_ref, b_ref, c_ref, ac_ref, at_ref, y_ref):
        i = pl.program_id(2)
        ci = c_ref[0]
        acol = [ac_ref[0, :, j:j + 1] for j in range(HEADS_PER_GROUP)]

        def step(jb, accs):
            ks = pl.ds(pl.multiple_of(jb * ab, ab), ab)
            cb = lax.dot_general(ci, b_ref[0, ks, :], _NT, preferred_element_type=F32)
            ok = _causal_ok(i, jb)
            new = []
            for j in range(HEADS_PER_GROUP):
                decay = jnp.exp(jnp.where(ok, acol[j] - at_ref[0, j:j + 1, ks], NEG))
                g = _bf(cb * decay)
                new.append(accs[j] + jnp.dot(g, x_ref[0, ks, HEAD_DIM * j:HEAD_DIM * (j + 1)], preferred_element_type=F32))
            return tuple(new)

        accs = lax.fori_loop(0, i + 1, step, tuple(jnp.zeros((ab, HEAD_DIM), F32) for _ in range(HEADS_PER_GROUP)))
        y_ref[0] = jnp.concatenate(list(accs) + [jnp.zeros((ab, GROUP_LANES - HEADS_PER_GROUP * HEAD_DIM), F32)], axis=1)

    return pl.pallas_call(
        body, name="ssd_fwd", grid=(nb_, N_GROUPS, s // ab),
        in_specs=[pl.BlockSpec((1, s, GROUP_LANES), lambda b, g, i: (b, 0, g)),
                  pl.BlockSpec((1, s, D_STATE), lambda b, g, i: (b, 0, g)),
                  pl.BlockSpec((1, ab, D_STATE), lambda b, g, i: (b, i, N_GROUPS + g)),
                  pl.BlockSpec((1, ab, LANES), lambda b, g, i: (b, i, g)),
                  pl.BlockSpec((1, 8, s), lambda b, g, i: (b, (LANES // 8) * g, 0))],
        out_specs=pl.BlockSpec((1, ab, GROUP_LANES), lambda b, g, i: (b, i, g)),
        out_shape=jax.ShapeDtypeStruct((nb_, s, N_GROUPS * GROUP_LANES), F32),
        compiler_params=_params("parallel", "parallel", "parallel"),
    )(xdtg, bc, bc, acum, acum_t)


def _ssd_bwd(xdtg, bc, acum, acum_t, dyg):
    nb_, s, _ = xdtg.shape
    ab = SEQ_BLOCK
    nblk = s // ab
    hpg = HEADS_PER_GROUP

    def body(x_ref, b_ref, c_ref, ac_ref, at_ref, dy_ref, dx_ref, db_ref, dc_ref, dac_ref, dar_ref):
        dx_ref[...] = jnp.zeros_like(dx_ref)
        db_ref[...] = jnp.zeros_like(db_ref)
        dac_ref[...] = jnp.zeros_like(dac_ref)
        dar_ref[...] = jnp.zeros_like(dar_ref)

        def outer(i, carry):
            qs = pl.ds(pl.multiple_of(i * ab, ab), ab)
            ci = c_ref[0, qs, :]
            dyi = [_bf(dy_ref[0, qs, HEAD_DIM * j:HEAD_DIM * (j + 1)]) for j in range(hpg)]
            acol = [ac_ref[0, qs, j:j + 1] for j in range(hpg)]

            def inner(jb, st):
                dc_acc, rs = st[0], list(st[1:])
                ks = pl.ds(pl.multiple_of(jb * ab, ab), ab)
                bj = b_ref[0, ks, :]
                cb = lax.dot_general(ci, bj, _NT, preferred_element_type=F32)
                ok = _causal_ok(i, jb)
                dcb = jnp.zeros((ab, ab), F32)
                for j in range(hpg):
                    hs = slice(HEAD_DIM * j, HEAD_DIM * (j + 1))
                    decay = jnp.exp(jnp.where(ok, acol[j] - at_ref[0, j:j + 1, ks], NEG))
                    g = cb * decay
                    dg = lax.dot_general(dyi[j], x_ref[0, ks, hs], _NT, preferred_element_type=F32)
                    dx_ref[0, ks, hs] += lax.dot_general(_bf(g), dyi[j], _TN, preferred_element_type=F32)
                    dcb = dcb + dg * decay
                    mm = dg * g
                    rs[j] = rs[j] + jnp.sum(mm, axis=1, keepdims=True)
                    dar_ref[0, j:j + 1, ks] -= jnp.sum(mm, axis=0, keepdims=True)
                dcb16 = _bf(dcb)
                db_ref[0, ks, :] += lax.dot_general(dcb16, ci, _TN, preferred_element_type=F32)
                return (dc_acc + jnp.dot(dcb16, bj, preferred_element_type=F32), *rs)

            init = (jnp.zeros((ab, D_STATE), F32),) + tuple(jnp.zeros((ab, 1), F32) for _ in range(hpg))
            st = lax.fori_loop(0, i + 1, inner, init)
            dc_ref[0, qs, :] = st[0]
            for j in range(hpg):
                dac_ref[0, qs, j:j + 1] = st[1 + j]
            return carry

        lax.fori_loop(0, nblk, outer, 0)

    xblk = pl.BlockSpec((1, s, GROUP_LANES), lambda b, g: (b, 0, g))
    sblk = pl.BlockSpec((1, s, D_STATE), lambda b, g: (b, 0, g))
    tblk = pl.BlockSpec((1, 8, s), lambda b, g: (b, (LANES // 8) * g, 0))
    return pl.pallas_call(
        body, name="ssd_bwd", grid=(nb_, N_GROUPS),
        in_specs=[xblk, sblk, pl.BlockSpec((1, s, D_STATE), lambda b, g: (b, 0, N_GROUPS + g)), sblk, tblk, xblk],
        out_specs=[xblk, sblk, sblk, sblk, pl.BlockSpec((1, 8, s), lambda b, g: (b, g, 0))],
        out_shape=[jax.ShapeDtypeStruct((nb_, s, N_GROUPS * GROUP_LANES), F32),
                   jax.ShapeDtypeStruct((nb_, s, N_GROUPS * D_STATE), F32),
                   jax.ShapeDtypeStruct((nb_, s, N_GROUPS * D_STATE), F32),
                   jax.ShapeDtypeStruct((nb_, s, N_GROUPS * LANES), F32),
                   jax.ShapeDtypeStruct((nb_, N_GROUPS * 8, s), F32)],
        compiler_params=_params("parallel", "parallel"),
    )(xdtg, bc, bc, acum, acum_t, dyg)


def _interleave(wg, wu):
    k, f = wg.shape
    gi = GATE_UP_INTERLEAVE
    return jnp.stack([wg.reshape(k, f // gi, gi), wu.reshape(k, f // gi, gi)], axis=2).reshape(k, 2 * f)


def _head_expanders():
    e_x = np.zeros((LANES, N_GROUPS * GROUP_LANES), np.float32)
    e_a = np.zeros((LANES, N_GROUPS * LANES), np.float32)
    for h in range(N_HEADS):
        g, j = divmod(h, HEADS_PER_GROUP)
        e_x[h, GROUP_LANES * g + HEAD_DIM * j:GROUP_LANES * g + HEAD_DIM * (j + 1)] = 1.0
        e_a[h, LANES * g + j] = 1.0
    return jnp.asarray(e_x), jnp.asarray(e_a)


def _pad_lanes(v, n=LANES):
    return jnp.pad(v, ((0, 0), (0, n - v.shape[1])))


def _local_step(x, positions, target, w, late_job=None, late_weights=None):
    nb, s, d = x.shape
    t = nb * s
    x2 = x.reshape(t, d)
    tgt2 = target.reshape(t, d)

    wgu1 = _interleave(w["ffn1_gate"], w["ffn1_up"])
    if late_job is None:
        au1, hm1 = _mm_swiglu("ffn1_up", x2, wgu1)
    else:
        au1, hm1, late = _mm_swiglu("ffn1_up", x2, wgu1, carry=late_job)
        w = {**w, **late_weights(late)}

    wgu2 = _interleave(w["ffn2_gate"], w["ffn2_up"])
    w_in = w["w_in"]
    wqk, wv, wz = w_in[:, :2 * D_ATTN], w_in[:, 2 * D_ATTN:3 * D_ATTN], w_in[:, 3 * D_ATTN:3 * D_ATTN + D_SSD]
    wxbc = w_in[:, 3 * D_ATTN + D_SSD:3 * D_ATTN + D_SSD + D_CONV]
    wdt = _pad_lanes(w_in[:, 3 * D_ATTN + D_SSD + D_CONV:])

    inv_freq = ROPE_THETA ** (-jnp.arange(0, ROPE_DIM, 2, dtype=F32) / ROPE_DIM)
    half = ROPE_DIM // 2
    head_invf = jnp.concatenate([inv_freq, inv_freq, jnp.zeros((HEAD_DIM - ROPE_DIM,), F32)])
    head_sgn = jnp.concatenate([-jnp.ones((half,), F32), jnp.ones((half,), F32), jnp.zeros((HEAD_DIM - ROPE_DIM,), F32)])
    invf = jnp.tile(head_invf, LANES // HEAD_DIM)[None, :]
    sgn = jnp.tile(head_sgn, LANES // HEAD_DIM)[None, :]
    posf = positions.astype(F32).reshape(t, 1)
    bias_fwd, bias_bwd = _branch_bias_table(s, FWD_KEY_BLOCK), _branch_bias_table(s, SEQ_BLOCK)
    e_x, e_a = _head_expanders()
    dtb, alog = _pad_lanes(w["dt_bias"]), _pad_lanes(w["a_log"])
    dskip = jnp.repeat(w["d_skip"], HEAD_DIM, axis=1)

    h1, r1 = _mm_res_ln("ffn1_down_ln1", hm1, w["ffn1_down"], x2, w["ln1_g"], w["ln1_b"], scale=0.5)

    qk = _mm("proj_qk", [(h1, wqk)], tn=768)
    v16 = _mm("proj_v", [(h1, wv)], tn=768, out_dtype=BF16)
    z = _mm("proj_z", [(h1, wz)], tn=768)
    xbc_pre = _mm("proj_xbc", [(h1, wxbc)], tn=896)
    dtp = _mm("proj_dt", [(h1, wdt)], tn=LANES)

    q16, k16, cs = _rope_fwd(qk, posf, invf, sgn)
    to3 = lambda a: a.reshape(nb, s, a.shape[-1])
    attn_o, lse = _attn_fwd(to3(q16), to3(k16), to3(v16), bias_fwd)

    xbc = _conv_fwd(to3(xbc_pre), w["conv_w"], w["conv_b"]).reshape(t, D_CONV)
    xdtg, bc16, dag = _ssd_prep_fwd(xbc, dtp, dtb, alog, e_x, e_a)
    acum, acum_t = _cumsum_fwd(to3(dag))
    yg = _ssd_fwd(to3(xdtg), to3(bc16), acum, acum_t)

    cat = _norms_fwd(attn_o.reshape(t, D_ATTN), yg.reshape(t, -1), xbc, z, w["attn_norm_w"], w["ssd_norm_w"], dskip)
    h2, r2 = _mm_res_ln("w_out_ln2", cat, w["w_out"], h1, w["ln2_g"], w["ln2_b"], scale=1.0)

    au2, hm2 = _mm_swiglu("ffn2_up", h2, wgu2)
    _, r3 = _mm_res_ln("ffn2_down_ln3", hm2, w["ffn2_down"], h2, w["ln3_g"], w["ln3_b"], scale=0.5)

    g = {}
    dr3, g["ln3_g"], g["ln3_b"], loss = _ln_loss_bwd("loss_ln3_bwd", r3, w["ln3_g"], w["ln3_b"], tgt2)

    dau2 = _mm_swiglu_bwd("ffn2_act_bwd", dr3, w["ffn2_down"].T, au2, scale=0.5)
    g["ffn2_down"] = _mm_tn("ffn2_down_dw", hm2, dr3, scale=0.5, tk=D_FF // 2, tn=512)
    g["ffn2_gate"], g["ffn2_up"] = _mm_tn_gate_up("ffn2_up_dw", h2, dau2)
    dh2 = _mm("ffn2_dx", [(dau2, wgu2.T)], res=dr3, res_scale=ALPHA)

    dr2, g["ln2_g"], g["ln2_b"] = _ln_bwd("ln2_bwd", r2, w["ln2_g"], w["ln2_b"], dh2)
    dcat = _mm("w_out_dx", [(dr2, w["w_out"].T)], tn=768)
    g["w_out"] = _mm_tn("w_out_dw", cat, dr2, tk=768, tn=1024)

    d_attn, dyg, dxs_a, dz16, g["attn_norm_w"], g["ssd_norm_w"], ddskip = _norms_bwd(
        attn_o.reshape(t, D_ATTN), yg.reshape(t, -1), xbc, z, w["attn_norm_w"], w["ssd_norm_w"], dskip, dcat)
    g["d_skip"] = ddskip.reshape(N_HEADS, HEAD_DIM).sum(axis=1)[None, :]

    dq, dk, dv16 = _attn_bwd(to3(q16), to3(k16), to3(v16), attn_o, to3(d_attn), lse, bias_bwd)
    dqk16 = _rope_bwd(dq.reshape(t, D_ATTN), dk.reshape(t, D_ATTN), cs)

    dxdtg, dbm, dcm, dacol, darow = _ssd_bwd(to3(xdtg), to3(bc16), acum, acum_t, to3(dyg))
    ddag = _cumsum_bwd(dacol, darow)
    dxbc, ddtp16, ddtb, dalog = _ssd_prep_bwd(xbc, dtp, dtb, alog, e_x, e_a, dxdtg.reshape(t, -1), ddag.reshape(t, -1),
                                               dxs_a, dbm.reshape(t, -1), dcm.reshape(t, -1))
    g["dt_bias"], g["a_log"] = ddtb[:, :N_HEADS], dalog[:, :N_HEADS]
    dxbc_pre16, dconv_w, g["conv_b"] = _conv_bwd(to3(xbc_pre), w["conv_w"], w["conv_b"], to3(dxbc))
    g["conv_w"] = dconv_w[:CONV_WIDTH]
    dxbc_pre16 = dxbc_pre16.reshape(t, D_CONV)
    dv16 = dv16.reshape(t, D_ATTN)

    dh1 = _mm("w_in_dx", [(dqk16, wqk.T), (dv16, wv.T), (dz16, wz.T), (dxbc_pre16, wxbc.T), (ddtp16, wdt.T)],
              res=dr2, res_scale=ALPHA)
    g["w_in"] = jnp.concatenate([
        _mm_tn("w_in_dw_qk", h1, dqk16, tk=1024, tn=512),
        _mm_tn("w_in_dw_v", h1, dv16, tk=1024, tn=768),
        _mm_tn("w_in_dw_z", h1, dz16, tk=1024, tn=768),
        _mm_tn("w_in_dw_xbc", h1, dxbc_pre16, tk=1024, tn=896),
        _mm_tn("w_in_dw_dt", h1, ddtp16, tk=1024, tn=LANES)[:, :N_HEADS],
    ], axis=1)

    dr1, g["ln1_g"], g["ln1_b"] = _ln_bwd("ln1_bwd", r1, w["ln1_g"], w["ln1_b"], dh1)
    dau1 = _mm_swiglu_bwd("ffn1_act_bwd", dr1, w["ffn1_down"].T, au1, scale=0.5)
    g["ffn1_down"] = _mm_tn("ffn1_down_dw", hm1, dr1, scale=0.5, tk=D_FF // 2, tn=512)
    g["ffn1_gate"], g["ffn1_up"] = _mm_tn_gate_up("ffn1_up_dw", x2, dau1)
    dx = _mm("ffn1_dx", [(dau1, wgu1.T)], res=dr1, res_scale=ALPHA)
    return loss, dx.reshape(nb, s, d), g


_HBM = pl.BlockSpec(memory_space=pltpu.HBM)
N_CHIPS = 4
N_DEVICES = 8


def _place():
    return lax.axis_index("x"), lax.axis_index("y"), lax.axis_index("c")


def _other_chips(x, y):
    return [(1 - x, y), (x, 1 - y), (1 - x, 1 - y)]


class _GatherJob:
    def __init__(self, shards):
        assert all((a.shape[0] // 2) % 16 == 0 for a in shards)
        self.n = len(shards)
        self.shapes = [a.shape for a in shards]
        self.operands = [a.reshape(2, a.shape[0] // 2, a.shape[1]) for a in shards]
        self.out_shape = [jax.ShapeDtypeStruct((N_CHIPS,) + a.shape, a.dtype) for a in self.operands]
        pair = pltpu.SemaphoreType.DMA((self.n, N_CHIPS - 1))
        self.scratch_shapes = [pair, pair, pair, pair, pltpu.SemaphoreType.DMA((self.n,))]

    def results(self, outs):
        return [o.reshape((N_CHIPS,) + s) for o, s in zip(outs, self.shapes)]

    def phases(self, ins, outs, sems):
        n = self.n
        send_sems, recv_sems, fwd_send_sems, fwd_recv_sems, loc_sems = sems
        x, y, c = _place()
        me = 2 * x + y
        peers = _other_chips(x, y)

        def ici(t, p, src_chip):
            px, py = peers[p]
            return pltpu.make_async_remote_copy(
                ins[t].at[c] if src_chip is None else outs[t].at[src_chip, c],
                outs[t].at[me if src_chip is None else src_chip, c],
                send_sems.at[t, p], recv_sems.at[t, p], device_id=(px, py, c), device_id_type=MESH)

        def d2d(t, p, core):
            px, py = peers[p]
            return pltpu.make_async_remote_copy(
                outs[t].at[2 * px + py, core], outs[t].at[2 * px + py, core],
                fwd_send_sems.at[t, p], fwd_recv_sems.at[t, p], device_id=(x, y, 1 - c), device_id_type=MESH)

        def local(t):
            return pltpu.make_async_copy(ins[t], outs[t].at[me], loc_sems.at[t])

        pairs = [(t, p) for t in range(n) for p in range(N_CHIPS - 1)]

        def start():
            for t in range(n):
                local(t).start()
            for t, p in pairs:
                ici(t, p, None).start()

        def forward():
            for t, p in pairs:
                px, py = peers[p]
                ici(t, p, 2 * px + py).wait_recv()
                d2d(t, p, c).start()

        def finish():
            for t, p in pairs:
                d2d(t, p, 1 - c).wait_recv()
            for t, p in pairs:
                ici(t, p, None).wait_send()
                d2d(t, p, c).wait_send()
            for t in range(n):
                local(t).wait()

        return start, forward, finish


def _gather_shards(shards, name="gather_weights"):
    job = _GatherJob(shards)
    n = job.n

    def body(*refs):
        start, forward, finish = job.phases(refs[:n], refs[n:2 * n], refs[2 * n:])
        start()
        forward()
        finish()

    outs = pl.pallas_call(
        body, name=name, in_specs=[_HBM] * n, out_specs=[_HBM] * n,
        out_shape=job.out_shape, scratch_shapes=job.scratch_shapes,
    )(*job.operands)
    return job.results(outs)


def _exchange_partials(stacks):
    n = len(stacks)

    def body(*refs):
        ins, outs = refs[:n], refs[n:2 * n]
        send_sems, recv_sems, loc_sems = refs[2 * n:]
        x, y, c = _place()
        me = 2 * x + y
        peers = _other_chips(x, y)
        locs, sends = [], []
        for t in range(n):
            loc = pltpu.make_async_copy(ins[t].at[me], outs[t].at[me], loc_sems.at[t])
            loc.start()
            locs.append(loc)
            for p, (px, py) in enumerate(peers):
                cp = pltpu.make_async_remote_copy(ins[t].at[2 * px + py], outs[t].at[me], send_sems.at[t, p],
                                                  recv_sems.at[t, p], device_id=(px, py, c), device_id_type=MESH)
                cp.start()
                sends.append(cp)
        for t in range(n):
            for p, (px, py) in enumerate(peers):
                pltpu.make_async_remote_copy(ins[t].at[me], outs[t].at[2 * px + py], send_sems.at[t, p],
                                             recv_sems.at[t, p], device_id=(px, py, c), device_id_type=MESH).wait_recv()
        for cp in sends:
            cp.wait_send()
        for loc in locs:
            loc.wait()

    return pl.pallas_call(
        body, name="exchange_partials",
        in_specs=[_HBM] * n, out_specs=[_HBM] * n,
        out_shape=[jax.ShapeDtypeStruct(a.shape, a.dtype) for a in stacks],
        scratch_shapes=[pltpu.SemaphoreType.DMA((n, N_CHIPS - 1)), pltpu.SemaphoreType.DMA((n, N_CHIPS - 1)),
                        pltpu.SemaphoreType.DMA((n,))],
    )(*stacks)


def _sibling_halves(stacks):
    n = len(stacks)
    halves = [a.shape[1] // 2 for a in stacks]
    split = [a.reshape(a.shape[0], 2, h, a.shape[2]) for a, h in zip(stacks, halves)]

    def body(*refs):
        ins, outs = refs[:n], refs[n:2 * n]
        send_sems, recv_sems = refs[2 * n:]
        x, y, c = _place()
        cps = []
        for t in range(n):
            cp = pltpu.make_async_remote_copy(ins[t].at[:, 1 - c], outs[t], send_sems.at[t], recv_sems.at[t],
                                              device_id=(x, y, 1 - c), device_id_type=MESH)
            cp.start()
            cps.append(cp)
        for cp in cps:
            cp.wait()

    return pl.pallas_call(
        body, name="sibling_halves",
        in_specs=[_HBM] * n, out_specs=[_HBM] * n,
        out_shape=[jax.ShapeDtypeStruct((a.shape[0], h, a.shape[2]), a.dtype) for a, h in zip(stacks, halves)],
        scratch_shapes=[pltpu.SemaphoreType.DMA((n,)), pltpu.SemaphoreType.DMA((n,))],
    )(*split)


def _sibling_fill(halves):
    n = len(halves)
    hs = [a.shape[0] for a in halves]

    def body(*refs):
        ins, outs = refs[:n], refs[n:2 * n]
        send_sems, recv_sems, loc_sems = refs[2 * n:]
        x, y, c = _place()

        def rows(t, core):
            return outs[t].at[core]

        locs, cps = [], []
        for t in range(n):
            loc = pltpu.make_async_copy(ins[t], rows(t, c), loc_sems.at[t])
            loc.start()
            locs.append(loc)
            cp = pltpu.make_async_remote_copy(ins[t], rows(t, c), send_sems.at[t], recv_sems.at[t],
                                              device_id=(x, y, 1 - c), device_id_type=MESH)
            cp.start()
            cps.append(cp)
        for t in range(n):
            pltpu.make_async_remote_copy(ins[t], rows(t, 1 - c), send_sems.at[t], recv_sems.at[t],
                                         device_id=(x, y, 1 - c), device_id_type=MESH).wait_recv()
        for cp in cps:
            cp.wait_send()
        for loc in locs:
            loc.wait()

    outs = pl.pallas_call(
        body, name="sibling_fill",
        in_specs=[_HBM] * n, out_specs=[_HBM] * n,
        out_shape=[jax.ShapeDtypeStruct((2,) + a.shape, a.dtype) for a in halves],
        scratch_shapes=[pltpu.SemaphoreType.DMA((n,)), pltpu.SemaphoreType.DMA((n,)), pltpu.SemaphoreType.DMA((n,))],
    )(*halves)
    return [o.reshape(2 * a.shape[0], a.shape[1]) for o, a in zip(outs, halves)]


def _half_sum(name, own, other, core):
    k, r, cols = own.shape
    h = r // 2
    tr = next(cand for cand in (128, 176, 64, 32, 16) if h % cand == 0)
    nblk = h // tr

    def body(core_ref, own_ref, other_ref, o_ref):
        o_ref[...] = _bf(own_ref[...] + other_ref[...].astype(F32))

    grid_spec = pltpu.PrefetchScalarGridSpec(
        num_scalar_prefetch=1, grid=(nblk,),
        in_specs=[pl.BlockSpec((k, tr, cols), lambda i, core_ref: (0, i + core_ref[0] * nblk, 0)),
                  pl.BlockSpec((k, tr, cols), lambda i, core_ref: (0, i, 0))],
        out_specs=pl.BlockSpec((k, tr, cols), lambda i, core_ref: (0, i, 0)))
    return pl.pallas_call(
        body, name=name, grid_spec=grid_spec, out_shape=jax.ShapeDtypeStruct((k, h, cols), BF16),
        compiler_params=_params("parallel"),
    )(core.reshape(1).astype(jnp.int32), own, other)


def _small_allreduce(v):
    r = v.shape[0]

    def body(v_ref, tot_ref, slots, send_sems, recv_sems):
        x, y, c = _place()
        me = 4 * x + 2 * y + c
        slots[me] = v_ref[...]
        cps, peers = [], []
        for k in range(1, N_DEVICES):
            px = 1 - x if (k >> 2) & 1 else x
            py = 1 - y if (k >> 1) & 1 else y
            pc = 1 - c if k & 1 else c
            cp = pltpu.make_async_remote_copy(v_ref, slots.at[me], send_sems.at[k - 1], recv_sems.at[k - 1],
                                              device_id=(px, py, pc), device_id_type=MESH)
            cp.start()
            cps.append(cp)
            peers.append((px, py, pc))
        for k, (px, py, pc) in enumerate(peers):
            pltpu.make_async_remote_copy(v_ref, slots.at[4 * px + 2 * py + pc], send_sems.at[k], recv_sems.at[k],
                                         device_id=(px, py, pc), device_id_type=MESH).wait_recv()
        for cp in cps:
            cp.wait_send()
        acc = slots[0]
        for s in range(1, N_DEVICES):
            acc = acc + slots[s]
        tot_ref[...] = acc

    return pl.pallas_call(
        body, name="small_allreduce",
        in_specs=[pl.BlockSpec(memory_space=pltpu.VMEM)], out_specs=pl.BlockSpec(memory_space=pltpu.VMEM),
        out_shape=jax.ShapeDtypeStruct((r, LANES), F32),
        scratch_shapes=[pltpu.VMEM((N_DEVICES, r, LANES), F32), pltpu.SemaphoreType.DMA((N_DEVICES - 1,)),
                        pltpu.SemaphoreType.DMA((N_DEVICES - 1,))],
    )(v)


def _elementwise(name, fn, ins, out_dtypes):
    r, c = ins[0].shape[-2:]
    tr = next((cand for cand in (256, 176, 128, 64, 32, 16) if r % cand == 0), r)
    nin = len(ins)

    def body(*refs):
        outs = fn(*[ref[...] for ref in refs[:nin]])
        for o_ref, o in zip(refs[nin:], outs):
            o_ref[...] = o.astype(o_ref.dtype)

    in_specs = [pl.BlockSpec((tr, c), lambda i: (i, 0)) if a.ndim == 2 else pl.BlockSpec((a.shape[0], tr, c), lambda i: (0, i, 0))
                for a in ins]
    return pl.pallas_call(
        body, name=name, grid=(r // tr,), in_specs=in_specs,
        out_specs=[pl.BlockSpec((tr, c), lambda i: (i, 0)) for _ in out_dtypes],
        out_shape=[jax.ShapeDtypeStruct((r, c), dt) for dt in out_dtypes],
        compiler_params=_params("parallel"),
    )(*ins)


def _sum_slots(name, stack):
    def fn(v):
        acc = v[0].astype(F32)
        for s in range(1, v.shape[0]):
            acc = acc + v[s].astype(F32)
        return [acc]
    return _elementwise(name, fn, [stack], [F32])[0]


def _adamw(name, grads, w, m, v):
    ng = len(grads)

    def fn(*vals):
        g = vals[0] if ng == 1 else vals[0] + vals[1]
        w_v, m_v, v_v = vals[ng:]
        m2 = ADAM_B1 * m_v + (1.0 - ADAM_B1) * g
        v2 = ADAM_B2 * v_v + (1.0 - ADAM_B2) * jnp.square(g)
        m_hat = m2 / (1.0 - ADAM_B1 ** ADAM_STEP)
        v_hat = v2 / (1.0 - ADAM_B2 ** ADAM_STEP)
        delta = -ADAM_LR * (m_hat / (jnp.sqrt(v_hat) + ADAM_EPS) + ADAM_WD * w_v)
        return [g, delta, m2, v2]

    return _elementwise(name, fn, list(grads) + [w, m, v], [F32] * 4)


_MATRICES = (("ffn1_gate", 1), ("ffn1_up", 1), ("ffn1_down", 0), ("w_in", 1), ("w_out", 0),
             ("ffn2_gate", 1), ("ffn2_up", 1), ("ffn2_down", 0))
_VECTORS = ("ln1_g", "ln1_b", "conv_b", "dt_bias", "a_log", "d_skip", "attn_norm_w", "ssd_norm_w",
            "ln2_g", "ln2_b", "ln3_g", "ln3_b")
_WEIGHT_ORDER = ("ln1_g", "ln1_b", "ffn1_gate", "ffn1_up", "ffn1_down", "w_in", "conv_w", "conv_b", "dt_bias", "a_log",
                 "d_skip", "attn_norm_w", "ssd_norm_w", "w_out", "ln2_g", "ln2_b", "ffn2_gate", "ffn2_up", "ffn2_down",
                 "ln3_g", "ln3_b")


def _pack_rows(vectors):
    parts = []
    for vec in vectors:
        flat = vec.reshape(-1)
        parts.append(jnp.pad(flat, (0, (-flat.shape[0]) % LANES)))
    flat = jnp.concatenate(parts)
    flat = jnp.pad(flat, (0, (-flat.shape[0]) % (8 * LANES)))
    return flat.reshape(-1, LANES)


def _unpack_rows(packed, shapes):
    flat = packed.reshape(-1)
    out, off = [], 0
    for shape in shapes:
        size = int(np.prod(shape))
        out.append(flat[off:off + size].reshape(shape))
        off += size + (-size) % LANES
    return out


def _assemble(stack, axis):
    if axis == 0:
        return stack.reshape(-1, stack.shape[2])
    return jnp.concatenate([stack[s] for s in range(N_CHIPS)], axis=1)


def _split(full, axis):
    if axis == 0:
        return full.reshape(N_CHIPS, -1, full.shape[1])
    cols = full.shape[1] // N_CHIPS
    return jnp.stack([full[:, cols * s:cols * (s + 1)] for s in range(N_CHIPS)])


def kernel(x, positions, ln1_g, ln1_b, ffn1_gate, ffn1_up, ffn1_down, w_in, conv_w, conv_b, dt_bias, a_log, d_skip, attn_norm_w, ssd_norm_w, w_out, ln2_g, ln2_b, ffn2_gate, ffn2_up, ffn2_down, ln3_g, ln3_b, loss_target, m_ln1_g, m_ln1_b, m_ffn1_gate, m_ffn1_up, m_ffn1_down, m_w_in, m_conv_w, m_conv_b, m_dt_bias, m_a_log, m_d_skip, m_attn_norm_w, m_ssd_norm_w, m_w_out, m_ln2_g, m_ln2_b, m_ffn2_gate, m_ffn2_up, m_ffn2_down, m_ln3_g, m_ln3_b, v_ln1_g, v_ln1_b, v_ffn1_gate, v_ffn1_up, v_ffn1_down, v_w_in, v_conv_w, v_conv_b, v_dt_bias, v_a_log, v_d_skip, v_attn_norm_w, v_ssd_norm_w, v_w_out, v_ln2_g, v_ln2_b, v_ffn2_gate, v_ffn2_up, v_ffn2_down, v_ln3_g, v_ln3_b):
    given = dict(locals())
    wts = {n: given[n] for n in _WEIGHT_ORDER}
    mom_m = {n: given["m_" + n] for n in _WEIGHT_ORDER}
    mom_v = {n: given["v_" + n] for n in _WEIGHT_ORDER}
    chip = 2 * lax.axis_index("x") + lax.axis_index("y")

    early = [(n, axis) for n, axis in _MATRICES if n.startswith("ffn1")]
    late = [(n, axis) for n, axis in _MATRICES if not n.startswith("ffn1")]
    gathered = _gather_shards([wts[n][0].astype(BF16) for n, _ in early], name="gather_ffn1")
    full = {n: _assemble(st, axis) for (n, axis), st in zip(early, gathered)}
    for n in _VECTORS:
        full[n] = wts[n]
    conv_rows = jnp.pad(wts["conv_w"][0], ((0, 32 - CONV_WIDTH), (0, 0)))
    late_job = _GatherJob([wts[n][0].astype(BF16) for n, _ in late] + [conv_rows])

    def late_weights(results):
        out = {n: _assemble(st, axis) for (n, axis), st in zip(late, results)}
        out["conv_w"] = _assemble(results[-1][:, :CONV_WIDTH], 1)
        return out

    loss, grad_x, g = _local_step(x, positions, loss_target, full, late_job, late_weights)

    core = lax.axis_index("c")
    partials = [_split(g[n], axis) for n, axis in _MATRICES]
    from_sibling = _sibling_halves([p.astype(BF16) for p in partials])
    chip_sums = [_half_sum("core_sum_" + n, p, o, core) for (n, _), p, o in zip(_MATRICES, partials, from_sibling)]
    received = _exchange_partials(chip_sums)
    half_totals = [_sum_slots("sum_partials_" + n, st) for (n, _), st in zip(_MATRICES, received)]
    totals = _sibling_fill(half_totals)

    small_shapes = [g[n].shape for n in _VECTORS] + [g["conv_w"].shape, (1,)]
    total = _small_allreduce(_pack_rows([g[n] for n in _VECTORS] + [g["conv_w"], loss[0, :1]]))
    small = _unpack_rows(total, small_shapes)
    loss_out = small[-1].reshape(())

    grads, deltas, new_m, new_v = {}, {}, {}, {}
    for (n, _), g_tot in zip(_MATRICES, totals):
        res = _adamw("adamw_" + n, [g_tot], wts[n][0], mom_m[n][0], mom_v[n][0])
        grads[n], deltas[n], new_m[n], new_v[n] = [r[None] for r in res]

    vec_shapes = [wts[n].shape for n in _VECTORS]
    res = _adamw("adamw_vectors", [_pack_rows(small[:len(_VECTORS)])], _pack_rows([wts[n] for n in _VECTORS]),
                 _pack_rows([mom_m[n] for n in _VECTORS]), _pack_rows([mom_v[n] for n in _VECTORS]))
    for dst, packed in zip((grads, deltas, new_m, new_v), res):
        for n, val in zip(_VECTORS, _unpack_rows(packed, vec_shapes)):
            dst[n] = val

    cols = conv_w.shape[2]
    g_conv = lax.dynamic_slice_in_dim(small[len(_VECTORS)], chip * cols, cols, axis=1)
    res = _adamw("adamw_conv_w", [g_conv], wts["conv_w"][0], mom_m["conv_w"][0], mom_v["conv_w"][0])
    grads["conv_w"], deltas["conv_w"], new_m["conv_w"], new_v["conv_w"] = [r[None] for r in res]

    return (loss_out, grad_x, *[grads[n] for n in _WEIGHT_ORDER], *[deltas[n] for n in _WEIGHT_ORDER],
            *[new_m[n] for n in _WEIGHT_ORDER], *[new_v[n] for n in _WEIGHT_ORDER])
```

```python
import functools

import numpy as np
import jax
import jax.numpy as jnp
from jax import lax
from jax.experimental import pallas as pl
from jax.experimental.pallas import tpu as pltpu

F32, BF16 = jnp.float32, jnp.bfloat16

D_MODEL = 1024
D_FF = 2816
N_HEADS = 12
HEAD_DIM = 64
D_ATTN = 768
D_SSD = 768
N_GROUPS = 4
HEADS_PER_GROUP = 3
D_STATE = 128
D_CONV = 1792
CONV_WIDTH = 4
ROPE_DIM = 16
ROPE_THETA = 500000.0
ALPHA = 2.0 ** 0.25
LN_EPS = 1e-5
RMS_EPS = 1e-6
ADAM_LR, ADAM_B1, ADAM_B2, ADAM_EPS, ADAM_WD, ADAM_STEP = 0.001, 0.9, 0.999, 1e-08, 0.01, 10

LANES = 128
GATE_UP_INTERLEAVE = 256
SEQ_BLOCK = 256
GROUP_LANES = 256
VMEM_LIMIT = 56 * 1024 * 1024
NEG = -1e30
MESH = pl.DeviceIdType.MESH
HIGHEST = lax.Precision.HIGHEST

_NT = (((1,), (1,)), ((), ()))
_TN = (((0,), (0,)), ((), ()))


def _params(*sem):
    return pltpu.CompilerParams(dimension_semantics=sem, vmem_limit_bytes=VMEM_LIMIT)


def _bf(v):
    return v.astype(BF16)


def _mm(name, pairs, *, scale=1.0, res=None, res_scale=1.0, out_dtype=F32, tm=512, tn=512):
    m, n = pairs[0][0].shape[0], pairs[0][1].shape[1]
    tm, tn = min(tm, m), min(tn, n)
    assert m % tm == 0 and n % tn == 0, (name, m, n, tm, tn)
    npair = len(pairs)

    def body(*refs):
        acc = None
        for a_ref, b_ref in zip(refs[:npair], refs[npair:2 * npair]):
            d = jnp.dot(_bf(a_ref[...]), b_ref[...], preferred_element_type=F32)
            acc = d if acc is None else acc + d
        if scale != 1.0:
            acc = acc * scale
        if res is not None:
            acc = acc + res_scale * refs[2 * npair][...]
        refs[-1][...] = acc.astype(out_dtype)

    in_specs = [pl.BlockSpec((tm, a.shape[1]), lambda i, j: (i, 0)) for a, _ in pairs]
    in_specs += [pl.BlockSpec((b.shape[0], tn), lambda i, j: (0, j)) for _, b in pairs]
    args = [a for a, _ in pairs] + [b for _, b in pairs]
    if res is not None:
        in_specs.append(pl.BlockSpec((tm, tn), lambda i, j: (i, j)))
        args.append(res)
    return pl.pallas_call(
        body, name=name, grid=(m // tm, n // tn), in_specs=in_specs,
        out_specs=pl.BlockSpec((tm, tn), lambda i, j: (i, j)),
        out_shape=jax.ShapeDtypeStruct((m, n), out_dtype),
        compiler_params=_params("parallel", "parallel"),
    )(*args)


def _mm_tn(name, x, dy, *, scale=1.0, tk=512, tn=512, tt=1024):
    t, k = x.shape
    n = dy.shape[1]
    tk, tn, tt = min(tk, k), min(tn, n), min(tt, t)
    assert k % tk == 0 and n % tn == 0 and t % tt == 0, (name, k, n, t)
    nt = t // tt

    def body(x_ref, dy_ref, o_ref):
        step = pl.program_id(2)
        d = lax.dot_general(_bf(x_ref[...]), _bf(dy_ref[...]), _TN, preferred_element_type=F32)

        @pl.when(step == 0)
        def _():
            o_ref[...] = d

        @pl.when(step > 0)
        def _():
            o_ref[...] += d

        if scale != 1.0:
            @pl.when(step == nt - 1)
            def _():
                o_ref[...] = o_ref[...] * scale

    return pl.pallas_call(
        body, name=name, grid=(k // tk, n // tn, nt),
        in_specs=[pl.BlockSpec((tt, tk), lambda i, j, s: (s, i)), pl.BlockSpec((tt, tn), lambda i, j, s: (s, j))],
        out_specs=pl.BlockSpec((tk, tn), lambda i, j, s: (i, j)),
        out_shape=jax.ShapeDtypeStruct((k, n), F32),
        compiler_params=_params("parallel", "parallel", "arbitrary"),
    )(x, dy)


def _mm_tn_gate_up(name, x, dau, *, tt=1024):
    t, k = x.shape
    gi = GATE_UP_INTERLEAVE
    nj = dau.shape[1] // (2 * gi)
    tt = min(tt, t)
    nt = t // tt

    def body(x_ref, dy_ref, g_ref, u_ref):
        step = pl.program_id(1)
        d = lax.dot_general(_bf(x_ref[...]), dy_ref[...], _TN, preferred_element_type=F32)

        @pl.when(step == 0)
        def _():
            g_ref[...] = d[:, :gi]
            u_ref[...] = d[:, gi:]

        @pl.when(step > 0)
        def _():
            g_ref[...] += d[:, :gi]
            u_ref[...] += d[:, gi:]

    out = pl.BlockSpec((k, gi), lambda j, s: (0, j))
    return pl.pallas_call(
        body, name=name, grid=(nj, nt),
        in_specs=[pl.BlockSpec((tt, k), lambda j, s: (s, 0)), pl.BlockSpec((tt, 2 * gi), lambda j, s: (s, j))],
        out_specs=[out, out],
        out_shape=[jax.ShapeDtypeStruct((k, gi * nj), F32)] * 2,
        compiler_params=_params("parallel", "arbitrary"),
    )(x, dau)


def _carried(carry, ins, outs, sems, step, total):
    start, forward, finish = carry.phases(ins, outs, sems)
    pl.when(step == 0)(start)
    if forward is not None:
        pl.when(step == (3 * total) // 4)(forward)
    return lambda: pl.when(step == total - 1)(finish)


def _mm_swiglu(name, x, wgu, *, tm=512, carry=None):
    t, k = x.shape
    gi = GATE_UP_INTERLEAVE
    ni, nj = t // tm, wgu.shape[1] // (2 * gi)
    nc = carry.n if carry is not None else 0

    def body(*refs):
        x_ref, w_ref = refs[:2]
        au_ref, hm_ref = refs[2 + nc:4 + nc]
        if carry is not None:
            step = pl.program_id(0) * nj + pl.program_id(1)
            finish = _carried(carry, refs[2:2 + nc], refs[4 + nc:4 + 2 * nc], refs[4 + 2 * nc:], step, ni * nj)
        au = jnp.dot(_bf(x_ref[...]), w_ref[...], preferred_element_type=F32)
        a, u = au[:, :gi], au[:, gi:]
        au_ref[...] = _bf(au)
        hm_ref[...] = _bf(a * jax.nn.sigmoid(a) * u)
        if carry is not None:
            finish()

    hbm = pl.BlockSpec(memory_space=pltpu.HBM)
    res = pl.pallas_call(
        body, name=name, grid=(ni, nj),
        in_specs=[pl.BlockSpec((tm, k), lambda i, j: (i, 0)), pl.BlockSpec((k, 2 * gi), lambda i, j: (0, j))] + [hbm] * nc,
        out_specs=[pl.BlockSpec((tm, 2 * gi), lambda i, j: (i, j)), pl.BlockSpec((tm, gi), lambda i, j: (i, j))] + [hbm] * nc,
        out_shape=[jax.ShapeDtypeStruct((t, 2 * gi * nj), BF16), jax.ShapeDtypeStruct((t, gi * nj), BF16)]
        + (carry.out_shape if carry is not None else []),
        scratch_shapes=carry.scratch_shapes if carry is not None else [],
        compiler_params=_params(*(("arbitrary", "arbitrary") if carry is not None else ("parallel", "parallel"))),
    )(x, wgu, *(carry.operands if carry is not None else []))
    return res if carry is None else (res[0], res[1], carry.results(res[2:]))


def _mm_swiglu_bwd(name, dr, wdt, au, *, scale, tm=512, carry=None):
    t, k = dr.shape
    gi = GATE_UP_INTERLEAVE
    ni, nj = t // tm, wdt.shape[1] // gi
    nc = carry.n if carry is not None else 0

    def body(*refs):
        dr_ref, w_ref, au_ref = refs[:3]
        o_ref = refs[3 + nc]
        if carry is not None:
            step = pl.program_id(0) * nj + pl.program_id(1)
            finish = _carried(carry, refs[3:3 + nc], refs[4 + nc:4 + 2 * nc], refs[4 + 2 * nc:], step, ni * nj)
        dhm = jnp.dot(_bf(dr_ref[...]), w_ref[...], preferred_element_type=F32) * scale
        au_v = au_ref[...].astype(F32)
        a, u = au_v[:, :gi], au_v[:, gi:]
        sig = jax.nn.sigmoid(a)
        da = dhm * u * (sig * (1.0 + a * (1.0 - sig)))
        du = dhm * (a * sig)
        o_ref[:, :gi] = _bf(da)
        o_ref[:, gi:] = _bf(du)
        if carry is not None:
            finish()

    hbm = pl.BlockSpec(memory_space=pltpu.HBM)
    res = pl.pallas_call(
        body, name=name, grid=(ni, nj),
        in_specs=[pl.BlockSpec((tm, k), lambda i, j: (i, 0)), pl.BlockSpec((k, gi), lambda i, j: (0, j)),
                  pl.BlockSpec((tm, 2 * gi), lambda i, j: (i, j))] + [hbm] * nc,
        out_specs=[pl.BlockSpec((tm, 2 * gi), lambda i, j: (i, j))] + [hbm] * nc,
        out_shape=[jax.ShapeDtypeStruct((t, 2 * gi * nj), BF16)] + (carry.out_shape if carry is not None else []),
        scratch_shapes=carry.scratch_shapes if carry is not None else [],
        compiler_params=_params(*(("arbitrary", "arbitrary") if carry is not None else ("parallel", "parallel"))),
    )(dr, wdt, au, *(carry.operands if carry is not None else []))
    return res[0] if carry is None else (res[0], carry.results(res[1:]))


def _layer_norm(r, g, b):
    mu = jnp.mean(r, axis=-1, keepdims=True)
    var = jnp.mean(jnp.square(r - mu), axis=-1, keepdims=True)
    return (r - mu) * lax.rsqrt(var + LN_EPS) * g + b


def _mm_res_ln(name, a, w, res, g, b, *, scale, tm=256):
    t, k = a.shape
    n = w.shape[1]

    def body(a_ref, w_ref, res_ref, g_ref, b_ref, y_ref, r_ref):
        r = ALPHA * res_ref[...] + scale * jnp.dot(_bf(a_ref[...]), w_ref[...], preferred_element_type=F32)
        r_ref[...] = r
        y_ref[...] = _layer_norm(r, g_ref[...], b_ref[...])

    row = lambda c: pl.BlockSpec((tm, c), lambda i: (i, 0))
    const = lambda shape: pl.BlockSpec(shape, lambda i: (0, 0))
    return pl.pallas_call(
        body, name=name, grid=(t // tm,),
        in_specs=[row(k), const((k, n)), row(n), const((1, n)), const((1, n))],
        out_specs=[row(n), row(n)],
        out_shape=[jax.ShapeDtypeStruct((t, n), F32), jax.ShapeDtypeStruct((t, n), F32)],
        compiler_params=_params("parallel"),
    )(a, w, res, g, b)


def _rowwise(name, fn, rows, consts, row_outs, acc_outs=(), tm=256):
    rows = [r if isinstance(r, tuple) else (r, r.shape[1]) for r in rows]
    t = rows[0][0].shape[0]
    tm = min(tm, t)
    assert t % tm == 0
    nr, nc, no, na = len(rows), len(consts), len(row_outs), len(acc_outs)

    def body(*refs):
        vals = [r[...] for r in refs[:nr + nc]]
        outs, accs = fn(*vals)
        for o_ref, o in zip(refs[nr + nc:nr + nc + no], outs):
            o_ref[...] = o.astype(o_ref.dtype)
        if na:
            step = pl.program_id(0)
            acc_refs = refs[nr + nc + no:]

            @pl.when(step == 0)
            def _():
                for a_ref, a in zip(acc_refs, accs):
                    a_ref[...] = a

            @pl.when(step > 0)
            def _():
                for a_ref, a in zip(acc_refs, accs):
                    a_ref[...] += a

    in_specs = [pl.BlockSpec((tm, w), lambda i: (i, 0)) for _, w in rows]
    in_specs += [pl.BlockSpec(c.shape, lambda i, nd=c.ndim: (0,) * nd) for c in consts]
    out_specs = [pl.BlockSpec((tm, c), lambda i: (i, 0)) for c, _ in row_outs]
    out_specs += [pl.BlockSpec(s, lambda i: (0, 0)) for s in acc_outs]
    out_shape = [jax.ShapeDtypeStruct((t, c), dt) for c, dt in row_outs]
    out_shape += [jax.ShapeDtypeStruct(s, F32) for s in acc_outs]
    res = pl.pallas_call(
        body, name=name, grid=(t // tm,), in_specs=in_specs, out_specs=out_specs, out_shape=out_shape,
        compiler_params=_params("arbitrary" if na else "parallel"),
    )(*[r for r, _ in rows], *consts)
    return res


def _ln_bwd(name, r, g, b, dy):
    def fn(r_v, dy_v, g_v, b_v):
        _, vjp = jax.vjp(_layer_norm, r_v, g_v, b_v)
        dr, dg, db = vjp(dy_v)
        return [dr], [dg, db]
    return _rowwise(name, fn, [r, dy], [g, b], [(r.shape[1], F32)], [(1, r.shape[1])] * 2)


def _ln_loss_bwd(name, r, g, b, target):
    def fn(r_v, t_v, g_v, b_v):
        def loss_fn(rr, gg, bb):
            err = jnp.square(_layer_norm(rr, gg, bb) - t_v)
            return 0.5 * jnp.sum(jnp.mean(err, axis=-1, keepdims=True), axis=0, keepdims=True)
        loss, vjp = jax.vjp(loss_fn, r_v, g_v, b_v)
        dr, dg, db = vjp(jnp.ones((1, 1), F32))
        return [dr], [dg, db, jnp.broadcast_to(loss, (1, LANES))]
    return _rowwise(name, fn, [r, target], [g, b], [(r.shape[1], F32)], [(1, r.shape[1])] * 2 + [(1, LANES)])


def _rope_tables(posf, invf, sgn):
    ang = posf * invf
    return jnp.cos(ang), jnp.sin(ang) * sgn


def _rope_apply(tv, cos, sin):
    lane = lax.broadcasted_iota(jnp.int32, cos.shape, 1)
    first = (lane % HEAD_DIM) < (ROPE_DIM // 2)
    outs = []
    for gidx in range(tv.shape[1] // LANES):
        tg = tv[:, LANES * gidx:LANES * (gidx + 1)]
        sw = jnp.where(first, pltpu.roll(tg, LANES - ROPE_DIM // 2, 1), pltpu.roll(tg, ROPE_DIM // 2, 1))
        outs.append(tg * cos + sw * sin)
    return jnp.concatenate(outs, axis=1)


def _rope_fwd(qk, posf, invf, sgn):
    def fn(qk_v, pos_v, invf_v, sgn_v):
        cos, sin = _rope_tables(pos_v, invf_v, sgn_v)
        q = _rope_apply(qk_v[:, :D_ATTN], cos, sin) * (HEAD_DIM ** -0.5)
        k = _rope_apply(qk_v[:, D_ATTN:], cos, sin)
        return [q, k, jnp.concatenate([cos, sin], axis=1)], []
    return _rowwise("rope_fwd", fn, [qk, posf], [invf, sgn], [(D_ATTN, BF16), (D_ATTN, BF16), (2 * LANES, F32)])


def _rope_bwd(dq, dk, cs):
    def fn(dq_v, dk_v, cs_v):
        cos, sin = cs_v[:, :LANES], -cs_v[:, LANES:]
        gq = _rope_apply(dq_v * (HEAD_DIM ** -0.5), cos, sin)
        gk = _rope_apply(dk_v, cos, sin)
        return [jnp.concatenate([gq, gk], axis=1)], []
    return _rowwise("rope_bwd", fn, [dq, dk, cs], [], [(2 * D_ATTN, BF16)])[0]


def _rms(v, w):
    return v * lax.rsqrt(jnp.mean(v * v, axis=-1, keepdims=True) + RMS_EPS) * w


def _ungroup(yg):
    w = HEADS_PER_GROUP * HEAD_DIM
    return jnp.concatenate([yg[:, GROUP_LANES * g:GROUP_LANES * g + w] for g in range(N_GROUPS)], axis=1)


def _group(xs):
    w = HEADS_PER_GROUP * HEAD_DIM
    parts = []
    for g in range(N_GROUPS):
        parts += [xs[:, w * g:w * (g + 1)], jnp.zeros((xs.shape[0], GROUP_LANES - w), xs.dtype)]
    return jnp.concatenate(parts, axis=1)


def _norms_fn(attn, yg, xs, z, w_attn, w_ssd, dskip):
    a_n = _rms(attn, w_attn)
    y = _ungroup(yg) + dskip * xs
    y_n = _rms(y * (z * jax.nn.sigmoid(z)), w_ssd)
    return jnp.concatenate([a_n, y_n], axis=1)


def _norms_fwd(attn, yg, xbc, z, w_attn, w_ssd, dskip):
    def fn(*v):
        return [_norms_fn(*v)], []
    return _rowwise("norms_fwd", fn, [attn, yg, (xbc, D_SSD), z], [w_attn, w_ssd, dskip], [(D_ATTN + D_SSD, BF16)])[0]


def _norms_bwd(attn, yg, xbc, z, w_attn, w_ssd, dskip, dcat):
    def fn(attn_v, yg_v, xs_v, z_v, dcat_v, wa_v, ws_v, dk_v):
        _, vjp = jax.vjp(_norms_fn, attn_v, yg_v, xs_v, z_v, wa_v, ws_v, dk_v)
        d_attn, d_yg, d_xs, d_z, d_wa, d_ws, d_dk = vjp(dcat_v)
        return [d_attn, d_yg, d_xs, d_z], [d_wa, d_ws, d_dk]
    return _rowwise("norms_bwd", fn, [attn, yg, (xbc, D_SSD), z, dcat], [w_attn, w_ssd, dskip],
                    [(D_ATTN, F32), (N_GROUPS * GROUP_LANES, F32), (D_SSD, F32), (D_SSD, BF16)], [(1, D_SSD)] * 3)


def _ssd_prep_fn(xs, dtp, dtb, alog, e_x, e_a):
    dt = jax.nn.softplus(dtp + dtb)
    a = -jnp.exp(alog)
    dtg = jnp.dot(dt, e_x, precision=HIGHEST, preferred_element_type=F32)
    xdtg = _group(xs) * dtg
    dag = jnp.dot(dt * a, e_a, precision=HIGHEST, preferred_element_type=F32)
    return xdtg, dag


def _ssd_prep_fwd(xbc, dtp, dtb, alog, e_x, e_a):
    def fn(xbc_v, dtp_v, dtb_v, alog_v, ex_v, ea_v):
        xdtg, dag = _ssd_prep_fn(xbc_v[:, :D_SSD], dtp_v, dtb_v, alog_v, ex_v, ea_v)
        return [xdtg, xbc_v[:, D_SSD:], dag], []
    return _rowwise("ssd_prep_fwd", fn, [xbc, dtp], [dtb, alog, e_x, e_a],
                    [(N_GROUPS * GROUP_LANES, BF16), (D_CONV - D_SSD, BF16), (N_GROUPS * LANES, F32)])


def _ssd_prep_bwd(xbc, dtp, dtb, alog, e_x, e_a, dxdtg, ddag, dxs_a, db, dc):
    def fn(xs_v, dtp_v, dxdtg_v, ddag_v, dxs_a_v, db_v, dc_v, dtb_v, alog_v, ex_v, ea_v):
        _, vjp = jax.vjp(lambda a, b, c, d: _ssd_prep_fn(a, b, c, d, ex_v, ea_v), xs_v, dtp_v, dtb_v, alog_v)
        dxs, ddtp, ddtb, dalog = vjp((dxdtg_v, ddag_v))
        return [jnp.concatenate([dxs + dxs_a_v, db_v, dc_v], axis=1), ddtp], [ddtb, dalog]
    return _rowwise("ssd_prep_bwd", fn, [(xbc, D_SSD), dtp, dxdtg, ddag, dxs_a, db, dc], [dtb, alog, e_x, e_a],
                    [(D_CONV, F32), (LANES, BF16)], [(1, LANES)] * 2)


def _shift_down(u, d):
    if d == 0:
        return u
    row = lax.broadcasted_iota(jnp.int32, u.shape, 0)
    return jnp.where(row >= d, pltpu.roll(u, d, 0), 0.0)


def _shift_up(u, d):
    if d == 0:
        return u
    s = u.shape[0]
    row = lax.broadcasted_iota(jnp.int32, u.shape, 0)
    return jnp.where(row < s - d, pltpu.roll(u, s - d, 0), 0.0)


def _conv_pre(u, w, b):
    acc = b
    for k in range(CONV_WIDTH):
        acc = acc + w[k:k + 1, :] * _shift_down(u, CONV_WIDTH - 1 - k)
    return acc


def _conv_fwd(u, w, b, *, tc=256):
    nb, s, c = u.shape

    def body(u_ref, w_ref, b_ref, o_ref):
        pre = _conv_pre(u_ref[0], w_ref[...], b_ref[...])
        o_ref[0] = pre * jax.nn.sigmoid(pre)

    return pl.pallas_call(
        body, name="conv_fwd", grid=(c // tc, nb),
        in_specs=[pl.BlockSpec((1, s, tc), lambda j, i: (i, 0, j)), pl.BlockSpec((CONV_WIDTH, tc), lambda j, i: (0, j)),
                  pl.BlockSpec((1, tc), lambda j, i: (0, j))],
        out_specs=pl.BlockSpec((1, s, tc), lambda j, i: (i, 0, j)),
        out_shape=jax.ShapeDtypeStruct((nb, s, c), F32),
        compiler_params=_params("parallel", "parallel"),
    )(u, w, b)


def _conv_bwd(u, w, b, dout, *, tc=256):
    nb, s, c = u.shape

    def body(u_ref, w_ref, b_ref, d_ref, du_ref, dw_ref, db_ref):
        uv, wv = u_ref[0], w_ref[...]
        pre = _conv_pre(uv, wv, b_ref[...])
        sig = jax.nn.sigmoid(pre)
        dpre = d_ref[0] * (sig * (1.0 + pre * (1.0 - sig)))
        du = jnp.zeros_like(uv)
        dws = []
        for k in range(CONV_WIDTH):
            du = du + wv[k:k + 1, :] * _shift_up(dpre, CONV_WIDTH - 1 - k)
            dws.append(jnp.sum(dpre * _shift_down(uv, CONV_WIDTH - 1 - k), axis=0, keepdims=True))
        du_ref[0] = _bf(du)
        dwv = jnp.concatenate(dws + [jnp.zeros((8 - CONV_WIDTH, tc), F32)], axis=0)
        dbv = jnp.sum(dpre, axis=0, keepdims=True)
        first = pl.program_id(1) == 0

        @pl.when(first)
        def _():
            dw_ref[...] = dwv
            db_ref[...] = dbv

        @pl.when(jnp.logical_not(first))
        def _():
            dw_ref[...] += dwv
            db_ref[...] += dbv

    blk = pl.BlockSpec((1, s, tc), lambda j, i: (i, 0, j))
    return pl.pallas_call(
        body, name="conv_bwd", grid=(c // tc, nb),
        in_specs=[blk, pl.BlockSpec((CONV_WIDTH, tc), lambda j, i: (0, j)), pl.BlockSpec((1, tc), lambda j, i: (0, j)), blk],
        out_specs=[blk, pl.BlockSpec((8, tc), lambda j, i: (0, j)), pl.BlockSpec((1, tc), lambda j, i: (0, j))],
        out_shape=[jax.ShapeDtypeStruct((nb, s, c), BF16), jax.ShapeDtypeStruct((8, c), F32), jax.ShapeDtypeStruct((1, c), F32)],
        compiler_params=_params("parallel", "arbitrary"),
    )(u, w, b, dout)


FWD_KEY_BLOCK = 256


def _branch_bias_table(seq, kb):
    ratio = SEQ_BLOCK // kb
    key = np.arange(kb)[None, :, None]
    query = np.arange(SEQ_BLOCK)[None, None, :]
    delta = (np.arange(seq // kb)[:, None, None] - (ratio - 1)) * kb + query - key
    cnt = np.zeros(delta.shape, np.float64)
    for window, dilation in ((128, 1), (512, 4), (2048, 16)):
        cnt += (delta >= 0) & (delta % dilation == 0) & (delta <= window)
    return jnp.asarray(np.where(cnt > 0, np.log(np.maximum(cnt, 1.0)), NEG).astype(np.float32))


HEADS_PER_BLOCK = LANES // HEAD_DIM


def _head_rows(v, h):
    row = lax.broadcasted_iota(jnp.int32, v.shape, 0)
    return jnp.where((row >= HEAD_DIM * h) & (row < HEAD_DIM * (h + 1)), v, jnp.zeros_like(v))


def _attn_fwd(q, k, v, bias):
    nb_, s, _ = q.shape
    ab, kb = SEQ_BLOCK, FWD_KEY_BLOCK
    nblk, nkb, ratio = s // ab, s // kb, ab // kb

    def body(q_ref, k_ref, v_ref, b_ref, o_ref, lse_ref, vt_s):
        i = pl.program_id(2)

        @pl.when(i == 0)
        def _():
            for jb in range(nkb):
                vt_s[jb] = v_ref[0, kb * jb:kb * (jb + 1), :].T

        qt = q_ref[0].T
        qts = [_head_rows(qt, h) for h in range(HEADS_PER_BLOCK)]

        def step(j, carry):
            ks = pl.ds(pl.multiple_of(j * kb, kb), kb)
            kj = k_ref[0, ks, :]
            lb = b_ref[ratio * i - j + (ratio - 1)]
            out = []
            for h in range(HEADS_PER_BLOCK):
                m, l, acc = carry[3 * h:3 * h + 3]
                st = jnp.dot(kj, qts[h], preferred_element_type=F32) + lb
                m_new = jnp.maximum(m, jnp.max(st, axis=0, keepdims=True))
                p = jnp.exp(st - m_new)
                a = jnp.exp(m - m_new)
                l = a * l + jnp.sum(p, axis=0, keepdims=True)
                vt = vt_s[j, HEAD_DIM * h:HEAD_DIM * (h + 1), :]
                acc = a * acc + jnp.dot(vt, _bf(p), preferred_element_type=F32)
                out += [m_new, l, acc]
            return tuple(out)

        init = (jnp.full((1, ab), NEG, F32), jnp.zeros((1, ab), F32), jnp.zeros((HEAD_DIM, ab), F32)) * HEADS_PER_BLOCK
        res = lax.fori_loop(0, ratio * (i + 1), step, init)
        ot = jnp.concatenate([res[3 * h + 2] / res[3 * h + 1] for h in range(HEADS_PER_BLOCK)], axis=0)
        o_ref[0] = ot.T
        rows = [res[3 * h] + jnp.log(res[3 * h + 1]) for h in range(HEADS_PER_BLOCK)]
        lse_ref[0, 0, 0] = jnp.concatenate(rows + [jnp.zeros((8 - HEADS_PER_BLOCK, ab), F32)], axis=0)

    qblk = pl.BlockSpec((1, ab, LANES), lambda b, hp, i: (b, i, hp))
    full = pl.BlockSpec((1, s, LANES), lambda b, hp, i: (b, 0, hp))
    return pl.pallas_call(
        body, name="attn_fwd", grid=(nb_, D_ATTN // LANES, nblk),
        in_specs=[qblk, full, full, pl.BlockSpec((nkb, kb, ab), lambda b, hp, i: (0, 0, 0))],
        out_specs=[qblk, pl.BlockSpec((1, 1, 1, 8, ab), lambda b, hp, i: (b, hp, i, 0, 0))],
        out_shape=[jax.ShapeDtypeStruct((nb_, s, D_ATTN), F32),
                   jax.ShapeDtypeStruct((nb_, D_ATTN // LANES, nblk, 8, ab), F32)],
        scratch_shapes=[pltpu.VMEM((nkb, LANES, kb), BF16)],
        compiler_params=_params("parallel", "parallel", "arbitrary"),
    )(q, k, v, bias)


def _attn_bwd(q, k, v, o, do, lse, bias):
    nb_, s, _ = q.shape
    ab = SEQ_BLOCK
    nblk = s // ab

    nh = HEADS_PER_BLOCK

    def body(q_ref, k_ref, v_ref, o_ref, do_ref, lse_ref, b_ref, dq_ref, dk_ref, dv_ref,
             qt_s, dot_s, kt_s, dqt_s, do16_s, d_s, dk_acc, dv_acc):
        for jb in range(nblk):
            sl = slice(ab * jb, ab * (jb + 1))
            qt, kt = q_ref[0, sl, :].T, k_ref[0, sl, :].T
            do = do_ref[0, sl, :]
            dot = do.T
            prod = dot * o_ref[0, sl, :].T
            do16_s[sl, :] = _bf(do)
            for h in range(nh):
                qt_s[nh * jb + h] = _head_rows(qt, h)
                kt_s[nh * jb + h] = _head_rows(kt, h)
                dot_s[nh * jb + h] = _head_rows(_bf(dot), h)
            d_s[jb] = jnp.concatenate(
                [jnp.sum(prod[HEAD_DIM * h:HEAD_DIM * (h + 1)], axis=0, keepdims=True) for h in range(nh)]
                + [jnp.zeros((8 - nh, ab), F32)], axis=0)
            dqt_s[jb] = jnp.zeros((LANES, ab), F32)

        def outer(j, carry):
            ks = pl.ds(pl.multiple_of(j * ab, ab), ab)
            kj, vj = k_ref[0, ks, :], v_ref[0, ks, :]
            dk_acc[...] = jnp.zeros_like(dk_acc)
            dv_acc[...] = jnp.zeros_like(dv_acc)

            def inner(i, c2):
                qs = pl.ds(pl.multiple_of(i * ab, ab), ab)
                qi, doi = q_ref[0, qs, :], do16_s[qs, :]
                lb = b_ref[i - j]
                for h in range(nh):
                    st = jnp.dot(kj, qt_s[nh * i + h], preferred_element_type=F32) + lb
                    pt = jnp.exp(st - lse_ref[0, 0, i, h:h + 1, :])
                    dpt = jnp.dot(vj, dot_s[nh * i + h], preferred_element_type=F32)
                    dst16 = _bf(pt * (dpt - d_s[i, h:h + 1, :]))
                    dv_acc[h] += jnp.dot(_bf(pt), doi, preferred_element_type=F32)
                    dk_acc[h] += jnp.dot(dst16, qi, preferred_element_type=F32)
                    dqt_s[i] += jnp.dot(kt_s[nh * j + h], dst16, preferred_element_type=F32)
                return c2

            lax.fori_loop(j, nblk, inner, 0)
            lane = lax.broadcasted_iota(jnp.int32, (ab, LANES), 1)
            dk_ref[0, ks, :] = jnp.where(lane < HEAD_DIM, dk_acc[0], dk_acc[1])
            dv_ref[0, ks, :] = _bf(jnp.where(lane < HEAD_DIM, dv_acc[0], dv_acc[1]))
            return carry

        lax.fori_loop(0, nblk, outer, 0)
        for jb in range(nblk):
            dq_ref[0, ab * jb:ab * (jb + 1), :] = dqt_s[jb].T

    assert nh == 2
    full = pl.BlockSpec((1, s, LANES), lambda b, hp: (b, 0, hp))
    return pl.pallas_call(
        body, name="attn_bwd", grid=(nb_, D_ATTN // LANES),
        in_specs=[full] * 5 + [pl.BlockSpec((1, 1, nblk, 8, ab), lambda b, hp: (b, hp, 0, 0, 0)),
                               pl.BlockSpec((nblk, ab, ab), lambda b, hp: (0, 0, 0))],
        out_specs=[full, full, full],
        out_shape=[jax.ShapeDtypeStruct((nb_, s, D_ATTN), F32), jax.ShapeDtypeStruct((nb_, s, D_ATTN), F32),
                   jax.ShapeDtypeStruct((nb_, s, D_ATTN), BF16)],
        scratch_shapes=[pltpu.VMEM((nh * nblk, LANES, ab), BF16), pltpu.VMEM((nh * nblk, LANES, ab), BF16),
                        pltpu.VMEM((nh * nblk, LANES, ab), BF16), pltpu.VMEM((nblk, LANES, ab), F32),
                        pltpu.VMEM((s, LANES), BF16), pltpu.VMEM((nblk, 8, ab), F32),
                        pltpu.VMEM((nh, ab, LANES), F32), pltpu.VMEM((nh, ab, LANES), F32)],
        compiler_params=_params("parallel", "parallel"),
    )(q, k, v, o, do, lse, bias)


def _cumsum_fwd(dag):
    nb_, s, c = dag.shape
    ab = SEQ_BLOCK

    def body(a_ref, o_ref, ot_ref):
        r = lax.broadcasted_iota(jnp.int32, (ab, ab), 0)
        cc = lax.broadcasted_iota(jnp.int32, (ab, ab), 1)
        tri = (r >= cc).astype(F32)
        carry = jnp.zeros((1, c), F32)
        for i in range(s // ab):
            loc = jnp.dot(tri, a_ref[0, ab * i:ab * (i + 1), :], precision=HIGHEST, preferred_element_type=F32) + carry
            o_ref[0, ab * i:ab * (i + 1), :] = loc
            ot_ref[0, :, ab * i:ab * (i + 1)] = loc.T
            carry = loc[ab - 1:ab, :]

    return pl.pallas_call(
        body, name="ssd_cumsum", grid=(nb_,),
        in_specs=[pl.BlockSpec((1, s, c), lambda b: (b, 0, 0))],
        out_specs=[pl.BlockSpec((1, s, c), lambda b: (b, 0, 0)), pl.BlockSpec((1, c, s), lambda b: (b, 0, 0))],
        out_shape=[jax.ShapeDtypeStruct((nb_, s, c), F32), jax.ShapeDtypeStruct((nb_, c, s), F32)],
        compiler_params=_params("parallel"),
    )(dag)


def _cumsum_bwd(dcol, drow):
    nb_, s, c = dcol.shape
    ab = SEQ_BLOCK

    def body(c_ref, r_ref, o_ref):
        r = lax.broadcasted_iota(jnp.int32, (ab, ab), 0)
        cc = lax.broadcasted_iota(jnp.int32, (ab, ab), 1)
        tri = (r <= cc).astype(F32)
        carry = jnp.zeros((1, c), F32)
        for i in reversed(range(s // ab)):
            rows = r_ref[0, :, ab * i:ab * (i + 1)].T
            parts = []
            for g in range(N_GROUPS):
                parts += [rows[:, 8 * g:8 * (g + 1)], jnp.zeros((ab, LANES - 8), F32)]
            blk = c_ref[0, ab * i:ab * (i + 1), :] + jnp.concatenate(parts, axis=1)
            loc = jnp.dot(tri, blk, precision=HIGHEST, preferred_element_type=F32) + carry
            o_ref[0, ab * i:ab * (i + 1), :] = loc
            carry = loc[0:1, :]

    return pl.pallas_call(
        body, name="ssd_cumsum_bwd", grid=(nb_,),
        in_specs=[pl.BlockSpec((1, s, c), lambda b: (b, 0, 0)), pl.BlockSpec((1, N_GROUPS * 8, s), lambda b: (b, 0, 0))],
        out_specs=pl.BlockSpec((1, s, c), lambda b: (b, 0, 0)),
        out_shape=jax.ShapeDtypeStruct((nb_, s, c), F32),
        compiler_params=_params("parallel"),
    )(dcol, drow)


def _causal_ok(i, j):
    ab = SEQ_BLOCK
    r = lax.broadcasted_iota(jnp.int32, (ab, ab), 0)
    c = lax.broadcasted_iota(jnp.int32, (ab, ab), 1)
    return (r + (i - j) * ab) >= c


def _ssd_fwd(xdtg, bc, acum, acum_t):
    nb_, s, _ = xdtg.shape
    ab = SEQ_BLOCK

    def body(x_ref, b_ref, c_ref, ac_ref, at_ref, y_ref):
        i = pl.program_id(2)
        ci = c_ref[0]
        acol = [ac_ref[0, :, j:j + 1] for j in range(HEADS_PER_GROUP)]

        def step(jb, accs):
            ks = pl.ds(pl.multiple_of(jb * ab, ab), ab)
            cb = lax.dot_general(ci, b_ref[0, ks, :], _NT, preferred_element_type=F32)
            ok = _causal_ok(i, jb)
            new = []
            for j in range(HEADS_PER_GROUP):
                decay = jnp.exp(jnp.where(ok, acol[j] - at_ref[0, j:j + 1, ks], NEG))
                g = _bf(cb * decay)
                new.append(accs[j] + jnp.dot(g, x_ref[0, ks, HEAD_DIM * j:HEAD_DIM * (j + 1)], preferred_element_type=F32))
            return tuple(new)

        accs = lax.fori_loop(0, i + 1, step, tuple(jnp.zeros((ab, HEAD_DIM), F32) for _ in range(HEADS_PER_GROUP)))
        y_ref[0] = jnp.concatenate(list(accs) + [jnp.zeros((ab, GROUP_LANES - HEADS_PER_GROUP * HEAD_DIM), F32)], axis=1)

    return pl.pallas_call(
        body, name="ssd_fwd", grid=(nb_, N_GROUPS, s // ab),
        in_specs=[pl.BlockSpec((1, s, GROUP_LANES), lambda b, g, i: (b, 0, g)),
                  pl.BlockSpec((1, s, D_STATE), lambda b, g, i: (b, 0, g)),
                  pl.BlockSpec((1, ab, D_STATE), lambda b, g, i: (b, i, N_GROUPS + g)),
                  pl.BlockSpec((1, ab, LANES), lambda b, g, i: (b, i, g)),
                  pl.BlockSpec((1, 8, s), lambda b, g, i: (b, (LANES // 8) * g, 0))],
        out_specs=pl.BlockSpec((1, ab, GROUP_LANES), lambda b, g, i: (b, i, g)),
        out_shape=jax.ShapeDtypeStruct((nb_, s, N_GROUPS * GROUP_LANES), F32),
        compiler_params=_params("parallel", "parallel", "parallel"),
    )(xdtg, bc, bc, acum, acum_t)


def _ssd_bwd(xdtg, bc, acum, acum_t, dyg):
    nb_, s, _ = xdtg.shape
    ab = SEQ_BLOCK
    nblk = s // ab
    hpg = HEADS_PER_GROUP

    def body(x_ref, b_ref, c_ref, ac_ref, at_ref, dy_ref, dx_ref, db_ref, dc_ref, dac_ref, dar_ref):
        dx_ref[...] = jnp.zeros_like(dx_ref)
        db_ref[...] = jnp.zeros_like(db_ref)
        dac_ref[...] = jnp.zeros_like(dac_ref)
        dar_ref[...] = jnp.zeros_like(dar_ref)

        def outer(i, carry):
            qs = pl.ds(pl.multiple_of(i * ab, ab), ab)
            ci = c_ref[0, qs, :]
            dyi = [_bf(dy_ref[0, qs, HEAD_DIM * j:HEAD_DIM * (j + 1)]) for j in range(hpg)]
            acol = [ac_ref[0, qs, j:j + 1] for j in range(hpg)]

            def inner(jb, st):
                dc_acc, rs = st[0], list(st[1:])
                ks = pl.ds(pl.multiple_of(jb * ab, ab), ab)
                bj = b_ref[0, ks, :]
                cb = lax.dot_general(ci, bj, _NT, preferred_element_type=F32)
                ok = _causal_ok(i, jb)
                dcb = jnp.zeros((ab, ab), F32)
                for j in range(hpg):
                    hs = slice(HEAD_DIM * j, HEAD_DIM * (j + 1))
                    decay = jnp.exp(jnp.where(ok, acol[j] - at_ref[0, j:j + 1, ks], NEG))
                    g = cb * decay
                    dg = lax.dot_general(dyi[j], x_ref[0, ks, hs], _NT, preferred_element_type=F32)
                    dx_ref[0, ks, hs] += lax.dot_general(_bf(g), dyi[j], _TN, preferred_element_type=F32)
                    dcb = dcb + dg * decay
                    mm = dg * g
                    rs[j] = rs[j] + jnp.sum(mm, axis=1, keepdims=True)
                    dar_ref[0, j:j + 1, ks] -= jnp.sum(mm, axis=0, keepdims=True)
                dcb16 = _bf(dcb)
                db_ref[0, ks, :] += lax.dot_general(dcb16, ci, _TN, preferred_element_type=F32)
                return (dc_acc + jnp.dot(dcb16, bj, preferred_element_type=F32), *rs)

            init = (jnp.zeros((ab, D_STATE), F32),) + tuple(jnp.zeros((ab, 1), F32) for _ in range(hpg))
            st = lax.fori_loop(0, i + 1, inner, init)
            dc_ref[0, qs, :] = st[0]
            for j in range(hpg):
                dac_ref[0, qs, j:j + 1] = st[1 + j]
            return carry

        lax.fori_loop(0, nblk, outer, 0)

    xblk = pl.BlockSpec((1, s, GROUP_LANES), lambda b, g: (b, 0, g))
    sblk = pl.BlockSpec((1, s, D_STATE), lambda b, g: (b, 0, g))
    tblk = pl.BlockSpec((1, 8, s), lambda b, g: (b, (LANES // 8) * g, 0))
    return pl.pallas_call(
        body, name="ssd_bwd", grid=(nb_, N_GROUPS),
        in_specs=[xblk, sblk, pl.BlockSpec((1, s, D_STATE), lambda b, g: (b, 0, N_GROUPS + g)), sblk, tblk, xblk],
        out_specs=[xblk, sblk, sblk, sblk, pl.BlockSpec((1, 8, s), lambda b, g: (b, g, 0))],
        out_shape=[jax.ShapeDtypeStruct((nb_, s, N_GROUPS * GROUP_LANES), F32),
                   jax.ShapeDtypeStruct((nb_, s, N_GROUPS * D_STATE), F32),
                   jax.ShapeDtypeStruct((nb_, s, N_GROUPS * D_STATE), F32),
                   jax.ShapeDtypeStruct((nb_, s, N_GROUPS * LANES), F32),
                   jax.ShapeDtypeStruct((nb_, N_GROUPS * 8, s), F32)],
        compiler_params=_params("parallel", "parallel"),
    )(xdtg, bc, bc, acum, acum_t, dyg)


def _interleave(wg, wu):
    k, f = wg.shape
    gi = GATE_UP_INTERLEAVE
    return jnp.stack([wg.reshape(k, f // gi, gi), wu.reshape(k, f // gi, gi)], axis=2).reshape(k, 2 * f)


def _head_expanders():
    e_x = np.zeros((LANES, N_GROUPS * GROUP_LANES), np.float32)
    e_a = np.zeros((LANES, N_GROUPS * LANES), np.float32)
    for h in range(N_HEADS):
        g, j = divmod(h, HEADS_PER_GROUP)
        e_x[h, GROUP_LANES * g + HEAD_DIM * j:GROUP_LANES * g + HEAD_DIM * (j + 1)] = 1.0
        e_a[h, LANES * g + j] = 1.0
    return jnp.asarray(e_x), jnp.asarray(e_a)


def _pad_lanes(v, n=LANES):
    return jnp.pad(v, ((0, 0), (0, n - v.shape[1])))


def _local_step(x, positions, target, w, late_job=None, late_weights=None, early_grad_job=None):
    nb, s, d = x.shape
    t = nb * s
    x2 = x.reshape(t, d)
    tgt2 = target.reshape(t, d)

    wgu1 = _interleave(w["ffn1_gate"], w["ffn1_up"])
    if late_job is None:
        au1, hm1 = _mm_swiglu("ffn1_up", x2, wgu1)
    else:
        au1, hm1, late = _mm_swiglu("ffn1_up", x2, wgu1, carry=late_job)
        w = {**w, **late_weights(late)}

    wgu2 = _interleave(w["ffn2_gate"], w["ffn2_up"])
    w_in = w["w_in"]
    wqk, wv, wz = w_in[:, :2 * D_ATTN], w_in[:, 2 * D_ATTN:3 * D_ATTN], w_in[:, 3 * D_ATTN:3 * D_ATTN + D_SSD]
    wxbc = w_in[:, 3 * D_ATTN + D_SSD:3 * D_ATTN + D_SSD + D_CONV]
    wdt = _pad_lanes(w_in[:, 3 * D_ATTN + D_SSD + D_CONV:])

    inv_freq = ROPE_THETA ** (-jnp.arange(0, ROPE_DIM, 2, dtype=F32) / ROPE_DIM)
    half = ROPE_DIM // 2
    head_invf = jnp.concatenate([inv_freq, inv_freq, jnp.zeros((HEAD_DIM - ROPE_DIM,), F32)])
    head_sgn = jnp.concatenate([-jnp.ones((half,), F32), jnp.ones((half,), F32), jnp.zeros((HEAD_DIM - ROPE_DIM,), F32)])
    invf = jnp.tile(head_invf, LANES // HEAD_DIM)[None, :]
    sgn = jnp.tile(head_sgn, LANES // HEAD_DIM)[None, :]
    posf = positions.astype(F32).reshape(t, 1)
    bias_fwd, bias_bwd = _branch_bias_table(s, FWD_KEY_BLOCK), _branch_bias_table(s, SEQ_BLOCK)
    e_x, e_a = _head_expanders()
    dtb, alog = _pad_lanes(w["dt_bias"]), _pad_lanes(w["a_log"])
    dskip = jnp.repeat(w["d_skip"], HEAD_DIM, axis=1)

    h1, r1 = _mm_res_ln("ffn1_down_ln1", hm1, w["ffn1_down"], x2, w["ln1_g"], w["ln1_b"], scale=0.5)

    qk = _mm("proj_qk", [(h1, wqk)], tn=768)
    v16 = _mm("proj_v", [(h1, wv)], tn=768, out_dtype=BF16)
    z = _mm("proj_z", [(h1, wz)], tn=768)
    xbc_pre = _mm("proj_xbc", [(h1, wxbc)], tn=896)
    dtp = _mm("proj_dt", [(h1, wdt)], tn=LANES)

    q16, k16, cs = _rope_fwd(qk, posf, invf, sgn)
    to3 = lambda a: a.reshape(nb, s, a.shape[-1])
    attn_o, lse = _attn_fwd(to3(q16), to3(k16), to3(v16), bias_fwd)

    xbc = _conv_fwd(to3(xbc_pre), w["conv_w"], w["conv_b"]).reshape(t, D_CONV)
    xdtg, bc16, dag = _ssd_prep_fwd(xbc, dtp, dtb, alog, e_x, e_a)
    acum, acum_t = _cumsum_fwd(to3(dag))
    yg = _ssd_fwd(to3(xdtg), to3(bc16), acum, acum_t)

    cat = _norms_fwd(attn_o.reshape(t, D_ATTN), yg.reshape(t, -1), xbc, z, w["attn_norm_w"], w["ssd_norm_w"], dskip)
    h2, r2 = _mm_res_ln("w_out_ln2", cat, w["w_out"], h1, w["ln2_g"], w["ln2_b"], scale=1.0)

    au2, hm2 = _mm_swiglu("ffn2_up", h2, wgu2)
    _, r3 = _mm_res_ln("ffn2_down_ln3", hm2, w["ffn2_down"], h2, w["ln3_g"], w["ln3_b"], scale=0.5)

    g = {}
    dr3, g["ln3_g"], g["ln3_b"], loss = _ln_loss_bwd("loss_ln3_bwd", r3, w["ln3_g"], w["ln3_b"], tgt2)

    dau2 = _mm_swiglu_bwd("ffn2_act_bwd", dr3, w["ffn2_down"].T, au2, scale=0.5)
    g["ffn2_down"] = _mm_tn("ffn2_down_dw", hm2, dr3, scale=0.5, tk=D_FF // 2, tn=512)
    g["ffn2_gate"], g["ffn2_up"] = _mm_tn_gate_up("ffn2_up_dw", h2, dau2)
    dh2 = _mm("ffn2_dx", [(dau2, wgu2.T)], res=dr3, res_scale=ALPHA)

    dr2, g["ln2_g"], g["ln2_b"] = _ln_bwd("ln2_bwd", r2, w["ln2_g"], w["ln2_b"], dh2)
    dcat = _mm("w_out_dx", [(dr2, w["w_out"].T)], tn=768)
    g["w_out"] = _mm_tn("w_out_dw", cat, dr2, tk=768, tn=1024)

    d_attn, dyg, dxs_a, dz16, g["attn_norm_w"], g["ssd_norm_w"], ddskip = _norms_bwd(
        attn_o.reshape(t, D_ATTN), yg.reshape(t, -1), xbc, z, w["attn_norm_w"], w["ssd_norm_w"], dskip, dcat)
    g["d_skip"] = ddskip.reshape(N_HEADS, HEAD_DIM).sum(axis=1)[None, :]

    dq, dk, dv16 = _attn_bwd(to3(q16), to3(k16), to3(v16), attn_o, to3(d_attn), lse, bias_bwd)
    dqk16 = _rope_bwd(dq.reshape(t, D_ATTN), dk.reshape(t, D_ATTN), cs)

    dxdtg, dbm, dcm, dacol, darow = _ssd_bwd(to3(xdtg), to3(bc16), acum, acum_t, to3(dyg))
    ddag = _cumsum_bwd(dacol, darow)
    dxbc, ddtp16, ddtb, dalog = _ssd_prep_bwd(xbc, dtp, dtb, alog, e_x, e_a, dxdtg.reshape(t, -1), ddag.reshape(t, -1),
                                               dxs_a, dbm.reshape(t, -1), dcm.reshape(t, -1))
    g["dt_bias"], g["a_log"] = ddtb[:, :N_HEADS], dalog[:, :N_HEADS]
    dxbc_pre16, dconv_w, g["conv_b"] = _conv_bwd(to3(xbc_pre), w["conv_w"], w["conv_b"], to3(dxbc))
    g["conv_w"] = dconv_w[:CONV_WIDTH]
    dxbc_pre16 = dxbc_pre16.reshape(t, D_CONV)
    dv16 = dv16.reshape(t, D_ATTN)

    dh1 = _mm("w_in_dx", [(dqk16, wqk.T), (dv16, wv.T), (dz16, wz.T), (dxbc_pre16, wxbc.T), (ddtp16, wdt.T)],
              res=dr2, res_scale=ALPHA)
    g["w_in"] = jnp.concatenate([
        _mm_tn("w_in_dw_qk", h1, dqk16, tk=1024, tn=512),
        _mm_tn("w_in_dw_v", h1, dv16, tk=1024, tn=768),
        _mm_tn("w_in_dw_z", h1, dz16, tk=1024, tn=768),
        _mm_tn("w_in_dw_xbc", h1, dxbc_pre16, tk=1024, tn=896),
        _mm_tn("w_in_dw_dt", h1, ddtp16, tk=1024, tn=LANES)[:, :N_HEADS],
    ], axis=1)

    dr1, g["ln1_g"], g["ln1_b"] = _ln_bwd("ln1_bwd", r1, w["ln1_g"], w["ln1_b"], dh1)
    if early_grad_job is None:
        dau1, early = _mm_swiglu_bwd("ffn1_act_bwd", dr1, w["ffn1_down"].T, au1, scale=0.5), None
    else:
        dau1, early = _mm_swiglu_bwd("ffn1_act_bwd", dr1, w["ffn1_down"].T, au1, scale=0.5, carry=early_grad_job(g))
    g["ffn1_down"] = _mm_tn("ffn1_down_dw", hm1, dr1, scale=0.5, tk=D_FF // 2, tn=512)
    g["ffn1_gate"], g["ffn1_up"] = _mm_tn_gate_up("ffn1_up_dw", x2, dau1)
    dx = _mm("ffn1_dx", [(dau1, wgu1.T)], res=dr1, res_scale=ALPHA)
    return loss, dx.reshape(nb, s, d), g, early


_HBM = pl.BlockSpec(memory_space=pltpu.HBM)
N_CHIPS = 4
N_DEVICES = 8


def _place():
    return lax.axis_index("x"), lax.axis_index("y"), lax.axis_index("c")


def _other_chips(x, y):
    return [(1 - x, y), (x, 1 - y), (1 - x, 1 - y)]


class _GatherJob:
    def __init__(self, shards):
        assert all((a.shape[0] // 2) % 16 == 0 for a in shards)
        self.n = len(shards)
        self.shapes = [a.shape for a in shards]
        self.operands = [a.reshape(2, a.shape[0] // 2, a.shape[1]) for a in shards]
        self.out_shape = [jax.ShapeDtypeStruct((N_CHIPS,) + a.shape, a.dtype) for a in self.operands]
        pair = pltpu.SemaphoreType.DMA((self.n, N_CHIPS - 1))
        self.scratch_shapes = [pair, pair, pair, pair]

    def results(self, outs):
        return [o.reshape((N_CHIPS,) + s) for o, s in zip(outs, self.shapes)]

    def phases(self, ins, outs, sems):
        n = self.n
        send_sems, recv_sems, fwd_send_sems, fwd_recv_sems = sems
        x, y, c = _place()
        me = 2 * x + y
        peers = _other_chips(x, y)

        def ici(t, p, src_chip):
            px, py = peers[p]
            return pltpu.make_async_remote_copy(
                ins[t].at[c] if src_chip is None else outs[t].at[src_chip, c],
                outs[t].at[me if src_chip is None else src_chip, c],
                send_sems.at[t, p], recv_sems.at[t, p], device_id=(px, py, c), device_id_type=MESH)

        def d2d(t, p, core):
            px, py = peers[p]
            return pltpu.make_async_remote_copy(
                outs[t].at[2 * px + py, core], outs[t].at[2 * px + py, core],
                fwd_send_sems.at[t, p], fwd_recv_sems.at[t, p], device_id=(x, y, 1 - c), device_id_type=MESH)

        pairs = [(t, p) for t in range(n) for p in range(N_CHIPS - 1)]

        def start():
            for t, p in pairs:
                ici(t, p, None).start()

        def forward():
            for t, p in pairs:
                px, py = peers[p]
                ici(t, p, 2 * px + py).wait_recv()
                d2d(t, p, c).start()

        def finish():
            for t, p in pairs:
                d2d(t, p, 1 - c).wait_recv()
            for t, p in pairs:
                ici(t, p, None).wait_send()
                d2d(t, p, c).wait_send()

        return start, forward, finish


class _ExchangeJob:
    def __init__(self, stacks):
        self.n = len(stacks)
        self.operands = list(stacks)
        self.out_shape = [jax.ShapeDtypeStruct(a.shape, a.dtype) for a in stacks]
        pair = pltpu.SemaphoreType.DMA((self.n, N_CHIPS - 1))
        self.scratch_shapes = [pair, pair]

    def results(self, outs):
        return list(outs)

    def phases(self, ins, outs, sems):
        send_sems, recv_sems = sems
        x, y, c = _place()
        me = 2 * x + y
        peers = _other_chips(x, y)
        pairs = [(t, p) for t in range(self.n) for p in range(N_CHIPS - 1)]

        def copy(t, p):
            px, py = peers[p]
            return pltpu.make_async_remote_copy(ins[t].at[2 * px + py], outs[t].at[me], send_sems.at[t, p],
                                                recv_sems.at[t, p], device_id=(px, py, c), device_id_type=MESH)

        def arrival(t, p):
            px, py = peers[p]
            return pltpu.make_async_remote_copy(ins[t].at[me], outs[t].at[2 * px + py], send_sems.at[t, p],
                                                recv_sems.at[t, p], device_id=(px, py, c), device_id_type=MESH)

        def start():
            for t, p in pairs:
                copy(t, p).start()

        def finish():
            for t, p in pairs:
                arrival(t, p).wait_recv()
            for t, p in pairs:
                copy(t, p).wait_send()

        return start, None, finish


def _run_job(job, name):
    n = job.n

    def body(*refs):
        for phase in job.phases(refs[:n], refs[n:2 * n], refs[2 * n:]):
            if phase is not None:
                phase()

    outs = pl.pallas_call(
        body, name=name, in_specs=[_HBM] * n, out_specs=[_HBM] * n,
        out_shape=job.out_shape, scratch_shapes=job.scratch_shapes,
    )(*job.operands)
    return job.results(outs)


def _sibling_halves(stacks, name):
    n = len(stacks)
    halves = [a.shape[1] // 2 for a in stacks]
    split = [a.reshape(a.shape[0], 2, h, a.shape[2]) for a, h in zip(stacks, halves)]

    def body(*refs):
        ins, outs = refs[:n], refs[n:2 * n]
        send_sems, recv_sems = refs[2 * n:]
        x, y, c = _place()
        cps = []
        for t in range(n):
            cp = pltpu.make_async_remote_copy(ins[t].at[:, 1 - c], outs[t], send_sems.at[t], recv_sems.at[t],
                                              device_id=(x, y, 1 - c), device_id_type=MESH)
            cp.start()
            cps.append(cp)
        for cp in cps:
            cp.wait()

    return pl.pallas_call(
        body, name=name,
        in_specs=[_HBM] * n, out_specs=[_HBM] * n,
        out_shape=[jax.ShapeDtypeStruct((a.shape[0], h, a.shape[2]), a.dtype) for a, h in zip(stacks, halves)],
        scratch_shapes=[pltpu.SemaphoreType.DMA((n,)), pltpu.SemaphoreType.DMA((n,))],
    )(*split)


def _sibling_swap(arrs):
    n = len(arrs)

    def body(*refs):
        ins, outs = refs[:n], refs[n:2 * n]
        send_sems, recv_sems = refs[2 * n:]
        x, y, c = _place()
        cps = []
        for t in range(n):
            cp = pltpu.make_async_remote_copy(ins[t], outs[t], send_sems.at[t], recv_sems.at[t],
                                              device_id=(x, y, 1 - c), device_id_type=MESH)
            cp.start()
            cps.append(cp)
        for cp in cps:
            cp.wait()

    return pl.pallas_call(
        body, name="sibling_swap",
        in_specs=[_HBM] * n, out_specs=[_HBM] * n,
        out_shape=[jax.ShapeDtypeStruct(a.shape, a.dtype) for a in arrs],
        scratch_shapes=[pltpu.SemaphoreType.DMA((n,)), pltpu.SemaphoreType.DMA((n,))],
    )(*arrs)


def _half_sum(name, own, other, core):
    k, r, cols = own.shape
    h = r // 2
    tr = next(cand for cand in (128, 176, 64, 32, 16) if h % cand == 0)
    nblk = h // tr

    def body(core_ref, own_ref, other_ref, o_ref):
        o_ref[...] = _bf(own_ref[...] + other_ref[...].astype(F32))

    grid_spec = pltpu.PrefetchScalarGridSpec(
        num_scalar_prefetch=1, grid=(nblk,),
        in_specs=[pl.BlockSpec((k, tr, cols), lambda i, core_ref: (0, i + core_ref[0] * nblk, 0)),
                  pl.BlockSpec((k, tr, cols), lambda i, core_ref: (0, i, 0))],
        out_specs=pl.BlockSpec((k, tr, cols), lambda i, core_ref: (0, i, 0)))
    return pl.pallas_call(
        body, name=name, grid_spec=grid_spec, out_shape=jax.ShapeDtypeStruct((k, h, cols), BF16),
        compiler_params=_params("parallel"),
    )(core.reshape(1).astype(jnp.int32), own, other)


def _small_allreduce(v):
    r = v.shape[0]

    def body(v_ref, tot_ref, slots, send_sems, recv_sems):
        x, y, c = _place()
        me = 4 * x + 2 * y + c
        slots[me] = v_ref[...]
        cps, peers = [], []
        for k in range(1, N_DEVICES):
            px = 1 - x if (k >> 2) & 1 else x
            py = 1 - y if (k >> 1) & 1 else y
            pc = 1 - c if k & 1 else c
            cp = pltpu.make_async_remote_copy(v_ref, slots.at[me], send_sems.at[k - 1], recv_sems.at[k - 1],
                                              device_id=(px, py, pc), device_id_type=MESH)
            cp.start()
            cps.append(cp)
            peers.append((px, py, pc))
        for k, (px, py, pc) in enumerate(peers):
            pltpu.make_async_remote_copy(v_ref, slots.at[4 * px + 2 * py + pc], send_sems.at[k], recv_sems.at[k],
                                         device_id=(px, py, pc), device_id_type=MESH).wait_recv()
        for cp in cps:
            cp.wait_send()
        acc = slots[0]
        for s in range(1, N_DEVICES):
            acc = acc + slots[s]
        tot_ref[...] = acc

    return pl.pallas_call(
        body, name="small_allreduce",
        in_specs=[pl.BlockSpec(memory_space=pltpu.VMEM)], out_specs=pl.BlockSpec(memory_space=pltpu.VMEM),
        out_shape=jax.ShapeDtypeStruct((r, LANES), F32),
        scratch_shapes=[pltpu.VMEM((N_DEVICES, r, LANES), F32), pltpu.SemaphoreType.DMA((N_DEVICES - 1,)),
                        pltpu.SemaphoreType.DMA((N_DEVICES - 1,))],
    )(v)


def _elementwise(name, fn, ins, out_dtypes):
    r, c = ins[0].shape[-2:]
    tr = next((cand for cand in (256, 176, 128, 64, 32, 16) if r % cand == 0), r)
    nin = len(ins)

    def body(*refs):
        outs = fn(*[ref[...] for ref in refs[:nin]])
        for o_ref, o in zip(refs[nin:], outs):
            o_ref[...] = o.astype(o_ref.dtype)

    in_specs = [pl.BlockSpec((tr, c), lambda i: (i, 0)) if a.ndim == 2 else pl.BlockSpec((a.shape[0], tr, c), lambda i: (0, i, 0))
                for a in ins]
    return pl.pallas_call(
        body, name=name, grid=(r // tr,), in_specs=in_specs,
        out_specs=[pl.BlockSpec((tr, c), lambda i: (i, 0)) for _ in out_dtypes],
        out_shape=[jax.ShapeDtypeStruct((r, c), dt) for dt in out_dtypes],
        compiler_params=_params("parallel"),
    )(*ins)


def _row_tile(rows):
    return next((cand for cand in (128, 176, 64, 32, 16) if rows % cand == 0), rows)


def _sum_slots(name, received, own, chip):
    _, r, cols = own.shape
    tr = _row_tile(r)

    def body(chip_ref, own_ref, a_ref, b_ref, c_ref, o_ref):
        o_ref[...] = ((own_ref[0].astype(F32) + a_ref[0].astype(F32)) + b_ref[0].astype(F32)) + c_ref[0].astype(F32)

    def slot(flip):
        return pl.BlockSpec((1, tr, cols), lambda i, chip_ref: (jnp.bitwise_xor(chip_ref[0], flip), i, 0))

    grid_spec = pltpu.PrefetchScalarGridSpec(
        num_scalar_prefetch=1, grid=(r // tr,), in_specs=[slot(0), slot(1), slot(2), slot(3)],
        out_specs=pl.BlockSpec((tr, cols), lambda i, chip_ref: (i, 0)))
    return pl.pallas_call(
        body, name=name, grid_spec=grid_spec, out_shape=jax.ShapeDtypeStruct((r, cols), F32),
        compiler_params=_params("parallel"),
    )(chip.reshape(1).astype(jnp.int32), own, received, received, received)


def _adamw_halves(name, mine, theirs, core, w, m, v):
    h, cols = mine.shape
    tr = _row_tile(h)
    nh = h // tr

    def body(core_ref, mine_ref, theirs_ref, w_ref, m_ref, v_ref, g_ref, d_ref, m2_ref, v2_ref):
        is_mine = (pl.program_id(0) // nh) == core_ref[0]
        g = jnp.where(is_mine, mine_ref[...], theirs_ref[...])
        outs = _adamw_math(g, w_ref[...], m_ref[...], v_ref[...])
        for ref, val in zip((g_ref, d_ref, m2_ref, v2_ref), outs):
            ref[...] = val

    half = pl.BlockSpec((tr, cols), lambda i, core_ref: (i % nh, 0))
    full = pl.BlockSpec((tr, cols), lambda i, core_ref: (i, 0))
    grid_spec = pltpu.PrefetchScalarGridSpec(
        num_scalar_prefetch=1, grid=(2 * nh,), in_specs=[half, half, full, full, full], out_specs=[full] * 4)
    return pl.pallas_call(
        body, name=name, grid_spec=grid_spec, out_shape=[jax.ShapeDtypeStruct((2 * h, cols), F32)] * 4,
        compiler_params=_params("parallel"),
    )(core.reshape(1).astype(jnp.int32), mine, theirs, w, m, v)


def _adamw_math(g, w_v, m_v, v_v):
    m2 = ADAM_B1 * m_v + (1.0 - ADAM_B1) * g
    v2 = ADAM_B2 * v_v + (1.0 - ADAM_B2) * jnp.square(g)
    m_hat = m2 / (1.0 - ADAM_B1 ** ADAM_STEP)
    v_hat = v2 / (1.0 - ADAM_B2 ** ADAM_STEP)
    delta = -ADAM_LR * (m_hat / (jnp.sqrt(v_hat) + ADAM_EPS) + ADAM_WD * w_v)
    return [g, delta, m2, v2]


def _adamw(name, g, w, m, v):
    return _elementwise(name, _adamw_math, [g, w, m, v], [F32] * 4)


_MATRICES = (("ffn1_gate", 1), ("ffn1_up", 1), ("ffn1_down", 0), ("w_in", 1), ("w_out", 0),
             ("ffn2_gate", 1), ("ffn2_up", 1), ("ffn2_down", 0))
_VECTORS = ("ln1_g", "ln1_b", "conv_b", "dt_bias", "a_log", "d_skip", "attn_norm_w", "ssd_norm_w",
            "ln2_g", "ln2_b", "ln3_g", "ln3_b")
_WEIGHT_ORDER = ("ln1_g", "ln1_b", "ffn1_gate", "ffn1_up", "ffn1_down", "w_in", "conv_w", "conv_b", "dt_bias", "a_log",
                 "d_skip", "attn_norm_w", "ssd_norm_w", "w_out", "ln2_g", "ln2_b", "ffn2_gate", "ffn2_up", "ffn2_down",
                 "ln3_g", "ln3_b")


def _pack_rows(vectors):
    parts = []
    for vec in vectors:
        flat = vec.reshape(-1)
        parts.append(jnp.pad(flat, (0, (-flat.shape[0]) % LANES)))
    flat = jnp.concatenate(parts)
    flat = jnp.pad(flat, (0, (-flat.shape[0]) % (8 * LANES)))
    return flat.reshape(-1, LANES)


def _unpack_rows(packed, shapes):
    flat = packed.reshape(-1)
    out, off = [], 0
    for shape in shapes:
        size = int(np.prod(shape))
        out.append(flat[off:off + size].reshape(shape))
        off += size + (-size) % LANES
    return out


def _assemble(stack, own, chip, axis):
    blocks = [jnp.where(chip == s, own, stack[s]) for s in range(N_CHIPS)]
    return jnp.concatenate(blocks, axis=axis)


def _split(full, axis):
    if axis == 0:
        return full.reshape(N_CHIPS, -1, full.shape[1])
    cols = full.shape[1] // N_CHIPS
    return jnp.stack([full[:, cols * s:cols * (s + 1)] for s in range(N_CHIPS)])


def kernel(x, positions, ln1_g, ln1_b, ffn1_gate, ffn1_up, ffn1_down, w_in, conv_w, conv_b, dt_bias, a_log, d_skip, attn_norm_w, ssd_norm_w, w_out, ln2_g, ln2_b, ffn2_gate, ffn2_up, ffn2_down, ln3_g, ln3_b, loss_target, m_ln1_g, m_ln1_b, m_ffn1_gate, m_ffn1_up, m_ffn1_down, m_w_in, m_conv_w, m_conv_b, m_dt_bias, m_a_log, m_d_skip, m_attn_norm_w, m_ssd_norm_w, m_w_out, m_ln2_g, m_ln2_b, m_ffn2_gate, m_ffn2_up, m_ffn2_down, m_ln3_g, m_ln3_b, v_ln1_g, v_ln1_b, v_ffn1_gate, v_ffn1_up, v_ffn1_down, v_w_in, v_conv_w, v_conv_b, v_dt_bias, v_a_log, v_d_skip, v_attn_norm_w, v_ssd_norm_w, v_w_out, v_ln2_g, v_ln2_b, v_ffn2_gate, v_ffn2_up, v_ffn2_down, v_ln3_g, v_ln3_b):
    given = dict(locals())
    wts = {n: given[n] for n in _WEIGHT_ORDER}
    mom_m = {n: given["m_" + n] for n in _WEIGHT_ORDER}
    mom_v = {n: given["v_" + n] for n in _WEIGHT_ORDER}
    chip = 2 * lax.axis_index("x") + lax.axis_index("y")

    core = lax.axis_index("c")
    first = [(n, axis) for n, axis in _MATRICES if n.startswith("ffn1")]
    rest = [(n, axis) for n, axis in _MATRICES if not n.startswith("ffn1")]
    own16 = {n: wts[n][0].astype(BF16) for n, _ in _MATRICES}
    gathered = _run_job(_GatherJob([own16[n] for n, _ in first]), "gather_ffn1")
    full = {n: _assemble(st, own16[n], chip, axis) for (n, axis), st in zip(first, gathered)}
    for n in _VECTORS:
        full[n] = wts[n]
    conv_rows = jnp.pad(wts["conv_w"][0], ((0, 32 - CONV_WIDTH), (0, 0)))
    late_job = _GatherJob([own16[n] for n, _ in rest] + [conv_rows])

    def late_weights(results):
        out = {n: _assemble(st, own16[n], chip, axis) for (n, axis), st in zip(rest, results)}
        out["conv_w"] = _assemble(results[-1], conv_rows, chip, 1)[:CONV_WIDTH]
        return out

    chip_sums = {}

    def core_sums(g, which, tag):
        partials = [_split(g[n], axis) for n, axis in which]
        from_sibling = _sibling_halves([p.astype(BF16) for p in partials], "sibling_halves_" + tag)
        for (n, _), p, o in zip(which, partials, from_sibling):
            chip_sums[n] = _half_sum("core_sum_" + n, p, o, core)
        return _ExchangeJob([chip_sums[n] for n, _ in which])

    loss, grad_x, g, received_rest = _local_step(x, positions, loss_target, full, late_job, late_weights,
                                                 lambda g_now: core_sums(g_now, rest, "rest"))
    received_first = _run_job(core_sums(g, first, "ffn1"), "exchange_ffn1")
    received = dict(zip([n for n, _ in first + rest], received_first + received_rest))
    half_totals = [_sum_slots("sum_partials_" + n, received[n], chip_sums[n], chip) for n, _ in _MATRICES]
    other_halves = _sibling_swap(half_totals)

    small_shapes = [g[n].shape for n in _VECTORS] + [g["conv_w"].shape, (1,)]
    total = _small_allreduce(_pack_rows([g[n] for n in _VECTORS] + [g["conv_w"], loss[0, :1]]))
    small = _unpack_rows(total, small_shapes)
    loss_out = small[-1].reshape(())

    grads, deltas, new_m, new_v = {}, {}, {}, {}
    for (n, _), mine, theirs in zip(_MATRICES, half_totals, other_halves):
        res = _adamw_halves("adamw_" + n, mine, theirs, core, wts[n][0], mom_m[n][0], mom_v[n][0])
        grads[n], deltas[n], new_m[n], new_v[n] = [r[None] for r in res]

    vec_shapes = [wts[n].shape for n in _VECTORS]
    res = _adamw("adamw_vectors", _pack_rows(small[:len(_VECTORS)]), _pack_rows([wts[n] for n in _VECTORS]),
                 _pack_rows([mom_m[n] for n in _VECTORS]), _pack_rows([mom_v[n] for n in _VECTORS]))
    for dst, packed in zip((grads, deltas, new_m, new_v), res):
        for n, val in zip(_VECTORS, _unpack_rows(packed, vec_shapes)):
            dst[n] = val

    cols = conv_w.shape[2]
    g_conv = lax.dynamic_slice_in_dim(small[len(_VECTORS)], chip * cols, cols, axis=1)
    res = _adamw("adamw_conv_w", g_conv, wts["conv_w"][0], mom_m["conv_w"][0], mom_v["conv_w"][0])
    grads["conv_w"], deltas["conv_w"], new_m["conv_w"], new_v["conv_w"] = [r[None] for r in res]

    return (loss_out, grad_x, *[grads[n] for n in _WEIGHT_ORDER], *[deltas[n] for n in _WEIGHT_ORDER],
            *[new_m[n] for n in _WEIGHT_ORDER], *[new_v[n] for n in _WEIGHT_ORDER])
```

```python
import functools

import numpy as np
import jax
import jax.numpy as jnp
from jax import lax
from jax.experimental import pallas as pl
from jax.experimental.pallas import tpu as pltpu

F32, BF16 = jnp.float32, jnp.bfloat16

D_MODEL = 1024
D_FF = 2816
N_HEADS = 12
HEAD_DIM = 64
D_ATTN = 768
D_SSD = 768
N_GROUPS = 4
HEADS_PER_GROUP = 3
D_STATE = 128
D_CONV = 1792
CONV_WIDTH = 4
ROPE_DIM = 16
ROPE_THETA = 500000.0
ALPHA = 2.0 ** 0.25
LN_EPS = 1e-5
RMS_EPS = 1e-6
ADAM_LR, ADAM_B1, ADAM_B2, ADAM_EPS, ADAM_WD, ADAM_STEP = 0.001, 0.9, 0.999, 1e-08, 0.01, 10

LANES = 128
GATE_UP_INTERLEAVE = 256
SEQ_BLOCK = 256
GROUP_LANES = 256
VMEM_LIMIT = 56 * 1024 * 1024
NEG = -1e30
MESH = pl.DeviceIdType.MESH
HIGHEST = lax.Precision.HIGHEST

_NT = (((1,), (1,)), ((), ()))
_TN = (((0,), (0,)), ((), ()))


def _params(*sem):
    return pltpu.CompilerParams(dimension_semantics=sem, vmem_limit_bytes=VMEM_LIMIT)


def _bf(v):
    return v.astype(BF16)


EPILOGUE_ROWS = 128


def _row_chunks(tm):
    return [slice(r, min(r + EPILOGUE_ROWS, tm)) for r in range(0, tm, EPILOGUE_ROWS)]


def _sigmoid(v):
    return 0.5 * jnp.tanh(0.5 * v) + 0.5


def _mm(name, pairs, *, scale=1.0, res=None, res_scale=1.0, out_dtype=F32, tm=512, tn=512):
    m, n = pairs[0][0].shape[0], pairs[0][1].shape[1]
    tm, tn = min(tm, m), min(tn, n)
    assert m % tm == 0 and n % tn == 0, (name, m, n, tm, tn)
    npair = len(pairs)

    def body(*refs):
        acc = None
        for a_ref, b_ref in zip(refs[:npair], refs[npair:2 * npair]):
            d = jnp.dot(_bf(a_ref[...]), b_ref[...], preferred_element_type=F32)
            acc = d if acc is None else acc + d
        if scale != 1.0:
            acc = acc * scale
        if res is not None:
            acc = acc + res_scale * refs[2 * npair][...]
        refs[-1][...] = acc.astype(out_dtype)

    in_specs = [pl.BlockSpec((tm, a.shape[1]), lambda i, j: (i, 0)) for a, _ in pairs]
    in_specs += [pl.BlockSpec((b.shape[0], tn), lambda i, j: (0, j)) for _, b in pairs]
    args = [a for a, _ in pairs] + [b for _, b in pairs]
    if res is not None:
        in_specs.append(pl.BlockSpec((tm, tn), lambda i, j: (i, j)))
        args.append(res)
    return pl.pallas_call(
        body, name=name, grid=(m // tm, n // tn), in_specs=in_specs,
        out_specs=pl.BlockSpec((tm, tn), lambda i, j: (i, j)),
        out_shape=jax.ShapeDtypeStruct((m, n), out_dtype),
        compiler_params=_params("parallel", "parallel"),
    )(*args)


def _mm_tn(name, x, dy, *, scale=1.0, tk=512, tn=512, tt=1024):
    t, k = x.shape
    n = dy.shape[1]
    tk, tn, tt = min(tk, k), min(tn, n), min(tt, t)
    assert k % tk == 0 and n % tn == 0 and t % tt == 0, (name, k, n, t)
    nt = t // tt

    def body(x_ref, dy_ref, o_ref):
        step = pl.program_id(2)
        d = lax.dot_general(_bf(x_ref[...]), _bf(dy_ref[...]), _TN, preferred_element_type=F32)

        @pl.when(step == 0)
        def _():
            o_ref[...] = d

        @pl.when(step > 0)
        def _():
            o_ref[...] += d

        if scale != 1.0:
            @pl.when(step == nt - 1)
            def _():
                o_ref[...] = o_ref[...] * scale

    return pl.pallas_call(
        body, name=name, grid=(k // tk, n // tn, nt),
        in_specs=[pl.BlockSpec((tt, tk), lambda i, j, s: (s, i)), pl.BlockSpec((tt, tn), lambda i, j, s: (s, j))],
        out_specs=pl.BlockSpec((tk, tn), lambda i, j, s: (i, j)),
        out_shape=jax.ShapeDtypeStruct((k, n), F32),
        compiler_params=_params("parallel", "parallel", "arbitrary"),
    )(x, dy)


def _mm_tn_gate_up(name, x, dau, *, tt=1024):
    t, k = x.shape
    gi = GATE_UP_INTERLEAVE
    nj = dau.shape[1] // (2 * gi)
    tt = min(tt, t)
    nt = t // tt

    def body(x_ref, dy_ref, g_ref, u_ref):
        step = pl.program_id(1)
        d = lax.dot_general(_bf(x_ref[...]), dy_ref[...], _TN, preferred_element_type=F32)

        @pl.when(step == 0)
        def _():
            g_ref[...] = d[:, :gi]
            u_ref[...] = d[:, gi:]

        @pl.when(step > 0)
        def _():
            g_ref[...] += d[:, :gi]
            u_ref[...] += d[:, gi:]

    out = pl.BlockSpec((k, gi), lambda j, s: (0, j))
    return pl.pallas_call(
        body, name=name, grid=(nj, nt),
        in_specs=[pl.BlockSpec((tt, k), lambda j, s: (s, 0)), pl.BlockSpec((tt, 2 * gi), lambda j, s: (s, j))],
        out_specs=[out, out],
        out_shape=[jax.ShapeDtypeStruct((k, gi * nj), F32)] * 2,
        compiler_params=_params("parallel", "arbitrary"),
    )(x, dau)


def _carried(carry, ins, outs, sems, step, total):
    start, forward, finish = carry.phases(ins, outs, sems)
    pl.when(step == 0)(start)
    if forward is not None:
        pl.when(step == (3 * total) // 4)(forward)
    return lambda: pl.when(step == total - 1)(finish)


def _mm_swiglu(name, x, wgu, *, tm=512, carry=None):
    t, k = x.shape
    gi = GATE_UP_INTERLEAVE
    ni, nj = t // tm, wgu.shape[1] // (2 * gi)
    nc = carry.n if carry is not None else 0

    def body(*refs):
        x_ref, w_ref = refs[:2]
        au_ref, hm_ref = refs[2 + nc:4 + nc]
        if carry is not None:
            step = pl.program_id(0) * nj + pl.program_id(1)
            finish = _carried(carry, refs[2:2 + nc], refs[4 + nc:4 + 2 * nc], refs[4 + 2 * nc:], step, ni * nj)
        for rows in _row_chunks(tm):
            au = jnp.dot(_bf(x_ref[rows, :]), w_ref[...], preferred_element_type=F32)
            a, u = au[:, :gi], au[:, gi:]
            au_ref[rows, :] = _bf(au)
            hm_ref[rows, :] = _bf(a * _sigmoid(a) * u)
        if carry is not None:
            finish()

    hbm = pl.BlockSpec(memory_space=pltpu.HBM)
    res = pl.pallas_call(
        body, name=name, grid=(ni, nj),
        in_specs=[pl.BlockSpec((tm, k), lambda i, j: (i, 0)), pl.BlockSpec((k, 2 * gi), lambda i, j: (0, j))] + [hbm] * nc,
        out_specs=[pl.BlockSpec((tm, 2 * gi), lambda i, j: (i, j)), pl.BlockSpec((tm, gi), lambda i, j: (i, j))] + [hbm] * nc,
        out_shape=[jax.ShapeDtypeStruct((t, 2 * gi * nj), BF16), jax.ShapeDtypeStruct((t, gi * nj), BF16)]
        + (carry.out_shape if carry is not None else []),
        scratch_shapes=carry.scratch_shapes if carry is not None else [],
        compiler_params=_params(*(("arbitrary", "arbitrary") if carry is not None else ("parallel", "parallel"))),
    )(x, wgu, *(carry.operands if carry is not None else []))
    return res if carry is None else (res[0], res[1], carry.results(res[2:]))


def _mm_swiglu_bwd(name, dr, wdt, au, *, scale, tm=512, carry=None):
    t, k = dr.shape
    gi = GATE_UP_INTERLEAVE
    ni, nj = t // tm, wdt.shape[1] // gi
    nc = carry.n if carry is not None else 0

    def body(*refs):
        dr_ref, w_ref, au_ref = refs[:3]
        o_ref = refs[3 + nc]
        if carry is not None:
            step = pl.program_id(0) * nj + pl.program_id(1)
            finish = _carried(carry, refs[3:3 + nc], refs[4 + nc:4 + 2 * nc], refs[4 + 2 * nc:], step, ni * nj)
        for rows in _row_chunks(tm):
            dhm = jnp.dot(_bf(dr_ref[rows, :]), w_ref[...], preferred_element_type=F32) * scale
            au_v = au_ref[rows, :].astype(F32)
            a, u = au_v[:, :gi], au_v[:, gi:]
            sig = _sigmoid(a)
            silu = a * sig
            o_ref[rows, :gi] = _bf(dhm * u * (sig + silu - silu * sig))
            o_ref[rows, gi:] = _bf(dhm * silu)
        if carry is not None:
            finish()

    hbm = pl.BlockSpec(memory_space=pltpu.HBM)
    res = pl.pallas_call(
        body, name=name, grid=(ni, nj),
        in_specs=[pl.BlockSpec((tm, k), lambda i, j: (i, 0)), pl.BlockSpec((k, gi), lambda i, j: (0, j)),
                  pl.BlockSpec((tm, 2 * gi), lambda i, j: (i, j))] + [hbm] * nc,
        out_specs=[pl.BlockSpec((tm, 2 * gi), lambda i, j: (i, j))] + [hbm] * nc,
        out_shape=[jax.ShapeDtypeStruct((t, 2 * gi * nj), BF16)] + (carry.out_shape if carry is not None else []),
        scratch_shapes=carry.scratch_shapes if carry is not None else [],
        compiler_params=_params(*(("arbitrary", "arbitrary") if carry is not None else ("parallel", "parallel"))),
    )(dr, wdt, au, *(carry.operands if carry is not None else []))
    return res[0] if carry is None else (res[0], carry.results(res[1:]))


def _layer_norm(r, g, b):
    mu = jnp.mean(r, axis=-1, keepdims=True)
    var = jnp.mean(jnp.square(r - mu), axis=-1, keepdims=True)
    return (r - mu) * lax.rsqrt(var + LN_EPS) * g + b


def _mm_res_ln(name, a, w, res, g, b, *, scale, tm=256):
    t, k = a.shape
    n = w.shape[1]

    def body(a_ref, w_ref, res_ref, g_ref, b_ref, y_ref, r_ref, y16_ref):
        for rows in _row_chunks(tm):
            r = ALPHA * res_ref[rows, :] + scale * jnp.dot(_bf(a_ref[rows, :]), w_ref[...], preferred_element_type=F32)
            r_ref[rows, :] = r
            y = _layer_norm(r, g_ref[...], b_ref[...])
            y_ref[rows, :] = y
            y16_ref[rows, :] = _bf(y)

    row = lambda c: pl.BlockSpec((tm, c), lambda i: (i, 0))
    const = lambda shape: pl.BlockSpec(shape, lambda i: (0, 0))
    return pl.pallas_call(
        body, name=name, grid=(t // tm,),
        in_specs=[row(k), const((k, n)), row(n), const((1, n)), const((1, n))],
        out_specs=[row(n), row(n), row(n)],
        out_shape=[jax.ShapeDtypeStruct((t, n), F32), jax.ShapeDtypeStruct((t, n), F32), jax.ShapeDtypeStruct((t, n), BF16)],
        compiler_params=_params("parallel"),
    )(a, w, res, g, b)


def _rowwise(name, fn, rows, consts, row_outs, acc_outs=(), tm=256):
    rows = [r if isinstance(r, tuple) else (r, r.shape[1]) for r in rows]
    t = rows[0][0].shape[0]
    tm = min(tm, t)
    assert t % tm == 0
    nr, nc, no, na = len(rows), len(consts), len(row_outs), len(acc_outs)

    def body(*refs):
        vals = [r[...] for r in refs[:nr + nc]]
        outs, accs = fn(*vals)
        for o_ref, o in zip(refs[nr + nc:nr + nc + no], outs):
            o_ref[...] = o.astype(o_ref.dtype)
        if na:
            step = pl.program_id(0)
            acc_refs = refs[nr + nc + no:]

            @pl.when(step == 0)
            def _():
                for a_ref, a in zip(acc_refs, accs):
                    a_ref[...] = a

            @pl.when(step > 0)
            def _():
                for a_ref, a in zip(acc_refs, accs):
                    a_ref[...] += a

    in_specs = [pl.BlockSpec((tm, w), lambda i: (i, 0)) for _, w in rows]
    in_specs += [pl.BlockSpec(c.shape, lambda i, nd=c.ndim: (0,) * nd) for c in consts]
    out_specs = [pl.BlockSpec((tm, c), lambda i: (i, 0)) for c, _ in row_outs]
    out_specs += [pl.BlockSpec(s, lambda i: (0, 0)) for s in acc_outs]
    out_shape = [jax.ShapeDtypeStruct((t, c), dt) for c, dt in row_outs]
    out_shape += [jax.ShapeDtypeStruct(s, F32) for s in acc_outs]
    res = pl.pallas_call(
        body, name=name, grid=(t // tm,), in_specs=in_specs, out_specs=out_specs, out_shape=out_shape,
        compiler_params=_params("arbitrary" if na else "parallel"),
    )(*[r for r, _ in rows], *consts)
    return res


def _ln_bwd(name, r, g, b, dy):
    def fn(r_v, dy_v, g_v, b_v):
        _, vjp = jax.vjp(_layer_norm, r_v, g_v, b_v)
        dr, dg, db = vjp(dy_v)
        return [dr, dr], [dg, db]
    return _rowwise(name, fn, [r, dy], [g, b], [(r.shape[1], F32), (r.shape[1], BF16)], [(1, r.shape[1])] * 2)


def _ln_loss_bwd(name, r, g, b, target):
    def fn(r_v, t_v, g_v, b_v):
        def loss_fn(rr, gg, bb):
            err = jnp.square(_layer_norm(rr, gg, bb) - t_v)
            return 0.5 * jnp.sum(jnp.mean(err, axis=-1, keepdims=True), axis=0, keepdims=True)
        loss, vjp = jax.vjp(loss_fn, r_v, g_v, b_v)
        dr, dg, db = vjp(jnp.ones((1, 1), F32))
        return [dr, dr], [dg, db, jnp.broadcast_to(loss, (1, LANES))]
    return _rowwise(name, fn, [r, target], [g, b], [(r.shape[1], F32), (r.shape[1], BF16)],
                    [(1, r.shape[1])] * 2 + [(1, LANES)])


def _rope_tables(posf, invf, sgn):
    ang = posf * invf
    return jnp.cos(ang), jnp.sin(ang) * sgn


def _rope_apply(tv, cos, sin):
    lane = lax.broadcasted_iota(jnp.int32, cos.shape, 1)
    first = (lane % HEAD_DIM) < (ROPE_DIM // 2)
    outs = []
    for gidx in range(tv.shape[1] // LANES):
        tg = tv[:, LANES * gidx:LANES * (gidx + 1)]
        sw = jnp.where(first, pltpu.roll(tg, LANES - ROPE_DIM // 2, 1), pltpu.roll(tg, ROPE_DIM // 2, 1))
        outs.append(tg * cos + sw * sin)
    return jnp.concatenate(outs, axis=1)


def _rope_fwd(qk, posf, invf, sgn):
    def fn(qk_v, pos_v, invf_v, sgn_v):
        cos, sin = _rope_tables(pos_v, invf_v, sgn_v)
        q = _rope_apply(qk_v[:, :D_ATTN], cos, sin) * (HEAD_DIM ** -0.5)
        k = _rope_apply(qk_v[:, D_ATTN:], cos, sin)
        return [q, k, jnp.concatenate([cos, sin], axis=1)], []
    return _rowwise("rope_fwd", fn, [qk, posf], [invf, sgn], [(D_ATTN, BF16), (D_ATTN, BF16), (2 * LANES, F32)])


def _rope_bwd(dq, dk, cs):
    def fn(dq_v, dk_v, cs_v):
        cos, sin = cs_v[:, :LANES], -cs_v[:, LANES:]
        gq = _rope_apply(dq_v * (HEAD_DIM ** -0.5), cos, sin)
        gk = _rope_apply(dk_v, cos, sin)
        return [jnp.concatenate([gq, gk], axis=1)], []
    return _rowwise("rope_bwd", fn, [dq, dk, cs], [], [(2 * D_ATTN, BF16)])[0]


def _rms(v, w):
    return v * lax.rsqrt(jnp.mean(v * v, axis=-1, keepdims=True) + RMS_EPS) * w


def _ungroup(yg):
    w = HEADS_PER_GROUP * HEAD_DIM
    return jnp.concatenate([yg[:, GROUP_LANES * g:GROUP_LANES * g + w] for g in range(N_GROUPS)], axis=1)


def _group(xs):
    w = HEADS_PER_GROUP * HEAD_DIM
    parts = []
    for g in range(N_GROUPS):
        parts += [xs[:, w * g:w * (g + 1)], jnp.zeros((xs.shape[0], GROUP_LANES - w), xs.dtype)]
    return jnp.concatenate(parts, axis=1)


def _norms_fn(attn, yg, xs, z, w_attn, w_ssd, dskip):
    a_n = _rms(attn, w_attn)
    y = _ungroup(yg) + dskip * xs
    y_n = _rms(y * (z * jax.nn.sigmoid(z)), w_ssd)
    return jnp.concatenate([a_n, y_n], axis=1)


def _norms_fwd(attn, yg, xbc, z, w_attn, w_ssd, dskip):
    def fn(*v):
        return [_norms_fn(*v)], []
    return _rowwise("norms_fwd", fn, [attn, yg, (xbc, D_SSD), z], [w_attn, w_ssd, dskip], [(D_ATTN + D_SSD, BF16)])[0]


def _norms_bwd(attn, yg, xbc, z, w_attn, w_ssd, dskip, dcat):
    def fn(attn_v, yg_v, xs_v, z_v, dcat_v, wa_v, ws_v, dk_v):
        _, vjp = jax.vjp(_norms_fn, attn_v, yg_v, xs_v, z_v, wa_v, ws_v, dk_v)
        d_attn, d_yg, d_xs, d_z, d_wa, d_ws, d_dk = vjp(dcat_v)
        return [d_attn, d_yg, d_xs, d_z], [d_wa, d_ws, d_dk]
    return _rowwise("norms_bwd", fn, [attn, yg, (xbc, D_SSD), z, dcat], [w_attn, w_ssd, dskip],
                    [(D_ATTN, F32), (N_GROUPS * GROUP_LANES, F32), (D_SSD, F32), (D_SSD, BF16)], [(1, D_SSD)] * 3)


def _ssd_prep_fn(xs, dtp, dtb, alog, e_x, e_a):
    dt = jax.nn.softplus(dtp + dtb)
    a = -jnp.exp(alog)
    dtg = jnp.dot(dt, e_x, precision=HIGHEST, preferred_element_type=F32)
    xdtg = _group(xs) * dtg
    dag = jnp.dot(dt * a, e_a, precision=HIGHEST, preferred_element_type=F32)
    return xdtg, dag


def _ssd_prep_fwd(xbc, dtp, dtb, alog, e_x, e_a):
    def fn(xbc_v, dtp_v, dtb_v, alog_v, ex_v, ea_v):
        xdtg, dag = _ssd_prep_fn(xbc_v[:, :D_SSD], dtp_v, dtb_v, alog_v, ex_v, ea_v)
        return [xdtg, xbc_v[:, D_SSD:], dag], []
    return _rowwise("ssd_prep_fwd", fn, [xbc, dtp], [dtb, alog, e_x, e_a],
                    [(N_GROUPS * GROUP_LANES, BF16), (D_CONV - D_SSD, BF16), (N_GROUPS * LANES, F32)])


def _ssd_prep_bwd(xbc, dtp, dtb, alog, e_x, e_a, dxdtg, ddag, dxs_a, db, dc):
    def fn(xs_v, dtp_v, dxdtg_v, ddag_v, dxs_a_v, db_v, dc_v, dtb_v, alog_v, ex_v, ea_v):
        _, vjp = jax.vjp(lambda a, b, c, d: _ssd_prep_fn(a, b, c, d, ex_v, ea_v), xs_v, dtp_v, dtb_v, alog_v)
        dxs, ddtp, ddtb, dalog = vjp((dxdtg_v, ddag_v))
        return [jnp.concatenate([dxs + dxs_a_v, db_v, dc_v], axis=1), ddtp], [ddtb, dalog]
    return _rowwise("ssd_prep_bwd", fn, [(xbc, D_SSD), dtp, dxdtg, ddag, dxs_a, db, dc], [dtb, alog, e_x, e_a],
                    [(D_CONV, F32), (LANES, BF16)], [(1, LANES)] * 2)


def _shift_down(u, d):
    if d == 0:
        return u
    row = lax.broadcasted_iota(jnp.int32, u.shape, 0)
    return jnp.where(row >= d, pltpu.roll(u, d, 0), 0.0)


def _shift_up(u, d):
    if d == 0:
        return u
    s = u.shape[0]
    row = lax.broadcasted_iota(jnp.int32, u.shape, 0)
    return jnp.where(row < s - d, pltpu.roll(u, s - d, 0), 0.0)


def _conv_pre(u, w, b):
    acc = b
    for k in range(CONV_WIDTH):
        acc = acc + w[k:k + 1, :] * _shift_down(u, CONV_WIDTH - 1 - k)
    return acc


def _conv_fwd(u, w, b, *, tc=256):
    nb, s, c = u.shape

    def body(u_ref, w_ref, b_ref, o_ref):
        pre = _conv_pre(u_ref[0], w_ref[...], b_ref[...])
        o_ref[0] = pre * jax.nn.sigmoid(pre)

    return pl.pallas_call(
        body, name="conv_fwd", grid=(c // tc, nb),
        in_specs=[pl.BlockSpec((1, s, tc), lambda j, i: (i, 0, j)), pl.BlockSpec((CONV_WIDTH, tc), lambda j, i: (0, j)),
                  pl.BlockSpec((1, tc), lambda j, i: (0, j))],
        out_specs=pl.BlockSpec((1, s, tc), lambda j, i: (i, 0, j)),
        out_shape=jax.ShapeDtypeStruct((nb, s, c), F32),
        compiler_params=_params("parallel", "parallel"),
    )(u, w, b)


def _conv_bwd(u, w, b, dout, *, tc=256):
    nb, s, c = u.shape

    def body(u_ref, w_ref, b_ref, d_ref, du_ref, dw_ref, db_ref):
        uv, wv = u_ref[0], w_ref[...]
        pre = _conv_pre(uv, wv, b_ref[...])
        sig = jax.nn.sigmoid(pre)
        dpre = d_ref[0] * (sig * (1.0 + pre * (1.0 - sig)))
        du = jnp.zeros_like(uv)
        dws = []
        for k in range(CONV_WIDTH):
            du = du + wv[k:k + 1, :] * _shift_up(dpre, CONV_WIDTH - 1 - k)
            dws.append(jnp.sum(dpre * _shift_down(uv, CONV_WIDTH - 1 - k), axis=0, keepdims=True))
        du_ref[0] = _bf(du)
        dwv = jnp.concatenate(dws + [jnp.zeros((8 - CONV_WIDTH, tc), F32)], axis=0)
        dbv = jnp.sum(dpre, axis=0, keepdims=True)
        first = pl.program_id(1) == 0

        @pl.when(first)
        def _():
            dw_ref[...] = dwv
            db_ref[...] = dbv

        @pl.when(jnp.logical_not(first))
        def _():
            dw_ref[...] += dwv
            db_ref[...] += dbv

    blk = pl.BlockSpec((1, s, tc), lambda j, i: (i, 0, j))
    return pl.pallas_call(
        body, name="conv_bwd", grid=(c // tc, nb),
        in_specs=[blk, pl.BlockSpec((CONV_WIDTH, tc), lambda j, i: (0, j)), pl.BlockSpec((1, tc), lambda j, i: (0, j)), blk],
        out_specs=[blk, pl.BlockSpec((8, tc), lambda j, i: (0, j)), pl.BlockSpec((1, tc), lambda j, i: (0, j))],
        out_shape=[jax.ShapeDtypeStruct((nb, s, c), BF16), jax.ShapeDtypeStruct((8, c), F32), jax.ShapeDtypeStruct((1, c), F32)],
        compiler_params=_params("parallel", "arbitrary"),
    )(u, w, b, dout)


FWD_KEY_BLOCK = 256


def _branch_bias_table(seq, kb):
    ratio = SEQ_BLOCK // kb
    key = np.arange(kb)[None, :, None]
    query = np.arange(SEQ_BLOCK)[None, None, :]
    delta = (np.arange(seq // kb)[:, None, None] - (ratio - 1)) * kb + query - key
    cnt = np.zeros(delta.shape, np.float64)
    for window, dilation in ((128, 1), (512, 4), (2048, 16)):
        cnt += (delta >= 0) & (delta % dilation == 0) & (delta <= window)
    return jnp.asarray(np.where(cnt > 0, np.log(np.maximum(cnt, 1.0)), NEG).astype(np.float32))


HEADS_PER_BLOCK = LANES // HEAD_DIM


def _head_rows(v, h):
    row = lax.broadcasted_iota(jnp.int32, v.shape, 0)
    return jnp.where((row >= HEAD_DIM * h) & (row < HEAD_DIM * (h + 1)), v, jnp.zeros_like(v))


def _attn_fwd(q, k, v, bias):
    nb_, s, _ = q.shape
    ab, kb = SEQ_BLOCK, FWD_KEY_BLOCK
    nblk, nkb, ratio = s // ab, s // kb, ab // kb

    def body(q_ref, k_ref, v_ref, b_ref, o_ref, lse_ref, vt_s):
        i = pl.program_id(2)

        @pl.when(i == 0)
        def _():
            for jb in range(nkb):
                vt_s[jb] = v_ref[0, kb * jb:kb * (jb + 1), :].T

        qt = q_ref[0].T
        qts = [_head_rows(qt, h) for h in range(HEADS_PER_BLOCK)]

        def step(j, carry):
            ks = pl.ds(pl.multiple_of(j * kb, kb), kb)
            kj = k_ref[0, ks, :]
            lb = b_ref[ratio * i - j + (ratio - 1)]
            out = []
            for h in range(HEADS_PER_BLOCK):
                m, l, acc = carry[3 * h:3 * h + 3]
                st = jnp.dot(kj, qts[h], preferred_element_type=F32) + lb
                m_new = jnp.maximum(m, jnp.max(st, axis=0, keepdims=True))
                p = jnp.exp(st - m_new)
                a = jnp.exp(m - m_new)
                l = a * l + jnp.sum(p, axis=0, keepdims=True)
                vt = vt_s[j, HEAD_DIM * h:HEAD_DIM * (h + 1), :]
                acc = a * acc + jnp.dot(vt, _bf(p), preferred_element_type=F32)
                out += [m_new, l, acc]
            return tuple(out)

        init = (jnp.full((1, ab), NEG, F32), jnp.zeros((1, ab), F32), jnp.zeros((HEAD_DIM, ab), F32)) * HEADS_PER_BLOCK
        res = lax.fori_loop(0, ratio * (i + 1), step, init)
        ot = jnp.concatenate([res[3 * h + 2] / res[3 * h + 1] for h in range(HEADS_PER_BLOCK)], axis=0)
        o_ref[0] = ot.T
        rows = [res[3 * h] + jnp.log(res[3 * h + 1]) for h in range(HEADS_PER_BLOCK)]
        lse_ref[0, 0, 0] = jnp.concatenate(rows + [jnp.zeros((8 - HEADS_PER_BLOCK, ab), F32)], axis=0)

    qblk = pl.BlockSpec((1, ab, LANES), lambda b, hp, i: (b, i, hp))
    full = pl.BlockSpec((1, s, LANES), lambda b, hp, i: (b, 0, hp))
    return pl.pallas_call(
        body, name="attn_fwd", grid=(nb_, D_ATTN // LANES, nblk),
        in_specs=[qblk, full, full, pl.BlockSpec((nkb, kb, ab), lambda b, hp, i: (0, 0, 0))],
        out_specs=[qblk, pl.BlockSpec((1, 1, 1, 8, ab), lambda b, hp, i: (b, hp, i, 0, 0))],
        out_shape=[jax.ShapeDtypeStruct((nb_, s, D_ATTN), F32),
                   jax.ShapeDtypeStruct((nb_, D_ATTN // LANES, nblk, 8, ab), F32)],
        scratch_shapes=[pltpu.VMEM((nkb, LANES, kb), BF16)],
        compiler_params=_params("parallel", "parallel", "arbitrary"),
    )(q, k, v, bias)


def _attn_bwd(q, k, v, o, do, lse, bias):
    nb_, s, _ = q.shape
    ab = SEQ_BLOCK
    nblk = s // ab

    nh = HEADS_PER_BLOCK

    def body(q_ref, k_ref, v_ref, o_ref, do_ref, lse_ref, b_ref, dq_ref, dk_ref, dv_ref,
             qt_s, dot_s, kt_s, dqt_s, do16_s, d_s, dk_acc, dv_acc):
        for jb in range(nblk):
            sl = slice(ab * jb, ab * (jb + 1))
            qt, kt = q_ref[0, sl, :].T, k_ref[0, sl, :].T
            do = do_ref[0, sl, :]
            dot = do.T
            prod = dot * o_ref[0, sl, :].T
            do16_s[sl, :] = _bf(do)
            for h in range(nh):
                qt_s[nh * jb + h] = _head_rows(qt, h)
                kt_s[nh * jb + h] = _head_rows(kt, h)
                dot_s[nh * jb + h] = _head_rows(_bf(dot), h)
            d_s[jb] = jnp.concatenate(
                [jnp.sum(prod[HEAD_DIM * h:HEAD_DIM * (h + 1)], axis=0, keepdims=True) for h in range(nh)]
                + [jnp.zeros((8 - nh, ab), F32)], axis=0)
            dqt_s[jb] = jnp.zeros((LANES, ab), F32)

        def outer(j, carry):
            ks = pl.ds(pl.multiple_of(j * ab, ab), ab)
            kj, vj = k_ref[0, ks, :], v_ref[0, ks, :]
            dk_acc[...] = jnp.zeros_like(dk_acc)
            dv_acc[...] = jnp.zeros_like(dv_acc)

            def inner(i, c2):
                qs = pl.ds(pl.multiple_of(i * ab, ab), ab)
                qi, doi = q_ref[0, qs, :], do16_s[qs, :]
                lb = b_ref[i - j]
                for h in range(nh):
                    st = jnp.dot(kj, qt_s[nh * i + h], preferred_element_type=F32) + lb
                    pt = jnp.exp(st - lse_ref[0, 0, i, h:h + 1, :])
                    dpt = jnp.dot(vj, dot_s[nh * i + h], preferred_element_type=F32)
                    dst16 = _bf(pt * (dpt - d_s[i, h:h + 1, :]))
                    dv_acc[h] += jnp.dot(_bf(pt), doi, preferred_element_type=F32)
                    dk_acc[h] += jnp.dot(dst16, qi, preferred_element_type=F32)
                    dqt_s[i] += jnp.dot(kt_s[nh * j + h], dst16, preferred_element_type=F32)
                return c2

            lax.fori_loop(j, nblk, inner, 0)
            lane = lax.broadcasted_iota(jnp.int32, (ab, LANES), 1)
            dk_ref[0, ks, :] = jnp.where(lane < HEAD_DIM, dk_acc[0], dk_acc[1])
            dv_ref[0, ks, :] = _bf(jnp.where(lane < HEAD_DIM, dv_acc[0], dv_acc[1]))
            return carry

        lax.fori_loop(0, nblk, outer, 0)
        for jb in range(nblk):
            dq_ref[0, ab * jb:ab * (jb + 1), :] = dqt_s[jb].T

    assert nh == 2
    full = pl.BlockSpec((1, s, LANES), lambda b, hp: (b, 0, hp))
    return pl.pallas_call(
        body, name="attn_bwd", grid=(nb_, D_ATTN // LANES),
        in_specs=[full] * 5 + [pl.BlockSpec((1, 1, nblk, 8, ab), lambda b, hp: (b, hp, 0, 0, 0)),
                               pl.BlockSpec((nblk, ab, ab), lambda b, hp: (0, 0, 0))],
        out_specs=[full, full, full],
        out_shape=[jax.ShapeDtypeStruct((nb_, s, D_ATTN), F32), jax.ShapeDtypeStruct((nb_, s, D_ATTN), F32),
                   jax.ShapeDtypeStruct((nb_, s, D_ATTN), BF16)],
        scratch_shapes=[pltpu.VMEM((nh * nblk, LANES, ab), BF16), pltpu.VMEM((nh * nblk, LANES, ab), BF16),
                        pltpu.VMEM((nh * nblk, LANES, ab), BF16), pltpu.VMEM((nblk, LANES, ab), F32),
                        pltpu.VMEM((s, LANES), BF16), pltpu.VMEM((nblk, 8, ab), F32),
                        pltpu.VMEM((nh, ab, LANES), F32), pltpu.VMEM((nh, ab, LANES), F32)],
        compiler_params=_params("parallel", "parallel"),
    )(q, k, v, o, do, lse, bias)


def _cumsum_fwd(dag):
    nb_, s, c = dag.shape
    ab = SEQ_BLOCK

    def body(a_ref, o_ref, ot_ref):
        r = lax.broadcasted_iota(jnp.int32, (ab, ab), 0)
        cc = lax.broadcasted_iota(jnp.int32, (ab, ab), 1)
        tri = (r >= cc).astype(F32)
        carry = jnp.zeros((1, c), F32)
        for i in range(s // ab):
            loc = jnp.dot(tri, a_ref[0, ab * i:ab * (i + 1), :], precision=HIGHEST, preferred_element_type=F32) + carry
            o_ref[0, ab * i:ab * (i + 1), :] = loc
            ot_ref[0, :, ab * i:ab * (i + 1)] = loc.T
            carry = loc[ab - 1:ab, :]

    return pl.pallas_call(
        body, name="ssd_cumsum", grid=(nb_,),
        in_specs=[pl.BlockSpec((1, s, c), lambda b: (b, 0, 0))],
        out_specs=[pl.BlockSpec((1, s, c), lambda b: (b, 0, 0)), pl.BlockSpec((1, c, s), lambda b: (b, 0, 0))],
        out_shape=[jax.ShapeDtypeStruct((nb_, s, c), F32), jax.ShapeDtypeStruct((nb_, c, s), F32)],
        compiler_params=_params("parallel"),
    )(dag)


def _cumsum_bwd(dcol, drow):
    nb_, s, c = dcol.shape
    ab = SEQ_BLOCK

    def body(c_ref, r_ref, o_ref):
        r = lax.broadcasted_iota(jnp.int32, (ab, ab), 0)
        cc = lax.broadcasted_iota(jnp.int32, (ab, ab), 1)
        tri = (r <= cc).astype(F32)
        carry = jnp.zeros((1, c), F32)
        for i in reversed(range(s // ab)):
            rows = r_ref[0, :, ab * i:ab * (i + 1)].T
            parts = []
            for g in range(N_GROUPS):
                parts += [rows[:, 8 * g:8 * (g + 1)], jnp.zeros((ab, LANES - 8), F32)]
            blk = c_ref[0, ab * i:ab * (i + 1), :] + jnp.concatenate(parts, axis=1)
            loc = jnp.dot(tri, blk, precision=HIGHEST, preferred_element_type=F32) + carry
            o_ref[0, ab * i:ab * (i + 1), :] = loc
            carry = loc[0:1, :]

    return pl.pallas_call(
        body, name="ssd_cumsum_bwd", grid=(nb_,),
        in_specs=[pl.BlockSpec((1, s, c), lambda b: (b, 0, 0)), pl.BlockSpec((1, N_GROUPS * 8, s), lambda b: (b, 0, 0))],
        out_specs=pl.BlockSpec((1, s, c), lambda b: (b, 0, 0)),
        out_shape=jax.ShapeDtypeStruct((nb_, s, c), F32),
        compiler_params=_params("parallel"),
    )(dcol, drow)


def _causal_ok(i, j):
    ab = SEQ_BLOCK
    r = lax.broadcasted_iota(jnp.int32, (ab, ab), 0)
    c = lax.broadcasted_iota(jnp.int32, (ab, ab), 1)
    return (r + (i - j) * ab) >= c


def _causal_ok_t(i, j):
    ab = SEQ_BLOCK
    r = lax.broadcasted_iota(jnp.int32, (ab, ab), 0)
    c = lax.broadcasted_iota(jnp.int32, (ab, ab), 1)
    return (c + (i - j) * ab) >= r


def _ssd_fwd(xdtg, bc, acum, acum_t):
    nb_, s, _ = xdtg.shape
    ab = SEQ_BLOCK

    def body(x_ref, b_ref, c_ref, ac_ref, at_ref, y_ref):
        i = pl.program_id(2)
        ci = c_ref[0]
        acol = [ac_ref[0, :, j:j + 1] for j in range(HEADS_PER_GROUP)]

        def step(jb, accs):
            ks = pl.ds(pl.multiple_of(jb * ab, ab), ab)
            cb = lax.dot_general(ci, b_ref[0, ks, :], _NT, preferred_element_type=F32)
            ok = _causal_ok(i, jb)
            new = []
            for j in range(HEADS_PER_GROUP):
                decay = jnp.exp(jnp.where(ok, acol[j] - at_ref[0, j:j + 1, ks], NEG))
                g = _bf(cb * decay)
                new.append(accs[j] + jnp.dot(g, x_ref[0, ks, HEAD_DIM * j:HEAD_DIM * (j + 1)], preferred_element_type=F32))
            return tuple(new)

        accs = lax.fori_loop(0, i + 1, step, tuple(jnp.zeros((ab, HEAD_DIM), F32) for _ in range(HEADS_PER_GROUP)))
        y_ref[0] = jnp.concatenate(list(accs) + [jnp.zeros((ab, GROUP_LANES - HEADS_PER_GROUP * HEAD_DIM), F32)], axis=1)

    return pl.pallas_call(
        body, name="ssd_fwd", grid=(nb_, N_GROUPS, s // ab),
        in_specs=[pl.BlockSpec((1, s, GROUP_LANES), lambda b, g, i: (b, 0, g)),
                  pl.BlockSpec((1, s, D_STATE), lambda b, g, i: (b, 0, g)),
                  pl.BlockSpec((1, ab, D_STATE), lambda b, g, i: (b, i, N_GROUPS + g)),
                  pl.BlockSpec((1, ab, LANES), lambda b, g, i: (b, i, g)),
                  pl.BlockSpec((1, 8, s), lambda b, g, i: (b, (LANES // 8) * g, 0))],
        out_specs=pl.BlockSpec((1, ab, GROUP_LANES), lambda b, g, i: (b, i, g)),
        out_shape=jax.ShapeDtypeStruct((nb_, s, N_GROUPS * GROUP_LANES), F32),
        compiler_params=_params("parallel", "parallel", "parallel"),
    )(xdtg, bc, bc, acum, acum_t)


def _ssd_bwd(xdtg, bc, acum, acum_t, dyg):
    nb_, s, _ = xdtg.shape
    ab = SEQ_BLOCK
    nblk = s // ab
    hpg = HEADS_PER_GROUP

    def body(x_ref, b_ref, c_ref, ac_ref, at_ref, dy_ref, dx_ref, db_ref, dc_ref, dac_ref, dar_ref):
        dx_ref[...] = jnp.zeros_like(dx_ref)
        db_ref[...] = jnp.zeros_like(db_ref)
        dac_ref[...] = jnp.zeros_like(dac_ref)
        dar_ref[...] = jnp.zeros_like(dar_ref)

        def outer(i, carry):
            qs = pl.ds(pl.multiple_of(i * ab, ab), ab)
            ci = c_ref[0, qs, :]
            dyi = [_bf(dy_ref[0, qs, HEAD_DIM * j:HEAD_DIM * (j + 1)]) for j in range(hpg)]
            arow = [at_ref[0, j:j + 1, qs] for j in range(hpg)]

            def inner(jb, st):
                dc_acc, rs = st[0], list(st[1:])
                ks = pl.ds(pl.multiple_of(jb * ab, ab), ab)
                bj = b_ref[0, ks, :]
                cbt = lax.dot_general(bj, ci, _NT, preferred_element_type=F32)
                ok = _causal_ok_t(i, jb)
                dcbt = jnp.zeros((ab, ab), F32)
                for j in range(hpg):
                    hs = slice(HEAD_DIM * j, HEAD_DIM * (j + 1))
                    decay = jnp.exp(jnp.where(ok, arow[j] - ac_ref[0, ks, j:j + 1], NEG))
                    gt = cbt * decay
                    dgt = lax.dot_general(x_ref[0, ks, hs], dyi[j], _NT, preferred_element_type=F32)
                    dx_ref[0, ks, hs] += jnp.dot(_bf(gt), dyi[j], preferred_element_type=F32)
                    dcbt = dcbt + dgt * decay
                    mm = dgt * gt
                    rs[j] = rs[j] + jnp.sum(mm, axis=0, keepdims=True)
                    dac_ref[0, ks, j:j + 1] -= jnp.sum(mm, axis=1, keepdims=True)
                dcbt16 = _bf(dcbt)
                db_ref[0, ks, :] += jnp.dot(dcbt16, ci, preferred_element_type=F32)
                return (dc_acc + lax.dot_general(dcbt16, bj, _TN, preferred_element_type=F32), *rs)

            init = (jnp.zeros((ab, D_STATE), F32),) + tuple(jnp.zeros((1, ab), F32) for _ in range(hpg))
            st = lax.fori_loop(0, i + 1, inner, init)
            dc_ref[0, qs, :] = st[0]
            for j in range(hpg):
                dar_ref[0, j:j + 1, qs] = st[1 + j]
            return carry

        lax.fori_loop(0, nblk, outer, 0)

    xblk = pl.BlockSpec((1, s, GROUP_LANES), lambda b, g: (b, 0, g))
    sblk = pl.BlockSpec((1, s, D_STATE), lambda b, g: (b, 0, g))
    tblk = pl.BlockSpec((1, 8, s), lambda b, g: (b, (LANES // 8) * g, 0))
    return pl.pallas_call(
        body, name="ssd_bwd", grid=(nb_, N_GROUPS),
        in_specs=[xblk, sblk, pl.BlockSpec((1, s, D_STATE), lambda b, g: (b, 0, N_GROUPS + g)), sblk, tblk, xblk],
        out_specs=[xblk, sblk, sblk, sblk, pl.BlockSpec((1, 8, s), lambda b, g: (b, g, 0))],
        out_shape=[jax.ShapeDtypeStruct((nb_, s, N_GROUPS * GROUP_LANES), F32),
                   jax.ShapeDtypeStruct((nb_, s, N_GROUPS * D_STATE), F32),
                   jax.ShapeDtypeStruct((nb_, s, N_GROUPS * D_STATE), F32),
                   jax.ShapeDtypeStruct((nb_, s, N_GROUPS * LANES), F32),
                   jax.ShapeDtypeStruct((nb_, N_GROUPS * 8, s), F32)],
        compiler_params=_params("parallel", "parallel"),
    )(xdtg, bc, bc, acum, acum_t, dyg)


def _interleave(wg, wu):
    k, f = wg.shape
    gi = GATE_UP_INTERLEAVE
    return jnp.stack([wg.reshape(k, f // gi, gi), wu.reshape(k, f // gi, gi)], axis=2).reshape(k, 2 * f)


def _head_expanders():
    e_x = np.zeros((LANES, N_GROUPS * GROUP_LANES), np.float32)
    e_a = np.zeros((LANES, N_GROUPS * LANES), np.float32)
    for h in range(N_HEADS):
        g, j = divmod(h, HEADS_PER_GROUP)
        e_x[h, GROUP_LANES * g + HEAD_DIM * j:GROUP_LANES * g + HEAD_DIM * (j + 1)] = 1.0
        e_a[h, LANES * g + j] = 1.0
    return jnp.asarray(e_x), jnp.asarray(e_a)


def _pad_lanes(v, n=LANES):
    return jnp.pad(v, ((0, 0), (0, n - v.shape[1])))


def _local_step(x, positions, target, w, late_job=None, late_weights=None, early_grad_job=None):
    nb, s, d = x.shape
    t = nb * s
    x2 = x.reshape(t, d)
    tgt2 = target.reshape(t, d)

    x16 = _bf(x2)
    wgu1 = _interleave(w["ffn1_gate"], w["ffn1_up"])
    if late_job is None:
        au1, hm1 = _mm_swiglu("ffn1_up", x16, wgu1)
    else:
        au1, hm1, late = _mm_swiglu("ffn1_up", x16, wgu1, carry=late_job)
        w = {**w, **late_weights(late)}

    wgu2 = _interleave(w["ffn2_gate"], w["ffn2_up"])
    w_in = w["w_in"]
    wqk, wv, wz = w_in[:, :2 * D_ATTN], w_in[:, 2 * D_ATTN:3 * D_ATTN], w_in[:, 3 * D_ATTN:3 * D_ATTN + D_SSD]
    wxbc = w_in[:, 3 * D_ATTN + D_SSD:3 * D_ATTN + D_SSD + D_CONV]
    wdt = _pad_lanes(w_in[:, 3 * D_ATTN + D_SSD + D_CONV:])

    inv_freq = ROPE_THETA ** (-jnp.arange(0, ROPE_DIM, 2, dtype=F32) / ROPE_DIM)
    half = ROPE_DIM // 2
    head_invf = jnp.concatenate([inv_freq, inv_freq, jnp.zeros((HEAD_DIM - ROPE_DIM,), F32)])
    head_sgn = jnp.concatenate([-jnp.ones((half,), F32), jnp.ones((half,), F32), jnp.zeros((HEAD_DIM - ROPE_DIM,), F32)])
    invf = jnp.tile(head_invf, LANES // HEAD_DIM)[None, :]
    sgn = jnp.tile(head_sgn, LANES // HEAD_DIM)[None, :]
    posf = positions.astype(F32).reshape(t, 1)
    bias_fwd, bias_bwd = _branch_bias_table(s, FWD_KEY_BLOCK), _branch_bias_table(s, SEQ_BLOCK)
    e_x, e_a = _head_expanders()
    dtb, alog = _pad_lanes(w["dt_bias"]), _pad_lanes(w["a_log"])
    dskip = jnp.repeat(w["d_skip"], HEAD_DIM, axis=1)

    h1, r1, h1_16 = _mm_res_ln("ffn1_down_ln1", hm1, w["ffn1_down"], x2, w["ln1_g"], w["ln1_b"], scale=0.5)

    qk = _mm("proj_qk", [(h1_16, wqk)], tn=768)
    v16 = _mm("proj_v", [(h1_16, wv)], tn=768, out_dtype=BF16)
    z = _mm("proj_z", [(h1_16, wz)], tn=768)
    xbc_pre = _mm("proj_xbc", [(h1_16, wxbc)], tn=896)
    dtp = _mm("proj_dt", [(h1_16, wdt)], tn=LANES)

    q16, k16, cs = _rope_fwd(qk, posf, invf, sgn)
    to3 = lambda a: a.reshape(nb, s, a.shape[-1])
    attn_o, lse = _attn_fwd(to3(q16), to3(k16), to3(v16), bias_fwd)

    xbc = _conv_fwd(to3(xbc_pre), w["conv_w"], w["conv_b"]).reshape(t, D_CONV)
    xdtg, bc16, dag = _ssd_prep_fwd(xbc, dtp, dtb, alog, e_x, e_a)
    acum, acum_t = _cumsum_fwd(to3(dag))
    yg = _ssd_fwd(to3(xdtg), to3(bc16), acum, acum_t)

    cat = _norms_fwd(attn_o.reshape(t, D_ATTN), yg.reshape(t, -1), xbc, z, w["attn_norm_w"], w["ssd_norm_w"], dskip)
    h2, r2, h2_16 = _mm_res_ln("w_out_ln2", cat, w["w_out"], h1, w["ln2_g"], w["ln2_b"], scale=1.0)

    au2, hm2 = _mm_swiglu("ffn2_up", h2_16, wgu2)
    _, r3, _ = _mm_res_ln("ffn2_down_ln3", hm2, w["ffn2_down"], h2, w["ln3_g"], w["ln3_b"], scale=0.5)

    g = {}
    dr3, dr3_16, g["ln3_g"], g["ln3_b"], loss = _ln_loss_bwd("loss_ln3_bwd", r3, w["ln3_g"], w["ln3_b"], tgt2)

    dau2 = _mm_swiglu_bwd("ffn2_act_bwd", dr3_16, w["ffn2_down"].T, au2, scale=0.5)
    g["ffn2_down"] = _mm_tn("ffn2_down_dw", hm2, dr3_16, scale=0.5, tk=D_FF // 2, tn=512)
    g["ffn2_gate"], g["ffn2_up"] = _mm_tn_gate_up("ffn2_up_dw", h2_16, dau2)
    dh2 = _mm("ffn2_dx", [(dau2, wgu2.T)], res=dr3, res_scale=ALPHA)

    dr2, dr2_16, g["ln2_g"], g["ln2_b"] = _ln_bwd("ln2_bwd", r2, w["ln2_g"], w["ln2_b"], dh2)
    dcat = _mm("w_out_dx", [(dr2_16, w["w_out"].T)], tn=768)
    g["w_out"] = _mm_tn("w_out_dw", cat, dr2_16, tk=768, tn=1024)

    d_attn, dyg, dxs_a, dz16, g["attn_norm_w"], g["ssd_norm_w"], ddskip = _norms_bwd(
        attn_o.reshape(t, D_ATTN), yg.reshape(t, -1), xbc, z, w["attn_norm_w"], w["ssd_norm_w"], dskip, dcat)
    g["d_skip"] = ddskip.reshape(N_HEADS, HEAD_DIM).sum(axis=1)[None, :]

    dq, dk, dv16 = _attn_bwd(to3(q16), to3(k16), to3(v16), attn_o, to3(d_attn), lse, bias_bwd)
    dqk16 = _rope_bwd(dq.reshape(t, D_ATTN), dk.reshape(t, D_ATTN), cs)

    dxdtg, dbm, dcm, dacol, darow = _ssd_bwd(to3(xdtg), to3(bc16), acum, acum_t, to3(dyg))
    ddag = _cumsum_bwd(dacol, darow)
    dxbc, ddtp16, ddtb, dalog = _ssd_prep_bwd(xbc, dtp, dtb, alog, e_x, e_a, dxdtg.reshape(t, -1), ddag.reshape(t, -1),
                                               dxs_a, dbm.reshape(t, -1), dcm.reshape(t, -1))
    g["dt_bias"], g["a_log"] = ddtb[:, :N_HEADS], dalog[:, :N_HEADS]
    dxbc_pre16, dconv_w, g["conv_b"] = _conv_bwd(to3(xbc_pre), w["conv_w"], w["conv_b"], to3(dxbc))
    g["conv_w"] = dconv_w[:CONV_WIDTH]
    dxbc_pre16 = dxbc_pre16.reshape(t, D_CONV)
    dv16 = dv16.reshape(t, D_ATTN)

    dh1 = _mm("w_in_dx", [(dqk16, wqk.T), (dv16, wv.T), (dz16, wz.T), (dxbc_pre16, wxbc.T), (ddtp16, wdt.T)],
              res=dr2, res_scale=ALPHA)
    g["w_in"] = jnp.concatenate([
        _mm_tn("w_in_dw_qk", h1_16, dqk16, tk=1024, tn=512),
        _mm_tn("w_in_dw_v", h1_16, dv16, tk=1024, tn=768),
        _mm_tn("w_in_dw_z", h1_16, dz16, tk=1024, tn=768),
        _mm_tn("w_in_dw_xbc", h1_16, dxbc_pre16, tk=1024, tn=896),
        _mm_tn("w_in_dw_dt", h1_16, ddtp16, tk=1024, tn=LANES)[:, :N_HEADS],
    ], axis=1)

    dr1, dr1_16, g["ln1_g"], g["ln1_b"] = _ln_bwd("ln1_bwd", r1, w["ln1_g"], w["ln1_b"], dh1)
    if early_grad_job is None:
        dau1, early = _mm_swiglu_bwd("ffn1_act_bwd", dr1_16, w["ffn1_down"].T, au1, scale=0.5), None
    else:
        dau1, early = _mm_swiglu_bwd("ffn1_act_bwd", dr1_16, w["ffn1_down"].T, au1, scale=0.5, carry=early_grad_job(g))
    g["ffn1_down"] = _mm_tn("ffn1_down_dw", hm1, dr1_16, scale=0.5, tk=D_FF // 2, tn=512)
    g["ffn1_gate"], g["ffn1_up"] = _mm_tn_gate_up("ffn1_up_dw", x16, dau1)
    dx = _mm("ffn1_dx", [(dau1, wgu1.T)], res=dr1, res_scale=ALPHA)
    return loss, dx.reshape(nb, s, d), g, early


_HBM = pl.BlockSpec(memory_space=pltpu.HBM)
N_CHIPS = 4
N_DEVICES = 8


def _place():
    return lax.axis_index("x"), lax.axis_index("y"), lax.axis_index("c")


def _other_chips(x, y):
    return [(1 - x, y), (x, 1 - y), (1 - x, 1 - y)]


class _GatherJob:
    def __init__(self, shards):
        assert all((a.shape[0] // 2) % 16 == 0 for a in shards)
        self.n = len(shards)
        self.shapes = [a.shape for a in shards]
        self.operands = [a.reshape(2, a.shape[0] // 2, a.shape[1]) for a in shards]
        self.out_shape = [jax.ShapeDtypeStruct((N_CHIPS,) + a.shape, a.dtype) for a in self.operands]
        pair = pltpu.SemaphoreType.DMA((self.n, N_CHIPS - 1))
        self.scratch_shapes = [pair, pair, pair, pair]

    def results(self, outs):
        return [o.reshape((N_CHIPS,) + s) for o, s in zip(outs, self.shapes)]

    def phases(self, ins, outs, sems):
        n = self.n
        send_sems, recv_sems, fwd_send_sems, fwd_recv_sems = sems
        x, y, c = _place()
        me = 2 * x + y
        peers = _other_chips(x, y)

        def ici(t, p, src_chip):
            px, py = peers[p]
            return pltpu.make_async_remote_copy(
                ins[t].at[c] if src_chip is None else outs[t].at[src_chip, c],
                outs[t].at[me if src_chip is None else src_chip, c],
                send_sems.at[t, p], recv_sems.at[t, p], device_id=(px, py, c), device_id_type=MESH)

        def d2d(t, p, core):
            px, py = peers[p]
            return pltpu.make_async_remote_copy(
                outs[t].at[2 * px + py, core], outs[t].at[2 * px + py, core],
                fwd_send_sems.at[t, p], fwd_recv_sems.at[t, p], device_id=(x, y, 1 - c), device_id_type=MESH)

        pairs = [(t, p) for t in range(n) for p in range(N_CHIPS - 1)]

        def start():
            for t, p in pairs:
                ici(t, p, None).start()

        def forward():
            for t, p in pairs:
                px, py = peers[p]
                ici(t, p, 2 * px + py).wait_recv()
                d2d(t, p, c).start()

        def finish():
            for t, p in pairs:
                d2d(t, p, 1 - c).wait_recv()
            for t, p in pairs:
                ici(t, p, None).wait_send()
                d2d(t, p, c).wait_send()

        return start, forward, finish


class _ExchangeJob:
    def __init__(self, stacks):
        self.n = len(stacks)
        self.operands = list(stacks)
        self.out_shape = [jax.ShapeDtypeStruct(a.shape, a.dtype) for a in stacks]
        pair = pltpu.SemaphoreType.DMA((self.n, N_CHIPS - 1))
        self.scratch_shapes = [pair, pair]

    def results(self, outs):
        return list(outs)

    def phases(self, ins, outs, sems):
        send_sems, recv_sems = sems
        x, y, c = _place()
        me = 2 * x + y
        peers = _other_chips(x, y)
        pairs = [(t, p) for t in range(self.n) for p in range(N_CHIPS - 1)]

        def copy(t, p):
            px, py = peers[p]
            return pltpu.make_async_remote_copy(ins[t].at[2 * px + py], outs[t].at[me], send_sems.at[t, p],
                                                recv_sems.at[t, p], device_id=(px, py, c), device_id_type=MESH)

        def arrival(t, p):
            px, py = peers[p]
            return pltpu.make_async_remote_copy(ins[t].at[me], outs[t].at[2 * px + py], send_sems.at[t, p],
                                                recv_sems.at[t, p], device_id=(px, py, c), device_id_type=MESH)

        def start():
            for t, p in pairs:
                copy(t, p).start()

        def finish():
            for t, p in pairs:
                arrival(t, p).wait_recv()
            for t, p in pairs:
                copy(t, p).wait_send()

        return start, None, finish


def _run_job(job, name):
    n = job.n

    def body(*refs):
        for phase in job.phases(refs[:n], refs[n:2 * n], refs[2 * n:]):
            if phase is not None:
                phase()

    outs = pl.pallas_call(
        body, name=name, in_specs=[_HBM] * n, out_specs=[_HBM] * n,
        out_shape=job.out_shape, scratch_shapes=job.scratch_shapes,
    )(*job.operands)
    return job.results(outs)


def _sibling_halves(stacks, name):
    n = len(stacks)
    halves = [a.shape[1] // 2 for a in stacks]
    split = [a.reshape(a.shape[0], 2, h, a.shape[2]) for a, h in zip(stacks, halves)]

    def body(*refs):
        ins, outs = refs[:n], refs[n:2 * n]
        send_sems, recv_sems = refs[2 * n:]
        x, y, c = _place()
        cps = []
        for t in range(n):
            cp = pltpu.make_async_remote_copy(ins[t].at[:, 1 - c], outs[t], send_sems.at[t], recv_sems.at[t],
                                              device_id=(x, y, 1 - c), device_id_type=MESH)
            cp.start()
            cps.append(cp)
        for cp in cps:
            cp.wait()

    return pl.pallas_call(
        body, name=name,
        in_specs=[_HBM] * n, out_specs=[_HBM] * n,
        out_shape=[jax.ShapeDtypeStruct((a.shape[0], h, a.shape[2]), a.dtype) for a, h in zip(stacks, halves)],
        scratch_shapes=[pltpu.SemaphoreType.DMA((n,)), pltpu.SemaphoreType.DMA((n,))],
    )(*split)


def _sibling_swap(arrs):
    n = len(arrs)

    def body(*refs):
        ins, outs = refs[:n], refs[n:2 * n]
        send_sems, recv_sems = refs[2 * n:]
        x, y, c = _place()
        cps = []
        for t in range(n):
            cp = pltpu.make_async_remote_copy(ins[t], outs[t], send_sems.at[t], recv_sems.at[t],
                                              device_id=(x, y, 1 - c), device_id_type=MESH)
            cp.start()
            cps.append(cp)
        for cp in cps:
            cp.wait()

    return pl.pallas_call(
        body, name="sibling_swap",
        in_specs=[_HBM] * n, out_specs=[_HBM] * n,
        out_shape=[jax.ShapeDtypeStruct(a.shape, a.dtype) for a in arrs],
        scratch_shapes=[pltpu.SemaphoreType.DMA((n,)), pltpu.SemaphoreType.DMA((n,))],
    )(*arrs)


def _half_sum(name, own, other, core):
    k, r, cols = own.shape
    h = r // 2
    tr = next(cand for cand in (128, 176, 64, 32, 16) if h % cand == 0)
    nblk = h // tr

    def body(core_ref, own_ref, other_ref, o_ref):
        o_ref[...] = _bf(own_ref[...] + other_ref[...].astype(F32))

    grid_spec = pltpu.PrefetchScalarGridSpec(
        num_scalar_prefetch=1, grid=(nblk,),
        in_specs=[pl.BlockSpec((k, tr, cols), lambda i, core_ref: (0, i + core_ref[0] * nblk, 0)),
                  pl.BlockSpec((k, tr, cols), lambda i, core_ref: (0, i, 0))],
        out_specs=pl.BlockSpec((k, tr, cols), lambda i, core_ref: (0, i, 0)))
    return pl.pallas_call(
        body, name=name, grid_spec=grid_spec, out_shape=jax.ShapeDtypeStruct((k, h, cols), BF16),
        compiler_params=_params("parallel"),
    )(core.reshape(1).astype(jnp.int32), own, other)


def _small_allreduce(v):
    r = v.shape[0]

    def body(v_ref, tot_ref, slots, send_sems, recv_sems):
        x, y, c = _place()
        me = 4 * x + 2 * y + c
        slots[me] = v_ref[...]
        cps, peers = [], []
        for k in range(1, N_DEVICES):
            px = 1 - x if (k >> 2) & 1 else x
            py = 1 - y if (k >> 1) & 1 else y
            pc = 1 - c if k & 1 else c
            cp = pltpu.make_async_remote_copy(v_ref, slots.at[me], send_sems.at[k - 1], recv_sems.at[k - 1],
                                              device_id=(px, py, pc), device_id_type=MESH)
            cp.start()
            cps.append(cp)
            peers.append((px, py, pc))
        for k, (px, py, pc) in enumerate(peers):
            pltpu.make_async_remote_copy(v_ref, slots.at[4 * px + 2 * py + pc], send_sems.at[k], recv_sems.at[k],
                                         device_id=(px, py, pc), device_id_type=MESH).wait_recv()
        for cp in cps:
            cp.wait_send()
        acc = slots[0]
        for s in range(1, N_DEVICES):
            acc = acc + slots[s]
        tot_ref[...] = acc

    return pl.pallas_call(
        body, name="small_allreduce",
        in_specs=[pl.BlockSpec(memory_space=pltpu.VMEM)], out_specs=pl.BlockSpec(memory_space=pltpu.VMEM),
        out_shape=jax.ShapeDtypeStruct((r, LANES), F32),
        scratch_shapes=[pltpu.VMEM((N_DEVICES, r, LANES), F32), pltpu.SemaphoreType.DMA((N_DEVICES - 1,)),
                        pltpu.SemaphoreType.DMA((N_DEVICES - 1,))],
    )(v)


def _elementwise(name, fn, ins, out_dtypes):
    r, c = ins[0].shape[-2:]
    tr = next((cand for cand in (256, 176, 128, 64, 32, 16) if r % cand == 0), r)
    nin = len(ins)

    def body(*refs):
        outs = fn(*[ref[...] for ref in refs[:nin]])
        for o_ref, o in zip(refs[nin:], outs):
            o_ref[...] = o.astype(o_ref.dtype)

    in_specs = [pl.BlockSpec((tr, c), lambda i: (i, 0)) if a.ndim == 2 else pl.BlockSpec((a.shape[0], tr, c), lambda i: (0, i, 0))
                for a in ins]
    return pl.pallas_call(
        body, name=name, grid=(r // tr,), in_specs=in_specs,
        out_specs=[pl.BlockSpec((tr, c), lambda i: (i, 0)) for _ in out_dtypes],
        out_shape=[jax.ShapeDtypeStruct((r, c), dt) for dt in out_dtypes],
        compiler_params=_params("parallel"),
    )(*ins)


def _row_tile(rows):
    return next((cand for cand in (128, 176, 64, 32, 16) if rows % cand == 0), rows)


def _sum_slots(name, received, own, chip):
    _, r, cols = own.shape
    tr = _row_tile(r)

    def body(chip_ref, own_ref, a_ref, b_ref, c_ref, o_ref):
        o_ref[...] = ((own_ref[0].astype(F32) + a_ref[0].astype(F32)) + b_ref[0].astype(F32)) + c_ref[0].astype(F32)

    def slot(flip):
        return pl.BlockSpec((1, tr, cols), lambda i, chip_ref: (jnp.bitwise_xor(chip_ref[0], flip), i, 0))

    grid_spec = pltpu.PrefetchScalarGridSpec(
        num_scalar_prefetch=1, grid=(r // tr,), in_specs=[slot(0), slot(1), slot(2), slot(3)],
        out_specs=pl.BlockSpec((tr, cols), lambda i, chip_ref: (i, 0)))
    return pl.pallas_call(
        body, name=name, grid_spec=grid_spec, out_shape=jax.ShapeDtypeStruct((r, cols), F32),
        compiler_params=_params("parallel"),
    )(chip.reshape(1).astype(jnp.int32), own, received, received, received)


def _adamw_halves(name, mine, theirs, core, w, m, v):
    h, cols = mine.shape
    tr = _row_tile(h)
    nh = h // tr

    def body(core_ref, mine_ref, theirs_ref, w_ref, m_ref, v_ref, g_ref, d_ref, m2_ref, v2_ref):
        is_mine = (pl.program_id(0) // nh) == core_ref[0]
        g = jnp.where(is_mine, mine_ref[...], theirs_ref[...])
        outs = _adamw_math(g, w_ref[...], m_ref[...], v_ref[...])
        for ref, val in zip((g_ref, d_ref, m2_ref, v2_ref), outs):
            ref[...] = val

    half = pl.BlockSpec((tr, cols), lambda i, core_ref: (i % nh, 0))
    full = pl.BlockSpec((tr, cols), lambda i, core_ref: (i, 0))
    grid_spec = pltpu.PrefetchScalarGridSpec(
        num_scalar_prefetch=1, grid=(2 * nh,), in_specs=[half, half, full, full, full], out_specs=[full] * 4)
    return pl.pallas_call(
        body, name=name, grid_spec=grid_spec, out_shape=[jax.ShapeDtypeStruct((2 * h, cols), F32)] * 4,
        compiler_params=_params("parallel"),
    )(core.reshape(1).astype(jnp.int32), mine, theirs, w, m, v)


def _adamw_math(g, w_v, m_v, v_v):
    m2 = ADAM_B1 * m_v + (1.0 - ADAM_B1) * g
    v2 = ADAM_B2 * v_v + (1.0 - ADAM_B2) * jnp.square(g)
    m_hat = m2 / (1.0 - ADAM_B1 ** ADAM_STEP)
    v_hat = v2 / (1.0 - ADAM_B2 ** ADAM_STEP)
    delta = -ADAM_LR * (m_hat / (jnp.sqrt(v_hat) + ADAM_EPS) + ADAM_WD * w_v)
    return [g, delta, m2, v2]


def _adamw(name, g, w, m, v):
    return _elementwise(name, _adamw_math, [g, w, m, v], [F32] * 4)


_MATRICES = (("ffn1_gate", 1), ("ffn1_up", 1), ("ffn1_down", 0), ("w_in", 1), ("w_out", 0),
             ("ffn2_gate", 1), ("ffn2_up", 1), ("ffn2_down", 0))
_VECTORS = ("ln1_g", "ln1_b", "conv_b", "dt_bias", "a_log", "d_skip", "attn_norm_w", "ssd_norm_w",
            "ln2_g", "ln2_b", "ln3_g", "ln3_b")
_WEIGHT_ORDER = ("ln1_g", "ln1_b", "ffn1_gate", "ffn1_up", "ffn1_down", "w_in", "conv_w", "conv_b", "dt_bias", "a_log",
                 "d_skip", "attn_norm_w", "ssd_norm_w", "w_out", "ln2_g", "ln2_b", "ffn2_gate", "ffn2_up", "ffn2_down",
                 "ln3_g", "ln3_b")


def _pack_rows(vectors):
    parts = []
    for vec in vectors:
        flat = vec.reshape(-1)
        parts.append(jnp.pad(flat, (0, (-flat.shape[0]) % LANES)))
    flat = jnp.concatenate(parts)
    flat = jnp.pad(flat, (0, (-flat.shape[0]) % (8 * LANES)))
    return flat.reshape(-1, LANES)


def _unpack_rows(packed, shapes):
    flat = packed.reshape(-1)
    out, off = [], 0
    for shape in shapes:
        size = int(np.prod(shape))
        out.append(flat[off:off + size].reshape(shape))
        off += size + (-size) % LANES
    return out


def _assemble(stack, own, chip, axis):
    blocks = [jnp.where(chip == s, own, stack[s]) for s in range(N_CHIPS)]
    return jnp.concatenate(blocks, axis=axis)


def _split(full, axis):
    if axis == 0:
        return full.reshape(N_CHIPS, -1, full.shape[1])
    cols = full.shape[1] // N_CHIPS
    return jnp.stack([full[:, cols * s:cols * (s + 1)] for s in range(N_CHIPS)])


def kernel(x, positions, ln1_g, ln1_b, ffn1_gate, ffn1_up, ffn1_down, w_in, conv_w, conv_b, dt_bias, a_log, d_skip, attn_norm_w, ssd_norm_w, w_out, ln2_g, ln2_b, ffn2_gate, ffn2_up, ffn2_down, ln3_g, ln3_b, loss_target, m_ln1_g, m_ln1_b, m_ffn1_gate, m_ffn1_up, m_ffn1_down, m_w_in, m_conv_w, m_conv_b, m_dt_bias, m_a_log, m_d_skip, m_attn_norm_w, m_ssd_norm_w, m_w_out, m_ln2_g, m_ln2_b, m_ffn2_gate, m_ffn2_up, m_ffn2_down, m_ln3_g, m_ln3_b, v_ln1_g, v_ln1_b, v_ffn1_gate, v_ffn1_up, v_ffn1_down, v_w_in, v_conv_w, v_conv_b, v_dt_bias, v_a_log, v_d_skip, v_attn_norm_w, v_ssd_norm_w, v_w_out, v_ln2_g, v_ln2_b, v_ffn2_gate, v_ffn2_up, v_ffn2_down, v_ln3_g, v_ln3_b):
    given = dict(locals())
    wts = {n: given[n] for n in _WEIGHT_ORDER}
    mom_m = {n: given["m_" + n] for n in _WEIGHT_ORDER}
    mom_v = {n: given["v_" + n] for n in _WEIGHT_ORDER}
    chip = 2 * lax.axis_index("x") + lax.axis_index("y")

    core = lax.axis_index("c")
    first = [(n, axis) for n, axis in _MATRICES if n.startswith("ffn1")]
    rest = [(n, axis) for n, axis in _MATRICES if not n.startswith("ffn1")]
    own16 = {n: wts[n][0].astype(BF16) for n, _ in _MATRICES}
    gathered = _run_job(_GatherJob([own16[n] for n, _ in first]), "gather_ffn1")
    full = {n: _assemble(st, own16[n], chip, axis) for (n, axis), st in zip(first, gathered)}
    for n in _VECTORS:
        full[n] = wts[n]
    conv_rows = jnp.pad(wts["conv_w"][0], ((0, 32 - CONV_WIDTH), (0, 0)))
    late_job = _GatherJob([own16[n] for n, _ in rest] + [conv_rows])

    def late_weights(results):
        out = {n: _assemble(st, own16[n], chip, axis) for (n, axis), st in zip(rest, results)}
        out["conv_w"] = _assemble(results[-1], conv_rows, chip, 1)[:CONV_WIDTH]
        return out

    chip_sums = {}

    def core_sums(g, which, tag):
        partials = [_split(g[n], axis) for n, axis in which]
        from_sibling = _sibling_halves([p.astype(BF16) for p in partials], "sibling_halves_" + tag)
        for (n, _), p, o in zip(which, partials, from_sibling):
            chip_sums[n] = _half_sum("core_sum_" + n, p, o, core)
        return _ExchangeJob([chip_sums[n] for n, _ in which])

    loss, grad_x, g, received_rest = _local_step(x, positions, loss_target, full, late_job, late_weights,
                                                 lambda g_now: core_sums(g_now, rest, "rest"))
    received_first = _run_job(core_sums(g, first, "ffn1"), "exchange_ffn1")
    received = dict(zip([n for n, _ in first + rest], received_first + received_rest))
    half_totals = [_sum_slots("sum_partials_" + n, received[n], chip_sums[n], chip) for n, _ in _MATRICES]
    other_halves = _sibling_swap(half_totals)

    small_shapes = [g[n].shape for n in _VECTORS] + [g["conv_w"].shape, (1,)]
    total = _small_allreduce(_pack_rows([g[n] for n in _VECTORS] + [g["conv_w"], loss[0, :1]]))
    small = _unpack_rows(total, small_shapes)
    loss_out = small[-1].reshape(())

    grads, deltas, new_m, new_v = {}, {}, {}, {}
    for (n, _), mine, theirs in zip(_MATRICES, half_totals, other_halves):
        res = _adamw_halves("adamw_" + n, mine, theirs, core, wts[n][0], mom_m[n][0], mom_v[n][0])
        grads[n], deltas[n], new_m[n], new_v[n] = [r[None] for r in res]

    vec_shapes = [wts[n].shape for n in _VECTORS]
    res = _adamw("adamw_vectors", _pack_rows(small[:len(_VECTORS)]), _pack_rows([wts[n] for n in _VECTORS]),
                 _pack_rows([mom_m[n] for n in _VECTORS]), _pack_rows([mom_v[n] for n in _VECTORS]))
    for dst, packed in zip((grads, deltas, new_m, new_v), res):
        for n, val in zip(_VECTORS, _unpack_rows(packed, vec_shapes)):
            dst[n] = val

    cols = conv_w.shape[2]
    g_conv = lax.dynamic_slice_in_dim(small[len(_VECTORS)], chip * cols, cols, axis=1)
    res = _adamw("adamw_conv_w", g_conv, wts["conv_w"][0], mom_m["conv_w"][0], mom_v["conv_w"][0])
    grads["conv_w"], deltas["conv_w"], new_m["conv_w"], new_v["conv_w"] = [r[None] for r in res]

    return (loss_out, grad_x, *[grads[n] for n in _WEIGHT_ORDER], *[deltas[n] for n in _WEIGHT_ORDER],
            *[new_m[n] for n in _WEIGHT_ORDER], *[new_v[n] for n in _WEIGHT_ORDER])
```

```python
import functools

import numpy as np
import jax
import jax.numpy as jnp
from jax import lax
from jax.experimental import pallas as pl
from jax.experimental.pallas import tpu as pltpu

F32, BF16 = jnp.float32, jnp.bfloat16

D_MODEL = 1024
D_FF = 2816
N_HEADS = 12
HEAD_DIM = 64
D_ATTN = 768
D_SSD = 768
N_GROUPS = 4
HEADS_PER_GROUP = 3
D_STATE = 128
D_CONV = 1792
CONV_WIDTH = 4
ROPE_DIM = 16
ROPE_THETA = 500000.0
ALPHA = 2.0 ** 0.25
LN_EPS = 1e-5
RMS_EPS = 1e-6
ADAM_LR, ADAM_B1, ADAM_B2, ADAM_EPS, ADAM_WD, ADAM_STEP = 0.001, 0.9, 0.999, 1e-08, 0.01, 10

LANES = 128
GATE_UP_INTERLEAVE = 256
SEQ_BLOCK = 256
GROUP_LANES = 256
VMEM_LIMIT = 56 * 1024 * 1024
NEG = -1e30
MESH = pl.DeviceIdType.MESH
HIGHEST = lax.Precision.HIGHEST

_NT = (((1,), (1,)), ((), ()))
_TN = (((0,), (0,)), ((), ()))


def _params(*sem):
    return pltpu.CompilerParams(dimension_semantics=sem, vmem_limit_bytes=VMEM_LIMIT)


def _bf(v):
    return v.astype(BF16)


EPILOGUE_ROWS = 128


def _row_chunks(tm):
    return [slice(r, min(r + EPILOGUE_ROWS, tm)) for r in range(0, tm, EPILOGUE_ROWS)]


def _sigmoid(v):
    return 0.5 * jnp.tanh(0.5 * v) + 0.5


def _mm(name, pairs, *, scale=1.0, res=None, res_scale=1.0, out_dtype=F32, tm=512, tn=512):
    m, n = pairs[0][0].shape[0], pairs[0][1].shape[1]
    tm, tn = min(tm, m), min(tn, n)
    assert m % tm == 0 and n % tn == 0, (name, m, n, tm, tn)
    npair = len(pairs)

    def body(*refs):
        acc = None
        for a_ref, b_ref in zip(refs[:npair], refs[npair:2 * npair]):
            d = jnp.dot(_bf(a_ref[...]), b_ref[...], preferred_element_type=F32)
            acc = d if acc is None else acc + d
        if scale != 1.0:
            acc = acc * scale
        if res is not None:
            acc = acc + res_scale * refs[2 * npair][...]
        refs[-1][...] = acc.astype(out_dtype)

    in_specs = [pl.BlockSpec((tm, a.shape[1]), lambda i, j: (i, 0)) for a, _ in pairs]
    in_specs += [pl.BlockSpec((b.shape[0], tn), lambda i, j: (0, j)) for _, b in pairs]
    args = [a for a, _ in pairs] + [b for _, b in pairs]
    if res is not None:
        in_specs.append(pl.BlockSpec((tm, tn), lambda i, j: (i, j)))
        args.append(res)
    return pl.pallas_call(
        body, name=name, grid=(m // tm, n // tn), in_specs=in_specs,
        out_specs=pl.BlockSpec((tm, tn), lambda i, j: (i, j)),
        out_shape=jax.ShapeDtypeStruct((m, n), out_dtype),
        compiler_params=_params("parallel", "parallel"),
    )(*args)


def _mm_tn(name, x, dy, *, scale=1.0, tk=512, tn=512, tt=1024):
    t, k = x.shape
    n = dy.shape[1]
    tk, tn, tt = min(tk, k), min(tn, n), min(tt, t)
    assert k % tk == 0 and n % tn == 0 and t % tt == 0, (name, k, n, t)
    nt = t // tt

    def body(x_ref, dy_ref, o_ref):
        step = pl.program_id(2)
        d = lax.dot_general(_bf(x_ref[...]), _bf(dy_ref[...]), _TN, preferred_element_type=F32)

        @pl.when(step == 0)
        def _():
            o_ref[...] = d

        @pl.when(step > 0)
        def _():
            o_ref[...] += d

        if scale != 1.0:
            @pl.when(step == nt - 1)
            def _():
                o_ref[...] = o_ref[...] * scale

    return pl.pallas_call(
        body, name=name, grid=(k // tk, n // tn, nt),
        in_specs=[pl.BlockSpec((tt, tk), lambda i, j, s: (s, i)), pl.BlockSpec((tt, tn), lambda i, j, s: (s, j))],
        out_specs=pl.BlockSpec((tk, tn), lambda i, j, s: (i, j)),
        out_shape=jax.ShapeDtypeStruct((k, n), F32),
        compiler_params=_params("parallel", "parallel", "arbitrary"),
    )(x, dy)


def _mm_tn_gate_up(name, x, dau, *, tt=1024):
    t, k = x.shape
    gi = GATE_UP_INTERLEAVE
    nj = dau.shape[1] // (2 * gi)
    tt = min(tt, t)
    nt = t // tt

    def body(x_ref, dy_ref, g_ref, u_ref):
        step = pl.program_id(1)
        d = lax.dot_general(_bf(x_ref[...]), dy_ref[...], _TN, preferred_element_type=F32)

        @pl.when(step == 0)
        def _():
            g_ref[...] = d[:, :gi]
            u_ref[...] = d[:, gi:]

        @pl.when(step > 0)
        def _():
            g_ref[...] += d[:, :gi]
            u_ref[...] += d[:, gi:]

    out = pl.BlockSpec((k, gi), lambda j, s: (0, j))
    return pl.pallas_call(
        body, name=name, grid=(nj, nt),
        in_specs=[pl.BlockSpec((tt, k), lambda j, s: (s, 0)), pl.BlockSpec((tt, 2 * gi), lambda j, s: (s, j))],
        out_specs=[out, out],
        out_shape=[jax.ShapeDtypeStruct((k, gi * nj), F32)] * 2,
        compiler_params=_params("parallel", "arbitrary"),
    )(x, dau)


def _carried(carry, ins, outs, sems, step, total):
    start, forward, finish = carry.phases(ins, outs, sems)
    pl.when(step == 0)(start)
    if forward is not None:
        pl.when(step == (3 * total) // 4)(forward)
    return lambda: pl.when(step == total - 1)(finish)


def _mm_swiglu(name, x, wgu, *, tm=512, carry=None):
    t, k = x.shape
    gi = GATE_UP_INTERLEAVE
    ni, nj = t // tm, wgu.shape[1] // (2 * gi)
    nc = carry.n if carry is not None else 0

    def body(*refs):
        x_ref, w_ref = refs[:2]
        au_ref, hm_ref = refs[2 + nc:4 + nc]
        if carry is not None:
            step = pl.program_id(0) * nj + pl.program_id(1)
            finish = _carried(carry, refs[2:2 + nc], refs[4 + nc:4 + 2 * nc], refs[4 + 2 * nc:], step, ni * nj)
        for rows in _row_chunks(tm):
            au = jnp.dot(_bf(x_ref[rows, :]), w_ref[...], preferred_element_type=F32)
            a, u = au[:, :gi], au[:, gi:]
            au_ref[rows, :] = _bf(au)
            hm_ref[rows, :] = _bf(a * _sigmoid(a) * u)
        if carry is not None:
            finish()

    hbm = pl.BlockSpec(memory_space=pltpu.HBM)
    res = pl.pallas_call(
        body, name=name, grid=(ni, nj),
        in_specs=[pl.BlockSpec((tm, k), lambda i, j: (i, 0)), pl.BlockSpec((k, 2 * gi), lambda i, j: (0, j))] + [hbm] * nc,
        out_specs=[pl.BlockSpec((tm, 2 * gi), lambda i, j: (i, j)), pl.BlockSpec((tm, gi), lambda i, j: (i, j))] + [hbm] * nc,
        out_shape=[jax.ShapeDtypeStruct((t, 2 * gi * nj), BF16), jax.ShapeDtypeStruct((t, gi * nj), BF16)]
        + (carry.out_shape if carry is not None else []),
        scratch_shapes=carry.scratch_shapes if carry is not None else [],
        compiler_params=_params(*(("arbitrary", "arbitrary") if carry is not None else ("parallel", "parallel"))),
    )(x, wgu, *(carry.operands if carry is not None else []))
    return res if carry is None else (res[0], res[1], carry.results(res[2:]))


def _mm_swiglu_bwd(name, dr, wdt, au, *, scale, tm=512, carry=None):
    t, k = dr.shape
    gi = GATE_UP_INTERLEAVE
    ni, nj = t // tm, wdt.shape[1] // gi
    nc = carry.n if carry is not None else 0

    def body(*refs):
        dr_ref, w_ref, au_ref = refs[:3]
        o_ref = refs[3 + nc]
        if carry is not None:
            step = pl.program_id(0) * nj + pl.program_id(1)
            finish = _carried(carry, refs[3:3 + nc], refs[4 + nc:4 + 2 * nc], refs[4 + 2 * nc:], step, ni * nj)
        for rows in _row_chunks(tm):
            dhm = jnp.dot(_bf(dr_ref[rows, :]), w_ref[...], preferred_element_type=F32) * scale
            au_v = au_ref[rows, :].astype(F32)
            a, u = au_v[:, :gi], au_v[:, gi:]
            sig = _sigmoid(a)
            silu = a * sig
            o_ref[rows, :gi] = _bf(dhm * u * (sig + silu - silu * sig))
            o_ref[rows, gi:] = _bf(dhm * silu)
        if carry is not None:
            finish()

    hbm = pl.BlockSpec(memory_space=pltpu.HBM)
    res = pl.pallas_call(
        body, name=name, grid=(ni, nj),
        in_specs=[pl.BlockSpec((tm, k), lambda i, j: (i, 0)), pl.BlockSpec((k, gi), lambda i, j: (0, j)),
                  pl.BlockSpec((tm, 2 * gi), lambda i, j: (i, j))] + [hbm] * nc,
        out_specs=[pl.BlockSpec((tm, 2 * gi), lambda i, j: (i, j))] + [hbm] * nc,
        out_shape=[jax.ShapeDtypeStruct((t, 2 * gi * nj), BF16)] + (carry.out_shape if carry is not None else []),
        scratch_shapes=carry.scratch_shapes if carry is not None else [],
        compiler_params=_params(*(("arbitrary", "arbitrary") if carry is not None else ("parallel", "parallel"))),
    )(dr, wdt, au, *(carry.operands if carry is not None else []))
    return res[0] if carry is None else (res[0], carry.results(res[1:]))


def _resident(shape):
    return pl.BlockSpec(shape, lambda i: (0,) * len(shape), pipeline_mode=pl.Buffered(1))


def _ffn_fwd(name, x16, res, wgu, wd, g, b, *, tm=512, carry=None):
    t, k = x16.shape
    gi = GATE_UP_INTERLEAVE
    nj, n, ni = wd.shape[0] // gi, wd.shape[1], t // tm
    nc = carry.n if carry is not None else 0

    def body(*refs):
        x_ref, res_ref, wgu_ref, wd_ref, g_ref, b_ref = refs[:6]
        au_ref, hm_ref, y_ref, r_ref, y16_ref = refs[6 + nc:11 + nc]
        if carry is not None:
            finish = _carried(carry, refs[6:6 + nc], refs[11 + nc:11 + 2 * nc], refs[11 + 2 * nc:], pl.program_id(0), ni)
        xv = x_ref[...]
        acc = jnp.zeros((tm, n), F32)
        for j in range(nj):
            au = jnp.dot(xv, wgu_ref[:, 2 * gi * j:2 * gi * (j + 1)], preferred_element_type=F32)
            a, u = au[:, :gi], au[:, gi:]
            au_ref[:, 2 * gi * j:2 * gi * (j + 1)] = _bf(au)
            hm = _bf(a * _sigmoid(a) * u)
            hm_ref[:, gi * j:gi * (j + 1)] = hm
            acc = acc + jnp.dot(hm, wd_ref[gi * j:gi * (j + 1), :], preferred_element_type=F32)
        r = ALPHA * res_ref[...] + 0.5 * acc
        r_ref[...] = r
        y = _layer_norm(r, g_ref[...], b_ref[...])
        y_ref[...] = y
        y16_ref[...] = _bf(y)
        if carry is not None:
            finish()

    row = lambda c: pl.BlockSpec((tm, c), lambda i: (i, 0))
    hbm = pl.BlockSpec(memory_space=pltpu.HBM)
    res_ = pl.pallas_call(
        body, name=name, grid=(ni,),
        in_specs=[row(k), row(n), _resident(wgu.shape), _resident(wd.shape), _resident(g.shape), _resident(b.shape)] + [hbm] * nc,
        out_specs=[row(2 * gi * nj), row(gi * nj), row(n), row(n), row(n)] + [hbm] * nc,
        out_shape=[jax.ShapeDtypeStruct((t, 2 * gi * nj), BF16), jax.ShapeDtypeStruct((t, gi * nj), BF16),
                   jax.ShapeDtypeStruct((t, n), F32), jax.ShapeDtypeStruct((t, n), F32), jax.ShapeDtypeStruct((t, n), BF16)]
        + (carry.out_shape if carry is not None else []),
        scratch_shapes=carry.scratch_shapes if carry is not None else [],
        compiler_params=_params("arbitrary" if carry is not None else "parallel"),
    )(x16, res, wgu, wd, g, b, *(carry.operands if carry is not None else []))
    return tuple(res_[:5]) + ((carry.results(res_[5:]),) if carry is not None else ())


def _ffn_bwd(name, dr16, dr, wdt, au, wgut, *, tm=512, carry=None):
    t, n = dr16.shape
    gi = GATE_UP_INTERLEAVE
    nj, ni = wdt.shape[1] // gi, t // tm
    nc = carry.n if carry is not None else 0

    def body(*refs):
        dr16_ref, dr_ref, wdt_ref, au_ref, wgut_ref = refs[:5]
        dau_ref, dx_ref = refs[5 + nc:7 + nc]
        if carry is not None:
            finish = _carried(carry, refs[5:5 + nc], refs[7 + nc:7 + 2 * nc], refs[7 + 2 * nc:], pl.program_id(0), ni)
        drv = dr16_ref[...]
        acc = jnp.zeros((tm, n), F32)
        for j in range(nj):
            dhm = jnp.dot(drv, wdt_ref[:, gi * j:gi * (j + 1)], preferred_element_type=F32) * 0.5
            au_v = au_ref[:, 2 * gi * j:2 * gi * (j + 1)].astype(F32)
            a, u = au_v[:, :gi], au_v[:, gi:]
            sig = _sigmoid(a)
            silu = a * sig
            dau = jnp.concatenate([_bf(dhm * u * (sig + silu - silu * sig)), _bf(dhm * silu)], axis=1)
            dau_ref[:, 2 * gi * j:2 * gi * (j + 1)] = dau
            acc = acc + jnp.dot(dau, wgut_ref[2 * gi * j:2 * gi * (j + 1), :], preferred_element_type=F32)
        dx_ref[...] = ALPHA * dr_ref[...] + acc
        if carry is not None:
            finish()

    row = lambda c: pl.BlockSpec((tm, c), lambda i: (i, 0))
    hbm = pl.BlockSpec(memory_space=pltpu.HBM)
    res_ = pl.pallas_call(
        body, name=name, grid=(ni,),
        in_specs=[row(n), row(n), _resident(wdt.shape), row(2 * gi * nj), _resident(wgut.shape)] + [hbm] * nc,
        out_specs=[row(2 * gi * nj), row(n)] + [hbm] * nc,
        out_shape=[jax.ShapeDtypeStruct((t, 2 * gi * nj), BF16), jax.ShapeDtypeStruct((t, n), F32)]
        + (carry.out_shape if carry is not None else []),
        scratch_shapes=carry.scratch_shapes if carry is not None else [],
        compiler_params=_params("arbitrary" if carry is not None else "parallel"),
    )(dr16, dr, wdt, au, wgut, *(carry.operands if carry is not None else []))
    return tuple(res_[:2]) + ((carry.results(res_[2:]),) if carry is not None else ())


def _layer_norm(r, g, b):
    mu = jnp.mean(r, axis=-1, keepdims=True)
    var = jnp.mean(jnp.square(r - mu), axis=-1, keepdims=True)
    return (r - mu) * lax.rsqrt(var + LN_EPS) * g + b


def _mm_res_ln(name, a, w, res, g, b, *, scale, tm=256):
    t, k = a.shape
    n = w.shape[1]

    def body(a_ref, w_ref, res_ref, g_ref, b_ref, y_ref, r_ref, y16_ref):
        for rows in _row_chunks(tm):
            r = ALPHA * res_ref[rows, :] + scale * jnp.dot(_bf(a_ref[rows, :]), w_ref[...], preferred_element_type=F32)
            r_ref[rows, :] = r
            y = _layer_norm(r, g_ref[...], b_ref[...])
            y_ref[rows, :] = y
            y16_ref[rows, :] = _bf(y)

    row = lambda c: pl.BlockSpec((tm, c), lambda i: (i, 0))
    const = lambda shape: pl.BlockSpec(shape, lambda i: (0, 0))
    return pl.pallas_call(
        body, name=name, grid=(t // tm,),
        in_specs=[row(k), const((k, n)), row(n), const((1, n)), const((1, n))],
        out_specs=[row(n), row(n), row(n)],
        out_shape=[jax.ShapeDtypeStruct((t, n), F32), jax.ShapeDtypeStruct((t, n), F32), jax.ShapeDtypeStruct((t, n), BF16)],
        compiler_params=_params("parallel"),
    )(a, w, res, g, b)


def _rowwise(name, fn, rows, consts, row_outs, acc_outs=(), tm=256):
    rows = [r if isinstance(r, tuple) else (r, r.shape[1]) for r in rows]
    t = rows[0][0].shape[0]
    tm = min(tm, t)
    assert t % tm == 0
    nr, nc, no, na = len(rows), len(consts), len(row_outs), len(acc_outs)

    def body(*refs):
        vals = [r[...] for r in refs[:nr + nc]]
        outs, accs = fn(*vals)
        for o_ref, o in zip(refs[nr + nc:nr + nc + no], outs):
            o_ref[...] = o.astype(o_ref.dtype)
        if na:
            step = pl.program_id(0)
            acc_refs = refs[nr + nc + no:]

            @pl.when(step == 0)
            def _():
                for a_ref, a in zip(acc_refs, accs):
                    a_ref[...] = a

            @pl.when(step > 0)
            def _():
                for a_ref, a in zip(acc_refs, accs):
                    a_ref[...] += a

    in_specs = [pl.BlockSpec((tm, w), lambda i: (i, 0)) for _, w in rows]
    in_specs += [pl.BlockSpec(c.shape, lambda i, nd=c.ndim: (0,) * nd) for c in consts]
    out_specs = [pl.BlockSpec((tm, c), lambda i: (i, 0)) for c, _ in row_outs]
    out_specs += [pl.BlockSpec(s, lambda i: (0, 0)) for s in acc_outs]
    out_shape = [jax.ShapeDtypeStruct((t, c), dt) for c, dt in row_outs]
    out_shape += [jax.ShapeDtypeStruct(s, F32) for s in acc_outs]
    res = pl.pallas_call(
        body, name=name, grid=(t // tm,), in_specs=in_specs, out_specs=out_specs, out_shape=out_shape,
        compiler_params=_params("arbitrary" if na else "parallel"),
    )(*[r for r, _ in rows], *consts)
    return res


def _ln_bwd(name, r, g, b, dy):
    def fn(r_v, dy_v, g_v, b_v):
        _, vjp = jax.vjp(_layer_norm, r_v, g_v, b_v)
        dr, dg, db = vjp(dy_v)
        return [dr, dr], [dg, db]
    return _rowwise(name, fn, [r, dy], [g, b], [(r.shape[1], F32), (r.shape[1], BF16)], [(1, r.shape[1])] * 2)


def _ln_loss_bwd(name, r, g, b, target):
    def fn(r_v, t_v, g_v, b_v):
        def loss_fn(rr, gg, bb):
            err = jnp.square(_layer_norm(rr, gg, bb) - t_v)
            return 0.5 * jnp.sum(jnp.mean(err, axis=-1, keepdims=True), axis=0, keepdims=True)
        loss, vjp = jax.vjp(loss_fn, r_v, g_v, b_v)
        dr, dg, db = vjp(jnp.ones((1, 1), F32))
        return [dr, dr], [dg, db, jnp.broadcast_to(loss, (1, LANES))]
    return _rowwise(name, fn, [r, target], [g, b], [(r.shape[1], F32), (r.shape[1], BF16)],
                    [(1, r.shape[1])] * 2 + [(1, LANES)])


def _rope_tables(posf, invf, sgn):
    ang = posf * invf
    return jnp.cos(ang), jnp.sin(ang) * sgn


def _rope_apply(tv, cos, sin):
    lane = lax.broadcasted_iota(jnp.int32, cos.shape, 1)
    first = (lane % HEAD_DIM) < (ROPE_DIM // 2)
    outs = []
    for gidx in range(tv.shape[1] // LANES):
        tg = tv[:, LANES * gidx:LANES * (gidx + 1)]
        sw = jnp.where(first, pltpu.roll(tg, LANES - ROPE_DIM // 2, 1), pltpu.roll(tg, ROPE_DIM // 2, 1))
        outs.append(tg * cos + sw * sin)
    return jnp.concatenate(outs, axis=1)


def _rope_fwd(qk, posf, invf, sgn):
    def fn(qk_v, pos_v, invf_v, sgn_v):
        cos, sin = _rope_tables(pos_v, invf_v, sgn_v)
        q = _rope_apply(qk_v[:, :D_ATTN], cos, sin) * (HEAD_DIM ** -0.5)
        k = _rope_apply(qk_v[:, D_ATTN:], cos, sin)
        return [q, k, jnp.concatenate([cos, sin], axis=1)], []
    return _rowwise("rope_fwd", fn, [qk, posf], [invf, sgn], [(D_ATTN, BF16), (D_ATTN, BF16), (2 * LANES, F32)])


def _rope_bwd(dq, dk, cs):
    def fn(dq_v, dk_v, cs_v):
        cos, sin = cs_v[:, :LANES], -cs_v[:, LANES:]
        gq = _rope_apply(dq_v * (HEAD_DIM ** -0.5), cos, sin)
        gk = _rope_apply(dk_v, cos, sin)
        return [jnp.concatenate([gq, gk], axis=1)], []
    return _rowwise("rope_bwd", fn, [dq, dk, cs], [], [(2 * D_ATTN, BF16)])[0]


def _rms(v, w):
    return v * lax.rsqrt(jnp.mean(v * v, axis=-1, keepdims=True) + RMS_EPS) * w


def _ungroup(yg):
    w = HEADS_PER_GROUP * HEAD_DIM
    return jnp.concatenate([yg[:, GROUP_LANES * g:GROUP_LANES * g + w] for g in range(N_GROUPS)], axis=1)


def _group(xs):
    w = HEADS_PER_GROUP * HEAD_DIM
    parts = []
    for g in range(N_GROUPS):
        parts += [xs[:, w * g:w * (g + 1)], jnp.zeros((xs.shape[0], GROUP_LANES - w), xs.dtype)]
    return jnp.concatenate(parts, axis=1)


def _norms_fn(attn, yg, xs, z, w_attn, w_ssd, dskip):
    a_n = _rms(attn, w_attn)
    y = _ungroup(yg) + dskip * xs
    y_n = _rms(y * (z * jax.nn.sigmoid(z)), w_ssd)
    return jnp.concatenate([a_n, y_n], axis=1)


def _norms_fwd(attn, yg, xbc, z, w_attn, w_ssd, dskip):
    def fn(*v):
        return [_norms_fn(*v)], []
    return _rowwise("norms_fwd", fn, [attn, yg, (xbc, D_SSD), z], [w_attn, w_ssd, dskip], [(D_ATTN + D_SSD, BF16)])[0]


def _norms_bwd(attn, yg, xbc, z, w_attn, w_ssd, dskip, dcat):
    def fn(attn_v, yg_v, xs_v, z_v, dcat_v, wa_v, ws_v, dk_v):
        _, vjp = jax.vjp(_norms_fn, attn_v, yg_v, xs_v, z_v, wa_v, ws_v, dk_v)
        d_attn, d_yg, d_xs, d_z, d_wa, d_ws, d_dk = vjp(dcat_v)
        return [d_attn, d_yg, d_xs, d_z], [d_wa, d_ws, d_dk]
    return _rowwise("norms_bwd", fn, [attn, yg, (xbc, D_SSD), z, dcat], [w_attn, w_ssd, dskip],
                    [(D_ATTN, F32), (N_GROUPS * GROUP_LANES, F32), (D_SSD, F32), (D_SSD, BF16)], [(1, D_SSD)] * 3)


def _ssd_prep_fn(xs, dtp, dtb, alog, e_x, e_a):
    dt = jax.nn.softplus(dtp + dtb)
    a = -jnp.exp(alog)
    dtg = jnp.dot(dt, e_x, precision=HIGHEST, preferred_element_type=F32)
    xdtg = _group(xs) * dtg
    dag = jnp.dot(dt * a, e_a, precision=HIGHEST, preferred_element_type=F32)
    return xdtg, dag


def _ssd_prep_fwd(xbc, dtp, dtb, alog, e_x, e_a):
    def fn(xbc_v, dtp_v, dtb_v, alog_v, ex_v, ea_v):
        xdtg, dag = _ssd_prep_fn(xbc_v[:, :D_SSD], dtp_v, dtb_v, alog_v, ex_v, ea_v)
        return [xdtg, xbc_v[:, D_SSD:], dag], []
    return _rowwise("ssd_prep_fwd", fn, [xbc, dtp], [dtb, alog, e_x, e_a],
                    [(N_GROUPS * GROUP_LANES, BF16), (D_CONV - D_SSD, BF16), (N_GROUPS * LANES, F32)])


def _ssd_prep_bwd(xbc, dtp, dtb, alog, e_x, e_a, dxdtg, ddag, dxs_a, db, dc):
    def fn(xs_v, dtp_v, dxdtg_v, ddag_v, dxs_a_v, db_v, dc_v, dtb_v, alog_v, ex_v, ea_v):
        _, vjp = jax.vjp(lambda a, b, c, d: _ssd_prep_fn(a, b, c, d, ex_v, ea_v), xs_v, dtp_v, dtb_v, alog_v)
        dxs, ddtp, ddtb, dalog = vjp((dxdtg_v, ddag_v))
        return [jnp.concatenate([dxs + dxs_a_v, db_v, dc_v], axis=1), ddtp], [ddtb, dalog]
    return _rowwise("ssd_prep_bwd", fn, [(xbc, D_SSD), dtp, dxdtg, ddag, dxs_a, db, dc], [dtb, alog, e_x, e_a],
                    [(D_CONV, F32), (LANES, BF16)], [(1, LANES)] * 2)


def _shift_down(u, d):
    if d == 0:
        return u
    row = lax.broadcasted_iota(jnp.int32, u.shape, 0)
    return jnp.where(row >= d, pltpu.roll(u, d, 0), 0.0)


def _shift_up(u, d):
    if d == 0:
        return u
    s = u.shape[0]
    row = lax.broadcasted_iota(jnp.int32, u.shape, 0)
    return jnp.where(row < s - d, pltpu.roll(u, s - d, 0), 0.0)


def _conv_pre(u, w, b):
    acc = b
    for k in range(CONV_WIDTH):
        acc = acc + w[k:k + 1, :] * _shift_down(u, CONV_WIDTH - 1 - k)
    return acc


def _conv_fwd(u, w, b, *, tc=256):
    nb, s, c = u.shape

    def body(u_ref, w_ref, b_ref, o_ref):
        pre = _conv_pre(u_ref[0], w_ref[...], b_ref[...])
        o_ref[0] = pre * jax.nn.sigmoid(pre)

    return pl.pallas_call(
        body, name="conv_fwd", grid=(c // tc, nb),
        in_specs=[pl.BlockSpec((1, s, tc), lambda j, i: (i, 0, j)), pl.BlockSpec((CONV_WIDTH, tc), lambda j, i: (0, j)),
                  pl.BlockSpec((1, tc), lambda j, i: (0, j))],
        out_specs=pl.BlockSpec((1, s, tc), lambda j, i: (i, 0, j)),
        out_shape=jax.ShapeDtypeStruct((nb, s, c), F32),
        compiler_params=_params("parallel", "parallel"),
    )(u, w, b)


def _conv_bwd(u, w, b, dout, *, tc=256):
    nb, s, c = u.shape

    def body(u_ref, w_ref, b_ref, d_ref, du_ref, dw_ref, db_ref):
        uv, wv = u_ref[0], w_ref[...]
        pre = _conv_pre(uv, wv, b_ref[...])
        sig = jax.nn.sigmoid(pre)
        dpre = d_ref[0] * (sig * (1.0 + pre * (1.0 - sig)))
        du = jnp.zeros_like(uv)
        dws = []
        for k in range(CONV_WIDTH):
            du = du + wv[k:k + 1, :] * _shift_up(dpre, CONV_WIDTH - 1 - k)
            dws.append(jnp.sum(dpre * _shift_down(uv, CONV_WIDTH - 1 - k), axis=0, keepdims=True))
        du_ref[0] = _bf(du)
        dwv = jnp.concatenate(dws + [jnp.zeros((8 - CONV_WIDTH, tc), F32)], axis=0)
        dbv = jnp.sum(dpre, axis=0, keepdims=True)
        first = pl.program_id(1) == 0

        @pl.when(first)
        def _():
            dw_ref[...] = dwv
            db_ref[...] = dbv

        @pl.when(jnp.logical_not(first))
        def _():
            dw_ref[...] += dwv
            db_ref[...] += dbv

    blk = pl.BlockSpec((1, s, tc), lambda j, i: (i, 0, j))
    return pl.pallas_call(
        body, name="conv_bwd", grid=(c // tc, nb),
        in_specs=[blk, pl.BlockSpec((CONV_WIDTH, tc), lambda j, i: (0, j)), pl.BlockSpec((1, tc), lambda j, i: (0, j)), blk],
        out_specs=[blk, pl.BlockSpec((8, tc), lambda j, i: (0, j)), pl.BlockSpec((1, tc), lambda j, i: (0, j))],
        out_shape=[jax.ShapeDtypeStruct((nb, s, c), BF16), jax.ShapeDtypeStruct((8, c), F32), jax.ShapeDtypeStruct((1, c), F32)],
        compiler_params=_params("parallel", "arbitrary"),
    )(u, w, b, dout)


FWD_KEY_BLOCK = 256


def _branch_bias_table(seq, kb):
    ratio = SEQ_BLOCK // kb
    key = np.arange(kb)[None, :, None]
    query = np.arange(SEQ_BLOCK)[None, None, :]
    delta = (np.arange(seq // kb)[:, None, None] - (ratio - 1)) * kb + query - key
    cnt = np.zeros(delta.shape, np.float64)
    for window, dilation in ((128, 1), (512, 4), (2048, 16)):
        cnt += (delta >= 0) & (delta % dilation == 0) & (delta <= window)
    return jnp.asarray(np.where(cnt > 0, np.log(np.maximum(cnt, 1.0)), NEG).astype(np.float32))


HEADS_PER_BLOCK = LANES // HEAD_DIM


def _head_rows(v, h):
    row = lax.broadcasted_iota(jnp.int32, v.shape, 0)
    return jnp.where((row >= HEAD_DIM * h) & (row < HEAD_DIM * (h + 1)), v, jnp.zeros_like(v))


def _attn_fwd(q, k, v, bias):
    nb_, s, _ = q.shape
    ab, kb = SEQ_BLOCK, FWD_KEY_BLOCK
    nblk, nkb, ratio = s // ab, s // kb, ab // kb

    def body(q_ref, k_ref, v_ref, b_ref, o_ref, lse_ref, vt_s):
        i = pl.program_id(2)

        @pl.when(i == 0)
        def _():
            for jb in range(nkb):
                vt_s[jb] = v_ref[0, kb * jb:kb * (jb + 1), :].T

        qt = q_ref[0].T
        qts = [_head_rows(qt, h) for h in range(HEADS_PER_BLOCK)]

        def step(j, carry):
            ks = pl.ds(pl.multiple_of(j * kb, kb), kb)
            kj = k_ref[0, ks, :]
            lb = b_ref[ratio * i - j + (ratio - 1)]
            out = []
            for h in range(HEADS_PER_BLOCK):
                m, l, acc = carry[3 * h:3 * h + 3]
                st = jnp.dot(kj, qts[h], preferred_element_type=F32) + lb
                m_new = jnp.maximum(m, jnp.max(st, axis=0, keepdims=True))
                p = jnp.exp(st - m_new)
                a = jnp.exp(m - m_new)
                l = a * l + jnp.sum(p, axis=0, keepdims=True)
                vt = vt_s[j, HEAD_DIM * h:HEAD_DIM * (h + 1), :]
                acc = a * acc + jnp.dot(vt, _bf(p), preferred_element_type=F32)
                out += [m_new, l, acc]
            return tuple(out)

        init = (jnp.full((1, ab), NEG, F32), jnp.zeros((1, ab), F32), jnp.zeros((HEAD_DIM, ab), F32)) * HEADS_PER_BLOCK
        res = lax.fori_loop(0, ratio * (i + 1), step, init)
        ot = jnp.concatenate([res[3 * h + 2] / res[3 * h + 1] for h in range(HEADS_PER_BLOCK)], axis=0)
        o_ref[0] = ot.T
        rows = [res[3 * h] + jnp.log(res[3 * h + 1]) for h in range(HEADS_PER_BLOCK)]
        lse_ref[0, 0, 0] = jnp.concatenate(rows + [jnp.zeros((8 - HEADS_PER_BLOCK, ab), F32)], axis=0)

    qblk = pl.BlockSpec((1, ab, LANES), lambda b, hp, i: (b, i, hp))
    full = pl.BlockSpec((1, s, LANES), lambda b, hp, i: (b, 0, hp))
    return pl.pallas_call(
        body, name="attn_fwd", grid=(nb_, D_ATTN // LANES, nblk),
        in_specs=[qblk, full, full, pl.BlockSpec((nkb, kb, ab), lambda b, hp, i: (0, 0, 0))],
        out_specs=[qblk, pl.BlockSpec((1, 1, 1, 8, ab), lambda b, hp, i: (b, hp, i, 0, 0))],
        out_shape=[jax.ShapeDtypeStruct((nb_, s, D_ATTN), F32),
                   jax.ShapeDtypeStruct((nb_, D_ATTN // LANES, nblk, 8, ab), F32)],
        scratch_shapes=[pltpu.VMEM((nkb, LANES, kb), BF16)],
        compiler_params=_params("parallel", "parallel", "arbitrary"),
    )(q, k, v, bias)


def _attn_bwd(q, k, v, o, do, lse, bias):
    nb_, s, _ = q.shape
    ab = SEQ_BLOCK
    nblk = s // ab

    nh = HEADS_PER_BLOCK

    def body(q_ref, k_ref, v_ref, o_ref, do_ref, lse_ref, b_ref, dq_ref, dk_ref, dv_ref,
             qt_s, dot_s, kt_s, dqt_s, do16_s, d_s, dk_acc, dv_acc):
        for jb in range(nblk):
            sl = slice(ab * jb, ab * (jb + 1))
            qt, kt = q_ref[0, sl, :].T, k_ref[0, sl, :].T
            do = do_ref[0, sl, :]
            dot = do.T
            prod = dot * o_ref[0, sl, :].T
            do16_s[sl, :] = _bf(do)
            for h in range(nh):
                qt_s[nh * jb + h] = _head_rows(qt, h)
                kt_s[nh * jb + h] = _head_rows(kt, h)
                dot_s[nh * jb + h] = _head_rows(_bf(dot), h)
            d_s[jb] = jnp.concatenate(
                [jnp.sum(prod[HEAD_DIM * h:HEAD_DIM * (h + 1)], axis=0, keepdims=True) for h in range(nh)]
                + [jnp.zeros((8 - nh, ab), F32)], axis=0)
            dqt_s[jb] = jnp.zeros((LANES, ab), F32)

        def outer(j, carry):
            ks = pl.ds(pl.multiple_of(j * ab, ab), ab)
            kj, vj = k_ref[0, ks, :], v_ref[0, ks, :]
            dk_acc[...] = jnp.zeros_like(dk_acc)
            dv_acc[...] = jnp.zeros_like(dv_acc)

            def inner(i, c2):
                qs = pl.ds(pl.multiple_of(i * ab, ab), ab)
                qi, doi = q_ref[0, qs, :], do16_s[qs, :]
                lb = b_ref[i - j]
                for h in range(nh):
                    st = jnp.dot(kj, qt_s[nh * i + h], preferred_element_type=F32) + lb
                    pt = jnp.exp(st - lse_ref[0, 0, i, h:h + 1, :])
                    dpt = jnp.dot(vj, dot_s[nh * i + h], preferred_element_type=F32)
                    dst16 = _bf(pt * (dpt - d_s[i, h:h + 1, :]))
                    dv_acc[h] += jnp.dot(_bf(pt), doi, preferred_element_type=F32)
                    dk_acc[h] += jnp.dot(dst16, qi, preferred_element_type=F32)
                    dqt_s[i] += jnp.dot(kt_s[nh * j + h], dst16, preferred_element_type=F32)
                return c2

            lax.fori_loop(j, nblk, inner, 0)
            lane = lax.broadcasted_iota(jnp.int32, (ab, LANES), 1)
            dk_ref[0, ks, :] = jnp.where(lane < HEAD_DIM, dk_acc[0], dk_acc[1])
            dv_ref[0, ks, :] = _bf(jnp.where(lane < HEAD_DIM, dv_acc[0], dv_acc[1]))
            return carry

        lax.fori_loop(0, nblk, outer, 0)
        for jb in range(nblk):
            dq_ref[0, ab * jb:ab * (jb + 1), :] = dqt_s[jb].T

    assert nh == 2
    full = pl.BlockSpec((1, s, LANES), lambda b, hp: (b, 0, hp))
    return pl.pallas_call(
        body, name="attn_bwd", grid=(nb_, D_ATTN // LANES),
        in_specs=[full] * 5 + [pl.BlockSpec((1, 1, nblk, 8, ab), lambda b, hp: (b, hp, 0, 0, 0)),
                               pl.BlockSpec((nblk, ab, ab), lambda b, hp: (0, 0, 0))],
        out_specs=[full, full, full],
        out_shape=[jax.ShapeDtypeStruct((nb_, s, D_ATTN), F32), jax.ShapeDtypeStruct((nb_, s, D_ATTN), F32),
                   jax.ShapeDtypeStruct((nb_, s, D_ATTN), BF16)],
        scratch_shapes=[pltpu.VMEM((nh * nblk, LANES, ab), BF16), pltpu.VMEM((nh * nblk, LANES, ab), BF16),
                        pltpu.VMEM((nh * nblk, LANES, ab), BF16), pltpu.VMEM((nblk, LANES, ab), F32),
                        pltpu.VMEM((s, LANES), BF16), pltpu.VMEM((nblk, 8, ab), F32),
                        pltpu.VMEM((nh, ab, LANES), F32), pltpu.VMEM((nh, ab, LANES), F32)],
        compiler_params=_params("parallel", "parallel"),
    )(q, k, v, o, do, lse, bias)


def _cumsum_fwd(dag):
    nb_, s, c = dag.shape
    ab = SEQ_BLOCK

    def body(a_ref, o_ref, ot_ref):
        r = lax.broadcasted_iota(jnp.int32, (ab, ab), 0)
        cc = lax.broadcasted_iota(jnp.int32, (ab, ab), 1)
        tri = (r >= cc).astype(F32)
        carry = jnp.zeros((1, c), F32)
        for i in range(s // ab):
            loc = jnp.dot(tri, a_ref[0, ab * i:ab * (i + 1), :], precision=HIGHEST, preferred_element_type=F32) + carry
            o_ref[0, ab * i:ab * (i + 1), :] = loc
            ot_ref[0, :, ab * i:ab * (i + 1)] = loc.T
            carry = loc[ab - 1:ab, :]

    return pl.pallas_call(
        body, name="ssd_cumsum", grid=(nb_,),
        in_specs=[pl.BlockSpec((1, s, c), lambda b: (b, 0, 0))],
        out_specs=[pl.BlockSpec((1, s, c), lambda b: (b, 0, 0)), pl.BlockSpec((1, c, s), lambda b: (b, 0, 0))],
        out_shape=[jax.ShapeDtypeStruct((nb_, s, c), F32), jax.ShapeDtypeStruct((nb_, c, s), F32)],
        compiler_params=_params("parallel"),
    )(dag)


def _cumsum_bwd(dcol, drow):
    nb_, s, c = dcol.shape
    ab = SEQ_BLOCK

    def body(c_ref, r_ref, o_ref):
        r = lax.broadcasted_iota(jnp.int32, (ab, ab), 0)
        cc = lax.broadcasted_iota(jnp.int32, (ab, ab), 1)
        tri = (r <= cc).astype(F32)
        carry = jnp.zeros((1, c), F32)
        for i in reversed(range(s // ab)):
            rows = r_ref[0, :, ab * i:ab * (i + 1)].T
            parts = []
            for g in range(N_GROUPS):
                parts += [rows[:, 8 * g:8 * (g + 1)], jnp.zeros((ab, LANES - 8), F32)]
            blk = c_ref[0, ab * i:ab * (i + 1), :] + jnp.concatenate(parts, axis=1)
            loc = jnp.dot(tri, blk, precision=HIGHEST, preferred_element_type=F32) + carry
            o_ref[0, ab * i:ab * (i + 1), :] = loc
            carry = loc[0:1, :]

    return pl.pallas_call(
        body, name="ssd_cumsum_bwd", grid=(nb_,),
        in_specs=[pl.BlockSpec((1, s, c), lambda b: (b, 0, 0)), pl.BlockSpec((1, N_GROUPS * 8, s), lambda b: (b, 0, 0))],
        out_specs=pl.BlockSpec((1, s, c), lambda b: (b, 0, 0)),
        out_shape=jax.ShapeDtypeStruct((nb_, s, c), F32),
        compiler_params=_params("parallel"),
    )(dcol, drow)


def _causal_ok(i, j):
    ab = SEQ_BLOCK
    r = lax.broadcasted_iota(jnp.int32, (ab, ab), 0)
    c = lax.broadcasted_iota(jnp.int32, (ab, ab), 1)
    return (r + (i - j) * ab) >= c


def _causal_ok_t(i, j):
    ab = SEQ_BLOCK
    r = lax.broadcasted_iota(jnp.int32, (ab, ab), 0)
    c = lax.broadcasted_iota(jnp.int32, (ab, ab), 1)
    return (c + (i - j) * ab) >= r


def _ssd_fwd(xdtg, bc, acum, acum_t):
    nb_, s, _ = xdtg.shape
    ab = SEQ_BLOCK

    def body(x_ref, b_ref, c_ref, ac_ref, at_ref, y_ref):
        i = pl.program_id(2)
        ci = c_ref[0]
        acol = [ac_ref[0, :, j:j + 1] for j in range(HEADS_PER_GROUP)]

        def step(jb, accs):
            ks = pl.ds(pl.multiple_of(jb * ab, ab), ab)
            cb = lax.dot_general(ci, b_ref[0, ks, :], _NT, preferred_element_type=F32)
            ok = _causal_ok(i, jb)
            new = []
            for j in range(HEADS_PER_GROUP):
                decay = jnp.exp(jnp.where(ok, acol[j] - at_ref[0, j:j + 1, ks], NEG))
                g = _bf(cb * decay)
                new.append(accs[j] + jnp.dot(g, x_ref[0, ks, HEAD_DIM * j:HEAD_DIM * (j + 1)], preferred_element_type=F32))
            return tuple(new)

        accs = lax.fori_loop(0, i + 1, step, tuple(jnp.zeros((ab, HEAD_DIM), F32) for _ in range(HEADS_PER_GROUP)))
        y_ref[0] = jnp.concatenate(list(accs) + [jnp.zeros((ab, GROUP_LANES - HEADS_PER_GROUP * HEAD_DIM), F32)], axis=1)

    return pl.pallas_call(
        body, name="ssd_fwd", grid=(nb_, N_GROUPS, s // ab),
        in_specs=[pl.BlockSpec((1, s, GROUP_LANES), lambda b, g, i: (b, 0, g)),
                  pl.BlockSpec((1, s, D_STATE), lambda b, g, i: (b, 0, g)),
                  pl.BlockSpec((1, ab, D_STATE), lambda b, g, i: (b, i, N_GROUPS + g)),
                  pl.BlockSpec((1, ab, LANES), lambda b, g, i: (b, i, g)),
                  pl.BlockSpec((1, 8, s), lambda b, g, i: (b, (LANES // 8) * g, 0))],
        out_specs=pl.BlockSpec((1, ab, GROUP_LANES), lambda b, g, i: (b, i, g)),
        out_shape=jax.ShapeDtypeStruct((nb_, s, N_GROUPS * GROUP_LANES), F32),
        compiler_params=_params("parallel", "parallel", "parallel"),
    )(xdtg, bc, bc, acum, acum_t)


def _ssd_bwd(xdtg, bc, acum, acum_t, dyg):
    nb_, s, _ = xdtg.shape
    ab = SEQ_BLOCK
    nblk = s // ab
    hpg = HEADS_PER_GROUP

    def body(x_ref, b_ref, c_ref, ac_ref, at_ref, dy_ref, dx_ref, db_ref, dc_ref, dac_ref, dar_ref):
        dx_ref[...] = jnp.zeros_like(dx_ref)
        db_ref[...] = jnp.zeros_like(db_ref)
        dac_ref[...] = jnp.zeros_like(dac_ref)
        dar_ref[...] = jnp.zeros_like(dar_ref)

        def outer(i, carry):
            qs = pl.ds(pl.multiple_of(i * ab, ab), ab)
            ci = c_ref[0, qs, :]
            dyi = [_bf(dy_ref[0, qs, HEAD_DIM * j:HEAD_DIM * (j + 1)]) for j in range(hpg)]
            arow = [at_ref[0, j:j + 1, qs] for j in range(hpg)]

            def inner(jb, st):
                dc_acc, rs = st[0], list(st[1:])
                ks = pl.ds(pl.multiple_of(jb * ab, ab), ab)
                bj = b_ref[0, ks, :]
                cbt = lax.dot_general(bj, ci, _NT, preferred_element_type=F32)
                ok = _causal_ok_t(i, jb)
                dcbt = jnp.zeros((ab, ab), F32)
                for j in range(hpg):
                    hs = slice(HEAD_DIM * j, HEAD_DIM * (j + 1))
                    decay = jnp.exp(jnp.where(ok, arow[j] - ac_ref[0, ks, j:j + 1], NEG))
                    gt = cbt * decay
                    dgt = lax.dot_general(x_ref[0, ks, hs], dyi[j], _NT, preferred_element_type=F32)
                    dx_ref[0, ks, hs] += jnp.dot(_bf(gt), dyi[j], preferred_element_type=F32)
                    dcbt = dcbt + dgt * decay
                    mm = dgt * gt
                    rs[j] = rs[j] + jnp.sum(mm, axis=0, keepdims=True)
                    dac_ref[0, ks, j:j + 1] -= jnp.sum(mm, axis=1, keepdims=True)
                dcbt16 = _bf(dcbt)
                db_ref[0, ks, :] += jnp.dot(dcbt16, ci, preferred_element_type=F32)
                return (dc_acc + lax.dot_general(dcbt16, bj, _TN, preferred_element_type=F32), *rs)

            init = (jnp.zeros((ab, D_STATE), F32),) + tuple(jnp.zeros((1, ab), F32) for _ in range(hpg))
            st = lax.fori_loop(0, i + 1, inner, init)
            dc_ref[0, qs, :] = st[0]
            for j in range(hpg):
                dar_ref[0, j:j + 1, qs] = st[1 + j]
            return carry

        lax.fori_loop(0, nblk, outer, 0)

    xblk = pl.BlockSpec((1, s, GROUP_LANES), lambda b, g: (b, 0, g))
    sblk = pl.BlockSpec((1, s, D_STATE), lambda b, g: (b, 0, g))
    tblk = pl.BlockSpec((1, 8, s), lambda b, g: (b, (LANES // 8) * g, 0))
    return pl.pallas_call(
        body, name="ssd_bwd", grid=(nb_, N_GROUPS),
        in_specs=[xblk, sblk, pl.BlockSpec((1, s, D_STATE), lambda b, g: (b, 0, N_GROUPS + g)), sblk, tblk, xblk],
        out_specs=[xblk, sblk, sblk, sblk, pl.BlockSpec((1, 8, s), lambda b, g: (b, g, 0))],
        out_shape=[jax.ShapeDtypeStruct((nb_, s, N_GROUPS * GROUP_LANES), F32),
                   jax.ShapeDtypeStruct((nb_, s, N_GROUPS * D_STATE), F32),
                   jax.ShapeDtypeStruct((nb_, s, N_GROUPS * D_STATE), F32),
                   jax.ShapeDtypeStruct((nb_, s, N_GROUPS * LANES), F32),
                   jax.ShapeDtypeStruct((nb_, N_GROUPS * 8, s), F32)],
        compiler_params=_params("parallel", "parallel"),
    )(xdtg, bc, bc, acum, acum_t, dyg)


def _interleave(wg, wu):
    k, f = wg.shape
    gi = GATE_UP_INTERLEAVE
    return jnp.stack([wg.reshape(k, f // gi, gi), wu.reshape(k, f // gi, gi)], axis=2).reshape(k, 2 * f)


def _head_expanders():
    e_x = np.zeros((LANES, N_GROUPS * GROUP_LANES), np.float32)
    e_a = np.zeros((LANES, N_GROUPS * LANES), np.float32)
    for h in range(N_HEADS):
        g, j = divmod(h, HEADS_PER_GROUP)
        e_x[h, GROUP_LANES * g + HEAD_DIM * j:GROUP_LANES * g + HEAD_DIM * (j + 1)] = 1.0
        e_a[h, LANES * g + j] = 1.0
    return jnp.asarray(e_x), jnp.asarray(e_a)


def _pad_lanes(v, n=LANES):
    return jnp.pad(v, ((0, 0), (0, n - v.shape[1])))


def _local_step(x, positions, target, w, late_job=None, late_weights=None, early_grad_job=None):
    nb, s, d = x.shape
    t = nb * s
    x2 = x.reshape(t, d)
    tgt2 = target.reshape(t, d)

    x16 = _bf(x2)
    wgu1 = _interleave(w["ffn1_gate"], w["ffn1_up"])
    ffn1 = _ffn_fwd("ffn1_fwd", x16, x2, wgu1, w["ffn1_down"], w["ln1_g"], w["ln1_b"], carry=late_job)
    au1, hm1, h1, r1, h1_16 = ffn1[:5]
    if late_job is not None:
        w = {**w, **late_weights(ffn1[5])}

    wgu2 = _interleave(w["ffn2_gate"], w["ffn2_up"])
    w_in = w["w_in"]
    wqk, wv, wz = w_in[:, :2 * D_ATTN], w_in[:, 2 * D_ATTN:3 * D_ATTN], w_in[:, 3 * D_ATTN:3 * D_ATTN + D_SSD]
    wxbc = w_in[:, 3 * D_ATTN + D_SSD:3 * D_ATTN + D_SSD + D_CONV]
    wdt = _pad_lanes(w_in[:, 3 * D_ATTN + D_SSD + D_CONV:])

    inv_freq = ROPE_THETA ** (-jnp.arange(0, ROPE_DIM, 2, dtype=F32) / ROPE_DIM)
    half = ROPE_DIM // 2
    head_invf = jnp.concatenate([inv_freq, inv_freq, jnp.zeros((HEAD_DIM - ROPE_DIM,), F32)])
    head_sgn = jnp.concatenate([-jnp.ones((half,), F32), jnp.ones((half,), F32), jnp.zeros((HEAD_DIM - ROPE_DIM,), F32)])
    invf = jnp.tile(head_invf, LANES // HEAD_DIM)[None, :]
    sgn = jnp.tile(head_sgn, LANES // HEAD_DIM)[None, :]
    posf = positions.astype(F32).reshape(t, 1)
    bias_fwd, bias_bwd = _branch_bias_table(s, FWD_KEY_BLOCK), _branch_bias_table(s, SEQ_BLOCK)
    e_x, e_a = _head_expanders()
    dtb, alog = _pad_lanes(w["dt_bias"]), _pad_lanes(w["a_log"])
    dskip = jnp.repeat(w["d_skip"], HEAD_DIM, axis=1)

    qk =_mm("proj_qk", [(h1_16, wqk)], tn=768)
    v16 = _mm("proj_v", [(h1_16, wv)], tn=768, out_dtype=BF16)
    z = _mm("proj_z", [(h1_16, wz)], tn=768)
    xbc_pre = _mm("proj_xbc", [(h1_16, wxbc)], tn=896)
    dtp = _mm("proj_dt", [(h1_16, wdt)], tn=LANES)

    q16, k16, cs = _rope_fwd(qk, posf, invf, sgn)
    to3 = lambda a: a.reshape(nb, s, a.shape[-1])
    attn_o, lse = _attn_fwd(to3(q16), to3(k16), to3(v16), bias_fwd)

    xbc = _conv_fwd(to3(xbc_pre), w["conv_w"], w["conv_b"]).reshape(t, D_CONV)
    xdtg, bc16, dag = _ssd_prep_fwd(xbc, dtp, dtb, alog, e_x, e_a)
    acum, acum_t = _cumsum_fwd(to3(dag))
    yg = _ssd_fwd(to3(xdtg), to3(bc16), acum, acum_t)

    cat = _norms_fwd(attn_o.reshape(t, D_ATTN), yg.reshape(t, -1), xbc, z, w["attn_norm_w"], w["ssd_norm_w"], dskip)
    h2, r2, h2_16 = _mm_res_ln("w_out_ln2", cat, w["w_out"], h1, w["ln2_g"], w["ln2_b"], scale=1.0)

    au2, hm2, _, r3, _ = _ffn_fwd("ffn2_fwd", h2_16, h2, wgu2, w["ffn2_down"], w["ln3_g"], w["ln3_b"])

    g = {}
    dr3, dr3_16, g["ln3_g"], g["ln3_b"], loss = _ln_loss_bwd("loss_ln3_bwd", r3, w["ln3_g"], w["ln3_b"], tgt2)

    dau2, dh2 = _ffn_bwd("ffn2_bwd", dr3_16, dr3, w["ffn2_down"].T, au2, wgu2.T)
    g["ffn2_down"] = _mm_tn("ffn2_down_dw", hm2, dr3_16, scale=0.5, tk=D_FF // 2, tn=512)
    g["ffn2_gate"], g["ffn2_up"] = _mm_tn_gate_up("ffn2_up_dw", h2_16, dau2)

    dr2, dr2_16, g["ln2_g"], g["ln2_b"] = _ln_bwd("ln2_bwd", r2, w["ln2_g"], w["ln2_b"], dh2)
    dcat = _mm("w_out_dx", [(dr2_16, w["w_out"].T)], tn=768)
    g["w_out"] = _mm_tn("w_out_dw", cat, dr2_16, tk=768, tn=1024)

    d_attn, dyg, dxs_a, dz16, g["attn_norm_w"], g["ssd_norm_w"], ddskip = _norms_bwd(
        attn_o.reshape(t, D_ATTN), yg.reshape(t, -1), xbc, z, w["attn_norm_w"], w["ssd_norm_w"], dskip, dcat)
    g["d_skip"] = ddskip.reshape(N_HEADS, HEAD_DIM).sum(axis=1)[None, :]

    dq, dk, dv16 = _attn_bwd(to3(q16), to3(k16), to3(v16), attn_o, to3(d_attn), lse, bias_bwd)
    dqk16 = _rope_bwd(dq.reshape(t, D_ATTN), dk.reshape(t, D_ATTN), cs)

    dxdtg, dbm, dcm, dacol, darow = _ssd_bwd(to3(xdtg), to3(bc16), acum, acum_t, to3(dyg))
    ddag = _cumsum_bwd(dacol, darow)
    dxbc, ddtp16, ddtb, dalog = _ssd_prep_bwd(xbc, dtp, dtb, alog, e_x, e_a, dxdtg.reshape(t, -1), ddag.reshape(t, -1),
                                               dxs_a, dbm.reshape(t, -1), dcm.reshape(t, -1))
    g["dt_bias"], g["a_log"] = ddtb[:, :N_HEADS], dalog[:, :N_HEADS]
    dxbc_pre16, dconv_w, g["conv_b"] = _conv_bwd(to3(xbc_pre), w["conv_w"], w["conv_b"], to3(dxbc))
    g["conv_w"] = dconv_w[:CONV_WIDTH]
    dxbc_pre16 = dxbc_pre16.reshape(t, D_CONV)
    dv16 = dv16.reshape(t, D_ATTN)

    dh1 = _mm("w_in_dx", [(dqk16, wqk.T), (dv16, wv.T), (dz16, wz.T), (dxbc_pre16, wxbc.T), (ddtp16, wdt.T)],
              res=dr2, res_scale=ALPHA)
    g["w_in"] = jnp.concatenate([
        _mm_tn("w_in_dw_qk", h1_16, dqk16, tk=1024, tn=512),
        _mm_tn("w_in_dw_v", h1_16, dv16, tk=1024, tn=768),
        _mm_tn("w_in_dw_z", h1_16, dz16, tk=1024, tn=768),
        _mm_tn("w_in_dw_xbc", h1_16, dxbc_pre16, tk=1024, tn=896),
        _mm_tn("w_in_dw_dt", h1_16, ddtp16, tk=1024, tn=LANES)[:, :N_HEADS],
    ], axis=1)

    dr1, dr1_16, g["ln1_g"], g["ln1_b"] = _ln_bwd("ln1_bwd", r1, w["ln1_g"], w["ln1_b"], dh1)
    ffn1b = _ffn_bwd("ffn1_bwd", dr1_16, dr1, w["ffn1_down"].T, au1, wgu1.T,
                     carry=None if early_grad_job is None else early_grad_job(g))
    dau1, dx = ffn1b[:2]
    early = ffn1b[2] if early_grad_job is not None else None
    g["ffn1_down"] = _mm_tn("ffn1_down_dw", hm1, dr1_16, scale=0.5, tk=D_FF // 2, tn=512)
    g["ffn1_gate"], g["ffn1_up"] = _mm_tn_gate_up("ffn1_up_dw", x16, dau1)
    return loss, dx.reshape(nb, s, d), g, early


_HBM = pl.BlockSpec(memory_space=pltpu.HBM)
N_CHIPS = 4
N_DEVICES = 8


def _place():
    return lax.axis_index("x"), lax.axis_index("y"), lax.axis_index("c")


def _other_chips(x, y):
    return [(1 - x, y), (x, 1 - y), (1 - x, 1 - y)]


class _GatherJob:
    def __init__(self, shards):
        assert all((a.shape[0] // 2) % 16 == 0 for a in shards)
        self.n = len(shards)
        self.shapes = [a.shape for a in shards]
        self.operands = [a.reshape(2, a.shape[0] // 2, a.shape[1]) for a in shards]
        self.out_shape = [jax.ShapeDtypeStruct((N_CHIPS,) + a.shape, a.dtype) for a in self.operands]
        pair = pltpu.SemaphoreType.DMA((self.n, N_CHIPS - 1))
        self.scratch_shapes = [pair, pair, pair, pair]

    def results(self, outs):
        return [o.reshape((N_CHIPS,) + s) for o, s in zip(outs, self.shapes)]

    def phases(self, ins, outs, sems):
        n = self.n
        send_sems, recv_sems, fwd_send_sems, fwd_recv_sems = sems
        x, y, c = _place()
        me = 2 * x + y
        peers = _other_chips(x, y)

        def ici(t, p, src_chip):
            px, py = peers[p]
            return pltpu.make_async_remote_copy(
                ins[t].at[c] if src_chip is None else outs[t].at[src_chip, c],
                outs[t].at[me if src_chip is None else src_chip, c],
                send_sems.at[t, p], recv_sems.at[t, p], device_id=(px, py, c), device_id_type=MESH)

        def d2d(t, p, core):
            px, py = peers[p]
            return pltpu.make_async_remote_copy(
                outs[t].at[2 * px + py, core], outs[t].at[2 * px + py, core],
                fwd_send_sems.at[t, p], fwd_recv_sems.at[t, p], device_id=(x, y, 1 - c), device_id_type=MESH)

        pairs = [(t, p) for t in range(n) for p in range(N_CHIPS - 1)]

        def start():
            for t, p in pairs:
                ici(t, p, None).start()

        def forward():
            for t, p in pairs:
                px, py = peers[p]
                ici(t, p, 2 * px + py).wait_recv()
                d2d(t, p, c).start()

        def finish():
            for t, p in pairs:
                d2d(t, p, 1 - c).wait_recv()
            for t, p in pairs:
                ici(t, p, None).wait_send()
                d2d(t, p, c).wait_send()

        return start, forward, finish


class _ExchangeJob:
    def __init__(self, stacks):
        self.n = len(stacks)
        self.operands = list(stacks)
        self.out_shape = [jax.ShapeDtypeStruct(a.shape, a.dtype) for a in stacks]
        pair = pltpu.SemaphoreType.DMA((self.n, N_CHIPS - 1))
        self.scratch_shapes = [pair, pair]

    def results(self, outs):
        return list(outs)

    def phases(self, ins, outs, sems):
        send_sems, recv_sems = sems
        x, y, c = _place()
        me = 2 * x + y
        peers = _other_chips(x, y)
        pairs = [(t, p) for t in range(self.n) for p in range(N_CHIPS - 1)]

        def copy(t, p):
            px, py = peers[p]
            return pltpu.make_async_remote_copy(ins[t].at[2 * px + py], outs[t].at[me], send_sems.at[t, p],
                                                recv_sems.at[t, p], device_id=(px, py, c), device_id_type=MESH)

        def arrival(t, p):
            px, py = peers[p]
            return pltpu.make_async_remote_copy(ins[t].at[me], outs[t].at[2 * px + py], send_sems.at[t, p],
                                                recv_sems.at[t, p], device_id=(px, py, c), device_id_type=MESH)

        def start():
            for t, p in pairs:
                copy(t, p).start()

        def finish():
            for t, p in pairs:
                arrival(t, p).wait_recv()
            for t, p in pairs:
                copy(t, p).wait_send()

        return start, None, finish


def _run_job(job, name):
    n = job.n

    def body(*refs):
        for phase in job.phases(refs[:n], refs[n:2 * n], refs[2 * n:]):
            if phase is not None:
                phase()

    outs = pl.pallas_call(
        body, name=name, in_specs=[_HBM] * n, out_specs=[_HBM] * n,
        out_shape=job.out_shape, scratch_shapes=job.scratch_shapes,
    )(*job.operands)
    return job.results(outs)


def _sibling_halves(stacks, name):
    n = len(stacks)
    halves = [a.shape[1] // 2 for a in stacks]
    split = [a.reshape(a.shape[0], 2, h, a.shape[2]) for a, h in zip(stacks, halves)]

    def body(*refs):
        ins, outs = refs[:n], refs[n:2 * n]
        send_sems, recv_sems = refs[2 * n:]
        x, y, c = _place()
        cps = []
        for t in range(n):
            cp = pltpu.make_async_remote_copy(ins[t].at[:, 1 - c], outs[t], send_sems.at[t], recv_sems.at[t],
                                              device_id=(x, y, 1 - c), device_id_type=MESH)
            cp.start()
            cps.append(cp)
        for cp in cps:
            cp.wait()

    return pl.pallas_call(
        body, name=name,
        in_specs=[_HBM] * n, out_specs=[_HBM] * n,
        out_shape=[jax.ShapeDtypeStruct((a.shape[0], h, a.shape[2]), a.dtype) for a, h in zip(stacks, halves)],
        scratch_shapes=[pltpu.SemaphoreType.DMA((n,)), pltpu.SemaphoreType.DMA((n,))],
    )(*split)


def _sibling_swap(arrs):
    n = len(arrs)

    def body(*refs):
        ins, outs = refs[:n], refs[n:2 * n]
        send_sems, recv_sems = refs[2 * n:]
        x, y, c = _place()
        cps = []
        for t in range(n):
            cp = pltpu.make_async_remote_copy(ins[t], outs[t], send_sems.at[t], recv_sems.at[t],
                                              device_id=(x, y, 1 - c), device_id_type=MESH)
            cp.start()
            cps.append(cp)
        for cp in cps:
            cp.wait()

    return pl.pallas_call(
        body, name="sibling_swap",
        in_specs=[_HBM] * n, out_specs=[_HBM] * n,
        out_shape=[jax.ShapeDtypeStruct(a.shape, a.dtype) for a in arrs],
        scratch_shapes=[pltpu.SemaphoreType.DMA((n,)), pltpu.SemaphoreType.DMA((n,))],
    )(*arrs)


def _half_sum(name, own, other, core):
    k, r, cols = own.shape
    h = r // 2
    tr = next(cand for cand in (128, 176, 64, 32, 16) if h % cand == 0)
    nblk = h // tr

    def body(core_ref, own_ref, other_ref, o_ref):
        o_ref[...] = _bf(own_ref[...] + other_ref[...].astype(F32))

    grid_spec = pltpu.PrefetchScalarGridSpec(
        num_scalar_prefetch=1, grid=(nblk,),
        in_specs=[pl.BlockSpec((k, tr, cols), lambda i, core_ref: (0, i + core_ref[0] * nblk, 0)),
                  pl.BlockSpec((k, tr, cols), lambda i, core_ref: (0, i, 0))],
        out_specs=pl.BlockSpec((k, tr, cols), lambda i, core_ref: (0, i, 0)))
    return pl.pallas_call(
        body, name=name, grid_spec=grid_spec, out_shape=jax.ShapeDtypeStruct((k, h, cols), BF16),
        compiler_params=_params("parallel"),
    )(core.reshape(1).astype(jnp.int32), own, other)


def _small_allreduce(v):
    r = v.shape[0]

    def body(v_ref, tot_ref, slots, send_sems, recv_sems):
        x, y, c = _place()
        me = 4 * x + 2 * y + c
        slots[me] = v_ref[...]
        cps, peers = [], []
        for k in range(1, N_DEVICES):
            px = 1 - x if (k >> 2) & 1 else x
            py = 1 - y if (k >> 1) & 1 else y
            pc = 1 - c if k & 1 else c
            cp = pltpu.make_async_remote_copy(v_ref, slots.at[me], send_sems.at[k - 1], recv_sems.at[k - 1],
                                              device_id=(px, py, pc), device_id_type=MESH)
            cp.start()
            cps.append(cp)
            peers.append((px, py, pc))
        for k, (px, py, pc) in enumerate(peers):
            pltpu.make_async_remote_copy(v_ref, slots.at[4 * px + 2 * py + pc], send_sems.at[k], recv_sems.at[k],
                                         device_id=(px, py, pc), device_id_type=MESH).wait_recv()
        for cp in cps:
            cp.wait_send()
        acc = slots[0]
        for s in range(1, N_DEVICES):
            acc = acc + slots[s]
        tot_ref[...] = acc

    return pl.pallas_call(
        body, name="small_allreduce",
        in_specs=[pl.BlockSpec(memory_space=pltpu.VMEM)], out_specs=pl.BlockSpec(memory_space=pltpu.VMEM),
        out_shape=jax.ShapeDtypeStruct((r, LANES), F32),
        scratch_shapes=[pltpu.VMEM((N_DEVICES, r, LANES), F32), pltpu.SemaphoreType.DMA((N_DEVICES - 1,)),
                        pltpu.SemaphoreType.DMA((N_DEVICES - 1,))],
    )(v)


def _elementwise(name, fn, ins, out_dtypes):
    r, c = ins[0].shape[-2:]
    tr = next((cand for cand in (256, 176, 128, 64, 32, 16) if r % cand == 0), r)
    nin = len(ins)

    def body(*refs):
        outs = fn(*[ref[...] for ref in refs[:nin]])
        for o_ref, o in zip(refs[nin:], outs):
            o_ref[...] = o.astype(o_ref.dtype)

    in_specs = [pl.BlockSpec((tr, c), lambda i: (i, 0)) if a.ndim == 2 else pl.BlockSpec((a.shape[0], tr, c), lambda i: (0, i, 0))
                for a in ins]
    return pl.pallas_call(
        body, name=name, grid=(r // tr,), in_specs=in_specs,
        out_specs=[pl.BlockSpec((tr, c), lambda i: (i, 0)) for _ in out_dtypes],
        out_shape=[jax.ShapeDtypeStruct((r, c), dt) for dt in out_dtypes],
        compiler_params=_params("parallel"),
    )(*ins)


def _row_tile(rows):
    return next((cand for cand in (128, 176, 64, 32, 16) if rows % cand == 0), rows)


def _sum_slots(name, received, own, chip):
    _, r, cols = own.shape
    tr = _row_tile(r)

    def body(chip_ref, own_ref, a_ref, b_ref, c_ref, o_ref):
        o_ref[...] = ((own_ref[0].astype(F32) + a_ref[0].astype(F32)) + b_ref[0].astype(F32)) + c_ref[0].astype(F32)

    def slot(flip):
        return pl.BlockSpec((1, tr, cols), lambda i, chip_ref: (jnp.bitwise_xor(chip_ref[0], flip), i, 0))

    grid_spec = pltpu.PrefetchScalarGridSpec(
        num_scalar_prefetch=1, grid=(r // tr,), in_specs=[slot(0), slot(1), slot(2), slot(3)],
        out_specs=pl.BlockSpec((tr, cols), lambda i, chip_ref: (i, 0)))
    return pl.pallas_call(
        body, name=name, grid_spec=grid_spec, out_shape=jax.ShapeDtypeStruct((r, cols), F32),
        compiler_params=_params("parallel"),
    )(chip.reshape(1).astype(jnp.int32), own, received, received, received)


def _adamw_halves(name, mine, theirs, core, w, m, v):
    h, cols = mine.shape
    tr = _row_tile(h)
    nh = h // tr

    def body(core_ref, mine_ref, theirs_ref, w_ref, m_ref, v_ref, g_ref, d_ref, m2_ref, v2_ref):
        is_mine = (pl.program_id(0) // nh) == core_ref[0]
        g = jnp.where(is_mine, mine_ref[...], theirs_ref[...])
        outs = _adamw_math(g, w_ref[...], m_ref[...], v_ref[...])
        for ref, val in zip((g_ref, d_ref, m2_ref, v2_ref), outs):
            ref[...] = val

    half = pl.BlockSpec((tr, cols), lambda i, core_ref: (i % nh, 0))
    full = pl.BlockSpec((tr, cols), lambda i, core_ref: (i, 0))
    grid_spec = pltpu.PrefetchScalarGridSpec(
        num_scalar_prefetch=1, grid=(2 * nh,), in_specs=[half, half, full, full, full], out_specs=[full] * 4)
    return pl.pallas_call(
        body, name=name, grid_spec=grid_spec, out_shape=[jax.ShapeDtypeStruct((2 * h, cols), F32)] * 4,
        compiler_params=_params("parallel"),
    )(core.reshape(1).astype(jnp.int32), mine, theirs, w, m, v)


def _adamw_math(g, w_v, m_v, v_v):
    m2 = ADAM_B1 * m_v + (1.0 - ADAM_B1) * g
    v2 = ADAM_B2 * v_v + (1.0 - ADAM_B2) * jnp.square(g)
    m_hat = m2 / (1.0 - ADAM_B1 ** ADAM_STEP)
    v_hat = v2 / (1.0 - ADAM_B2 ** ADAM_STEP)
    delta = -ADAM_LR * (m_hat / (jnp.sqrt(v_hat) + ADAM_EPS) + ADAM_WD * w_v)
    return [g, delta, m2, v2]


def _adamw(name, g, w, m, v):
    return _elementwise(name, _adamw_math, [g, w, m, v], [F32] * 4)


_MATRICES = (("ffn1_gate", 1), ("ffn1_up", 1), ("ffn1_down", 0), ("w_in", 1), ("w_out", 0),
             ("ffn2_gate", 1), ("ffn2_up", 1), ("ffn2_down", 0))
_VECTORS = ("ln1_g", "ln1_b", "conv_b", "dt_bias", "a_log", "d_skip", "attn_norm_w", "ssd_norm_w",
            "ln2_g", "ln2_b", "ln3_g", "ln3_b")
_WEIGHT_ORDER = ("ln1_g", "ln1_b", "ffn1_gate", "ffn1_up", "ffn1_down", "w_in", "conv_w", "conv_b", "dt_bias", "a_log",
                 "d_skip", "attn_norm_w", "ssd_norm_w", "w_out", "ln2_g", "ln2_b", "ffn2_gate", "ffn2_up", "ffn2_down",
                 "ln3_g", "ln3_b")


def _pack_rows(vectors):
    parts = []
    for vec in vectors:
        flat = vec.reshape(-1)
        parts.append(jnp.pad(flat, (0, (-flat.shape[0]) % LANES)))
    flat = jnp.concatenate(parts)
    flat = jnp.pad(flat, (0, (-flat.shape[0]) % (8 * LANES)))
    return flat.reshape(-1, LANES)


def _unpack_rows(packed, shapes):
    flat = packed.reshape(-1)
    out, off = [], 0
    for shape in shapes:
        size = int(np.prod(shape))
        out.append(flat[off:off + size].reshape(shape))
        off += size + (-size) % LANES
    return out


def _assemble(stack, own, chip, axis):
    blocks = [jnp.where(chip == s, own, stack[s]) for s in range(N_CHIPS)]
    return jnp.concatenate(blocks, axis=axis)


def _split(full, axis):
    if axis == 0:
        return full.reshape(N_CHIPS, -1, full.shape[1])
    cols = full.shape[1] // N_CHIPS
    return jnp.stack([full[:, cols * s:cols * (s + 1)] for s in range(N_CHIPS)])


def kernel(x, positions, ln1_g, ln1_b, ffn1_gate, ffn1_up, ffn1_down, w_in, conv_w, conv_b, dt_bias, a_log, d_skip, attn_norm_w, ssd_norm_w, w_out, ln2_g, ln2_b, ffn2_gate, ffn2_up, ffn2_down, ln3_g, ln3_b, loss_target, m_ln1_g, m_ln1_b, m_ffn1_gate, m_ffn1_up, m_ffn1_down, m_w_in, m_conv_w, m_conv_b, m_dt_bias, m_a_log, m_d_skip, m_attn_norm_w, m_ssd_norm_w, m_w_out, m_ln2_g, m_ln2_b, m_ffn2_gate, m_ffn2_up, m_ffn2_down, m_ln3_g, m_ln3_b, v_ln1_g, v_ln1_b, v_ffn1_gate, v_ffn1_up, v_ffn1_down, v_w_in, v_conv_w, v_conv_b, v_dt_bias, v_a_log, v_d_skip, v_attn_norm_w, v_ssd_norm_w, v_w_out, v_ln2_g, v_ln2_b, v_ffn2_gate, v_ffn2_up, v_ffn2_down, v_ln3_g, v_ln3_b):
    given = dict(locals())
    wts = {n: given[n] for n in _WEIGHT_ORDER}
    mom_m = {n: given["m_" + n] for n in _WEIGHT_ORDER}
    mom_v = {n: given["v_" + n] for n in _WEIGHT_ORDER}
    chip = 2 * lax.axis_index("x") + lax.axis_index("y")

    core = lax.axis_index("c")
    first = [(n, axis) for n, axis in _MATRICES if n.startswith("ffn1")]
    rest = [(n, axis) for n, axis in _MATRICES if not n.startswith("ffn1")]
    own16 = {n: wts[n][0].astype(BF16) for n, _ in _MATRICES}
    gathered = _run_job(_GatherJob([own16[n] for n, _ in first]), "gather_ffn1")
    full = {n: _assemble(st, own16[n], chip, axis) for (n, axis), st in zip(first, gathered)}
    for n in _VECTORS:
        full[n] = wts[n]
    conv_rows = jnp.pad(wts["conv_w"][0], ((0, 32 - CONV_WIDTH), (0, 0)))
    late_job = _GatherJob([own16[n] for n, _ in rest] + [conv_rows])

    def late_weights(results):
        out = {n: _assemble(st, own16[n], chip, axis) for (n, axis), st in zip(rest, results)}
        out["conv_w"] = _assemble(results[-1], conv_rows, chip, 1)[:CONV_WIDTH]
        return out

    chip_sums = {}

    def core_sums(g, which, tag):
        partials = [_split(g[n], axis) for n, axis in which]
        from_sibling = _sibling_halves([p.astype(BF16) for p in partials], "sibling_halves_" + tag)
        for (n, _), p, o in zip(which, partials, from_sibling):
            chip_sums[n] = _half_sum("core_sum_" + n, p, o, core)
        return _ExchangeJob([chip_sums[n] for n, _ in which])

    loss, grad_x, g, received_rest = _local_step(x, positions, loss_target, full, late_job, late_weights,
                                                 lambda g_now: core_sums(g_now, rest, "rest"))
    received_first = _run_job(core_sums(g, first, "ffn1"), "exchange_ffn1")
    received = dict(zip([n for n, _ in first + rest], received_first + received_rest))
    half_totals = [_sum_slots("sum_partials_" + n, received[n], chip_sums[n], chip) for n, _ in _MATRICES]
    other_halves = _sibling_swap(half_totals)

    small_shapes = [g[n].shape for n in _VECTORS] + [g["conv_w"].shape, (1,)]
    total = _small_allreduce(_pack_rows([g[n] for n in _VECTORS] + [g["conv_w"], loss[0, :1]]))
    small = _unpack_rows(total, small_shapes)
    loss_out = small[-1].reshape(())

    grads, deltas, new_m, new_v = {}, {}, {}, {}
    for (n, _), mine, theirs in zip(_MATRICES, half_totals, other_halves):
        res = _adamw_halves("adamw_" + n, mine, theirs, core, wts[n][0], mom_m[n][0], mom_v[n][0])
        grads[n], deltas[n], new_m[n], new_v[n] = [r[None] for r in res]

    vec_shapes = [wts[n].shape for n in _VECTORS]
    res = _adamw("adamw_vectors", _pack_rows(small[:len(_VECTORS)]), _pack_rows([wts[n] for n in _VECTORS]),
                 _pack_rows([mom_m[n] for n in _VECTORS]), _pack_rows([mom_v[n] for n in _VECTORS]))
    for dst, packed in zip((grads, deltas, new_m, new_v), res):
        for n, val in zip(_VECTORS, _unpack_rows(packed, vec_shapes)):
            dst[n] = val

    cols = conv_w.shape[2]
    g_conv = lax.dynamic_slice_in_dim(small[len(_VECTORS)], chip * cols, cols, axis=1)
    res = _adamw("adamw_conv_w", g_conv, wts["conv_w"][0], mom_m["conv_w"][0], mom_v["conv_w"][0])
    grads["conv_w"], deltas["conv_w"], new_m["conv_w"], new_v["conv_w"] = [r[None] for r in res]

    return (loss_out, grad_x, *[grads[n] for n in _WEIGHT_ORDER], *[deltas[n] for n in _WEIGHT_ORDER],
            *[new_m[n] for n in _WEIGHT_ORDER], *[new_v[n] for n in _WEIGHT_ORDER])
```

```python
import functools

import numpy as np
import jax
import jax.numpy as jnp
from jax import lax
from jax.experimental import pallas as pl
from jax.experimental.pallas import tpu as pltpu

F32, BF16 = jnp.float32, jnp.bfloat16

D_MODEL = 1024
D_FF = 2816
N_HEADS = 12
HEAD_DIM = 64
D_ATTN = 768
D_SSD = 768
N_GROUPS = 4
HEADS_PER_GROUP = 3
D_STATE = 128
D_CONV = 1792
CONV_WIDTH = 4
ROPE_DIM = 16
ROPE_THETA = 500000.0
ALPHA = 2.0 ** 0.25
LN_EPS = 1e-5
RMS_EPS = 1e-6
ADAM_LR, ADAM_B1, ADAM_B2, ADAM_EPS, ADAM_WD, ADAM_STEP = 0.001, 0.9, 0.999, 1e-08, 0.01, 10

LANES = 128
GATE_UP_INTERLEAVE = 256
SEQ_BLOCK = 256
GROUP_LANES = 256
VMEM_LIMIT = 56 * 1024 * 1024
NEG = -1e30
MESH = pl.DeviceIdType.MESH
HIGHEST = lax.Precision.HIGHEST

_NT = (((1,), (1,)), ((), ()))
_TN = (((0,), (0,)), ((), ()))


def _params(*sem):
    return pltpu.CompilerParams(dimension_semantics=sem, vmem_limit_bytes=VMEM_LIMIT)


def _bf(v):
    return v.astype(BF16)


EPILOGUE_ROWS = 128


def _row_chunks(tm):
    return [slice(r, min(r + EPILOGUE_ROWS, tm)) for r in range(0, tm, EPILOGUE_ROWS)]


def _sigmoid(v):
    return 0.5 * jnp.tanh(0.5 * v) + 0.5


def _mm(name, pairs, *, scale=1.0, res=None, res_scale=1.0, out_dtype=F32, tm=512, tn=512):
    m, n = pairs[0][0].shape[0], pairs[0][1].shape[1]
    tm, tn = min(tm, m), min(tn, n)
    assert m % tm == 0 and n % tn == 0, (name, m, n, tm, tn)
    npair = len(pairs)

    def body(*refs):
        acc = None
        for a_ref, b_ref in zip(refs[:npair], refs[npair:2 * npair]):
            d = jnp.dot(_bf(a_ref[...]), b_ref[...], preferred_element_type=F32)
            acc = d if acc is None else acc + d
        if scale != 1.0:
            acc = acc * scale
        if res is not None:
            acc = acc + res_scale * refs[2 * npair][...]
        refs[-1][...] = acc.astype(out_dtype)

    in_specs = [pl.BlockSpec((tm, a.shape[1]), lambda i, j: (i, 0)) for a, _ in pairs]
    in_specs += [pl.BlockSpec((b.shape[0], tn), lambda i, j: (0, j)) for _, b in pairs]
    args = [a for a, _ in pairs] + [b for _, b in pairs]
    if res is not None:
        in_specs.append(pl.BlockSpec((tm, tn), lambda i, j: (i, j)))
        args.append(res)
    return pl.pallas_call(
        body, name=name, grid=(m // tm, n // tn), in_specs=in_specs,
        out_specs=pl.BlockSpec((tm, tn), lambda i, j: (i, j)),
        out_shape=jax.ShapeDtypeStruct((m, n), out_dtype),
        compiler_params=_params("parallel", "parallel"),
    )(*args)


def _mm_tn(name, x, dy, *, scale=1.0, tk=512, tn=512, tt=1024):
    t, k = x.shape
    n = dy.shape[1]
    tk, tn, tt = min(tk, k), min(tn, n), min(tt, t)
    assert k % tk == 0 and n % tn == 0 and t % tt == 0, (name, k, n, t)
    nt = t // tt

    def body(x_ref, dy_ref, o_ref):
        step = pl.program_id(2)
        d = lax.dot_general(_bf(x_ref[...]), _bf(dy_ref[...]), _TN, preferred_element_type=F32)

        @pl.when(step == 0)
        def _():
            o_ref[...] = d

        @pl.when(step > 0)
        def _():
            o_ref[...] += d

        if scale != 1.0:
            @pl.when(step == nt - 1)
            def _():
                o_ref[...] = o_ref[...] * scale

    return pl.pallas_call(
        body, name=name, grid=(k // tk, n // tn, nt),
        in_specs=[pl.BlockSpec((tt, tk), lambda i, j, s: (s, i)), pl.BlockSpec((tt, tn), lambda i, j, s: (s, j))],
        out_specs=pl.BlockSpec((tk, tn), lambda i, j, s: (i, j)),
        out_shape=jax.ShapeDtypeStruct((k, n), F32),
        compiler_params=_params("parallel", "parallel", "arbitrary"),
    )(x, dy)


def _mm_tn_gate_up(name, x, dau, *, tt=1024):
    t, k = x.shape
    gi = GATE_UP_INTERLEAVE
    nj = dau.shape[1] // (2 * gi)
    tt = min(tt, t)
    nt = t // tt

    def body(x_ref, dy_ref, g_ref, u_ref):
        step = pl.program_id(1)
        d = lax.dot_general(_bf(x_ref[...]), dy_ref[...], _TN, preferred_element_type=F32)

        @pl.when(step == 0)
        def _():
            g_ref[...] = d[:, :gi]
            u_ref[...] = d[:, gi:]

        @pl.when(step > 0)
        def _():
            g_ref[...] += d[:, :gi]
            u_ref[...] += d[:, gi:]

    out = pl.BlockSpec((k, gi), lambda j, s: (0, j))
    return pl.pallas_call(
        body, name=name, grid=(nj, nt),
        in_specs=[pl.BlockSpec((tt, k), lambda j, s: (s, 0)), pl.BlockSpec((tt, 2 * gi), lambda j, s: (s, j))],
        out_specs=[out, out],
        out_shape=[jax.ShapeDtypeStruct((k, gi * nj), F32)] * 2,
        compiler_params=_params("parallel", "arbitrary"),
    )(x, dau)


def _carried(carry, ins, outs, sems, step, total):
    start, forward, finish = carry.phases(ins, outs, sems)
    pl.when(step == 0)(start)
    if forward is not None:
        pl.when(step == (3 * total) // 4)(forward)
    return lambda: pl.when(step == total - 1)(finish)


def _mm_swiglu(name, x, wgu, *, tm=512, carry=None):
    t, k = x.shape
    gi = GATE_UP_INTERLEAVE
    ni, nj = t // tm, wgu.shape[1] // (2 * gi)
    nc = carry.n if carry is not None else 0

    def body(*refs):
        x_ref, w_ref = refs[:2]
        au_ref, hm_ref = refs[2 + nc:4 + nc]
        if carry is not None:
            step = pl.program_id(0) * nj + pl.program_id(1)
            finish = _carried(carry, refs[2:2 + nc], refs[4 + nc:4 + 2 * nc], refs[4 + 2 * nc:], step, ni * nj)
        for rows in _row_chunks(tm):
            au = jnp.dot(_bf(x_ref[rows, :]), w_ref[...], preferred_element_type=F32)
            a, u = au[:, :gi], au[:, gi:]
            au_ref[rows, :] = _bf(au)
            hm_ref[rows, :] = _bf(a * _sigmoid(a) * u)
        if carry is not None:
            finish()

    hbm = pl.BlockSpec(memory_space=pltpu.HBM)
    res = pl.pallas_call(
        body, name=name, grid=(ni, nj),
        in_specs=[pl.BlockSpec((tm, k), lambda i, j: (i, 0)), pl.BlockSpec((k, 2 * gi), lambda i, j: (0, j))] + [hbm] * nc,
        out_specs=[pl.BlockSpec((tm, 2 * gi), lambda i, j: (i, j)), pl.BlockSpec((tm, gi), lambda i, j: (i, j))] + [hbm] * nc,
        out_shape=[jax.ShapeDtypeStruct((t, 2 * gi * nj), BF16), jax.ShapeDtypeStruct((t, gi * nj), BF16)]
        + (carry.out_shape if carry is not None else []),
        scratch_shapes=carry.scratch_shapes if carry is not None else [],
        compiler_params=_params(*(("arbitrary", "arbitrary") if carry is not None else ("parallel", "parallel"))),
    )(x, wgu, *(carry.operands if carry is not None else []))
    return res if carry is None else (res[0], res[1], carry.results(res[2:]))


def _mm_swiglu_bwd(name, dr, wdt, au, *, scale, tm=512, carry=None):
    t, k = dr.shape
    gi = GATE_UP_INTERLEAVE
    ni, nj = t // tm, wdt.shape[1] // gi
    nc = carry.n if carry is not None else 0

    def body(*refs):
        dr_ref, w_ref, au_ref = refs[:3]
        o_ref = refs[3 + nc]
        if carry is not None:
            step = pl.program_id(0) * nj + pl.program_id(1)
            finish = _carried(carry, refs[3:3 + nc], refs[4 + nc:4 + 2 * nc], refs[4 + 2 * nc:], step, ni * nj)
        for rows in _row_chunks(tm):
            dhm = jnp.dot(_bf(dr_ref[rows, :]), w_ref[...], preferred_element_type=F32) * scale
            au_v = au_ref[rows, :].astype(F32)
            a, u = au_v[:, :gi], au_v[:, gi:]
            sig = _sigmoid(a)
            silu = a * sig
            o_ref[rows, :gi] = _bf(dhm * u * (sig + silu - silu * sig))
            o_ref[rows, gi:] = _bf(dhm * silu)
        if carry is not None:
            finish()

    hbm = pl.BlockSpec(memory_space=pltpu.HBM)
    res = pl.pallas_call(
        body, name=name, grid=(ni, nj),
        in_specs=[pl.BlockSpec((tm, k), lambda i, j: (i, 0)), pl.BlockSpec((k, gi), lambda i, j: (0, j)),
                  pl.BlockSpec((tm, 2 * gi), lambda i, j: (i, j))] + [hbm] * nc,
        out_specs=[pl.BlockSpec((tm, 2 * gi), lambda i, j: (i, j))] + [hbm] * nc,
        out_shape=[jax.ShapeDtypeStruct((t, 2 * gi * nj), BF16)] + (carry.out_shape if carry is not None else []),
        scratch_shapes=carry.scratch_shapes if carry is not None else [],
        compiler_params=_params(*(("arbitrary", "arbitrary") if carry is not None else ("parallel", "parallel"))),
    )(dr, wdt, au, *(carry.operands if carry is not None else []))
    return res[0] if carry is None else (res[0], carry.results(res[1:]))


def _resident(shape):
    return pl.BlockSpec(shape, lambda i: (0,) * len(shape), pipeline_mode=pl.Buffered(1))


def _ffn_fwd(name, x16, res, wgu, wd, g, b, *, tm=512, carry=None):
    t, k = x16.shape
    gi = GATE_UP_INTERLEAVE
    nj, n, ni = wd.shape[0] // gi, wd.shape[1], t // tm
    nc = carry.n if carry is not None else 0

    def body(*refs):
        x_ref, res_ref, wgu_ref, wd_ref, g_ref, b_ref = refs[:6]
        au_ref, hm_ref, y_ref, r_ref, y16_ref = refs[6 + nc:11 + nc]
        if carry is not None:
            finish = _carried(carry, refs[6:6 + nc], refs[11 + nc:11 + 2 * nc], refs[11 + 2 * nc:], pl.program_id(0), ni)
        xv = x_ref[...]
        acc = jnp.zeros((tm, n), F32)
        for j in range(nj):
            au = jnp.dot(xv, wgu_ref[:, 2 * gi * j:2 * gi * (j + 1)], preferred_element_type=F32)
            a, u = au[:, :gi], au[:, gi:]
            au_ref[:, 2 * gi * j:2 * gi * (j + 1)] = _bf(au)
            hm = _bf(a * _sigmoid(a) * u)
            hm_ref[:, gi * j:gi * (j + 1)] = hm
            acc = acc + jnp.dot(hm, wd_ref[gi * j:gi * (j + 1), :], preferred_element_type=F32)
        r = ALPHA * res_ref[...] + 0.5 * acc
        r_ref[...] = r
        y = _layer_norm(r, g_ref[...], b_ref[...])
        y_ref[...] = y
        y16_ref[...] = _bf(y)
        if carry is not None:
            finish()

    row = lambda c: pl.BlockSpec((tm, c), lambda i: (i, 0))
    hbm = pl.BlockSpec(memory_space=pltpu.HBM)
    res_ = pl.pallas_call(
        body, name=name, grid=(ni,),
        in_specs=[row(k), row(n), _resident(wgu.shape), _resident(wd.shape), _resident(g.shape), _resident(b.shape)] + [hbm] * nc,
        out_specs=[row(2 * gi * nj), row(gi * nj), row(n), row(n), row(n)] + [hbm] * nc,
        out_shape=[jax.ShapeDtypeStruct((t, 2 * gi * nj), BF16), jax.ShapeDtypeStruct((t, gi * nj), BF16),
                   jax.ShapeDtypeStruct((t, n), F32), jax.ShapeDtypeStruct((t, n), F32), jax.ShapeDtypeStruct((t, n), BF16)]
        + (carry.out_shape if carry is not None else []),
        scratch_shapes=carry.scratch_shapes if carry is not None else [],
        compiler_params=_params("arbitrary" if carry is not None else "parallel"),
    )(x16, res, wgu, wd, g, b, *(carry.operands if carry is not None else []))
    return tuple(res_[:5]) + ((carry.results(res_[5:]),) if carry is not None else ())


def _ffn_bwd(name, dr16, dr, wdt, au, wgut, *, tm=512, carry=None):
    t, n = dr16.shape
    gi = GATE_UP_INTERLEAVE
    nj, ni = wdt.shape[1] // gi, t // tm
    nc = carry.n if carry is not None else 0

    def body(*refs):
        dr16_ref, dr_ref, wdt_ref, au_ref, wgut_ref = refs[:5]
        dau_ref, dx_ref = refs[5 + nc:7 + nc]
        if carry is not None:
            finish = _carried(carry, refs[5:5 + nc], refs[7 + nc:7 + 2 * nc], refs[7 + 2 * nc:], pl.program_id(0), ni)
        drv = dr16_ref[...]
        acc = jnp.zeros((tm, n), F32)
        for j in range(nj):
            dhm = jnp.dot(drv, wdt_ref[:, gi * j:gi * (j + 1)], preferred_element_type=F32) * 0.5
            au_v = au_ref[:, 2 * gi * j:2 * gi * (j + 1)].astype(F32)
            a, u = au_v[:, :gi], au_v[:, gi:]
            sig = _sigmoid(a)
            silu = a * sig
            dau = jnp.concatenate([_bf(dhm * u * (sig + silu - silu * sig)), _bf(dhm * silu)], axis=1)
            dau_ref[:, 2 * gi * j:2 * gi * (j + 1)] = dau
            acc = acc + jnp.dot(dau, wgut_ref[2 * gi * j:2 * gi * (j + 1), :], preferred_element_type=F32)
        dx_ref[...] = ALPHA * dr_ref[...] + acc
        if carry is not None:
            finish()

    row = lambda c: pl.BlockSpec((tm, c), lambda i: (i, 0))
    hbm = pl.BlockSpec(memory_space=pltpu.HBM)
    res_ = pl.pallas_call(
        body, name=name, grid=(ni,),
        in_specs=[row(n), row(n), _resident(wdt.shape), row(2 * gi * nj), _resident(wgut.shape)] + [hbm] * nc,
        out_specs=[row(2 * gi * nj), row(n)] + [hbm] * nc,
        out_shape=[jax.ShapeDtypeStruct((t, 2 * gi * nj), BF16), jax.ShapeDtypeStruct((t, n), F32)]
        + (carry.out_shape if carry is not None else []),
        scratch_shapes=carry.scratch_shapes if carry is not None else [],
        compiler_params=_params("arbitrary" if carry is not None else "parallel"),
    )(dr16, dr, wdt, au, wgut, *(carry.operands if carry is not None else []))
    return tuple(res_[:2]) + ((carry.results(res_[2:]),) if carry is not None else ())


def _layer_norm(r, g, b):
    mu = jnp.mean(r, axis=-1, keepdims=True)
    var = jnp.mean(jnp.square(r - mu), axis=-1, keepdims=True)
    return (r - mu) * lax.rsqrt(var + LN_EPS) * g + b


def _mm_res_ln(name, a, w, res, g, b, *, scale, tm=256):
    t, k = a.shape
    n = w.shape[1]

    def body(a_ref, w_ref, res_ref, g_ref, b_ref, y_ref, r_ref, y16_ref):
        for rows in _row_chunks(tm):
            r = ALPHA * res_ref[rows, :] + scale * jnp.dot(_bf(a_ref[rows, :]), w_ref[...], preferred_element_type=F32)
            r_ref[rows, :] = r
            y = _layer_norm(r, g_ref[...], b_ref[...])
            y_ref[rows, :] = y
            y16_ref[rows, :] = _bf(y)

    row = lambda c: pl.BlockSpec((tm, c), lambda i: (i, 0))
    const = lambda shape: pl.BlockSpec(shape, lambda i: (0, 0))
    return pl.pallas_call(
        body, name=name, grid=(t // tm,),
        in_specs=[row(k), const((k, n)), row(n), const((1, n)), const((1, n))],
        out_specs=[row(n), row(n), row(n)],
        out_shape=[jax.ShapeDtypeStruct((t, n), F32), jax.ShapeDtypeStruct((t, n), F32), jax.ShapeDtypeStruct((t, n), BF16)],
        compiler_params=_params("parallel"),
    )(a, w, res, g, b)


def _rowwise(name, fn, rows, consts, row_outs, acc_outs=(), tm=256):
    rows = [r if isinstance(r, tuple) else (r, r.shape[1]) for r in rows]
    t = rows[0][0].shape[0]
    tm = min(tm, t)
    assert t % tm == 0
    nr, nc, no, na = len(rows), len(consts), len(row_outs), len(acc_outs)

    def body(*refs):
        vals = [r[...] for r in refs[:nr + nc]]
        outs, accs = fn(*vals)
        for o_ref, o in zip(refs[nr + nc:nr + nc + no], outs):
            o_ref[...] = o.astype(o_ref.dtype)
        if na:
            step = pl.program_id(0)
            acc_refs = refs[nr + nc + no:]

            @pl.when(step == 0)
            def _():
                for a_ref, a in zip(acc_refs, accs):
                    a_ref[...] = a

            @pl.when(step > 0)
            def _():
                for a_ref, a in zip(acc_refs, accs):
                    a_ref[...] += a

    in_specs = [pl.BlockSpec((tm, w), lambda i: (i, 0)) for _, w in rows]
    in_specs += [pl.BlockSpec(c.shape, lambda i, nd=c.ndim: (0,) * nd) for c in consts]
    out_specs = [pl.BlockSpec((tm, c), lambda i: (i, 0)) for c, _ in row_outs]
    out_specs += [pl.BlockSpec(s, lambda i: (0, 0)) for s in acc_outs]
    out_shape = [jax.ShapeDtypeStruct((t, c), dt) for c, dt in row_outs]
    out_shape += [jax.ShapeDtypeStruct(s, F32) for s in acc_outs]
    res = pl.pallas_call(
        body, name=name, grid=(t // tm,), in_specs=in_specs, out_specs=out_specs, out_shape=out_shape,
        compiler_params=_params("arbitrary" if na else "parallel"),
    )(*[r for r, _ in rows], *consts)
    return res


def _ln_bwd(name, r, g, b, dy):
    def fn(r_v, dy_v, g_v, b_v):
        _, vjp = jax.vjp(_layer_norm, r_v, g_v, b_v)
        dr, dg, db = vjp(dy_v)
        return [dr, dr], [dg, db]
    return _rowwise(name, fn, [r, dy], [g, b], [(r.shape[1], F32), (r.shape[1], BF16)], [(1, r.shape[1])] * 2)


def _ln_loss_bwd(name, r, g, b, target):
    def fn(r_v, t_v, g_v, b_v):
        def loss_fn(rr, gg, bb):
            err = jnp.square(_layer_norm(rr, gg, bb) - t_v)
            return 0.5 * jnp.sum(jnp.mean(err, axis=-1, keepdims=True), axis=0, keepdims=True)
        loss, vjp = jax.vjp(loss_fn, r_v, g_v, b_v)
        dr, dg, db = vjp(jnp.ones((1, 1), F32))
        return [dr, dr], [dg, db, jnp.broadcast_to(loss, (1, LANES))]
    return _rowwise(name, fn, [r, target], [g, b], [(r.shape[1], F32), (r.shape[1], BF16)],
                    [(1, r.shape[1])] * 2 + [(1, LANES)])


def _rope_tables(posf, invf, sgn):
    ang = posf * invf
    return jnp.cos(ang), jnp.sin(ang) * sgn


def _rope_apply(tv, cos, sin):
    lane = lax.broadcasted_iota(jnp.int32, cos.shape, 1)
    first = (lane % HEAD_DIM) < (ROPE_DIM // 2)
    outs = []
    for gidx in range(tv.shape[1] // LANES):
        tg = tv[:, LANES * gidx:LANES * (gidx + 1)]
        sw = jnp.where(first, pltpu.roll(tg, LANES - ROPE_DIM // 2, 1), pltpu.roll(tg, ROPE_DIM // 2, 1))
        outs.append(tg * cos + sw * sin)
    return jnp.concatenate(outs, axis=1)


def _rope_fwd(qk, posf, invf, sgn):
    def fn(qk_v, pos_v, invf_v, sgn_v):
        cos, sin = _rope_tables(pos_v, invf_v, sgn_v)
        q = _rope_apply(qk_v[:, :D_ATTN], cos, sin) * (HEAD_DIM ** -0.5)
        k = _rope_apply(qk_v[:, D_ATTN:], cos, sin)
        return [q, k, jnp.concatenate([cos, sin], axis=1)], []
    return _rowwise("rope_fwd", fn, [qk, posf], [invf, sgn], [(D_ATTN, BF16), (D_ATTN, BF16), (2 * LANES, F32)])


def _rope_bwd(dq, dk, cs):
    def fn(dq_v, dk_v, cs_v):
        cos, sin = cs_v[:, :LANES], -cs_v[:, LANES:]
        gq = _rope_apply(dq_v * (HEAD_DIM ** -0.5), cos, sin)
        gk = _rope_apply(dk_v, cos, sin)
        return [jnp.concatenate([gq, gk], axis=1)], []
    return _rowwise("rope_bwd", fn, [dq, dk, cs], [], [(2 * D_ATTN, BF16)])[0]


def _rms(v, w):
    return v * lax.rsqrt(jnp.mean(v * v, axis=-1, keepdims=True) + RMS_EPS) * w


def _ungroup(yg):
    w = HEADS_PER_GROUP * HEAD_DIM
    return jnp.concatenate([yg[:, GROUP_LANES * g:GROUP_LANES * g + w] for g in range(N_GROUPS)], axis=1)


def _group(xs):
    w = HEADS_PER_GROUP * HEAD_DIM
    parts = []
    for g in range(N_GROUPS):
        parts += [xs[:, w * g:w * (g + 1)], jnp.zeros((xs.shape[0], GROUP_LANES - w), xs.dtype)]
    return jnp.concatenate(parts, axis=1)


def _norms_fn(attn, yg, xs, z, w_attn, w_ssd, dskip):
    a_n = _rms(attn, w_attn)
    y = _ungroup(yg) + dskip * xs
    y_n = _rms(y * (z * jax.nn.sigmoid(z)), w_ssd)
    return jnp.concatenate([a_n, y_n], axis=1)


def _norms_fwd(attn, yg, xbc, z, w_attn, w_ssd, dskip):
    def fn(*v):
        return [_norms_fn(*v)], []
    return _rowwise("norms_fwd", fn, [attn, yg, (xbc, D_SSD), z], [w_attn, w_ssd, dskip], [(D_ATTN + D_SSD, BF16)])[0]


def _norms_bwd(attn, yg, xbc, z, w_attn, w_ssd, dskip, dcat):
    def fn(attn_v, yg_v, xs_v, z_v, dcat_v, wa_v, ws_v, dk_v):
        _, vjp = jax.vjp(_norms_fn, attn_v, yg_v, xs_v, z_v, wa_v, ws_v, dk_v)
        d_attn, d_yg, d_xs, d_z, d_wa, d_ws, d_dk = vjp(dcat_v)
        return [d_attn, d_yg, d_xs, d_z], [d_wa, d_ws, d_dk]
    return _rowwise("norms_bwd", fn, [attn, yg, (xbc, D_SSD), z, dcat], [w_attn, w_ssd, dskip],
                    [(D_ATTN, F32), (N_GROUPS * GROUP_LANES, F32), (D_SSD, F32), (D_SSD, BF16)], [(1, D_SSD)] * 3)


def _ssd_prep_fn(xs, dtp, dtb, alog, e_x, e_a):
    dt = jax.nn.softplus(dtp + dtb)
    a = -jnp.exp(alog)
    dtg = jnp.dot(dt, e_x, precision=HIGHEST, preferred_element_type=F32)
    xdtg = _group(xs) * dtg
    dag = jnp.dot(dt * a, e_a, precision=HIGHEST, preferred_element_type=F32)
    return xdtg, dag


def _ssd_prep_fwd(xbc, dtp, dtb, alog, e_x, e_a):
    def fn(xbc_v, dtp_v, dtb_v, alog_v, ex_v, ea_v):
        xdtg, dag = _ssd_prep_fn(xbc_v[:, :D_SSD], dtp_v, dtb_v, alog_v, ex_v, ea_v)
        return [xdtg, xbc_v[:, D_SSD:], dag], []
    return _rowwise("ssd_prep_fwd", fn, [xbc, dtp], [dtb, alog, e_x, e_a],
                    [(N_GROUPS * GROUP_LANES, BF16), (D_CONV - D_SSD, BF16), (N_GROUPS * LANES, F32)])


def _ssd_prep_bwd(xbc, dtp, dtb, alog, e_x, e_a, dxdtg, ddag, dxs_a, db, dc):
    def fn(xs_v, dtp_v, dxdtg_v, ddag_v, dxs_a_v, db_v, dc_v, dtb_v, alog_v, ex_v, ea_v):
        _, vjp = jax.vjp(lambda a, b, c, d: _ssd_prep_fn(a, b, c, d, ex_v, ea_v), xs_v, dtp_v, dtb_v, alog_v)
        dxs, ddtp, ddtb, dalog = vjp((dxdtg_v, ddag_v))
        return [jnp.concatenate([dxs + dxs_a_v, db_v, dc_v], axis=1), ddtp], [ddtb, dalog]
    return _rowwise("ssd_prep_bwd", fn, [(xbc, D_SSD), dtp, dxdtg, ddag, dxs_a, db, dc], [dtb, alog, e_x, e_a],
                    [(D_CONV, F32), (LANES, BF16)], [(1, LANES)] * 2)


def _shift_down(u, d):
    if d == 0:
        return u
    row = lax.broadcasted_iota(jnp.int32, u.shape, 0)
    return jnp.where(row >= d, pltpu.roll(u, d, 0), 0.0)


def _shift_up(u, d):
    if d == 0:
        return u
    s = u.shape[0]
    row = lax.broadcasted_iota(jnp.int32, u.shape, 0)
    return jnp.where(row < s - d, pltpu.roll(u, s - d, 0), 0.0)


def _conv_pre(u, w, b):
    acc = b
    for k in range(CONV_WIDTH):
        acc = acc + w[k:k + 1, :] * _shift_down(u, CONV_WIDTH - 1 - k)
    return acc


def _conv_fwd(u, w, b, *, tc=256):
    nb, s, c = u.shape

    def body(u_ref, w_ref, b_ref, o_ref):
        pre = _conv_pre(u_ref[0], w_ref[...], b_ref[...])
        o_ref[0] = pre * jax.nn.sigmoid(pre)

    return pl.pallas_call(
        body, name="conv_fwd", grid=(c // tc, nb),
        in_specs=[pl.BlockSpec((1, s, tc), lambda j, i: (i, 0, j)), pl.BlockSpec((CONV_WIDTH, tc), lambda j, i: (0, j)),
                  pl.BlockSpec((1, tc), lambda j, i: (0, j))],
        out_specs=pl.BlockSpec((1, s, tc), lambda j, i: (i, 0, j)),
        out_shape=jax.ShapeDtypeStruct((nb, s, c), F32),
        compiler_params=_params("parallel", "parallel"),
    )(u, w, b)


def _conv_bwd(u, w, b, dout, *, tc=256):
    nb, s, c = u.shape

    def body(u_ref, w_ref, b_ref, d_ref, du_ref, dw_ref, db_ref):
        uv, wv = u_ref[0], w_ref[...]
        pre = _conv_pre(uv, wv, b_ref[...])
        sig = jax.nn.sigmoid(pre)
        dpre = d_ref[0] * (sig * (1.0 + pre * (1.0 - sig)))
        du = jnp.zeros_like(uv)
        dws = []
        for k in range(CONV_WIDTH):
            du = du + wv[k:k + 1, :] * _shift_up(dpre, CONV_WIDTH - 1 - k)
            dws.append(jnp.sum(dpre * _shift_down(uv, CONV_WIDTH - 1 - k), axis=0, keepdims=True))
        du_ref[0] = _bf(du)
        dwv = jnp.concatenate(dws + [jnp.zeros((8 - CONV_WIDTH, tc), F32)], axis=0)
        dbv = jnp.sum(dpre, axis=0, keepdims=True)
        first = pl.program_id(1) == 0

        @pl.when(first)
        def _():
            dw_ref[...] = dwv
            db_ref[...] = dbv

        @pl.when(jnp.logical_not(first))
        def _():
            dw_ref[...] += dwv
            db_ref[...] += dbv

    blk = pl.BlockSpec((1, s, tc), lambda j, i: (i, 0, j))
    return pl.pallas_call(
        body, name="conv_bwd", grid=(c // tc, nb),
        in_specs=[blk, pl.BlockSpec((CONV_WIDTH, tc), lambda j, i: (0, j)), pl.BlockSpec((1, tc), lambda j, i: (0, j)), blk],
        out_specs=[blk, pl.BlockSpec((8, tc), lambda j, i: (0, j)), pl.BlockSpec((1, tc), lambda j, i: (0, j))],
        out_shape=[jax.ShapeDtypeStruct((nb, s, c), BF16), jax.ShapeDtypeStruct((8, c), F32), jax.ShapeDtypeStruct((1, c), F32)],
        compiler_params=_params("parallel", "arbitrary"),
    )(u, w, b, dout)


FWD_KEY_BLOCK = 256


def _branch_bias_table(seq, kb):
    ratio = SEQ_BLOCK // kb
    key = np.arange(kb)[None, :, None]
    query = np.arange(SEQ_BLOCK)[None, None, :]
    delta = (np.arange(seq // kb)[:, None, None] - (ratio - 1)) * kb + query - key
    cnt = np.zeros(delta.shape, np.float64)
    for window, dilation in ((128, 1), (512, 4), (2048, 16)):
        cnt += (delta >= 0) & (delta % dilation == 0) & (delta <= window)
    return jnp.asarray(np.where(cnt > 0, np.log(np.maximum(cnt, 1.0)), NEG).astype(np.float32))


HEADS_PER_BLOCK = LANES // HEAD_DIM


def _head_rows(v, h):
    row = lax.broadcasted_iota(jnp.int32, v.shape, 0)
    return jnp.where((row >= HEAD_DIM * h) & (row < HEAD_DIM * (h + 1)), v, jnp.zeros_like(v))


def _attn_fwd(q, k, v, bias):
    nb_, s, _ = q.shape
    ab, kb = SEQ_BLOCK, FWD_KEY_BLOCK
    nblk, nkb, ratio = s // ab, s // kb, ab // kb

    def body(q_ref, k_ref, v_ref, b_ref, o_ref, lse_ref, vt_s):
        i = pl.program_id(2)

        @pl.when(i == 0)
        def _():
            for jb in range(nkb):
                vt_s[jb] = v_ref[0, kb * jb:kb * (jb + 1), :].T

        qt = q_ref[0].T
        qts = [_head_rows(qt, h) for h in range(HEADS_PER_BLOCK)]

        def step(j, carry):
            ks = pl.ds(pl.multiple_of(j * kb, kb), kb)
            kj = k_ref[0, ks, :]
            lb = b_ref[ratio * i - j + (ratio - 1)]
            out = []
            for h in range(HEADS_PER_BLOCK):
                m, l, acc = carry[3 * h:3 * h + 3]
                st = jnp.dot(kj, qts[h], preferred_element_type=F32) + lb
                m_new = jnp.maximum(m, jnp.max(st, axis=0, keepdims=True))
                p = jnp.exp(st - m_new)
                a = jnp.exp(m - m_new)
                l = a * l + jnp.sum(p, axis=0, keepdims=True)
                vt = vt_s[j, HEAD_DIM * h:HEAD_DIM * (h + 1), :]
                acc = a * acc + jnp.dot(vt, _bf(p), preferred_element_type=F32)
                out += [m_new, l, acc]
            return tuple(out)

        init = (jnp.full((1, ab), NEG, F32), jnp.zeros((1, ab), F32), jnp.zeros((HEAD_DIM, ab), F32)) * HEADS_PER_BLOCK
        res = lax.fori_loop(0, ratio * (i + 1), step, init)
        ot = jnp.concatenate([res[3 * h + 2] / res[3 * h + 1] for h in range(HEADS_PER_BLOCK)], axis=0)
        o_ref[0] = ot.T
        rows = [res[3 * h] + jnp.log(res[3 * h + 1]) for h in range(HEADS_PER_BLOCK)]
        lse_ref[0, 0, 0] = jnp.concatenate(rows + [jnp.zeros((8 - HEADS_PER_BLOCK, ab), F32)], axis=0)

    qblk = pl.BlockSpec((1, ab, LANES), lambda b, hp, i: (b, i, hp))
    full = pl.BlockSpec((1, s, LANES), lambda b, hp, i: (b, 0, hp))
    return pl.pallas_call(
        body, name="attn_fwd", grid=(nb_, D_ATTN // LANES, nblk),
        in_specs=[qblk, full, full, pl.BlockSpec((nkb, kb, ab), lambda b, hp, i: (0, 0, 0))],
        out_specs=[qblk, pl.BlockSpec((1, 1, 1, 8, ab), lambda b, hp, i: (b, hp, i, 0, 0))],
        out_shape=[jax.ShapeDtypeStruct((nb_, s, D_ATTN), F32),
                   jax.ShapeDtypeStruct((nb_, D_ATTN // LANES, nblk, 8, ab), F32)],
        scratch_shapes=[pltpu.VMEM((nkb, LANES, kb), BF16)],
        compiler_params=_params("parallel", "parallel", "arbitrary"),
    )(q, k, v, bias)


def _attn_bwd(q, k, v, o, do, lse, bias):
    nb_, s, _ = q.shape
    ab = SEQ_BLOCK
    nblk = s // ab

    nh = HEADS_PER_BLOCK

    def body(q_ref, k_ref, v_ref, o_ref, do_ref, lse_ref, b_ref, dq_ref, dk_ref, dv_ref,
             qt_s, dot_s, kt_s, dqt_s, do16_s, d_s, dk_acc, dv_acc):
        for jb in range(nblk):
            sl = slice(ab * jb, ab * (jb + 1))
            qt, kt = q_ref[0, sl, :].T, k_ref[0, sl, :].T
            do = do_ref[0, sl, :]
            dot = do.T
            prod = dot * o_ref[0, sl, :].T
            do16_s[sl, :] = _bf(do)
            for h in range(nh):
                qt_s[nh * jb + h] = _head_rows(qt, h)
                kt_s[nh * jb + h] = _head_rows(kt, h)
                dot_s[nh * jb + h] = _head_rows(_bf(dot), h)
            d_s[jb] = jnp.concatenate(
                [jnp.sum(prod[HEAD_DIM * h:HEAD_DIM * (h + 1)], axis=0, keepdims=True) for h in range(nh)]
                + [jnp.zeros((8 - nh, ab), F32)], axis=0)
            dqt_s[jb] = jnp.zeros((LANES, ab), F32)

        def outer(j, carry):
            ks = pl.ds(pl.multiple_of(j * ab, ab), ab)
            kj, vj = k_ref[0, ks, :], v_ref[0, ks, :]
            dk_acc[...] = jnp.zeros_like(dk_acc)
            dv_acc[...] = jnp.zeros_like(dv_acc)

            def inner(i, c2):
                qs = pl.ds(pl.multiple_of(i * ab, ab), ab)
                qi, doi = q_ref[0, qs, :], do16_s[qs, :]
                lb = b_ref[i - j]
                for h in range(nh):
                    st = jnp.dot(kj, qt_s[nh * i + h], preferred_element_type=F32) + lb
                    pt = jnp.exp(st - lse_ref[0, 0, i, h:h + 1, :])
                    dpt = jnp.dot(vj, dot_s[nh * i + h], preferred_element_type=F32)
                    dst16 = _bf(pt * (dpt - d_s[i, h:h + 1, :]))
                    dv_acc[h] += jnp.dot(_bf(pt), doi, preferred_element_type=F32)
                    dk_acc[h] += jnp.dot(dst16, qi, preferred_element_type=F32)
                    dqt_s[i] += jnp.dot(kt_s[nh * j + h], dst16, preferred_element_type=F32)
                return c2

            lax.fori_loop(j, nblk, inner, 0)
            lane = lax.broadcasted_iota(jnp.int32, (ab, LANES), 1)
            dk_ref[0, ks, :] = jnp.where(lane < HEAD_DIM, dk_acc[0], dk_acc[1])
            dv_ref[0, ks, :] = _bf(jnp.where(lane < HEAD_DIM, dv_acc[0], dv_acc[1]))
            return carry

        lax.fori_loop(0, nblk, outer, 0)
        for jb in range(nblk):
            dq_ref[0, ab * jb:ab * (jb + 1), :] = dqt_s[jb].T

    assert nh == 2
    full = pl.BlockSpec((1, s, LANES), lambda b, hp: (b, 0, hp))
    return pl.pallas_call(
        body, name="attn_bwd", grid=(nb_, D_ATTN // LANES),
        in_specs=[full] * 5 + [pl.BlockSpec((1, 1, nblk, 8, ab), lambda b, hp: (b, hp, 0, 0, 0)),
                               pl.BlockSpec((nblk, ab, ab), lambda b, hp: (0, 0, 0))],
        out_specs=[full, full, full],
        out_shape=[jax.ShapeDtypeStruct((nb_, s, D_ATTN), F32), jax.ShapeDtypeStruct((nb_, s, D_ATTN), F32),
                   jax.ShapeDtypeStruct((nb_, s, D_ATTN), BF16)],
        scratch_shapes=[pltpu.VMEM((nh * nblk, LANES, ab), BF16), pltpu.VMEM((nh * nblk, LANES, ab), BF16),
                        pltpu.VMEM((nh * nblk, LANES, ab), BF16), pltpu.VMEM((nblk, LANES, ab), F32),
                        pltpu.VMEM((s, LANES), BF16), pltpu.VMEM((nblk, 8, ab), F32),
                        pltpu.VMEM((nh, ab, LANES), F32), pltpu.VMEM((nh, ab, LANES), F32)],
        compiler_params=_params("parallel", "parallel"),
    )(q, k, v, o, do, lse, bias)


def _cumsum_fwd(dag):
    nb_, s, c = dag.shape
    ab = SEQ_BLOCK

    def body(a_ref, o_ref, ot_ref):
        r = lax.broadcasted_iota(jnp.int32, (ab, ab), 0)
        cc = lax.broadcasted_iota(jnp.int32, (ab, ab), 1)
        tri = (r >= cc).astype(F32)
        carry = jnp.zeros((1, c), F32)
        for i in range(s // ab):
            loc = jnp.dot(tri, a_ref[0, ab * i:ab * (i + 1), :], precision=HIGHEST, preferred_element_type=F32) + carry
            o_ref[0, ab * i:ab * (i + 1), :] = loc
            ot_ref[0, :, ab * i:ab * (i + 1)] = loc.T
            carry = loc[ab - 1:ab, :]

    return pl.pallas_call(
        body, name="ssd_cumsum", grid=(nb_,),
        in_specs=[pl.BlockSpec((1, s, c), lambda b: (b, 0, 0))],
        out_specs=[pl.BlockSpec((1, s, c), lambda b: (b, 0, 0)), pl.BlockSpec((1, c, s), lambda b: (b, 0, 0))],
        out_shape=[jax.ShapeDtypeStruct((nb_, s, c), F32), jax.ShapeDtypeStruct((nb_, c, s), F32)],
        compiler_params=_params("parallel"),
    )(dag)


def _cumsum_bwd(dcol, drow):
    nb_, s, c = dcol.shape
    ab = SEQ_BLOCK

    def body(c_ref, r_ref, o_ref):
        r = lax.broadcasted_iota(jnp.int32, (ab, ab), 0)
        cc = lax.broadcasted_iota(jnp.int32, (ab, ab), 1)
        tri = (r <= cc).astype(F32)
        carry = jnp.zeros((1, c), F32)
        for i in reversed(range(s // ab)):
            rows = r_ref[0, :, ab * i:ab * (i + 1)].T
            parts = []
            for g in range(N_GROUPS):
                parts += [rows[:, 8 * g:8 * (g + 1)], jnp.zeros((ab, LANES - 8), F32)]
            blk = c_ref[0, ab * i:ab * (i + 1), :] + jnp.concatenate(parts, axis=1)
            loc = jnp.dot(tri, blk, precision=HIGHEST, preferred_element_type=F32) + carry
            o_ref[0, ab * i:ab * (i + 1), :] = loc
            carry = loc[0:1, :]

    return pl.pallas_call(
        body, name="ssd_cumsum_bwd", grid=(nb_,),
        in_specs=[pl.BlockSpec((1, s, c), lambda b: (b, 0, 0)), pl.BlockSpec((1, N_GROUPS * 8, s), lambda b: (b, 0, 0))],
        out_specs=pl.BlockSpec((1, s, c), lambda b: (b, 0, 0)),
        out_shape=jax.ShapeDtypeStruct((nb_, s, c), F32),
        compiler_params=_params("parallel"),
    )(dcol, drow)


def _causal_ok(i, j):
    ab = SEQ_BLOCK
    r = lax.broadcasted_iota(jnp.int32, (ab, ab), 0)
    c = lax.broadcasted_iota(jnp.int32, (ab, ab), 1)
    return (r + (i - j) * ab) >= c


def _causal_ok_t(i, j):
    ab = SEQ_BLOCK
    r = lax.broadcasted_iota(jnp.int32, (ab, ab), 0)
    c = lax.broadcasted_iota(jnp.int32, (ab, ab), 1)
    return (c + (i - j) * ab) >= r


def _ssd_fwd(xdtg, bc, acum, acum_t):
    nb_, s, _ = xdtg.shape
    ab = SEQ_BLOCK

    def body(x_ref, b_ref, c_ref, ac_ref, at_ref, y_ref):
        i = pl.program_id(2)
        ci = c_ref[0]
        acol = [ac_ref[0, :, j:j + 1] for j in range(HEADS_PER_GROUP)]

        def step(jb, accs):
            ks = pl.ds(pl.multiple_of(jb * ab, ab), ab)
            cb = lax.dot_general(ci, b_ref[0, ks, :], _NT, preferred_element_type=F32)
            ok = _causal_ok(i, jb)
            new = []
            for j in range(HEADS_PER_GROUP):
                decay = jnp.exp(jnp.where(ok, acol[j] - at_ref[0, j:j + 1, ks], NEG))
                g = _bf(cb * decay)
                new.append(accs[j] + jnp.dot(g, x_ref[0, ks, HEAD_DIM * j:HEAD_DIM * (j + 1)], preferred_element_type=F32))
            return tuple(new)

        accs = lax.fori_loop(0, i + 1, step, tuple(jnp.zeros((ab, HEAD_DIM), F32) for _ in range(HEADS_PER_GROUP)))
        y_ref[0] = jnp.concatenate(list(accs) + [jnp.zeros((ab, GROUP_LANES - HEADS_PER_GROUP * HEAD_DIM), F32)], axis=1)

    return pl.pallas_call(
        body, name="ssd_fwd", grid=(nb_, N_GROUPS, s // ab),
        in_specs=[pl.BlockSpec((1, s, GROUP_LANES), lambda b, g, i: (b, 0, g)),
                  pl.BlockSpec((1, s, D_STATE), lambda b, g, i: (b, 0, g)),
                  pl.BlockSpec((1, ab, D_STATE), lambda b, g, i: (b, i, N_GROUPS + g)),
                  pl.BlockSpec((1, ab, LANES), lambda b, g, i: (b, i, g)),
                  pl.BlockSpec((1, 8, s), lambda b, g, i: (b, (LANES // 8) * g, 0))],
        out_specs=pl.BlockSpec((1, ab, GROUP_LANES), lambda b, g, i: (b, i, g)),
        out_shape=jax.ShapeDtypeStruct((nb_, s, N_GROUPS * GROUP_LANES), F32),
        compiler_params=_params("parallel", "parallel", "parallel"),
    )(xdtg, bc, bc, acum, acum_t)


def _ssd_bwd(xdtg, bc, acum, acum_t, dyg):
    nb_, s, _ = xdtg.shape
    ab = SEQ_BLOCK
    nblk = s // ab
    hpg = HEADS_PER_GROUP

    def body(x_ref, b_ref, c_ref, ac_ref, at_ref, dy_ref, dx_ref, db_ref, dc_ref, dac_ref, dar_ref):
        dx_ref[...] = jnp.zeros_like(dx_ref)
        db_ref[...] = jnp.zeros_like(db_ref)
        dac_ref[...] = jnp.zeros_like(dac_ref)
        dar_ref[...] = jnp.zeros_like(dar_ref)

        def outer(i, carry):
            qs = pl.ds(pl.multiple_of(i * ab, ab), ab)
            ci = c_ref[0, qs, :]
            dyi = [_bf(dy_ref[0, qs, HEAD_DIM * j:HEAD_DIM * (j + 1)]) for j in range(hpg)]
            arow = [at_ref[0, j:j + 1, qs] for j in range(hpg)]

            def inner(jb, st):
                dc_acc, rs = st[0], list(st[1:])
                ks = pl.ds(pl.multiple_of(jb * ab, ab), ab)
                bj = b_ref[0, ks, :]
                cbt = lax.dot_general(bj, ci, _NT, preferred_element_type=F32)
                ok = _causal_ok_t(i, jb)
                dcbt = jnp.zeros((ab, ab), F32)
                for j in range(hpg):
                    hs = slice(HEAD_DIM * j, HEAD_DIM * (j + 1))
                    decay = jnp.exp(jnp.where(ok, arow[j] - ac_ref[0, ks, j:j + 1], NEG))
                    gt = cbt * decay
                    dgt = lax.dot_general(x_ref[0, ks, hs], dyi[j], _NT, preferred_element_type=F32)
                    dx_ref[0, ks, hs] += jnp.dot(_bf(gt), dyi[j], preferred_element_type=F32)
                    dcbt = dcbt + dgt * decay
                    mm = dgt * gt
                    rs[j] = rs[j] + jnp.sum(mm, axis=0, keepdims=True)
                    dac_ref[0, ks, j:j + 1] -= jnp.sum(mm, axis=1, keepdims=True)
                dcbt16 = _bf(dcbt)
                db_ref[0, ks, :] += jnp.dot(dcbt16, ci, preferred_element_type=F32)
                return (dc_acc + lax.dot_general(dcbt16, bj, _TN, preferred_element_type=F32), *rs)

            init = (jnp.zeros((ab, D_STATE), F32),) + tuple(jnp.zeros((1, ab), F32) for _ in range(hpg))
            st = lax.fori_loop(0, i + 1, inner, init)
            dc_ref[0, qs, :] = st[0]
            for j in range(hpg):
                dar_ref[0, j:j + 1, qs] = st[1 + j]
            return carry

        lax.fori_loop(0, nblk, outer, 0)

    xblk = pl.BlockSpec((1, s, GROUP_LANES), lambda b, g: (b, 0, g))
    sblk = pl.BlockSpec((1, s, D_STATE), lambda b, g: (b, 0, g))
    tblk = pl.BlockSpec((1, 8, s), lambda b, g: (b, (LANES // 8) * g, 0))
    return pl.pallas_call(
        body, name="ssd_bwd", grid=(nb_, N_GROUPS),
        in_specs=[xblk, sblk, pl.BlockSpec((1, s, D_STATE), lambda b, g: (b, 0, N_GROUPS + g)), sblk, tblk, xblk],
        out_specs=[xblk, sblk, sblk, sblk, pl.BlockSpec((1, 8, s), lambda b, g: (b, g, 0))],
        out_shape=[jax.ShapeDtypeStruct((nb_, s, N_GROUPS * GROUP_LANES), F32),
                   jax.ShapeDtypeStruct((nb_, s, N_GROUPS * D_STATE), F32),
                   jax.ShapeDtypeStruct((nb_, s, N_GROUPS * D_STATE), F32),
                   jax.ShapeDtypeStruct((nb_, s, N_GROUPS * LANES), F32),
                   jax.ShapeDtypeStruct((nb_, N_GROUPS * 8, s), F32)],
        compiler_params=_params("parallel", "parallel"),
    )(xdtg, bc, bc, acum, acum_t, dyg)


def _ssd_chunk(s_in, x, bm_t, cm, cb, acol, arow, a_prev, ok):
    q = x.shape[0]
    decay = jnp.exp(jnp.where(ok, acol - arow, NEG))
    y = jnp.dot(_bf(cb * decay), x, preferred_element_type=F32)
    y = y + jnp.exp(acol - a_prev) * jnp.dot(cm, _bf(s_in), preferred_element_type=F32)
    a_end = acol[q - 1:q, :]
    wx = _bf(jnp.exp(a_end - acol) * x.astype(F32))
    s_out = jnp.exp(a_end - a_prev) * s_in + jnp.dot(bm_t, wx, preferred_element_type=F32)
    return y, s_out


def _ssd_specs(s):
    xblk = pl.BlockSpec((1, s, GROUP_LANES), lambda b, g: (b, 0, g))
    bblk = pl.BlockSpec((1, s, D_STATE), lambda b, g: (b, 0, g))
    cblk = pl.BlockSpec((1, s, D_STATE), lambda b, g: (b, 0, N_GROUPS + g))
    tblk = pl.BlockSpec((1, 8, s), lambda b, g: (b, (LANES // 8) * g, 0))
    return xblk, bblk, cblk, tblk


def _chunk_views(i, j, x_ref, ac_ref, at_ref):
    ab = SEQ_BLOCK
    sl = slice(ab * i, ab * (i + 1))
    hs = slice(HEAD_DIM * j, HEAD_DIM * (j + 1))
    a_prev = jnp.zeros((1, 1), F32) if i == 0 else ac_ref[0, ab * i - 1:ab * i, j:j + 1]
    return sl, hs, ac_ref[0, sl, j:j + 1], at_ref[0, j:j + 1, sl], a_prev


def _ssd_fwd_chunked(xdtg, bc, acum, acum_t):
    nb_, s, _ = xdtg.shape
    ab = SEQ_BLOCK
    hpg = HEADS_PER_GROUP

    def body(x_ref, b_ref, c_ref, ac_ref, at_ref, y_ref):
        ok = _causal_ok(0, 0)
        states = [jnp.zeros((D_STATE, HEAD_DIM), F32) for _ in range(hpg)]
        for i in range(s // ab):
            bm, cm = b_ref[0, ab * i:ab * (i + 1), :], c_ref[0, ab * i:ab * (i + 1), :]
            bm_t = bm.T
            cb = jnp.dot(cm, bm_t, preferred_element_type=F32)
            ys = []
            for j in range(hpg):
                sl, hs, acol, arow, a_prev = _chunk_views(i, j, x_ref, ac_ref, at_ref)
                y, states[j] = _ssd_chunk(states[j], x_ref[0, sl, hs], bm_t, cm, cb, acol, arow, a_prev, ok)
                ys.append(y)
            y_ref[0, sl, :] = jnp.concatenate(ys + [jnp.zeros((ab, GROUP_LANES - hpg * HEAD_DIM), F32)], axis=1)

    xblk, bblk, cblk, tblk = _ssd_specs(s)
    ablk = pl.BlockSpec((1, s, LANES), lambda b, g: (b, 0, g))
    return pl.pallas_call(
        body, name="ssd_fwd", grid=(nb_, N_GROUPS), in_specs=[xblk, bblk, cblk, ablk, tblk], out_specs=xblk,
        out_shape=jax.ShapeDtypeStruct((nb_, s, N_GROUPS * GROUP_LANES), F32),
        compiler_params=_params("parallel", "parallel"),
    )(xdtg, bc, bc, acum, acum_t)


def _ssd_bwd_chunked(xdtg, bc, acum, acum_t, dyg):
    nb_, s, _ = xdtg.shape
    ab = SEQ_BLOCK
    nblk = s // ab
    hpg = HEADS_PER_GROUP

    def body(x_ref, b_ref, c_ref, ac_ref, at_ref, dy_ref, dx_ref, db_ref, dc_ref, dac_ref, dar_ref, s_s):
        ok = _causal_ok(0, 0)
        dx_ref[...] = jnp.zeros_like(dx_ref)
        dac_ref[...] = jnp.zeros_like(dac_ref)
        dar_ref[...] = jnp.zeros_like(dar_ref)
        states = [jnp.zeros((D_STATE, HEAD_DIM), F32) for _ in range(hpg)]
        for i in range(nblk):
            bm_t = b_ref[0, ab * i:ab * (i + 1), :].T
            for j in range(hpg):
                sl, hs, acol, arow, a_prev = _chunk_views(i, j, x_ref, ac_ref, at_ref)
                s_s[hpg * i + j] = states[j]
                if i + 1 < nblk:
                    a_end = acol[ab - 1:ab, :]
                    wx = _bf(jnp.exp(a_end - acol) * x_ref[0, sl, hs].astype(F32))
                    states[j] = jnp.exp(a_end - a_prev) * states[j] + jnp.dot(bm_t, wx, preferred_element_type=F32)
        ok_t = _causal_ok_t(0, 0)
        last_row = lax.broadcasted_iota(jnp.int32, (ab, 1), 0) == ab - 1
        d_state = [jnp.zeros((D_STATE, HEAD_DIM), F32) for _ in range(hpg)]
        pending = [jnp.zeros((1, 1), F32) for _ in range(hpg)]
        total = lambda v: jnp.sum(v, keepdims=True)
        for i in reversed(range(nblk)):
            bm, cm = b_ref[0, ab * i:ab * (i + 1), :], c_ref[0, ab * i:ab * (i + 1), :]
            cm_t = cm.T
            cbt = jnp.dot(bm, cm_t, preferred_element_type=F32)
            dcbt = jnp.zeros((ab, ab), F32)
            d_bm, d_cm = jnp.zeros((ab, D_STATE), F32), jnp.zeros((ab, D_STATE), F32)
            for j in range(hpg):
                sl, hs, acol, arow, a_prev = _chunk_views(i, j, x_ref, ac_ref, at_ref)
                x, dy = x_ref[0, sl, hs], dy_ref[0, sl, hs]
                dy16 = _bf(dy)
                s_in, g_out = s_s[hpg * i + j], d_state[j]
                s16, g16 = _bf(s_in), _bf(g_out)
                decay = jnp.exp(jnp.where(ok_t, arow - acol, NEG))
                gt = cbt * decay
                dgt = lax.dot_general(x, dy16, _NT, preferred_element_type=F32)
                d_x = jnp.dot(_bf(gt), dy16, preferred_element_type=F32)
                dcbt = dcbt + dgt * decay
                mm = dgt * gt
                d_arow = jnp.sum(mm, axis=0, keepdims=True)
                d_acol = -jnp.sum(mm, axis=1, keepdims=True)
                e = jnp.exp(acol - a_prev)
                edy16 = _bf(e * dy)
                d_cm = d_cm + lax.dot_general(edy16, s16, _NT, preferred_element_type=F32)
                d_s = jnp.dot(cm_t, edy16, preferred_element_type=F32)
                de_e = jnp.sum(dy * jnp.dot(cm, s16, preferred_element_type=F32), axis=1, keepdims=True) * e
                a_end = acol[ab - 1:ab, :]
                w = jnp.exp(a_end - acol)
                f = jnp.exp(a_end - a_prev)
                x32 = x.astype(F32)
                bg = jnp.dot(bm, g16, preferred_element_type=F32)
                d_x = d_x + w * bg
                d_bm = d_bm + lax.dot_general(_bf(w * x32), g16, _NT, preferred_element_type=F32)
                dw_w = jnp.sum(bg * x32, axis=1, keepdims=True) * w
                df_f = total(g_out * s_in) * f
                d_end = total(dw_w) + df_f
                d_acol = d_acol + de_e - dw_w + jnp.where(last_row, d_end + pending[j], 0.0)
                pending[j] = -total(de_e) - df_f
                d_state[j] = d_s + f * g_out
                dx_ref[0, sl, hs] = d_x
                dac_ref[0, sl, j:j + 1] = d_acol
                dar_ref[0, j:j + 1, sl] = d_arow
            dcbt16 = _bf(dcbt)
            db_ref[0, ab * i:ab * (i + 1), :] = d_bm + jnp.dot(dcbt16, cm, preferred_element_type=F32)
            dc_ref[0, ab * i:ab * (i + 1), :] = d_cm + lax.dot_general(dcbt16, bm, _TN, preferred_element_type=F32)

    xblk, bblk, cblk, tblk = _ssd_specs(s)
    ablk = pl.BlockSpec((1, s, LANES), lambda b, g: (b, 0, g))
    return pl.pallas_call(
        body, name="ssd_bwd", grid=(nb_, N_GROUPS),
        in_specs=[xblk, bblk, cblk, ablk, tblk, xblk],
        out_specs=[xblk, bblk, bblk, ablk, pl.BlockSpec((1, 8, s), lambda b, g: (b, g, 0))],
        out_shape=[jax.ShapeDtypeStruct((nb_, s, N_GROUPS * GROUP_LANES), F32),
                   jax.ShapeDtypeStruct((nb_, s, N_GROUPS * D_STATE), F32),
                   jax.ShapeDtypeStruct((nb_, s, N_GROUPS * D_STATE), F32),
                   jax.ShapeDtypeStruct((nb_, s, N_GROUPS * LANES), F32),
                   jax.ShapeDtypeStruct((nb_, N_GROUPS * 8, s), F32)],
        scratch_shapes=[pltpu.VMEM((nblk * hpg, D_STATE, HEAD_DIM), F32)],
        compiler_params=_params("parallel", "parallel"),
    )(xdtg, bc, bc, acum, acum_t, dyg)


def _interleave(wg, wu):
    k, f = wg.shape
    gi = GATE_UP_INTERLEAVE
    return jnp.stack([wg.reshape(k, f // gi, gi), wu.reshape(k, f // gi, gi)], axis=2).reshape(k, 2 * f)


def _head_expanders():
    e_x = np.zeros((LANES, N_GROUPS * GROUP_LANES), np.float32)
    e_a = np.zeros((LANES, N_GROUPS * LANES), np.float32)
    for h in range(N_HEADS):
        g, j = divmod(h, HEADS_PER_GROUP)
        e_x[h, GROUP_LANES * g + HEAD_DIM * j:GROUP_LANES * g + HEAD_DIM * (j + 1)] = 1.0
        e_a[h, LANES * g + j] = 1.0
    return jnp.asarray(e_x), jnp.asarray(e_a)


def _pad_lanes(v, n=LANES):
    return jnp.pad(v, ((0, 0), (0, n - v.shape[1])))


def _local_step(x, positions, target, w, late_job=None, late_weights=None, early_grad_job=None):
    nb, s, d = x.shape
    t = nb * s
    x2 = x.reshape(t, d)
    tgt2 = target.reshape(t, d)

    x16 = _bf(x2)
    wgu1 = _interleave(w["ffn1_gate"], w["ffn1_up"])
    ffn1 = _ffn_fwd("ffn1_fwd", x16, x2, wgu1, w["ffn1_down"], w["ln1_g"], w["ln1_b"], carry=late_job)
    au1, hm1, h1, r1, h1_16 = ffn1[:5]
    if late_job is not None:
        w = {**w, **late_weights(ffn1[5])}

    wgu2 = _interleave(w["ffn2_gate"], w["ffn2_up"])
    w_in = w["w_in"]
    wqk, wv, wz = w_in[:, :2 * D_ATTN], w_in[:, 2 * D_ATTN:3 * D_ATTN], w_in[:, 3 * D_ATTN:3 * D_ATTN + D_SSD]
    wxbc = w_in[:, 3 * D_ATTN + D_SSD:3 * D_ATTN + D_SSD + D_CONV]
    wdt = _pad_lanes(w_in[:, 3 * D_ATTN + D_SSD + D_CONV:])

    inv_freq = ROPE_THETA ** (-jnp.arange(0, ROPE_DIM, 2, dtype=F32) / ROPE_DIM)
    half = ROPE_DIM // 2
    head_invf = jnp.concatenate([inv_freq, inv_freq, jnp.zeros((HEAD_DIM - ROPE_DIM,), F32)])
    head_sgn = jnp.concatenate([-jnp.ones((half,), F32), jnp.ones((half,), F32), jnp.zeros((HEAD_DIM - ROPE_DIM,), F32)])
    invf = jnp.tile(head_invf, LANES // HEAD_DIM)[None, :]
    sgn = jnp.tile(head_sgn, LANES // HEAD_DIM)[None, :]
    posf = positions.astype(F32).reshape(t, 1)
    bias_fwd, bias_bwd = _branch_bias_table(s, FWD_KEY_BLOCK), _branch_bias_table(s, SEQ_BLOCK)
    e_x, e_a = _head_expanders()
    dtb, alog = _pad_lanes(w["dt_bias"]), _pad_lanes(w["a_log"])
    dskip = jnp.repeat(w["d_skip"], HEAD_DIM, axis=1)

    qk =_mm("proj_qk", [(h1_16, wqk)], tn=768)
    v16 = _mm("proj_v", [(h1_16, wv)], tn=768, out_dtype=BF16)
    z = _mm("proj_z", [(h1_16, wz)], tn=768)
    xbc_pre = _mm("proj_xbc", [(h1_16, wxbc)], tn=896)
    dtp = _mm("proj_dt", [(h1_16, wdt)], tn=LANES)

    q16, k16, cs = _rope_fwd(qk, posf, invf, sgn)
    to3 = lambda a: a.reshape(nb, s, a.shape[-1])
    attn_o, lse = _attn_fwd(to3(q16), to3(k16), to3(v16), bias_fwd)

    xbc = _conv_fwd(to3(xbc_pre), w["conv_w"], w["conv_b"]).reshape(t, D_CONV)
    xdtg, bc16, dag = _ssd_prep_fwd(xbc, dtp, dtb, alog, e_x, e_a)
    acum, acum_t = _cumsum_fwd(to3(dag))
    yg = _ssd_fwd_chunked(to3(xdtg), to3(bc16), acum, acum_t)

    cat = _norms_fwd(attn_o.reshape(t, D_ATTN), yg.reshape(t, -1), xbc, z, w["attn_norm_w"], w["ssd_norm_w"], dskip)
    h2, r2, h2_16 = _mm_res_ln("w_out_ln2", cat, w["w_out"], h1, w["ln2_g"], w["ln2_b"], scale=1.0)

    au2, hm2, _, r3, _ = _ffn_fwd("ffn2_fwd", h2_16, h2, wgu2, w["ffn2_down"], w["ln3_g"], w["ln3_b"])

    g = {}
    dr3, dr3_16, g["ln3_g"], g["ln3_b"], loss = _ln_loss_bwd("loss_ln3_bwd", r3, w["ln3_g"], w["ln3_b"], tgt2)

    dau2, dh2 = _ffn_bwd("ffn2_bwd", dr3_16, dr3, w["ffn2_down"].T, au2, wgu2.T)
    g["ffn2_down"] = _mm_tn("ffn2_down_dw", hm2, dr3_16, scale=0.5, tk=D_FF // 2, tn=512)
    g["ffn2_gate"], g["ffn2_up"] = _mm_tn_gate_up("ffn2_up_dw", h2_16, dau2)

    dr2, dr2_16, g["ln2_g"], g["ln2_b"] = _ln_bwd("ln2_bwd", r2, w["ln2_g"], w["ln2_b"], dh2)
    dcat = _mm("w_out_dx", [(dr2_16, w["w_out"].T)], tn=768)
    g["w_out"] = _mm_tn("w_out_dw", cat, dr2_16, tk=768, tn=1024)

    d_attn, dyg, dxs_a, dz16, g["attn_norm_w"], g["ssd_norm_w"], ddskip = _norms_bwd(
        attn_o.reshape(t, D_ATTN), yg.reshape(t, -1), xbc, z, w["attn_norm_w"], w["ssd_norm_w"], dskip, dcat)
    g["d_skip"] = ddskip.reshape(N_HEADS, HEAD_DIM).sum(axis=1)[None, :]

    dq, dk, dv16 = _attn_bwd(to3(q16), to3(k16), to3(v16), attn_o, to3(d_attn), lse, bias_bwd)
    dqk16 = _rope_bwd(dq.reshape(t, D_ATTN), dk.reshape(t, D_ATTN), cs)

    dxdtg, dbm, dcm, dacol, darow = _ssd_bwd_chunked(to3(xdtg), to3(bc16), acum, acum_t, to3(dyg))
    ddag = _cumsum_bwd(dacol, darow)
    dxbc, ddtp16, ddtb, dalog = _ssd_prep_bwd(xbc, dtp, dtb, alog, e_x, e_a, dxdtg.reshape(t, -1), ddag.reshape(t, -1),
                                               dxs_a, dbm.reshape(t, -1), dcm.reshape(t, -1))
    g["dt_bias"], g["a_log"] = ddtb[:, :N_HEADS], dalog[:, :N_HEADS]
    dxbc_pre16, dconv_w, g["conv_b"] = _conv_bwd(to3(xbc_pre), w["conv_w"], w["conv_b"], to3(dxbc))
    g["conv_w"] = dconv_w[:CONV_WIDTH]
    dxbc_pre16 = dxbc_pre16.reshape(t, D_CONV)
    dv16 = dv16.reshape(t, D_ATTN)

    dh1 = _mm("w_in_dx", [(dqk16, wqk.T), (dv16, wv.T), (dz16, wz.T), (dxbc_pre16, wxbc.T), (ddtp16, wdt.T)],
              res=dr2, res_scale=ALPHA)
    g["w_in"] = jnp.concatenate([
        _mm_tn("w_in_dw_qk", h1_16, dqk16, tk=1024, tn=512),
        _mm_tn("w_in_dw_v", h1_16, dv16, tk=1024, tn=768),
        _mm_tn("w_in_dw_z", h1_16, dz16, tk=1024, tn=768),
        _mm_tn("w_in_dw_xbc", h1_16, dxbc_pre16, tk=1024, tn=896),
        _mm_tn("w_in_dw_dt", h1_16, ddtp16, tk=1024, tn=LANES)[:, :N_HEADS],
    ], axis=1)

    dr1, dr1_16, g["ln1_g"], g["ln1_b"] = _ln_bwd("ln1_bwd", r1, w["ln1_g"], w["ln1_b"], dh1)
    ffn1b = _ffn_bwd("ffn1_bwd", dr1_16, dr1, w["ffn1_down"].T, au1, wgu1.T,
                     carry=None if early_grad_job is None else early_grad_job(g))
    dau1, dx = ffn1b[:2]
    early = ffn1b[2] if early_grad_job is not None else None
    g["ffn1_down"] = _mm_tn("ffn1_down_dw", hm1, dr1_16, scale=0.5, tk=D_FF // 2, tn=512)
    g["ffn1_gate"], g["ffn1_up"] = _mm_tn_gate_up("ffn1_up_dw", x16, dau1)
    return loss, dx.reshape(nb, s, d), g, early


_HBM = pl.BlockSpec(memory_space=pltpu.HBM)
N_CHIPS = 4
N_DEVICES = 8


def _place():
    return lax.axis_index("x"), lax.axis_index("y"), lax.axis_index("c")


def _other_chips(x, y):
    return [(1 - x, y), (x, 1 - y), (1 - x, 1 - y)]


class _GatherJob:
    def __init__(self, shards):
        assert all((a.shape[0] // 2) % 16 == 0 for a in shards)
        self.n = len(shards)
        self.shapes = [a.shape for a in shards]
        self.operands = [a.reshape(2, a.shape[0] // 2, a.shape[1]) for a in shards]
        self.out_shape = [jax.ShapeDtypeStruct((N_CHIPS,) + a.shape, a.dtype) for a in self.operands]
        pair = pltpu.SemaphoreType.DMA((self.n, N_CHIPS - 1))
        self.scratch_shapes = [pair, pair, pair, pair]

    def results(self, outs):
        return [o.reshape((N_CHIPS,) + s) for o, s in zip(outs, self.shapes)]

    def phases(self, ins, outs, sems):
        n = self.n
        send_sems, recv_sems, fwd_send_sems, fwd_recv_sems = sems
        x, y, c = _place()
        me = 2 * x + y
        peers = _other_chips(x, y)

        def ici(t, p, src_chip):
            px, py = peers[p]
            return pltpu.make_async_remote_copy(
                ins[t].at[c] if src_chip is None else outs[t].at[src_chip, c],
                outs[t].at[me if src_chip is None else src_chip, c],
                send_sems.at[t, p], recv_sems.at[t, p], device_id=(px, py, c), device_id_type=MESH)

        def d2d(t, p, core):
            px, py = peers[p]
            return pltpu.make_async_remote_copy(
                outs[t].at[2 * px + py, core], outs[t].at[2 * px + py, core],
                fwd_send_sems.at[t, p], fwd_recv_sems.at[t, p], device_id=(x, y, 1 - c), device_id_type=MESH)

        pairs = [(t, p) for t in range(n) for p in range(N_CHIPS - 1)]

        def start():
            for t, p in pairs:
                ici(t, p, None).start()

        def forward():
            for t, p in pairs:
                px, py = peers[p]
                ici(t, p, 2 * px + py).wait_recv()
                d2d(t, p, c).start()

        def finish():
            for t, p in pairs:
                d2d(t, p, 1 - c).wait_recv()
            for t, p in pairs:
                ici(t, p, None).wait_send()
                d2d(t, p, c).wait_send()

        return start, forward, finish


class _ExchangeJob:
    def __init__(self, stacks):
        self.n = len(stacks)
        self.operands = list(stacks)
        self.out_shape = [jax.ShapeDtypeStruct(a.shape, a.dtype) for a in stacks]
        pair = pltpu.SemaphoreType.DMA((self.n, N_CHIPS - 1))
        self.scratch_shapes = [pair, pair]

    def results(self, outs):
        return list(outs)

    def phases(self, ins, outs, sems):
        send_sems, recv_sems = sems
        x, y, c = _place()
        me = 2 * x + y
        peers = _other_chips(x, y)
        pairs = [(t, p) for t in range(self.n) for p in range(N_CHIPS - 1)]

        def copy(t, p):
            px, py = peers[p]
            return pltpu.make_async_remote_copy(ins[t].at[2 * px + py], outs[t].at[me], send_sems.at[t, p],
                                                recv_sems.at[t, p], device_id=(px, py, c), device_id_type=MESH)

        def arrival(t, p):
            px, py = peers[p]
            return pltpu.make_async_remote_copy(ins[t].at[me], outs[t].at[2 * px + py], send_sems.at[t, p],
                                                recv_sems.at[t, p], device_id=(px, py, c), device_id_type=MESH)

        def start():
            for t, p in pairs:
                copy(t, p).start()

        def finish():
            for t, p in pairs:
                arrival(t, p).wait_recv()
            for t, p in pairs:
                copy(t, p).wait_send()

        return start, None, finish


def _run_job(job, name):
    n = job.n

    def body(*refs):
        for phase in job.phases(refs[:n], refs[n:2 * n], refs[2 * n:]):
            if phase is not None:
                phase()

    outs = pl.pallas_call(
        body, name=name, in_specs=[_HBM] * n, out_specs=[_HBM] * n,
        out_shape=job.out_shape, scratch_shapes=job.scratch_shapes,
    )(*job.operands)
    return job.results(outs)


def _sibling_halves(stacks, name):
    n = len(stacks)
    halves = [a.shape[1] // 2 for a in stacks]
    split = [a.reshape(a.shape[0], 2, h, a.shape[2]) for a, h in zip(stacks, halves)]

    def body(*refs):
        ins, outs = refs[:n], refs[n:2 * n]
        send_sems, recv_sems = refs[2 * n:]
        x, y, c = _place()
        cps = []
        for t in range(n):
            cp = pltpu.make_async_remote_copy(ins[t].at[:, 1 - c], outs[t], send_sems.at[t], recv_sems.at[t],
                                              device_id=(x, y, 1 - c), device_id_type=MESH)
            cp.start()
            cps.append(cp)
        for cp in cps:
            cp.wait()

    return pl.pallas_call(
        body, name=name,
        in_specs=[_HBM] * n, out_specs=[_HBM] * n,
        out_shape=[jax.ShapeDtypeStruct((a.shape[0], h, a.shape[2]), a.dtype) for a, h in zip(stacks, halves)],
        scratch_shapes=[pltpu.SemaphoreType.DMA((n,)), pltpu.SemaphoreType.DMA((n,))],
    )(*split)


def _sibling_swap(arrs):
    n = len(arrs)

    def body(*refs):
        ins, outs = refs[:n], refs[n:2 * n]
        send_sems, recv_sems = refs[2 * n:]
        x, y, c = _place()
        cps = []
        for t in range(n):
            cp = pltpu.make_async_remote_copy(ins[t], outs[t], send_sems.at[t], recv_sems.at[t],
                                              device_id=(x, y, 1 - c), device_id_type=MESH)
            cp.start()
            cps.append(cp)
        for cp in cps:
            cp.wait()

    return pl.pallas_call(
        body, name="sibling_swap",
        in_specs=[_HBM] * n, out_specs=[_HBM] * n,
        out_shape=[jax.ShapeDtypeStruct(a.shape, a.dtype) for a in arrs],
        scratch_shapes=[pltpu.SemaphoreType.DMA((n,)), pltpu.SemaphoreType.DMA((n,))],
    )(*arrs)


def _half_sum(name, own, other, core):
    k, r, cols = own.shape
    h = r // 2
    tr = next(cand for cand in (128, 176, 64, 32, 16) if h % cand == 0)
    nblk = h // tr

    def body(core_ref, own_ref, other_ref, o_ref):
        o_ref[...] = _bf(own_ref[...] + other_ref[...].astype(F32))

    grid_spec = pltpu.PrefetchScalarGridSpec(
        num_scalar_prefetch=1, grid=(nblk,),
        in_specs=[pl.BlockSpec((k, tr, cols), lambda i, core_ref: (0, i + core_ref[0] * nblk, 0)),
                  pl.BlockSpec((k, tr, cols), lambda i, core_ref: (0, i, 0))],
        out_specs=pl.BlockSpec((k, tr, cols), lambda i, core_ref: (0, i, 0)))
    return pl.pallas_call(
        body, name=name, grid_spec=grid_spec, out_shape=jax.ShapeDtypeStruct((k, h, cols), BF16),
        compiler_params=_params("parallel"),
    )(core.reshape(1).astype(jnp.int32), own, other)


def _small_allreduce(v):
    r = v.shape[0]

    def body(v_ref, tot_ref, slots, send_sems, recv_sems):
        x, y, c = _place()
        me = 4 * x + 2 * y + c
        slots[me] = v_ref[...]
        cps, peers = [], []
        for k in range(1, N_DEVICES):
            px = 1 - x if (k >> 2) & 1 else x
            py = 1 - y if (k >> 1) & 1 else y
            pc = 1 - c if k & 1 else c
            cp = pltpu.make_async_remote_copy(v_ref, slots.at[me], send_sems.at[k - 1], recv_sems.at[k - 1],
                                              device_id=(px, py, pc), device_id_type=MESH)
            cp.start()
            cps.append(cp)
            peers.append((px, py, pc))
        for k, (px, py, pc) in enumerate(peers):
            pltpu.make_async_remote_copy(v_ref, slots.at[4 * px + 2 * py + pc], send_sems.at[k], recv_sems.at[k],
                                         device_id=(px, py, pc), device_id_type=MESH).wait_recv()
        for cp in cps:
            cp.wait_send()
        acc = slots[0]
        for s in range(1, N_DEVICES):
            acc = acc + slots[s]
        tot_ref[...] = acc

    return pl.pallas_call(
        body, name="small_allreduce",
        in_specs=[pl.BlockSpec(memory_space=pltpu.VMEM)], out_specs=pl.BlockSpec(memory_space=pltpu.VMEM),
        out_shape=jax.ShapeDtypeStruct((r, LANES), F32),
        scratch_shapes=[pltpu.VMEM((N_DEVICES, r, LANES), F32), pltpu.SemaphoreType.DMA((N_DEVICES - 1,)),
                        pltpu.SemaphoreType.DMA((N_DEVICES - 1,))],
    )(v)


def _elementwise(name, fn, ins, out_dtypes):
    r, c = ins[0].shape[-2:]
    tr = next((cand for cand in (256, 176, 128, 64, 32, 16) if r % cand == 0), r)
    nin = len(ins)

    def body(*refs):
        outs = fn(*[ref[...] for ref in refs[:nin]])
        for o_ref, o in zip(refs[nin:], outs):
            o_ref[...] = o.astype(o_ref.dtype)

    in_specs = [pl.BlockSpec((tr, c), lambda i: (i, 0)) if a.ndim == 2 else pl.BlockSpec((a.shape[0], tr, c), lambda i: (0, i, 0))
                for a in ins]
    return pl.pallas_call(
        body, name=name, grid=(r // tr,), in_specs=in_specs,
        out_specs=[pl.BlockSpec((tr, c), lambda i: (i, 0)) for _ in out_dtypes],
        out_shape=[jax.ShapeDtypeStruct((r, c), dt) for dt in out_dtypes],
        compiler_params=_params("parallel"),
    )(*ins)


def _row_tile(rows):
    return next((cand for cand in (128, 176, 64, 32, 16) if rows % cand == 0), rows)


def _sum_slots(name, received, own, chip):
    _, r, cols = own.shape
    tr = _row_tile(r)

    def body(chip_ref, own_ref, a_ref, b_ref, c_ref, o_ref):
        o_ref[...] = ((own_ref[0].astype(F32) + a_ref[0].astype(F32)) + b_ref[0].astype(F32)) + c_ref[0].astype(F32)

    def slot(flip):
        return pl.BlockSpec((1, tr, cols), lambda i, chip_ref: (jnp.bitwise_xor(chip_ref[0], flip), i, 0))

    grid_spec = pltpu.PrefetchScalarGridSpec(
        num_scalar_prefetch=1, grid=(r // tr,), in_specs=[slot(0), slot(1), slot(2), slot(3)],
        out_specs=pl.BlockSpec((tr, cols), lambda i, chip_ref: (i, 0)))
    return pl.pallas_call(
        body, name=name, grid_spec=grid_spec, out_shape=jax.ShapeDtypeStruct((r, cols), F32),
        compiler_params=_params("parallel"),
    )(chip.reshape(1).astype(jnp.int32), own, received, received, received)


def _adamw_halves(name, mine, theirs, core, w, m, v):
    h, cols = mine.shape
    tr = _row_tile(h)
    nh = h // tr

    def body(core_ref, mine_ref, theirs_ref, w_ref, m_ref, v_ref, g_ref, d_ref, m2_ref, v2_ref):
        is_mine = (pl.program_id(0) // nh) == core_ref[0]
        g = jnp.where(is_mine, mine_ref[...], theirs_ref[...])
        outs = _adamw_math(g, w_ref[...], m_ref[...], v_ref[...])
        for ref, val in zip((g_ref, d_ref, m2_ref, v2_ref), outs):
            ref[...] = val

    half = pl.BlockSpec((tr, cols), lambda i, core_ref: (i % nh, 0))
    full = pl.BlockSpec((tr, cols), lambda i, core_ref: (i, 0))
    grid_spec = pltpu.PrefetchScalarGridSpec(
        num_scalar_prefetch=1, grid=(2 * nh,), in_specs=[half, half, full, full, full], out_specs=[full] * 4)
    return pl.pallas_call(
        body, name=name, grid_spec=grid_spec, out_shape=[jax.ShapeDtypeStruct((2 * h, cols), F32)] * 4,
        compiler_params=_params("parallel"),
    )(core.reshape(1).astype(jnp.int32), mine, theirs, w, m, v)


def _adamw_math(g, w_v, m_v, v_v):
    m2 = ADAM_B1 * m_v + (1.0 - ADAM_B1) * g
    v2 = ADAM_B2 * v_v + (1.0 - ADAM_B2) * jnp.square(g)
    m_hat = m2 / (1.0 - ADAM_B1 ** ADAM_STEP)
    v_hat = v2 / (1.0 - ADAM_B2 ** ADAM_STEP)
    delta = -ADAM_LR * (m_hat / (jnp.sqrt(v_hat) + ADAM_EPS) + ADAM_WD * w_v)
    return [g, delta, m2, v2]


def _adamw(name, g, w, m, v):
    return _elementwise(name, _adamw_math, [g, w, m, v], [F32] * 4)


_MATRICES = (("ffn1_gate", 1), ("ffn1_up", 1), ("ffn1_down", 0), ("w_in", 1), ("w_out", 0),
             ("ffn2_gate", 1), ("ffn2_up", 1), ("ffn2_down", 0))
_VECTORS = ("ln1_g", "ln1_b", "conv_b", "dt_bias", "a_log", "d_skip", "attn_norm_w", "ssd_norm_w",
            "ln2_g", "ln2_b", "ln3_g", "ln3_b")
_WEIGHT_ORDER = ("ln1_g", "ln1_b", "ffn1_gate", "ffn1_up", "ffn1_down", "w_in", "conv_w", "conv_b", "dt_bias", "a_log",
                 "d_skip", "attn_norm_w", "ssd_norm_w", "w_out", "ln2_g", "ln2_b", "ffn2_gate", "ffn2_up", "ffn2_down",
                 "ln3_g", "ln3_b")


def _pack_rows(vectors):
    parts = []
    for vec in vectors:
        flat = vec.reshape(-1)
        parts.append(jnp.pad(flat, (0, (-flat.shape[0]) % LANES)))
    flat = jnp.concatenate(parts)
    flat = jnp.pad(flat, (0, (-flat.shape[0]) % (8 * LANES)))
    return flat.reshape(-1, LANES)


def _unpack_rows(packed, shapes):
    flat = packed.reshape(-1)
    out, off = [], 0
    for shape in shapes:
        size = int(np.prod(shape))
        out.append(flat[off:off + size].reshape(shape))
        off += size + (-size) % LANES
    return out


def _assemble(stack, own, chip, axis):
    blocks = [jnp.where(chip == s, own, stack[s]) for s in range(N_CHIPS)]
    return jnp.concatenate(blocks, axis=axis)


def _split(full, axis):
    if axis == 0:
        return full.reshape(N_CHIPS, -1, full.shape[1])
    cols = full.shape[1] // N_CHIPS
    return jnp.stack([full[:, cols * s:cols * (s + 1)] for s in range(N_CHIPS)])


def kernel(x, positions, ln1_g, ln1_b, ffn1_gate, ffn1_up, ffn1_down, w_in, conv_w, conv_b, dt_bias, a_log, d_skip, attn_norm_w, ssd_norm_w, w_out, ln2_g, ln2_b, ffn2_gate, ffn2_up, ffn2_down, ln3_g, ln3_b, loss_target, m_ln1_g, m_ln1_b, m_ffn1_gate, m_ffn1_up, m_ffn1_down, m_w_in, m_conv_w, m_conv_b, m_dt_bias, m_a_log, m_d_skip, m_attn_norm_w, m_ssd_norm_w, m_w_out, m_ln2_g, m_ln2_b, m_ffn2_gate, m_ffn2_up, m_ffn2_down, m_ln3_g, m_ln3_b, v_ln1_g, v_ln1_b, v_ffn1_gate, v_ffn1_up, v_ffn1_down, v_w_in, v_conv_w, v_conv_b, v_dt_bias, v_a_log, v_d_skip, v_attn_norm_w, v_ssd_norm_w, v_w_out, v_ln2_g, v_ln2_b, v_ffn2_gate, v_ffn2_up, v_ffn2_down, v_ln3_g, v_ln3_b):
    given = dict(locals())
    wts = {n: given[n] for n in _WEIGHT_ORDER}
    mom_m = {n: given["m_" + n] for n in _WEIGHT_ORDER}
    mom_v = {n: given["v_" + n] for n in _WEIGHT_ORDER}
    chip = 2 * lax.axis_index("x") + lax.axis_index("y")

    core = lax.axis_index("c")
    first = [(n, axis) for n, axis in _MATRICES if n.startswith("ffn1")]
    rest = [(n, axis) for n, axis in _MATRICES if not n.startswith("ffn1")]
    own16 = {n: wts[n][0].astype(BF16) for n, _ in _MATRICES}
    gathered = _run_job(_GatherJob([own16[n] for n, _ in first]), "gather_ffn1")
    full = {n: _assemble(st, own16[n], chip, axis) for (n, axis), st in zip(first, gathered)}
    for n in _VECTORS:
        full[n] = wts[n]
    conv_rows = jnp.pad(wts["conv_w"][0], ((0, 32 - CONV_WIDTH), (0, 0)))
    late_job = _GatherJob([own16[n] for n, _ in rest] + [conv_rows])

    def late_weights(results):
        out = {n: _assemble(st, own16[n], chip, axis) for (n, axis), st in zip(rest, results)}
        out["conv_w"] = _assemble(results[-1], conv_rows, chip, 1)[:CONV_WIDTH]
        return out

    chip_sums = {}

    def core_sums(g, which, tag):
        partials = [_split(g[n], axis) for n, axis in which]
        from_sibling = _sibling_halves([p.astype(BF16) for p in partials], "sibling_halves_" + tag)
        for (n, _), p, o in zip(which, partials, from_sibling):
            chip_sums[n] = _half_sum("core_sum_" + n, p, o, core)
        return _ExchangeJob([chip_sums[n] for n, _ in which])

    loss, grad_x, g, received_rest = _local_step(x, positions, loss_target, full, late_job, late_weights,
                                                 lambda g_now: core_sums(g_now, rest, "rest"))
    received_first = _run_job(core_sums(g, first, "ffn1"), "exchange_ffn1")
    received = dict(zip([n for n, _ in first + rest], received_first + received_rest))
    half_totals = [_sum_slots("sum_partials_" + n, received[n], chip_sums[n], chip) for n, _ in _MATRICES]
    other_halves = _sibling_swap(half_totals)

    small_shapes = [g[n].shape for n in _VECTORS] + [g["conv_w"].shape, (1,)]
    total = _small_allreduce(_pack_rows([g[n] for n in _VECTORS] + [g["conv_w"], loss[0, :1]]))
    small = _unpack_rows(total, small_shapes)
    loss_out = small[-1].reshape(())

    grads, deltas, new_m, new_v = {}, {}, {}, {}
    for (n, _), mine, theirs in zip(_MATRICES, half_totals, other_halves):
        res = _adamw_halves("adamw_" + n, mine, theirs, core, wts[n][0], mom_m[n][0], mom_v[n][0])
        grads[n], deltas[n], new_m[n], new_v[n] = [r[None] for r in res]

    vec_shapes = [wts[n].shape for n in _VECTORS]
    res = _adamw("adamw_vectors", _pack_rows(small[:len(_VECTORS)]), _pack_rows([wts[n] for n in _VECTORS]),
                 _pack_rows([mom_m[n] for n in _VECTORS]), _pack_rows([mom_v[n] for n in _VECTORS]))
    for dst, packed in zip((grads, deltas, new_m, new_v), res):
        for n, val in zip(_VECTORS, _unpack_rows(packed, vec_shapes)):
            dst[n] = val

    cols = conv_w.shape[2]
    g_conv = lax.dynamic_slice_in_dim(small[len(_VECTORS)], chip * cols, cols, axis=1)
    res = _adamw("adamw_conv_w", g_conv, wts["conv_w"][0], mom_m["conv_w"][0], mom_v["conv_w"][0])
    grads["conv_w"], deltas["conv_w"], new_m["conv_w"], new_v["conv_w"] = [r[None] for r in res]

    return (loss_out, grad_x, *[grads[n] for n in _WEIGHT_ORDER], *[deltas[n] for n in _WEIGHT_ORDER],
            *[new_m[n] for n in _WEIGHT_ORDER], *[new_v[n] for n in _WEIGHT_ORDER])
```

```python
import functools

import numpy as np
import jax
import jax.numpy as jnp
from jax import lax
from jax.experimental import pallas as pl
from jax.experimental.pallas import tpu as pltpu

F32, BF16 = jnp.float32, jnp.bfloat16

D_MODEL = 1024
D_FF = 2816
N_HEADS = 12
HEAD_DIM = 64
D_ATTN = 768
D_SSD = 768
N_GROUPS = 4
HEADS_PER_GROUP = 3
D_STATE = 128
D_CONV = 1792
CONV_WIDTH = 4
ROPE_DIM = 16
ROPE_THETA = 500000.0
ALPHA = 2.0 ** 0.25
LN_EPS = 1e-5
RMS_EPS = 1e-6
ADAM_LR, ADAM_B1, ADAM_B2, ADAM_EPS, ADAM_WD, ADAM_STEP = 0.001, 0.9, 0.999, 1e-08, 0.01, 10

LANES = 128
GATE_UP_INTERLEAVE = 256
SEQ_BLOCK = 256
GROUP_LANES = 256
VMEM_LIMIT = 56 * 1024 * 1024
NEG = -1e30
MESH = pl.DeviceIdType.MESH
HIGHEST = lax.Precision.HIGHEST

_NT = (((1,), (1,)), ((), ()))
_TN = (((0,), (0,)), ((), ()))


def _params(*sem):
    return pltpu.CompilerParams(dimension_semantics=sem, vmem_limit_bytes=VMEM_LIMIT)


def _bf(v):
    return v.astype(BF16)


EPILOGUE_ROWS = 128


def _row_chunks(tm):
    return [slice(r, min(r + EPILOGUE_ROWS, tm)) for r in range(0, tm, EPILOGUE_ROWS)]


def _sigmoid(v):
    return 0.5 * jnp.tanh(0.5 * v) + 0.5


def _mm(name, pairs, *, scale=1.0, res=None, res_scale=1.0, out_dtype=F32, tm=512, tn=512):
    m, n = pairs[0][0].shape[0], pairs[0][1].shape[1]
    tm, tn = min(tm, m), min(tn, n)
    assert m % tm == 0 and n % tn == 0, (name, m, n, tm, tn)
    npair = len(pairs)

    def body(*refs):
        acc = None
        for a_ref, b_ref in zip(refs[:npair], refs[npair:2 * npair]):
            d = jnp.dot(_bf(a_ref[...]), b_ref[...], preferred_element_type=F32)
            acc = d if acc is None else acc + d
        if scale != 1.0:
            acc = acc * scale
        if res is not None:
            acc = acc + res_scale * refs[2 * npair][...]
        refs[-1][...] = acc.astype(out_dtype)

    in_specs = [pl.BlockSpec((tm, a.shape[1]), lambda i, j: (i, 0)) for a, _ in pairs]
    in_specs += [pl.BlockSpec((b.shape[0], tn), lambda i, j: (0, j)) for _, b in pairs]
    args = [a for a, _ in pairs] + [b for _, b in pairs]
    if res is not None:
        in_specs.append(pl.BlockSpec((tm, tn), lambda i, j: (i, j)))
        args.append(res)
    return pl.pallas_call(
        body, name=name, grid=(m // tm, n // tn), in_specs=in_specs,
        out_specs=pl.BlockSpec((tm, tn), lambda i, j: (i, j)),
        out_shape=jax.ShapeDtypeStruct((m, n), out_dtype),
        compiler_params=_params("parallel", "parallel"),
    )(*args)


def _mm_tn(name, x, dy, *, scale=1.0, tk=512, tn=512, tt=1024):
    t, k = x.shape
    n = dy.shape[1]
    tk, tn, tt = min(tk, k), min(tn, n), min(tt, t)
    assert k % tk == 0 and n % tn == 0 and t % tt == 0, (name, k, n, t)
    nt = t // tt

    def body(x_ref, dy_ref, o_ref):
        step = pl.program_id(2)
        d = lax.dot_general(_bf(x_ref[...]), _bf(dy_ref[...]), _TN, preferred_element_type=F32)

        @pl.when(step == 0)
        def _():
            o_ref[...] = d

        @pl.when(step > 0)
        def _():
            o_ref[...] += d

        if scale != 1.0:
            @pl.when(step == nt - 1)
            def _():
                o_ref[...] = o_ref[...] * scale

    return pl.pallas_call(
        body, name=name, grid=(k // tk, n // tn, nt),
        in_specs=[pl.BlockSpec((tt, tk), lambda i, j, s: (s, i)), pl.BlockSpec((tt, tn), lambda i, j, s: (s, j))],
        out_specs=pl.BlockSpec((tk, tn), lambda i, j, s: (i, j)),
        out_shape=jax.ShapeDtypeStruct((k, n), F32),
        compiler_params=_params("parallel", "parallel", "arbitrary"),
    )(x, dy)


def _mm_tn_sections(name, x, dys, *, tt=512):
    t, k = x.shape
    tt = min(tt, t)
    cuts = np.cumsum([0] + [d.shape[1] for d in dys]).tolist()
    ns = len(dys)

    def body(*refs):
        x_ref, o_ref = refs[0], refs[1 + ns]
        step = pl.program_id(0)
        xt = x_ref[...].T
        parts = [jnp.dot(xt, refs[1 + a][...], preferred_element_type=F32) for a in range(ns)]

        @pl.when(step == 0)
        def _():
            for a in range(ns):
                o_ref[:, cuts[a]:cuts[a + 1]] = parts[a]

        @pl.when(step > 0)
        def _():
            for a in range(ns):
                o_ref[:, cuts[a]:cuts[a + 1]] += parts[a]

    return pl.pallas_call(
        body, name=name, grid=(t // tt,),
        in_specs=[pl.BlockSpec((tt, k), lambda s: (s, 0))] + [pl.BlockSpec((tt, d.shape[1]), lambda s: (s, 0)) for d in dys],
        out_specs=pl.BlockSpec((k, cuts[-1]), lambda s: (0, 0)),
        out_shape=jax.ShapeDtypeStruct((k, cuts[-1]), F32),
        compiler_params=_params("arbitrary"),
    )(x, *dys)


def _mm_tn_gate_up(name, x, dau, *, tt=1024):
    t, k = x.shape
    gi = GATE_UP_INTERLEAVE
    nj = dau.shape[1] // (2 * gi)
    tt = min(tt, t)
    nt = t // tt

    def body(x_ref, dy_ref, g_ref, u_ref):
        step = pl.program_id(1)
        d = lax.dot_general(_bf(x_ref[...]), dy_ref[...], _TN, preferred_element_type=F32)

        @pl.when(step == 0)
        def _():
            g_ref[...] = d[:, :gi]
            u_ref[...] = d[:, gi:]

        @pl.when(step > 0)
        def _():
            g_ref[...] += d[:, :gi]
            u_ref[...] += d[:, gi:]

    out = pl.BlockSpec((k, gi), lambda j, s: (0, j))
    return pl.pallas_call(
        body, name=name, grid=(nj, nt),
        in_specs=[pl.BlockSpec((tt, k), lambda j, s: (s, 0)), pl.BlockSpec((tt, 2 * gi), lambda j, s: (s, j))],
        out_specs=[out, out],
        out_shape=[jax.ShapeDtypeStruct((k, gi * nj), F32)] * 2,
        compiler_params=_params("parallel", "arbitrary"),
    )(x, dau)


def _carried(carry, ins, outs, sems, step, total):
    start, forward, finish = carry.phases(ins, outs, sems)
    pl.when(step == 0)(start)
    if forward is not None:
        pl.when(step == (3 * total) // 4)(forward)
    return lambda: pl.when(step == total - 1)(finish)


def _resident(shape):
    return pl.BlockSpec(shape, lambda i: (0,) * len(shape), pipeline_mode=pl.Buffered(1))


def _ffn_fwd(name, x16, res, wgu, wd, g, b, *, tm=512, carry=None):
    t, k = x16.shape
    gi = GATE_UP_INTERLEAVE
    nj, n, ni = wd.shape[0] // gi, wd.shape[1], t // tm
    nc = carry.n if carry is not None else 0

    def body(*refs):
        x_ref, res_ref, wgu_ref, wd_ref, g_ref, b_ref = refs[:6]
        au_ref, hm_ref, y_ref, r_ref, y16_ref = refs[6 + nc:11 + nc]
        if carry is not None:
            finish = _carried(carry, refs[6:6 + nc], refs[11 + nc:11 + 2 * nc], refs[11 + 2 * nc:], pl.program_id(0), ni)
        xv = x_ref[...]
        acc = jnp.zeros((tm, n), F32)
        for j in range(nj):
            au = jnp.dot(xv, wgu_ref[:, 2 * gi * j:2 * gi * (j + 1)], preferred_element_type=F32)
            a, u = au[:, :gi], au[:, gi:]
            au_ref[:, 2 * gi * j:2 * gi * (j + 1)] = _bf(au)
            hm = _bf(a * _sigmoid(a) * u)
            hm_ref[:, gi * j:gi * (j + 1)] = hm
            acc = acc + jnp.dot(hm, wd_ref[gi * j:gi * (j + 1), :], preferred_element_type=F32)
        r = ALPHA * res_ref[...] + 0.5 * acc
        r_ref[...] = r
        y = _layer_norm(r, g_ref[...], b_ref[...])
        y_ref[...] = y
        y16_ref[...] = _bf(y)
        if carry is not None:
            finish()

    row = lambda c: pl.BlockSpec((tm, c), lambda i: (i, 0))
    hbm = pl.BlockSpec(memory_space=pltpu.HBM)
    res_ = pl.pallas_call(
        body, name=name, grid=(ni,),
        in_specs=[row(k), row(n), _resident(wgu.shape), _resident(wd.shape), _resident(g.shape), _resident(b.shape)] + [hbm] * nc,
        out_specs=[row(2 * gi * nj), row(gi * nj), row(n), row(n), row(n)] + [hbm] * nc,
        out_shape=[jax.ShapeDtypeStruct((t, 2 * gi * nj), BF16), jax.ShapeDtypeStruct((t, gi * nj), BF16),
                   jax.ShapeDtypeStruct((t, n), F32), jax.ShapeDtypeStruct((t, n), F32), jax.ShapeDtypeStruct((t, n), BF16)]
        + (carry.out_shape if carry is not None else []),
        scratch_shapes=carry.scratch_shapes if carry is not None else [],
        compiler_params=_params("arbitrary" if carry is not None else "parallel"),
    )(x16, res, wgu, wd, g, b, *(carry.operands if carry is not None else []))
    return tuple(res_[:5]) + ((carry.results(res_[5:]),) if carry is not None else ())


def _ffn_bwd(name, dr16, dr, wdt, au, wgut, *, tm=512, carry=None):
    t, n = dr16.shape
    gi = GATE_UP_INTERLEAVE
    nj, ni = wdt.shape[1] // gi, t // tm
    nc = carry.n if carry is not None else 0

    def body(*refs):
        dr16_ref, dr_ref, wdt_ref, au_ref, wgut_ref = refs[:5]
        dau_ref, dx_ref = refs[5 + nc:7 + nc]
        if carry is not None:
            finish = _carried(carry, refs[5:5 + nc], refs[7 + nc:7 + 2 * nc], refs[7 + 2 * nc:], pl.program_id(0), ni)
        drv = dr16_ref[...]
        acc = jnp.zeros((tm, n), F32)
        for j in range(nj):
            dhm = jnp.dot(drv, wdt_ref[:, gi * j:gi * (j + 1)], preferred_element_type=F32) * 0.5
            au_v = au_ref[:, 2 * gi * j:2 * gi * (j + 1)].astype(F32)
            a, u = au_v[:, :gi], au_v[:, gi:]
            sig = _sigmoid(a)
            silu = a * sig
            dau = jnp.concatenate([_bf(dhm * u * (sig + silu - silu * sig)), _bf(dhm * silu)], axis=1)
            dau_ref[:, 2 * gi * j:2 * gi * (j + 1)] = dau
            acc = acc + jnp.dot(dau, wgut_ref[2 * gi * j:2 * gi * (j + 1), :], preferred_element_type=F32)
        dx_ref[...] = ALPHA * dr_ref[...] + acc
        if carry is not None:
            finish()

    row = lambda c: pl.BlockSpec((tm, c), lambda i: (i, 0))
    hbm = pl.BlockSpec(memory_space=pltpu.HBM)
    res_ = pl.pallas_call(
        body, name=name, grid=(ni,),
        in_specs=[row(n), row(n), _resident(wdt.shape), row(2 * gi * nj), _resident(wgut.shape)] + [hbm] * nc,
        out_specs=[row(2 * gi * nj), row(n)] + [hbm] * nc,
        out_shape=[jax.ShapeDtypeStruct((t, 2 * gi * nj), BF16), jax.ShapeDtypeStruct((t, n), F32)]
        + (carry.out_shape if carry is not None else []),
        scratch_shapes=carry.scratch_shapes if carry is not None else [],
        compiler_params=_params("arbitrary" if carry is not None else "parallel"),
    )(dr16, dr, wdt, au, wgut, *(carry.operands if carry is not None else []))
    return tuple(res_[:2]) + ((carry.results(res_[2:]),) if carry is not None else ())


def _layer_norm(r, g, b):
    mu = jnp.mean(r, axis=-1, keepdims=True)
    var = jnp.mean(jnp.square(r - mu), axis=-1, keepdims=True)
    return (r - mu) * lax.rsqrt(var + LN_EPS) * g + b


def _mm_res_ln(name, a, w, res, g, b, *, scale, tm=256):
    t, k = a.shape
    n = w.shape[1]

    def body(a_ref, w_ref, res_ref, g_ref, b_ref, y_ref, r_ref, y16_ref):
        for rows in _row_chunks(tm):
            r = ALPHA * res_ref[rows, :] + scale * jnp.dot(_bf(a_ref[rows, :]), w_ref[...], preferred_element_type=F32)
            r_ref[rows, :] = r
            y = _layer_norm(r, g_ref[...], b_ref[...])
            y_ref[rows, :] = y
            y16_ref[rows, :] = _bf(y)

    row = lambda c: pl.BlockSpec((tm, c), lambda i: (i, 0))
    const = lambda shape: pl.BlockSpec(shape, lambda i: (0, 0))
    return pl.pallas_call(
        body, name=name, grid=(t // tm,),
        in_specs=[row(k), const((k, n)), row(n), const((1, n)), const((1, n))],
        out_specs=[row(n), row(n), row(n)],
        out_shape=[jax.ShapeDtypeStruct((t, n), F32), jax.ShapeDtypeStruct((t, n), F32), jax.ShapeDtypeStruct((t, n), BF16)],
        compiler_params=_params("parallel"),
    )(a, w, res, g, b)


def _rowwise(name, fn, rows, consts, row_outs, acc_outs=(), tm=256):
    rows = [r if isinstance(r, tuple) else (r, r.shape[1]) for r in rows]
    t = rows[0][0].shape[0]
    tm = min(tm, t)
    assert t % tm == 0
    nr, nc, no, na = len(rows), len(consts), len(row_outs), len(acc_outs)

    def body(*refs):
        vals = [r[...] for r in refs[:nr + nc]]
        outs, accs = fn(*vals)
        for o_ref, o in zip(refs[nr + nc:nr + nc + no], outs):
            o_ref[...] = o.astype(o_ref.dtype)
        if na:
            step = pl.program_id(0)
            acc_refs = refs[nr + nc + no:]

            @pl.when(step == 0)
            def _():
                for a_ref, a in zip(acc_refs, accs):
                    a_ref[...] = a

            @pl.when(step > 0)
            def _():
                for a_ref, a in zip(acc_refs, accs):
                    a_ref[...] += a

    in_specs = [pl.BlockSpec((tm, w), lambda i: (i, 0)) for _, w in rows]
    in_specs += [pl.BlockSpec(c.shape, lambda i, nd=c.ndim: (0,) * nd) for c in consts]
    out_specs = [pl.BlockSpec((tm, c), lambda i: (i, 0)) for c, _ in row_outs]
    out_specs += [pl.BlockSpec(s, lambda i: (0, 0)) for s in acc_outs]
    out_shape = [jax.ShapeDtypeStruct((t, c), dt) for c, dt in row_outs]
    out_shape += [jax.ShapeDtypeStruct(s, F32) for s in acc_outs]
    res = pl.pallas_call(
        body, name=name, grid=(t // tm,), in_specs=in_specs, out_specs=out_specs, out_shape=out_shape,
        compiler_params=_params("arbitrary" if na else "parallel"),
    )(*[r for r, _ in rows], *consts)
    return res


def _ln_bwd(name, r, g, b, dy):
    def fn(r_v, dy_v, g_v, b_v):
        _, vjp = jax.vjp(_layer_norm, r_v, g_v, b_v)
        dr, dg, db = vjp(dy_v)
        return [dr, dr], [dg, db]
    return _rowwise(name, fn, [r, dy], [g, b], [(r.shape[1], F32), (r.shape[1], BF16)], [(1, r.shape[1])] * 2)


def _ln_loss_bwd(name, r, g, b, target):
    def fn(r_v, t_v, g_v, b_v):
        def loss_fn(rr, gg, bb):
            err = jnp.square(_layer_norm(rr, gg, bb) - t_v)
            return 0.5 * jnp.sum(jnp.mean(err, axis=-1, keepdims=True), axis=0, keepdims=True)
        loss, vjp = jax.vjp(loss_fn, r_v, g_v, b_v)
        dr, dg, db = vjp(jnp.ones((1, 1), F32))
        return [dr, dr], [dg, db, jnp.broadcast_to(loss, (1, LANES))]
    return _rowwise(name, fn, [r, target], [g, b], [(r.shape[1], F32), (r.shape[1], BF16)],
                    [(1, r.shape[1])] * 2 + [(1, LANES)])


def _rope_tables(posf, invf, sgn):
    ang = posf * invf
    return jnp.cos(ang), jnp.sin(ang) * sgn


def _rope_apply(tv, cos, sin):
    lane = lax.broadcasted_iota(jnp.int32, cos.shape, 1)
    first = (lane % HEAD_DIM) < (ROPE_DIM // 2)
    outs = []
    for gidx in range(tv.shape[1] // LANES):
        tg = tv[:, LANES * gidx:LANES * (gidx + 1)]
        sw = jnp.where(first, pltpu.roll(tg, LANES - ROPE_DIM // 2, 1), pltpu.roll(tg, ROPE_DIM // 2, 1))
        outs.append(tg * cos + sw * sin)
    return jnp.concatenate(outs, axis=1)


def _proj_in(h16, w_in, posf, invf, sgn, *, tm=512):
    t, k = h16.shape
    cuts = [0, D_ATTN, 2 * D_ATTN, 3 * D_ATTN, 3 * D_ATTN + D_SSD, 3 * D_ATTN + D_SSD + D_CONV, w_in.shape[1]]

    def body(h_ref, w_ref, pos_ref, invf_ref, sgn_ref, q_ref, k_ref, v_ref, z_ref, xbc_ref, dt_ref, cs_ref):
        hv = h_ref[...]
        part = lambda a: jnp.dot(hv, w_ref[:, cuts[a]:cuts[a + 1]], preferred_element_type=F32)
        cos, sin = _rope_tables(pos_ref[...], invf_ref[...], sgn_ref[...])
        cs_ref[...] = jnp.concatenate([cos, sin], axis=1)
        q_ref[...] = _bf(_rope_apply(part(0), cos, sin) * (HEAD_DIM ** -0.5))
        k_ref[...] = _bf(_rope_apply(part(1), cos, sin))
        v_ref[...] = _bf(part(2))
        z_ref[...] = part(3)
        xbc_ref[...] = part(4)
        dt_ref[...] = part(5)

    row = lambda c: pl.BlockSpec((tm, c), lambda i: (i, 0))
    widths = [D_ATTN, D_ATTN, D_ATTN, D_SSD, D_CONV, LANES, 2 * LANES]
    dtypes = [BF16, BF16, BF16, F32, F32, F32, F32]
    return pl.pallas_call(
        body, name="proj_in", grid=(t // tm,),
        in_specs=[row(k), _resident(w_in.shape), row(1), _resident(invf.shape), _resident(sgn.shape)],
        out_specs=[row(c) for c in widths],
        out_shape=[jax.ShapeDtypeStruct((t, c), dt) for c, dt in zip(widths, dtypes)],
        compiler_params=_params("parallel"),
    )(h16, w_in, posf, invf, sgn)


def _rope_bwd(dq, dk, cs):
    def fn(dq_v, dk_v, cs_v):
        cos, sin = cs_v[:, :LANES], -cs_v[:, LANES:]
        gq = _rope_apply(dq_v * (HEAD_DIM ** -0.5), cos, sin)
        gk = _rope_apply(dk_v, cos, sin)
        return [jnp.concatenate([gq, gk], axis=1)], []
    return _rowwise("rope_bwd", fn, [dq, dk, cs], [], [(2 * D_ATTN, BF16)])[0]


def _rms(v, w):
    return v * lax.rsqrt(jnp.mean(v * v, axis=-1, keepdims=True) + RMS_EPS) * w


def _ungroup(yg):
    w = HEADS_PER_GROUP * HEAD_DIM
    return jnp.concatenate([yg[:, GROUP_LANES * g:GROUP_LANES * g + w] for g in range(N_GROUPS)], axis=1)


def _group(xs):
    w = HEADS_PER_GROUP * HEAD_DIM
    parts = []
    for g in range(N_GROUPS):
        parts += [xs[:, w * g:w * (g + 1)], jnp.zeros((xs.shape[0], GROUP_LANES - w), xs.dtype)]
    return jnp.concatenate(parts, axis=1)


def _norms_fn(attn, yg, xs, z, w_attn, w_ssd, dskip):
    a_n = _rms(attn, w_attn)
    y = _ungroup(yg) + dskip * xs
    y_n = _rms(y * (z * jax.nn.sigmoid(z)), w_ssd)
    return jnp.concatenate([a_n, y_n], axis=1)


def _norms_fwd(attn, yg, xbc, z, w_attn, w_ssd, dskip):
    def fn(*v):
        return [_norms_fn(*v)], []
    return _rowwise("norms_fwd", fn, [attn, yg, (xbc, D_SSD), z], [w_attn, w_ssd, dskip], [(D_ATTN + D_SSD, BF16)])[0]


def _norms_bwd(attn, yg, xbc, z, w_attn, w_ssd, dskip, dcat):
    def fn(attn_v, yg_v, xs_v, z_v, dcat_v, wa_v, ws_v, dk_v):
        _, vjp = jax.vjp(_norms_fn, attn_v, yg_v, xs_v, z_v, wa_v, ws_v, dk_v)
        d_attn, d_yg, d_xs, d_z, d_wa, d_ws, d_dk = vjp(dcat_v)
        return [d_attn, d_yg, d_xs, d_z], [d_wa, d_ws, d_dk]
    return _rowwise("norms_bwd", fn, [attn, yg, (xbc, D_SSD), z, dcat], [w_attn, w_ssd, dskip],
                    [(D_ATTN, F32), (N_GROUPS * GROUP_LANES, F32), (D_SSD, F32), (D_SSD, BF16)], [(1, D_SSD)] * 3)


def _ssd_prep_fn(xs, dtp, dtb, alog, e_x, e_a):
    dt = jax.nn.softplus(dtp + dtb)
    a = -jnp.exp(alog)
    dtg = jnp.dot(dt, e_x, precision=HIGHEST, preferred_element_type=F32)
    xdtg = _group(xs) * dtg
    dag = jnp.dot(dt * a, e_a, precision=HIGHEST, preferred_element_type=F32)
    return xdtg, dag


def _ssd_prep_fwd(xbc, dtp, dtb, alog, e_x, e_a):
    def fn(xbc_v, dtp_v, dtb_v, alog_v, ex_v, ea_v):
        xdtg, dag = _ssd_prep_fn(xbc_v[:, :D_SSD], dtp_v, dtb_v, alog_v, ex_v, ea_v)
        return [xdtg, xbc_v[:, D_SSD:], dag], []
    return _rowwise("ssd_prep_fwd", fn, [xbc, dtp], [dtb, alog, e_x, e_a],
                    [(N_GROUPS * GROUP_LANES, BF16), (D_CONV - D_SSD, BF16), (N_GROUPS * LANES, F32)])


def _ssd_prep_bwd(xbc, dtp, dtb, alog, e_x, e_a, dxdtg, ddag, dxs_a, db, dc):
    def fn(xs_v, dtp_v, dxdtg_v, ddag_v, dxs_a_v, db_v, dc_v, dtb_v, alog_v, ex_v, ea_v):
        _, vjp = jax.vjp(lambda a, b, c, d: _ssd_prep_fn(a, b, c, d, ex_v, ea_v), xs_v, dtp_v, dtb_v, alog_v)
        dxs, ddtp, ddtb, dalog = vjp((dxdtg_v, ddag_v))
        return [jnp.concatenate([dxs + dxs_a_v, db_v, dc_v], axis=1), ddtp], [ddtb, dalog]
    return _rowwise("ssd_prep_bwd", fn, [(xbc, D_SSD), dtp, dxdtg, ddag, dxs_a, db, dc], [dtb, alog, e_x, e_a],
                    [(D_CONV, F32), (LANES, BF16)], [(1, LANES)] * 2)


def _shift_down(u, d):
    if d == 0:
        return u
    row = lax.broadcasted_iota(jnp.int32, u.shape, 0)
    return jnp.where(row >= d, pltpu.roll(u, d, 0), 0.0)


def _shift_up(u, d):
    if d == 0:
        return u
    s = u.shape[0]
    row = lax.broadcasted_iota(jnp.int32, u.shape, 0)
    return jnp.where(row < s - d, pltpu.roll(u, s - d, 0), 0.0)


def _conv_pre(u, w, b):
    acc = b
    for k in range(CONV_WIDTH):
        acc = acc + w[k:k + 1, :] * _shift_down(u, CONV_WIDTH - 1 - k)
    return acc


def _conv_fwd(u, w, b, *, tc=256):
    nb, s, c = u.shape

    def body(u_ref, w_ref, b_ref, o_ref):
        pre = _conv_pre(u_ref[0], w_ref[...], b_ref[...])
        o_ref[0] = pre * jax.nn.sigmoid(pre)

    return pl.pallas_call(
        body, name="conv_fwd", grid=(c // tc, nb),
        in_specs=[pl.BlockSpec((1, s, tc), lambda j, i: (i, 0, j)), pl.BlockSpec((CONV_WIDTH, tc), lambda j, i: (0, j)),
                  pl.BlockSpec((1, tc), lambda j, i: (0, j))],
        out_specs=pl.BlockSpec((1, s, tc), lambda j, i: (i, 0, j)),
        out_shape=jax.ShapeDtypeStruct((nb, s, c), F32),
        compiler_params=_params("parallel", "parallel"),
    )(u, w, b)


def _conv_bwd(u, w, b, dout, *, tc=256):
    nb, s, c = u.shape

    def body(u_ref, w_ref, b_ref, d_ref, du_ref, dw_ref, db_ref):
        uv, wv = u_ref[0], w_ref[...]
        pre = _conv_pre(uv, wv, b_ref[...])
        sig = jax.nn.sigmoid(pre)
        dpre = d_ref[0] * (sig * (1.0 + pre * (1.0 - sig)))
        du = jnp.zeros_like(uv)
        dws = []
        for k in range(CONV_WIDTH):
            du = du + wv[k:k + 1, :] * _shift_up(dpre, CONV_WIDTH - 1 - k)
            dws.append(jnp.sum(dpre * _shift_down(uv, CONV_WIDTH - 1 - k), axis=0, keepdims=True))
        du_ref[0] = _bf(du)
        dwv = jnp.concatenate(dws + [jnp.zeros((8 - CONV_WIDTH, tc), F32)], axis=0)
        dbv = jnp.sum(dpre, axis=0, keepdims=True)
        first = pl.program_id(1) == 0

        @pl.when(first)
        def _():
            dw_ref[...] = dwv
            db_ref[...] = dbv

        @pl.when(jnp.logical_not(first))
        def _():
            dw_ref[...] += dwv
            db_ref[...] += dbv

    blk = pl.BlockSpec((1, s, tc), lambda j, i: (i, 0, j))
    return pl.pallas_call(
        body, name="conv_bwd", grid=(c // tc, nb),
        in_specs=[blk, pl.BlockSpec((CONV_WIDTH, tc), lambda j, i: (0, j)), pl.BlockSpec((1, tc), lambda j, i: (0, j)), blk],
        out_specs=[blk, pl.BlockSpec((8, tc), lambda j, i: (0, j)), pl.BlockSpec((1, tc), lambda j, i: (0, j))],
        out_shape=[jax.ShapeDtypeStruct((nb, s, c), BF16), jax.ShapeDtypeStruct((8, c), F32), jax.ShapeDtypeStruct((1, c), F32)],
        compiler_params=_params("parallel", "arbitrary"),
    )(u, w, b, dout)


FWD_KEY_BLOCK = 256


def _branch_bias_table(seq, kb):
    ratio = SEQ_BLOCK // kb
    key = np.arange(kb)[None, :, None]
    query = np.arange(SEQ_BLOCK)[None, None, :]
    delta = (np.arange(seq // kb)[:, None, None] - (ratio - 1)) * kb + query - key
    cnt = np.zeros(delta.shape, np.float64)
    for window, dilation in ((128, 1), (512, 4), (2048, 16)):
        cnt += (delta >= 0) & (delta % dilation == 0) & (delta <= window)
    return jnp.asarray(np.where(cnt > 0, np.log(np.maximum(cnt, 1.0)), NEG).astype(np.float32))


HEADS_PER_BLOCK = LANES // HEAD_DIM


def _head_rows(v, h):
    row = lax.broadcasted_iota(jnp.int32, v.shape, 0)
    return jnp.where((row >= HEAD_DIM * h) & (row < HEAD_DIM * (h + 1)), v, jnp.zeros_like(v))


def _attn_fwd(q, k, v, bias):
    nb_, s, _ = q.shape
    ab, kb = SEQ_BLOCK, FWD_KEY_BLOCK
    nblk, nkb, ratio = s // ab, s // kb, ab // kb

    def body(q_ref, k_ref, v_ref, b_ref, o_ref, lse_ref, vt_s):
        i = pl.program_id(2)

        @pl.when(i == 0)
        def _():
            for jb in range(nkb):
                vt_s[jb] = v_ref[0, kb * jb:kb * (jb + 1), :].T

        qt = q_ref[0].T
        qts = [_head_rows(qt, h) for h in range(HEADS_PER_BLOCK)]

        def step(j, carry):
            ks = pl.ds(pl.multiple_of(j * kb, kb), kb)
            kj = k_ref[0, ks, :]
            lb = b_ref[ratio * i - j + (ratio - 1)]
            out = []
            for h in range(HEADS_PER_BLOCK):
                m, l, acc = carry[3 * h:3 * h + 3]
                st = jnp.dot(kj, qts[h], preferred_element_type=F32) + lb
                m_new = jnp.maximum(m, jnp.max(st, axis=0, keepdims=True))
                p = jnp.exp(st - m_new)
                a = jnp.exp(m - m_new)
                l = a * l + jnp.sum(p, axis=0, keepdims=True)
                vt = vt_s[j, HEAD_DIM * h:HEAD_DIM * (h + 1), :]
                acc = a * acc + jnp.dot(vt, _bf(p), preferred_element_type=F32)
                out += [m_new, l, acc]
            return tuple(out)

        init = (jnp.full((1, ab), NEG, F32), jnp.zeros((1, ab), F32), jnp.zeros((HEAD_DIM, ab), F32)) * HEADS_PER_BLOCK
        res = lax.fori_loop(0, ratio * (i + 1), step, init)
        ot = jnp.concatenate([res[3 * h + 2] / res[3 * h + 1] for h in range(HEADS_PER_BLOCK)], axis=0)
        o_ref[0] = ot.T
        rows = [res[3 * h] + jnp.log(res[3 * h + 1]) for h in range(HEADS_PER_BLOCK)]
        lse_ref[0, 0, 0] = jnp.concatenate(rows + [jnp.zeros((8 - HEADS_PER_BLOCK, ab), F32)], axis=0)

    qblk = pl.BlockSpec((1, ab, LANES), lambda b, hp, i: (b, i, hp))
    full = pl.BlockSpec((1, s, LANES), lambda b, hp, i: (b, 0, hp))
    return pl.pallas_call(
        body, name="attn_fwd", grid=(nb_, D_ATTN // LANES, nblk),
        in_specs=[qblk, full, full, pl.BlockSpec((nkb, kb, ab), lambda b, hp, i: (0, 0, 0))],
        out_specs=[qblk, pl.BlockSpec((1, 1, 1, 8, ab), lambda b, hp, i: (b, hp, i, 0, 0))],
        out_shape=[jax.ShapeDtypeStruct((nb_, s, D_ATTN), F32),
                   jax.ShapeDtypeStruct((nb_, D_ATTN // LANES, nblk, 8, ab), F32)],
        scratch_shapes=[pltpu.VMEM((nkb, LANES, kb), BF16)],
        compiler_params=_params("parallel", "parallel", "arbitrary"),
    )(q, k, v, bias)


def _attn_bwd(q, k, v, o, do, lse, bias):
    nb_, s, _ = q.shape
    ab = SEQ_BLOCK
    nblk = s // ab

    nh = HEADS_PER_BLOCK

    def body(q_ref, k_ref, v_ref, o_ref, do_ref, lse_ref, b_ref, dq_ref, dk_ref, dv_ref,
             qt_s, dot_s, kt_s, dqt_s, do16_s, d_s, dk_acc, dv_acc):
        for jb in range(nblk):
            sl = slice(ab * jb, ab * (jb + 1))
            qt, kt = q_ref[0, sl, :].T, k_ref[0, sl, :].T
            do = do_ref[0, sl, :]
            dot = do.T
            prod = dot * o_ref[0, sl, :].T
            do16_s[sl, :] = _bf(do)
            for h in range(nh):
                qt_s[nh * jb + h] = _head_rows(qt, h)
                kt_s[nh * jb + h] = _head_rows(kt, h)
                dot_s[nh * jb + h] = _head_rows(_bf(dot), h)
            d_s[jb] = jnp.concatenate(
                [jnp.sum(prod[HEAD_DIM * h:HEAD_DIM * (h + 1)], axis=0, keepdims=True) for h in range(nh)]
                + [jnp.zeros((8 - nh, ab), F32)], axis=0)
            dqt_s[jb] = jnp.zeros((LANES, ab), F32)

        def outer(j, carry):
            ks = pl.ds(pl.multiple_of(j * ab, ab), ab)
            kj, vj = k_ref[0, ks, :], v_ref[0, ks, :]
            dk_acc[...] = jnp.zeros_like(dk_acc)
            dv_acc[...] = jnp.zeros_like(dv_acc)

            def inner(i, c2):
                qs = pl.ds(pl.multiple_of(i * ab, ab), ab)
                qi, doi = q_ref[0, qs, :], do16_s[qs, :]
                lb = b_ref[i - j]
                for h in range(nh):
                    st = jnp.dot(kj, qt_s[nh * i + h], preferred_element_type=F32) + lb
                    pt = jnp.exp(st - lse_ref[0, 0, i, h:h + 1, :])
                    dpt = jnp.dot(vj, dot_s[nh * i + h], preferred_element_type=F32)
                    dst16 = _bf(pt * (dpt - d_s[i, h:h + 1, :]))
                    dv_acc[h] += jnp.dot(_bf(pt), doi, preferred_element_type=F32)
                    dk_acc[h] += jnp.dot(dst16, qi, preferred_element_type=F32)
                    dqt_s[i] += jnp.dot(kt_s[nh * j + h], dst16, preferred_element_type=F32)
                return c2

            lax.fori_loop(j, nblk, inner, 0)
            lane = lax.broadcasted_iota(jnp.int32, (ab, LANES), 1)
            dk_ref[0, ks, :] = jnp.where(lane < HEAD_DIM, dk_acc[0], dk_acc[1])
            dv_ref[0, ks, :] = _bf(jnp.where(lane < HEAD_DIM, dv_acc[0], dv_acc[1]))
            return carry

        lax.fori_loop(0, nblk, outer, 0)
        for jb in range(nblk):
            dq_ref[0, ab * jb:ab * (jb + 1), :] = dqt_s[jb].T

    assert nh == 2
    full = pl.BlockSpec((1, s, LANES), lambda b, hp: (b, 0, hp))
    return pl.pallas_call(
        body, name="attn_bwd", grid=(nb_, D_ATTN // LANES),
        in_specs=[full] * 5 + [pl.BlockSpec((1, 1, nblk, 8, ab), lambda b, hp: (b, hp, 0, 0, 0)),
                               pl.BlockSpec((nblk, ab, ab), lambda b, hp: (0, 0, 0))],
        out_specs=[full, full, full],
        out_shape=[jax.ShapeDtypeStruct((nb_, s, D_ATTN), F32), jax.ShapeDtypeStruct((nb_, s, D_ATTN), F32),
                   jax.ShapeDtypeStruct((nb_, s, D_ATTN), BF16)],
        scratch_shapes=[pltpu.VMEM((nh * nblk, LANES, ab), BF16), pltpu.VMEM((nh * nblk, LANES, ab), BF16),
                        pltpu.VMEM((nh * nblk, LANES, ab), BF16), pltpu.VMEM((nblk, LANES, ab), F32),
                        pltpu.VMEM((s, LANES), BF16), pltpu.VMEM((nblk, 8, ab), F32),
                        pltpu.VMEM((nh, ab, LANES), F32), pltpu.VMEM((nh, ab, LANES), F32)],
        compiler_params=_params("parallel", "parallel"),
    )(q, k, v, o, do, lse, bias)


def _cumsum_fwd(dag):
    nb_, s, c = dag.shape
    ab = SEQ_BLOCK

    def body(a_ref, o_ref, ot_ref):
        r = lax.broadcasted_iota(jnp.int32, (ab, ab), 0)
        cc = lax.broadcasted_iota(jnp.int32, (ab, ab), 1)
        tri = (r >= cc).astype(F32)
        carry = jnp.zeros((1, c), F32)
        for i in range(s // ab):
            loc = jnp.dot(tri, a_ref[0, ab * i:ab * (i + 1), :], precision=HIGHEST, preferred_element_type=F32) + carry
            o_ref[0, ab * i:ab * (i + 1), :] = loc
            ot_ref[0, :, ab * i:ab * (i + 1)] = loc.T
            carry = loc[ab - 1:ab, :]

    return pl.pallas_call(
        body, name="ssd_cumsum", grid=(nb_,),
        in_specs=[pl.BlockSpec((1, s, c), lambda b: (b, 0, 0))],
        out_specs=[pl.BlockSpec((1, s, c), lambda b: (b, 0, 0)), pl.BlockSpec((1, c, s), lambda b: (b, 0, 0))],
        out_shape=[jax.ShapeDtypeStruct((nb_, s, c), F32), jax.ShapeDtypeStruct((nb_, c, s), F32)],
        compiler_params=_params("parallel"),
    )(dag)


def _cumsum_bwd(dcol, drow):
    nb_, s, c = dcol.shape
    ab = SEQ_BLOCK

    def body(c_ref, r_ref, o_ref):
        r = lax.broadcasted_iota(jnp.int32, (ab, ab), 0)
        cc = lax.broadcasted_iota(jnp.int32, (ab, ab), 1)
        tri = (r <= cc).astype(F32)
        carry = jnp.zeros((1, c), F32)
        for i in reversed(range(s // ab)):
            rows = r_ref[0, :, ab * i:ab * (i + 1)].T
            parts = []
            for g in range(N_GROUPS):
                parts += [rows[:, 8 * g:8 * (g + 1)], jnp.zeros((ab, LANES - 8), F32)]
            blk = c_ref[0, ab * i:ab * (i + 1), :] + jnp.concatenate(parts, axis=1)
            loc = jnp.dot(tri, blk, precision=HIGHEST, preferred_element_type=F32) + carry
            o_ref[0, ab * i:ab * (i + 1), :] = loc
            carry = loc[0:1, :]

    return pl.pallas_call(
        body, name="ssd_cumsum_bwd", grid=(nb_,),
        in_specs=[pl.BlockSpec((1, s, c), lambda b: (b, 0, 0)), pl.BlockSpec((1, N_GROUPS * 8, s), lambda b: (b, 0, 0))],
        out_specs=pl.BlockSpec((1, s, c), lambda b: (b, 0, 0)),
        out_shape=jax.ShapeDtypeStruct((nb_, s, c), F32),
        compiler_params=_params("parallel"),
    )(dcol, drow)


def _causal_ok(i, j):
    ab = SEQ_BLOCK
    r = lax.broadcasted_iota(jnp.int32, (ab, ab), 0)
    c = lax.broadcasted_iota(jnp.int32, (ab, ab), 1)
    return (r + (i - j) * ab) >= c


def _causal_ok_t(i, j):
    ab = SEQ_BLOCK
    r = lax.broadcasted_iota(jnp.int32, (ab, ab), 0)
    c = lax.broadcasted_iota(jnp.int32, (ab, ab), 1)
    return (c + (i - j) * ab) >= r


def _ssd_chunk(s_in, x, bm_t, cm, cb, acol, arow, a_prev, ok):
    q = x.shape[0]
    decay = jnp.exp(jnp.where(ok, acol - arow, NEG))
    y = jnp.dot(_bf(cb * decay), x, preferred_element_type=F32)
    y = y + jnp.exp(acol - a_prev) * jnp.dot(cm, _bf(s_in), preferred_element_type=F32)
    a_end = acol[q - 1:q, :]
    wx = _bf(jnp.exp(a_end - acol) * x.astype(F32))
    s_out = jnp.exp(a_end - a_prev) * s_in + jnp.dot(bm_t, wx, preferred_element_type=F32)
    return y, s_out


def _ssd_specs(s):
    xblk = pl.BlockSpec((1, s, GROUP_LANES), lambda b, g: (b, 0, g))
    bblk = pl.BlockSpec((1, s, D_STATE), lambda b, g: (b, 0, g))
    cblk = pl.BlockSpec((1, s, D_STATE), lambda b, g: (b, 0, N_GROUPS + g))
    tblk = pl.BlockSpec((1, 8, s), lambda b, g: (b, (LANES // 8) * g, 0))
    return xblk, bblk, cblk, tblk


def _chunk_views(i, j, x_ref, ac_ref, at_ref):
    ab = SEQ_BLOCK
    sl = slice(ab * i, ab * (i + 1))
    hs = slice(HEAD_DIM * j, HEAD_DIM * (j + 1))
    a_prev = jnp.zeros((1, 1), F32) if i == 0 else ac_ref[0, ab * i - 1:ab * i, j:j + 1]
    return sl, hs, ac_ref[0, sl, j:j + 1], at_ref[0, j:j + 1, sl], a_prev


def _ssd_fwd_chunked(xdtg, bc, acum, acum_t):
    nb_, s, _ = xdtg.shape
    ab = SEQ_BLOCK
    hpg = HEADS_PER_GROUP

    def body(x_ref, b_ref, c_ref, ac_ref, at_ref, y_ref):
        ok = _causal_ok(0, 0)
        states = [jnp.zeros((D_STATE, HEAD_DIM), F32) for _ in range(hpg)]
        for i in range(s // ab):
            bm, cm = b_ref[0, ab * i:ab * (i + 1), :], c_ref[0, ab * i:ab * (i + 1), :]
            bm_t = bm.T
            cb = jnp.dot(cm, bm_t, preferred_element_type=F32)
            ys = []
            for j in range(hpg):
                sl, hs, acol, arow, a_prev = _chunk_views(i, j, x_ref, ac_ref, at_ref)
                y, states[j] = _ssd_chunk(states[j], x_ref[0, sl, hs], bm_t, cm, cb, acol, arow, a_prev, ok)
                ys.append(y)
            y_ref[0, sl, :] = jnp.concatenate(ys + [jnp.zeros((ab, GROUP_LANES - hpg * HEAD_DIM), F32)], axis=1)

    xblk, bblk, cblk, tblk = _ssd_specs(s)
    ablk = pl.BlockSpec((1, s, LANES), lambda b, g: (b, 0, g))
    return pl.pallas_call(
        body, name="ssd_fwd", grid=(nb_, N_GROUPS), in_specs=[xblk, bblk, cblk, ablk, tblk], out_specs=xblk,
        out_shape=jax.ShapeDtypeStruct((nb_, s, N_GROUPS * GROUP_LANES), F32),
        compiler_params=_params("parallel", "parallel"),
    )(xdtg, bc, bc, acum, acum_t)


def _ssd_bwd_chunked(xdtg, bc, acum, acum_t, dyg):
    nb_, s, _ = xdtg.shape
    ab = SEQ_BLOCK
    nblk = s // ab
    hpg = HEADS_PER_GROUP

    def body(x_ref, b_ref, c_ref, ac_ref, at_ref, dy_ref, dx_ref, db_ref, dc_ref, dac_ref, dar_ref, s_s):
        ok = _causal_ok(0, 0)
        dx_ref[...] = jnp.zeros_like(dx_ref)
        dac_ref[...] = jnp.zeros_like(dac_ref)
        dar_ref[...] = jnp.zeros_like(dar_ref)
        states = [jnp.zeros((D_STATE, HEAD_DIM), F32) for _ in range(hpg)]
        for i in range(nblk):
            bm_t = b_ref[0, ab * i:ab * (i + 1), :].T
            for j in range(hpg):
                sl, hs, acol, arow, a_prev = _chunk_views(i, j, x_ref, ac_ref, at_ref)
                s_s[hpg * i + j] = states[j]
                if i + 1 < nblk:
                    a_end = acol[ab - 1:ab, :]
                    wx = _bf(jnp.exp(a_end - acol) * x_ref[0, sl, hs].astype(F32))
                    states[j] = jnp.exp(a_end - a_prev) * states[j] + jnp.dot(bm_t, wx, preferred_element_type=F32)
        ok_t = _causal_ok_t(0, 0)
        last_row = lax.broadcasted_iota(jnp.int32, (ab, 1), 0) == ab - 1
        d_state = [jnp.zeros((D_STATE, HEAD_DIM), F32) for _ in range(hpg)]
        pending = [jnp.zeros((1, 1), F32) for _ in range(hpg)]
        total = lambda v: jnp.sum(v, keepdims=True)
        for i in reversed(range(nblk)):
            bm, cm = b_ref[0, ab * i:ab * (i + 1), :], c_ref[0, ab * i:ab * (i + 1), :]
            cm_t = cm.T
            cbt = jnp.dot(bm, cm_t, preferred_element_type=F32)
            dcbt = jnp.zeros((ab, ab), F32)
            d_bm, d_cm = jnp.zeros((ab, D_STATE), F32), jnp.zeros((ab, D_STATE), F32)
            for j in range(hpg):
                sl, hs, acol, arow, a_prev = _chunk_views(i, j, x_ref, ac_ref, at_ref)
                x, dy = x_ref[0, sl, hs], dy_ref[0, sl, hs]
                dy16 = _bf(dy)
                s_in, g_out = s_s[hpg * i + j], d_state[j]
                s16, g16 = _bf(s_in), _bf(g_out)
                decay = jnp.exp(jnp.where(ok_t, arow - acol, NEG))
                gt = cbt * decay
                dgt = lax.dot_general(x, dy16, _NT, preferred_element_type=F32)
                d_x = jnp.dot(_bf(gt), dy16, preferred_element_type=F32)
                dcbt = dcbt + dgt * decay
                mm = dgt * gt
                d_arow = jnp.sum(mm, axis=0, keepdims=True)
                d_acol = -jnp.sum(mm, axis=1, keepdims=True)
                e = jnp.exp(acol - a_prev)
                edy16 = _bf(e * dy)
                d_cm = d_cm + lax.dot_general(edy16, s16, _NT, preferred_element_type=F32)
                d_s = jnp.dot(cm_t, edy16, preferred_element_type=F32)
                de_e = jnp.sum(dy * jnp.dot(cm, s16, preferred_element_type=F32), axis=1, keepdims=True) * e
                a_end = acol[ab - 1:ab, :]
                w = jnp.exp(a_end - acol)
                f = jnp.exp(a_end - a_prev)
                x32 = x.astype(F32)
                bg = jnp.dot(bm, g16, preferred_element_type=F32)
                d_x = d_x + w * bg
                d_bm = d_bm + lax.dot_general(_bf(w * x32), g16, _NT, preferred_element_type=F32)
                dw_w = jnp.sum(bg * x32, axis=1, keepdims=True) * w
                df_f = total(g_out * s_in) * f
                d_end = total(dw_w) + df_f
                d_acol = d_acol + de_e - dw_w + jnp.where(last_row, d_end + pending[j], 0.0)
                pending[j] = -total(de_e) - df_f
                d_state[j] = d_s + f * g_out
                dx_ref[0, sl, hs] = d_x
                dac_ref[0, sl, j:j + 1] = d_acol
                dar_ref[0, j:j + 1, sl] = d_arow
            dcbt16 = _bf(dcbt)
            db_ref[0, ab * i:ab * (i + 1), :] = d_bm + jnp.dot(dcbt16, cm, preferred_element_type=F32)
            dc_ref[0, ab * i:ab * (i + 1), :] = d_cm + lax.dot_general(dcbt16, bm, _TN, preferred_element_type=F32)

    xblk, bblk, cblk, tblk = _ssd_specs(s)
    ablk = pl.BlockSpec((1, s, LANES), lambda b, g: (b, 0, g))
    return pl.pallas_call(
        body, name="ssd_bwd", grid=(nb_, N_GROUPS),
        in_specs=[xblk, bblk, cblk, ablk, tblk, xblk],
        out_specs=[xblk, bblk, bblk, ablk, pl.BlockSpec((1, 8, s), lambda b, g: (b, g, 0))],
        out_shape=[jax.ShapeDtypeStruct((nb_, s, N_GROUPS * GROUP_LANES), F32),
                   jax.ShapeDtypeStruct((nb_, s, N_GROUPS * D_STATE), F32),
                   jax.ShapeDtypeStruct((nb_, s, N_GROUPS * D_STATE), F32),
                   jax.ShapeDtypeStruct((nb_, s, N_GROUPS * LANES), F32),
                   jax.ShapeDtypeStruct((nb_, N_GROUPS * 8, s), F32)],
        scratch_shapes=[pltpu.VMEM((nblk * hpg, D_STATE, HEAD_DIM), F32)],
        compiler_params=_params("parallel", "parallel"),
    )(xdtg, bc, bc, acum, acum_t, dyg)


def _interleave(wg, wu):
    k, f = wg.shape
    gi = GATE_UP_INTERLEAVE
    return jnp.stack([wg.reshape(k, f // gi, gi), wu.reshape(k, f // gi, gi)], axis=2).reshape(k, 2 * f)


def _head_expanders():
    e_x = np.zeros((LANES, N_GROUPS * GROUP_LANES), np.float32)
    e_a = np.zeros((LANES, N_GROUPS * LANES), np.float32)
    for h in range(N_HEADS):
        g, j = divmod(h, HEADS_PER_GROUP)
        e_x[h, GROUP_LANES * g + HEAD_DIM * j:GROUP_LANES * g + HEAD_DIM * (j + 1)] = 1.0
        e_a[h, LANES * g + j] = 1.0
    return jnp.asarray(e_x), jnp.asarray(e_a)


def _pad_lanes(v, n=LANES):
    return jnp.pad(v, ((0, 0), (0, n - v.shape[1])))


def _local_step(x, positions, target, w, late_job=None, late_weights=None, early_grad_job=None):
    nb, s, d = x.shape
    t = nb * s
    x2 = x.reshape(t, d)
    tgt2 = target.reshape(t, d)

    x16 = _bf(x2)
    wgu1 = _interleave(w["ffn1_gate"], w["ffn1_up"])
    ffn1 = _ffn_fwd("ffn1_fwd", x16, x2, wgu1, w["ffn1_down"], w["ln1_g"], w["ln1_b"], carry=late_job)
    au1, hm1, h1, r1, h1_16 = ffn1[:5]
    if late_job is not None:
        w = {**w, **late_weights(ffn1[5])}

    wgu2 = _interleave(w["ffn2_gate"], w["ffn2_up"])
    w_in = w["w_in"]
    wqk, wv, wz = w_in[:, :2 * D_ATTN], w_in[:, 2 * D_ATTN:3 * D_ATTN], w_in[:, 3 * D_ATTN:3 * D_ATTN + D_SSD]
    wxbc = w_in[:, 3 * D_ATTN + D_SSD:3 * D_ATTN + D_SSD + D_CONV]
    wdt = _pad_lanes(w_in[:, 3 * D_ATTN + D_SSD + D_CONV:])

    inv_freq = ROPE_THETA ** (-jnp.arange(0, ROPE_DIM, 2, dtype=F32) / ROPE_DIM)
    half = ROPE_DIM // 2
    head_invf = jnp.concatenate([inv_freq, inv_freq, jnp.zeros((HEAD_DIM - ROPE_DIM,), F32)])
    head_sgn = jnp.concatenate([-jnp.ones((half,), F32), jnp.ones((half,), F32), jnp.zeros((HEAD_DIM - ROPE_DIM,), F32)])
    invf = jnp.tile(head_invf, LANES // HEAD_DIM)[None, :]
    sgn = jnp.tile(head_sgn, LANES // HEAD_DIM)[None, :]
    posf = positions.astype(F32).reshape(t, 1)
    bias_fwd, bias_bwd = _branch_bias_table(s, FWD_KEY_BLOCK), _branch_bias_table(s, SEQ_BLOCK)
    e_x, e_a = _head_expanders()
    dtb, alog = _pad_lanes(w["dt_bias"]), _pad_lanes(w["a_log"])
    dskip = jnp.repeat(w["d_skip"], HEAD_DIM, axis=1)

    q16, k16, v16, z, xbc_pre, dtp, cs = _proj_in(h1_16, _pad_lanes(w_in, w_in.shape[1] - N_HEADS + LANES), posf, invf, sgn)
    to3 =lambda a: a.reshape(nb, s, a.shape[-1])
    attn_o, lse = _attn_fwd(to3(q16), to3(k16), to3(v16), bias_fwd)

    xbc = _conv_fwd(to3(xbc_pre), w["conv_w"], w["conv_b"]).reshape(t, D_CONV)
    xdtg, bc16, dag = _ssd_prep_fwd(xbc, dtp, dtb, alog, e_x, e_a)
    acum, acum_t = _cumsum_fwd(to3(dag))
    yg = _ssd_fwd_chunked(to3(xdtg), to3(bc16), acum, acum_t)

    cat = _norms_fwd(attn_o.reshape(t, D_ATTN), yg.reshape(t, -1), xbc, z, w["attn_norm_w"], w["ssd_norm_w"], dskip)
    h2, r2, h2_16 = _mm_res_ln("w_out_ln2", cat, w["w_out"], h1, w["ln2_g"], w["ln2_b"], scale=1.0)

    au2, hm2, _, r3, _ = _ffn_fwd("ffn2_fwd", h2_16, h2, wgu2, w["ffn2_down"], w["ln3_g"], w["ln3_b"])

    g = {}
    dr3, dr3_16, g["ln3_g"], g["ln3_b"], loss = _ln_loss_bwd("loss_ln3_bwd", r3, w["ln3_g"], w["ln3_b"], tgt2)

    dau2, dh2 = _ffn_bwd("ffn2_bwd", dr3_16, dr3, w["ffn2_down"].T, au2, wgu2.T)
    g["ffn2_down"] = _mm_tn("ffn2_down_dw", hm2, dr3_16, scale=0.5, tk=D_FF // 2, tn=512)
    g["ffn2_gate"], g["ffn2_up"] = _mm_tn_gate_up("ffn2_up_dw", h2_16, dau2)

    dr2, dr2_16, g["ln2_g"], g["ln2_b"] = _ln_bwd("ln2_bwd", r2, w["ln2_g"], w["ln2_b"], dh2)
    dcat = _mm("w_out_dx", [(dr2_16, w["w_out"].T)], tn=768)
    g["w_out"] = _mm_tn("w_out_dw", cat, dr2_16, tk=768, tn=1024)

    d_attn, dyg, dxs_a, dz16, g["attn_norm_w"], g["ssd_norm_w"], ddskip = _norms_bwd(
        attn_o.reshape(t, D_ATTN), yg.reshape(t, -1), xbc, z, w["attn_norm_w"], w["ssd_norm_w"], dskip, dcat)
    g["d_skip"] = ddskip.reshape(N_HEADS, HEAD_DIM).sum(axis=1)[None, :]

    dq, dk, dv16 = _attn_bwd(to3(q16), to3(k16), to3(v16), attn_o, to3(d_attn), lse, bias_bwd)
    dqk16 = _rope_bwd(dq.reshape(t, D_ATTN), dk.reshape(t, D_ATTN), cs)

    dxdtg, dbm, dcm, dacol, darow = _ssd_bwd_chunked(to3(xdtg), to3(bc16), acum, acum_t, to3(dyg))
    ddag = _cumsum_bwd(dacol, darow)
    dxbc, ddtp16, ddtb, dalog = _ssd_prep_bwd(xbc, dtp, dtb, alog, e_x, e_a, dxdtg.reshape(t, -1), ddag.reshape(t, -1),
                                               dxs_a, dbm.reshape(t, -1), dcm.reshape(t, -1))
    g["dt_bias"], g["a_log"] = ddtb[:, :N_HEADS], dalog[:, :N_HEADS]
    dxbc_pre16, dconv_w, g["conv_b"] = _conv_bwd(to3(xbc_pre), w["conv_w"], w["conv_b"], to3(dxbc))
    g["conv_w"] = dconv_w[:CONV_WIDTH]
    dxbc_pre16 = dxbc_pre16.reshape(t, D_CONV)
    dv16 = dv16.reshape(t, D_ATTN)

    dh1 = _mm("w_in_dx", [(dqk16, wqk.T), (dv16, wv.T), (dz16, wz.T), (dxbc_pre16, wxbc.T), (ddtp16, wdt.T)],
              res=dr2, res_scale=ALPHA)
    g["w_in"] = _mm_tn_sections("w_in_dw", h1_16, [dqk16, dv16, dz16, dxbc_pre16, ddtp16])[:, :w_in.shape[1]]

    dr1, dr1_16, g["ln1_g"], g["ln1_b"] = _ln_bwd("ln1_bwd", r1, w["ln1_g"], w["ln1_b"], dh1)
    g["ffn1_down"] = _mm_tn("ffn1_down_dw", hm1, dr1_16, scale=0.5, tk=D_FF // 2, tn=512)
    ffn1b = _ffn_bwd("ffn1_bwd", dr1_16, dr1, w["ffn1_down"].T, au1, wgu1.T,
                     carry=None if early_grad_job is None else early_grad_job(g))
    dau1, dx = ffn1b[:2]
    early = ffn1b[2] if early_grad_job is not None else None
    g["ffn1_gate"], g["ffn1_up"] = _mm_tn_gate_up("ffn1_up_dw", x16, dau1)
    return loss, dx.reshape(nb, s, d), g, early


_HBM = pl.BlockSpec(memory_space=pltpu.HBM)
N_CHIPS = 4
N_DEVICES = 8


def _place():
    return lax.axis_index("x"), lax.axis_index("y"), lax.axis_index("c")


def _other_chips(x, y):
    return [(1 - x, y), (x, 1 - y), (1 - x, 1 - y)]


class _GatherJob:
    def __init__(self, shards):
        assert all((a.shape[0] // 2) % 16 == 0 for a in shards)
        self.n = len(shards)
        self.shapes = [a.shape for a in shards]
        self.operands = [a.reshape(2, a.shape[0] // 2, a.shape[1]) for a in shards]
        self.out_shape = [jax.ShapeDtypeStruct((N_CHIPS,) + a.shape, a.dtype) for a in self.operands]
        pair = pltpu.SemaphoreType.DMA((self.n, N_CHIPS - 1))
        self.scratch_shapes = [pair, pair, pair, pair]

    def results(self, outs):
        return [o.reshape((N_CHIPS,) + s) for o, s in zip(outs, self.shapes)]

    def phases(self, ins, outs, sems):
        n = self.n
        send_sems, recv_sems, fwd_send_sems, fwd_recv_sems = sems
        x, y, c = _place()
        me = 2 * x + y
        peers = _other_chips(x, y)

        def ici(t, p, src_chip):
            px, py = peers[p]
            return pltpu.make_async_remote_copy(
                ins[t].at[c] if src_chip is None else outs[t].at[src_chip, c],
                outs[t].at[me if src_chip is None else src_chip, c],
                send_sems.at[t, p], recv_sems.at[t, p], device_id=(px, py, c), device_id_type=MESH)

        def d2d(t, p, core):
            px, py = peers[p]
            return pltpu.make_async_remote_copy(
                outs[t].at[2 * px + py, core], outs[t].at[2 * px + py, core],
                fwd_send_sems.at[t, p], fwd_recv_sems.at[t, p], device_id=(x, y, 1 - c), device_id_type=MESH)

        pairs = [(t, p) for t in range(n) for p in range(N_CHIPS - 1)]

        def start():
            for t, p in pairs:
                ici(t, p, None).start()

        def forward():
            for t, p in pairs:
                px, py = peers[p]
                ici(t, p, 2 * px + py).wait_recv()
                d2d(t, p, c).start()

        def finish():
            for t, p in pairs:
                d2d(t, p, 1 - c).wait_recv()
            for t, p in pairs:
                ici(t, p, None).wait_send()
                d2d(t, p, c).wait_send()

        return start, forward, finish


class _ExchangeJob:
    def __init__(self, stacks):
        self.n = len(stacks)
        self.operands = list(stacks)
        self.out_shape = [jax.ShapeDtypeStruct(a.shape, a.dtype) for a in stacks]
        pair = pltpu.SemaphoreType.DMA((self.n, N_CHIPS - 1))
        self.scratch_shapes = [pair, pair]

    def results(self, outs):
        return list(outs)

    def phases(self, ins, outs, sems):
        send_sems, recv_sems = sems
        x, y, c = _place()
        me = 2 * x + y
        peers = _other_chips(x, y)
        pairs = [(t, p) for t in range(self.n) for p in range(N_CHIPS - 1)]

        def copy(t, p):
            px, py = peers[p]
            return pltpu.make_async_remote_copy(ins[t].at[2 * px + py], outs[t].at[me], send_sems.at[t, p],
                                                recv_sems.at[t, p], device_id=(px, py, c), device_id_type=MESH)

        def arrival(t, p):
            px, py = peers[p]
            return pltpu.make_async_remote_copy(ins[t].at[me], outs[t].at[2 * px + py], send_sems.at[t, p],
                                                recv_sems.at[t, p], device_id=(px, py, c), device_id_type=MESH)

        def start():
            for t, p in pairs:
                copy(t, p).start()

        def finish():
            for t, p in pairs:
                arrival(t, p).wait_recv()
            for t, p in pairs:
                copy(t, p).wait_send()

        return start, None, finish


def _run_job(job, name):
    n = job.n

    def body(*refs):
        for phase in job.phases(refs[:n], refs[n:2 * n], refs[2 * n:]):
            if phase is not None:
                phase()

    outs = pl.pallas_call(
        body, name=name, in_specs=[_HBM] * n, out_specs=[_HBM] * n,
        out_shape=job.out_shape, scratch_shapes=job.scratch_shapes,
    )(*job.operands)
    return job.results(outs)


def _sibling_halves(stacks, name):
    n = len(stacks)
    halves = [a.shape[1] // 2 for a in stacks]
    split = [a.reshape(a.shape[0], 2, h, a.shape[2]) for a, h in zip(stacks, halves)]

    def body(*refs):
        ins, outs = refs[:n], refs[n:2 * n]
        send_sems, recv_sems = refs[2 * n:]
        x, y, c = _place()
        cps = []
        for t in range(n):
            cp = pltpu.make_async_remote_copy(ins[t].at[:, 1 - c], outs[t], send_sems.at[t], recv_sems.at[t],
                                              device_id=(x, y, 1 - c), device_id_type=MESH)
            cp.start()
            cps.append(cp)
        for cp in cps:
            cp.wait()

    return pl.pallas_call(
        body, name=name,
        in_specs=[_HBM] * n, out_specs=[_HBM] * n,
        out_shape=[jax.ShapeDtypeStruct((a.shape[0], h, a.shape[2]), a.dtype) for a, h in zip(stacks, halves)],
        scratch_shapes=[pltpu.SemaphoreType.DMA((n,)), pltpu.SemaphoreType.DMA((n,))],
    )(*split)


def _sibling_swap(arrs):
    n = len(arrs)

    def body(*refs):
        ins, outs = refs[:n], refs[n:2 * n]
        send_sems, recv_sems = refs[2 * n:]
        x, y, c = _place()
        cps = []
        for t in range(n):
            cp = pltpu.make_async_remote_copy(ins[t], outs[t], send_sems.at[t], recv_sems.at[t],
                                              device_id=(x, y, 1 - c), device_id_type=MESH)
            cp.start()
            cps.append(cp)
        for cp in cps:
            cp.wait()

    return pl.pallas_call(
        body, name="sibling_swap",
        in_specs=[_HBM] * n, out_specs=[_HBM] * n,
        out_shape=[jax.ShapeDtypeStruct(a.shape, a.dtype) for a in arrs],
        scratch_shapes=[pltpu.SemaphoreType.DMA((n,)), pltpu.SemaphoreType.DMA((n,))],
    )(*arrs)


def _half_sum(name, own, other, core):
    k, r, cols = own.shape
    h = r // 2
    tr = next(cand for cand in (128, 176, 64, 32, 16) if h % cand == 0)
    nblk = h // tr

    def body(core_ref, own_ref, other_ref, o_ref):
        o_ref[...] = _bf(own_ref[...] + other_ref[...].astype(F32))

    grid_spec = pltpu.PrefetchScalarGridSpec(
        num_scalar_prefetch=1, grid=(nblk,),
        in_specs=[pl.BlockSpec((k, tr, cols), lambda i, core_ref: (0, i + core_ref[0] * nblk, 0)),
                  pl.BlockSpec((k, tr, cols), lambda i, core_ref: (0, i, 0))],
        out_specs=pl.BlockSpec((k, tr, cols), lambda i, core_ref: (0, i, 0)))
    return pl.pallas_call(
        body, name=name, grid_spec=grid_spec, out_shape=jax.ShapeDtypeStruct((k, h, cols), BF16),
        compiler_params=_params("parallel"),
    )(core.reshape(1).astype(jnp.int32), own, other)


def _small_allreduce(v):
    r = v.shape[0]

    def body(v_ref, tot_ref, slots, send_sems, recv_sems):
        x, y, c = _place()
        me = 4 * x + 2 * y + c
        slots[me] = v_ref[...]
        cps, peers = [], []
        for k in range(1, N_DEVICES):
            px = 1 - x if (k >> 2) & 1 else x
            py = 1 - y if (k >> 1) & 1 else y
            pc = 1 - c if k & 1 else c
            cp = pltpu.make_async_remote_copy(v_ref, slots.at[me], send_sems.at[k - 1], recv_sems.at[k - 1],
                                              device_id=(px, py, pc), device_id_type=MESH)
            cp.start()
            cps.append(cp)
            peers.append((px, py, pc))
        for k, (px, py, pc) in enumerate(peers):
            pltpu.make_async_remote_copy(v_ref, slots.at[4 * px + 2 * py + pc], send_sems.at[k], recv_sems.at[k],
                                         device_id=(px, py, pc), device_id_type=MESH).wait_recv()
        for cp in cps:
            cp.wait_send()
        acc = slots[0]
        for s in range(1, N_DEVICES):
            acc = acc + slots[s]
        tot_ref[...] = acc

    return pl.pallas_call(
        body, name="small_allreduce",
        in_specs=[pl.BlockSpec(memory_space=pltpu.VMEM)], out_specs=pl.BlockSpec(memory_space=pltpu.VMEM),
        out_shape=jax.ShapeDtypeStruct((r, LANES), F32),
        scratch_shapes=[pltpu.VMEM((N_DEVICES, r, LANES), F32), pltpu.SemaphoreType.DMA((N_DEVICES - 1,)),
                        pltpu.SemaphoreType.DMA((N_DEVICES - 1,))],
    )(v)


def _elementwise(name, fn, ins, out_dtypes):
    r, c = ins[0].shape[-2:]
    tr = next((cand for cand in (256, 176, 128, 64, 32, 16) if r % cand == 0), r)
    nin = len(ins)

    def body(*refs):
        outs = fn(*[ref[...] for ref in refs[:nin]])
        for o_ref, o in zip(refs[nin:], outs):
            o_ref[...] = o.astype(o_ref.dtype)

    in_specs = [pl.BlockSpec((tr, c), lambda i: (i, 0)) if a.ndim == 2 else pl.BlockSpec((a.shape[0], tr, c), lambda i: (0, i, 0))
                for a in ins]
    return pl.pallas_call(
        body, name=name, grid=(r // tr,), in_specs=in_specs,
        out_specs=[pl.BlockSpec((tr, c), lambda i: (i, 0)) for _ in out_dtypes],
        out_shape=[jax.ShapeDtypeStruct((r, c), dt) for dt in out_dtypes],
        compiler_params=_params("parallel"),
    )(*ins)


def _row_tile(rows):
    return next((cand for cand in (128, 176, 64, 32, 16) if rows % cand == 0), rows)


def _sum_slots(name, received, own, chip):
    _, r, cols = own.shape
    tr = _row_tile(r)

    def body(chip_ref, own_ref, a_ref, b_ref, c_ref, o_ref):
        o_ref[...] = ((own_ref[0].astype(F32) + a_ref[0].astype(F32)) + b_ref[0].astype(F32)) + c_ref[0].astype(F32)

    def slot(flip):
        return pl.BlockSpec((1, tr, cols), lambda i, chip_ref: (jnp.bitwise_xor(chip_ref[0], flip), i, 0))

    grid_spec = pltpu.PrefetchScalarGridSpec(
        num_scalar_prefetch=1, grid=(r // tr,), in_specs=[slot(0), slot(1), slot(2), slot(3)],
        out_specs=pl.BlockSpec((tr, cols), lambda i, chip_ref: (i, 0)))
    return pl.pallas_call(
        body, name=name, grid_spec=grid_spec, out_shape=jax.ShapeDtypeStruct((r, cols), F32),
        compiler_params=_params("parallel"),
    )(chip.reshape(1).astype(jnp.int32), own, received, received, received)


def _adamw_halves(name, mine, theirs, core, w, m, v):
    h, cols = mine.shape
    tr = _row_tile(h)
    nh = h // tr

    def body(core_ref, mine_ref, theirs_ref, w_ref, m_ref, v_ref, g_ref, d_ref, m2_ref, v2_ref):
        is_mine = (pl.program_id(0) // nh) == core_ref[0]
        g = jnp.where(is_mine, mine_ref[...], theirs_ref[...])
        outs = _adamw_math(g, w_ref[...], m_ref[...], v_ref[...])
        for ref, val in zip((g_ref, d_ref, m2_ref, v2_ref), outs):
            ref[...] = val

    half = pl.BlockSpec((tr, cols), lambda i, core_ref: (i % nh, 0))
    full = pl.BlockSpec((tr, cols), lambda i, core_ref: (i, 0))
    grid_spec = pltpu.PrefetchScalarGridSpec(
        num_scalar_prefetch=1, grid=(2 * nh,), in_specs=[half, half, full, full, full], out_specs=[full] * 4)
    return pl.pallas_call(
        body, name=name, grid_spec=grid_spec, out_shape=[jax.ShapeDtypeStruct((2 * h, cols), F32)] * 4,
        compiler_params=_params("parallel"),
    )(core.reshape(1).astype(jnp.int32), mine, theirs, w, m, v)


def _adamw_math(g, w_v, m_v, v_v):
    m2 = ADAM_B1 * m_v + (1.0 - ADAM_B1) * g
    v2 = ADAM_B2 * v_v + (1.0 - ADAM_B2) * jnp.square(g)
    m_hat = m2 / (1.0 - ADAM_B1 ** ADAM_STEP)
    v_hat = v2 / (1.0 - ADAM_B2 ** ADAM_STEP)
    delta = -ADAM_LR * (m_hat / (jnp.sqrt(v_hat) + ADAM_EPS) + ADAM_WD * w_v)
    return [g, delta, m2, v2]


def _adamw(name, g, w, m, v):
    return _elementwise(name, _adamw_math, [g, w, m, v], [F32] * 4)


_MATRICES = (("ffn1_gate", 1), ("ffn1_up", 1), ("ffn1_down", 0), ("w_in", 1), ("w_out", 0),
             ("ffn2_gate", 1), ("ffn2_up", 1), ("ffn2_down", 0))
_VECTORS = ("ln1_g", "ln1_b", "conv_b", "dt_bias", "a_log", "d_skip", "attn_norm_w", "ssd_norm_w",
            "ln2_g", "ln2_b", "ln3_g", "ln3_b")
_WEIGHT_ORDER = ("ln1_g", "ln1_b", "ffn1_gate", "ffn1_up", "ffn1_down", "w_in", "conv_w", "conv_b", "dt_bias", "a_log",
                 "d_skip", "attn_norm_w", "ssd_norm_w", "w_out", "ln2_g", "ln2_b", "ffn2_gate", "ffn2_up", "ffn2_down",
                 "ln3_g", "ln3_b")


def _pack_rows(vectors):
    parts = []
    for vec in vectors:
        flat = vec.reshape(-1)
        parts.append(jnp.pad(flat, (0, (-flat.shape[0]) % LANES)))
    flat = jnp.concatenate(parts)
    flat = jnp.pad(flat, (0, (-flat.shape[0]) % (8 * LANES)))
    return flat.reshape(-1, LANES)


def _unpack_rows(packed, shapes):
    flat = packed.reshape(-1)
    out, off = [], 0
    for shape in shapes:
        size = int(np.prod(shape))
        out.append(flat[off:off + size].reshape(shape))
        off += size + (-size) % LANES
    return out


def _assemble(stack, own, chip, axis):
    blocks = [jnp.where(chip == s, own, stack[s]) for s in range(N_CHIPS)]
    return jnp.concatenate(blocks, axis=axis)


def _split(full, axis):
    if axis == 0:
        return full.reshape(N_CHIPS, -1, full.shape[1])
    cols = full.shape[1] // N_CHIPS
    return jnp.stack([full[:, cols * s:cols * (s + 1)] for s in range(N_CHIPS)])


def kernel(x, positions, ln1_g, ln1_b, ffn1_gate, ffn1_up, ffn1_down, w_in, conv_w, conv_b, dt_bias, a_log, d_skip, attn_norm_w, ssd_norm_w, w_out, ln2_g, ln2_b, ffn2_gate, ffn2_up, ffn2_down, ln3_g, ln3_b, loss_target, m_ln1_g, m_ln1_b, m_ffn1_gate, m_ffn1_up, m_ffn1_down, m_w_in, m_conv_w, m_conv_b, m_dt_bias, m_a_log, m_d_skip, m_attn_norm_w, m_ssd_norm_w, m_w_out, m_ln2_g, m_ln2_b, m_ffn2_gate, m_ffn2_up, m_ffn2_down, m_ln3_g, m_ln3_b, v_ln1_g, v_ln1_b, v_ffn1_gate, v_ffn1_up, v_ffn1_down, v_w_in, v_conv_w, v_conv_b, v_dt_bias, v_a_log, v_d_skip, v_attn_norm_w, v_ssd_norm_w, v_w_out, v_ln2_g, v_ln2_b, v_ffn2_gate, v_ffn2_up, v_ffn2_down, v_ln3_g, v_ln3_b):
    given = dict(locals())
    wts = {n: given[n] for n in _WEIGHT_ORDER}
    mom_m = {n: given["m_" + n] for n in _WEIGHT_ORDER}
    mom_v = {n: given["v_" + n] for n in _WEIGHT_ORDER}
    chip = 2 * lax.axis_index("x") + lax.axis_index("y")

    core = lax.axis_index("c")
    first = [(n, axis) for n, axis in _MATRICES if n.startswith("ffn1")]
    rest = [(n, axis) for n, axis in _MATRICES if not n.startswith("ffn1")]
    own16 = {n: wts[n][0].astype(BF16) for n, _ in _MATRICES}
    gathered = _run_job(_GatherJob([own16[n] for n, _ in first]), "gather_ffn1")
    full = {n: _assemble(st, own16[n], chip, axis) for (n, axis), st in zip(first, gathered)}
    for n in _VECTORS:
        full[n] = wts[n]
    conv_rows = jnp.pad(wts["conv_w"][0], ((0, 32 - CONV_WIDTH), (0, 0)))
    late_job = _GatherJob([own16[n] for n, _ in rest] + [conv_rows])

    def late_weights(results):
        out = {n: _assemble(st, own16[n], chip, axis) for (n, axis), st in zip(rest, results)}
        out["conv_w"] = _assemble(results[-1], conv_rows, chip, 1)[:CONV_WIDTH]
        return out

    chip_sums = {}

    def core_sums(g, which, tag):
        partials = [_split(g[n], axis) for n, axis in which]
        from_sibling = _sibling_halves([p.astype(BF16) for p in partials], "sibling_halves_" + tag)
        for (n, _), p, o in zip(which, partials, from_sibling):
            chip_sums[n] = _half_sum("core_sum_" + n, p, o, core)
        return _ExchangeJob([chip_sums[n] for n, _ in which])

    last = [(n, axis) for n, axis in _MATRICES if n in ("ffn1_gate", "ffn1_up")]
    early = [(n, axis) for n, axis in _MATRICES if (n, axis) not in last]
    loss, grad_x, g, received_early = _local_step(x, positions, loss_target, full, late_job, late_weights,
                                                  lambda g_now: core_sums(g_now, early, "early"))
    received_last = _run_job(core_sums(g, last, "last"), "exchange_last")
    received = dict(zip([n for n, _ in last + early], received_last + received_early))
    half_totals = [_sum_slots("sum_partials_" + n, received[n], chip_sums[n], chip) for n, _ in _MATRICES]
    other_halves = _sibling_swap(half_totals)

    small_shapes = [g[n].shape for n in _VECTORS] + [g["conv_w"].shape, (1,)]
    total = _small_allreduce(_pack_rows([g[n] for n in _VECTORS] + [g["conv_w"], loss[0, :1]]))
    small = _unpack_rows(total, small_shapes)
    loss_out = small[-1].reshape(())

    grads, deltas, new_m, new_v = {}, {}, {}, {}
    for (n, _), mine, theirs in zip(_MATRICES, half_totals, other_halves):
        res = _adamw_halves("adamw_" + n, mine, theirs, core, wts[n][0], mom_m[n][0], mom_v[n][0])
        grads[n], deltas[n], new_m[n], new_v[n] = [r[None] for r in res]

    vec_shapes = [wts[n].shape for n in _VECTORS]
    res = _adamw("adamw_vectors", _pack_rows(small[:len(_VECTORS)]), _pack_rows([wts[n] for n in _VECTORS]),
                 _pack_rows([mom_m[n] for n in _VECTORS]), _pack_rows([mom_v[n] for n in _VECTORS]))
    for dst, packed in zip((grads, deltas, new_m, new_v), res):
        for n, val in zip(_VECTORS, _unpack_rows(packed, vec_shapes)):
            dst[n] = val

    cols = conv_w.shape[2]
    g_conv = lax.dynamic_slice_in_dim(small[len(_VECTORS)], chip * cols, cols, axis=1)
    res = _adamw("adamw_conv_w", g_conv, wts["conv_w"][0], mom_m["conv_w"][0], mom_v["conv_w"][0])
    grads["conv_w"], deltas["conv_w"], new_m["conv_w"], new_v["conv_w"] = [r[None] for r in res]

    return (loss_out, grad_x, *[grads[n] for n in _WEIGHT_ORDER], *[deltas[n] for n in _WEIGHT_ORDER],
            *[new_m[n] for n in _WEIGHT_ORDER], *[new_v[n] for n in _WEIGHT_ORDER])
```

```python
import functools

import numpy as np
import jax
import jax.numpy as jnp
from jax import lax
from jax.experimental import pallas as pl
from jax.experimental.pallas import tpu as pltpu

F32, BF16 = jnp.float32, jnp.bfloat16

D_MODEL = 1024
D_FF = 2816
N_HEADS = 12
HEAD_DIM = 64
D_ATTN = 768
D_SSD = 768
N_GROUPS = 4
HEADS_PER_GROUP = 3
D_STATE = 128
D_CONV = 1792
CONV_WIDTH = 4
ROPE_DIM = 16
ROPE_THETA = 500000.0
ALPHA = 2.0 ** 0.25
LN_EPS = 1e-5
RMS_EPS = 1e-6
ADAM_LR, ADAM_B1, ADAM_B2, ADAM_EPS, ADAM_WD, ADAM_STEP = 0.001, 0.9, 0.999, 1e-08, 0.01, 10

LANES = 128
GATE_UP_INTERLEAVE = 256
SEQ_BLOCK = 256
GROUP_LANES = 256
VMEM_LIMIT = 56 * 1024 * 1024
NEG = -1e30
MESH = pl.DeviceIdType.MESH
HIGHEST = lax.Precision.HIGHEST

_NT = (((1,), (1,)), ((), ()))
_TN = (((0,), (0,)), ((), ()))


def _params(*sem):
    return pltpu.CompilerParams(dimension_semantics=sem, vmem_limit_bytes=VMEM_LIMIT)


def _bf(v):
    return v.astype(BF16)


EPILOGUE_ROWS = 128


def _row_chunks(tm):
    return [slice(r, min(r + EPILOGUE_ROWS, tm)) for r in range(0, tm, EPILOGUE_ROWS)]


def _sigmoid(v):
    return 0.5 * jnp.tanh(0.5 * v) + 0.5


def _mm(name, pairs, *, scale=1.0, res=None, res_scale=1.0, out_dtype=F32, tm=512, tn=512):
    m, n = pairs[0][0].shape[0], pairs[0][1].shape[1]
    tm, tn = min(tm, m), min(tn, n)
    assert m % tm == 0 and n % tn == 0, (name, m, n, tm, tn)
    npair = len(pairs)

    def body(*refs):
        acc = None
        for a_ref, b_ref in zip(refs[:npair], refs[npair:2 * npair]):
            d = jnp.dot(_bf(a_ref[...]), b_ref[...], preferred_element_type=F32)
            acc = d if acc is None else acc + d
        if scale != 1.0:
            acc = acc * scale
        if res is not None:
            acc = acc + res_scale * refs[2 * npair][...]
        refs[-1][...] = acc.astype(out_dtype)

    in_specs = [pl.BlockSpec((tm, a.shape[1]), lambda i, j: (i, 0)) for a, _ in pairs]
    in_specs += [pl.BlockSpec((b.shape[0], tn), lambda i, j: (0, j)) for _, b in pairs]
    args = [a for a, _ in pairs] + [b for _, b in pairs]
    if res is not None:
        in_specs.append(pl.BlockSpec((tm, tn), lambda i, j: (i, j)))
        args.append(res)
    return pl.pallas_call(
        body, name=name, grid=(m // tm, n // tn), in_specs=in_specs,
        out_specs=pl.BlockSpec((tm, tn), lambda i, j: (i, j)),
        out_shape=jax.ShapeDtypeStruct((m, n), out_dtype),
        compiler_params=_params("parallel", "parallel"),
    )(*args)


def _mm_tn(name, x, dy, *, scale=1.0, tk=512, tn=512, tt=1024):
    t, k = x.shape
    n = dy.shape[1]
    tk, tn, tt = min(tk, k), min(tn, n), min(tt, t)
    assert k % tk == 0 and n % tn == 0 and t % tt == 0, (name, k, n, t)
    nt = t // tt

    def body(x_ref, dy_ref, o_ref):
        step = pl.program_id(2)
        d = lax.dot_general(_bf(x_ref[...]), _bf(dy_ref[...]), _TN, preferred_element_type=F32)

        @pl.when(step == 0)
        def _():
            o_ref[...] = d

        @pl.when(step > 0)
        def _():
            o_ref[...] += d

        if scale != 1.0:
            @pl.when(step == nt - 1)
            def _():
                o_ref[...] = o_ref[...] * scale

    return pl.pallas_call(
        body, name=name, grid=(k // tk, n // tn, nt),
        in_specs=[pl.BlockSpec((tt, tk), lambda i, j, s: (s, i)), pl.BlockSpec((tt, tn), lambda i, j, s: (s, j))],
        out_specs=pl.BlockSpec((tk, tn), lambda i, j, s: (i, j)),
        out_shape=jax.ShapeDtypeStruct((k, n), F32),
        compiler_params=_params("parallel", "parallel", "arbitrary"),
    )(x, dy)


def _mm_tn_sections(name, x, dys, *, tt=512):
    t, k = x.shape
    tt = min(tt, t)
    cuts = np.cumsum([0] + [d.shape[1] for d in dys]).tolist()
    ns = len(dys)

    def body(*refs):
        x_ref, o_ref = refs[0], refs[1 + ns]
        step = pl.program_id(0)
        xt = x_ref[...].T
        parts = [jnp.dot(xt, refs[1 + a][...], preferred_element_type=F32) for a in range(ns)]

        @pl.when(step == 0)
        def _():
            for a in range(ns):
                o_ref[:, cuts[a]:cuts[a + 1]] = parts[a]

        @pl.when(step > 0)
        def _():
            for a in range(ns):
                o_ref[:, cuts[a]:cuts[a + 1]] += parts[a]

    return pl.pallas_call(
        body, name=name, grid=(t // tt,),
        in_specs=[pl.BlockSpec((tt, k), lambda s: (s, 0))] + [pl.BlockSpec((tt, d.shape[1]), lambda s: (s, 0)) for d in dys],
        out_specs=pl.BlockSpec((k, cuts[-1]), lambda s: (0, 0)),
        out_shape=jax.ShapeDtypeStruct((k, cuts[-1]), F32),
        compiler_params=_params("arbitrary"),
    )(x, *dys)


def _mm_tn_gate_up(name, x, dau, *, tt=1024):
    t, k = x.shape
    gi = GATE_UP_INTERLEAVE
    nj = dau.shape[1] // (2 * gi)
    tt = min(tt, t)
    nt = t // tt

    def body(x_ref, dy_ref, g_ref, u_ref):
        step = pl.program_id(1)
        d = lax.dot_general(_bf(x_ref[...]), dy_ref[...], _TN, preferred_element_type=F32)

        @pl.when(step == 0)
        def _():
            g_ref[...] = d[:, :gi]
            u_ref[...] = d[:, gi:]

        @pl.when(step > 0)
        def _():
            g_ref[...] += d[:, :gi]
            u_ref[...] += d[:, gi:]

    out = pl.BlockSpec((k, gi), lambda j, s: (0, j))
    return pl.pallas_call(
        body, name=name, grid=(nj, nt),
        in_specs=[pl.BlockSpec((tt, k), lambda j, s: (s, 0)), pl.BlockSpec((tt, 2 * gi), lambda j, s: (s, j))],
        out_specs=[out, out],
        out_shape=[jax.ShapeDtypeStruct((k, gi * nj), F32)] * 2,
        compiler_params=_params("parallel", "arbitrary"),
    )(x, dau)


def _carried(carry, ins, outs, sems, step, total):
    start, forward, finish = carry.phases(ins, outs, sems)
    pl.when(step == 0)(start)
    if forward is not None:
        pl.when(step == (3 * total) // 4)(forward)
    return lambda: pl.when(step == total - 1)(finish)


def _resident(shape):
    return pl.BlockSpec(shape, lambda i: (0,) * len(shape), pipeline_mode=pl.Buffered(1))


def _ffn_fwd(name, x16, res, wgu, wd, g, b, *, tm=512, carry=None):
    t, k = x16.shape
    gi = GATE_UP_INTERLEAVE
    nj, n, ni = wd.shape[0] // gi, wd.shape[1], t // tm
    nc = carry.n if carry is not None else 0

    def body(*refs):
        x_ref, res_ref, wgu_ref, wd_ref, g_ref, b_ref = refs[:6]
        au_ref, hm_ref, y_ref, r_ref, y16_ref = refs[6 + nc:11 + nc]
        if carry is not None:
            finish = _carried(carry, refs[6:6 + nc], refs[11 + nc:11 + 2 * nc], refs[11 + 2 * nc:], pl.program_id(0), ni)
        xv = x_ref[...]
        acc = jnp.zeros((tm, n), F32)
        for j in range(nj):
            au = jnp.dot(xv, wgu_ref[:, 2 * gi * j:2 * gi * (j + 1)], preferred_element_type=F32)
            a, u = au[:, :gi], au[:, gi:]
            au_ref[:, 2 * gi * j:2 * gi * (j + 1)] = _bf(au)
            hm = _bf(a * _sigmoid(a) * u)
            hm_ref[:, gi * j:gi * (j + 1)] = hm
            acc = acc + jnp.dot(hm, wd_ref[gi * j:gi * (j + 1), :], preferred_element_type=F32)
        r = ALPHA * res_ref[...] + 0.5 * acc
        r_ref[...] = r
        y = _layer_norm(r, g_ref[...], b_ref[...])
        y_ref[...] = y
        y16_ref[...] = _bf(y)
        if carry is not None:
            finish()

    row = lambda c: pl.BlockSpec((tm, c), lambda i: (i, 0))
    hbm = pl.BlockSpec(memory_space=pltpu.HBM)
    res_ = pl.pallas_call(
        body, name=name, grid=(ni,),
        in_specs=[row(k), row(n), _resident(wgu.shape), _resident(wd.shape), _resident(g.shape), _resident(b.shape)] + [hbm] * nc,
        out_specs=[row(2 * gi * nj), row(gi * nj), row(n), row(n), row(n)] + [hbm] * nc,
        out_shape=[jax.ShapeDtypeStruct((t, 2 * gi * nj), BF16), jax.ShapeDtypeStruct((t, gi * nj), BF16),
                   jax.ShapeDtypeStruct((t, n), F32), jax.ShapeDtypeStruct((t, n), F32), jax.ShapeDtypeStruct((t, n), BF16)]
        + (carry.out_shape if carry is not None else []),
        scratch_shapes=carry.scratch_shapes if carry is not None else [],
        compiler_params=_params("arbitrary" if carry is not None else "parallel"),
    )(x16, res, wgu, wd, g, b, *(carry.operands if carry is not None else []))
    return tuple(res_[:5]) + ((carry.results(res_[5:]),) if carry is not None else ())


def _ffn_bwd(name, dr16, dr, wdt, au, wgut, *, tm=512, carry=None):
    t, n = dr16.shape
    gi = GATE_UP_INTERLEAVE
    nj, ni = wdt.shape[1] // gi, t // tm
    nc = carry.n if carry is not None else 0

    def body(*refs):
        dr16_ref, dr_ref, wdt_ref, au_ref, wgut_ref = refs[:5]
        dau_ref, dx_ref = refs[5 + nc:7 + nc]
        if carry is not None:
            finish = _carried(carry, refs[5:5 + nc], refs[7 + nc:7 + 2 * nc], refs[7 + 2 * nc:], pl.program_id(0), ni)
        drv = dr16_ref[...]
        acc = jnp.zeros((tm, n), F32)
        for j in range(nj):
            dhm = jnp.dot(drv, wdt_ref[:, gi * j:gi * (j + 1)], preferred_element_type=F32) * 0.5
            au_v = au_ref[:, 2 * gi * j:2 * gi * (j + 1)].astype(F32)
            a, u = au_v[:, :gi], au_v[:, gi:]
            sig = _sigmoid(a)
            silu = a * sig
            dau = jnp.concatenate([_bf(dhm * u * (sig + silu - silu * sig)), _bf(dhm * silu)], axis=1)
            dau_ref[:, 2 * gi * j:2 * gi * (j + 1)] = dau
            acc = acc + jnp.dot(dau, wgut_ref[2 * gi * j:2 * gi * (j + 1), :], preferred_element_type=F32)
        dx_ref[...] = ALPHA * dr_ref[...] + acc
        if carry is not None:
            finish()

    row = lambda c: pl.BlockSpec((tm, c), lambda i: (i, 0))
    hbm = pl.BlockSpec(memory_space=pltpu.HBM)
    res_ = pl.pallas_call(
        body, name=name, grid=(ni,),
        in_specs=[row(n), row(n), _resident(wdt.shape), row(2 * gi * nj), _resident(wgut.shape)] + [hbm] * nc,
        out_specs=[row(2 * gi * nj), row(n)] + [hbm] * nc,
        out_shape=[jax.ShapeDtypeStruct((t, 2 * gi * nj), BF16), jax.ShapeDtypeStruct((t, n), F32)]
        + (carry.out_shape if carry is not None else []),
        scratch_shapes=carry.scratch_shapes if carry is not None else [],
        compiler_params=_params("arbitrary" if carry is not None else "parallel"),
    )(dr16, dr, wdt, au, wgut, *(carry.operands if carry is not None else []))
    return tuple(res_[:2]) + ((carry.results(res_[2:]),) if carry is not None else ())


def _layer_norm(r, g, b):
    mu = jnp.mean(r, axis=-1, keepdims=True)
    var = jnp.mean(jnp.square(r - mu), axis=-1, keepdims=True)
    return (r - mu) * lax.rsqrt(var + LN_EPS) * g + b


def _mm_res_ln(name, a, w, res, g, b, *, scale, tm=256):
    t, k = a.shape
    n = w.shape[1]

    def body(a_ref, w_ref, res_ref, g_ref, b_ref, y_ref, r_ref, y16_ref):
        for rows in _row_chunks(tm):
            r = ALPHA * res_ref[rows, :] + scale * jnp.dot(_bf(a_ref[rows, :]), w_ref[...], preferred_element_type=F32)
            r_ref[rows, :] = r
            y = _layer_norm(r, g_ref[...], b_ref[...])
            y_ref[rows, :] = y
            y16_ref[rows, :] = _bf(y)

    row = lambda c: pl.BlockSpec((tm, c), lambda i: (i, 0))
    const = lambda shape: pl.BlockSpec(shape, lambda i: (0, 0))
    return pl.pallas_call(
        body, name=name, grid=(t // tm,),
        in_specs=[row(k), const((k, n)), row(n), const((1, n)), const((1, n))],
        out_specs=[row(n), row(n), row(n)],
        out_shape=[jax.ShapeDtypeStruct((t, n), F32), jax.ShapeDtypeStruct((t, n), F32), jax.ShapeDtypeStruct((t, n), BF16)],
        compiler_params=_params("parallel"),
    )(a, w, res, g, b)


def _rowwise(name, fn, rows, consts, row_outs, acc_outs=(), tm=256):
    rows = [r if isinstance(r, tuple) else (r, r.shape[1]) for r in rows]
    t = rows[0][0].shape[0]
    tm = min(tm, t)
    assert t % tm == 0
    nr, nc, no, na = len(rows), len(consts), len(row_outs), len(acc_outs)

    def body(*refs):
        vals = [r[...] for r in refs[:nr + nc]]
        outs, accs = fn(*vals)
        for o_ref, o in zip(refs[nr + nc:nr + nc + no], outs):
            o_ref[...] = o.astype(o_ref.dtype)
        if na:
            step = pl.program_id(0)
            acc_refs = refs[nr + nc + no:]

            @pl.when(step == 0)
            def _():
                for a_ref, a in zip(acc_refs, accs):
                    a_ref[...] = a

            @pl.when(step > 0)
            def _():
                for a_ref, a in zip(acc_refs, accs):
                    a_ref[...] += a

    in_specs = [pl.BlockSpec((tm, w), lambda i: (i, 0)) for _, w in rows]
    in_specs += [pl.BlockSpec(c.shape, lambda i, nd=c.ndim: (0,) * nd) for c in consts]
    out_specs = [pl.BlockSpec((tm, c), lambda i: (i, 0)) for c, _ in row_outs]
    out_specs += [pl.BlockSpec(s, lambda i: (0, 0)) for s in acc_outs]
    out_shape = [jax.ShapeDtypeStruct((t, c), dt) for c, dt in row_outs]
    out_shape += [jax.ShapeDtypeStruct(s, F32) for s in acc_outs]
    res = pl.pallas_call(
        body, name=name, grid=(t // tm,), in_specs=in_specs, out_specs=out_specs, out_shape=out_shape,
        compiler_params=_params("arbitrary" if na else "parallel"),
    )(*[r for r, _ in rows], *consts)
    return res


def _ln_bwd(name, r, g, b, dy):
    def fn(r_v, dy_v, g_v, b_v):
        _, vjp = jax.vjp(_layer_norm, r_v, g_v, b_v)
        dr, dg, db = vjp(dy_v)
        return [dr, dr], [dg, db]
    return _rowwise(name, fn, [r, dy], [g, b], [(r.shape[1], F32), (r.shape[1], BF16)], [(1, r.shape[1])] * 2)


def _ln_loss_bwd(name, r, g, b, target):
    def fn(r_v, t_v, g_v, b_v):
        def loss_fn(rr, gg, bb):
            err = jnp.square(_layer_norm(rr, gg, bb) - t_v)
            return 0.5 * jnp.sum(jnp.mean(err, axis=-1, keepdims=True), axis=0, keepdims=True)
        loss, vjp = jax.vjp(loss_fn, r_v, g_v, b_v)
        dr, dg, db = vjp(jnp.ones((1, 1), F32))
        return [dr, dr], [dg, db, jnp.broadcast_to(loss, (1, LANES))]
    return _rowwise(name, fn, [r, target], [g, b], [(r.shape[1], F32), (r.shape[1], BF16)],
                    [(1, r.shape[1])] * 2 + [(1, LANES)])


def _rope_tables(posf, invf, sgn):
    ang = posf * invf
    return jnp.cos(ang), jnp.sin(ang) * sgn


def _rope_apply(tv, cos, sin):
    lane = lax.broadcasted_iota(jnp.int32, cos.shape, 1)
    first = (lane % HEAD_DIM) < (ROPE_DIM // 2)
    outs = []
    for gidx in range(tv.shape[1] // LANES):
        tg = tv[:, LANES * gidx:LANES * (gidx + 1)]
        sw = jnp.where(first, pltpu.roll(tg, LANES - ROPE_DIM // 2, 1), pltpu.roll(tg, ROPE_DIM // 2, 1))
        outs.append(tg * cos + sw * sin)
    return jnp.concatenate(outs, axis=1)


def _proj_in(h16, w_in, posf, invf, sgn, *, tm=512):
    t, k = h16.shape
    cuts = [0, D_ATTN, 2 * D_ATTN, 3 * D_ATTN, 3 * D_ATTN + D_SSD, 3 * D_ATTN + D_SSD + D_CONV, w_in.shape[1]]

    def body(h_ref, w_ref, pos_ref, invf_ref, sgn_ref, q_ref, k_ref, v_ref, z_ref, xbc_ref, dt_ref, cs_ref):
        hv = h_ref[...]
        part = lambda a: jnp.dot(hv, w_ref[:, cuts[a]:cuts[a + 1]], preferred_element_type=F32)
        cos, sin = _rope_tables(pos_ref[...], invf_ref[...], sgn_ref[...])
        cs_ref[...] = jnp.concatenate([cos, sin], axis=1)
        q_ref[...] = _bf(_rope_apply(part(0), cos, sin) * (HEAD_DIM ** -0.5))
        k_ref[...] = _bf(_rope_apply(part(1), cos, sin))
        v_ref[...] = _bf(part(2))
        z_ref[...] = part(3)
        xbc_ref[...] = part(4)
        dt_ref[...] = part(5)

    row = lambda c: pl.BlockSpec((tm, c), lambda i: (i, 0))
    widths = [D_ATTN, D_ATTN, D_ATTN, D_SSD, D_CONV, LANES, 2 * LANES]
    dtypes = [BF16, BF16, BF16, F32, F32, F32, F32]
    return pl.pallas_call(
        body, name="proj_in", grid=(t // tm,),
        in_specs=[row(k), _resident(w_in.shape), row(1), _resident(invf.shape), _resident(sgn.shape)],
        out_specs=[row(c) for c in widths],
        out_shape=[jax.ShapeDtypeStruct((t, c), dt) for c, dt in zip(widths, dtypes)],
        compiler_params=_params("parallel"),
    )(h16, w_in, posf, invf, sgn)


def _rope_bwd(dq, dk, cs):
    def fn(dq_v, dk_v, cs_v):
        cos, sin = cs_v[:, :LANES], -cs_v[:, LANES:]
        gq = _rope_apply(dq_v * (HEAD_DIM ** -0.5), cos, sin)
        gk = _rope_apply(dk_v, cos, sin)
        return [jnp.concatenate([gq, gk], axis=1)], []
    return _rowwise("rope_bwd", fn, [dq, dk, cs], [], [(2 * D_ATTN, BF16)])[0]


def _rms(v, w):
    return v * lax.rsqrt(jnp.mean(v * v, axis=-1, keepdims=True) + RMS_EPS) * w


def _ungroup(yg):
    w = HEADS_PER_GROUP * HEAD_DIM
    return jnp.concatenate([yg[:, GROUP_LANES * g:GROUP_LANES * g + w] for g in range(N_GROUPS)], axis=1)


def _group(xs):
    w = HEADS_PER_GROUP * HEAD_DIM
    parts = []
    for g in range(N_GROUPS):
        parts += [xs[:, w * g:w * (g + 1)], jnp.zeros((xs.shape[0], GROUP_LANES - w), xs.dtype)]
    return jnp.concatenate(parts, axis=1)


def _norms_fn(attn, yg, xs, z, w_attn, w_ssd, dskip):
    a_n = _rms(attn, w_attn)
    y = _ungroup(yg) + dskip * xs
    y_n = _rms(y * (z * jax.nn.sigmoid(z)), w_ssd)
    return jnp.concatenate([a_n, y_n], axis=1)


def _norms_fwd(attn, yg, xbc, z, w_attn, w_ssd, dskip):
    def fn(*v):
        return [_norms_fn(*v)], []
    return _rowwise("norms_fwd", fn, [attn, yg, (xbc, D_SSD), z], [w_attn, w_ssd, dskip], [(D_ATTN + D_SSD, BF16)])[0]


def _norms_bwd(attn, yg, xbc, z, w_attn, w_ssd, dskip, dcat):
    def fn(attn_v, yg_v, xs_v, z_v, dcat_v, wa_v, ws_v, dk_v):
        _, vjp = jax.vjp(_norms_fn, attn_v, yg_v, xs_v, z_v, wa_v, ws_v, dk_v)
        d_attn, d_yg, d_xs, d_z, d_wa, d_ws, d_dk = vjp(dcat_v)
        return [d_attn, d_yg, d_xs, d_z], [d_wa, d_ws, d_dk]
    return _rowwise("norms_bwd", fn, [attn, yg, (xbc, D_SSD), z, dcat], [w_attn, w_ssd, dskip],
                    [(D_ATTN, F32), (N_GROUPS * GROUP_LANES, F32), (D_SSD, F32), (D_SSD, BF16)], [(1, D_SSD)] * 3)


def _ssd_prep_fn(xs, dtp, dtb, alog, e_x, e_a):
    dt = jax.nn.softplus(dtp + dtb)
    a = -jnp.exp(alog)
    dtg = jnp.dot(dt, e_x, precision=HIGHEST, preferred_element_type=F32)
    xdtg = _group(xs) * dtg
    dag = jnp.dot(dt * a, e_a, precision=HIGHEST, preferred_element_type=F32)
    return xdtg, dag


def _ssd_prep_fwd(xbc, dtp, dtb, alog, e_x, e_a):
    def fn(xbc_v, dtp_v, dtb_v, alog_v, ex_v, ea_v):
        xdtg, dag = _ssd_prep_fn(xbc_v[:, :D_SSD], dtp_v, dtb_v, alog_v, ex_v, ea_v)
        return [xdtg, xbc_v[:, D_SSD:], dag], []
    return _rowwise("ssd_prep_fwd", fn, [xbc, dtp], [dtb, alog, e_x, e_a],
                    [(N_GROUPS * GROUP_LANES, BF16), (D_CONV - D_SSD, BF16), (N_GROUPS * LANES, F32)])


def _ssd_prep_bwd(xbc, dtp, dtb, alog, e_x, e_a, dxdtg, ddag, dxs_a, db, dc):
    def fn(xs_v, dtp_v, dxdtg_v, ddag_v, dxs_a_v, db_v, dc_v, dtb_v, alog_v, ex_v, ea_v):
        _, vjp = jax.vjp(lambda a, b, c, d: _ssd_prep_fn(a, b, c, d, ex_v, ea_v), xs_v, dtp_v, dtb_v, alog_v)
        dxs, ddtp, ddtb, dalog = vjp((dxdtg_v, ddag_v))
        return [jnp.concatenate([dxs + dxs_a_v, db_v, dc_v], axis=1), ddtp], [ddtb, dalog]
    return _rowwise("ssd_prep_bwd", fn, [(xbc, D_SSD), dtp, dxdtg, ddag, dxs_a, db, dc], [dtb, alog, e_x, e_a],
                    [(D_CONV, F32), (LANES, BF16)], [(1, LANES)] * 2)


def _shift_down(u, d):
    if d == 0:
        return u
    row = lax.broadcasted_iota(jnp.int32, u.shape, 0)
    return jnp.where(row >= d, pltpu.roll(u, d, 0), 0.0)


def _shift_up(u, d):
    if d == 0:
        return u
    s = u.shape[0]
    row = lax.broadcasted_iota(jnp.int32, u.shape, 0)
    return jnp.where(row < s - d, pltpu.roll(u, s - d, 0), 0.0)


def _conv_pre(u, w, b):
    acc = b
    for k in range(CONV_WIDTH):
        acc = acc + w[k:k + 1, :] * _shift_down(u, CONV_WIDTH - 1 - k)
    return acc


def _conv_fwd(u, w, b, *, tc=256):
    nb, s, c = u.shape

    def body(u_ref, w_ref, b_ref, o_ref):
        pre = _conv_pre(u_ref[0], w_ref[...], b_ref[...])
        o_ref[0] = pre * jax.nn.sigmoid(pre)

    return pl.pallas_call(
        body, name="conv_fwd", grid=(c // tc, nb),
        in_specs=[pl.BlockSpec((1, s, tc), lambda j, i: (i, 0, j)), pl.BlockSpec((CONV_WIDTH, tc), lambda j, i: (0, j)),
                  pl.BlockSpec((1, tc), lambda j, i: (0, j))],
        out_specs=pl.BlockSpec((1, s, tc), lambda j, i: (i, 0, j)),
        out_shape=jax.ShapeDtypeStruct((nb, s, c), F32),
        compiler_params=_params("parallel", "parallel"),
    )(u, w, b)


def _conv_bwd(u, w, b, dout, *, tc=256):
    nb, s, c = u.shape

    def body(u_ref, w_ref, b_ref, d_ref, du_ref, dw_ref, db_ref):
        uv, wv = u_ref[0], w_ref[...]
        pre = _conv_pre(uv, wv, b_ref[...])
        sig = jax.nn.sigmoid(pre)
        dpre = d_ref[0] * (sig * (1.0 + pre * (1.0 - sig)))
        du = jnp.zeros_like(uv)
        dws = []
        for k in range(CONV_WIDTH):
            du = du + wv[k:k + 1, :] * _shift_up(dpre, CONV_WIDTH - 1 - k)
            dws.append(jnp.sum(dpre * _shift_down(uv, CONV_WIDTH - 1 - k), axis=0, keepdims=True))
        du_ref[0] = _bf(du)
        dwv = jnp.concatenate(dws + [jnp.zeros((8 - CONV_WIDTH, tc), F32)], axis=0)
        dbv = jnp.sum(dpre, axis=0, keepdims=True)
        first = pl.program_id(1) == 0

        @pl.when(first)
        def _():
            dw_ref[...] = dwv
            db_ref[...] = dbv

        @pl.when(jnp.logical_not(first))
        def _():
            dw_ref[...] += dwv
            db_ref[...] += dbv

    blk = pl.BlockSpec((1, s, tc), lambda j, i: (i, 0, j))
    return pl.pallas_call(
        body, name="conv_bwd", grid=(c // tc, nb),
        in_specs=[blk, pl.BlockSpec((CONV_WIDTH, tc), lambda j, i: (0, j)), pl.BlockSpec((1, tc), lambda j, i: (0, j)), blk],
        out_specs=[blk, pl.BlockSpec((8, tc), lambda j, i: (0, j)), pl.BlockSpec((1, tc), lambda j, i: (0, j))],
        out_shape=[jax.ShapeDtypeStruct((nb, s, c), BF16), jax.ShapeDtypeStruct((8, c), F32), jax.ShapeDtypeStruct((1, c), F32)],
        compiler_params=_params("parallel", "arbitrary"),
    )(u, w, b, dout)


FWD_KEY_BLOCK = 256


def _branch_bias_table(seq, kb):
    ratio = SEQ_BLOCK // kb
    key = np.arange(kb)[None, :, None]
    query = np.arange(SEQ_BLOCK)[None, None, :]
    delta = (np.arange(seq // kb)[:, None, None] - (ratio - 1)) * kb + query - key
    cnt = np.zeros(delta.shape, np.float64)
    for window, dilation in ((128, 1), (512, 4), (2048, 16)):
        cnt += (delta >= 0) & (delta % dilation == 0) & (delta <= window)
    return jnp.asarray(np.where(cnt > 0, np.log(np.maximum(cnt, 1.0)), NEG).astype(np.float32))


HEADS_PER_BLOCK = LANES // HEAD_DIM


def _head_rows(v, h):
    row = lax.broadcasted_iota(jnp.int32, v.shape, 0)
    return jnp.where((row >= HEAD_DIM * h) & (row < HEAD_DIM * (h + 1)), v, jnp.zeros_like(v))


def _attn_fwd(q, k, v, bias):
    nb_, s, _ = q.shape
    ab, kb = SEQ_BLOCK, FWD_KEY_BLOCK
    nblk, nkb, ratio = s // ab, s // kb, ab // kb

    def body(q_ref, k_ref, v_ref, b_ref, o_ref, lse_ref, vt_s):
        i = pl.program_id(2)

        @pl.when(i == 0)
        def _():
            for jb in range(nkb):
                vt_s[jb] = v_ref[0, kb * jb:kb * (jb + 1), :].T

        qt = q_ref[0].T
        qts = [_head_rows(qt, h) for h in range(HEADS_PER_BLOCK)]

        last = ratio * (i + 1) - 1

        def scores(j):
            kj = k_ref[0, pl.ds(pl.multiple_of(j * kb, kb), kb), :]
            return [jnp.dot(kj, qts[h], preferred_element_type=F32) for h in range(HEADS_PER_BLOCK)]

        def step(j, carry):
            ahead = scores(jnp.minimum(j + 1, last))
            lb = b_ref[ratio * i - j + (ratio - 1)]
            out = []
            for h in range(HEADS_PER_BLOCK):
                m, l, acc = carry[3 * h:3 * h + 3]
                st = carry[3 * HEADS_PER_BLOCK + h] + lb
                m_new = jnp.maximum(m, jnp.max(st, axis=0, keepdims=True))
                p = jnp.exp(st - m_new)
                a = jnp.exp(m - m_new)
                l = a * l + jnp.sum(p, axis=0, keepdims=True)
                vt = vt_s[j, HEAD_DIM * h:HEAD_DIM * (h + 1), :]
                acc = a * acc + jnp.dot(vt, _bf(p), preferred_element_type=F32)
                out += [m_new, l, acc]
            return tuple(out) + tuple(ahead)

        init = (jnp.full((1, ab), NEG, F32), jnp.zeros((1, ab), F32), jnp.zeros((HEAD_DIM, ab), F32)) * HEADS_PER_BLOCK
        res = lax.fori_loop(0, ratio * (i + 1), step, init + tuple(scores(0)))
        ot = jnp.concatenate([res[3 * h + 2] / res[3 * h + 1] for h in range(HEADS_PER_BLOCK)], axis=0)
        o_ref[0] = ot.T
        rows = [res[3 * h] + jnp.log(res[3 * h + 1]) for h in range(HEADS_PER_BLOCK)]
        lse_ref[0, 0, 0] = jnp.concatenate(rows + [jnp.zeros((8 - HEADS_PER_BLOCK, ab), F32)], axis=0)

    qblk = pl.BlockSpec((1, ab, LANES), lambda b, hp, i: (b, i, hp))
    full = pl.BlockSpec((1, s, LANES), lambda b, hp, i: (b, 0, hp))
    return pl.pallas_call(
        body, name="attn_fwd", grid=(nb_, D_ATTN // LANES, nblk),
        in_specs=[qblk, full, full, pl.BlockSpec((nkb, kb, ab), lambda b, hp, i: (0, 0, 0))],
        out_specs=[qblk, pl.BlockSpec((1, 1, 1, 8, ab), lambda b, hp, i: (b, hp, i, 0, 0))],
        out_shape=[jax.ShapeDtypeStruct((nb_, s, D_ATTN), F32),
                   jax.ShapeDtypeStruct((nb_, D_ATTN // LANES, nblk, 8, ab), F32)],
        scratch_shapes=[pltpu.VMEM((nkb, LANES, kb), BF16)],
        compiler_params=_params("parallel", "parallel", "arbitrary"),
    )(q, k, v, bias)


def _attn_bwd(q, k, v, o, do, lse, bias):
    nb_, s, _ = q.shape
    ab = SEQ_BLOCK
    nblk = s // ab

    nh = HEADS_PER_BLOCK

    def body(q_ref, k_ref, v_ref, o_ref, do_ref, lse_ref, b_ref, dq_ref, dk_ref, dv_ref,
             qt_s, dot_s, kt_s, dqt_s, do16_s, d_s, dk_acc, dv_acc):
        for jb in range(nblk):
            sl = slice(ab * jb, ab * (jb + 1))
            qt, kt = q_ref[0, sl, :].T, k_ref[0, sl, :].T
            do = do_ref[0, sl, :]
            dot = do.T
            prod = dot * o_ref[0, sl, :].T
            do16_s[sl, :] = _bf(do)
            for h in range(nh):
                qt_s[nh * jb + h] = _head_rows(qt, h)
                kt_s[nh * jb + h] = _head_rows(kt, h)
                dot_s[nh * jb + h] = _head_rows(_bf(dot), h)
            d_s[jb] = jnp.concatenate(
                [jnp.sum(prod[HEAD_DIM * h:HEAD_DIM * (h + 1)], axis=0, keepdims=True) for h in range(nh)]
                + [jnp.zeros((8 - nh, ab), F32)], axis=0)
            dqt_s[jb] = jnp.zeros((LANES, ab), F32)

        def outer(j, carry):
            ks = pl.ds(pl.multiple_of(j * ab, ab), ab)
            kj, vj = k_ref[0, ks, :], v_ref[0, ks, :]
            dk_acc[...] = jnp.zeros_like(dk_acc)
            dv_acc[...] = jnp.zeros_like(dv_acc)

            def inner(i, c2):
                qs = pl.ds(pl.multiple_of(i * ab, ab), ab)
                qi, doi = q_ref[0, qs, :], do16_s[qs, :]
                lb = b_ref[i - j]
                for h in range(nh):
                    st = jnp.dot(kj, qt_s[nh * i + h], preferred_element_type=F32) + lb
                    pt = jnp.exp(st - lse_ref[0, 0, i, h:h + 1, :])
                    dpt = jnp.dot(vj, dot_s[nh * i + h], preferred_element_type=F32)
                    dst16 = _bf(pt * (dpt - d_s[i, h:h + 1, :]))
                    dv_acc[h] += jnp.dot(_bf(pt), doi, preferred_element_type=F32)
                    dk_acc[h] += jnp.dot(dst16, qi, preferred_element_type=F32)
                    dqt_s[i] += jnp.dot(kt_s[nh * j + h], dst16, preferred_element_type=F32)
                return c2

            lax.fori_loop(j, nblk, inner, 0)
            lane = lax.broadcasted_iota(jnp.int32, (ab, LANES), 1)
            dk_ref[0, ks, :] = jnp.where(lane < HEAD_DIM, dk_acc[0], dk_acc[1])
            dv_ref[0, ks, :] = _bf(jnp.where(lane < HEAD_DIM, dv_acc[0], dv_acc[1]))
            return carry

        lax.fori_loop(0, nblk, outer, 0)
        for jb in range(nblk):
            dq_ref[0, ab * jb:ab * (jb + 1), :] = dqt_s[jb].T

    assert nh == 2
    full = pl.BlockSpec((1, s, LANES), lambda b, hp: (b, 0, hp))
    return pl.pallas_call(
        body, name="attn_bwd", grid=(nb_, D_ATTN // LANES),
        in_specs=[full] * 5 + [pl.BlockSpec((1, 1, nblk, 8, ab), lambda b, hp: (b, hp, 0, 0, 0)),
                               pl.BlockSpec((nblk, ab, ab), lambda b, hp: (0, 0, 0))],
        out_specs=[full, full, full],
        out_shape=[jax.ShapeDtypeStruct((nb_, s, D_ATTN), F32), jax.ShapeDtypeStruct((nb_, s, D_ATTN), F32),
                   jax.ShapeDtypeStruct((nb_, s, D_ATTN), BF16)],
        scratch_shapes=[pltpu.VMEM((nh * nblk, LANES, ab), BF16), pltpu.VMEM((nh * nblk, LANES, ab), BF16),
                        pltpu.VMEM((nh * nblk, LANES, ab), BF16), pltpu.VMEM((nblk, LANES, ab), F32),
                        pltpu.VMEM((s, LANES), BF16), pltpu.VMEM((nblk, 8, ab), F32),
                        pltpu.VMEM((nh, ab, LANES), F32), pltpu.VMEM((nh, ab, LANES), F32)],
        compiler_params=_params("parallel", "parallel"),
    )(q, k, v, o, do, lse, bias)


def _cumsum_fwd(dag):
    nb_, s, c = dag.shape
    ab = SEQ_BLOCK

    def body(a_ref, o_ref, ot_ref):
        r = lax.broadcasted_iota(jnp.int32, (ab, ab), 0)
        cc = lax.broadcasted_iota(jnp.int32, (ab, ab), 1)
        tri = (r >= cc).astype(F32)
        carry = jnp.zeros((1, c), F32)
        for i in range(s // ab):
            loc = jnp.dot(tri, a_ref[0, ab * i:ab * (i + 1), :], precision=HIGHEST, preferred_element_type=F32) + carry
            o_ref[0, ab * i:ab * (i + 1), :] = loc
            ot_ref[0, :, ab * i:ab * (i + 1)] = loc.T
            carry = loc[ab - 1:ab, :]

    return pl.pallas_call(
        body, name="ssd_cumsum", grid=(nb_,),
        in_specs=[pl.BlockSpec((1, s, c), lambda b: (b, 0, 0))],
        out_specs=[pl.BlockSpec((1, s, c), lambda b: (b, 0, 0)), pl.BlockSpec((1, c, s), lambda b: (b, 0, 0))],
        out_shape=[jax.ShapeDtypeStruct((nb_, s, c), F32), jax.ShapeDtypeStruct((nb_, c, s), F32)],
        compiler_params=_params("parallel"),
    )(dag)


def _cumsum_bwd(dcol, drow):
    nb_, s, c = dcol.shape
    ab = SEQ_BLOCK

    def body(c_ref, r_ref, o_ref):
        r = lax.broadcasted_iota(jnp.int32, (ab, ab), 0)
        cc = lax.broadcasted_iota(jnp.int32, (ab, ab), 1)
        tri = (r <= cc).astype(F32)
        carry = jnp.zeros((1, c), F32)
        for i in reversed(range(s // ab)):
            rows = r_ref[0, :, ab * i:ab * (i + 1)].T
            parts = []
            for g in range(N_GROUPS):
                parts += [rows[:, 8 * g:8 * (g + 1)], jnp.zeros((ab, LANES - 8), F32)]
            blk = c_ref[0, ab * i:ab * (i + 1), :] + jnp.concatenate(parts, axis=1)
            loc = jnp.dot(tri, blk, precision=HIGHEST, preferred_element_type=F32) + carry
            o_ref[0, ab * i:ab * (i + 1), :] = loc
            carry = loc[0:1, :]

    return pl.pallas_call(
        body, name="ssd_cumsum_bwd", grid=(nb_,),
        in_specs=[pl.BlockSpec((1, s, c), lambda b: (b, 0, 0)), pl.BlockSpec((1, N_GROUPS * 8, s), lambda b: (b, 0, 0))],
        out_specs=pl.BlockSpec((1, s, c), lambda b: (b, 0, 0)),
        out_shape=jax.ShapeDtypeStruct((nb_, s, c), F32),
        compiler_params=_params("parallel"),
    )(dcol, drow)


def _causal_ok(i, j):
    ab = SEQ_BLOCK
    r = lax.broadcasted_iota(jnp.int32, (ab, ab), 0)
    c = lax.broadcasted_iota(jnp.int32, (ab, ab), 1)
    return (r + (i - j) * ab) >= c


def _causal_ok_t(i, j):
    ab = SEQ_BLOCK
    r = lax.broadcasted_iota(jnp.int32, (ab, ab), 0)
    c = lax.broadcasted_iota(jnp.int32, (ab, ab), 1)
    return (c + (i - j) * ab) >= r


def _ssd_chunk(s_in, x, bm_t, cm, cb, acol, arow, a_prev, ok):
    q = x.shape[0]
    decay = jnp.exp(jnp.where(ok, acol - arow, NEG))
    y = jnp.dot(_bf(cb * decay), x, preferred_element_type=F32)
    y = y + jnp.exp(acol - a_prev) * jnp.dot(cm, _bf(s_in), preferred_element_type=F32)
    a_end = acol[q - 1:q, :]
    wx = _bf(jnp.exp(a_end - acol) * x.astype(F32))
    s_out = jnp.exp(a_end - a_prev) * s_in + jnp.dot(bm_t, wx, preferred_element_type=F32)
    return y, s_out


def _ssd_specs(s):
    xblk = pl.BlockSpec((1, s, GROUP_LANES), lambda b, g: (b, 0, g))
    bblk = pl.BlockSpec((1, s, D_STATE), lambda b, g: (b, 0, g))
    cblk = pl.BlockSpec((1, s, D_STATE), lambda b, g: (b, 0, N_GROUPS + g))
    tblk = pl.BlockSpec((1, 8, s), lambda b, g: (b, (LANES // 8) * g, 0))
    return xblk, bblk, cblk, tblk


def _chunk_views(i, j, x_ref, ac_ref, at_ref):
    ab = SEQ_BLOCK
    sl = slice(ab * i, ab * (i + 1))
    hs = slice(HEAD_DIM * j, HEAD_DIM * (j + 1))
    a_prev = jnp.zeros((1, 1), F32) if i == 0 else ac_ref[0, ab * i - 1:ab * i, j:j + 1]
    return sl, hs, ac_ref[0, sl, j:j + 1], at_ref[0, j:j + 1, sl], a_prev


def _ssd_fwd_chunked(xdtg, bc, acum, acum_t):
    nb_, s, _ = xdtg.shape
    ab = SEQ_BLOCK
    hpg = HEADS_PER_GROUP

    def body(x_ref, b_ref, c_ref, ac_ref, at_ref, y_ref):
        ok = _causal_ok(0, 0)
        states = [jnp.zeros((D_STATE, HEAD_DIM), F32) for _ in range(hpg)]
        for i in range(s // ab):
            bm, cm = b_ref[0, ab * i:ab * (i + 1), :], c_ref[0, ab * i:ab * (i + 1), :]
            bm_t = bm.T
            cb = jnp.dot(cm, bm_t, preferred_element_type=F32)
            ys = []
            for j in range(hpg):
                sl, hs, acol, arow, a_prev = _chunk_views(i, j, x_ref, ac_ref, at_ref)
                y, states[j] = _ssd_chunk(states[j], x_ref[0, sl, hs], bm_t, cm, cb, acol, arow, a_prev, ok)
                ys.append(y)
            y_ref[0, sl, :] = jnp.concatenate(ys + [jnp.zeros((ab, GROUP_LANES - hpg * HEAD_DIM), F32)], axis=1)

    xblk, bblk, cblk, tblk = _ssd_specs(s)
    ablk = pl.BlockSpec((1, s, LANES), lambda b, g: (b, 0, g))
    return pl.pallas_call(
        body, name="ssd_fwd", grid=(nb_, N_GROUPS), in_specs=[xblk, bblk, cblk, ablk, tblk], out_specs=xblk,
        out_shape=jax.ShapeDtypeStruct((nb_, s, N_GROUPS * GROUP_LANES), F32),
        compiler_params=_params("parallel", "parallel"),
    )(xdtg, bc, bc, acum, acum_t)


def _ssd_bwd_chunked(xdtg, bc, acum, acum_t, dyg):
    nb_, s, _ = xdtg.shape
    ab = SEQ_BLOCK
    nblk = s // ab
    hpg = HEADS_PER_GROUP

    def body(x_ref, b_ref, c_ref, ac_ref, at_ref, dy_ref, dx_ref, db_ref, dc_ref, dac_ref, dar_ref, s_s):
        ok = _causal_ok(0, 0)
        dx_ref[...] = jnp.zeros_like(dx_ref)
        dac_ref[...] = jnp.zeros_like(dac_ref)
        dar_ref[...] = jnp.zeros_like(dar_ref)
        states = [jnp.zeros((D_STATE, HEAD_DIM), F32) for _ in range(hpg)]
        for i in range(nblk):
            bm_t = b_ref[0, ab * i:ab * (i + 1), :].T
            for j in range(hpg):
                sl, hs, acol, arow, a_prev = _chunk_views(i, j, x_ref, ac_ref, at_ref)
                s_s[hpg * i + j] = states[j]
                if i + 1 < nblk:
                    a_end = acol[ab - 1:ab, :]
                    wx = _bf(jnp.exp(a_end - acol) * x_ref[0, sl, hs].astype(F32))
                    states[j] = jnp.exp(a_end - a_prev) * states[j] + jnp.dot(bm_t, wx, preferred_element_type=F32)
        ok_t = _causal_ok_t(0, 0)
        last_row = lax.broadcasted_iota(jnp.int32, (ab, 1), 0) == ab - 1
        d_state = [jnp.zeros((D_STATE, HEAD_DIM), F32) for _ in range(hpg)]
        pending = [jnp.zeros((1, 1), F32) for _ in range(hpg)]
        total = lambda v: jnp.sum(v, keepdims=True)
        for i in reversed(range(nblk)):
            bm, cm = b_ref[0, ab * i:ab * (i + 1), :], c_ref[0, ab * i:ab * (i + 1), :]
            cm_t = cm.T
            cbt = jnp.dot(bm, cm_t, preferred_element_type=F32)
            dcbt = jnp.zeros((ab, ab), F32)
            d_bm, d_cm = jnp.zeros((ab, D_STATE), F32), jnp.zeros((ab, D_STATE), F32)
            for j in range(hpg):
                sl, hs, acol, arow, a_prev = _chunk_views(i, j, x_ref, ac_ref, at_ref)
                x, dy = x_ref[0, sl, hs], dy_ref[0, sl, hs]
                dy16 = _bf(dy)
                s_in, g_out = s_s[hpg * i + j], d_state[j]
                s16, g16 = _bf(s_in), _bf(g_out)
                decay = jnp.exp(jnp.where(ok_t, arow - acol, NEG))
                gt = cbt * decay
                dgt = lax.dot_general(x, dy16, _NT, preferred_element_type=F32)
                d_x = jnp.dot(_bf(gt), dy16, preferred_element_type=F32)
                dcbt = dcbt + dgt * decay
                mm = dgt * gt
                d_arow = jnp.sum(mm, axis=0, keepdims=True)
                d_acol = -jnp.sum(mm, axis=1, keepdims=True)
                e = jnp.exp(acol - a_prev)
                edy16 = _bf(e * dy)
                d_cm = d_cm + lax.dot_general(edy16, s16, _NT, preferred_element_type=F32)
                d_s = jnp.dot(cm_t, edy16, preferred_element_type=F32)
                de_e = jnp.sum(dy * jnp.dot(cm, s16, preferred_element_type=F32), axis=1, keepdims=True) * e
                a_end = acol[ab - 1:ab, :]
                w = jnp.exp(a_end - acol)
                f = jnp.exp(a_end - a_prev)
                x32 = x.astype(F32)
                bg = jnp.dot(bm, g16, preferred_element_type=F32)
                d_x = d_x + w * bg
                d_bm = d_bm + lax.dot_general(_bf(w * x32), g16, _NT, preferred_element_type=F32)
                dw_w = jnp.sum(bg * x32, axis=1, keepdims=True) * w
                df_f = total(g_out * s_in) * f
                d_end = total(dw_w) + df_f
                d_acol = d_acol + de_e - dw_w + jnp.where(last_row, d_end + pending[j], 0.0)
                pending[j] = -total(de_e) - df_f
                d_state[j] = d_s + f * g_out
                dx_ref[0, sl, hs] = d_x
                dac_ref[0, sl, j:j + 1] = d_acol
                dar_ref[0, j:j + 1, sl] = d_arow
            dcbt16 = _bf(dcbt)
            db_ref[0, ab * i:ab * (i + 1), :] = d_bm + jnp.dot(dcbt16, cm, preferred_element_type=F32)
            dc_ref[0, ab * i:ab * (i + 1), :] = d_cm + lax.dot_general(dcbt16, bm, _TN, preferred_element_type=F32)

    xblk, bblk, cblk, tblk = _ssd_specs(s)
    ablk = pl.BlockSpec((1, s, LANES), lambda b, g: (b, 0, g))
    return pl.pallas_call(
        body, name="ssd_bwd", grid=(nb_, N_GROUPS),
        in_specs=[xblk, bblk, cblk, ablk, tblk, xblk],
        out_specs=[xblk, bblk, bblk, ablk, pl.BlockSpec((1, 8, s), lambda b, g: (b, g, 0))],
        out_shape=[jax.ShapeDtypeStruct((nb_, s, N_GROUPS * GROUP_LANES), F32),
                   jax.ShapeDtypeStruct((nb_, s, N_GROUPS * D_STATE), F32),
                   jax.ShapeDtypeStruct((nb_, s, N_GROUPS * D_STATE), F32),
                   jax.ShapeDtypeStruct((nb_, s, N_GROUPS * LANES), F32),
                   jax.ShapeDtypeStruct((nb_, N_GROUPS * 8, s), F32)],
        scratch_shapes=[pltpu.VMEM((nblk * hpg, D_STATE, HEAD_DIM), F32)],
        compiler_params=_params("parallel", "parallel"),
    )(xdtg, bc, bc, acum, acum_t, dyg)


def _interleave(wg, wu):
    k, f = wg.shape
    gi = GATE_UP_INTERLEAVE
    return jnp.stack([wg.reshape(k, f // gi, gi), wu.reshape(k, f // gi, gi)], axis=2).reshape(k, 2 * f)


def _head_expanders():
    e_x = np.zeros((LANES, N_GROUPS * GROUP_LANES), np.float32)
    e_a = np.zeros((LANES, N_GROUPS * LANES), np.float32)
    for h in range(N_HEADS):
        g, j = divmod(h, HEADS_PER_GROUP)
        e_x[h, GROUP_LANES * g + HEAD_DIM * j:GROUP_LANES * g + HEAD_DIM * (j + 1)] = 1.0
        e_a[h, LANES * g + j] = 1.0
    return jnp.asarray(e_x), jnp.asarray(e_a)


def _pad_lanes(v, n=LANES):
    return jnp.pad(v, ((0, 0), (0, n - v.shape[1])))


def _local_step(x, positions, target, w, late_job=None, late_weights=None, early_grad_job=None):
    nb, s, d = x.shape
    t = nb * s
    x2 = x.reshape(t, d)
    tgt2 = target.reshape(t, d)

    x16 = _bf(x2)
    wgu1 = _interleave(w["ffn1_gate"], w["ffn1_up"])
    ffn1 = _ffn_fwd("ffn1_fwd", x16, x2, wgu1, w["ffn1_down"], w["ln1_g"], w["ln1_b"], carry=late_job)
    au1, hm1, h1, r1, h1_16 = ffn1[:5]
    if late_job is not None:
        w = {**w, **late_weights(ffn1[5])}

    wgu2 = _interleave(w["ffn2_gate"], w["ffn2_up"])
    w_in = w["w_in"]
    wqk, wv, wz = w_in[:, :2 * D_ATTN], w_in[:, 2 * D_ATTN:3 * D_ATTN], w_in[:, 3 * D_ATTN:3 * D_ATTN + D_SSD]
    wxbc = w_in[:, 3 * D_ATTN + D_SSD:3 * D_ATTN + D_SSD + D_CONV]
    wdt = _pad_lanes(w_in[:, 3 * D_ATTN + D_SSD + D_CONV:])

    inv_freq = ROPE_THETA ** (-jnp.arange(0, ROPE_DIM, 2, dtype=F32) / ROPE_DIM)
    half = ROPE_DIM // 2
    head_invf = jnp.concatenate([inv_freq, inv_freq, jnp.zeros((HEAD_DIM - ROPE_DIM,), F32)])
    head_sgn = jnp.concatenate([-jnp.ones((half,), F32), jnp.ones((half,), F32), jnp.zeros((HEAD_DIM - ROPE_DIM,), F32)])
    invf = jnp.tile(head_invf, LANES // HEAD_DIM)[None, :]
    sgn = jnp.tile(head_sgn, LANES // HEAD_DIM)[None, :]
    posf = positions.astype(F32).reshape(t, 1)
    bias_fwd, bias_bwd = _branch_bias_table(s, FWD_KEY_BLOCK), _branch_bias_table(s, SEQ_BLOCK)
    e_x, e_a = _head_expanders()
    dtb, alog = _pad_lanes(w["dt_bias"]), _pad_lanes(w["a_log"])
    dskip = jnp.repeat(w["d_skip"], HEAD_DIM, axis=1)

    q16, k16, v16, z, xbc_pre, dtp, cs = _proj_in(h1_16, _pad_lanes(w_in, w_in.shape[1] - N_HEADS + LANES), posf, invf, sgn)
    to3 =lambda a: a.reshape(nb, s, a.shape[-1])
    attn_o, lse = _attn_fwd(to3(q16), to3(k16), to3(v16), bias_fwd)

    xbc = _conv_fwd(to3(xbc_pre), w["conv_w"], w["conv_b"]).reshape(t, D_CONV)
    xdtg, bc16, dag = _ssd_prep_fwd(xbc, dtp, dtb, alog, e_x, e_a)
    acum, acum_t = _cumsum_fwd(to3(dag))
    yg = _ssd_fwd_chunked(to3(xdtg), to3(bc16), acum, acum_t)

    cat = _norms_fwd(attn_o.reshape(t, D_ATTN), yg.reshape(t, -1), xbc, z, w["attn_norm_w"], w["ssd_norm_w"], dskip)
    h2, r2, h2_16 = _mm_res_ln("w_out_ln2", cat, w["w_out"], h1, w["ln2_g"], w["ln2_b"], scale=1.0)

    au2, hm2, _, r3, _ = _ffn_fwd("ffn2_fwd", h2_16, h2, wgu2, w["ffn2_down"], w["ln3_g"], w["ln3_b"])

    g = {}
    dr3, dr3_16, g["ln3_g"], g["ln3_b"], loss = _ln_loss_bwd("loss_ln3_bwd", r3, w["ln3_g"], w["ln3_b"], tgt2)

    dau2, dh2 = _ffn_bwd("ffn2_bwd", dr3_16, dr3, w["ffn2_down"].T, au2, wgu2.T)
    g["ffn2_down"] = _mm_tn("ffn2_down_dw", hm2, dr3_16, scale=0.5, tk=D_FF // 2, tn=512)
    g["ffn2_gate"], g["ffn2_up"] = _mm_tn_gate_up("ffn2_up_dw", h2_16, dau2)

    dr2, dr2_16, g["ln2_g"], g["ln2_b"] = _ln_bwd("ln2_bwd", r2, w["ln2_g"], w["ln2_b"], dh2)
    dcat = _mm("w_out_dx", [(dr2_16, w["w_out"].T)], tn=768)
    g["w_out"] = _mm_tn("w_out_dw", cat, dr2_16, tk=768, tn=1024)

    d_attn, dyg, dxs_a, dz16, g["attn_norm_w"], g["ssd_norm_w"], ddskip = _norms_bwd(
        attn_o.reshape(t, D_ATTN), yg.reshape(t, -1), xbc, z, w["attn_norm_w"], w["ssd_norm_w"], dskip, dcat)
    g["d_skip"] = ddskip.reshape(N_HEADS, HEAD_DIM).sum(axis=1)[None, :]

    dq, dk, dv16 = _attn_bwd(to3(q16), to3(k16), to3(v16), attn_o, to3(d_attn), lse, bias_bwd)
    dqk16 = _rope_bwd(dq.reshape(t, D_ATTN), dk.reshape(t, D_ATTN), cs)

    dxdtg, dbm, dcm, dacol, darow = _ssd_bwd_chunked(to3(xdtg), to3(bc16), acum, acum_t, to3(dyg))
    ddag = _cumsum_bwd(dacol, darow)
    dxbc, ddtp16, ddtb, dalog = _ssd_prep_bwd(xbc, dtp, dtb, alog, e_x, e_a, dxdtg.reshape(t, -1), ddag.reshape(t, -1),
                                               dxs_a, dbm.reshape(t, -1), dcm.reshape(t, -1))
    g["dt_bias"], g["a_log"] = ddtb[:, :N_HEADS], dalog[:, :N_HEADS]
    dxbc_pre16, dconv_w, g["conv_b"] = _conv_bwd(to3(xbc_pre), w["conv_w"], w["conv_b"], to3(dxbc))
    g["conv_w"] = dconv_w[:CONV_WIDTH]
    dxbc_pre16 = dxbc_pre16.reshape(t, D_CONV)
    dv16 = dv16.reshape(t, D_ATTN)

    dh1 = _mm("w_in_dx", [(dqk16, wqk.T), (dv16, wv.T), (dz16, wz.T), (dxbc_pre16, wxbc.T), (ddtp16, wdt.T)],
              res=dr2, res_scale=ALPHA)
    g["w_in"] = _mm_tn_sections("w_in_dw", h1_16, [dqk16, dv16, dz16, dxbc_pre16, ddtp16])[:, :w_in.shape[1]]

    dr1, dr1_16, g["ln1_g"], g["ln1_b"] = _ln_bwd("ln1_bwd", r1, w["ln1_g"], w["ln1_b"], dh1)
    g["ffn1_down"] = _mm_tn("ffn1_down_dw", hm1, dr1_16, scale=0.5, tk=D_FF // 2, tn=512)
    ffn1b = _ffn_bwd("ffn1_bwd", dr1_16, dr1, w["ffn1_down"].T, au1, wgu1.T,
                     carry=None if early_grad_job is None else early_grad_job(g))
    dau1, dx = ffn1b[:2]
    early = ffn1b[2] if early_grad_job is not None else None
    g["ffn1_gate"], g["ffn1_up"] = _mm_tn_gate_up("ffn1_up_dw", x16, dau1)
    return loss, dx.reshape(nb, s, d), g, early


_HBM = pl.BlockSpec(memory_space=pltpu.HBM)
N_CHIPS = 4
N_DEVICES = 8


def _place():
    return lax.axis_index("x"), lax.axis_index("y"), lax.axis_index("c")


def _other_chips(x, y):
    return [(1 - x, y), (x, 1 - y), (1 - x, 1 - y)]


class _GatherJob:
    def __init__(self, shards):
        assert all((a.shape[0] // 2) % 16 == 0 for a in shards)
        self.n = len(shards)
        self.shapes = [a.shape for a in shards]
        self.operands = [a.reshape(2, a.shape[0] // 2, a.shape[1]) for a in shards]
        self.out_shape = [jax.ShapeDtypeStruct((N_CHIPS,) + a.shape, a.dtype) for a in self.operands]
        pair = pltpu.SemaphoreType.DMA((self.n, N_CHIPS - 1))
        self.scratch_shapes = [pair, pair, pair, pair]

    def results(self, outs):
        return [o.reshape((N_CHIPS,) + s) for o, s in zip(outs, self.shapes)]

    def phases(self, ins, outs, sems):
        n = self.n
        send_sems, recv_sems, fwd_send_sems, fwd_recv_sems = sems
        x, y, c = _place()
        me = 2 * x + y
        peers = _other_chips(x, y)

        def ici(t, p, src_chip):
            px, py = peers[p]
            return pltpu.make_async_remote_copy(
                ins[t].at[c] if src_chip is None else outs[t].at[src_chip, c],
                outs[t].at[me if src_chip is None else src_chip, c],
                send_sems.at[t, p], recv_sems.at[t, p], device_id=(px, py, c), device_id_type=MESH)

        def d2d(t, p, core):
            px, py = peers[p]
            return pltpu.make_async_remote_copy(
                outs[t].at[2 * px + py, core], outs[t].at[2 * px + py, core],
                fwd_send_sems.at[t, p], fwd_recv_sems.at[t, p], device_id=(x, y, 1 - c), device_id_type=MESH)

        pairs = [(t, p) for t in range(n) for p in range(N_CHIPS - 1)]

        def start():
            for t, p in pairs:
                ici(t, p, None).start()

        def forward():
            for t, p in pairs:
                px, py = peers[p]
                ici(t, p, 2 * px + py).wait_recv()
                d2d(t, p, c).start()

        def finish():
            for t, p in pairs:
                d2d(t, p, 1 - c).wait_recv()
            for t, p in pairs:
                ici(t, p, None).wait_send()
                d2d(t, p, c).wait_send()

        return start, forward, finish


class _ExchangeJob:
    def __init__(self, stacks):
        self.n = len(stacks)
        self.operands = list(stacks)
        self.out_shape = [jax.ShapeDtypeStruct(a.shape, a.dtype) for a in stacks]
        pair = pltpu.SemaphoreType.DMA((self.n, N_CHIPS - 1))
        self.scratch_shapes = [pair, pair]

    def results(self, outs):
        return list(outs)

    def phases(self, ins, outs, sems):
        send_sems, recv_sems = sems
        x, y, c = _place()
        me = 2 * x + y
        peers = _other_chips(x, y)
        pairs = [(t, p) for t in range(self.n) for p in range(N_CHIPS - 1)]

        def copy(t, p):
            px, py = peers[p]
            return pltpu.make_async_remote_copy(ins[t].at[2 * px + py], outs[t].at[me], send_sems.at[t, p],
                                                recv_sems.at[t, p], device_id=(px, py, c), device_id_type=MESH)

        def arrival(t, p):
            px, py = peers[p]
            return pltpu.make_async_remote_copy(ins[t].at[me], outs[t].at[2 * px + py], send_sems.at[t, p],
                                                recv_sems.at[t, p], device_id=(px, py, c), device_id_type=MESH)

        def start():
            for t, p in pairs:
                copy(t, p).start()

        def finish():
            for t, p in pairs:
                arrival(t, p).wait_recv()
            for t, p in pairs:
                copy(t, p).wait_send()

        return start, None, finish


def _run_job(job, name):
    n = job.n

    def body(*refs):
        for phase in job.phases(refs[:n], refs[n:2 * n], refs[2 * n:]):
            if phase is not None:
                phase()

    outs = pl.pallas_call(
        body, name=name, in_specs=[_HBM] * n, out_specs=[_HBM] * n,
        out_shape=job.out_shape, scratch_shapes=job.scratch_shapes,
    )(*job.operands)
    return job.results(outs)


def _sibling_halves(stacks, name):
    n = len(stacks)
    halves = [a.shape[1] // 2 for a in stacks]
    split = [a.reshape(a.shape[0], 2, h, a.shape[2]) for a, h in zip(stacks, halves)]

    def body(*refs):
        ins, outs = refs[:n], refs[n:2 * n]
        send_sems, recv_sems = refs[2 * n:]
        x, y, c = _place()
        cps = []
        for t in range(n):
            cp = pltpu.make_async_remote_copy(ins[t].at[:, 1 - c], outs[t], send_sems.at[t], recv_sems.at[t],
                                              device_id=(x, y, 1 - c), device_id_type=MESH)
            cp.start()
            cps.append(cp)
        for cp in cps:
            cp.wait()

    return pl.pallas_call(
        body, name=name,
        in_specs=[_HBM] * n, out_specs=[_HBM] * n,
        out_shape=[jax.ShapeDtypeStruct((a.shape[0], h, a.shape[2]), a.dtype) for a, h in zip(stacks, halves)],
        scratch_shapes=[pltpu.SemaphoreType.DMA((n,)), pltpu.SemaphoreType.DMA((n,))],
    )(*split)


def _sibling_swap(arrs):
    n = len(arrs)

    def body(*refs):
        ins, outs = refs[:n], refs[n:2 * n]
        send_sems, recv_sems = refs[2 * n:]
        x, y, c = _place()
        cps = []
        for t in range(n):
            cp = pltpu.make_async_remote_copy(ins[t], outs[t], send_sems.at[t], recv_sems.at[t],
                                              device_id=(x, y, 1 - c), device_id_type=MESH)
            cp.start()
            cps.append(cp)
        for cp in cps:
            cp.wait()

    return pl.pallas_call(
        body, name="sibling_swap",
        in_specs=[_HBM] * n, out_specs=[_HBM] * n,
        out_shape=[jax.ShapeDtypeStruct(a.shape, a.dtype) for a in arrs],
        scratch_shapes=[pltpu.SemaphoreType.DMA((n,)), pltpu.SemaphoreType.DMA((n,))],
    )(*arrs)


def _half_sum(name, own, other, core):
    k, r, cols = own.shape
    h = r // 2
    tr = next(cand for cand in (128, 176, 64, 32, 16) if h % cand == 0)
    nblk = h // tr

    def body(core_ref, own_ref, other_ref, o_ref):
        o_ref[...] = _bf(own_ref[...] + other_ref[...].astype(F32))

    grid_spec = pltpu.PrefetchScalarGridSpec(
        num_scalar_prefetch=1, grid=(nblk,),
        in_specs=[pl.BlockSpec((k, tr, cols), lambda i, core_ref: (0, i + core_ref[0] * nblk, 0)),
                  pl.BlockSpec((k, tr, cols), lambda i, core_ref: (0, i, 0))],
        out_specs=pl.BlockSpec((k, tr, cols), lambda i, core_ref: (0, i, 0)))
    return pl.pallas_call(
        body, name=name, grid_spec=grid_spec, out_shape=jax.ShapeDtypeStruct((k, h, cols), BF16),
        compiler_params=_params("parallel"),
    )(core.reshape(1).astype(jnp.int32), own, other)


def _small_allreduce(v):
    r = v.shape[0]

    def body(v_ref, tot_ref, slots, send_sems, recv_sems):
        x, y, c = _place()
        me = 4 * x + 2 * y + c
        slots[me] = v_ref[...]
        cps, peers = [], []
        for k in range(1, N_DEVICES):
            px = 1 - x if (k >> 2) & 1 else x
            py = 1 - y if (k >> 1) & 1 else y
            pc = 1 - c if k & 1 else c
            cp = pltpu.make_async_remote_copy(v_ref, slots.at[me], send_sems.at[k - 1], recv_sems.at[k - 1],
                                              device_id=(px, py, pc), device_id_type=MESH)
            cp.start()
            cps.append(cp)
            peers.append((px, py, pc))
        for k, (px, py, pc) in enumerate(peers):
            pltpu.make_async_remote_copy(v_ref, slots.at[4 * px + 2 * py + pc], send_sems.at[k], recv_sems.at[k],
                                         device_id=(px, py, pc), device_id_type=MESH).wait_recv()
        for cp in cps:
            cp.wait_send()
        acc = slots[0]
        for s in range(1, N_DEVICES):
            acc = acc + slots[s]
        tot_ref[...] = acc

    return pl.pallas_call(
        body, name="small_allreduce",
        in_specs=[pl.BlockSpec(memory_space=pltpu.VMEM)], out_specs=pl.BlockSpec(memory_space=pltpu.VMEM),
        out_shape=jax.ShapeDtypeStruct((r, LANES), F32),
        scratch_shapes=[pltpu.VMEM((N_DEVICES, r, LANES), F32), pltpu.SemaphoreType.DMA((N_DEVICES - 1,)),
                        pltpu.SemaphoreType.DMA((N_DEVICES - 1,))],
    )(v)


def _elementwise(name, fn, ins, out_dtypes):
    r, c = ins[0].shape[-2:]
    tr = next((cand for cand in (256, 176, 128, 64, 32, 16) if r % cand == 0), r)
    nin = len(ins)

    def body(*refs):
        outs = fn(*[ref[...] for ref in refs[:nin]])
        for o_ref, o in zip(refs[nin:], outs):
            o_ref[...] = o.astype(o_ref.dtype)

    in_specs = [pl.BlockSpec((tr, c), lambda i: (i, 0)) if a.ndim == 2 else pl.BlockSpec((a.shape[0], tr, c), lambda i: (0, i, 0))
                for a in ins]
    return pl.pallas_call(
        body, name=name, grid=(r // tr,), in_specs=in_specs,
        out_specs=[pl.BlockSpec((tr, c), lambda i: (i, 0)) for _ in out_dtypes],
        out_shape=[jax.ShapeDtypeStruct((r, c), dt) for dt in out_dtypes],
        compiler_params=_params("parallel"),
    )(*ins)


def _row_tile(rows):
    return next((cand for cand in (128, 176, 64, 32, 16) if rows % cand == 0), rows)


def _sum_slots(name, received, own, chip):
    _, r, cols = own.shape
    tr = _row_tile(r)

    def body(chip_ref, own_ref, a_ref, b_ref, c_ref, o_ref):
        o_ref[...] = ((own_ref[0].astype(F32) + a_ref[0].astype(F32)) + b_ref[0].astype(F32)) + c_ref[0].astype(F32)

    def slot(flip):
        return pl.BlockSpec((1, tr, cols), lambda i, chip_ref: (jnp.bitwise_xor(chip_ref[0], flip), i, 0))

    grid_spec = pltpu.PrefetchScalarGridSpec(
        num_scalar_prefetch=1, grid=(r // tr,), in_specs=[slot(0), slot(1), slot(2), slot(3)],
        out_specs=pl.BlockSpec((tr, cols), lambda i, chip_ref: (i, 0)))
    return pl.pallas_call(
        body, name=name, grid_spec=grid_spec, out_shape=jax.ShapeDtypeStruct((r, cols), F32),
        compiler_params=_params("parallel"),
    )(chip.reshape(1).astype(jnp.int32), own, received, received, received)


def _adamw_halves(name, mine, theirs, core, w, m, v):
    h, cols = mine.shape
    tr = _row_tile(h)
    nh = h // tr

    def body(core_ref, mine_ref, theirs_ref, w_ref, m_ref, v_ref, g_ref, d_ref, m2_ref, v2_ref):
        is_mine = (pl.program_id(0) // nh) == core_ref[0]
        g = jnp.where(is_mine, mine_ref[...], theirs_ref[...])
        outs = _adamw_math(g, w_ref[...], m_ref[...], v_ref[...])
        for ref, val in zip((g_ref, d_ref, m2_ref, v2_ref), outs):
            ref[...] = val

    half = pl.BlockSpec((tr, cols), lambda i, core_ref: (i % nh, 0))
    full = pl.BlockSpec((tr, cols), lambda i, core_ref: (i, 0))
    grid_spec = pltpu.PrefetchScalarGridSpec(
        num_scalar_prefetch=1, grid=(2 * nh,), in_specs=[half, half, full, full, full], out_specs=[full] * 4)
    return pl.pallas_call(
        body, name=name, grid_spec=grid_spec, out_shape=[jax.ShapeDtypeStruct((2 * h, cols), F32)] * 4,
        compiler_params=_params("parallel"),
    )(core.reshape(1).astype(jnp.int32), mine, theirs, w, m, v)


def _adamw_math(g, w_v, m_v, v_v):
    m2 = ADAM_B1 * m_v + (1.0 - ADAM_B1) * g
    v2 = ADAM_B2 * v_v + (1.0 - ADAM_B2) * jnp.square(g)
    m_hat = m2 / (1.0 - ADAM_B1 ** ADAM_STEP)
    v_hat = v2 / (1.0 - ADAM_B2 ** ADAM_STEP)
    delta = -ADAM_LR * (m_hat / (jnp.sqrt(v_hat) + ADAM_EPS) + ADAM_WD * w_v)
    return [g, delta, m2, v2]


def _adamw(name, g, w, m, v):
    return _elementwise(name, _adamw_math, [g, w, m, v], [F32] * 4)


_MATRICES = (("ffn1_gate", 1), ("ffn1_up", 1), ("ffn1_down", 0), ("w_in", 1), ("w_out", 0),
             ("ffn2_gate", 1), ("ffn2_up", 1), ("ffn2_down", 0))
_VECTORS = ("ln1_g", "ln1_b", "conv_b", "dt_bias", "a_log", "d_skip", "attn_norm_w", "ssd_norm_w",
            "ln2_g", "ln2_b", "ln3_g", "ln3_b")
_WEIGHT_ORDER = ("ln1_g", "ln1_b", "ffn1_gate", "ffn1_up", "ffn1_down", "w_in", "conv_w", "conv_b", "dt_bias", "a_log",
                 "d_skip", "attn_norm_w", "ssd_norm_w", "w_out", "ln2_g", "ln2_b", "ffn2_gate", "ffn2_up", "ffn2_down",
                 "ln3_g", "ln3_b")


def _pack_rows(vectors):
    parts = []
    for vec in vectors:
        flat = vec.reshape(-1)
        parts.append(jnp.pad(flat, (0, (-flat.shape[0]) % LANES)))
    flat = jnp.concatenate(parts)
    flat = jnp.pad(flat, (0, (-flat.shape[0]) % (8 * LANES)))
    return flat.reshape(-1, LANES)


def _unpack_rows(packed, shapes):
    flat = packed.reshape(-1)
    out, off = [], 0
    for shape in shapes:
        size = int(np.prod(shape))
        out.append(flat[off:off + size].reshape(shape))
        off += size + (-size) % LANES
    return out


def _assemble(stack, own, chip, axis):
    blocks = [jnp.where(chip == s, own, stack[s]) for s in range(N_CHIPS)]
    return jnp.concatenate(blocks, axis=axis)


def _split(full, axis):
    if axis == 0:
        return full.reshape(N_CHIPS, -1, full.shape[1])
    cols = full.shape[1] // N_CHIPS
    return jnp.stack([full[:, cols * s:cols * (s + 1)] for s in range(N_CHIPS)])


def kernel(x, positions, ln1_g, ln1_b, ffn1_gate, ffn1_up, ffn1_down, w_in, conv_w, conv_b, dt_bias, a_log, d_skip, attn_norm_w, ssd_norm_w, w_out, ln2_g, ln2_b, ffn2_gate, ffn2_up, ffn2_down, ln3_g, ln3_b, loss_target, m_ln1_g, m_ln1_b, m_ffn1_gate, m_ffn1_up, m_ffn1_down, m_w_in, m_conv_w, m_conv_b, m_dt_bias, m_a_log, m_d_skip, m_attn_norm_w, m_ssd_norm_w, m_w_out, m_ln2_g, m_ln2_b, m_ffn2_gate, m_ffn2_up, m_ffn2_down, m_ln3_g, m_ln3_b, v_ln1_g, v_ln1_b, v_ffn1_gate, v_ffn1_up, v_ffn1_down, v_w_in, v_conv_w, v_conv_b, v_dt_bias, v_a_log, v_d_skip, v_attn_norm_w, v_ssd_norm_w, v_w_out, v_ln2_g, v_ln2_b, v_ffn2_gate, v_ffn2_up, v_ffn2_down, v_ln3_g, v_ln3_b):
    given = dict(locals())
    wts = {n: given[n] for n in _WEIGHT_ORDER}
    mom_m = {n: given["m_" + n] for n in _WEIGHT_ORDER}
    mom_v = {n: given["v_" + n] for n in _WEIGHT_ORDER}
    chip = 2 * lax.axis_index("x") + lax.axis_index("y")

    core = lax.axis_index("c")
    first = [(n, axis) for n, axis in _MATRICES if n.startswith("ffn1")]
    rest = [(n, axis) for n, axis in _MATRICES if not n.startswith("ffn1")]
    own16 = {n: wts[n][0].astype(BF16) for n, _ in _MATRICES}
    gathered = _run_job(_GatherJob([own16[n] for n, _ in first]), "gather_ffn1")
    full = {n: _assemble(st, own16[n], chip, axis) for (n, axis), st in zip(first, gathered)}
    for n in _VECTORS:
        full[n] = wts[n]
    conv_rows = jnp.pad(wts["conv_w"][0], ((0, 32 - CONV_WIDTH), (0, 0)))
    late_job = _GatherJob([own16[n] for n, _ in rest] + [conv_rows])

    def late_weights(results):
        out = {n: _assemble(st, own16[n], chip, axis) for (n, axis), st in zip(rest, results)}
        out["conv_w"] = _assemble(results[-1], conv_rows, chip, 1)[:CONV_WIDTH]
        return out

    chip_sums = {}

    def core_sums(g, which, tag):
        partials = [_split(g[n], axis) for n, axis in which]
        from_sibling = _sibling_halves([p.astype(BF16) for p in partials], "sibling_halves_" + tag)
        for (n, _), p, o in zip(which, partials, from_sibling):
            chip_sums[n] = _half_sum("core_sum_" + n, p, o, core)
        return _ExchangeJob([chip_sums[n] for n, _ in which])

    last = [(n, axis) for n, axis in _MATRICES if n in ("ffn1_gate", "ffn1_up")]
    early = [(n, axis) for n, axis in _MATRICES if (n, axis) not in last]
    loss, grad_x, g, received_early = _local_step(x, positions, loss_target, full, late_job, late_weights,
                                                  lambda g_now: core_sums(g_now, early, "early"))
    received_last = _run_job(core_sums(g, last, "last"), "exchange_last")
    received = dict(zip([n for n, _ in last + early], received_last + received_early))
    half_totals = [_sum_slots("sum_partials_" + n, received[n], chip_sums[n], chip) for n, _ in _MATRICES]
    other_halves = _sibling_swap(half_totals)

    small_shapes = [g[n].shape for n in _VECTORS] + [g["conv_w"].shape, (1,)]
    total = _small_allreduce(_pack_rows([g[n] for n in _VECTORS] + [g["conv_w"], loss[0, :1]]))
    small = _unpack_rows(total, small_shapes)
    loss_out = small[-1].reshape(())

    grads, deltas, new_m, new_v = {}, {}, {}, {}
    for (n, _), mine, theirs in zip(_MATRICES, half_totals, other_halves):
        res = _adamw_halves("adamw_" + n, mine, theirs, core, wts[n][0], mom_m[n][0], mom_v[n][0])
        grads[n], deltas[n], new_m[n], new_v[n] = [r[None] for r in res]

    vec_shapes = [wts[n].shape for n in _VECTORS]
    res = _adamw("adamw_vectors", _pack_rows(small[:len(_VECTORS)]), _pack_rows([wts[n] for n in _VECTORS]),
                 _pack_rows([mom_m[n] for n in _VECTORS]), _pack_rows([mom_v[n] for n in _VECTORS]))
    for dst, packed in zip((grads, deltas, new_m, new_v), res):
        for n, val in zip(_VECTORS, _unpack_rows(packed, vec_shapes)):
            dst[n] = val

    cols = conv_w.shape[2]
    g_conv = lax.dynamic_slice_in_dim(small[len(_VECTORS)], chip * cols, cols, axis=1)
    res = _adamw("adamw_conv_w", g_conv, wts["conv_w"][0], mom_m["conv_w"][0], mom_v["conv_w"][0])
    grads["conv_w"], deltas["conv_w"], new_m["conv_w"], new_v["conv_w"] = [r[None] for r in res]

    return (loss_out, grad_x, *[grads[n] for n in _WEIGHT_ORDER], *[deltas[n] for n in _WEIGHT_ORDER],
            *[new_m[n] for n in _WEIGHT_ORDER], *[new_v[n] for n in _WEIGHT_ORDER])
```

```python
import functools

import numpy as np
import jax
import jax.numpy as jnp
from jax import lax
from jax.experimental import pallas as pl
from jax.experimental.pallas import tpu as pltpu

F32, BF16 = jnp.float32, jnp.bfloat16

D_MODEL = 1024
D_FF = 2816
N_HEADS = 12
HEAD_DIM = 64
D_ATTN = 768
D_SSD = 768
N_GROUPS = 4
HEADS_PER_GROUP = 3
D_STATE = 128
D_CONV = 1792
CONV_WIDTH = 4
ROPE_DIM = 16
ROPE_THETA = 500000.0
ALPHA = 2.0 ** 0.25
LN_EPS = 1e-5
RMS_EPS = 1e-6
ADAM_LR, ADAM_B1, ADAM_B2, ADAM_EPS, ADAM_WD, ADAM_STEP = 0.001, 0.9, 0.999, 1e-08, 0.01, 10

LANES = 128
GATE_UP_INTERLEAVE = 256
SEQ_BLOCK = 256
GROUP_LANES = 256
VMEM_LIMIT = 56 * 1024 * 1024
NEG = -1e30
MESH = pl.DeviceIdType.MESH
HIGHEST = lax.Precision.HIGHEST

_NT = (((1,), (1,)), ((), ()))
_TN = (((0,), (0,)), ((), ()))


def _params(*sem):
    return pltpu.CompilerParams(dimension_semantics=sem, vmem_limit_bytes=VMEM_LIMIT)


def _bf(v):
    return v.astype(BF16)


EPILOGUE_ROWS = 128


def _row_chunks(tm):
    return [slice(r, min(r + EPILOGUE_ROWS, tm)) for r in range(0, tm, EPILOGUE_ROWS)]


def _sigmoid(v):
    return 0.5 * jnp.tanh(0.5 * v) + 0.5


def _mm(name, pairs, *, scale=1.0, res=None, res_scale=1.0, out_dtype=F32, tm=512, tn=512):
    m, n = pairs[0][0].shape[0], pairs[0][1].shape[1]
    tm, tn = min(tm, m), min(tn, n)
    assert m % tm == 0 and n % tn == 0, (name, m, n, tm, tn)
    npair = len(pairs)

    def body(*refs):
        acc = None
        for a_ref, b_ref in zip(refs[:npair], refs[npair:2 * npair]):
            d = jnp.dot(_bf(a_ref[...]), b_ref[...], preferred_element_type=F32)
            acc = d if acc is None else acc + d
        if scale != 1.0:
            acc = acc * scale
        if res is not None:
            acc = acc + res_scale * refs[2 * npair][...]
        refs[-1][...] = acc.astype(out_dtype)

    in_specs = [pl.BlockSpec((tm, a.shape[1]), lambda i, j: (i, 0)) for a, _ in pairs]
    in_specs += [pl.BlockSpec((b.shape[0], tn), lambda i, j: (0, j)) for _, b in pairs]
    args = [a for a, _ in pairs] + [b for _, b in pairs]
    if res is not None:
        in_specs.append(pl.BlockSpec((tm, tn), lambda i, j: (i, j)))
        args.append(res)
    return pl.pallas_call(
        body, name=name, grid=(m // tm, n // tn), in_specs=in_specs,
        out_specs=pl.BlockSpec((tm, tn), lambda i, j: (i, j)),
        out_shape=jax.ShapeDtypeStruct((m, n), out_dtype),
        compiler_params=_params("parallel", "parallel"),
    )(*args)


def _mm_tn(name, x, dy, *, scale=1.0, tk=512, tn=512, tt=1024):
    t, k = x.shape
    n = dy.shape[1]
    tk, tn, tt = min(tk, k), min(tn, n), min(tt, t)
    assert k % tk == 0 and n % tn == 0 and t % tt == 0, (name, k, n, t)
    nt = t // tt

    def body(x_ref, dy_ref, o_ref):
        step = pl.program_id(2)
        d = lax.dot_general(_bf(x_ref[...]), _bf(dy_ref[...]), _TN, preferred_element_type=F32)

        @pl.when(step == 0)
        def _():
            o_ref[...] = d

        @pl.when(step > 0)
        def _():
            o_ref[...] += d

        if scale != 1.0:
            @pl.when(step == nt - 1)
            def _():
                o_ref[...] = o_ref[...] * scale

    return pl.pallas_call(
        body, name=name, grid=(k // tk, n // tn, nt),
        in_specs=[pl.BlockSpec((tt, tk), lambda i, j, s: (s, i)), pl.BlockSpec((tt, tn), lambda i, j, s: (s, j))],
        out_specs=pl.BlockSpec((tk, tn), lambda i, j, s: (i, j)),
        out_shape=jax.ShapeDtypeStruct((k, n), F32),
        compiler_params=_params("parallel", "parallel", "arbitrary"),
    )(x, dy)


def _mm_tn_sections(name, x, dys, *, tt=512):
    t, k = x.shape
    tt = min(tt, t)
    cuts = np.cumsum([0] + [d.shape[1] for d in dys]).tolist()
    ns = len(dys)

    def body(*refs):
        x_ref, o_ref = refs[0], refs[1 + ns]
        step = pl.program_id(0)
        xt = x_ref[...].T
        parts = [jnp.dot(xt, refs[1 + a][...], preferred_element_type=F32) for a in range(ns)]

        @pl.when(step == 0)
        def _():
            for a in range(ns):
                o_ref[:, cuts[a]:cuts[a + 1]] = parts[a]

        @pl.when(step > 0)
        def _():
            for a in range(ns):
                o_ref[:, cuts[a]:cuts[a + 1]] += parts[a]

    return pl.pallas_call(
        body, name=name, grid=(t // tt,),
        in_specs=[pl.BlockSpec((tt, k), lambda s: (s, 0))] + [pl.BlockSpec((tt, d.shape[1]), lambda s: (s, 0)) for d in dys],
        out_specs=pl.BlockSpec((k, cuts[-1]), lambda s: (0, 0)),
        out_shape=jax.ShapeDtypeStruct((k, cuts[-1]), F32),
        compiler_params=_params("arbitrary"),
    )(x, *dys)


def _mm_tn_gate_up(name, x, dau, *, tt=1024):
    t, k = x.shape
    gi = GATE_UP_INTERLEAVE
    nj = dau.shape[1] // (2 * gi)
    tt = min(tt, t)
    nt = t // tt

    def body(x_ref, dy_ref, g_ref, u_ref):
        step = pl.program_id(1)
        d = lax.dot_general(_bf(x_ref[...]), dy_ref[...], _TN, preferred_element_type=F32)

        @pl.when(step == 0)
        def _():
            g_ref[...] = d[:, :gi]
            u_ref[...] = d[:, gi:]

        @pl.when(step > 0)
        def _():
            g_ref[...] += d[:, :gi]
            u_ref[...] += d[:, gi:]

    out = pl.BlockSpec((k, gi), lambda j, s: (0, j))
    return pl.pallas_call(
        body, name=name, grid=(nj, nt),
        in_specs=[pl.BlockSpec((tt, k), lambda j, s: (s, 0)), pl.BlockSpec((tt, 2 * gi), lambda j, s: (s, j))],
        out_specs=[out, out],
        out_shape=[jax.ShapeDtypeStruct((k, gi * nj), F32)] * 2,
        compiler_params=_params("parallel", "arbitrary"),
    )(x, dau)


def _carried(carry, ins, outs, sems, step, total):
    start, forward, finish = carry.phases(ins, outs, sems)
    pl.when(step == 0)(start)
    if forward is not None:
        pl.when(step == (3 * total) // 4)(forward)
    return lambda: pl.when(step == total - 1)(finish)


def _resident(shape):
    return pl.BlockSpec(shape, lambda i: (0,) * len(shape), pipeline_mode=pl.Buffered(1))


def _ffn_fwd(name, x16, res, wgu, wd, g, b, *, tm=512, carry=None):
    t, k = x16.shape
    gi = GATE_UP_INTERLEAVE
    nj, n, ni = wd.shape[0] // gi, wd.shape[1], t // tm
    nc = carry.n if carry is not None else 0

    def body(*refs):
        x_ref, res_ref, wgu_ref, wd_ref, g_ref, b_ref = refs[:6]
        au_ref, hm_ref, y_ref, r_ref, y16_ref = refs[6 + nc:11 + nc]
        if carry is not None:
            finish = _carried(carry, refs[6:6 + nc], refs[11 + nc:11 + 2 * nc], refs[11 + 2 * nc:], pl.program_id(0), ni)
        xv = x_ref[...]
        acc = jnp.zeros((tm, n), F32)
        for j in range(nj):
            au = jnp.dot(xv, wgu_ref[:, 2 * gi * j:2 * gi * (j + 1)], preferred_element_type=F32)
            a, u = au[:, :gi], au[:, gi:]
            au_ref[:, 2 * gi * j:2 * gi * (j + 1)] = _bf(au)
            hm = _bf(a * _sigmoid(a) * u)
            hm_ref[:, gi * j:gi * (j + 1)] = hm
            acc = acc + jnp.dot(hm, wd_ref[gi * j:gi * (j + 1), :], preferred_element_type=F32)
        r = ALPHA * res_ref[...] + 0.5 * acc
        r_ref[...] = r
        y = _layer_norm(r, g_ref[...], b_ref[...])
        y_ref[...] = y
        y16_ref[...] = _bf(y)
        if carry is not None:
            finish()

    row = lambda c: pl.BlockSpec((tm, c), lambda i: (i, 0))
    hbm = pl.BlockSpec(memory_space=pltpu.HBM)
    res_ = pl.pallas_call(
        body, name=name, grid=(ni,),
        in_specs=[row(k), row(n), _resident(wgu.shape), _resident(wd.shape), _resident(g.shape), _resident(b.shape)] + [hbm] * nc,
        out_specs=[row(2 * gi * nj), row(gi * nj), row(n), row(n), row(n)] + [hbm] * nc,
        out_shape=[jax.ShapeDtypeStruct((t, 2 * gi * nj), BF16), jax.ShapeDtypeStruct((t, gi * nj), BF16),
                   jax.ShapeDtypeStruct((t, n), F32), jax.ShapeDtypeStruct((t, n), F32), jax.ShapeDtypeStruct((t, n), BF16)]
        + (carry.out_shape if carry is not None else []),
        scratch_shapes=carry.scratch_shapes if carry is not None else [],
        compiler_params=_params("arbitrary" if carry is not None else "parallel"),
    )(x16, res, wgu, wd, g, b, *(carry.operands if carry is not None else []))
    return tuple(res_[:5]) + ((carry.results(res_[5:]),) if carry is not None else ())


def _ffn_bwd(name, dr16, dr, wdt, au, wgut, *, tm=512, carry=None):
    t, n = dr16.shape
    gi = GATE_UP_INTERLEAVE
    nj, ni = wdt.shape[1] // gi, t // tm
    nc = carry.n if carry is not None else 0

    def body(*refs):
        dr16_ref, dr_ref, wdt_ref, au_ref, wgut_ref = refs[:5]
        dau_ref, dx_ref = refs[5 + nc:7 + nc]
        if carry is not None:
            finish = _carried(carry, refs[5:5 + nc], refs[7 + nc:7 + 2 * nc], refs[7 + 2 * nc:], pl.program_id(0), ni)
        drv = dr16_ref[...]
        acc = jnp.zeros((tm, n), F32)
        for j in range(nj):
            dhm = jnp.dot(drv, wdt_ref[:, gi * j:gi * (j + 1)], preferred_element_type=F32) * 0.5
            au_v = au_ref[:, 2 * gi * j:2 * gi * (j + 1)].astype(F32)
            a, u = au_v[:, :gi], au_v[:, gi:]
            sig = _sigmoid(a)
            silu = a * sig
            dau = jnp.concatenate([_bf(dhm * u * (sig + silu - silu * sig)), _bf(dhm * silu)], axis=1)
            dau_ref[:, 2 * gi * j:2 * gi * (j + 1)] = dau
            acc = acc + jnp.dot(dau, wgut_ref[2 * gi * j:2 * gi * (j + 1), :], preferred_element_type=F32)
        dx_ref[...] = ALPHA * dr_ref[...] + acc
        if carry is not None:
            finish()

    row = lambda c: pl.BlockSpec((tm, c), lambda i: (i, 0))
    hbm = pl.BlockSpec(memory_space=pltpu.HBM)
    res_ = pl.pallas_call(
        body, name=name, grid=(ni,),
        in_specs=[row(n), row(n), _resident(wdt.shape), row(2 * gi * nj), _resident(wgut.shape)] + [hbm] * nc,
        out_specs=[row(2 * gi * nj), row(n)] + [hbm] * nc,
        out_shape=[jax.ShapeDtypeStruct((t, 2 * gi * nj), BF16), jax.ShapeDtypeStruct((t, n), F32)]
        + (carry.out_shape if carry is not None else []),
        scratch_shapes=carry.scratch_shapes if carry is not None else [],
        compiler_params=_params("arbitrary" if carry is not None else "parallel"),
    )(dr16, dr, wdt, au, wgut, *(carry.operands if carry is not None else []))
    return tuple(res_[:2]) + ((carry.results(res_[2:]),) if carry is not None else ())


def _layer_norm(r, g, b):
    mu = jnp.mean(r, axis=-1, keepdims=True)
    var = jnp.mean(jnp.square(r - mu), axis=-1, keepdims=True)
    return (r - mu) * lax.rsqrt(var + LN_EPS) * g + b


def _mm_res_ln(name, a, w, res, g, b, *, scale, tm=256):
    t, k = a.shape
    n = w.shape[1]

    def body(a_ref, w_ref, res_ref, g_ref, b_ref, y_ref, r_ref, y16_ref):
        for rows in _row_chunks(tm):
            r = ALPHA * res_ref[rows, :] + scale * jnp.dot(_bf(a_ref[rows, :]), w_ref[...], preferred_element_type=F32)
            r_ref[rows, :] = r
            y = _layer_norm(r, g_ref[...], b_ref[...])
            y_ref[rows, :] = y
            y16_ref[rows, :] = _bf(y)

    row = lambda c: pl.BlockSpec((tm, c), lambda i: (i, 0))
    const = lambda shape: pl.BlockSpec(shape, lambda i: (0, 0))
    return pl.pallas_call(
        body, name=name, grid=(t // tm,),
        in_specs=[row(k), const((k, n)), row(n), const((1, n)), const((1, n))],
        out_specs=[row(n), row(n), row(n)],
        out_shape=[jax.ShapeDtypeStruct((t, n), F32), jax.ShapeDtypeStruct((t, n), F32), jax.ShapeDtypeStruct((t, n), BF16)],
        compiler_params=_params("parallel"),
    )(a, w, res, g, b)


def _rowwise(name, fn, rows, consts, row_outs, acc_outs=(), tm=256):
    rows = [r if isinstance(r, tuple) else (r, r.shape[1]) for r in rows]
    t = rows[0][0].shape[0]
    tm = min(tm, t)
    assert t % tm == 0
    nr, nc, no, na = len(rows), len(consts), len(row_outs), len(acc_outs)

    def body(*refs):
        vals = [r[...] for r in refs[:nr + nc]]
        outs, accs = fn(*vals)
        for o_ref, o in zip(refs[nr + nc:nr + nc + no], outs):
            o_ref[...] = o.astype(o_ref.dtype)
        if na:
            step = pl.program_id(0)
            acc_refs = refs[nr + nc + no:]

            @pl.when(step == 0)
            def _():
                for a_ref, a in zip(acc_refs, accs):
                    a_ref[...] = a

            @pl.when(step > 0)
            def _():
                for a_ref, a in zip(acc_refs, accs):
                    a_ref[...] += a

    in_specs = [pl.BlockSpec((tm, w), lambda i: (i, 0)) for _, w in rows]
    in_specs += [pl.BlockSpec(c.shape, lambda i, nd=c.ndim: (0,) * nd) for c in consts]
    out_specs = [pl.BlockSpec((tm, c), lambda i: (i, 0)) for c, _ in row_outs]
    out_specs += [pl.BlockSpec(s, lambda i: (0, 0)) for s in acc_outs]
    out_shape = [jax.ShapeDtypeStruct((t, c), dt) for c, dt in row_outs]
    out_shape += [jax.ShapeDtypeStruct(s, F32) for s in acc_outs]
    res = pl.pallas_call(
        body, name=name, grid=(t // tm,), in_specs=in_specs, out_specs=out_specs, out_shape=out_shape,
        compiler_params=_params("arbitrary" if na else "parallel"),
    )(*[r for r, _ in rows], *consts)
    return res


def _ln_bwd(name, r, g, b, dy):
    def fn(r_v, dy_v, g_v, b_v):
        _, vjp = jax.vjp(_layer_norm, r_v, g_v, b_v)
        dr, dg, db = vjp(dy_v)
        return [dr, dr], [dg, db]
    return _rowwise(name, fn, [r, dy], [g, b], [(r.shape[1], F32), (r.shape[1], BF16)], [(1, r.shape[1])] * 2)


def _ln_loss_bwd(name, r, g, b, target):
    def fn(r_v, t_v, g_v, b_v):
        def loss_fn(rr, gg, bb):
            err = jnp.square(_layer_norm(rr, gg, bb) - t_v)
            return 0.5 * jnp.sum(jnp.mean(err, axis=-1, keepdims=True), axis=0, keepdims=True)
        loss, vjp = jax.vjp(loss_fn, r_v, g_v, b_v)
        dr, dg, db = vjp(jnp.ones((1, 1), F32))
        return [dr, dr], [dg, db, jnp.broadcast_to(loss, (1, LANES))]
    return _rowwise(name, fn, [r, target], [g, b], [(r.shape[1], F32), (r.shape[1], BF16)],
                    [(1, r.shape[1])] * 2 + [(1, LANES)])


def _rope_tables(posf, invf, sgn):
    ang = posf * invf
    return jnp.cos(ang), jnp.sin(ang) * sgn


def _rope_apply(tv, cos, sin):
    lane = lax.broadcasted_iota(jnp.int32, cos.shape, 1)
    first = (lane % HEAD_DIM) < (ROPE_DIM // 2)
    outs = []
    for gidx in range(tv.shape[1] // LANES):
        tg = tv[:, LANES * gidx:LANES * (gidx + 1)]
        sw = jnp.where(first, pltpu.roll(tg, LANES - ROPE_DIM // 2, 1), pltpu.roll(tg, ROPE_DIM // 2, 1))
        outs.append(tg * cos + sw * sin)
    return jnp.concatenate(outs, axis=1)


def _proj_in(h16, w_in, posf, invf, sgn, *, tm=512, carry=None):
    t, k = h16.shape
    cuts = [0, D_ATTN, 2 * D_ATTN, 3 * D_ATTN, 3 * D_ATTN + D_SSD, 3 * D_ATTN + D_SSD + D_CONV, w_in.shape[1]]
    nc = carry.n if carry is not None else 0

    def body(*refs):
        h_ref, w_ref, pos_ref, invf_ref, sgn_ref = refs[:5]
        q_ref, k_ref, v_ref, z_ref, xbc_ref, dt_ref, cs_ref = refs[5 + nc:12 + nc]
        if carry is not None:
            finish = _carried(carry, refs[5:5 + nc], refs[12 + nc:12 + 2 * nc], refs[12 + 2 * nc:], pl.program_id(0), t // tm)
        hv = h_ref[...]
        part = lambda a: jnp.dot(hv, w_ref[:, cuts[a]:cuts[a + 1]], preferred_element_type=F32)
        cos, sin = _rope_tables(pos_ref[...], invf_ref[...], sgn_ref[...])
        cs_ref[...] = jnp.concatenate([cos, sin], axis=1)
        q_ref[...] = _bf(_rope_apply(part(0), cos, sin) * (HEAD_DIM ** -0.5))
        k_ref[...] = _bf(_rope_apply(part(1), cos, sin))
        v_ref[...] = _bf(part(2))
        z_ref[...] = part(3)
        xbc_ref[...] = part(4)
        dt_ref[...] = part(5)
        if carry is not None:
            finish()

    row = lambda c: pl.BlockSpec((tm, c), lambda i: (i, 0))
    hbm = pl.BlockSpec(memory_space=pltpu.HBM)
    widths = [D_ATTN, D_ATTN, D_ATTN, D_SSD, D_CONV, LANES, 2 * LANES]
    dtypes = [BF16, BF16, BF16, F32, F32, F32, F32]
    res = pl.pallas_call(
        body, name="proj_in", grid=(t // tm,),
        in_specs=[row(k), _resident(w_in.shape), row(1), _resident(invf.shape), _resident(sgn.shape)] + [hbm] * nc,
        out_specs=[row(c) for c in widths] + [hbm] * nc,
        out_shape=[jax.ShapeDtypeStruct((t, c), dt) for c, dt in zip(widths, dtypes)]
        + (carry.out_shape if carry is not None else []),
        scratch_shapes=carry.scratch_shapes if carry is not None else [],
        compiler_params=_params("arbitrary" if carry is not None else "parallel"),
    )(h16, w_in, posf, invf, sgn, *(carry.operands if carry is not None else []))
    return tuple(res[:7]) + ((carry.results(res[7:]),) if carry is not None else ())


def _rope_bwd(dq, dk, cs):
    def fn(dq_v, dk_v, cs_v):
        cos, sin = cs_v[:, :LANES], -cs_v[:, LANES:]
        gq = _rope_apply(dq_v * (HEAD_DIM ** -0.5), cos, sin)
        gk = _rope_apply(dk_v, cos, sin)
        return [jnp.concatenate([gq, gk], axis=1)], []
    return _rowwise("rope_bwd", fn, [dq, dk, cs], [], [(2 * D_ATTN, BF16)])[0]


def _rms(v, w):
    return v * lax.rsqrt(jnp.mean(v * v, axis=-1, keepdims=True) + RMS_EPS) * w


def _ungroup(yg):
    w = HEADS_PER_GROUP * HEAD_DIM
    return jnp.concatenate([yg[:, GROUP_LANES * g:GROUP_LANES * g + w] for g in range(N_GROUPS)], axis=1)


def _group(xs):
    w = HEADS_PER_GROUP * HEAD_DIM
    parts = []
    for g in range(N_GROUPS):
        parts += [xs[:, w * g:w * (g + 1)], jnp.zeros((xs.shape[0], GROUP_LANES - w), xs.dtype)]
    return jnp.concatenate(parts, axis=1)


def _norms_fn(attn, yg, xs, z, w_attn, w_ssd, dskip):
    a_n = _rms(attn, w_attn)
    y = _ungroup(yg) + dskip * xs
    y_n = _rms(y * (z * jax.nn.sigmoid(z)), w_ssd)
    return jnp.concatenate([a_n, y_n], axis=1)


def _norms_fwd(attn, yg, xbc, z, w_attn, w_ssd, dskip):
    def fn(*v):
        return [_norms_fn(*v)], []
    return _rowwise("norms_fwd", fn, [attn, yg, (xbc, D_SSD), z], [w_attn, w_ssd, dskip], [(D_ATTN + D_SSD, BF16)])[0]


def _norms_bwd(attn, yg, xbc, z, w_attn, w_ssd, dskip, dcat):
    def fn(attn_v, yg_v, xs_v, z_v, dcat_v, wa_v, ws_v, dk_v):
        _, vjp = jax.vjp(_norms_fn, attn_v, yg_v, xs_v, z_v, wa_v, ws_v, dk_v)
        d_attn, d_yg, d_xs, d_z, d_wa, d_ws, d_dk = vjp(dcat_v)
        return [d_attn, d_yg, d_xs, d_z], [d_wa, d_ws, d_dk]
    return _rowwise("norms_bwd", fn, [attn, yg, (xbc, D_SSD), z, dcat], [w_attn, w_ssd, dskip],
                    [(D_ATTN, F32), (N_GROUPS * GROUP_LANES, F32), (D_SSD, F32), (D_SSD, BF16)], [(1, D_SSD)] * 3)


def _ssd_prep_fn(xs, dtp, dtb, alog, e_x, e_a):
    dt = jax.nn.softplus(dtp + dtb)
    a = -jnp.exp(alog)
    dtg = jnp.dot(dt, e_x, precision=HIGHEST, preferred_element_type=F32)
    xdtg = _group(xs) * dtg
    dag = jnp.dot(dt * a, e_a, precision=HIGHEST, preferred_element_type=F32)
    return xdtg, dag


def _ssd_prep_fwd(xbc, dtp, dtb, alog, e_x, e_a):
    def fn(xbc_v, dtp_v, dtb_v, alog_v, ex_v, ea_v):
        xdtg, dag = _ssd_prep_fn(xbc_v[:, :D_SSD], dtp_v, dtb_v, alog_v, ex_v, ea_v)
        return [xdtg, xbc_v[:, D_SSD:], dag], []
    return _rowwise("ssd_prep_fwd", fn, [xbc, dtp], [dtb, alog, e_x, e_a],
                    [(N_GROUPS * GROUP_LANES, BF16), (D_CONV - D_SSD, BF16), (N_GROUPS * LANES, F32)])


def _ssd_prep_bwd(xbc, dtp, dtb, alog, e_x, e_a, dxdtg, ddag, dxs_a, db, dc):
    def fn(xs_v, dtp_v, dxdtg_v, ddag_v, dxs_a_v, db_v, dc_v, dtb_v, alog_v, ex_v, ea_v):
        _, vjp = jax.vjp(lambda a, b, c, d: _ssd_prep_fn(a, b, c, d, ex_v, ea_v), xs_v, dtp_v, dtb_v, alog_v)
        dxs, ddtp, ddtb, dalog = vjp((dxdtg_v, ddag_v))
        return [jnp.concatenate([dxs + dxs_a_v, db_v, dc_v], axis=1), ddtp], [ddtb, dalog]
    return _rowwise("ssd_prep_bwd", fn, [(xbc, D_SSD), dtp, dxdtg, ddag, dxs_a, db, dc], [dtb, alog, e_x, e_a],
                    [(D_CONV, F32), (LANES, BF16)], [(1, LANES)] * 2)


def _shift_down(u, d):
    if d == 0:
        return u
    row = lax.broadcasted_iota(jnp.int32, u.shape, 0)
    return jnp.where(row >= d, pltpu.roll(u, d, 0), 0.0)


def _shift_up(u, d):
    if d == 0:
        return u
    s = u.shape[0]
    row = lax.broadcasted_iota(jnp.int32, u.shape, 0)
    return jnp.where(row < s - d, pltpu.roll(u, s - d, 0), 0.0)


def _conv_pre(u, w, b):
    acc = b
    for k in range(CONV_WIDTH):
        acc = acc + w[k:k + 1, :] * _shift_down(u, CONV_WIDTH - 1 - k)
    return acc


def _conv_fwd(u, w, b, *, tc=256):
    nb, s, c = u.shape

    def body(u_ref, w_ref, b_ref, o_ref):
        pre = _conv_pre(u_ref[0], w_ref[...], b_ref[...])
        o_ref[0] = pre * jax.nn.sigmoid(pre)

    return pl.pallas_call(
        body, name="conv_fwd", grid=(c // tc, nb),
        in_specs=[pl.BlockSpec((1, s, tc), lambda j, i: (i, 0, j)), pl.BlockSpec((CONV_WIDTH, tc), lambda j, i: (0, j)),
                  pl.BlockSpec((1, tc), lambda j, i: (0, j))],
        out_specs=pl.BlockSpec((1, s, tc), lambda j, i: (i, 0, j)),
        out_shape=jax.ShapeDtypeStruct((nb, s, c), F32),
        compiler_params=_params("parallel", "parallel"),
    )(u, w, b)


def _conv_bwd(u, w, b, dout, *, tc=256):
    nb, s, c = u.shape

    def body(u_ref, w_ref, b_ref, d_ref, du_ref, dw_ref, db_ref):
        uv, wv = u_ref[0], w_ref[...]
        pre = _conv_pre(uv, wv, b_ref[...])
        sig = jax.nn.sigmoid(pre)
        dpre = d_ref[0] * (sig * (1.0 + pre * (1.0 - sig)))
        du = jnp.zeros_like(uv)
        dws = []
        for k in range(CONV_WIDTH):
            du = du + wv[k:k + 1, :] * _shift_up(dpre, CONV_WIDTH - 1 - k)
            dws.append(jnp.sum(dpre * _shift_down(uv, CONV_WIDTH - 1 - k), axis=0, keepdims=True))
        du_ref[0] = _bf(du)
        dwv = jnp.concatenate(dws + [jnp.zeros((8 - CONV_WIDTH, tc), F32)], axis=0)
        dbv = jnp.sum(dpre, axis=0, keepdims=True)
        first = pl.program_id(1) == 0

        @pl.when(first)
        def _():
            dw_ref[...] = dwv
            db_ref[...] = dbv

        @pl.when(jnp.logical_not(first))
        def _():
            dw_ref[...] += dwv
            db_ref[...] += dbv

    blk = pl.BlockSpec((1, s, tc), lambda j, i: (i, 0, j))
    return pl.pallas_call(
        body, name="conv_bwd", grid=(c // tc, nb),
        in_specs=[blk, pl.BlockSpec((CONV_WIDTH, tc), lambda j, i: (0, j)), pl.BlockSpec((1, tc), lambda j, i: (0, j)), blk],
        out_specs=[blk, pl.BlockSpec((8, tc), lambda j, i: (0, j)), pl.BlockSpec((1, tc), lambda j, i: (0, j))],
        out_shape=[jax.ShapeDtypeStruct((nb, s, c), BF16), jax.ShapeDtypeStruct((8, c), F32), jax.ShapeDtypeStruct((1, c), F32)],
        compiler_params=_params("parallel", "arbitrary"),
    )(u, w, b, dout)


FWD_KEY_BLOCK = 256


def _branch_bias_table(seq, kb):
    ratio = SEQ_BLOCK // kb
    key = np.arange(kb)[None, :, None]
    query = np.arange(SEQ_BLOCK)[None, None, :]
    delta = (np.arange(seq // kb)[:, None, None] - (ratio - 1)) * kb + query - key
    cnt = np.zeros(delta.shape, np.float64)
    for window, dilation in ((128, 1), (512, 4), (2048, 16)):
        cnt += (delta >= 0) & (delta % dilation == 0) & (delta <= window)
    return jnp.asarray(np.where(cnt > 0, np.log(np.maximum(cnt, 1.0)), NEG).astype(np.float32))


HEADS_PER_BLOCK = LANES // HEAD_DIM


def _head_rows(v, h):
    row = lax.broadcasted_iota(jnp.int32, v.shape, 0)
    return jnp.where((row >= HEAD_DIM * h) & (row < HEAD_DIM * (h + 1)), v, jnp.zeros_like(v))


def _attn_fwd(q, k, v, bias):
    nb_, s, _ = q.shape
    ab, kb = SEQ_BLOCK, FWD_KEY_BLOCK
    nblk, nkb, ratio = s // ab, s // kb, ab // kb

    def body(q_ref, k_ref, v_ref, b_ref, o_ref, lse_ref, vt_s):
        i = pl.program_id(2)

        @pl.when(i == 0)
        def _():
            for jb in range(nkb):
                vt_s[jb] = v_ref[0, kb * jb:kb * (jb + 1), :].T

        qt = q_ref[0].T
        qts = [_head_rows(qt, h) for h in range(HEADS_PER_BLOCK)]

        last = ratio * (i + 1) - 1

        def scores(j):
            kj = k_ref[0, pl.ds(pl.multiple_of(j * kb, kb), kb), :]
            return [jnp.dot(kj, qts[h], preferred_element_type=F32) for h in range(HEADS_PER_BLOCK)]

        def step(j, carry):
            ahead = scores(jnp.minimum(j + 1, last))
            lb = b_ref[ratio * i - j + (ratio - 1)]
            out = []
            for h in range(HEADS_PER_BLOCK):
                m, l, acc = carry[3 * h:3 * h + 3]
                st = carry[3 * HEADS_PER_BLOCK + h] + lb
                m_new = jnp.maximum(m, jnp.max(st, axis=0, keepdims=True))
                p = jnp.exp(st - m_new)
                a = jnp.exp(m - m_new)
                l = a * l + jnp.sum(p, axis=0, keepdims=True)
                vt = vt_s[j, HEAD_DIM * h:HEAD_DIM * (h + 1), :]
                acc = a * acc + jnp.dot(vt, _bf(p), preferred_element_type=F32)
                out += [m_new, l, acc]
            return tuple(out) + tuple(ahead)

        init = (jnp.full((1, ab), NEG, F32), jnp.zeros((1, ab), F32), jnp.zeros((HEAD_DIM, ab), F32)) * HEADS_PER_BLOCK
        res = lax.fori_loop(0, ratio * (i + 1), step, init + tuple(scores(0)))
        ot = jnp.concatenate([res[3 * h + 2] / res[3 * h + 1] for h in range(HEADS_PER_BLOCK)], axis=0)
        o_ref[0] = ot.T
        rows = [res[3 * h] + jnp.log(res[3 * h + 1]) for h in range(HEADS_PER_BLOCK)]
        lse_ref[0, 0, 0] = jnp.concatenate(rows + [jnp.zeros((8 - HEADS_PER_BLOCK, ab), F32)], axis=0)

    qblk = pl.BlockSpec((1, ab, LANES), lambda b, hp, i: (b, i, hp))
    full = pl.BlockSpec((1, s, LANES), lambda b, hp, i: (b, 0, hp))
    return pl.pallas_call(
        body, name="attn_fwd", grid=(nb_, D_ATTN // LANES, nblk),
        in_specs=[qblk, full, full, pl.BlockSpec((nkb, kb, ab), lambda b, hp, i: (0, 0, 0))],
        out_specs=[qblk, pl.BlockSpec((1, 1, 1, 8, ab), lambda b, hp, i: (b, hp, i, 0, 0))],
        out_shape=[jax.ShapeDtypeStruct((nb_, s, D_ATTN), F32),
                   jax.ShapeDtypeStruct((nb_, D_ATTN // LANES, nblk, 8, ab), F32)],
        scratch_shapes=[pltpu.VMEM((nkb, LANES, kb), BF16)],
        compiler_params=_params("parallel", "parallel", "arbitrary"),
    )(q, k, v, bias)


def _attn_bwd(q, k, v, o, do, lse, bias):
    nb_, s, _ = q.shape
    ab = SEQ_BLOCK
    nblk = s // ab

    nh = HEADS_PER_BLOCK

    def body(q_ref, k_ref, v_ref, o_ref, do_ref, lse_ref, b_ref, dq_ref, dk_ref, dv_ref,
             qt_s, dot_s, kt_s, dqt_s, do16_s, d_s, dk_acc, dv_acc):
        for jb in range(nblk):
            sl = slice(ab * jb, ab * (jb + 1))
            qt, kt = q_ref[0, sl, :].T, k_ref[0, sl, :].T
            do = do_ref[0, sl, :]
            dot = do.T
            prod = dot * o_ref[0, sl, :].T
            do16_s[sl, :] = _bf(do)
            for h in range(nh):
                qt_s[nh * jb + h] = _head_rows(qt, h)
                kt_s[nh * jb + h] = _head_rows(kt, h)
                dot_s[nh * jb + h] = _head_rows(_bf(dot), h)
            d_s[jb] = jnp.concatenate(
                [jnp.sum(prod[HEAD_DIM * h:HEAD_DIM * (h + 1)], axis=0, keepdims=True) for h in range(nh)]
                + [jnp.zeros((8 - nh, ab), F32)], axis=0)
            dqt_s[jb] = jnp.zeros((LANES, ab), F32)

        def outer(j, carry):
            ks = pl.ds(pl.multiple_of(j * ab, ab), ab)
            kj, vj = k_ref[0, ks, :], v_ref[0, ks, :]
            dk_acc[...] = jnp.zeros_like(dk_acc)
            dv_acc[...] = jnp.zeros_like(dv_acc)

            def inner(i, c2):
                qs = pl.ds(pl.multiple_of(i * ab, ab), ab)
                qi, doi = q_ref[0, qs, :], do16_s[qs, :]
                lb = b_ref[i - j]
                for h in range(nh):
                    st = jnp.dot(kj, qt_s[nh * i + h], preferred_element_type=F32) + lb
                    pt = jnp.exp(st - lse_ref[0, 0, i, h:h + 1, :])
                    dpt = jnp.dot(vj, dot_s[nh * i + h], preferred_element_type=F32)
                    dst16 = _bf(pt * (dpt - d_s[i, h:h + 1, :]))
                    dv_acc[h] += jnp.dot(_bf(pt), doi, preferred_element_type=F32)
                    dk_acc[h] += jnp.dot(dst16, qi, preferred_element_type=F32)
                    dqt_s[i] += jnp.dot(kt_s[nh * j + h], dst16, preferred_element_type=F32)
                return c2

            lax.fori_loop(j, nblk, inner, 0)
            lane = lax.broadcasted_iota(jnp.int32, (ab, LANES), 1)
            dk_ref[0, ks, :] = jnp.where(lane < HEAD_DIM, dk_acc[0], dk_acc[1])
            dv_ref[0, ks, :] = _bf(jnp.where(lane < HEAD_DIM, dv_acc[0], dv_acc[1]))
            return carry

        lax.fori_loop(0, nblk, outer, 0)
        for jb in range(nblk):
            dq_ref[0, ab * jb:ab * (jb + 1), :] = dqt_s[jb].T

    assert nh == 2
    full = pl.BlockSpec((1, s, LANES), lambda b, hp: (b, 0, hp))
    return pl.pallas_call(
        body, name="attn_bwd", grid=(nb_, D_ATTN // LANES),
        in_specs=[full] * 5 + [pl.BlockSpec((1, 1, nblk, 8, ab), lambda b, hp: (b, hp, 0, 0, 0)),
                               pl.BlockSpec((nblk, ab, ab), lambda b, hp: (0, 0, 0))],
        out_specs=[full, full, full],
        out_shape=[jax.ShapeDtypeStruct((nb_, s, D_ATTN), F32), jax.ShapeDtypeStruct((nb_, s, D_ATTN), F32),
                   jax.ShapeDtypeStruct((nb_, s, D_ATTN), BF16)],
        scratch_shapes=[pltpu.VMEM((nh * nblk, LANES, ab), BF16), pltpu.VMEM((nh * nblk, LANES, ab), BF16),
                        pltpu.VMEM((nh * nblk, LANES, ab), BF16), pltpu.VMEM((nblk, LANES, ab), F32),
                        pltpu.VMEM((s, LANES), BF16), pltpu.VMEM((nblk, 8, ab), F32),
                        pltpu.VMEM((nh, ab, LANES), F32), pltpu.VMEM((nh, ab, LANES), F32)],
        compiler_params=_params("parallel", "parallel"),
    )(q, k, v, o, do, lse, bias)


def _cumsum_fwd(dag):
    nb_, s, c = dag.shape
    ab = SEQ_BLOCK

    def body(a_ref, o_ref, ot_ref):
        r = lax.broadcasted_iota(jnp.int32, (ab, ab), 0)
        cc = lax.broadcasted_iota(jnp.int32, (ab, ab), 1)
        tri = (r >= cc).astype(F32)
        carry = jnp.zeros((1, c), F32)
        for i in range(s // ab):
            loc = jnp.dot(tri, a_ref[0, ab * i:ab * (i + 1), :], precision=HIGHEST, preferred_element_type=F32) + carry
            o_ref[0, ab * i:ab * (i + 1), :] = loc
            ot_ref[0, :, ab * i:ab * (i + 1)] = loc.T
            carry = loc[ab - 1:ab, :]

    return pl.pallas_call(
        body, name="ssd_cumsum", grid=(nb_,),
        in_specs=[pl.BlockSpec((1, s, c), lambda b: (b, 0, 0))],
        out_specs=[pl.BlockSpec((1, s, c), lambda b: (b, 0, 0)), pl.BlockSpec((1, c, s), lambda b: (b, 0, 0))],
        out_shape=[jax.ShapeDtypeStruct((nb_, s, c), F32), jax.ShapeDtypeStruct((nb_, c, s), F32)],
        compiler_params=_params("parallel"),
    )(dag)


def _cumsum_bwd(dcol, drow):
    nb_, s, c = dcol.shape
    ab = SEQ_BLOCK

    def body(c_ref, r_ref, o_ref):
        r = lax.broadcasted_iota(jnp.int32, (ab, ab), 0)
        cc = lax.broadcasted_iota(jnp.int32, (ab, ab), 1)
        tri = (r <= cc).astype(F32)
        carry = jnp.zeros((1, c), F32)
        for i in reversed(range(s // ab)):
            rows = r_ref[0, :, ab * i:ab * (i + 1)].T
            parts = []
            for g in range(N_GROUPS):
                parts += [rows[:, 8 * g:8 * (g + 1)], jnp.zeros((ab, LANES - 8), F32)]
            blk = c_ref[0, ab * i:ab * (i + 1), :] + jnp.concatenate(parts, axis=1)
            loc = jnp.dot(tri, blk, precision=HIGHEST, preferred_element_type=F32) + carry
            o_ref[0, ab * i:ab * (i + 1), :] = loc
            carry = loc[0:1, :]

    return pl.pallas_call(
        body, name="ssd_cumsum_bwd", grid=(nb_,),
        in_specs=[pl.BlockSpec((1, s, c), lambda b: (b, 0, 0)), pl.BlockSpec((1, N_GROUPS * 8, s), lambda b: (b, 0, 0))],
        out_specs=pl.BlockSpec((1, s, c), lambda b: (b, 0, 0)),
        out_shape=jax.ShapeDtypeStruct((nb_, s, c), F32),
        compiler_params=_params("parallel"),
    )(dcol, drow)


def _causal_ok(i, j):
    ab = SEQ_BLOCK
    r = lax.broadcasted_iota(jnp.int32, (ab, ab), 0)
    c = lax.broadcasted_iota(jnp.int32, (ab, ab), 1)
    return (r + (i - j) * ab) >= c


def _causal_ok_t(i, j):
    ab = SEQ_BLOCK
    r = lax.broadcasted_iota(jnp.int32, (ab, ab), 0)
    c = lax.broadcasted_iota(jnp.int32, (ab, ab), 1)
    return (c + (i - j) * ab) >= r


def _ssd_chunk(s_in, x, bm_t, cm, cb, acol, arow, a_prev, ok):
    q = x.shape[0]
    decay = jnp.exp(jnp.where(ok, acol - arow, NEG))
    y = jnp.dot(_bf(cb * decay), x, preferred_element_type=F32)
    y = y + jnp.exp(acol - a_prev) * jnp.dot(cm, _bf(s_in), preferred_element_type=F32)
    a_end = acol[q - 1:q, :]
    wx = _bf(jnp.exp(a_end - acol) * x.astype(F32))
    s_out = jnp.exp(a_end - a_prev) * s_in + jnp.dot(bm_t, wx, preferred_element_type=F32)
    return y, s_out


def _ssd_specs(s):
    xblk = pl.BlockSpec((1, s, GROUP_LANES), lambda b, g: (b, 0, g))
    bblk = pl.BlockSpec((1, s, D_STATE), lambda b, g: (b, 0, g))
    cblk = pl.BlockSpec((1, s, D_STATE), lambda b, g: (b, 0, N_GROUPS + g))
    tblk = pl.BlockSpec((1, 8, s), lambda b, g: (b, (LANES // 8) * g, 0))
    return xblk, bblk, cblk, tblk


def _chunk_views(i, j, x_ref, ac_ref, at_ref):
    ab = SEQ_BLOCK
    sl = slice(ab * i, ab * (i + 1))
    hs = slice(HEAD_DIM * j, HEAD_DIM * (j + 1))
    a_prev = jnp.zeros((1, 1), F32) if i == 0 else ac_ref[0, ab * i - 1:ab * i, j:j + 1]
    return sl, hs, ac_ref[0, sl, j:j + 1], at_ref[0, j:j + 1, sl], a_prev


def _ssd_fwd_chunked(xdtg, bc, acum, acum_t):
    nb_, s, _ = xdtg.shape
    ab = SEQ_BLOCK
    hpg = HEADS_PER_GROUP

    def body(x_ref, b_ref, c_ref, ac_ref, at_ref, y_ref):
        ok = _causal_ok(0, 0)
        states = [jnp.zeros((D_STATE, HEAD_DIM), F32) for _ in range(hpg)]
        for i in range(s // ab):
            bm, cm = b_ref[0, ab * i:ab * (i + 1), :], c_ref[0, ab * i:ab * (i + 1), :]
            bm_t = bm.T
            cb = jnp.dot(cm, bm_t, preferred_element_type=F32)
            ys = []
            for j in range(hpg):
                sl, hs, acol, arow, a_prev = _chunk_views(i, j, x_ref, ac_ref, at_ref)
                y, states[j] = _ssd_chunk(states[j], x_ref[0, sl, hs], bm_t, cm, cb, acol, arow, a_prev, ok)
                ys.append(y)
            y_ref[0, sl, :] = jnp.concatenate(ys + [jnp.zeros((ab, GROUP_LANES - hpg * HEAD_DIM), F32)], axis=1)

    xblk, bblk, cblk, tblk = _ssd_specs(s)
    ablk = pl.BlockSpec((1, s, LANES), lambda b, g: (b, 0, g))
    return pl.pallas_call(
        body, name="ssd_fwd", grid=(nb_, N_GROUPS), in_specs=[xblk, bblk, cblk, ablk, tblk], out_specs=xblk,
        out_shape=jax.ShapeDtypeStruct((nb_, s, N_GROUPS * GROUP_LANES), F32),
        compiler_params=_params("parallel", "parallel"),
    )(xdtg, bc, bc, acum, acum_t)


def _ssd_bwd_chunked(xdtg, bc, acum, acum_t, dyg):
    nb_, s, _ = xdtg.shape
    ab = SEQ_BLOCK
    nblk = s // ab
    hpg = HEADS_PER_GROUP

    def body(x_ref, b_ref, c_ref, ac_ref, at_ref, dy_ref, dx_ref, db_ref, dc_ref, dac_ref, dar_ref, s_s):
        ok = _causal_ok(0, 0)
        dx_ref[...] = jnp.zeros_like(dx_ref)
        dac_ref[...] = jnp.zeros_like(dac_ref)
        dar_ref[...] = jnp.zeros_like(dar_ref)
        states = [jnp.zeros((D_STATE, HEAD_DIM), F32) for _ in range(hpg)]
        for i in range(nblk):
            bm_t = b_ref[0, ab * i:ab * (i + 1), :].T
            for j in range(hpg):
                sl, hs, acol, arow, a_prev = _chunk_views(i, j, x_ref, ac_ref, at_ref)
                s_s[hpg * i + j] = states[j]
                if i + 1 < nblk:
                    a_end = acol[ab - 1:ab, :]
                    wx = _bf(jnp.exp(a_end - acol) * x_ref[0, sl, hs].astype(F32))
                    states[j] = jnp.exp(a_end - a_prev) * states[j] + jnp.dot(bm_t, wx, preferred_element_type=F32)
        ok_t = _causal_ok_t(0, 0)
        last_row = lax.broadcasted_iota(jnp.int32, (ab, 1), 0) == ab - 1
        d_state = [jnp.zeros((D_STATE, HEAD_DIM), F32) for _ in range(hpg)]
        pending = [jnp.zeros((1, 1), F32) for _ in range(hpg)]
        total = lambda v: jnp.sum(v, keepdims=True)
        for i in reversed(range(nblk)):
            bm, cm = b_ref[0, ab * i:ab * (i + 1), :], c_ref[0, ab * i:ab * (i + 1), :]
            cm_t = cm.T
            cbt = jnp.dot(bm, cm_t, preferred_element_type=F32)
            dcbt = jnp.zeros((ab, ab), F32)
            d_bm, d_cm = jnp.zeros((ab, D_STATE), F32), jnp.zeros((ab, D_STATE), F32)
            for j in range(hpg):
                sl, hs, acol, arow, a_prev = _chunk_views(i, j, x_ref, ac_ref, at_ref)
                x, dy = x_ref[0, sl, hs], dy_ref[0, sl, hs]
                dy16 = _bf(dy)
                s_in, g_out = s_s[hpg * i + j], d_state[j]
                s16, g16 = _bf(s_in), _bf(g_out)
                decay = jnp.exp(jnp.where(ok_t, arow - acol, NEG))
                gt = cbt * decay
                dgt = lax.dot_general(x, dy16, _NT, preferred_element_type=F32)
                d_x = jnp.dot(_bf(gt), dy16, preferred_element_type=F32)
                dcbt = dcbt + dgt * decay
                mm = dgt * gt
                d_arow = jnp.sum(mm, axis=0, keepdims=True)
                d_acol = -jnp.sum(mm, axis=1, keepdims=True)
                e = jnp.exp(acol - a_prev)
                edy16 = _bf(e * dy)
                d_cm = d_cm + lax.dot_general(edy16, s16, _NT, preferred_element_type=F32)
                d_s = jnp.dot(cm_t, edy16, preferred_element_type=F32)
                de_e = jnp.sum(dy * jnp.dot(cm, s16, preferred_element_type=F32), axis=1, keepdims=True) * e
                a_end = acol[ab - 1:ab, :]
                w = jnp.exp(a_end - acol)
                f = jnp.exp(a_end - a_prev)
                x32 = x.astype(F32)
                bg = jnp.dot(bm, g16, preferred_element_type=F32)
                d_x = d_x + w * bg
                d_bm = d_bm + lax.dot_general(_bf(w * x32), g16, _NT, preferred_element_type=F32)
                dw_w = jnp.sum(bg * x32, axis=1, keepdims=True) * w
                df_f = total(g_out * s_in) * f
                d_end = total(dw_w) + df_f
                d_acol = d_acol + de_e - dw_w + jnp.where(last_row, d_end + pending[j], 0.0)
                pending[j] = -total(de_e) - df_f
                d_state[j] = d_s + f * g_out
                dx_ref[0, sl, hs] = d_x
                dac_ref[0, sl, j:j + 1] = d_acol
                dar_ref[0, j:j + 1, sl] = d_arow
            dcbt16 = _bf(dcbt)
            db_ref[0, ab * i:ab * (i + 1), :] = d_bm + jnp.dot(dcbt16, cm, preferred_element_type=F32)
            dc_ref[0, ab * i:ab * (i + 1), :] = d_cm + lax.dot_general(dcbt16, bm, _TN, preferred_element_type=F32)

    xblk, bblk, cblk, tblk = _ssd_specs(s)
    ablk = pl.BlockSpec((1, s, LANES), lambda b, g: (b, 0, g))
    return pl.pallas_call(
        body, name="ssd_bwd", grid=(nb_, N_GROUPS),
        in_specs=[xblk, bblk, cblk, ablk, tblk, xblk],
        out_specs=[xblk, bblk, bblk, ablk, pl.BlockSpec((1, 8, s), lambda b, g: (b, g, 0))],
        out_shape=[jax.ShapeDtypeStruct((nb_, s, N_GROUPS * GROUP_LANES), F32),
                   jax.ShapeDtypeStruct((nb_, s, N_GROUPS * D_STATE), F32),
                   jax.ShapeDtypeStruct((nb_, s, N_GROUPS * D_STATE), F32),
                   jax.ShapeDtypeStruct((nb_, s, N_GROUPS * LANES), F32),
                   jax.ShapeDtypeStruct((nb_, N_GROUPS * 8, s), F32)],
        scratch_shapes=[pltpu.VMEM((nblk * hpg, D_STATE, HEAD_DIM), F32)],
        compiler_params=_params("parallel", "parallel"),
    )(xdtg, bc, bc, acum, acum_t, dyg)


def _interleave(wg, wu):
    k, f = wg.shape
    gi = GATE_UP_INTERLEAVE
    return jnp.stack([wg.reshape(k, f // gi, gi), wu.reshape(k, f // gi, gi)], axis=2).reshape(k, 2 * f)


def _head_expanders():
    e_x = np.zeros((LANES, N_GROUPS * GROUP_LANES), np.float32)
    e_a = np.zeros((LANES, N_GROUPS * LANES), np.float32)
    for h in range(N_HEADS):
        g, j = divmod(h, HEADS_PER_GROUP)
        e_x[h, GROUP_LANES * g + HEAD_DIM * j:GROUP_LANES * g + HEAD_DIM * (j + 1)] = 1.0
        e_a[h, LANES * g + j] = 1.0
    return jnp.asarray(e_x), jnp.asarray(e_a)


def _pad_lanes(v, n=LANES):
    return jnp.pad(v, ((0, 0), (0, n - v.shape[1])))


def _local_step(x, positions, target, w, late=None, early_grad_job=None):
    nb, s, d = x.shape
    t = nb * s
    x2 = x.reshape(t, d)
    tgt2 = target.reshape(t, d)
    (job_a, weights_a), (job_b, weights_b) = late if late is not None else ((None, None), (None, None))

    x16 = _bf(x2)
    wgu1 = _interleave(w["ffn1_gate"], w["ffn1_up"])
    ffn1 = _ffn_fwd("ffn1_fwd", x16, x2, wgu1, w["ffn1_down"], w["ln1_g"], w["ln1_b"], carry=job_a)
    au1, hm1, h1, r1, h1_16 = ffn1[:5]
    if job_a is not None:
        w = {**w, **weights_a(ffn1[5])}

    w_in = w["w_in"]
    wqk, wv, wz = w_in[:, :2 * D_ATTN], w_in[:, 2 * D_ATTN:3 * D_ATTN], w_in[:, 3 * D_ATTN:3 * D_ATTN + D_SSD]
    wxbc = w_in[:, 3 * D_ATTN + D_SSD:3 * D_ATTN + D_SSD + D_CONV]
    wdt = _pad_lanes(w_in[:, 3 * D_ATTN + D_SSD + D_CONV:])

    inv_freq = ROPE_THETA ** (-jnp.arange(0, ROPE_DIM, 2, dtype=F32) / ROPE_DIM)
    half = ROPE_DIM // 2
    head_invf = jnp.concatenate([inv_freq, inv_freq, jnp.zeros((HEAD_DIM - ROPE_DIM,), F32)])
    head_sgn = jnp.concatenate([-jnp.ones((half,), F32), jnp.ones((half,), F32), jnp.zeros((HEAD_DIM - ROPE_DIM,), F32)])
    invf = jnp.tile(head_invf, LANES // HEAD_DIM)[None, :]
    sgn = jnp.tile(head_sgn, LANES // HEAD_DIM)[None, :]
    posf = positions.astype(F32).reshape(t, 1)
    bias_fwd, bias_bwd = _branch_bias_table(s, FWD_KEY_BLOCK), _branch_bias_table(s, SEQ_BLOCK)
    e_x, e_a = _head_expanders()
    dtb, alog = _pad_lanes(w["dt_bias"]), _pad_lanes(w["a_log"])
    dskip = jnp.repeat(w["d_skip"], HEAD_DIM, axis=1)

    proj = _proj_in(h1_16, _pad_lanes(w_in, w_in.shape[1] - N_HEADS + LANES), posf, invf, sgn, carry=job_b)
    q16, k16, v16, z, xbc_pre, dtp, cs = proj[:7]
    if job_b is not None:
        w = {**w, **weights_b(proj[7])}
    wgu2 = _interleave(w["ffn2_gate"], w["ffn2_up"])
    to3 =lambda a: a.reshape(nb, s, a.shape[-1])
    attn_o, lse = _attn_fwd(to3(q16), to3(k16), to3(v16), bias_fwd)

    xbc = _conv_fwd(to3(xbc_pre), w["conv_w"], w["conv_b"]).reshape(t, D_CONV)
    xdtg, bc16, dag = _ssd_prep_fwd(xbc, dtp, dtb, alog, e_x, e_a)
    acum, acum_t = _cumsum_fwd(to3(dag))
    yg = _ssd_fwd_chunked(to3(xdtg), to3(bc16), acum, acum_t)

    cat = _norms_fwd(attn_o.reshape(t, D_ATTN), yg.reshape(t, -1), xbc, z, w["attn_norm_w"], w["ssd_norm_w"], dskip)
    h2, r2, h2_16 = _mm_res_ln("w_out_ln2", cat, w["w_out"], h1, w["ln2_g"], w["ln2_b"], scale=1.0)

    au2, hm2, _, r3, _ = _ffn_fwd("ffn2_fwd", h2_16, h2, wgu2, w["ffn2_down"], w["ln3_g"], w["ln3_b"])

    g = {}
    dr3, dr3_16, g["ln3_g"], g["ln3_b"], loss = _ln_loss_bwd("loss_ln3_bwd", r3, w["ln3_g"], w["ln3_b"], tgt2)

    dau2, dh2 = _ffn_bwd("ffn2_bwd", dr3_16, dr3, w["ffn2_down"].T, au2, wgu2.T)
    g["ffn2_down"] = _mm_tn("ffn2_down_dw", hm2, dr3_16, scale=0.5, tk=D_FF // 2, tn=512)
    g["ffn2_gate"], g["ffn2_up"] = _mm_tn_gate_up("ffn2_up_dw", h2_16, dau2)

    dr2, dr2_16, g["ln2_g"], g["ln2_b"] = _ln_bwd("ln2_bwd", r2, w["ln2_g"], w["ln2_b"], dh2)
    dcat = _mm("w_out_dx", [(dr2_16, w["w_out"].T)], tn=768)
    g["w_out"] = _mm_tn("w_out_dw", cat, dr2_16, tk=768, tn=1024)

    d_attn, dyg, dxs_a, dz16, g["attn_norm_w"], g["ssd_norm_w"], ddskip = _norms_bwd(
        attn_o.reshape(t, D_ATTN), yg.reshape(t, -1), xbc, z, w["attn_norm_w"], w["ssd_norm_w"], dskip, dcat)
    g["d_skip"] = ddskip.reshape(N_HEADS, HEAD_DIM).sum(axis=1)[None, :]

    dq, dk, dv16 = _attn_bwd(to3(q16), to3(k16), to3(v16), attn_o, to3(d_attn), lse, bias_bwd)
    dqk16 = _rope_bwd(dq.reshape(t, D_ATTN), dk.reshape(t, D_ATTN), cs)

    dxdtg, dbm, dcm, dacol, darow = _ssd_bwd_chunked(to3(xdtg), to3(bc16), acum, acum_t, to3(dyg))
    ddag = _cumsum_bwd(dacol, darow)
    dxbc, ddtp16, ddtb, dalog = _ssd_prep_bwd(xbc, dtp, dtb, alog, e_x, e_a, dxdtg.reshape(t, -1), ddag.reshape(t, -1),
                                               dxs_a, dbm.reshape(t, -1), dcm.reshape(t, -1))
    g["dt_bias"], g["a_log"] = ddtb[:, :N_HEADS], dalog[:, :N_HEADS]
    dxbc_pre16, dconv_w, g["conv_b"] = _conv_bwd(to3(xbc_pre), w["conv_w"], w["conv_b"], to3(dxbc))
    g["conv_w"] = dconv_w[:CONV_WIDTH]
    dxbc_pre16 = dxbc_pre16.reshape(t, D_CONV)
    dv16 = dv16.reshape(t, D_ATTN)

    dh1 = _mm("w_in_dx", [(dqk16, wqk.T), (dv16, wv.T), (dz16, wz.T), (dxbc_pre16, wxbc.T), (ddtp16, wdt.T)],
              res=dr2, res_scale=ALPHA)
    g["w_in"] = _mm_tn_sections("w_in_dw", h1_16, [dqk16, dv16, dz16, dxbc_pre16, ddtp16])[:, :w_in.shape[1]]

    dr1, dr1_16, g["ln1_g"], g["ln1_b"] = _ln_bwd("ln1_bwd", r1, w["ln1_g"], w["ln1_b"], dh1)
    g["ffn1_down"] = _mm_tn("ffn1_down_dw", hm1, dr1_16, scale=0.5, tk=D_FF // 2, tn=512)
    ffn1b = _ffn_bwd("ffn1_bwd", dr1_16, dr1, w["ffn1_down"].T, au1, wgu1.T,
                     carry=None if early_grad_job is None else early_grad_job(g))
    dau1, dx = ffn1b[:2]
    early = ffn1b[2] if early_grad_job is not None else None
    g["ffn1_gate"], g["ffn1_up"] = _mm_tn_gate_up("ffn1_up_dw", x16, dau1)
    return loss, dx.reshape(nb, s, d), g, early


_HBM = pl.BlockSpec(memory_space=pltpu.HBM)
N_CHIPS = 4
N_DEVICES = 8


def _place():
    return lax.axis_index("x"), lax.axis_index("y"), lax.axis_index("c")


def _other_chips(x, y):
    return [(1 - x, y), (x, 1 - y), (1 - x, 1 - y)]


class _GatherJob:
    def __init__(self, shards):
        assert all((a.shape[0] // 2) % 16 == 0 for a in shards)
        self.n = len(shards)
        self.shapes = [a.shape for a in shards]
        self.operands = [a.reshape(2, a.shape[0] // 2, a.shape[1]) for a in shards]
        self.out_shape = [jax.ShapeDtypeStruct((N_CHIPS,) + a.shape, a.dtype) for a in self.operands]
        pair = pltpu.SemaphoreType.DMA((self.n, N_CHIPS - 1))
        self.scratch_shapes = [pair, pair, pair, pair]

    def results(self, outs):
        return [o.reshape((N_CHIPS,) + s) for o, s in zip(outs, self.shapes)]

    def phases(self, ins, outs, sems):
        n = self.n
        send_sems, recv_sems, fwd_send_sems, fwd_recv_sems = sems
        x, y, c = _place()
        me = 2 * x + y
        peers = _other_chips(x, y)

        def ici(t, p, src_chip):
            px, py = peers[p]
            return pltpu.make_async_remote_copy(
                ins[t].at[c] if src_chip is None else outs[t].at[src_chip, c],
                outs[t].at[me if src_chip is None else src_chip, c],
                send_sems.at[t, p], recv_sems.at[t, p], device_id=(px, py, c), device_id_type=MESH)

        def d2d(t, p, core):
            px, py = peers[p]
            return pltpu.make_async_remote_copy(
                outs[t].at[2 * px + py, core], outs[t].at[2 * px + py, core],
                fwd_send_sems.at[t, p], fwd_recv_sems.at[t, p], device_id=(x, y, 1 - c), device_id_type=MESH)

        pairs = [(t, p) for t in range(n) for p in range(N_CHIPS - 1)]

        def start():
            for t, p in pairs:
                ici(t, p, None).start()

        def forward():
            for t, p in pairs:
                px, py = peers[p]
                ici(t, p, 2 * px + py).wait_recv()
                d2d(t, p, c).start()

        def finish():
            for t, p in pairs:
                d2d(t, p, 1 - c).wait_recv()
            for t, p in pairs:
                ici(t, p, None).wait_send()
                d2d(t, p, c).wait_send()

        return start, forward, finish


class _ExchangeJob:
    def __init__(self, stacks):
        self.n = len(stacks)
        self.operands = list(stacks)
        self.out_shape = [jax.ShapeDtypeStruct(a.shape, a.dtype) for a in stacks]
        pair = pltpu.SemaphoreType.DMA((self.n, N_CHIPS - 1))
        self.scratch_shapes = [pair, pair]

    def results(self, outs):
        return list(outs)

    def phases(self, ins, outs, sems):
        send_sems, recv_sems = sems
        x, y, c = _place()
        me = 2 * x + y
        peers = _other_chips(x, y)
        pairs = [(t, p) for t in range(self.n) for p in range(N_CHIPS - 1)]

        def copy(t, p):
            px, py = peers[p]
            return pltpu.make_async_remote_copy(ins[t].at[2 * px + py], outs[t].at[me], send_sems.at[t, p],
                                                recv_sems.at[t, p], device_id=(px, py, c), device_id_type=MESH)

        def arrival(t, p):
            px, py = peers[p]
            return pltpu.make_async_remote_copy(ins[t].at[me], outs[t].at[2 * px + py], send_sems.at[t, p],
                                                recv_sems.at[t, p], device_id=(px, py, c), device_id_type=MESH)

        def start():
            for t, p in pairs:
                copy(t, p).start()

        def finish():
            for t, p in pairs:
                arrival(t, p).wait_recv()
            for t, p in pairs:
                copy(t, p).wait_send()

        return start, None, finish


def _run_job(job, name):
    n = job.n

    def body(*refs):
        for phase in job.phases(refs[:n], refs[n:2 * n], refs[2 * n:]):
            if phase is not None:
                phase()

    outs = pl.pallas_call(
        body, name=name, in_specs=[_HBM] * n, out_specs=[_HBM] * n,
        out_shape=job.out_shape, scratch_shapes=job.scratch_shapes,
    )(*job.operands)
    return job.results(outs)


def _sibling_halves(stacks, name):
    n = len(stacks)
    halves = [a.shape[1] // 2 for a in stacks]
    split = [a.reshape(a.shape[0], 2, h, a.shape[2]) for a, h in zip(stacks, halves)]

    def body(*refs):
        ins, outs = refs[:n], refs[n:2 * n]
        send_sems, recv_sems = refs[2 * n:]
        x, y, c = _place()
        cps = []
        for t in range(n):
            cp = pltpu.make_async_remote_copy(ins[t].at[:, 1 - c], outs[t], send_sems.at[t], recv_sems.at[t],
                                              device_id=(x, y, 1 - c), device_id_type=MESH)
            cp.start()
            cps.append(cp)
        for cp in cps:
            cp.wait()

    return pl.pallas_call(
        body, name=name,
        in_specs=[_HBM] * n, out_specs=[_HBM] * n,
        out_shape=[jax.ShapeDtypeStruct((a.shape[0], h, a.shape[2]), a.dtype) for a, h in zip(stacks, halves)],
        scratch_shapes=[pltpu.SemaphoreType.DMA((n,)), pltpu.SemaphoreType.DMA((n,))],
    )(*split)


def _sibling_swap(arrs):
    n = len(arrs)

    def body(*refs):
        ins, outs = refs[:n], refs[n:2 * n]
        send_sems, recv_sems = refs[2 * n:]
        x, y, c = _place()
        cps = []
        for t in range(n):
            cp = pltpu.make_async_remote_copy(ins[t], outs[t], send_sems.at[t], recv_sems.at[t],
                                              device_id=(x, y, 1 - c), device_id_type=MESH)
            cp.start()
            cps.append(cp)
        for cp in cps:
            cp.wait()

    return pl.pallas_call(
        body, name="sibling_swap",
        in_specs=[_HBM] * n, out_specs=[_HBM] * n,
        out_shape=[jax.ShapeDtypeStruct(a.shape, a.dtype) for a in arrs],
        scratch_shapes=[pltpu.SemaphoreType.DMA((n,)), pltpu.SemaphoreType.DMA((n,))],
    )(*arrs)


def _half_sum(name, own, other, core):
    k, r, cols = own.shape
    h = r // 2
    tr = next(cand for cand in (128, 176, 64, 32, 16) if h % cand == 0)
    nblk = h // tr

    def body(core_ref, own_ref, other_ref, o_ref):
        o_ref[...] = _bf(own_ref[...] + other_ref[...].astype(F32))

    grid_spec = pltpu.PrefetchScalarGridSpec(
        num_scalar_prefetch=1, grid=(nblk,),
        in_specs=[pl.BlockSpec((k, tr, cols), lambda i, core_ref: (0, i + core_ref[0] * nblk, 0)),
                  pl.BlockSpec((k, tr, cols), lambda i, core_ref: (0, i, 0))],
        out_specs=pl.BlockSpec((k, tr, cols), lambda i, core_ref: (0, i, 0)))
    return pl.pallas_call(
        body, name=name, grid_spec=grid_spec, out_shape=jax.ShapeDtypeStruct((k, h, cols), BF16),
        compiler_params=_params("parallel"),
    )(core.reshape(1).astype(jnp.int32), own, other)


def _small_allreduce(v):
    r = v.shape[0]

    def body(v_ref, tot_ref, slots, send_sems, recv_sems):
        x, y, c = _place()
        me = 4 * x + 2 * y + c
        slots[me] = v_ref[...]
        cps, peers = [], []
        for k in range(1, N_DEVICES):
            px = 1 - x if (k >> 2) & 1 else x
            py = 1 - y if (k >> 1) & 1 else y
            pc = 1 - c if k & 1 else c
            cp = pltpu.make_async_remote_copy(v_ref, slots.at[me], send_sems.at[k - 1], recv_sems.at[k - 1],
                                              device_id=(px, py, pc), device_id_type=MESH)
            cp.start()
            cps.append(cp)
            peers.append((px, py, pc))
        for k, (px, py, pc) in enumerate(peers):
            pltpu.make_async_remote_copy(v_ref, slots.at[4 * px + 2 * py + pc], send_sems.at[k], recv_sems.at[k],
                                         device_id=(px, py, pc), device_id_type=MESH).wait_recv()
        for cp in cps:
            cp.wait_send()
        acc = slots[0]
        for s in range(1, N_DEVICES):
            acc = acc + slots[s]
        tot_ref[...] = acc

    return pl.pallas_call(
        body, name="small_allreduce",
        in_specs=[pl.BlockSpec(memory_space=pltpu.VMEM)], out_specs=pl.BlockSpec(memory_space=pltpu.VMEM),
        out_shape=jax.ShapeDtypeStruct((r, LANES), F32),
        scratch_shapes=[pltpu.VMEM((N_DEVICES, r, LANES), F32), pltpu.SemaphoreType.DMA((N_DEVICES - 1,)),
                        pltpu.SemaphoreType.DMA((N_DEVICES - 1,))],
    )(v)


def _elementwise(name, fn, ins, out_dtypes):
    r, c = ins[0].shape[-2:]
    tr = next((cand for cand in (256, 176, 128, 64, 32, 16) if r % cand == 0), r)
    nin = len(ins)

    def body(*refs):
        outs = fn(*[ref[...] for ref in refs[:nin]])
        for o_ref, o in zip(refs[nin:], outs):
            o_ref[...] = o.astype(o_ref.dtype)

    in_specs = [pl.BlockSpec((tr, c), lambda i: (i, 0)) if a.ndim == 2 else pl.BlockSpec((a.shape[0], tr, c), lambda i: (0, i, 0))
                for a in ins]
    return pl.pallas_call(
        body, name=name, grid=(r // tr,), in_specs=in_specs,
        out_specs=[pl.BlockSpec((tr, c), lambda i: (i, 0)) for _ in out_dtypes],
        out_shape=[jax.ShapeDtypeStruct((r, c), dt) for dt in out_dtypes],
        compiler_params=_params("parallel"),
    )(*ins)


def _row_tile(rows):
    return next((cand for cand in (128, 176, 64, 32, 16) if rows % cand == 0), rows)


def _sum_slots(name, received, own, chip):
    _, r, cols = own.shape
    tr = _row_tile(r)

    def body(chip_ref, own_ref, a_ref, b_ref, c_ref, o_ref):
        o_ref[...] = ((own_ref[0].astype(F32) + a_ref[0].astype(F32)) + b_ref[0].astype(F32)) + c_ref[0].astype(F32)

    def slot(flip):
        return pl.BlockSpec((1, tr, cols), lambda i, chip_ref: (jnp.bitwise_xor(chip_ref[0], flip), i, 0))

    grid_spec = pltpu.PrefetchScalarGridSpec(
        num_scalar_prefetch=1, grid=(r // tr,), in_specs=[slot(0), slot(1), slot(2), slot(3)],
        out_specs=pl.BlockSpec((tr, cols), lambda i, chip_ref: (i, 0)))
    return pl.pallas_call(
        body, name=name, grid_spec=grid_spec, out_shape=jax.ShapeDtypeStruct((r, cols), F32),
        compiler_params=_params("parallel"),
    )(chip.reshape(1).astype(jnp.int32), own, received, received, received)


def _adamw_halves(name, mine, theirs, core, w, m, v):
    h, cols = mine.shape
    tr = _row_tile(h)
    nh = h // tr

    def body(core_ref, mine_ref, theirs_ref, w_ref, m_ref, v_ref, g_ref, d_ref, m2_ref, v2_ref):
        is_mine = (pl.program_id(0) // nh) == core_ref[0]
        g = jnp.where(is_mine, mine_ref[...], theirs_ref[...])
        outs = _adamw_math(g, w_ref[...], m_ref[...], v_ref[...])
        for ref, val in zip((g_ref, d_ref, m2_ref, v2_ref), outs):
            ref[...] = val

    half = pl.BlockSpec((tr, cols), lambda i, core_ref: (i % nh, 0))
    full = pl.BlockSpec((tr, cols), lambda i, core_ref: (i, 0))
    grid_spec = pltpu.PrefetchScalarGridSpec(
        num_scalar_prefetch=1, grid=(2 * nh,), in_specs=[half, half, full, full, full], out_specs=[full] * 4)
    return pl.pallas_call(
        body, name=name, grid_spec=grid_spec, out_shape=[jax.ShapeDtypeStruct((2 * h, cols), F32)] * 4,
        compiler_params=_params("parallel"),
    )(core.reshape(1).astype(jnp.int32), mine, theirs, w, m, v)


def _adamw_math(g, w_v, m_v, v_v):
    m2 = ADAM_B1 * m_v + (1.0 - ADAM_B1) * g
    v2 = ADAM_B2 * v_v + (1.0 - ADAM_B2) * jnp.square(g)
    m_hat = m2 / (1.0 - ADAM_B1 ** ADAM_STEP)
    v_hat = v2 / (1.0 - ADAM_B2 ** ADAM_STEP)
    delta = -ADAM_LR * (m_hat / (jnp.sqrt(v_hat) + ADAM_EPS) + ADAM_WD * w_v)
    return [g, delta, m2, v2]


def _adamw(name, g, w, m, v):
    return _elementwise(name, _adamw_math, [g, w, m, v], [F32] * 4)


_MATRICES = (("ffn1_gate", 1), ("ffn1_up", 1), ("ffn1_down", 0), ("w_in", 1), ("w_out", 0),
             ("ffn2_gate", 1), ("ffn2_up", 1), ("ffn2_down", 0))
_VECTORS = ("ln1_g", "ln1_b", "conv_b", "dt_bias", "a_log", "d_skip", "attn_norm_w", "ssd_norm_w",
            "ln2_g", "ln2_b", "ln3_g", "ln3_b")
_WEIGHT_ORDER = ("ln1_g", "ln1_b", "ffn1_gate", "ffn1_up", "ffn1_down", "w_in", "conv_w", "conv_b", "dt_bias", "a_log",
                 "d_skip", "attn_norm_w", "ssd_norm_w", "w_out", "ln2_g", "ln2_b", "ffn2_gate", "ffn2_up", "ffn2_down",
                 "ln3_g", "ln3_b")


def _pack_rows(vectors):
    parts = []
    for vec in vectors:
        flat = vec.reshape(-1)
        parts.append(jnp.pad(flat, (0, (-flat.shape[0]) % LANES)))
    flat = jnp.concatenate(parts)
    flat = jnp.pad(flat, (0, (-flat.shape[0]) % (8 * LANES)))
    return flat.reshape(-1, LANES)


def _unpack_rows(packed, shapes):
    flat = packed.reshape(-1)
    out, off = [], 0
    for shape in shapes:
        size = int(np.prod(shape))
        out.append(flat[off:off + size].reshape(shape))
        off += size + (-size) % LANES
    return out


def _assemble(stack, own, chip, axis):
    blocks = [jnp.where(chip == s, own, stack[s]) for s in range(N_CHIPS)]
    return jnp.concatenate(blocks, axis=axis)


def _split(full, axis):
    if axis == 0:
        return full.reshape(N_CHIPS, -1, full.shape[1])
    cols = full.shape[1] // N_CHIPS
    return jnp.stack([full[:, cols * s:cols * (s + 1)] for s in range(N_CHIPS)])


def kernel(x, positions, ln1_g, ln1_b, ffn1_gate, ffn1_up, ffn1_down, w_in, conv_w, conv_b, dt_bias, a_log, d_skip, attn_norm_w, ssd_norm_w, w_out, ln2_g, ln2_b, ffn2_gate, ffn2_up, ffn2_down, ln3_g, ln3_b, loss_target, m_ln1_g, m_ln1_b, m_ffn1_gate, m_ffn1_up, m_ffn1_down, m_w_in, m_conv_w, m_conv_b, m_dt_bias, m_a_log, m_d_skip, m_attn_norm_w, m_ssd_norm_w, m_w_out, m_ln2_g, m_ln2_b, m_ffn2_gate, m_ffn2_up, m_ffn2_down, m_ln3_g, m_ln3_b, v_ln1_g, v_ln1_b, v_ffn1_gate, v_ffn1_up, v_ffn1_down, v_w_in, v_conv_w, v_conv_b, v_dt_bias, v_a_log, v_d_skip, v_attn_norm_w, v_ssd_norm_w, v_w_out, v_ln2_g, v_ln2_b, v_ffn2_gate, v_ffn2_up, v_ffn2_down, v_ln3_g, v_ln3_b):
    given = dict(locals())
    wts = {n: given[n] for n in _WEIGHT_ORDER}
    mom_m = {n: given["m_" + n] for n in _WEIGHT_ORDER}
    mom_v = {n: given["v_" + n] for n in _WEIGHT_ORDER}
    chip = 2 * lax.axis_index("x") + lax.axis_index("y")

    core = lax.axis_index("c")
    groups = [[(n, axis) for n, axis in _MATRICES if n.startswith(prefix)] for prefix in ("ffn1", "w_", "ffn2")]
    own16 = {n: wts[n][0].astype(BF16) for n, _ in _MATRICES}
    gathered = _run_job(_GatherJob([own16[n] for n, _ in groups[0]]), "gather_ffn1")
    full = {n: _assemble(st, own16[n], chip, axis) for (n, axis), st in zip(groups[0], gathered)}
    for n in _VECTORS:
        full[n] = wts[n]
    conv_rows = jnp.pad(wts["conv_w"][0], ((0, 32 - CONV_WIDTH), (0, 0)))

    def mixer_weights(results):
        out = {n: _assemble(st, own16[n], chip, axis) for (n, axis), st in zip(groups[1], results)}
        out["conv_w"] = _assemble(results[-1], conv_rows, chip, 1)[:CONV_WIDTH]
        return out

    def ffn2_weights(results):
        return {n: _assemble(st, own16[n], chip, axis) for (n, axis), st in zip(groups[2], results)}

    late = [(_GatherJob([own16[n] for n, _ in groups[1]] + [conv_rows]), mixer_weights),
            (_GatherJob([own16[n] for n, _ in groups[2]]), ffn2_weights)]

    chip_sums = {}

    def core_sums(g, which, tag):
        partials = [_split(g[n], axis) for n, axis in which]
        from_sibling = _sibling_halves([p.astype(BF16) for p in partials], "sibling_halves_" + tag)
        for (n, _), p, o in zip(which, partials, from_sibling):
            chip_sums[n] = _half_sum("core_sum_" + n, p, o, core)
        return _ExchangeJob([chip_sums[n] for n, _ in which])

    last = [(n, axis) for n, axis in _MATRICES if n in ("ffn1_gate", "ffn1_up")]
    early = [(n, axis) for n, axis in _MATRICES if (n, axis) not in last]
    loss, grad_x, g, received_early = _local_step(x, positions, loss_target, full, late,
                                                  lambda g_now: core_sums(g_now, early, "early"))
    received_last = _run_job(core_sums(g, last, "last"), "exchange_last")
    received = dict(zip([n for n, _ in last + early], received_last + received_early))
    half_totals = [_sum_slots("sum_partials_" + n, received[n], chip_sums[n], chip) for n, _ in _MATRICES]
    other_halves = _sibling_swap(half_totals)

    small_shapes = [g[n].shape for n in _VECTORS] + [g["conv_w"].shape, (1,)]
    total = _small_allreduce(_pack_rows([g[n] for n in _VECTORS] + [g["conv_w"], loss[0, :1]]))
    small = _unpack_rows(total, small_shapes)
    loss_out = small[-1].reshape(())

    grads, deltas, new_m, new_v = {}, {}, {}, {}
    for (n, _), mine, theirs in zip(_MATRICES, half_totals, other_halves):
        res = _adamw_halves("adamw_" + n, mine, theirs, core, wts[n][0], mom_m[n][0], mom_v[n][0])
        grads[n], deltas[n], new_m[n], new_v[n] = [r[None] for r in res]

    vec_shapes = [wts[n].shape for n in _VECTORS]
    res = _adamw("adamw_vectors", _pack_rows(small[:len(_VECTORS)]), _pack_rows([wts[n] for n in _VECTORS]),
                 _pack_rows([mom_m[n] for n in _VECTORS]), _pack_rows([mom_v[n] for n in _VECTORS]))
    for dst, packed in zip((grads, deltas, new_m, new_v), res):
        for n, val in zip(_VECTORS, _unpack_rows(packed, vec_shapes)):
            dst[n] = val

    cols = conv_w.shape[2]
    g_conv = lax.dynamic_slice_in_dim(small[len(_VECTORS)], chip * cols, cols, axis=1)
    res = _adamw("adamw_conv_w", g_conv, wts["conv_w"][0], mom_m["conv_w"][0], mom_v["conv_w"][0])
    grads["conv_w"], deltas["conv_w"], new_m["conv_w"], new_v["conv_w"] = [r[None] for r in res]

    return (loss_out, grad_x, *[grads[n] for n in _WEIGHT_ORDER], *[deltas[n] for n in _WEIGHT_ORDER],
            *[new_m[n] for n in _WEIGHT_ORDER], *[new_v[n] for n in _WEIGHT_ORDER])
```

```python
import functools

import numpy as np
import jax
import jax.numpy as jnp
from jax import lax
from jax.experimental import pallas as pl
from jax.experimental.pallas import tpu as pltpu

F32, BF16 = jnp.float32, jnp.bfloat16

D_MODEL = 1024
D_FF = 2816
N_HEADS = 12
HEAD_DIM = 64
D_ATTN = 768
D_SSD = 768
N_GROUPS = 4
HEADS_PER_GROUP = 3
D_STATE = 128
D_CONV = 1792
CONV_WIDTH = 4
ROPE_DIM = 16
ROPE_THETA = 500000.0
ALPHA = 2.0 ** 0.25
LN_EPS = 1e-5
RMS_EPS = 1e-6
ADAM_LR, ADAM_B1, ADAM_B2, ADAM_EPS, ADAM_WD, ADAM_STEP = 0.001, 0.9, 0.999, 1e-08, 0.01, 10

LANES = 128
GATE_UP_INTERLEAVE = 256
SEQ_BLOCK = 256
GROUP_LANES = 256
VMEM_LIMIT = 56 * 1024 * 1024
NEG = -1e30
MESH = pl.DeviceIdType.MESH
HIGHEST = lax.Precision.HIGHEST

_NT = (((1,), (1,)), ((), ()))
_TN = (((0,), (0,)), ((), ()))


def _params(*sem):
    return pltpu.CompilerParams(dimension_semantics=sem, vmem_limit_bytes=VMEM_LIMIT)


def _bf(v):
    return v.astype(BF16)


EPILOGUE_ROWS = 128


def _row_chunks(tm):
    return [slice(r, min(r + EPILOGUE_ROWS, tm)) for r in range(0, tm, EPILOGUE_ROWS)]


def _sigmoid(v):
    return 0.5 * jnp.tanh(0.5 * v) + 0.5


def _mm(name, pairs, *, scale=1.0, res=None, res_scale=1.0, out_dtype=F32, tm=512, tn=512):
    m, n = pairs[0][0].shape[0], pairs[0][1].shape[1]
    tm, tn = min(tm, m), min(tn, n)
    assert m % tm == 0 and n % tn == 0, (name, m, n, tm, tn)
    npair = len(pairs)

    def body(*refs):
        acc = None
        for a_ref, b_ref in zip(refs[:npair], refs[npair:2 * npair]):
            d = jnp.dot(_bf(a_ref[...]), b_ref[...], preferred_element_type=F32)
            acc = d if acc is None else acc + d
        if scale != 1.0:
            acc = acc * scale
        if res is not None:
            acc = acc + res_scale * refs[2 * npair][...]
        refs[-1][...] = acc.astype(out_dtype)

    in_specs = [pl.BlockSpec((tm, a.shape[1]), lambda i, j: (i, 0)) for a, _ in pairs]
    in_specs += [pl.BlockSpec((b.shape[0], tn), lambda i, j: (0, j)) for _, b in pairs]
    args = [a for a, _ in pairs] + [b for _, b in pairs]
    if res is not None:
        in_specs.append(pl.BlockSpec((tm, tn), lambda i, j: (i, j)))
        args.append(res)
    return pl.pallas_call(
        body, name=name, grid=(m // tm, n // tn), in_specs=in_specs,
        out_specs=pl.BlockSpec((tm, tn), lambda i, j: (i, j)),
        out_shape=jax.ShapeDtypeStruct((m, n), out_dtype),
        compiler_params=_params("parallel", "parallel"),
    )(*args)


def _mm_tn(name, x, dy, *, scale=1.0, tk=512, tn=512, tt=1024):
    t, k = x.shape
    n = dy.shape[1]
    tk, tn, tt = min(tk, k), min(tn, n), min(tt, t)
    assert k % tk == 0 and n % tn == 0 and t % tt == 0, (name, k, n, t)
    nt = t // tt

    def body(x_ref, dy_ref, o_ref):
        step = pl.program_id(2)
        d = lax.dot_general(_bf(x_ref[...]), _bf(dy_ref[...]), _TN, preferred_element_type=F32)

        @pl.when(step == 0)
        def _():
            o_ref[...] = d

        @pl.when(step > 0)
        def _():
            o_ref[...] += d

        if scale != 1.0:
            @pl.when(step == nt - 1)
            def _():
                o_ref[...] = o_ref[...] * scale

    return pl.pallas_call(
        body, name=name, grid=(k // tk, n // tn, nt),
        in_specs=[pl.BlockSpec((tt, tk), lambda i, j, s: (s, i)), pl.BlockSpec((tt, tn), lambda i, j, s: (s, j))],
        out_specs=pl.BlockSpec((tk, tn), lambda i, j, s: (i, j)),
        out_shape=jax.ShapeDtypeStruct((k, n), F32),
        compiler_params=_params("parallel", "parallel", "arbitrary"),
    )(x, dy)


def _mm_tn_sections(name, x, dys, *, tt=512):
    t, k = x.shape
    tt = min(tt, t)
    cuts = np.cumsum([0] + [d.shape[1] for d in dys]).tolist()
    ns = len(dys)

    def body(*refs):
        x_ref, o_ref = refs[0], refs[1 + ns]
        step = pl.program_id(0)
        xt = x_ref[...].T
        parts = [jnp.dot(xt, refs[1 + a][...], preferred_element_type=F32) for a in range(ns)]

        @pl.when(step == 0)
        def _():
            for a in range(ns):
                o_ref[:, cuts[a]:cuts[a + 1]] = parts[a]

        @pl.when(step > 0)
        def _():
            for a in range(ns):
                o_ref[:, cuts[a]:cuts[a + 1]] += parts[a]

    return pl.pallas_call(
        body, name=name, grid=(t // tt,),
        in_specs=[pl.BlockSpec((tt, k), lambda s: (s, 0))] + [pl.BlockSpec((tt, d.shape[1]), lambda s: (s, 0)) for d in dys],
        out_specs=pl.BlockSpec((k, cuts[-1]), lambda s: (0, 0)),
        out_shape=jax.ShapeDtypeStruct((k, cuts[-1]), F32),
        compiler_params=_params("arbitrary"),
    )(x, *dys)


def _mm_tn_gate_up(name, x, dau, *, tt=1024):
    t, k = x.shape
    gi = GATE_UP_INTERLEAVE
    nj = dau.shape[1] // (2 * gi)
    tt = min(tt, t)
    nt = t // tt

    def body(x_ref, dy_ref, g_ref, u_ref):
        step = pl.program_id(1)
        d = lax.dot_general(_bf(x_ref[...]), dy_ref[...], _TN, preferred_element_type=F32)

        @pl.when(step == 0)
        def _():
            g_ref[...] = d[:, :gi]
            u_ref[...] = d[:, gi:]

        @pl.when(step > 0)
        def _():
            g_ref[...] += d[:, :gi]
            u_ref[...] += d[:, gi:]

    out = pl.BlockSpec((k, gi), lambda j, s: (0, j))
    return pl.pallas_call(
        body, name=name, grid=(nj, nt),
        in_specs=[pl.BlockSpec((tt, k), lambda j, s: (s, 0)), pl.BlockSpec((tt, 2 * gi), lambda j, s: (s, j))],
        out_specs=[out, out],
        out_shape=[jax.ShapeDtypeStruct((k, gi * nj), F32)] * 2,
        compiler_params=_params("parallel", "arbitrary"),
    )(x, dau)


def _carried(carry, ins, outs, sems, step, total):
    start, forward, finish = carry.phases(ins, outs, sems)
    pl.when(step == 0)(start)
    if forward is not None:
        pl.when(step == (3 * total) // 4)(forward)
    return lambda: pl.when(step == total - 1)(finish)


def _resident(shape):
    return pl.BlockSpec(shape, lambda i: (0,) * len(shape), pipeline_mode=pl.Buffered(1))


def _ffn_fwd(name, x16, res, wgu, wd, g, b, *, tm=512, carry=None):
    t, k = x16.shape
    gi = GATE_UP_INTERLEAVE
    nj, n, ni = wd.shape[0] // gi, wd.shape[1], t // tm
    nc = carry.n if carry is not None else 0

    def body(*refs):
        x_ref, res_ref, wgu_ref, wd_ref, g_ref, b_ref = refs[:6]
        au_ref, hm_ref, y_ref, r_ref, y16_ref = refs[6 + nc:11 + nc]
        if carry is not None:
            finish = _carried(carry, refs[6:6 + nc], refs[11 + nc:11 + 2 * nc], refs[11 + 2 * nc:], pl.program_id(0), ni)
        xv = x_ref[...]
        acc = jnp.zeros((tm, n), F32)
        for j in range(nj):
            au = jnp.dot(xv, wgu_ref[:, 2 * gi * j:2 * gi * (j + 1)], preferred_element_type=F32)
            a, u = au[:, :gi], au[:, gi:]
            au_ref[:, 2 * gi * j:2 * gi * (j + 1)] = _bf(au)
            hm = _bf(a * _sigmoid(a) * u)
            hm_ref[:, gi * j:gi * (j + 1)] = hm
            acc = acc + jnp.dot(hm, wd_ref[gi * j:gi * (j + 1), :], preferred_element_type=F32)
        r = ALPHA * res_ref[...] + 0.5 * acc
        r_ref[...] = r
        y = _layer_norm(r, g_ref[...], b_ref[...])
        y_ref[...] = y
        y16_ref[...] = _bf(y)
        if carry is not None:
            finish()

    row = lambda c: pl.BlockSpec((tm, c), lambda i: (i, 0))
    hbm = pl.BlockSpec(memory_space=pltpu.HBM)
    res_ = pl.pallas_call(
        body, name=name, grid=(ni,),
        in_specs=[row(k), row(n), _resident(wgu.shape), _resident(wd.shape), _resident(g.shape), _resident(b.shape)] + [hbm] * nc,
        out_specs=[row(2 * gi * nj), row(gi * nj), row(n), row(n), row(n)] + [hbm] * nc,
        out_shape=[jax.ShapeDtypeStruct((t, 2 * gi * nj), BF16), jax.ShapeDtypeStruct((t, gi * nj), BF16),
                   jax.ShapeDtypeStruct((t, n), F32), jax.ShapeDtypeStruct((t, n), F32), jax.ShapeDtypeStruct((t, n), BF16)]
        + (carry.out_shape if carry is not None else []),
        scratch_shapes=carry.scratch_shapes if carry is not None else [],
        compiler_params=_params("arbitrary" if carry is not None else "parallel"),
    )(x16, res, wgu, wd, g, b, *(carry.operands if carry is not None else []))
    return tuple(res_[:5]) + ((carry.results(res_[5:]),) if carry is not None else ())


def _ffn_bwd(name, dr16, dr, wdt, au, wgut, *, tm=512, carry=None):
    t, n = dr16.shape
    gi = GATE_UP_INTERLEAVE
    nj, ni = wdt.shape[1] // gi, t // tm
    nc = carry.n if carry is not None else 0

    def body(*refs):
        dr16_ref, dr_ref, wdt_ref, au_ref, wgut_ref = refs[:5]
        dau_ref, dx_ref = refs[5 + nc:7 + nc]
        if carry is not None:
            finish = _carried(carry, refs[5:5 + nc], refs[7 + nc:7 + 2 * nc], refs[7 + 2 * nc:], pl.program_id(0), ni)
        drv = dr16_ref[...]
        acc = jnp.zeros((tm, n), F32)
        for j in range(nj):
            dhm = jnp.dot(drv, wdt_ref[:, gi * j:gi * (j + 1)], preferred_element_type=F32) * 0.5
            au_v = au_ref[:, 2 * gi * j:2 * gi * (j + 1)].astype(F32)
            a, u = au_v[:, :gi], au_v[:, gi:]
            sig = _sigmoid(a)
            silu = a * sig
            dau = jnp.concatenate([_bf(dhm * u * (sig + silu - silu * sig)), _bf(dhm * silu)], axis=1)
            dau_ref[:, 2 * gi * j:2 * gi * (j + 1)] = dau
            acc = acc + jnp.dot(dau, wgut_ref[2 * gi * j:2 * gi * (j + 1), :], preferred_element_type=F32)
        dx_ref[...] = ALPHA * dr_ref[...] + acc
        if carry is not None:
            finish()

    row = lambda c: pl.BlockSpec((tm, c), lambda i: (i, 0))
    hbm = pl.BlockSpec(memory_space=pltpu.HBM)
    res_ = pl.pallas_call(
        body, name=name, grid=(ni,),
        in_specs=[row(n), row(n), _resident(wdt.shape), row(2 * gi * nj), _resident(wgut.shape)] + [hbm] * nc,
        out_specs=[row(2 * gi * nj), row(n)] + [hbm] * nc,
        out_shape=[jax.ShapeDtypeStruct((t, 2 * gi * nj), BF16), jax.ShapeDtypeStruct((t, n), F32)]
        + (carry.out_shape if carry is not None else []),
        scratch_shapes=carry.scratch_shapes if carry is not None else [],
        compiler_params=_params("arbitrary" if carry is not None else "parallel"),
    )(dr16, dr, wdt, au, wgut, *(carry.operands if carry is not None else []))
    return tuple(res_[:2]) + ((carry.results(res_[2:]),) if carry is not None else ())


def _layer_norm(r, g, b):
    mu = jnp.mean(r, axis=-1, keepdims=True)
    var = jnp.mean(jnp.square(r - mu), axis=-1, keepdims=True)
    return (r - mu) * lax.rsqrt(var + LN_EPS) * g + b


def _mm_res_ln(name, a, w, res, g, b, *, scale, tm=256):
    t, k = a.shape
    n = w.shape[1]

    def body(a_ref, w_ref, res_ref, g_ref, b_ref, y_ref, r_ref, y16_ref):
        for rows in _row_chunks(tm):
            r = ALPHA * res_ref[rows, :] + scale * jnp.dot(_bf(a_ref[rows, :]), w_ref[...], preferred_element_type=F32)
            r_ref[rows, :] = r
            y = _layer_norm(r, g_ref[...], b_ref[...])
            y_ref[rows, :] = y
            y16_ref[rows, :] = _bf(y)

    row = lambda c: pl.BlockSpec((tm, c), lambda i: (i, 0))
    const = lambda shape: pl.BlockSpec(shape, lambda i: (0, 0))
    return pl.pallas_call(
        body, name=name, grid=(t // tm,),
        in_specs=[row(k), const((k, n)), row(n), const((1, n)), const((1, n))],
        out_specs=[row(n), row(n), row(n)],
        out_shape=[jax.ShapeDtypeStruct((t, n), F32), jax.ShapeDtypeStruct((t, n), F32), jax.ShapeDtypeStruct((t, n), BF16)],
        compiler_params=_params("parallel"),
    )(a, w, res, g, b)


def _rowwise(name, fn, rows, consts, row_outs, acc_outs=(), tm=512):
    rows = [r if isinstance(r, tuple) else (r, r.shape[1]) for r in rows]
    t = rows[0][0].shape[0]
    tm = min(tm, t)
    assert t % tm == 0
    nr, nc, no, na = len(rows), len(consts), len(row_outs), len(acc_outs)

    def body(*refs):
        vals = [r[...] for r in refs[:nr + nc]]
        outs, accs = fn(*vals)
        for o_ref, o in zip(refs[nr + nc:nr + nc + no], outs):
            o_ref[...] = o.astype(o_ref.dtype)
        if na:
            step = pl.program_id(0)
            acc_refs = refs[nr + nc + no:]

            @pl.when(step == 0)
            def _():
                for a_ref, a in zip(acc_refs, accs):
                    a_ref[...] = a

            @pl.when(step > 0)
            def _():
                for a_ref, a in zip(acc_refs, accs):
                    a_ref[...] += a

    in_specs = [pl.BlockSpec((tm, w), lambda i: (i, 0)) for _, w in rows]
    in_specs += [pl.BlockSpec(c.shape, lambda i, nd=c.ndim: (0,) * nd) for c in consts]
    out_specs = [pl.BlockSpec((tm, c), lambda i: (i, 0)) for c, _ in row_outs]
    out_specs += [pl.BlockSpec(s, lambda i: (0, 0)) for s in acc_outs]
    out_shape = [jax.ShapeDtypeStruct((t, c), dt) for c, dt in row_outs]
    out_shape += [jax.ShapeDtypeStruct(s, F32) for s in acc_outs]
    res = pl.pallas_call(
        body, name=name, grid=(t // tm,), in_specs=in_specs, out_specs=out_specs, out_shape=out_shape,
        compiler_params=_params("arbitrary" if na else "parallel"),
    )(*[r for r, _ in rows], *consts)
    return res


def _ln_bwd(name, r, g, b, dy):
    def fn(r_v, dy_v, g_v, b_v):
        _, vjp = jax.vjp(_layer_norm, r_v, g_v, b_v)
        dr, dg, db = vjp(dy_v)
        return [dr, dr], [dg, db]
    return _rowwise(name, fn, [r, dy], [g, b], [(r.shape[1], F32), (r.shape[1], BF16)], [(1, r.shape[1])] * 2)


def _ln_loss_bwd(name, r, g, b, target):
    def fn(r_v, t_v, g_v, b_v):
        def loss_fn(rr, gg, bb):
            err = jnp.square(_layer_norm(rr, gg, bb) - t_v)
            return 0.5 * jnp.sum(jnp.mean(err, axis=-1, keepdims=True), axis=0, keepdims=True)
        loss, vjp = jax.vjp(loss_fn, r_v, g_v, b_v)
        dr, dg, db = vjp(jnp.ones((1, 1), F32))
        return [dr, dr], [dg, db, jnp.broadcast_to(loss, (1, LANES))]
    return _rowwise(name, fn, [r, target], [g, b], [(r.shape[1], F32), (r.shape[1], BF16)],
                    [(1, r.shape[1])] * 2 + [(1, LANES)])


def _rope_tables(posf, invf, sgn):
    ang = posf * invf
    return jnp.cos(ang), jnp.sin(ang) * sgn


def _rope_apply(tv, cos, sin):
    lane = lax.broadcasted_iota(jnp.int32, cos.shape, 1)
    first = (lane % HEAD_DIM) < (ROPE_DIM // 2)
    outs = []
    for gidx in range(tv.shape[1] // LANES):
        tg = tv[:, LANES * gidx:LANES * (gidx + 1)]
        sw = jnp.where(first, pltpu.roll(tg, LANES - ROPE_DIM // 2, 1), pltpu.roll(tg, ROPE_DIM // 2, 1))
        outs.append(tg * cos + sw * sin)
    return jnp.concatenate(outs, axis=1)


def _proj_in(h16, w_in, posf, invf, sgn, *, tm=512, carry=None):
    t, k = h16.shape
    cuts = [0, D_ATTN, 2 * D_ATTN, 3 * D_ATTN, 3 * D_ATTN + D_SSD, 3 * D_ATTN + D_SSD + D_CONV, w_in.shape[1]]
    nc = carry.n if carry is not None else 0

    def body(*refs):
        h_ref, w_ref, pos_ref, invf_ref, sgn_ref = refs[:5]
        q_ref, k_ref, v_ref, z_ref, xbc_ref, dt_ref, cs_ref = refs[5 + nc:12 + nc]
        if carry is not None:
            finish = _carried(carry, refs[5:5 + nc], refs[12 + nc:12 + 2 * nc], refs[12 + 2 * nc:], pl.program_id(0), t // tm)
        hv = h_ref[...]
        part = lambda a: jnp.dot(hv, w_ref[:, cuts[a]:cuts[a + 1]], preferred_element_type=F32)
        cos, sin = _rope_tables(pos_ref[...], invf_ref[...], sgn_ref[...])
        cs_ref[...] = jnp.concatenate([cos, sin], axis=1)
        q_ref[...] = _bf(_rope_apply(part(0), cos, sin) * (HEAD_DIM ** -0.5))
        k_ref[...] = _bf(_rope_apply(part(1), cos, sin))
        v_ref[...] = _bf(part(2))
        z_ref[...] = part(3)
        xbc_ref[...] = part(4)
        dt_ref[...] = part(5)
        if carry is not None:
            finish()

    row = lambda c: pl.BlockSpec((tm, c), lambda i: (i, 0))
    hbm = pl.BlockSpec(memory_space=pltpu.HBM)
    widths = [D_ATTN, D_ATTN, D_ATTN, D_SSD, D_CONV, LANES, 2 * LANES]
    dtypes = [BF16, BF16, BF16, F32, F32, F32, F32]
    res = pl.pallas_call(
        body, name="proj_in", grid=(t // tm,),
        in_specs=[row(k), _resident(w_in.shape), row(1), _resident(invf.shape), _resident(sgn.shape)] + [hbm] * nc,
        out_specs=[row(c) for c in widths] + [hbm] * nc,
        out_shape=[jax.ShapeDtypeStruct((t, c), dt) for c, dt in zip(widths, dtypes)]
        + (carry.out_shape if carry is not None else []),
        scratch_shapes=carry.scratch_shapes if carry is not None else [],
        compiler_params=_params("arbitrary" if carry is not None else "parallel"),
    )(h16, w_in, posf, invf, sgn, *(carry.operands if carry is not None else []))
    return tuple(res[:7]) + ((carry.results(res[7:]),) if carry is not None else ())


def _rope_bwd(dq, dk, cs):
    def fn(dq_v, dk_v, cs_v):
        cos, sin = cs_v[:, :LANES], -cs_v[:, LANES:]
        gq = _rope_apply(dq_v * (HEAD_DIM ** -0.5), cos, sin)
        gk = _rope_apply(dk_v, cos, sin)
        return [jnp.concatenate([gq, gk], axis=1)], []
    return _rowwise("rope_bwd", fn, [dq, dk, cs], [], [(2 * D_ATTN, BF16)])[0]


def _rms(v, w):
    return v * lax.rsqrt(jnp.mean(v * v, axis=-1, keepdims=True) + RMS_EPS) * w


def _ungroup(yg):
    w = HEADS_PER_GROUP * HEAD_DIM
    return jnp.concatenate([yg[:, GROUP_LANES * g:GROUP_LANES * g + w] for g in range(N_GROUPS)], axis=1)


def _group(xs):
    w = HEADS_PER_GROUP * HEAD_DIM
    parts = []
    for g in range(N_GROUPS):
        parts += [xs[:, w * g:w * (g + 1)], jnp.zeros((xs.shape[0], GROUP_LANES - w), xs.dtype)]
    return jnp.concatenate(parts, axis=1)


def _norms_fn(attn, yg, xs, z, w_attn, w_ssd, dskip):
    a_n = _rms(attn, w_attn)
    y = _ungroup(yg) + dskip * xs
    y_n = _rms(y * (z * jax.nn.sigmoid(z)), w_ssd)
    return jnp.concatenate([a_n, y_n], axis=1)


def _norms_fwd(attn, yg, xbc, z, w_attn, w_ssd, dskip):
    def fn(*v):
        return [_norms_fn(*v)], []
    return _rowwise("norms_fwd", fn, [attn, yg, (xbc, D_SSD), z], [w_attn, w_ssd, dskip], [(D_ATTN + D_SSD, BF16)])[0]


def _norms_bwd(attn, yg, xbc, z, w_attn, w_ssd, dskip, dcat):
    def fn(attn_v, yg_v, xs_v, z_v, dcat_v, wa_v, ws_v, dk_v):
        _, vjp = jax.vjp(_norms_fn, attn_v, yg_v, xs_v, z_v, wa_v, ws_v, dk_v)
        d_attn, d_yg, d_xs, d_z, d_wa, d_ws, d_dk = vjp(dcat_v)
        return [d_attn, d_yg, d_xs, d_z], [d_wa, d_ws, d_dk]
    return _rowwise("norms_bwd", fn, [attn, yg, (xbc, D_SSD), z, dcat], [w_attn, w_ssd, dskip],
                    [(D_ATTN, F32), (N_GROUPS * GROUP_LANES, F32), (D_SSD, F32), (D_SSD, BF16)], [(1, D_SSD)] * 3)


def _spread_sum(v, e):
    h1 = _bf(v)
    r1 = v - h1.astype(F32)
    h2 = _bf(r1)
    h3 = _bf(r1 - h2.astype(F32))
    return sum(jnp.dot(h, e, preferred_element_type=F32) for h in (h1, h2, h3))


@jax.custom_vjp
def _spread(v, e, e_t):
    return _spread_sum(v, e)


def _spread_fwd(v, e, e_t):
    return _spread_sum(v, e), (e, e_t)


def _spread_bwd(saved, g):
    e, e_t = saved
    return _spread_sum(g, e_t), jnp.zeros_like(e), jnp.zeros_like(e_t)


_spread.defvjp(_spread_fwd, _spread_bwd)


def _ssd_prep_fn(xs, dtp, dtb, alog, e_x, e_xt, e_a, e_at):
    dt = jax.nn.softplus(dtp + dtb)
    a = -jnp.exp(alog)
    xdtg = _group(xs) * _spread(dt, e_x, e_xt)
    dag = _spread(dt * a, e_a, e_at)
    return xdtg, dag


def _ssd_prep_fwd(xbc, dtp, dtb, alog, spreaders):
    def fn(xbc_v, dtp_v, dtb_v, alog_v, *e_v):
        xdtg, dag = _ssd_prep_fn(xbc_v[:, :D_SSD], dtp_v, dtb_v, alog_v, *e_v)
        return [xdtg, xbc_v[:, D_SSD:], dag], []
    return _rowwise("ssd_prep_fwd", fn, [xbc, dtp], [dtb, alog, *spreaders],
                    [(N_GROUPS * GROUP_LANES, BF16), (D_CONV - D_SSD, BF16), (N_GROUPS * LANES, F32)])


def _ssd_prep_bwd(xbc, dtp, dtb, alog, spreaders, dxdtg, ddag, dxs_a, db, dc):
    def fn(xs_v, dtp_v, dxdtg_v, ddag_v, dxs_a_v, db_v, dc_v, dtb_v, alog_v, *e_v):
        _, vjp = jax.vjp(lambda a, b, c, d: _ssd_prep_fn(a, b, c, d, *e_v), xs_v, dtp_v, dtb_v, alog_v)
        dxs, ddtp, ddtb, dalog = vjp((dxdtg_v, ddag_v))
        return [jnp.concatenate([dxs + dxs_a_v, db_v, dc_v], axis=1), ddtp], [ddtb, dalog]
    return _rowwise("ssd_prep_bwd", fn, [(xbc, D_SSD), dtp, dxdtg, ddag, dxs_a, db, dc], [dtb, alog, *spreaders],
                    [(D_CONV, F32), (LANES, BF16)], [(1, LANES)] * 2)


def _shift_down(u, d):
    if d == 0:
        return u
    row = lax.broadcasted_iota(jnp.int32, u.shape, 0)
    return jnp.where(row >= d, pltpu.roll(u, d, 0), 0.0)


def _shift_up(u, d):
    if d == 0:
        return u
    s = u.shape[0]
    row = lax.broadcasted_iota(jnp.int32, u.shape, 0)
    return jnp.where(row < s - d, pltpu.roll(u, s - d, 0), 0.0)


def _conv_pre(u, w, b):
    acc = b
    for k in range(CONV_WIDTH):
        acc = acc + w[k:k + 1, :] * _shift_down(u, CONV_WIDTH - 1 - k)
    return acc


def _conv_fwd(u, w, b, *, tc=256):
    nb, s, c = u.shape

    def body(u_ref, w_ref, b_ref, o_ref):
        pre = _conv_pre(u_ref[0], w_ref[...], b_ref[...])
        o_ref[0] = pre * jax.nn.sigmoid(pre)

    return pl.pallas_call(
        body, name="conv_fwd", grid=(c // tc, nb),
        in_specs=[pl.BlockSpec((1, s, tc), lambda j, i: (i, 0, j)), pl.BlockSpec((CONV_WIDTH, tc), lambda j, i: (0, j)),
                  pl.BlockSpec((1, tc), lambda j, i: (0, j))],
        out_specs=pl.BlockSpec((1, s, tc), lambda j, i: (i, 0, j)),
        out_shape=jax.ShapeDtypeStruct((nb, s, c), F32),
        compiler_params=_params("parallel", "parallel"),
    )(u, w, b)


def _conv_bwd(u, w, b, dout, *, tc=256):
    nb, s, c = u.shape

    def body(u_ref, w_ref, b_ref, d_ref, du_ref, dw_ref, db_ref):
        uv, wv = u_ref[0], w_ref[...]
        pre = _conv_pre(uv, wv, b_ref[...])
        sig = jax.nn.sigmoid(pre)
        dpre = d_ref[0] * (sig * (1.0 + pre * (1.0 - sig)))
        du = jnp.zeros_like(uv)
        dws = []
        for k in range(CONV_WIDTH):
            du = du + wv[k:k + 1, :] * _shift_up(dpre, CONV_WIDTH - 1 - k)
            dws.append(jnp.sum(dpre * _shift_down(uv, CONV_WIDTH - 1 - k), axis=0, keepdims=True))
        du_ref[0] = _bf(du)
        dwv = jnp.concatenate(dws + [jnp.zeros((8 - CONV_WIDTH, tc), F32)], axis=0)
        dbv = jnp.sum(dpre, axis=0, keepdims=True)
        first = pl.program_id(1) == 0

        @pl.when(first)
        def _():
            dw_ref[...] = dwv
            db_ref[...] = dbv

        @pl.when(jnp.logical_not(first))
        def _():
            dw_ref[...] += dwv
            db_ref[...] += dbv

    blk = pl.BlockSpec((1, s, tc), lambda j, i: (i, 0, j))
    return pl.pallas_call(
        body, name="conv_bwd", grid=(c // tc, nb),
        in_specs=[blk, pl.BlockSpec((CONV_WIDTH, tc), lambda j, i: (0, j)), pl.BlockSpec((1, tc), lambda j, i: (0, j)), blk],
        out_specs=[blk, pl.BlockSpec((8, tc), lambda j, i: (0, j)), pl.BlockSpec((1, tc), lambda j, i: (0, j))],
        out_shape=[jax.ShapeDtypeStruct((nb, s, c), BF16), jax.ShapeDtypeStruct((8, c), F32), jax.ShapeDtypeStruct((1, c), F32)],
        compiler_params=_params("parallel", "arbitrary"),
    )(u, w, b, dout)


FWD_KEY_BLOCK = 256


def _branch_bias_table(seq, kb):
    ratio = SEQ_BLOCK // kb
    key = np.arange(kb)[None, :, None]
    query = np.arange(SEQ_BLOCK)[None, None, :]
    delta = (np.arange(seq // kb)[:, None, None] - (ratio - 1)) * kb + query - key
    cnt = np.zeros(delta.shape, np.float64)
    for window, dilation in ((128, 1), (512, 4), (2048, 16)):
        cnt += (delta >= 0) & (delta % dilation == 0) & (delta <= window)
    return jnp.asarray(np.where(cnt > 0, np.log(np.maximum(cnt, 1.0)), NEG).astype(np.float32))


HEADS_PER_BLOCK = LANES // HEAD_DIM


def _head_rows(v, h):
    row = lax.broadcasted_iota(jnp.int32, v.shape, 0)
    return jnp.where((row >= HEAD_DIM * h) & (row < HEAD_DIM * (h + 1)), v, jnp.zeros_like(v))


def _attn_fwd(q, k, v, bias):
    nb_, s, _ = q.shape
    ab, kb = SEQ_BLOCK, FWD_KEY_BLOCK
    nblk, nkb, ratio = s // ab, s // kb, ab // kb

    def body(q_ref, k_ref, v_ref, b_ref, o_ref, lse_ref, vt_s):
        i = pl.program_id(2)

        @pl.when(i == 0)
        def _():
            for jb in range(nkb):
                vt_s[jb] = v_ref[0, kb * jb:kb * (jb + 1), :].T

        qt = q_ref[0].T
        qts = [_head_rows(qt, h) for h in range(HEADS_PER_BLOCK)]

        last = ratio * (i + 1) - 1

        def scores(j):
            kj = k_ref[0, pl.ds(pl.multiple_of(j * kb, kb), kb), :]
            return [jnp.dot(kj, qts[h], preferred_element_type=F32) for h in range(HEADS_PER_BLOCK)]

        def step(j, carry):
            ahead = scores(jnp.minimum(j + 1, last))
            lb = b_ref[ratio * i - j + (ratio - 1)]
            out = []
            for h in range(HEADS_PER_BLOCK):
                m, l, acc = carry[3 * h:3 * h + 3]
                st = carry[3 * HEADS_PER_BLOCK + h] + lb
                m_new = jnp.maximum(m, jnp.max(st, axis=0, keepdims=True))
                p = jnp.exp(st - m_new)
                a = jnp.exp(m - m_new)
                l = a * l + jnp.sum(p, axis=0, keepdims=True)
                vt = vt_s[j, HEAD_DIM * h:HEAD_DIM * (h + 1), :]
                acc = a * acc + jnp.dot(vt, _bf(p), preferred_element_type=F32)
                out += [m_new, l, acc]
            return tuple(out) + tuple(ahead)

        init = (jnp.full((1, ab), NEG, F32), jnp.zeros((1, ab), F32), jnp.zeros((HEAD_DIM, ab), F32)) * HEADS_PER_BLOCK
        res = lax.fori_loop(0, ratio * (i + 1), step, init + tuple(scores(0)))
        ot = jnp.concatenate([res[3 * h + 2] / res[3 * h + 1] for h in range(HEADS_PER_BLOCK)], axis=0)
        o_ref[0] = ot.T
        rows = [res[3 * h] + jnp.log(res[3 * h + 1]) for h in range(HEADS_PER_BLOCK)]
        lse_ref[0, 0, 0] = jnp.concatenate(rows + [jnp.zeros((8 - HEADS_PER_BLOCK, ab), F32)], axis=0)

    qblk = pl.BlockSpec((1, ab, LANES), lambda b, hp, i: (b, i, hp))
    full = pl.BlockSpec((1, s, LANES), lambda b, hp, i: (b, 0, hp))
    return pl.pallas_call(
        body, name="attn_fwd", grid=(nb_, D_ATTN // LANES, nblk),
        in_specs=[qblk, full, full, pl.BlockSpec((nkb, kb, ab), lambda b, hp, i: (0, 0, 0))],
        out_specs=[qblk, pl.BlockSpec((1, 1, 1, 8, ab), lambda b, hp, i: (b, hp, i, 0, 0))],
        out_shape=[jax.ShapeDtypeStruct((nb_, s, D_ATTN), F32),
                   jax.ShapeDtypeStruct((nb_, D_ATTN // LANES, nblk, 8, ab), F32)],
        scratch_shapes=[pltpu.VMEM((nkb, LANES, kb), BF16)],
        compiler_params=_params("parallel", "parallel", "arbitrary"),
    )(q, k, v, bias)


def _attn_bwd(q, k, v, o, do, lse, bias):
    nb_, s, _ = q.shape
    ab = SEQ_BLOCK
    nblk = s // ab

    nh = HEADS_PER_BLOCK

    def body(q_ref, k_ref, v_ref, o_ref, do_ref, lse_ref, b_ref, dq_ref, dk_ref, dv_ref,
             qt_s, dot_s, kt_s, dqt_s, do16_s, d_s, dk_acc, dv_acc):
        for jb in range(nblk):
            sl = slice(ab * jb, ab * (jb + 1))
            qt, kt = q_ref[0, sl, :].T, k_ref[0, sl, :].T
            do = do_ref[0, sl, :]
            dot = do.T
            prod = dot * o_ref[0, sl, :].T
            do16_s[sl, :] = _bf(do)
            for h in range(nh):
                qt_s[nh * jb + h] = _head_rows(qt, h)
                kt_s[nh * jb + h] = _head_rows(kt, h)
                dot_s[nh * jb + h] = _head_rows(_bf(dot), h)
            d_s[jb] = jnp.concatenate(
                [jnp.sum(prod[HEAD_DIM * h:HEAD_DIM * (h + 1)], axis=0, keepdims=True) for h in range(nh)]
                + [jnp.zeros((8 - nh, ab), F32)], axis=0)
            dqt_s[jb] = jnp.zeros((LANES, ab), F32)

        def outer(j, carry):
            ks = pl.ds(pl.multiple_of(j * ab, ab), ab)
            kj, vj = k_ref[0, ks, :], v_ref[0, ks, :]
            dk_acc[...] = jnp.zeros_like(dk_acc)
            dv_acc[...] = jnp.zeros_like(dv_acc)

            def inner(i, c2):
                qs = pl.ds(pl.multiple_of(i * ab, ab), ab)
                qi, doi = q_ref[0, qs, :], do16_s[qs, :]
                lb = b_ref[i - j]
                for h in range(nh):
                    st = jnp.dot(kj, qt_s[nh * i + h], preferred_element_type=F32) + lb
                    pt = jnp.exp(st - lse_ref[0, 0, i, h:h + 1, :])
                    dpt = jnp.dot(vj, dot_s[nh * i + h], preferred_element_type=F32)
                    dst16 = _bf(pt * (dpt - d_s[i, h:h + 1, :]))
                    dv_acc[h] += jnp.dot(_bf(pt), doi, preferred_element_type=F32)
                    dk_acc[h] += jnp.dot(dst16, qi, preferred_element_type=F32)
                    dqt_s[i] += jnp.dot(kt_s[nh * j + h], dst16, preferred_element_type=F32)
                return c2

            lax.fori_loop(j, nblk, inner, 0)
            lane = lax.broadcasted_iota(jnp.int32, (ab, LANES), 1)
            dk_ref[0, ks, :] = jnp.where(lane < HEAD_DIM, dk_acc[0], dk_acc[1])
            dv_ref[0, ks, :] = _bf(jnp.where(lane < HEAD_DIM, dv_acc[0], dv_acc[1]))
            return carry

        lax.fori_loop(0, nblk, outer, 0)
        for jb in range(nblk):
            dq_ref[0, ab * jb:ab * (jb + 1), :] = dqt_s[jb].T

    assert nh == 2
    full = pl.BlockSpec((1, s, LANES), lambda b, hp: (b, 0, hp))
    return pl.pallas_call(
        body, name="attn_bwd", grid=(nb_, D_ATTN // LANES),
        in_specs=[full] * 5 + [pl.BlockSpec((1, 1, nblk, 8, ab), lambda b, hp: (b, hp, 0, 0, 0)),
                               pl.BlockSpec((nblk, ab, ab), lambda b, hp: (0, 0, 0))],
        out_specs=[full, full, full],
        out_shape=[jax.ShapeDtypeStruct((nb_, s, D_ATTN), F32), jax.ShapeDtypeStruct((nb_, s, D_ATTN), F32),
                   jax.ShapeDtypeStruct((nb_, s, D_ATTN), BF16)],
        scratch_shapes=[pltpu.VMEM((nh * nblk, LANES, ab), BF16), pltpu.VMEM((nh * nblk, LANES, ab), BF16),
                        pltpu.VMEM((nh * nblk, LANES, ab), BF16), pltpu.VMEM((nblk, LANES, ab), F32),
                        pltpu.VMEM((s, LANES), BF16), pltpu.VMEM((nblk, 8, ab), F32),
                        pltpu.VMEM((nh, ab, LANES), F32), pltpu.VMEM((nh, ab, LANES), F32)],
        compiler_params=_params("parallel", "parallel"),
    )(q, k, v, o, do, lse, bias)


def _cumsum_fwd(dag):
    nb_, s, c = dag.shape
    ab = SEQ_BLOCK

    def body(a_ref, o_ref, ot_ref):
        r = lax.broadcasted_iota(jnp.int32, (ab, ab), 0)
        cc = lax.broadcasted_iota(jnp.int32, (ab, ab), 1)
        tri = (r >= cc).astype(F32)
        carry = jnp.zeros((1, c), F32)
        for i in range(s // ab):
            loc = jnp.dot(tri, a_ref[0, ab * i:ab * (i + 1), :], precision=HIGHEST, preferred_element_type=F32) + carry
            o_ref[0, ab * i:ab * (i + 1), :] = loc
            ot_ref[0, :, ab * i:ab * (i + 1)] = loc.T
            carry = loc[ab - 1:ab, :]

    return pl.pallas_call(
        body, name="ssd_cumsum", grid=(nb_,),
        in_specs=[pl.BlockSpec((1, s, c), lambda b: (b, 0, 0))],
        out_specs=[pl.BlockSpec((1, s, c), lambda b: (b, 0, 0)), pl.BlockSpec((1, c, s), lambda b: (b, 0, 0))],
        out_shape=[jax.ShapeDtypeStruct((nb_, s, c), F32), jax.ShapeDtypeStruct((nb_, c, s), F32)],
        compiler_params=_params("parallel"),
    )(dag)


def _cumsum_bwd(dcol, drow):
    nb_, s, c = dcol.shape
    ab = SEQ_BLOCK

    def body(c_ref, r_ref, o_ref):
        r = lax.broadcasted_iota(jnp.int32, (ab, ab), 0)
        cc = lax.broadcasted_iota(jnp.int32, (ab, ab), 1)
        tri = (r <= cc).astype(F32)
        carry = jnp.zeros((1, c), F32)
        for i in reversed(range(s // ab)):
            rows = r_ref[0, :, ab * i:ab * (i + 1)].T
            parts = []
            for g in range(N_GROUPS):
                parts += [rows[:, 8 * g:8 * (g + 1)], jnp.zeros((ab, LANES - 8), F32)]
            blk = c_ref[0, ab * i:ab * (i + 1), :] + jnp.concatenate(parts, axis=1)
            loc = jnp.dot(tri, blk, precision=HIGHEST, preferred_element_type=F32) + carry
            o_ref[0, ab * i:ab * (i + 1), :] = loc
            carry = loc[0:1, :]

    return pl.pallas_call(
        body, name="ssd_cumsum_bwd", grid=(nb_,),
        in_specs=[pl.BlockSpec((1, s, c), lambda b: (b, 0, 0)), pl.BlockSpec((1, N_GROUPS * 8, s), lambda b: (b, 0, 0))],
        out_specs=pl.BlockSpec((1, s, c), lambda b: (b, 0, 0)),
        out_shape=jax.ShapeDtypeStruct((nb_, s, c), F32),
        compiler_params=_params("parallel"),
    )(dcol, drow)


def _causal_ok(i, j):
    ab = SEQ_BLOCK
    r = lax.broadcasted_iota(jnp.int32, (ab, ab), 0)
    c = lax.broadcasted_iota(jnp.int32, (ab, ab), 1)
    return (r + (i - j) * ab) >= c


def _causal_ok_t(i, j):
    ab = SEQ_BLOCK
    r = lax.broadcasted_iota(jnp.int32, (ab, ab), 0)
    c = lax.broadcasted_iota(jnp.int32, (ab, ab), 1)
    return (c + (i - j) * ab) >= r


def _ssd_chunk(s_in, x, bm_t, cm, cb, acol, arow, a_prev, ok):
    q = x.shape[0]
    decay = jnp.exp(jnp.where(ok, acol - arow, NEG))
    y = jnp.dot(_bf(cb * decay), x, preferred_element_type=F32)
    y = y + jnp.exp(acol - a_prev) * jnp.dot(cm, _bf(s_in), preferred_element_type=F32)
    a_end = acol[q - 1:q, :]
    wx = _bf(jnp.exp(a_end - acol) * x.astype(F32))
    s_out = jnp.exp(a_end - a_prev) * s_in + jnp.dot(bm_t, wx, preferred_element_type=F32)
    return y, s_out


def _ssd_specs(s):
    xblk = pl.BlockSpec((1, s, GROUP_LANES), lambda b, g: (b, 0, g))
    bblk = pl.BlockSpec((1, s, D_STATE), lambda b, g: (b, 0, g))
    cblk = pl.BlockSpec((1, s, D_STATE), lambda b, g: (b, 0, N_GROUPS + g))
    tblk = pl.BlockSpec((1, 8, s), lambda b, g: (b, (LANES // 8) * g, 0))
    return xblk, bblk, cblk, tblk


def _chunk_views(i, j, x_ref, ac_ref, at_ref):
    ab = SEQ_BLOCK
    sl = slice(ab * i, ab * (i + 1))
    hs = slice(HEAD_DIM * j, HEAD_DIM * (j + 1))
    a_prev = jnp.zeros((1, 1), F32) if i == 0 else ac_ref[0, ab * i - 1:ab * i, j:j + 1]
    return sl, hs, ac_ref[0, sl, j:j + 1], at_ref[0, j:j + 1, sl], a_prev


def _ssd_fwd_chunked(xdtg, bc, acum, acum_t):
    nb_, s, _ = xdtg.shape
    ab = SEQ_BLOCK
    hpg = HEADS_PER_GROUP

    def body(x_ref, b_ref, c_ref, ac_ref, at_ref, y_ref):
        ok = _causal_ok(0, 0)
        states = [jnp.zeros((D_STATE, HEAD_DIM), F32) for _ in range(hpg)]
        for i in range(s // ab):
            bm, cm = b_ref[0, ab * i:ab * (i + 1), :], c_ref[0, ab * i:ab * (i + 1), :]
            bm_t = bm.T
            cb = jnp.dot(cm, bm_t, preferred_element_type=F32)
            ys = []
            for j in range(hpg):
                sl, hs, acol, arow, a_prev = _chunk_views(i, j, x_ref, ac_ref, at_ref)
                y, states[j] = _ssd_chunk(states[j], x_ref[0, sl, hs], bm_t, cm, cb, acol, arow, a_prev, ok)
                ys.append(y)
            y_ref[0, sl, :] = jnp.concatenate(ys + [jnp.zeros((ab, GROUP_LANES - hpg * HEAD_DIM), F32)], axis=1)

    xblk, bblk, cblk, tblk = _ssd_specs(s)
    ablk = pl.BlockSpec((1, s, LANES), lambda b, g: (b, 0, g))
    return pl.pallas_call(
        body, name="ssd_fwd", grid=(nb_, N_GROUPS), in_specs=[xblk, bblk, cblk, ablk, tblk], out_specs=xblk,
        out_shape=jax.ShapeDtypeStruct((nb_, s, N_GROUPS * GROUP_LANES), F32),
        compiler_params=_params("parallel", "parallel"),
    )(xdtg, bc, bc, acum, acum_t)


def _ssd_bwd_chunked(xdtg, bc, acum, acum_t, dyg):
    nb_, s, _ = xdtg.shape
    ab = SEQ_BLOCK
    nblk = s // ab
    hpg = HEADS_PER_GROUP

    def body(x_ref, b_ref, c_ref, ac_ref, at_ref, dy_ref, dx_ref, db_ref, dc_ref, dac_ref, dar_ref, s_s):
        ok = _causal_ok(0, 0)
        dx_ref[...] = jnp.zeros_like(dx_ref)
        dac_ref[...] = jnp.zeros_like(dac_ref)
        dar_ref[...] = jnp.zeros_like(dar_ref)
        states = [jnp.zeros((D_STATE, HEAD_DIM), F32) for _ in range(hpg)]
        for i in range(nblk):
            bm_t = b_ref[0, ab * i:ab * (i + 1), :].T
            for j in range(hpg):
                sl, hs, acol, arow, a_prev = _chunk_views(i, j, x_ref, ac_ref, at_ref)
                s_s[hpg * i + j] = states[j]
                if i + 1 < nblk:
                    a_end = acol[ab - 1:ab, :]
                    wx = _bf(jnp.exp(a_end - acol) * x_ref[0, sl, hs].astype(F32))
                    states[j] = jnp.exp(a_end - a_prev) * states[j] + jnp.dot(bm_t, wx, preferred_element_type=F32)
        ok_t = _causal_ok_t(0, 0)
        last_row = lax.broadcasted_iota(jnp.int32, (ab, 1), 0) == ab - 1
        d_state = [jnp.zeros((D_STATE, HEAD_DIM), F32) for _ in range(hpg)]
        pending = [jnp.zeros((1, 1), F32) for _ in range(hpg)]
        total = lambda v: jnp.sum(v, keepdims=True)
        for i in reversed(range(nblk)):
            bm, cm = b_ref[0, ab * i:ab * (i + 1), :], c_ref[0, ab * i:ab * (i + 1), :]
            cm_t = cm.T
            cbt = jnp.dot(bm, cm_t, preferred_element_type=F32)
            dcbt = jnp.zeros((ab, ab), F32)
            d_bm, d_cm = jnp.zeros((ab, D_STATE), F32), jnp.zeros((ab, D_STATE), F32)
            for j in range(hpg):
                sl, hs, acol, arow, a_prev = _chunk_views(i, j, x_ref, ac_ref, at_ref)
                x, dy = x_ref[0, sl, hs], dy_ref[0, sl, hs]
                dy16 = _bf(dy)
                s_in, g_out = s_s[hpg * i + j], d_state[j]
                s16, g16 = _bf(s_in), _bf(g_out)
                decay = jnp.exp(jnp.where(ok_t, arow - acol, NEG))
                gt = cbt * decay
                dgt = lax.dot_general(x, dy16, _NT, preferred_element_type=F32)
                d_x = jnp.dot(_bf(gt), dy16, preferred_element_type=F32)
                dcbt = dcbt + dgt * decay
                mm = dgt * gt
                d_arow = jnp.sum(mm, axis=0, keepdims=True)
                d_acol = -jnp.sum(mm, axis=1, keepdims=True)
                e = jnp.exp(acol - a_prev)
                edy16 = _bf(e * dy)
                d_cm = d_cm + lax.dot_general(edy16, s16, _NT, preferred_element_type=F32)
                d_s = jnp.dot(cm_t, edy16, preferred_element_type=F32)
                de_e = jnp.sum(dy * jnp.dot(cm, s16, preferred_element_type=F32), axis=1, keepdims=True) * e
                a_end = acol[ab - 1:ab, :]
                w = jnp.exp(a_end - acol)
                f = jnp.exp(a_end - a_prev)
                x32 = x.astype(F32)
                bg = jnp.dot(bm, g16, preferred_element_type=F32)
                d_x = d_x + w * bg
                d_bm = d_bm + lax.dot_general(_bf(w * x32), g16, _NT, preferred_element_type=F32)
                dw_w = jnp.sum(bg * x32, axis=1, keepdims=True) * w
                df_f = total(g_out * s_in) * f
                d_end = total(dw_w) + df_f
                d_acol = d_acol + de_e - dw_w + jnp.where(last_row, d_end + pending[j], 0.0)
                pending[j] = -total(de_e) - df_f
                d_state[j] = d_s + f * g_out
                dx_ref[0, sl, hs] = d_x
                dac_ref[0, sl, j:j + 1] = d_acol
                dar_ref[0, j:j + 1, sl] = d_arow
            dcbt16 = _bf(dcbt)
            db_ref[0, ab * i:ab * (i + 1), :] = d_bm + jnp.dot(dcbt16, cm, preferred_element_type=F32)
            dc_ref[0, ab * i:ab * (i + 1), :] = d_cm + lax.dot_general(dcbt16, bm, _TN, preferred_element_type=F32)

    xblk, bblk, cblk, tblk = _ssd_specs(s)
    ablk = pl.BlockSpec((1, s, LANES), lambda b, g: (b, 0, g))
    return pl.pallas_call(
        body, name="ssd_bwd", grid=(nb_, N_GROUPS),
        in_specs=[xblk, bblk, cblk, ablk, tblk, xblk],
        out_specs=[xblk, bblk, bblk, ablk, pl.BlockSpec((1, 8, s), lambda b, g: (b, g, 0))],
        out_shape=[jax.ShapeDtypeStruct((nb_, s, N_GROUPS * GROUP_LANES), F32),
                   jax.ShapeDtypeStruct((nb_, s, N_GROUPS * D_STATE), F32),
                   jax.ShapeDtypeStruct((nb_, s, N_GROUPS * D_STATE), F32),
                   jax.ShapeDtypeStruct((nb_, s, N_GROUPS * LANES), F32),
                   jax.ShapeDtypeStruct((nb_, N_GROUPS * 8, s), F32)],
        scratch_shapes=[pltpu.VMEM((nblk * hpg, D_STATE, HEAD_DIM), F32)],
        compiler_params=_params("parallel", "parallel"),
    )(xdtg, bc, bc, acum, acum_t, dyg)


def _interleave(wg, wu):
    k, f = wg.shape
    gi = GATE_UP_INTERLEAVE
    return jnp.stack([wg.reshape(k, f // gi, gi), wu.reshape(k, f // gi, gi)], axis=2).reshape(k, 2 * f)


def _head_expanders():
    e_x = np.zeros((LANES, N_GROUPS * GROUP_LANES), np.float32)
    e_a = np.zeros((LANES, N_GROUPS * LANES), np.float32)
    for h in range(N_HEADS):
        g, j = divmod(h, HEADS_PER_GROUP)
        e_x[h, GROUP_LANES * g + HEAD_DIM * j:GROUP_LANES * g + HEAD_DIM * (j + 1)] = 1.0
        e_a[h, LANES * g + j] = 1.0
    return [jnp.asarray(m, BF16) for m in (e_x, e_x.T, e_a, e_a.T)]


def _pad_lanes(v, n=LANES):
    return jnp.pad(v, ((0, 0), (0, n - v.shape[1])))


def _local_step(x, positions, target, w, late=None, early_grad_job=None):
    nb, s, d = x.shape
    t = nb * s
    x2 = x.reshape(t, d)
    tgt2 = target.reshape(t, d)
    (job_a, weights_a), (job_b, weights_b) = late if late is not None else ((None, None), (None, None))

    x16 = _bf(x2)
    wgu1 = _interleave(w["ffn1_gate"], w["ffn1_up"])
    ffn1 = _ffn_fwd("ffn1_fwd", x16, x2, wgu1, w["ffn1_down"], w["ln1_g"], w["ln1_b"], carry=job_a)
    au1, hm1, h1, r1, h1_16 = ffn1[:5]
    if job_a is not None:
        w = {**w, **weights_a(ffn1[5])}

    w_in = w["w_in"]
    wqk, wv, wz = w_in[:, :2 * D_ATTN], w_in[:, 2 * D_ATTN:3 * D_ATTN], w_in[:, 3 * D_ATTN:3 * D_ATTN + D_SSD]
    wxbc = w_in[:, 3 * D_ATTN + D_SSD:3 * D_ATTN + D_SSD + D_CONV]
    wdt = _pad_lanes(w_in[:, 3 * D_ATTN + D_SSD + D_CONV:])

    inv_freq = ROPE_THETA ** (-jnp.arange(0, ROPE_DIM, 2, dtype=F32) / ROPE_DIM)
    half = ROPE_DIM // 2
    head_invf = jnp.concatenate([inv_freq, inv_freq, jnp.zeros((HEAD_DIM - ROPE_DIM,), F32)])
    head_sgn = jnp.concatenate([-jnp.ones((half,), F32), jnp.ones((half,), F32), jnp.zeros((HEAD_DIM - ROPE_DIM,), F32)])
    invf = jnp.tile(head_invf, LANES // HEAD_DIM)[None, :]
    sgn = jnp.tile(head_sgn, LANES // HEAD_DIM)[None, :]
    posf = positions.astype(F32).reshape(t, 1)
    bias_fwd, bias_bwd = _branch_bias_table(s, FWD_KEY_BLOCK), _branch_bias_table(s, SEQ_BLOCK)
    spreaders = _head_expanders()
    dtb, alog = _pad_lanes(w["dt_bias"]), _pad_lanes(w["a_log"])
    dskip = jnp.repeat(w["d_skip"], HEAD_DIM, axis=1)

    proj = _proj_in(h1_16, _pad_lanes(w_in, w_in.shape[1] - N_HEADS + LANES), posf, invf, sgn, carry=job_b)
    q16, k16, v16, z, xbc_pre, dtp, cs = proj[:7]
    if job_b is not None:
        w = {**w, **weights_b(proj[7])}
    wgu2 = _interleave(w["ffn2_gate"], w["ffn2_up"])
    to3 =lambda a: a.reshape(nb, s, a.shape[-1])
    attn_o, lse = _attn_fwd(to3(q16), to3(k16), to3(v16), bias_fwd)

    xbc = _conv_fwd(to3(xbc_pre), w["conv_w"], w["conv_b"]).reshape(t, D_CONV)
    xdtg, bc16, dag = _ssd_prep_fwd(xbc, dtp, dtb, alog, spreaders)
    acum, acum_t = _cumsum_fwd(to3(dag))
    yg = _ssd_fwd_chunked(to3(xdtg), to3(bc16), acum, acum_t)

    cat = _norms_fwd(attn_o.reshape(t, D_ATTN), yg.reshape(t, -1), xbc, z, w["attn_norm_w"], w["ssd_norm_w"], dskip)
    h2, r2, h2_16 = _mm_res_ln("w_out_ln2", cat, w["w_out"], h1, w["ln2_g"], w["ln2_b"], scale=1.0)

    au2, hm2, _, r3, _ = _ffn_fwd("ffn2_fwd", h2_16, h2, wgu2, w["ffn2_down"], w["ln3_g"], w["ln3_b"])

    g = {}
    dr3, dr3_16, g["ln3_g"], g["ln3_b"], loss = _ln_loss_bwd("loss_ln3_bwd", r3, w["ln3_g"], w["ln3_b"], tgt2)

    dau2, dh2 = _ffn_bwd("ffn2_bwd", dr3_16, dr3, w["ffn2_down"].T, au2, wgu2.T)
    g["ffn2_down"] = _mm_tn("ffn2_down_dw", hm2, dr3_16, scale=0.5, tk=D_FF // 2, tn=512)
    g["ffn2_gate"], g["ffn2_up"] = _mm_tn_gate_up("ffn2_up_dw", h2_16, dau2)

    dr2, dr2_16, g["ln2_g"], g["ln2_b"] = _ln_bwd("ln2_bwd", r2, w["ln2_g"], w["ln2_b"], dh2)
    dcat = _mm("w_out_dx", [(dr2_16, w["w_out"].T)], tn=768)
    g["w_out"] = _mm_tn("w_out_dw", cat, dr2_16, tk=768, tn=1024)

    d_attn, dyg, dxs_a, dz16, g["attn_norm_w"], g["ssd_norm_w"], ddskip = _norms_bwd(
        attn_o.reshape(t, D_ATTN), yg.reshape(t, -1), xbc, z, w["attn_norm_w"], w["ssd_norm_w"], dskip, dcat)
    g["d_skip"] = ddskip.reshape(N_HEADS, HEAD_DIM).sum(axis=1)[None, :]

    dq, dk, dv16 = _attn_bwd(to3(q16), to3(k16), to3(v16), attn_o, to3(d_attn), lse, bias_bwd)
    dqk16 = _rope_bwd(dq.reshape(t, D_ATTN), dk.reshape(t, D_ATTN), cs)

    dxdtg, dbm, dcm, dacol, darow = _ssd_bwd_chunked(to3(xdtg), to3(bc16), acum, acum_t, to3(dyg))
    ddag = _cumsum_bwd(dacol, darow)
    dxbc, ddtp16, ddtb, dalog = _ssd_prep_bwd(xbc, dtp, dtb, alog, spreaders, dxdtg.reshape(t, -1), ddag.reshape(t, -1),
                                               dxs_a, dbm.reshape(t, -1), dcm.reshape(t, -1))
    g["dt_bias"], g["a_log"] = ddtb[:, :N_HEADS], dalog[:, :N_HEADS]
    dxbc_pre16, dconv_w, g["conv_b"] = _conv_bwd(to3(xbc_pre), w["conv_w"], w["conv_b"], to3(dxbc))
    g["conv_w"] = dconv_w[:CONV_WIDTH]
    dxbc_pre16 = dxbc_pre16.reshape(t, D_CONV)
    dv16 = dv16.reshape(t, D_ATTN)

    dh1 = _mm("w_in_dx", [(dqk16, wqk.T), (dv16, wv.T), (dz16, wz.T), (dxbc_pre16, wxbc.T), (ddtp16, wdt.T)],
              res=dr2, res_scale=ALPHA)
    g["w_in"] = _mm_tn_sections("w_in_dw", h1_16, [dqk16, dv16, dz16, dxbc_pre16, ddtp16])[:, :w_in.shape[1]]

    dr1, dr1_16, g["ln1_g"], g["ln1_b"] = _ln_bwd("ln1_bwd", r1, w["ln1_g"], w["ln1_b"], dh1)
    g["ffn1_down"] = _mm_tn("ffn1_down_dw", hm1, dr1_16, scale=0.5, tk=D_FF // 2, tn=512)
    ffn1b = _ffn_bwd("ffn1_bwd", dr1_16, dr1, w["ffn1_down"].T, au1, wgu1.T,
                     carry=None if early_grad_job is None else early_grad_job(g))
    dau1, dx = ffn1b[:2]
    early = ffn1b[2] if early_grad_job is not None else None
    g["ffn1_gate"], g["ffn1_up"] = _mm_tn_gate_up("ffn1_up_dw", x16, dau1)
    return loss, dx.reshape(nb, s, d), g, early


_HBM = pl.BlockSpec(memory_space=pltpu.HBM)
N_CHIPS = 4
N_DEVICES = 8


def _place():
    return lax.axis_index("x"), lax.axis_index("y"), lax.axis_index("c")


def _other_chips(x, y):
    return [(1 - x, y), (x, 1 - y), (1 - x, 1 - y)]


class _GatherJob:
    def __init__(self, shards):
        assert all((a.shape[0] // 2) % 16 == 0 for a in shards)
        self.n = len(shards)
        self.shapes = [a.shape for a in shards]
        self.operands = [a.reshape(2, a.shape[0] // 2, a.shape[1]) for a in shards]
        self.out_shape = [jax.ShapeDtypeStruct((N_CHIPS,) + a.shape, a.dtype) for a in self.operands]
        pair = pltpu.SemaphoreType.DMA((self.n, N_CHIPS - 1))
        self.scratch_shapes = [pair, pair, pair, pair]

    def results(self, outs):
        return [o.reshape((N_CHIPS,) + s) for o, s in zip(outs, self.shapes)]

    def phases(self, ins, outs, sems):
        n = self.n
        send_sems, recv_sems, fwd_send_sems, fwd_recv_sems = sems
        x, y, c = _place()
        me = 2 * x + y
        peers = _other_chips(x, y)

        def ici(t, p, src_chip):
            px, py = peers[p]
            return pltpu.make_async_remote_copy(
                ins[t].at[c] if src_chip is None else outs[t].at[src_chip, c],
                outs[t].at[me if src_chip is None else src_chip, c],
                send_sems.at[t, p], recv_sems.at[t, p], device_id=(px, py, c), device_id_type=MESH)

        def d2d(t, p, core):
            px, py = peers[p]
            return pltpu.make_async_remote_copy(
                outs[t].at[2 * px + py, core], outs[t].at[2 * px + py, core],
                fwd_send_sems.at[t, p], fwd_recv_sems.at[t, p], device_id=(x, y, 1 - c), device_id_type=MESH)

        pairs = [(t, p) for t in range(n) for p in range(N_CHIPS - 1)]

        def start():
            for t, p in pairs:
                ici(t, p, None).start()

        def forward():
            for t, p in pairs:
                px, py = peers[p]
                ici(t, p, 2 * px + py).wait_recv()
                d2d(t, p, c).start()

        def finish():
            for t, p in pairs:
                d2d(t, p, 1 - c).wait_recv()
            for t, p in pairs:
                ici(t, p, None).wait_send()
                d2d(t, p, c).wait_send()

        return start, forward, finish


class _ExchangeJob:
    def __init__(self, stacks):
        self.n = len(stacks)
        self.operands = list(stacks)
        self.out_shape = [jax.ShapeDtypeStruct(a.shape, a.dtype) for a in stacks]
        pair = pltpu.SemaphoreType.DMA((self.n, N_CHIPS - 1))
        self.scratch_shapes = [pair, pair]

    def results(self, outs):
        return list(outs)

    def phases(self, ins, outs, sems):
        send_sems, recv_sems = sems
        x, y, c = _place()
        me = 2 * x + y
        peers = _other_chips(x, y)
        pairs = [(t, p) for t in range(self.n) for p in range(N_CHIPS - 1)]

        def copy(t, p):
            px, py = peers[p]
            return pltpu.make_async_remote_copy(ins[t].at[2 * px + py], outs[t].at[me], send_sems.at[t, p],
                                                recv_sems.at[t, p], device_id=(px, py, c), device_id_type=MESH)

        def arrival(t, p):
            px, py = peers[p]
            return pltpu.make_async_remote_copy(ins[t].at[me], outs[t].at[2 * px + py], send_sems.at[t, p],
                                                recv_sems.at[t, p], device_id=(px, py, c), device_id_type=MESH)

        def start():
            for t, p in pairs:
                copy(t, p).start()

        def finish():
            for t, p in pairs:
                arrival(t, p).wait_recv()
            for t, p in pairs:
                copy(t, p).wait_send()

        return start, None, finish


def _run_job(job, name):
    n = job.n

    def body(*refs):
        for phase in job.phases(refs[:n], refs[n:2 * n], refs[2 * n:]):
            if phase is not None:
                phase()

    outs = pl.pallas_call(
        body, name=name, in_specs=[_HBM] * n, out_specs=[_HBM] * n,
        out_shape=job.out_shape, scratch_shapes=job.scratch_shapes,
    )(*job.operands)
    return job.results(outs)


def _sibling_halves(stacks, name):
    n = len(stacks)
    halves = [a.shape[1] // 2 for a in stacks]
    split = [a.reshape(a.shape[0], 2, h, a.shape[2]) for a, h in zip(stacks, halves)]

    def body(*refs):
        ins, outs = refs[:n], refs[n:2 * n]
        send_sems, recv_sems = refs[2 * n:]
        x, y, c = _place()
        cps = []
        for t in range(n):
            cp = pltpu.make_async_remote_copy(ins[t].at[:, 1 - c], outs[t], send_sems.at[t], recv_sems.at[t],
                                              device_id=(x, y, 1 - c), device_id_type=MESH)
            cp.start()
            cps.append(cp)
        for cp in cps:
            cp.wait()

    return pl.pallas_call(
        body, name=name,
        in_specs=[_HBM] * n, out_specs=[_HBM] * n,
        out_shape=[jax.ShapeDtypeStruct((a.shape[0], h, a.shape[2]), a.dtype) for a, h in zip(stacks, halves)],
        scratch_shapes=[pltpu.SemaphoreType.DMA((n,)), pltpu.SemaphoreType.DMA((n,))],
    )(*split)


def _sibling_swap(arrs):
    n = len(arrs)

    def body(*refs):
        ins, outs = refs[:n], refs[n:2 * n]
        send_sems, recv_sems = refs[2 * n:]
        x, y, c = _place()
        cps = []
        for t in range(n):
            cp = pltpu.make_async_remote_copy(ins[t], outs[t], send_sems.at[t], recv_sems.at[t],
                                              device_id=(x, y, 1 - c), device_id_type=MESH)
            cp.start()
            cps.append(cp)
        for cp in cps:
            cp.wait()

    return pl.pallas_call(
        body, name="sibling_swap",
        in_specs=[_HBM] * n, out_specs=[_HBM] * n,
        out_shape=[jax.ShapeDtypeStruct(a.shape, a.dtype) for a in arrs],
        scratch_shapes=[pltpu.SemaphoreType.DMA((n,)), pltpu.SemaphoreType.DMA((n,))],
    )(*arrs)


def _half_sum(name, own, other, core):
    k, r, cols = own.shape
    h = r // 2
    tr = next(cand for cand in (128, 176, 64, 32, 16) if h % cand == 0)
    nblk = h // tr

    def body(core_ref, own_ref, other_ref, o_ref):
        o_ref[...] = _bf(own_ref[...] + other_ref[...].astype(F32))

    grid_spec = pltpu.PrefetchScalarGridSpec(
        num_scalar_prefetch=1, grid=(nblk,),
        in_specs=[pl.BlockSpec((k, tr, cols), lambda i, core_ref: (0, i + core_ref[0] * nblk, 0)),
                  pl.BlockSpec((k, tr, cols), lambda i, core_ref: (0, i, 0))],
        out_specs=pl.BlockSpec((k, tr, cols), lambda i, core_ref: (0, i, 0)))
    return pl.pallas_call(
        body, name=name, grid_spec=grid_spec, out_shape=jax.ShapeDtypeStruct((k, h, cols), BF16),
        compiler_params=_params("parallel"),
    )(core.reshape(1).astype(jnp.int32), own, other)


def _small_allreduce(v):
    r = v.shape[0]

    def body(v_ref, tot_ref, slots, send_sems, recv_sems):
        x, y, c = _place()
        me = 4 * x + 2 * y + c
        slots[me] = v_ref[...]
        cps, peers = [], []
        for k in range(1, N_DEVICES):
            px = 1 - x if (k >> 2) & 1 else x
            py = 1 - y if (k >> 1) & 1 else y
            pc = 1 - c if k & 1 else c
            cp = pltpu.make_async_remote_copy(v_ref, slots.at[me], send_sems.at[k - 1], recv_sems.at[k - 1],
                                              device_id=(px, py, pc), device_id_type=MESH)
            cp.start()
            cps.append(cp)
            peers.append((px, py, pc))
        for k, (px, py, pc) in enumerate(peers):
            pltpu.make_async_remote_copy(v_ref, slots.at[4 * px + 2 * py + pc], send_sems.at[k], recv_sems.at[k],
                                         device_id=(px, py, pc), device_id_type=MESH).wait_recv()
        for cp in cps:
            cp.wait_send()
        acc = slots[0]
        for s in range(1, N_DEVICES):
            acc = acc + slots[s]
        tot_ref[...] = acc

    return pl.pallas_call(
        body, name="small_allreduce",
        in_specs=[pl.BlockSpec(memory_space=pltpu.VMEM)], out_specs=pl.BlockSpec(memory_space=pltpu.VMEM),
        out_shape=jax.ShapeDtypeStruct((r, LANES), F32),
        scratch_shapes=[pltpu.VMEM((N_DEVICES, r, LANES), F32), pltpu.SemaphoreType.DMA((N_DEVICES - 1,)),
                        pltpu.SemaphoreType.DMA((N_DEVICES - 1,))],
    )(v)


def _elementwise(name, fn, ins, out_dtypes):
    r, c = ins[0].shape[-2:]
    tr = next((cand for cand in (256, 176, 128, 64, 32, 16) if r % cand == 0), r)
    nin = len(ins)

    def body(*refs):
        outs = fn(*[ref[...] for ref in refs[:nin]])
        for o_ref, o in zip(refs[nin:], outs):
            o_ref[...] = o.astype(o_ref.dtype)

    in_specs = [pl.BlockSpec((tr, c), lambda i: (i, 0)) if a.ndim == 2 else pl.BlockSpec((a.shape[0], tr, c), lambda i: (0, i, 0))
                for a in ins]
    return pl.pallas_call(
        body, name=name, grid=(r // tr,), in_specs=in_specs,
        out_specs=[pl.BlockSpec((tr, c), lambda i: (i, 0)) for _ in out_dtypes],
        out_shape=[jax.ShapeDtypeStruct((r, c), dt) for dt in out_dtypes],
        compiler_params=_params("parallel"),
    )(*ins)


def _row_tile(rows):
    return next((cand for cand in (128, 176, 64, 32, 16) if rows % cand == 0), rows)


def _sum_slots(name, received, own, chip):
    _, r, cols = own.shape
    tr = _row_tile(r)

    def body(chip_ref, own_ref, a_ref, b_ref, c_ref, o_ref):
        o_ref[...] = ((own_ref[0].astype(F32) + a_ref[0].astype(F32)) + b_ref[0].astype(F32)) + c_ref[0].astype(F32)

    def slot(flip):
        return pl.BlockSpec((1, tr, cols), lambda i, chip_ref: (jnp.bitwise_xor(chip_ref[0], flip), i, 0))

    grid_spec = pltpu.PrefetchScalarGridSpec(
        num_scalar_prefetch=1, grid=(r // tr,), in_specs=[slot(0), slot(1), slot(2), slot(3)],
        out_specs=pl.BlockSpec((tr, cols), lambda i, chip_ref: (i, 0)))
    return pl.pallas_call(
        body, name=name, grid_spec=grid_spec, out_shape=jax.ShapeDtypeStruct((r, cols), F32),
        compiler_params=_params("parallel"),
    )(chip.reshape(1).astype(jnp.int32), own, received, received, received)


def _adamw_halves(name, mine, theirs, core, w, m, v):
    h, cols = mine.shape
    tr = _row_tile(h)
    nh = h // tr

    def body(core_ref, mine_ref, theirs_ref, w_ref, m_ref, v_ref, g_ref, d_ref, m2_ref, v2_ref):
        is_mine = (pl.program_id(0) // nh) == core_ref[0]
        g = jnp.where(is_mine, mine_ref[...], theirs_ref[...])
        outs = _adamw_math(g, w_ref[...], m_ref[...], v_ref[...])
        for ref, val in zip((g_ref, d_ref, m2_ref, v2_ref), outs):
            ref[...] = val

    half = pl.BlockSpec((tr, cols), lambda i, core_ref: (i % nh, 0))
    full = pl.BlockSpec((tr, cols), lambda i, core_ref: (i, 0))
    grid_spec = pltpu.PrefetchScalarGridSpec(
        num_scalar_prefetch=1, grid=(2 * nh,), in_specs=[half, half, full, full, full], out_specs=[full] * 4)
    return pl.pallas_call(
        body, name=name, grid_spec=grid_spec, out_shape=[jax.ShapeDtypeStruct((2 * h, cols), F32)] * 4,
        compiler_params=_params("parallel"),
    )(core.reshape(1).astype(jnp.int32), mine, theirs, w, m, v)


def _adamw_math(g, w_v, m_v, v_v):
    m2 = ADAM_B1 * m_v + (1.0 - ADAM_B1) * g
    v2 = ADAM_B2 * v_v + (1.0 - ADAM_B2) * jnp.square(g)
    m_hat = m2 / (1.0 - ADAM_B1 ** ADAM_STEP)
    v_hat = v2 / (1.0 - ADAM_B2 ** ADAM_STEP)
    delta = -ADAM_LR * (m_hat / (jnp.sqrt(v_hat) + ADAM_EPS) + ADAM_WD * w_v)
    return [g, delta, m2, v2]


def _adamw(name, g, w, m, v):
    return _elementwise(name, _adamw_math, [g, w, m, v], [F32] * 4)


_MATRICES = (("ffn1_gate", 1), ("ffn1_up", 1), ("ffn1_down", 0), ("w_in", 1), ("w_out", 0),
             ("ffn2_gate", 1), ("ffn2_up", 1), ("ffn2_down", 0))
_VECTORS = ("ln1_g", "ln1_b", "conv_b", "dt_bias", "a_log", "d_skip", "attn_norm_w", "ssd_norm_w",
            "ln2_g", "ln2_b", "ln3_g", "ln3_b")
_WEIGHT_ORDER = ("ln1_g", "ln1_b", "ffn1_gate", "ffn1_up", "ffn1_down", "w_in", "conv_w", "conv_b", "dt_bias", "a_log",
                 "d_skip", "attn_norm_w", "ssd_norm_w", "w_out", "ln2_g", "ln2_b", "ffn2_gate", "ffn2_up", "ffn2_down",
                 "ln3_g", "ln3_b")


def _pack_rows(vectors):
    parts = []
    for vec in vectors:
        flat = vec.reshape(-1)
        parts.append(jnp.pad(flat, (0, (-flat.shape[0]) % LANES)))
    flat = jnp.concatenate(parts)
    flat = jnp.pad(flat, (0, (-flat.shape[0]) % (8 * LANES)))
    return flat.reshape(-1, LANES)


def _unpack_rows(packed, shapes):
    flat = packed.reshape(-1)
    out, off = [], 0
    for shape in shapes:
        size = int(np.prod(shape))
        out.append(flat[off:off + size].reshape(shape))
        off += size + (-size) % LANES
    return out


def _assemble(stack, own, chip, axis):
    blocks = [jnp.where(chip == s, own, stack[s]) for s in range(N_CHIPS)]
    return jnp.concatenate(blocks, axis=axis)


def _split(full, axis):
    if axis == 0:
        return full.reshape(N_CHIPS, -1, full.shape[1])
    cols = full.shape[1] // N_CHIPS
    return jnp.stack([full[:, cols * s:cols * (s + 1)] for s in range(N_CHIPS)])


def kernel(x, positions, ln1_g, ln1_b, ffn1_gate, ffn1_up, ffn1_down, w_in, conv_w, conv_b, dt_bias, a_log, d_skip, attn_norm_w, ssd_norm_w, w_out, ln2_g, ln2_b, ffn2_gate, ffn2_up, ffn2_down, ln3_g, ln3_b, loss_target, m_ln1_g, m_ln1_b, m_ffn1_gate, m_ffn1_up, m_ffn1_down, m_w_in, m_conv_w, m_conv_b, m_dt_bias, m_a_log, m_d_skip, m_attn_norm_w, m_ssd_norm_w, m_w_out, m_ln2_g, m_ln2_b, m_ffn2_gate, m_ffn2_up, m_ffn2_down, m_ln3_g, m_ln3_b, v_ln1_g, v_ln1_b, v_ffn1_gate, v_ffn1_up, v_ffn1_down, v_w_in, v_conv_w, v_conv_b, v_dt_bias, v_a_log, v_d_skip, v_attn_norm_w, v_ssd_norm_w, v_w_out, v_ln2_g, v_ln2_b, v_ffn2_gate, v_ffn2_up, v_ffn2_down, v_ln3_g, v_ln3_b):
    given = dict(locals())
    wts = {n: given[n] for n in _WEIGHT_ORDER}
    mom_m = {n: given["m_" + n] for n in _WEIGHT_ORDER}
    mom_v = {n: given["v_" + n] for n in _WEIGHT_ORDER}
    chip = 2 * lax.axis_index("x") + lax.axis_index("y")

    core = lax.axis_index("c")
    groups = [[(n, axis) for n, axis in _MATRICES if n.startswith(prefix)] for prefix in ("ffn1", "w_", "ffn2")]
    own16 = {n: wts[n][0].astype(BF16) for n, _ in _MATRICES}
    gathered = _run_job(_GatherJob([own16[n] for n, _ in groups[0]]), "gather_ffn1")
    full = {n: _assemble(st, own16[n], chip, axis) for (n, axis), st in zip(groups[0], gathered)}
    for n in _VECTORS:
        full[n] = wts[n]
    conv_rows = jnp.pad(wts["conv_w"][0], ((0, 32 - CONV_WIDTH), (0, 0)))

    def mixer_weights(results):
        out = {n: _assemble(st, own16[n], chip, axis) for (n, axis), st in zip(groups[1], results)}
        out["conv_w"] = _assemble(results[-1], conv_rows, chip, 1)[:CONV_WIDTH]
        return out

    def ffn2_weights(results):
        return {n: _assemble(st, own16[n], chip, axis) for (n, axis), st in zip(groups[2], results)}

    late = [(_GatherJob([own16[n] for n, _ in groups[1]] + [conv_rows]), mixer_weights),
            (_GatherJob([own16[n] for n, _ in groups[2]]), ffn2_weights)]

    chip_sums = {}

    def core_sums(g, which, tag):
        partials = [_split(g[n], axis) for n, axis in which]
        from_sibling = _sibling_halves([p.astype(BF16) for p in partials], "sibling_halves_" + tag)
        for (n, _), p, o in zip(which, partials, from_sibling):
            chip_sums[n] = _half_sum("core_sum_" + n, p, o, core)
        return _ExchangeJob([chip_sums[n] for n, _ in which])

    last = [(n, axis) for n, axis in _MATRICES if n in ("ffn1_gate", "ffn1_up")]
    early = [(n, axis) for n, axis in _MATRICES if (n, axis) not in last]
    loss, grad_x, g, received_early = _local_step(x, positions, loss_target, full, late,
                                                  lambda g_now: core_sums(g_now, early, "early"))
    received_last = _run_job(core_sums(g, last, "last"), "exchange_last")
    received = dict(zip([n for n, _ in last + early], received_last + received_early))
    half_totals = [_sum_slots("sum_partials_" + n, received[n], chip_sums[n], chip) for n, _ in _MATRICES]
    other_halves = _sibling_swap(half_totals)

    small_shapes = [g[n].shape for n in _VECTORS] + [g["conv_w"].shape, (1,)]
    total = _small_allreduce(_pack_rows([g[n] for n in _VECTORS] + [g["conv_w"], loss[0, :1]]))
    small = _unpack_rows(total, small_shapes)
    loss_out = small[-1].reshape(())

    grads, deltas, new_m, new_v = {}, {}, {}, {}
    for (n, _), mine, theirs in zip(_MATRICES, half_totals, other_halves):
        res = _adamw_halves("adamw_" + n, mine, theirs, core, wts[n][0], mom_m[n][0], mom_v[n][0])
        grads[n], deltas[n], new_m[n], new_v[n] = [r[None] for r in res]

    vec_shapes = [wts[n].shape for n in _VECTORS]
    res = _adamw("adamw_vectors", _pack_rows(small[:len(_VECTORS)]), _pack_rows([wts[n] for n in _VECTORS]),
                 _pack_rows([mom_m[n] for n in _VECTORS]), _pack_rows([mom_v[n] for n in _VECTORS]))
    for dst, packed in zip((grads, deltas, new_m, new_v), res):
        for n, val in zip(_VECTORS, _unpack_rows(packed, vec_shapes)):
            dst[n] = val

    cols = conv_w.shape[2]
    g_conv = lax.dynamic_slice_in_dim(small[len(_VECTORS)], chip * cols, cols, axis=1)
    res = _adamw("adamw_conv_w", g_conv, wts["conv_w"][0], mom_m["conv_w"][0], mom_v["conv_w"][0])
    grads["conv_w"], deltas["conv_w"], new_m["conv_w"], new_v["conv_w"] = [r[None] for r in res]

    return (loss_out, grad_x, *[grads[n] for n in _WEIGHT_ORDER], *[deltas[n] for n in _WEIGHT_ORDER],
            *[new_m[n] for n in _WEIGHT_ORDER], *[new_v[n] for n in _WEIGHT_ORDER])
```

```python
import functools

import numpy as np
import jax
import jax.numpy as jnp
from jax import lax
from jax.experimental import pallas as pl
from jax.experimental.pallas import tpu as pltpu

F32, BF16 = jnp.float32, jnp.bfloat16

D_MODEL = 1024
D_FF = 2816
N_HEADS = 12
HEAD_DIM = 64
D_ATTN = 768
D_SSD = 768
N_GROUPS = 4
HEADS_PER_GROUP = 3
D_STATE = 128
D_CONV = 1792
CONV_WIDTH = 4
ROPE_DIM = 16
ROPE_THETA = 500000.0
ALPHA = 2.0 ** 0.25
LN_EPS = 1e-5
RMS_EPS = 1e-6
ADAM_LR, ADAM_B1, ADAM_B2, ADAM_EPS, ADAM_WD, ADAM_STEP = 0.001, 0.9, 0.999, 1e-08, 0.01, 10

LANES = 128
GATE_UP_INTERLEAVE = 256
SEQ_BLOCK = 256
GROUP_LANES = 256
VMEM_LIMIT = 56 * 1024 * 1024
NEG = -1e30
MESH = pl.DeviceIdType.MESH
HIGHEST = lax.Precision.HIGHEST

_NT = (((1,), (1,)), ((), ()))
_TN = (((0,), (0,)), ((), ()))


def _params(*sem):
    return pltpu.CompilerParams(dimension_semantics=sem, vmem_limit_bytes=VMEM_LIMIT)


def _bf(v):
    return v.astype(BF16)


EPILOGUE_ROWS = 128


def _row_chunks(tm):
    return [slice(r, min(r + EPILOGUE_ROWS, tm)) for r in range(0, tm, EPILOGUE_ROWS)]


def _sigmoid(v):
    return 0.5 * jnp.tanh(0.5 * v) + 0.5


def _mm(name, pairs, *, scale=1.0, res=None, res_scale=1.0, out_dtype=F32, tm=512, tn=512):
    m, n = pairs[0][0].shape[0], pairs[0][1].shape[1]
    tm, tn = min(tm, m), min(tn, n)
    assert m % tm == 0 and n % tn == 0, (name, m, n, tm, tn)
    npair = len(pairs)

    def body(*refs):
        acc = None
        for a_ref, b_ref in zip(refs[:npair], refs[npair:2 * npair]):
            d = jnp.dot(_bf(a_ref[...]), b_ref[...], preferred_element_type=F32)
            acc = d if acc is None else acc + d
        if scale != 1.0:
            acc = acc * scale
        if res is not None:
            acc = acc + res_scale * refs[2 * npair][...]
        refs[-1][...] = acc.astype(out_dtype)

    in_specs = [pl.BlockSpec((tm, a.shape[1]), lambda i, j: (i, 0)) for a, _ in pairs]
    in_specs += [pl.BlockSpec((b.shape[0], tn), lambda i, j: (0, j)) for _, b in pairs]
    args = [a for a, _ in pairs] + [b for _, b in pairs]
    if res is not None:
        in_specs.append(pl.BlockSpec((tm, tn), lambda i, j: (i, j)))
        args.append(res)
    return pl.pallas_call(
        body, name=name, grid=(m // tm, n // tn), in_specs=in_specs,
        out_specs=pl.BlockSpec((tm, tn), lambda i, j: (i, j)),
        out_shape=jax.ShapeDtypeStruct((m, n), out_dtype),
        compiler_params=_params("parallel", "parallel"),
    )(*args)


def _mm_tn(name, x, dy, *, scale=1.0, tk=512, tn=512, tt=1024):
    t, k = x.shape
    n = dy.shape[1]
    tk, tn, tt = min(tk, k), min(tn, n), min(tt, t)
    assert k % tk == 0 and n % tn == 0 and t % tt == 0, (name, k, n, t)
    nt = t // tt

    def body(x_ref, dy_ref, o_ref):
        step = pl.program_id(2)
        d = lax.dot_general(_bf(x_ref[...]), _bf(dy_ref[...]), _TN, preferred_element_type=F32)

        @pl.when(step == 0)
        def _():
            o_ref[...] = d

        @pl.when(step > 0)
        def _():
            o_ref[...] += d

        if scale != 1.0:
            @pl.when(step == nt - 1)
            def _():
                o_ref[...] = o_ref[...] * scale

    return pl.pallas_call(
        body, name=name, grid=(k // tk, n // tn, nt),
        in_specs=[pl.BlockSpec((tt, tk), lambda i, j, s: (s, i)), pl.BlockSpec((tt, tn), lambda i, j, s: (s, j))],
        out_specs=pl.BlockSpec((tk, tn), lambda i, j, s: (i, j)),
        out_shape=jax.ShapeDtypeStruct((k, n), F32),
        compiler_params=_params("parallel", "parallel", "arbitrary"),
    )(x, dy)


def _mm_tn_sections(name, x, dys, *, tt=512):
    t, k = x.shape
    tt = min(tt, t)
    cuts = np.cumsum([0] + [d.shape[1] for d in dys]).tolist()
    ns = len(dys)

    def body(*refs):
        x_ref, o_ref = refs[0], refs[1 + ns]
        step = pl.program_id(0)
        xt = x_ref[...].T
        parts = [jnp.dot(xt, refs[1 + a][...], preferred_element_type=F32) for a in range(ns)]

        @pl.when(step == 0)
        def _():
            for a in range(ns):
                o_ref[:, cuts[a]:cuts[a + 1]] = parts[a]

        @pl.when(step > 0)
        def _():
            for a in range(ns):
                o_ref[:, cuts[a]:cuts[a + 1]] += parts[a]

    return pl.pallas_call(
        body, name=name, grid=(t // tt,),
        in_specs=[pl.BlockSpec((tt, k), lambda s: (s, 0))] + [pl.BlockSpec((tt, d.shape[1]), lambda s: (s, 0)) for d in dys],
        out_specs=pl.BlockSpec((k, cuts[-1]), lambda s: (0, 0)),
        out_shape=jax.ShapeDtypeStruct((k, cuts[-1]), F32),
        compiler_params=_params("arbitrary"),
    )(x, *dys)


def _mm_tn_gate_up(name, x, dau, *, tt=1024):
    t, k = x.shape
    gi = GATE_UP_INTERLEAVE
    nj = dau.shape[1] // (2 * gi)
    tt = min(tt, t)
    nt = t // tt

    def body(x_ref, dy_ref, g_ref, u_ref):
        step = pl.program_id(1)
        d = lax.dot_general(dy_ref[...], _bf(x_ref[...]), _TN, preferred_element_type=F32)

        @pl.when(step == 0)
        def _():
            g_ref[...] = d[:gi]
            u_ref[...] = d[gi:]

        @pl.when(step > 0)
        def _():
            g_ref[...] += d[:gi]
            u_ref[...] += d[gi:]

    out = pl.BlockSpec((gi, k), lambda j, s: (j, 0))
    return pl.pallas_call(
        body, name=name, grid=(nj, nt),
        in_specs=[pl.BlockSpec((tt, k), lambda j, s: (s, 0)), pl.BlockSpec((tt, 2 * gi), lambda j, s: (s, j))],
        out_specs=[out, out],
        out_shape=[jax.ShapeDtypeStruct((gi * nj, k), F32)] * 2,
        compiler_params=_params("parallel", "arbitrary"),
    )(x, dau)


def _carried(carry, ins, outs, sems, step, total):
    start, forward, finish = carry.phases(ins, outs, sems)
    pl.when(step == 0)(start)
    if forward is not None:
        pl.when(step == (3 * total) // 4)(forward)
    return lambda: pl.when(step == total - 1)(finish)


def _resident(shape):
    return pl.BlockSpec(shape, lambda i: (0,) * len(shape), pipeline_mode=pl.Buffered(1))


def _ffn_fwd(name, x16, res, wgu, wd, g, b, *, tm=512, carry=None):
    t, k = x16.shape
    gi = GATE_UP_INTERLEAVE
    nj, n, ni = wd.shape[0] // gi, wd.shape[1], t // tm
    nc = carry.n if carry is not None else 0

    def body(*refs):
        x_ref, res_ref, wgu_ref, wd_ref, g_ref, b_ref = refs[:6]
        au_ref, hm_ref, y_ref, r_ref, y16_ref = refs[6 + nc:11 + nc]
        if carry is not None:
            finish = _carried(carry, refs[6:6 + nc], refs[11 + nc:11 + 2 * nc], refs[11 + 2 * nc:], pl.program_id(0), ni)
        xv = x_ref[...]
        acc = jnp.zeros((tm, n), F32)
        for j in range(nj):
            au = jnp.dot(xv, wgu_ref[:, 2 * gi * j:2 * gi * (j + 1)], preferred_element_type=F32)
            a, u = au[:, :gi], au[:, gi:]
            au_ref[:, 2 * gi * j:2 * gi * (j + 1)] = _bf(au)
            hm = _bf(a * _sigmoid(a) * u)
            hm_ref[:, gi * j:gi * (j + 1)] = hm
            acc = acc + jnp.dot(hm, wd_ref[gi * j:gi * (j + 1), :], preferred_element_type=F32)
        r = ALPHA * res_ref[...] + 0.5 * acc
        r_ref[...] = r
        y = _layer_norm(r, g_ref[...], b_ref[...])
        y_ref[...] = y
        y16_ref[...] = _bf(y)
        if carry is not None:
            finish()

    row = lambda c: pl.BlockSpec((tm, c), lambda i: (i, 0))
    hbm = pl.BlockSpec(memory_space=pltpu.HBM)
    res_ = pl.pallas_call(
        body, name=name, grid=(ni,),
        in_specs=[row(k), row(n), _resident(wgu.shape), _resident(wd.shape), _resident(g.shape), _resident(b.shape)] + [hbm] * nc,
        out_specs=[row(2 * gi * nj), row(gi * nj), row(n), row(n), row(n)] + [hbm] * nc,
        out_shape=[jax.ShapeDtypeStruct((t, 2 * gi * nj), BF16), jax.ShapeDtypeStruct((t, gi * nj), BF16),
                   jax.ShapeDtypeStruct((t, n), F32), jax.ShapeDtypeStruct((t, n), F32), jax.ShapeDtypeStruct((t, n), BF16)]
        + (carry.out_shape if carry is not None else []),
        scratch_shapes=carry.scratch_shapes if carry is not None else [],
        compiler_params=_params("arbitrary" if carry is not None else "parallel"),
    )(x16, res, wgu, wd, g, b, *(carry.operands if carry is not None else []))
    return tuple(res_[:5]) + ((carry.results(res_[5:]),) if carry is not None else ())


def _ffn_bwd(name, dr16, dr, wdt, au, wgut, *, tm=512, carry=None):
    t, n = dr16.shape
    gi = GATE_UP_INTERLEAVE
    nj, ni = wdt.shape[1] // gi, t // tm
    nc = carry.n if carry is not None else 0

    def body(*refs):
        dr16_ref, dr_ref, wdt_ref, au_ref, wgut_ref = refs[:5]
        dau_ref, dx_ref = refs[5 + nc:7 + nc]
        if carry is not None:
            finish = _carried(carry, refs[5:5 + nc], refs[7 + nc:7 + 2 * nc], refs[7 + 2 * nc:], pl.program_id(0), ni)
        drv = dr16_ref[...]
        acc = jnp.zeros((tm, n), F32)
        for j in range(nj):
            dhm = jnp.dot(drv, wdt_ref[:, gi * j:gi * (j + 1)], preferred_element_type=F32) * 0.5
            au_v = au_ref[:, 2 * gi * j:2 * gi * (j + 1)].astype(F32)
            a, u = au_v[:, :gi], au_v[:, gi:]
            sig = _sigmoid(a)
            silu = a * sig
            dau = jnp.concatenate([_bf(dhm * u * (sig + silu - silu * sig)), _bf(dhm * silu)], axis=1)
            dau_ref[:, 2 * gi * j:2 * gi * (j + 1)] = dau
            acc = acc + jnp.dot(dau, wgut_ref[2 * gi * j:2 * gi * (j + 1), :], preferred_element_type=F32)
        dx_ref[...] = ALPHA * dr_ref[...] + acc
        if carry is not None:
            finish()

    row = lambda c: pl.BlockSpec((tm, c), lambda i: (i, 0))
    hbm = pl.BlockSpec(memory_space=pltpu.HBM)
    res_ = pl.pallas_call(
        body, name=name, grid=(ni,),
        in_specs=[row(n), row(n), _resident(wdt.shape), row(2 * gi * nj), _resident(wgut.shape)] + [hbm] * nc,
        out_specs=[row(2 * gi * nj), row(n)] + [hbm] * nc,
        out_shape=[jax.ShapeDtypeStruct((t, 2 * gi * nj), BF16), jax.ShapeDtypeStruct((t, n), F32)]
        + (carry.out_shape if carry is not None else []),
        scratch_shapes=carry.scratch_shapes if carry is not None else [],
        compiler_params=_params("arbitrary" if carry is not None else "parallel"),
    )(dr16, dr, wdt, au, wgut, *(carry.operands if carry is not None else []))
    return tuple(res_[:2]) + ((carry.results(res_[2:]),) if carry is not None else ())


def _layer_norm(r, g, b):
    mu = jnp.mean(r, axis=-1, keepdims=True)
    var = jnp.mean(jnp.square(r - mu), axis=-1, keepdims=True)
    return (r - mu) * lax.rsqrt(var + LN_EPS) * g + b


def _mm_res_ln(name, a, w, res, g, b, *, scale, tm=256):
    t, k = a.shape
    n = w.shape[1]

    def body(a_ref, w_ref, res_ref, g_ref, b_ref, y_ref, r_ref, y16_ref):
        for rows in _row_chunks(tm):
            r = ALPHA * res_ref[rows, :] + scale * jnp.dot(_bf(a_ref[rows, :]), w_ref[...], preferred_element_type=F32)
            r_ref[rows, :] = r
            y = _layer_norm(r, g_ref[...], b_ref[...])
            y_ref[rows, :] = y
            y16_ref[rows, :] = _bf(y)

    row = lambda c: pl.BlockSpec((tm, c), lambda i: (i, 0))
    const = lambda shape: pl.BlockSpec(shape, lambda i: (0, 0))
    return pl.pallas_call(
        body, name=name, grid=(t // tm,),
        in_specs=[row(k), const((k, n)), row(n), const((1, n)), const((1, n))],
        out_specs=[row(n), row(n), row(n)],
        out_shape=[jax.ShapeDtypeStruct((t, n), F32), jax.ShapeDtypeStruct((t, n), F32), jax.ShapeDtypeStruct((t, n), BF16)],
        compiler_params=_params("parallel"),
    )(a, w, res, g, b)


def _rowwise(name, fn, rows, consts, row_outs, acc_outs=(), tm=512):
    rows = [r if isinstance(r, tuple) else (r, r.shape[1]) for r in rows]
    t = rows[0][0].shape[0]
    tm = min(tm, t)
    assert t % tm == 0
    nr, nc, no, na = len(rows), len(consts), len(row_outs), len(acc_outs)

    def body(*refs):
        vals = [r[...] for r in refs[:nr + nc]]
        outs, accs = fn(*vals)
        for o_ref, o in zip(refs[nr + nc:nr + nc + no], outs):
            o_ref[...] = o.astype(o_ref.dtype)
        if na:
            step = pl.program_id(0)
            acc_refs = refs[nr + nc + no:]

            @pl.when(step == 0)
            def _():
                for a_ref, a in zip(acc_refs, accs):
                    a_ref[...] = a

            @pl.when(step > 0)
            def _():
                for a_ref, a in zip(acc_refs, accs):
                    a_ref[...] += a

    in_specs = [pl.BlockSpec((tm, w), lambda i: (i, 0)) for _, w in rows]
    in_specs += [pl.BlockSpec(c.shape, lambda i, nd=c.ndim: (0,) * nd) for c in consts]
    out_specs = [pl.BlockSpec((tm, c), lambda i: (i, 0)) for c, _ in row_outs]
    out_specs += [pl.BlockSpec(s, lambda i: (0, 0)) for s in acc_outs]
    out_shape = [jax.ShapeDtypeStruct((t, c), dt) for c, dt in row_outs]
    out_shape += [jax.ShapeDtypeStruct(s, F32) for s in acc_outs]
    res = pl.pallas_call(
        body, name=name, grid=(t // tm,), in_specs=in_specs, out_specs=out_specs, out_shape=out_shape,
        compiler_params=_params("arbitrary" if na else "parallel"),
    )(*[r for r, _ in rows], *consts)
    return res


def _ln_bwd(name, r, g, b, dy):
    def fn(r_v, dy_v, g_v, b_v):
        _, vjp = jax.vjp(_layer_norm, r_v, g_v, b_v)
        dr, dg, db = vjp(dy_v)
        return [dr, dr], [dg, db]
    return _rowwise(name, fn, [r, dy], [g, b], [(r.shape[1], F32), (r.shape[1], BF16)], [(1, r.shape[1])] * 2)


def _ln_loss_bwd(name, r, g, b, target):
    def fn(r_v, t_v, g_v, b_v):
        def loss_fn(rr, gg, bb):
            err = jnp.square(_layer_norm(rr, gg, bb) - t_v)
            return 0.5 * jnp.sum(jnp.mean(err, axis=-1, keepdims=True), axis=0, keepdims=True)
        loss, vjp = jax.vjp(loss_fn, r_v, g_v, b_v)
        dr, dg, db = vjp(jnp.ones((1, 1), F32))
        return [dr, dr], [dg, db, jnp.broadcast_to(loss, (1, LANES))]
    return _rowwise(name, fn, [r, target], [g, b], [(r.shape[1], F32), (r.shape[1], BF16)],
                    [(1, r.shape[1])] * 2 + [(1, LANES)])


def _rope_tables(posf, invf, sgn):
    ang = posf * invf
    return jnp.cos(ang), jnp.sin(ang) * sgn


def _rope_apply(tv, cos, sin):
    lane = lax.broadcasted_iota(jnp.int32, cos.shape, 1)
    first = (lane % HEAD_DIM) < (ROPE_DIM // 2)
    outs = []
    for gidx in range(tv.shape[1] // LANES):
        tg = tv[:, LANES * gidx:LANES * (gidx + 1)]
        sw = jnp.where(first, pltpu.roll(tg, LANES - ROPE_DIM // 2, 1), pltpu.roll(tg, ROPE_DIM // 2, 1))
        outs.append(tg * cos + sw * sin)
    return jnp.concatenate(outs, axis=1)


def _proj_in(h16, w_in, posf, invf, sgn, *, tm=512, carry=None):
    t, k = h16.shape
    cuts = [0, D_ATTN, 2 * D_ATTN, 3 * D_ATTN, 3 * D_ATTN + D_SSD, 3 * D_ATTN + D_SSD + D_CONV, w_in.shape[1]]
    nc = carry.n if carry is not None else 0

    def body(*refs):
        h_ref, w_ref, pos_ref, invf_ref, sgn_ref = refs[:5]
        q_ref, k_ref, v_ref, z_ref, xbc_ref, dt_ref, cs_ref = refs[5 + nc:12 + nc]
        if carry is not None:
            finish = _carried(carry, refs[5:5 + nc], refs[12 + nc:12 + 2 * nc], refs[12 + 2 * nc:], pl.program_id(0), t // tm)
        hv = h_ref[...]
        part = lambda a: jnp.dot(hv, w_ref[:, cuts[a]:cuts[a + 1]], preferred_element_type=F32)
        cos, sin = _rope_tables(pos_ref[...], invf_ref[...], sgn_ref[...])
        cs_ref[...] = jnp.concatenate([cos, sin], axis=1)
        q_ref[...] = _bf(_rope_apply(part(0), cos, sin) * (HEAD_DIM ** -0.5))
        k_ref[...] = _bf(_rope_apply(part(1), cos, sin))
        v_ref[...] = _bf(part(2))
        z_ref[...] = part(3)
        xbc_ref[...] = part(4)
        dt_ref[...] = part(5)
        if carry is not None:
            finish()

    row = lambda c: pl.BlockSpec((tm, c), lambda i: (i, 0))
    hbm = pl.BlockSpec(memory_space=pltpu.HBM)
    widths = [D_ATTN, D_ATTN, D_ATTN, D_SSD, D_CONV, LANES, 2 * LANES]
    dtypes = [BF16, BF16, BF16, F32, F32, F32, F32]
    res = pl.pallas_call(
        body, name="proj_in", grid=(t // tm,),
        in_specs=[row(k), _resident(w_in.shape), row(1), _resident(invf.shape), _resident(sgn.shape)] + [hbm] * nc,
        out_specs=[row(c) for c in widths] + [hbm] * nc,
        out_shape=[jax.ShapeDtypeStruct((t, c), dt) for c, dt in zip(widths, dtypes)]
        + (carry.out_shape if carry is not None else []),
        scratch_shapes=carry.scratch_shapes if carry is not None else [],
        compiler_params=_params("arbitrary" if carry is not None else "parallel"),
    )(h16, w_in, posf, invf, sgn, *(carry.operands if carry is not None else []))
    return tuple(res[:7]) + ((carry.results(res[7:]),) if carry is not None else ())


def _rope_bwd(dq, dk, cs):
    def fn(dq_v, dk_v, cs_v):
        cos, sin = cs_v[:, :LANES], -cs_v[:, LANES:]
        gq = _rope_apply(dq_v * (HEAD_DIM ** -0.5), cos, sin)
        gk = _rope_apply(dk_v, cos, sin)
        return [jnp.concatenate([gq, gk], axis=1)], []
    return _rowwise("rope_bwd", fn, [dq, dk, cs], [], [(2 * D_ATTN, BF16)])[0]


def _rms(v, w):
    return v * lax.rsqrt(jnp.mean(v * v, axis=-1, keepdims=True) + RMS_EPS) * w


def _ungroup(yg):
    w = HEADS_PER_GROUP * HEAD_DIM
    return jnp.concatenate([yg[:, GROUP_LANES * g:GROUP_LANES * g + w] for g in range(N_GROUPS)], axis=1)


def _group(xs):
    w = HEADS_PER_GROUP * HEAD_DIM
    parts = []
    for g in range(N_GROUPS):
        parts += [xs[:, w * g:w * (g + 1)], jnp.zeros((xs.shape[0], GROUP_LANES - w), xs.dtype)]
    return jnp.concatenate(parts, axis=1)


def _norms_fn(attn, yg, xs, z, w_attn, w_ssd, dskip):
    a_n = _rms(attn, w_attn)
    y = _ungroup(yg) + dskip * xs
    y_n = _rms(y * (z * jax.nn.sigmoid(z)), w_ssd)
    return jnp.concatenate([a_n, y_n], axis=1)


def _norms_fwd(attn, yg, xbc, z, w_attn, w_ssd, dskip):
    def fn(*v):
        return [_norms_fn(*v)], []
    return _rowwise("norms_fwd", fn, [attn, yg, (xbc, D_SSD), z], [w_attn, w_ssd, dskip], [(D_ATTN + D_SSD, BF16)])[0]


def _norms_bwd(attn, yg, xbc, z, w_attn, w_ssd, dskip, dcat):
    def fn(attn_v, yg_v, xs_v, z_v, dcat_v, wa_v, ws_v, dk_v):
        _, vjp = jax.vjp(_norms_fn, attn_v, yg_v, xs_v, z_v, wa_v, ws_v, dk_v)
        d_attn, d_yg, d_xs, d_z, d_wa, d_ws, d_dk = vjp(dcat_v)
        return [d_attn, d_yg, d_xs, d_z], [d_wa, d_ws, d_dk]
    return _rowwise("norms_bwd", fn, [attn, yg, (xbc, D_SSD), z, dcat], [w_attn, w_ssd, dskip],
                    [(D_ATTN, F32), (N_GROUPS * GROUP_LANES, F32), (D_SSD, F32), (D_SSD, BF16)], [(1, D_SSD)] * 3)


def _spread_sum(v, e):
    h1 = _bf(v)
    r1 = v - h1.astype(F32)
    h2 = _bf(r1)
    h3 = _bf(r1 - h2.astype(F32))
    return sum(jnp.dot(h, e, preferred_element_type=F32) for h in (h1, h2, h3))


@jax.custom_vjp
def _spread(v, e, e_t):
    return _spread_sum(v, e)


def _spread_fwd(v, e, e_t):
    return _spread_sum(v, e), (e, e_t)


def _spread_bwd(saved, g):
    e, e_t = saved
    return _spread_sum(g, e_t), jnp.zeros_like(e), jnp.zeros_like(e_t)


_spread.defvjp(_spread_fwd, _spread_bwd)


def _ssd_prep_fn(xs, dtp, dtb, alog, e_x, e_xt, e_a, e_at):
    dt = jax.nn.softplus(dtp + dtb)
    a = -jnp.exp(alog)
    xdtg = _group(xs) * _spread(dt, e_x, e_xt)
    dag = _spread(dt * a, e_a, e_at)
    return xdtg, dag


def _ssd_prep_fwd(xbc, dtp, dtb, alog, spreaders):
    def fn(xbc_v, dtp_v, dtb_v, alog_v, *e_v):
        xdtg, dag = _ssd_prep_fn(xbc_v[:, :D_SSD], dtp_v, dtb_v, alog_v, *e_v)
        return [xdtg, xbc_v[:, D_SSD:], dag], []
    return _rowwise("ssd_prep_fwd", fn, [xbc, dtp], [dtb, alog, *spreaders],
                    [(N_GROUPS * GROUP_LANES, BF16), (D_CONV - D_SSD, BF16), (N_GROUPS * LANES, F32)])


def _ssd_prep_bwd(xbc, dtp, dtb, alog, spreaders, dxdtg, ddag, dxs_a, db, dc):
    def fn(xs_v, dtp_v, dxdtg_v, ddag_v, dxs_a_v, db_v, dc_v, dtb_v, alog_v, *e_v):
        _, vjp = jax.vjp(lambda a, b, c, d: _ssd_prep_fn(a, b, c, d, *e_v), xs_v, dtp_v, dtb_v, alog_v)
        dxs, ddtp, ddtb, dalog = vjp((dxdtg_v, ddag_v))
        return [jnp.concatenate([dxs + dxs_a_v, db_v, dc_v], axis=1), ddtp], [ddtb, dalog]
    return _rowwise("ssd_prep_bwd", fn, [(xbc, D_SSD), dtp, dxdtg, ddag, dxs_a, db, dc], [dtb, alog, *spreaders],
                    [(D_CONV, F32), (LANES, BF16)], [(1, LANES)] * 2)


def _shift_down(u, d):
    if d == 0:
        return u
    row = lax.broadcasted_iota(jnp.int32, u.shape, 0)
    return jnp.where(row >= d, pltpu.roll(u, d, 0), 0.0)


def _shift_up(u, d):
    if d == 0:
        return u
    s = u.shape[0]
    row = lax.broadcasted_iota(jnp.int32, u.shape, 0)
    return jnp.where(row < s - d, pltpu.roll(u, s - d, 0), 0.0)


def _conv_pre(u, w, b):
    acc = b
    for k in range(CONV_WIDTH):
        acc = acc + w[k:k + 1, :] * _shift_down(u, CONV_WIDTH - 1 - k)
    return acc


def _conv_fwd(u, w, b, *, tc=256):
    nb, s, c = u.shape

    def body(u_ref, w_ref, b_ref, o_ref):
        pre = _conv_pre(u_ref[0], w_ref[...], b_ref[...])
        o_ref[0] = pre * jax.nn.sigmoid(pre)

    return pl.pallas_call(
        body, name="conv_fwd", grid=(c // tc, nb),
        in_specs=[pl.BlockSpec((1, s, tc), lambda j, i: (i, 0, j)), pl.BlockSpec((CONV_WIDTH, tc), lambda j, i: (0, j)),
                  pl.BlockSpec((1, tc), lambda j, i: (0, j))],
        out_specs=pl.BlockSpec((1, s, tc), lambda j, i: (i, 0, j)),
        out_shape=jax.ShapeDtypeStruct((nb, s, c), F32),
        compiler_params=_params("parallel", "parallel"),
    )(u, w, b)


def _conv_bwd(u, w, b, dout, *, tc=256):
    nb, s, c = u.shape

    def body(u_ref, w_ref, b_ref, d_ref, du_ref, dw_ref, db_ref):
        uv, wv = u_ref[0], w_ref[...]
        pre = _conv_pre(uv, wv, b_ref[...])
        sig = jax.nn.sigmoid(pre)
        dpre = d_ref[0] * (sig * (1.0 + pre * (1.0 - sig)))
        du = jnp.zeros_like(uv)
        dws = []
        for k in range(CONV_WIDTH):
            du = du + wv[k:k + 1, :] * _shift_up(dpre, CONV_WIDTH - 1 - k)
            dws.append(jnp.sum(dpre * _shift_down(uv, CONV_WIDTH - 1 - k), axis=0, keepdims=True))
        du_ref[0] = _bf(du)
        dwv = jnp.concatenate(dws + [jnp.zeros((8 - CONV_WIDTH, tc), F32)], axis=0)
        dbv = jnp.sum(dpre, axis=0, keepdims=True)
        first = pl.program_id(1) == 0

        @pl.when(first)
        def _():
            dw_ref[...] = dwv
            db_ref[...] = dbv

        @pl.when(jnp.logical_not(first))
        def _():
            dw_ref[...] += dwv
            db_ref[...] += dbv

    blk = pl.BlockSpec((1, s, tc), lambda j, i: (i, 0, j))
    return pl.pallas_call(
        body, name="conv_bwd", grid=(c // tc, nb),
        in_specs=[blk, pl.BlockSpec((CONV_WIDTH, tc), lambda j, i: (0, j)), pl.BlockSpec((1, tc), lambda j, i: (0, j)), blk],
        out_specs=[blk, pl.BlockSpec((8, tc), lambda j, i: (0, j)), pl.BlockSpec((1, tc), lambda j, i: (0, j))],
        out_shape=[jax.ShapeDtypeStruct((nb, s, c), BF16), jax.ShapeDtypeStruct((8, c), F32), jax.ShapeDtypeStruct((1, c), F32)],
        compiler_params=_params("parallel", "arbitrary"),
    )(u, w, b, dout)


FWD_KEY_BLOCK = 256


def _branch_bias_table(seq, kb):
    ratio = SEQ_BLOCK // kb
    key = np.arange(kb)[None, :, None]
    query = np.arange(SEQ_BLOCK)[None, None, :]
    delta = (np.arange(seq // kb)[:, None, None] - (ratio - 1)) * kb + query - key
    cnt = np.zeros(delta.shape, np.float64)
    for window, dilation in ((128, 1), (512, 4), (2048, 16)):
        cnt += (delta >= 0) & (delta % dilation == 0) & (delta <= window)
    return jnp.asarray(np.where(cnt > 0, np.log(np.maximum(cnt, 1.0)), NEG).astype(np.float32))


HEADS_PER_BLOCK = LANES // HEAD_DIM


def _head_rows(v, h):
    row = lax.broadcasted_iota(jnp.int32, v.shape, 0)
    return jnp.where((row >= HEAD_DIM * h) & (row < HEAD_DIM * (h + 1)), v, jnp.zeros_like(v))


def _attn_fwd(q, k, v, bias):
    nb_, s, _ = q.shape
    ab, kb = SEQ_BLOCK, FWD_KEY_BLOCK
    nblk, nkb, ratio = s // ab, s // kb, ab // kb

    def body(q_ref, k_ref, v_ref, b_ref, o_ref, lse_ref, vt_s):
        i = pl.program_id(2)

        @pl.when(i == 0)
        def _():
            for jb in range(nkb):
                vt_s[jb] = v_ref[0, kb * jb:kb * (jb + 1), :].T

        qt = q_ref[0].T
        qts = [_head_rows(qt, h) for h in range(HEADS_PER_BLOCK)]

        last = ratio * (i + 1) - 1

        def scores(j):
            kj = k_ref[0, pl.ds(pl.multiple_of(j * kb, kb), kb), :]
            return [jnp.dot(kj, qts[h], preferred_element_type=F32) for h in range(HEADS_PER_BLOCK)]

        def step(j, carry):
            ahead = scores(jnp.minimum(j + 1, last))
            lb = b_ref[ratio * i - j + (ratio - 1)]
            out = []
            for h in range(HEADS_PER_BLOCK):
                m, l, acc = carry[3 * h:3 * h + 3]
                st = carry[3 * HEADS_PER_BLOCK + h] + lb
                m_new = jnp.maximum(m, jnp.max(st, axis=0, keepdims=True))
                p = jnp.exp(st - m_new)
                a = jnp.exp(m - m_new)
                l = a * l + jnp.sum(p, axis=0, keepdims=True)
                vt = vt_s[j, HEAD_DIM * h:HEAD_DIM * (h + 1), :]
                acc = a * acc + jnp.dot(vt, _bf(p), preferred_element_type=F32)
                out += [m_new, l, acc]
            return tuple(out) + tuple(ahead)

        init = (jnp.full((1, ab), NEG, F32), jnp.zeros((1, ab), F32), jnp.zeros((HEAD_DIM, ab), F32)) * HEADS_PER_BLOCK
        res = lax.fori_loop(0, ratio * (i + 1), step, init + tuple(scores(0)))
        ot = jnp.concatenate([res[3 * h + 2] / res[3 * h + 1] for h in range(HEADS_PER_BLOCK)], axis=0)
        o_ref[0] = ot.T
        rows = [res[3 * h] + jnp.log(res[3 * h + 1]) for h in range(HEADS_PER_BLOCK)]
        lse_ref[0, 0, 0] = jnp.concatenate(rows + [jnp.zeros((8 - HEADS_PER_BLOCK, ab), F32)], axis=0)

    qblk = pl.BlockSpec((1, ab, LANES), lambda b, hp, i: (b, i, hp))
    full = pl.BlockSpec((1, s, LANES), lambda b, hp, i: (b, 0, hp))
    return pl.pallas_call(
        body, name="attn_fwd", grid=(nb_, D_ATTN // LANES, nblk),
        in_specs=[qblk, full, full, pl.BlockSpec((nkb, kb, ab), lambda b, hp, i: (0, 0, 0))],
        out_specs=[qblk, pl.BlockSpec((1, 1, 1, 8, ab), lambda b, hp, i: (b, hp, i, 0, 0))],
        out_shape=[jax.ShapeDtypeStruct((nb_, s, D_ATTN), F32),
                   jax.ShapeDtypeStruct((nb_, D_ATTN // LANES, nblk, 8, ab), F32)],
        scratch_shapes=[pltpu.VMEM((nkb, LANES, kb), BF16)],
        compiler_params=_params("parallel", "parallel", "arbitrary"),
    )(q, k, v, bias)


def _attn_bwd(q, k, v, o, do, lse, bias):
    nb_, s, _ = q.shape
    ab = SEQ_BLOCK
    nblk = s // ab

    nh = HEADS_PER_BLOCK

    def body(q_ref, k_ref, v_ref, o_ref, do_ref, lse_ref, b_ref, dq_ref, dk_ref, dv_ref,
             qt_s, dot_s, kt_s, dqt_s, do16_s, d_s, dk_acc, dv_acc):
        for jb in range(nblk):
            sl = slice(ab * jb, ab * (jb + 1))
            qt, kt = q_ref[0, sl, :].T, k_ref[0, sl, :].T
            do = do_ref[0, sl, :]
            dot = do.T
            prod = dot * o_ref[0, sl, :].T
            do16_s[sl, :] = _bf(do)
            for h in range(nh):
                qt_s[nh * jb + h] = _head_rows(qt, h)
                kt_s[nh * jb + h] = _head_rows(kt, h)
                dot_s[nh * jb + h] = _head_rows(_bf(dot), h)
            d_s[jb] = jnp.concatenate(
                [jnp.sum(prod[HEAD_DIM * h:HEAD_DIM * (h + 1)], axis=0, keepdims=True) for h in range(nh)]
                + [jnp.zeros((8 - nh, ab), F32)], axis=0)
            dqt_s[jb] = jnp.zeros((LANES, ab), F32)

        def outer(j, carry):
            ks = pl.ds(pl.multiple_of(j * ab, ab), ab)
            kj, vj = k_ref[0, ks, :], v_ref[0, ks, :]
            dk_acc[...] = jnp.zeros_like(dk_acc)
            dv_acc[...] = jnp.zeros_like(dv_acc)

            def inner(i, c2):
                qs = pl.ds(pl.multiple_of(i * ab, ab), ab)
                qi, doi = q_ref[0, qs, :], do16_s[qs, :]
                lb = b_ref[i - j]
                for h in range(nh):
                    st = jnp.dot(kj, qt_s[nh * i + h], preferred_element_type=F32) + lb
                    pt = jnp.exp(st - lse_ref[0, 0, i, h:h + 1, :])
                    dpt = jnp.dot(vj, dot_s[nh * i + h], preferred_element_type=F32)
                    dst16 = _bf(pt * (dpt - d_s[i, h:h + 1, :]))
                    dv_acc[h] += jnp.dot(_bf(pt), doi, preferred_element_type=F32)
                    dk_acc[h] += jnp.dot(dst16, qi, preferred_element_type=F32)
                    dqt_s[i] += jnp.dot(kt_s[nh * j + h], dst16, preferred_element_type=F32)
                return c2

            lax.fori_loop(j, nblk, inner, 0)
            lane = lax.broadcasted_iota(jnp.int32, (ab, LANES), 1)
            dk_ref[0, ks, :] = jnp.where(lane < HEAD_DIM, dk_acc[0], dk_acc[1])
            dv_ref[0, ks, :] = _bf(jnp.where(lane < HEAD_DIM, dv_acc[0], dv_acc[1]))
            return carry

        lax.fori_loop(0, nblk, outer, 0)
        for jb in range(nblk):
            dq_ref[0, ab * jb:ab * (jb + 1), :] = dqt_s[jb].T

    assert nh == 2
    full = pl.BlockSpec((1, s, LANES), lambda b, hp: (b, 0, hp))
    return pl.pallas_call(
        body, name="attn_bwd", grid=(nb_, D_ATTN // LANES),
        in_specs=[full] * 5 + [pl.BlockSpec((1, 1, nblk, 8, ab), lambda b, hp: (b, hp, 0, 0, 0)),
                               pl.BlockSpec((nblk, ab, ab), lambda b, hp: (0, 0, 0))],
        out_specs=[full, full, full],
        out_shape=[jax.ShapeDtypeStruct((nb_, s, D_ATTN), F32), jax.ShapeDtypeStruct((nb_, s, D_ATTN), F32),
                   jax.ShapeDtypeStruct((nb_, s, D_ATTN), BF16)],
        scratch_shapes=[pltpu.VMEM((nh * nblk, LANES, ab), BF16), pltpu.VMEM((nh * nblk, LANES, ab), BF16),
                        pltpu.VMEM((nh * nblk, LANES, ab), BF16), pltpu.VMEM((nblk, LANES, ab), F32),
                        pltpu.VMEM((s, LANES), BF16), pltpu.VMEM((nblk, 8, ab), F32),
                        pltpu.VMEM((nh, ab, LANES), F32), pltpu.VMEM((nh, ab, LANES), F32)],
        compiler_params=_params("parallel", "parallel"),
    )(q, k, v, o, do, lse, bias)


def _cumsum_fwd(dag):
    nb_, s, c = dag.shape
    ab = SEQ_BLOCK

    def body(a_ref, o_ref, ot_ref):
        r = lax.broadcasted_iota(jnp.int32, (ab, ab), 0)
        cc = lax.broadcasted_iota(jnp.int32, (ab, ab), 1)
        tri = (r >= cc).astype(F32)
        carry = jnp.zeros((1, c), F32)
        for i in range(s // ab):
            loc = jnp.dot(tri, a_ref[0, ab * i:ab * (i + 1), :], precision=HIGHEST, preferred_element_type=F32) + carry
            o_ref[0, ab * i:ab * (i + 1), :] = loc
            ot_ref[0, :, ab * i:ab * (i + 1)] = loc.T
            carry = loc[ab - 1:ab, :]

    return pl.pallas_call(
        body, name="ssd_cumsum", grid=(nb_,),
        in_specs=[pl.BlockSpec((1, s, c), lambda b: (b, 0, 0))],
        out_specs=[pl.BlockSpec((1, s, c), lambda b: (b, 0, 0)), pl.BlockSpec((1, c, s), lambda b: (b, 0, 0))],
        out_shape=[jax.ShapeDtypeStruct((nb_, s, c), F32), jax.ShapeDtypeStruct((nb_, c, s), F32)],
        compiler_params=_params("parallel"),
    )(dag)


def _cumsum_bwd(dcol, drow):
    nb_, s, c = dcol.shape
    ab = SEQ_BLOCK

    def body(c_ref, r_ref, o_ref):
        r = lax.broadcasted_iota(jnp.int32, (ab, ab), 0)
        cc = lax.broadcasted_iota(jnp.int32, (ab, ab), 1)
        tri = (r <= cc).astype(F32)
        carry = jnp.zeros((1, c), F32)
        for i in reversed(range(s // ab)):
            rows = r_ref[0, :, ab * i:ab * (i + 1)].T
            parts = []
            for g in range(N_GROUPS):
                parts += [rows[:, 8 * g:8 * (g + 1)], jnp.zeros((ab, LANES - 8), F32)]
            blk = c_ref[0, ab * i:ab * (i + 1), :] + jnp.concatenate(parts, axis=1)
            loc = jnp.dot(tri, blk, precision=HIGHEST, preferred_element_type=F32) + carry
            o_ref[0, ab * i:ab * (i + 1), :] = loc
            carry = loc[0:1, :]

    return pl.pallas_call(
        body, name="ssd_cumsum_bwd", grid=(nb_,),
        in_specs=[pl.BlockSpec((1, s, c), lambda b: (b, 0, 0)), pl.BlockSpec((1, N_GROUPS * 8, s), lambda b: (b, 0, 0))],
        out_specs=pl.BlockSpec((1, s, c), lambda b: (b, 0, 0)),
        out_shape=jax.ShapeDtypeStruct((nb_, s, c), F32),
        compiler_params=_params("parallel"),
    )(dcol, drow)


def _causal_ok(i, j):
    ab = SEQ_BLOCK
    r = lax.broadcasted_iota(jnp.int32, (ab, ab), 0)
    c = lax.broadcasted_iota(jnp.int32, (ab, ab), 1)
    return (r + (i - j) * ab) >= c


def _causal_ok_t(i, j):
    ab = SEQ_BLOCK
    r = lax.broadcasted_iota(jnp.int32, (ab, ab), 0)
    c = lax.broadcasted_iota(jnp.int32, (ab, ab), 1)
    return (c + (i - j) * ab) >= r


def _ssd_chunk(s_in, x, bm_t, cm, cb, acol, arow, a_prev, ok):
    q = x.shape[0]
    decay = jnp.exp(jnp.where(ok, acol - arow, NEG))
    y = jnp.dot(_bf(cb * decay), x, preferred_element_type=F32)
    y = y + jnp.exp(acol - a_prev) * jnp.dot(cm, _bf(s_in), preferred_element_type=F32)
    a_end = acol[q - 1:q, :]
    wx = _bf(jnp.exp(a_end - acol) * x.astype(F32))
    s_out = jnp.exp(a_end - a_prev) * s_in + jnp.dot(bm_t, wx, preferred_element_type=F32)
    return y, s_out


def _ssd_specs(s):
    xblk = pl.BlockSpec((1, s, GROUP_LANES), lambda b, g: (b, 0, g))
    bblk = pl.BlockSpec((1, s, D_STATE), lambda b, g: (b, 0, g))
    cblk = pl.BlockSpec((1, s, D_STATE), lambda b, g: (b, 0, N_GROUPS + g))
    tblk = pl.BlockSpec((1, 8, s), lambda b, g: (b, (LANES // 8) * g, 0))
    return xblk, bblk, cblk, tblk


def _chunk_views(i, j, x_ref, ac_ref, at_ref):
    ab = SEQ_BLOCK
    sl = slice(ab * i, ab * (i + 1))
    hs = slice(HEAD_DIM * j, HEAD_DIM * (j + 1))
    a_prev = jnp.zeros((1, 1), F32) if i == 0 else ac_ref[0, ab * i - 1:ab * i, j:j + 1]
    return sl, hs, ac_ref[0, sl, j:j + 1], at_ref[0, j:j + 1, sl], a_prev


def _ssd_fwd_chunked(xdtg, bc, acum, acum_t):
    nb_, s, _ = xdtg.shape
    ab = SEQ_BLOCK
    hpg = HEADS_PER_GROUP

    def body(x_ref, b_ref, c_ref, ac_ref, at_ref, y_ref):
        ok = _causal_ok(0, 0)
        states = [jnp.zeros((D_STATE, HEAD_DIM), F32) for _ in range(hpg)]
        for i in range(s // ab):
            bm, cm = b_ref[0, ab * i:ab * (i + 1), :], c_ref[0, ab * i:ab * (i + 1), :]
            bm_t = bm.T
            cb = jnp.dot(cm, bm_t, preferred_element_type=F32)
            ys = []
            for j in range(hpg):
                sl, hs, acol, arow, a_prev = _chunk_views(i, j, x_ref, ac_ref, at_ref)
                y, states[j] = _ssd_chunk(states[j], x_ref[0, sl, hs], bm_t, cm, cb, acol, arow, a_prev, ok)
                ys.append(y)
            y_ref[0, sl, :] = jnp.concatenate(ys + [jnp.zeros((ab, GROUP_LANES - hpg * HEAD_DIM), F32)], axis=1)

    xblk, bblk, cblk, tblk = _ssd_specs(s)
    ablk = pl.BlockSpec((1, s, LANES), lambda b, g: (b, 0, g))
    return pl.pallas_call(
        body, name="ssd_fwd", grid=(nb_, N_GROUPS), in_specs=[xblk, bblk, cblk, ablk, tblk], out_specs=xblk,
        out_shape=jax.ShapeDtypeStruct((nb_, s, N_GROUPS * GROUP_LANES), F32),
        compiler_params=_params("parallel", "parallel"),
    )(xdtg, bc, bc, acum, acum_t)


def _ssd_bwd_chunked(xdtg, bc, acum, acum_t, dyg):
    nb_, s, _ = xdtg.shape
    ab = SEQ_BLOCK
    nblk = s // ab
    hpg = HEADS_PER_GROUP

    def body(x_ref, b_ref, c_ref, ac_ref, at_ref, dy_ref, dx_ref, db_ref, dc_ref, dac_ref, dar_ref, s_s):
        ok = _causal_ok(0, 0)
        dx_ref[...] = jnp.zeros_like(dx_ref)
        dac_ref[...] = jnp.zeros_like(dac_ref)
        dar_ref[...] = jnp.zeros_like(dar_ref)
        states = [jnp.zeros((D_STATE, HEAD_DIM), F32) for _ in range(hpg)]
        for i in range(nblk):
            bm_t = b_ref[0, ab * i:ab * (i + 1), :].T
            for j in range(hpg):
                sl, hs, acol, arow, a_prev = _chunk_views(i, j, x_ref, ac_ref, at_ref)
                s_s[hpg * i + j] = states[j]
                if i + 1 < nblk:
                    a_end = acol[ab - 1:ab, :]
                    wx = _bf(jnp.exp(a_end - acol) * x_ref[0, sl, hs].astype(F32))
                    states[j] = jnp.exp(a_end - a_prev) * states[j] + jnp.dot(bm_t, wx, preferred_element_type=F32)
        ok_t = _causal_ok_t(0, 0)
        last_row = lax.broadcasted_iota(jnp.int32, (ab, 1), 0) == ab - 1
        d_state = [jnp.zeros((D_STATE, HEAD_DIM), F32) for _ in range(hpg)]
        pending = [jnp.zeros((1, 1), F32) for _ in range(hpg)]
        total = lambda v: jnp.sum(v, keepdims=True)
        for i in reversed(range(nblk)):
            bm, cm = b_ref[0, ab * i:ab * (i + 1), :], c_ref[0, ab * i:ab * (i + 1), :]
            cm_t = cm.T
            cbt = jnp.dot(bm, cm_t, preferred_element_type=F32)
            dcbt = jnp.zeros((ab, ab), F32)
            d_bm, d_cm = jnp.zeros((ab, D_STATE), F32), jnp.zeros((ab, D_STATE), F32)
            for j in range(hpg):
                sl, hs, acol, arow, a_prev = _chunk_views(i, j, x_ref, ac_ref, at_ref)
                x, dy = x_ref[0, sl, hs], dy_ref[0, sl, hs]
                dy16 = _bf(dy)
                s_in, g_out = s_s[hpg * i + j], d_state[j]
                s16, g16 = _bf(s_in), _bf(g_out)
                decay = jnp.exp(jnp.where(ok_t, arow - acol, NEG))
                gt = cbt * decay
                dgt = lax.dot_general(x, dy16, _NT, preferred_element_type=F32)
                d_x = jnp.dot(_bf(gt), dy16, preferred_element_type=F32)
                dcbt = dcbt + dgt * decay
                mm = dgt * gt
                d_arow = jnp.sum(mm, axis=0, keepdims=True)
                d_acol = -jnp.sum(mm, axis=1, keepdims=True)
                e = jnp.exp(acol - a_prev)
                edy16 = _bf(e * dy)
                d_cm = d_cm + lax.dot_general(edy16, s16, _NT, preferred_element_type=F32)
                d_s = jnp.dot(cm_t, edy16, preferred_element_type=F32)
                de_e = jnp.sum(dy * jnp.dot(cm, s16, preferred_element_type=F32), axis=1, keepdims=True) * e
                a_end = acol[ab - 1:ab, :]
                w = jnp.exp(a_end - acol)
                f = jnp.exp(a_end - a_prev)
                x32 = x.astype(F32)
                bg = jnp.dot(bm, g16, preferred_element_type=F32)
                d_x = d_x + w * bg
                d_bm = d_bm + lax.dot_general(_bf(w * x32), g16, _NT, preferred_element_type=F32)
                dw_w = jnp.sum(bg * x32, axis=1, keepdims=True) * w
                df_f = total(g_out * s_in) * f
                d_end = total(dw_w) + df_f
                d_acol = d_acol + de_e - dw_w + jnp.where(last_row, d_end + pending[j], 0.0)
                pending[j] = -total(de_e) - df_f
                d_state[j] = d_s + f * g_out
                dx_ref[0, sl, hs] = d_x
                dac_ref[0, sl, j:j + 1] = d_acol
                dar_ref[0, j:j + 1, sl] = d_arow
            dcbt16 = _bf(dcbt)
            db_ref[0, ab * i:ab * (i + 1), :] = d_bm + jnp.dot(dcbt16, cm, preferred_element_type=F32)
            dc_ref[0, ab * i:ab * (i + 1), :] = d_cm + lax.dot_general(dcbt16, bm, _TN, preferred_element_type=F32)

    xblk, bblk, cblk, tblk = _ssd_specs(s)
    ablk = pl.BlockSpec((1, s, LANES), lambda b, g: (b, 0, g))
    return pl.pallas_call(
        body, name="ssd_bwd", grid=(nb_, N_GROUPS),
        in_specs=[xblk, bblk, cblk, ablk, tblk, xblk],
        out_specs=[xblk, bblk, bblk, ablk, pl.BlockSpec((1, 8, s), lambda b, g: (b, g, 0))],
        out_shape=[jax.ShapeDtypeStruct((nb_, s, N_GROUPS * GROUP_LANES), F32),
                   jax.ShapeDtypeStruct((nb_, s, N_GROUPS * D_STATE), F32),
                   jax.ShapeDtypeStruct((nb_, s, N_GROUPS * D_STATE), F32),
                   jax.ShapeDtypeStruct((nb_, s, N_GROUPS * LANES), F32),
                   jax.ShapeDtypeStruct((nb_, N_GROUPS * 8, s), F32)],
        scratch_shapes=[pltpu.VMEM((nblk * hpg, D_STATE, HEAD_DIM), F32)],
        compiler_params=_params("parallel", "parallel"),
    )(xdtg, bc, bc, acum, acum_t, dyg)


def _interleave(wg, wu):
    k, f = wg.shape
    gi = GATE_UP_INTERLEAVE
    return jnp.stack([wg.reshape(k, f // gi, gi), wu.reshape(k, f // gi, gi)], axis=2).reshape(k, 2 * f)


def _head_expanders():
    e_x = np.zeros((LANES, N_GROUPS * GROUP_LANES), np.float32)
    e_a = np.zeros((LANES, N_GROUPS * LANES), np.float32)
    for h in range(N_HEADS):
        g, j = divmod(h, HEADS_PER_GROUP)
        e_x[h, GROUP_LANES * g + HEAD_DIM * j:GROUP_LANES * g + HEAD_DIM * (j + 1)] = 1.0
        e_a[h, LANES * g + j] = 1.0
    return [jnp.asarray(m, BF16) for m in (e_x, e_x.T, e_a, e_a.T)]


def _pad_lanes(v, n=LANES):
    return jnp.pad(v, ((0, 0), (0, n - v.shape[1])))


def _local_step(x, positions, target, w, late=None, early_grad_job=None):
    nb, s, d = x.shape
    t = nb * s
    x2 = x.reshape(t, d)
    tgt2 = target.reshape(t, d)
    (job_a, weights_a), (job_b, weights_b) = late if late is not None else ((None, None), (None, None))

    x16 = _bf(x2)
    wgu1 = _interleave(w["ffn1_gate"], w["ffn1_up"])
    ffn1 = _ffn_fwd("ffn1_fwd", x16, x2, wgu1, w["ffn1_down"], w["ln1_g"], w["ln1_b"], carry=job_a)
    au1, hm1, h1, r1, h1_16 = ffn1[:5]
    if job_a is not None:
        w = {**w, **weights_a(ffn1[5])}

    w_in = w["w_in"]
    wqk, wv, wz = w_in[:, :2 * D_ATTN], w_in[:, 2 * D_ATTN:3 * D_ATTN], w_in[:, 3 * D_ATTN:3 * D_ATTN + D_SSD]
    wxbc = w_in[:, 3 * D_ATTN + D_SSD:3 * D_ATTN + D_SSD + D_CONV]
    wdt = _pad_lanes(w_in[:, 3 * D_ATTN + D_SSD + D_CONV:])

    inv_freq = ROPE_THETA ** (-jnp.arange(0, ROPE_DIM, 2, dtype=F32) / ROPE_DIM)
    half = ROPE_DIM // 2
    head_invf = jnp.concatenate([inv_freq, inv_freq, jnp.zeros((HEAD_DIM - ROPE_DIM,), F32)])
    head_sgn = jnp.concatenate([-jnp.ones((half,), F32), jnp.ones((half,), F32), jnp.zeros((HEAD_DIM - ROPE_DIM,), F32)])
    invf = jnp.tile(head_invf, LANES // HEAD_DIM)[None, :]
    sgn = jnp.tile(head_sgn, LANES // HEAD_DIM)[None, :]
    posf = positions.astype(F32).reshape(t, 1)
    bias_fwd, bias_bwd = _branch_bias_table(s, FWD_KEY_BLOCK), _branch_bias_table(s, SEQ_BLOCK)
    spreaders = _head_expanders()
    dtb, alog = _pad_lanes(w["dt_bias"]), _pad_lanes(w["a_log"])
    dskip = jnp.repeat(w["d_skip"], HEAD_DIM, axis=1)

    proj = _proj_in(h1_16, _pad_lanes(w_in, w_in.shape[1] - N_HEADS + LANES), posf, invf, sgn, carry=job_b)
    q16, k16, v16, z, xbc_pre, dtp, cs = proj[:7]
    if job_b is not None:
        w = {**w, **weights_b(proj[7])}
    wgu2 = _interleave(w["ffn2_gate"], w["ffn2_up"])
    to3 =lambda a: a.reshape(nb, s, a.shape[-1])
    attn_o, lse = _attn_fwd(to3(q16), to3(k16), to3(v16), bias_fwd)

    xbc = _conv_fwd(to3(xbc_pre), w["conv_w"], w["conv_b"]).reshape(t, D_CONV)
    xdtg, bc16, dag = _ssd_prep_fwd(xbc, dtp, dtb, alog, spreaders)
    acum, acum_t = _cumsum_fwd(to3(dag))
    yg = _ssd_fwd_chunked(to3(xdtg), to3(bc16), acum, acum_t)

    cat = _norms_fwd(attn_o.reshape(t, D_ATTN), yg.reshape(t, -1), xbc, z, w["attn_norm_w"], w["ssd_norm_w"], dskip)
    h2, r2, h2_16 = _mm_res_ln("w_out_ln2", cat, w["w_out"], h1, w["ln2_g"], w["ln2_b"], scale=1.0)

    au2, hm2, _, r3, _ = _ffn_fwd("ffn2_fwd", h2_16, h2, wgu2, w["ffn2_down"], w["ln3_g"], w["ln3_b"])

    g = {}
    dr3, dr3_16, g["ln3_g"], g["ln3_b"], loss = _ln_loss_bwd("loss_ln3_bwd", r3, w["ln3_g"], w["ln3_b"], tgt2)

    dau2, dh2 = _ffn_bwd("ffn2_bwd", dr3_16, dr3, w["ffn2_down"].T, au2, wgu2.T)
    g["ffn2_down"] = _mm_tn("ffn2_down_dw", hm2, dr3_16, scale=0.5, tk=D_FF // 2, tn=512)
    g["ffn2_gate"], g["ffn2_up"] = _mm_tn_gate_up("ffn2_up_dw", h2_16, dau2)

    dr2, dr2_16, g["ln2_g"], g["ln2_b"] = _ln_bwd("ln2_bwd", r2, w["ln2_g"], w["ln2_b"], dh2)
    dcat = _mm("w_out_dx", [(dr2_16, w["w_out"].T)], tn=768)
    g["w_out"] = _mm_tn("w_out_dw", cat, dr2_16, tk=768, tn=1024)

    d_attn, dyg, dxs_a, dz16, g["attn_norm_w"], g["ssd_norm_w"], ddskip = _norms_bwd(
        attn_o.reshape(t, D_ATTN), yg.reshape(t, -1), xbc, z, w["attn_norm_w"], w["ssd_norm_w"], dskip, dcat)
    g["d_skip"] = ddskip.reshape(N_HEADS, HEAD_DIM).sum(axis=1)[None, :]

    dq, dk, dv16 = _attn_bwd(to3(q16), to3(k16), to3(v16), attn_o, to3(d_attn), lse, bias_bwd)
    dqk16 = _rope_bwd(dq.reshape(t, D_ATTN), dk.reshape(t, D_ATTN), cs)

    dxdtg, dbm, dcm, dacol, darow = _ssd_bwd_chunked(to3(xdtg), to3(bc16), acum, acum_t, to3(dyg))
    ddag = _cumsum_bwd(dacol, darow)
    dxbc, ddtp16, ddtb, dalog = _ssd_prep_bwd(xbc, dtp, dtb, alog, spreaders, dxdtg.reshape(t, -1), ddag.reshape(t, -1),
                                               dxs_a, dbm.reshape(t, -1), dcm.reshape(t, -1))
    g["dt_bias"], g["a_log"] = ddtb[:, :N_HEADS], dalog[:, :N_HEADS]
    dxbc_pre16, dconv_w, g["conv_b"] = _conv_bwd(to3(xbc_pre), w["conv_w"], w["conv_b"], to3(dxbc))
    g["conv_w"] = dconv_w[:CONV_WIDTH]
    dxbc_pre16 = dxbc_pre16.reshape(t, D_CONV)
    dv16 = dv16.reshape(t, D_ATTN)

    dh1 = _mm("w_in_dx", [(dqk16, wqk.T), (dv16, wv.T), (dz16, wz.T), (dxbc_pre16, wxbc.T), (ddtp16, wdt.T)],
              res=dr2, res_scale=ALPHA)
    g["w_in"] = _mm_tn_sections("w_in_dw", h1_16, [dqk16, dv16, dz16, dxbc_pre16, ddtp16])[:, :w_in.shape[1]]

    dr1, dr1_16, g["ln1_g"], g["ln1_b"] = _ln_bwd("ln1_bwd", r1, w["ln1_g"], w["ln1_b"], dh1)
    g["ffn1_down"] = _mm_tn("ffn1_down_dw", hm1, dr1_16, scale=0.5, tk=D_FF // 2, tn=512)
    ffn1b = _ffn_bwd("ffn1_bwd", dr1_16, dr1, w["ffn1_down"].T, au1, wgu1.T,
                     carry=None if early_grad_job is None else early_grad_job(g))
    dau1, dx = ffn1b[:2]
    early = ffn1b[2] if early_grad_job is not None else None
    g["ffn1_gate"], g["ffn1_up"] = _mm_tn_gate_up("ffn1_up_dw", x16, dau1)
    return loss, dx.reshape(nb, s, d), g, early


_HBM = pl.BlockSpec(memory_space=pltpu.HBM)
N_CHIPS = 4
N_DEVICES = 8


def _place():
    return lax.axis_index("x"), lax.axis_index("y"), lax.axis_index("c")


def _other_chips(x, y):
    return [(1 - x, y), (x, 1 - y), (1 - x, 1 - y)]


class _GatherJob:
    def __init__(self, shards):
        assert all((a.shape[0] // 2) % 16 == 0 for a in shards)
        self.n = len(shards)
        self.shapes = [a.shape for a in shards]
        self.operands = [a.reshape(2, a.shape[0] // 2, a.shape[1]) for a in shards]
        self.out_shape = [jax.ShapeDtypeStruct((N_CHIPS,) + a.shape, a.dtype) for a in self.operands]
        pair = pltpu.SemaphoreType.DMA((self.n, N_CHIPS - 1))
        self.scratch_shapes = [pair, pair, pair, pair]

    def results(self, outs):
        return [o.reshape((N_CHIPS,) + s) for o, s in zip(outs, self.shapes)]

    def phases(self, ins, outs, sems):
        n = self.n
        send_sems, recv_sems, fwd_send_sems, fwd_recv_sems = sems
        x, y, c = _place()
        me = 2 * x + y
        peers = _other_chips(x, y)

        def ici(t, p, src_chip):
            px, py = peers[p]
            return pltpu.make_async_remote_copy(
                ins[t].at[c] if src_chip is None else outs[t].at[src_chip, c],
                outs[t].at[me if src_chip is None else src_chip, c],
                send_sems.at[t, p], recv_sems.at[t, p], device_id=(px, py, c), device_id_type=MESH)

        def d2d(t, p, core):
            px, py = peers[p]
            return pltpu.make_async_remote_copy(
                outs[t].at[2 * px + py, core], outs[t].at[2 * px + py, core],
                fwd_send_sems.at[t, p], fwd_recv_sems.at[t, p], device_id=(x, y, 1 - c), device_id_type=MESH)

        pairs = [(t, p) for t in range(n) for p in range(N_CHIPS - 1)]

        def start():
            for t, p in pairs:
                ici(t, p, None).start()

        def forward():
            for t, p in pairs:
                px, py = peers[p]
                ici(t, p, 2 * px + py).wait_recv()
                d2d(t, p, c).start()

        def finish():
            for t, p in pairs:
                d2d(t, p, 1 - c).wait_recv()
            for t, p in pairs:
                ici(t, p, None).wait_send()
                d2d(t, p, c).wait_send()

        return start, forward, finish


class _ExchangeJob:
    def __init__(self, stacks):
        self.n = len(stacks)
        self.operands = list(stacks)
        self.out_shape = [jax.ShapeDtypeStruct(a.shape, a.dtype) for a in stacks]
        pair = pltpu.SemaphoreType.DMA((self.n, N_CHIPS - 1))
        self.scratch_shapes = [pair, pair]

    def results(self, outs):
        return list(outs)

    def phases(self, ins, outs, sems):
        send_sems, recv_sems = sems
        x, y, c = _place()
        me = 2 * x + y
        peers = _other_chips(x, y)
        pairs = [(t, p) for t in range(self.n) for p in range(N_CHIPS - 1)]

        def copy(t, p):
            px, py = peers[p]
            return pltpu.make_async_remote_copy(ins[t].at[2 * px + py], outs[t].at[me], send_sems.at[t, p],
                                                recv_sems.at[t, p], device_id=(px, py, c), device_id_type=MESH)

        def arrival(t, p):
            px, py = peers[p]
            return pltpu.make_async_remote_copy(ins[t].at[me], outs[t].at[2 * px + py], send_sems.at[t, p],
                                                recv_sems.at[t, p], device_id=(px, py, c), device_id_type=MESH)

        def start():
            for t, p in pairs:
                copy(t, p).start()

        def finish():
            for t, p in pairs:
                arrival(t, p).wait_recv()
            for t, p in pairs:
                copy(t, p).wait_send()

        return start, None, finish


def _run_job(job, name):
    n = job.n

    def body(*refs):
        for phase in job.phases(refs[:n], refs[n:2 * n], refs[2 * n:]):
            if phase is not None:
                phase()

    outs = pl.pallas_call(
        body, name=name, in_specs=[_HBM] * n, out_specs=[_HBM] * n,
        out_shape=job.out_shape, scratch_shapes=job.scratch_shapes,
    )(*job.operands)
    return job.results(outs)


def _sibling_halves(stacks, name):
    n = len(stacks)
    halves = [a.shape[1] // 2 for a in stacks]
    split = [a.reshape(a.shape[0], 2, h, a.shape[2]) for a, h in zip(stacks, halves)]

    def body(*refs):
        ins, outs = refs[:n], refs[n:2 * n]
        send_sems, recv_sems = refs[2 * n:]
        x, y, c = _place()
        cps = []
        for t in range(n):
            cp = pltpu.make_async_remote_copy(ins[t].at[:, 1 - c], outs[t], send_sems.at[t], recv_sems.at[t],
                                              device_id=(x, y, 1 - c), device_id_type=MESH)
            cp.start()
            cps.append(cp)
        for cp in cps:
            cp.wait()

    return pl.pallas_call(
        body, name=name,
        in_specs=[_HBM] * n, out_specs=[_HBM] * n,
        out_shape=[jax.ShapeDtypeStruct((a.shape[0], h, a.shape[2]), a.dtype) for a, h in zip(stacks, halves)],
        scratch_shapes=[pltpu.SemaphoreType.DMA((n,)), pltpu.SemaphoreType.DMA((n,))],
    )(*split)


def _sibling_swap(arrs):
    n = len(arrs)

    def body(*refs):
        ins, outs = refs[:n], refs[n:2 * n]
        send_sems, recv_sems = refs[2 * n:]
        x, y, c = _place()
        cps = []
        for t in range(n):
            cp = pltpu.make_async_remote_copy(ins[t], outs[t], send_sems.at[t], recv_sems.at[t],
                                              device_id=(x, y, 1 - c), device_id_type=MESH)
            cp.start()
            cps.append(cp)
        for cp in cps:
            cp.wait()

    return pl.pallas_call(
        body, name="sibling_swap",
        in_specs=[_HBM] * n, out_specs=[_HBM] * n,
        out_shape=[jax.ShapeDtypeStruct(a.shape, a.dtype) for a in arrs],
        scratch_shapes=[pltpu.SemaphoreType.DMA((n,)), pltpu.SemaphoreType.DMA((n,))],
    )(*arrs)


def _half_sum(name, own, other, core):
    k, r, cols = own.shape
    h = r // 2
    tr = next(cand for cand in (128, 176, 64, 32, 16) if h % cand == 0)
    nblk = h // tr

    def body(core_ref, own_ref, other_ref, o_ref):
        o_ref[...] = _bf(own_ref[...] + other_ref[...].astype(F32))

    grid_spec = pltpu.PrefetchScalarGridSpec(
        num_scalar_prefetch=1, grid=(nblk,),
        in_specs=[pl.BlockSpec((k, tr, cols), lambda i, core_ref: (0, i + core_ref[0] * nblk, 0)),
                  pl.BlockSpec((k, tr, cols), lambda i, core_ref: (0, i, 0))],
        out_specs=pl.BlockSpec((k, tr, cols), lambda i, core_ref: (0, i, 0)))
    return pl.pallas_call(
        body, name=name, grid_spec=grid_spec, out_shape=jax.ShapeDtypeStruct((k, h, cols), BF16),
        compiler_params=_params("parallel"),
    )(core.reshape(1).astype(jnp.int32), own, other)


def _small_allreduce(v):
    r = v.shape[0]

    def body(v_ref, tot_ref, slots, send_sems, recv_sems):
        x, y, c = _place()
        me = 4 * x + 2 * y + c
        slots[me] = v_ref[...]
        cps, peers = [], []
        for k in range(1, N_DEVICES):
            px = 1 - x if (k >> 2) & 1 else x
            py = 1 - y if (k >> 1) & 1 else y
            pc = 1 - c if k & 1 else c
            cp = pltpu.make_async_remote_copy(v_ref, slots.at[me], send_sems.at[k - 1], recv_sems.at[k - 1],
                                              device_id=(px, py, pc), device_id_type=MESH)
            cp.start()
            cps.append(cp)
            peers.append((px, py, pc))
        for k, (px, py, pc) in enumerate(peers):
            pltpu.make_async_remote_copy(v_ref, slots.at[4 * px + 2 * py + pc], send_sems.at[k], recv_sems.at[k],
                                         device_id=(px, py, pc), device_id_type=MESH).wait_recv()
        for cp in cps:
            cp.wait_send()
        acc = slots[0]
        for s in range(1, N_DEVICES):
            acc = acc + slots[s]
        tot_ref[...] = acc

    return pl.pallas_call(
        body, name="small_allreduce",
        in_specs=[pl.BlockSpec(memory_space=pltpu.VMEM)], out_specs=pl.BlockSpec(memory_space=pltpu.VMEM),
        out_shape=jax.ShapeDtypeStruct((r, LANES), F32),
        scratch_shapes=[pltpu.VMEM((N_DEVICES, r, LANES), F32), pltpu.SemaphoreType.DMA((N_DEVICES - 1,)),
                        pltpu.SemaphoreType.DMA((N_DEVICES - 1,))],
    )(v)


def _elementwise(name, fn, ins, out_dtypes):
    r, c = ins[0].shape[-2:]
    tr = next((cand for cand in (256, 176, 128, 64, 32, 16) if r % cand == 0), r)
    nin = len(ins)

    def body(*refs):
        outs = fn(*[ref[...] for ref in refs[:nin]])
        for o_ref, o in zip(refs[nin:], outs):
            o_ref[...] = o.astype(o_ref.dtype)

    in_specs = [pl.BlockSpec((tr, c), lambda i: (i, 0)) if a.ndim == 2 else pl.BlockSpec((a.shape[0], tr, c), lambda i: (0, i, 0))
                for a in ins]
    return pl.pallas_call(
        body, name=name, grid=(r // tr,), in_specs=in_specs,
        out_specs=[pl.BlockSpec((tr, c), lambda i: (i, 0)) for _ in out_dtypes],
        out_shape=[jax.ShapeDtypeStruct((r, c), dt) for dt in out_dtypes],
        compiler_params=_params("parallel"),
    )(*ins)


def _row_tile(rows):
    return next((cand for cand in (128, 176, 64, 32, 16) if rows % cand == 0), rows)


def _sum_slots(name, received, own, chip):
    _, r, cols = own.shape
    tr = _row_tile(r)

    def body(chip_ref, own_ref, a_ref, b_ref, c_ref, o_ref):
        o_ref[...] = ((own_ref[0].astype(F32) + a_ref[0].astype(F32)) + b_ref[0].astype(F32)) + c_ref[0].astype(F32)

    def slot(flip):
        return pl.BlockSpec((1, tr, cols), lambda i, chip_ref: (jnp.bitwise_xor(chip_ref[0], flip), i, 0))

    grid_spec = pltpu.PrefetchScalarGridSpec(
        num_scalar_prefetch=1, grid=(r // tr,), in_specs=[slot(0), slot(1), slot(2), slot(3)],
        out_specs=pl.BlockSpec((tr, cols), lambda i, chip_ref: (i, 0)))
    return pl.pallas_call(
        body, name=name, grid_spec=grid_spec, out_shape=jax.ShapeDtypeStruct((r, cols), F32),
        compiler_params=_params("parallel"),
    )(chip.reshape(1).astype(jnp.int32), own, received, received, received)


def _adamw_halves(name, mine, theirs, core, w, m, v):
    h, cols = mine.shape
    tr = _row_tile(h)
    nh = h // tr

    def body(core_ref, mine_ref, theirs_ref, w_ref, m_ref, v_ref, g_ref, d_ref, m2_ref, v2_ref):
        is_mine = (pl.program_id(0) // nh) == core_ref[0]
        g = jnp.where(is_mine, mine_ref[...], theirs_ref[...])
        outs = _adamw_math(g, w_ref[...], m_ref[...], v_ref[...])
        for ref, val in zip((g_ref, d_ref, m2_ref, v2_ref), outs):
            ref[...] = val

    half = pl.BlockSpec((tr, cols), lambda i, core_ref: (i % nh, 0))
    full = pl.BlockSpec((tr, cols), lambda i, core_ref: (i, 0))
    grid_spec = pltpu.PrefetchScalarGridSpec(
        num_scalar_prefetch=1, grid=(2 * nh,), in_specs=[half, half, full, full, full], out_specs=[full] * 4)
    return pl.pallas_call(
        body, name=name, grid_spec=grid_spec, out_shape=[jax.ShapeDtypeStruct((2 * h, cols), F32)] * 4,
        compiler_params=_params("parallel"),
    )(core.reshape(1).astype(jnp.int32), mine, theirs, w, m, v)


def _adamw_math(g, w_v, m_v, v_v):
    m2 = ADAM_B1 * m_v + (1.0 - ADAM_B1) * g
    v2 = ADAM_B2 * v_v + (1.0 - ADAM_B2) * jnp.square(g)
    m_hat = m2 / (1.0 - ADAM_B1 ** ADAM_STEP)
    v_hat = v2 / (1.0 - ADAM_B2 ** ADAM_STEP)
    delta = -ADAM_LR * (m_hat / (jnp.sqrt(v_hat) + ADAM_EPS) + ADAM_WD * w_v)
    return [g, delta, m2, v2]


def _adamw(name, g, w, m, v):
    return _elementwise(name, _adamw_math, [g, w, m, v], [F32] * 4)


_TRANSPOSED = ("ffn1_gate", "ffn1_up", "ffn2_gate", "ffn2_up")
_MATRICES = (("ffn1_gate", 0), ("ffn1_up", 0), ("ffn1_down", 0), ("w_in", 1), ("w_out", 0),
             ("ffn2_gate", 0), ("ffn2_up", 0), ("ffn2_down", 0))


def _block2d(a, name):
    return jnp.swapaxes(a, 1, 2)[0] if name in _TRANSPOSED else a[0]


def _block3d(a, name):
    return jnp.swapaxes(a[None], 1, 2) if name in _TRANSPOSED else a[None]
_VECTORS = ("ln1_g", "ln1_b", "conv_b", "dt_bias", "a_log", "d_skip", "attn_norm_w", "ssd_norm_w",
            "ln2_g", "ln2_b", "ln3_g", "ln3_b")
_WEIGHT_ORDER = ("ln1_g", "ln1_b", "ffn1_gate", "ffn1_up", "ffn1_down", "w_in", "conv_w", "conv_b", "dt_bias", "a_log",
                 "d_skip", "attn_norm_w", "ssd_norm_w", "w_out", "ln2_g", "ln2_b", "ffn2_gate", "ffn2_up", "ffn2_down",
                 "ln3_g", "ln3_b")


def _pack_rows(vectors):
    parts = []
    for vec in vectors:
        flat = vec.reshape(-1)
        parts.append(jnp.pad(flat, (0, (-flat.shape[0]) % LANES)))
    flat = jnp.concatenate(parts)
    flat = jnp.pad(flat, (0, (-flat.shape[0]) % (8 * LANES)))
    return flat.reshape(-1, LANES)


def _unpack_rows(packed, shapes):
    flat = packed.reshape(-1)
    out, off = [], 0
    for shape in shapes:
        size = int(np.prod(shape))
        out.append(flat[off:off + size].reshape(shape))
        off += size + (-size) % LANES
    return out


def _assemble(stack, own, chip, axis):
    blocks = [jnp.where(chip == s, own, stack[s]) for s in range(N_CHIPS)]
    return jnp.concatenate(blocks, axis=axis)


def _split(full, axis):
    if axis == 0:
        return full.reshape(N_CHIPS, -1, full.shape[1])
    cols = full.shape[1] // N_CHIPS
    return jnp.stack([full[:, cols * s:cols * (s + 1)] for s in range(N_CHIPS)])


def kernel(x, positions, ln1_g, ln1_b, ffn1_gate, ffn1_up, ffn1_down, w_in, conv_w, conv_b, dt_bias, a_log, d_skip, attn_norm_w, ssd_norm_w, w_out, ln2_g, ln2_b, ffn2_gate, ffn2_up, ffn2_down, ln3_g, ln3_b, loss_target, m_ln1_g, m_ln1_b, m_ffn1_gate, m_ffn1_up, m_ffn1_down, m_w_in, m_conv_w, m_conv_b, m_dt_bias, m_a_log, m_d_skip, m_attn_norm_w, m_ssd_norm_w, m_w_out, m_ln2_g, m_ln2_b, m_ffn2_gate, m_ffn2_up, m_ffn2_down, m_ln3_g, m_ln3_b, v_ln1_g, v_ln1_b, v_ffn1_gate, v_ffn1_up, v_ffn1_down, v_w_in, v_conv_w, v_conv_b, v_dt_bias, v_a_log, v_d_skip, v_attn_norm_w, v_ssd_norm_w, v_w_out, v_ln2_g, v_ln2_b, v_ffn2_gate, v_ffn2_up, v_ffn2_down, v_ln3_g, v_ln3_b):
    given = dict(locals())
    wts = {n: given[n] for n in _WEIGHT_ORDER}
    mom_m = {n: given["m_" + n] for n in _WEIGHT_ORDER}
    mom_v = {n: given["v_" + n] for n in _WEIGHT_ORDER}
    chip = 2 * lax.axis_index("x") + lax.axis_index("y")

    core = lax.axis_index("c")
    groups = [[(n, axis) for n, axis in _MATRICES if n.startswith(prefix)] for prefix in ("ffn1", "w_", "ffn2")]
    own16 = {n: _block2d(wts[n], n).astype(BF16) for n, _ in _MATRICES}

    def full_weights(group, results):
        out = {}
        for (n, axis), st in zip(group, results):
            whole = _assemble(st, own16[n], chip, axis)
            out[n] = whole.T if n in _TRANSPOSED else whole
        return out

    full = full_weights(groups[0], _run_job(_GatherJob([own16[n] for n, _ in groups[0]]), "gather_ffn1"))
    for n in _VECTORS:
        full[n] = wts[n]
    conv_rows = jnp.pad(wts["conv_w"][0], ((0, 32 - CONV_WIDTH), (0, 0)))

    def mixer_weights(results):
        out = full_weights(groups[1], results)
        out["conv_w"] = _assemble(results[-1], conv_rows, chip, 1)[:CONV_WIDTH]
        return out

    def ffn2_weights(results):
        return full_weights(groups[2], results)

    late = [(_GatherJob([own16[n] for n, _ in groups[1]] + [conv_rows]), mixer_weights),
            (_GatherJob([own16[n] for n, _ in groups[2]]), ffn2_weights)]

    chip_sums = {}

    def core_sums(g, which, tag):
        partials = [_split(g[n], axis) for n, axis in which]
        from_sibling = _sibling_halves([p.astype(BF16) for p in partials], "sibling_halves_" + tag)
        for (n, _), p, o in zip(which, partials, from_sibling):
            chip_sums[n] = _half_sum("core_sum_" + n, p, o, core)
        return _ExchangeJob([chip_sums[n] for n, _ in which])

    last = [(n, axis) for n, axis in _MATRICES if n in ("ffn1_gate", "ffn1_up")]
    early = [(n, axis) for n, axis in _MATRICES if (n, axis) not in last]
    loss, grad_x, g, received_early = _local_step(x, positions, loss_target, full, late,
                                                  lambda g_now: core_sums(g_now, early, "early"))
    received_last = _run_job(core_sums(g, last, "last"), "exchange_last")
    received = dict(zip([n for n, _ in last + early], received_last + received_early))
    half_totals = [_sum_slots("sum_partials_" + n, received[n], chip_sums[n], chip) for n, _ in _MATRICES]
    other_halves = _sibling_swap(half_totals)

    small_shapes = [g[n].shape for n in _VECTORS] + [g["conv_w"].shape, (1,)]
    total = _small_allreduce(_pack_rows([g[n] for n in _VECTORS] + [g["conv_w"], loss[0, :1]]))
    small = _unpack_rows(total, small_shapes)
    loss_out = small[-1].reshape(())

    grads, deltas, new_m, new_v = {}, {}, {}, {}
    for (n, _), mine, theirs in zip(_MATRICES, half_totals, other_halves):
        res = _adamw_halves("adamw_" + n, mine, theirs, core, _block2d(wts[n], n), _block2d(mom_m[n], n), _block2d(mom_v[n], n))
        grads[n], deltas[n], new_m[n], new_v[n] = [_block3d(r, n) for r in res]

    vec_shapes = [wts[n].shape for n in _VECTORS]
    res = _adamw("adamw_vectors", _pack_rows(small[:len(_VECTORS)]), _pack_rows([wts[n] for n in _VECTORS]),
                 _pack_rows([mom_m[n] for n in _VECTORS]), _pack_rows([mom_v[n] for n in _VECTORS]))
    for dst, packed in zip((grads, deltas, new_m, new_v), res):
        for n, val in zip(_VECTORS, _unpack_rows(packed, vec_shapes)):
            dst[n] = val

    cols = conv_w.shape[2]
    g_conv = lax.dynamic_slice_in_dim(small[len(_VECTORS)], chip * cols, cols, axis=1)
    res = _adamw("adamw_conv_w", g_conv, wts["conv_w"][0], mom_m["conv_w"][0], mom_v["conv_w"][0])
    grads["conv_w"], deltas["conv_w"], new_m["conv_w"], new_v["conv_w"] = [r[None] for r in res]

    return (loss_out, grad_x, *[grads[n] for n in _WEIGHT_ORDER], *[deltas[n] for n in _WEIGHT_ORDER],
            *[new_m[n] for n in _WEIGHT_ORDER], *[new_v[n] for n in _WEIGHT_ORDER])
```

```python
import functools

import numpy as np
import jax
import jax.numpy as jnp
from jax import lax
from jax.experimental import pallas as pl
from jax.experimental.pallas import tpu as pltpu

F32, BF16 = jnp.float32, jnp.bfloat16

D_MODEL = 1024
D_FF = 2816
N_HEADS = 12
HEAD_DIM = 64
D_ATTN = 768
D_SSD = 768
N_GROUPS = 4
HEADS_PER_GROUP = 3
D_STATE = 128
D_CONV = 1792
CONV_WIDTH = 4
ROPE_DIM = 16
ROPE_THETA = 500000.0
ALPHA = 2.0 ** 0.25
LN_EPS = 1e-5
RMS_EPS = 1e-6
ADAM_LR, ADAM_B1, ADAM_B2, ADAM_EPS, ADAM_WD, ADAM_STEP = 0.001, 0.9, 0.999, 1e-08, 0.01, 10

LANES = 128
GATE_UP_INTERLEAVE = 256
SEQ_BLOCK = 256
GROUP_LANES = 256
VMEM_LIMIT = 56 * 1024 * 1024
NEG = -1e30
MESH = pl.DeviceIdType.MESH
HIGHEST = lax.Precision.HIGHEST

_NT = (((1,), (1,)), ((), ()))
_TN = (((0,), (0,)), ((), ()))


def _params(*sem):
    return pltpu.CompilerParams(dimension_semantics=sem, vmem_limit_bytes=VMEM_LIMIT)


def _bf(v):
    return v.astype(BF16)


EPILOGUE_ROWS = 128


def _row_chunks(tm):
    return [slice(r, min(r + EPILOGUE_ROWS, tm)) for r in range(0, tm, EPILOGUE_ROWS)]


def _sigmoid(v):
    return 0.5 * jnp.tanh(0.5 * v) + 0.5


def _mm(name, pairs, *, scale=1.0, res=None, res_scale=1.0, out_dtype=F32, tm=512, tn=512):
    m, n = pairs[0][0].shape[0], pairs[0][1].shape[1]
    tm, tn = min(tm, m), min(tn, n)
    assert m % tm == 0 and n % tn == 0, (name, m, n, tm, tn)
    npair = len(pairs)

    def body(*refs):
        acc = None
        for a_ref, b_ref in zip(refs[:npair], refs[npair:2 * npair]):
            d = jnp.dot(_bf(a_ref[...]), b_ref[...], preferred_element_type=F32)
            acc = d if acc is None else acc + d
        if scale != 1.0:
            acc = acc * scale
        if res is not None:
            acc = acc + res_scale * refs[2 * npair][...]
        refs[-1][...] = acc.astype(out_dtype)

    in_specs = [pl.BlockSpec((tm, a.shape[1]), lambda i, j: (i, 0)) for a, _ in pairs]
    in_specs += [pl.BlockSpec((b.shape[0], tn), lambda i, j: (0, j)) for _, b in pairs]
    args = [a for a, _ in pairs] + [b for _, b in pairs]
    if res is not None:
        in_specs.append(pl.BlockSpec((tm, tn), lambda i, j: (i, j)))
        args.append(res)
    return pl.pallas_call(
        body, name=name, grid=(m // tm, n // tn), in_specs=in_specs,
        out_specs=pl.BlockSpec((tm, tn), lambda i, j: (i, j)),
        out_shape=jax.ShapeDtypeStruct((m, n), out_dtype),
        compiler_params=_params("parallel", "parallel"),
    )(*args)


def _mm_tn(name, x, dy, *, scale=1.0, tk=512, tn=512, tt=1024):
    t, k = x.shape
    n = dy.shape[1]
    tk, tn, tt = min(tk, k), min(tn, n), min(tt, t)
    assert k % tk == 0 and n % tn == 0 and t % tt == 0, (name, k, n, t)
    nt = t // tt

    def body(x_ref, dy_ref, o_ref):
        step = pl.program_id(2)
        d = lax.dot_general(_bf(x_ref[...]), _bf(dy_ref[...]), _TN, preferred_element_type=F32)

        @pl.when(step == 0)
        def _():
            o_ref[...] = d

        @pl.when(step > 0)
        def _():
            o_ref[...] += d

        if scale != 1.0:
            @pl.when(step == nt - 1)
            def _():
                o_ref[...] = o_ref[...] * scale

    return pl.pallas_call(
        body, name=name, grid=(k // tk, n // tn, nt),
        in_specs=[pl.BlockSpec((tt, tk), lambda i, j, s: (s, i)), pl.BlockSpec((tt, tn), lambda i, j, s: (s, j))],
        out_specs=pl.BlockSpec((tk, tn), lambda i, j, s: (i, j)),
        out_shape=jax.ShapeDtypeStruct((k, n), F32),
        compiler_params=_params("parallel", "parallel", "arbitrary"),
    )(x, dy)


def _mm_tn_sections(name, x, dys, *, tt=512):
    t, k = x.shape
    tt = min(tt, t)
    cuts = np.cumsum([0] + [d.shape[1] for d in dys]).tolist()
    ns = len(dys)

    def body(*refs):
        x_ref, o_ref = refs[0], refs[1 + ns]
        step = pl.program_id(0)
        xt = x_ref[...].T
        parts = [jnp.dot(xt, refs[1 + a][...], preferred_element_type=F32) for a in range(ns)]

        @pl.when(step == 0)
        def _():
            for a in range(ns):
                o_ref[:, cuts[a]:cuts[a + 1]] = parts[a]

        @pl.when(step > 0)
        def _():
            for a in range(ns):
                o_ref[:, cuts[a]:cuts[a + 1]] += parts[a]

    return pl.pallas_call(
        body, name=name, grid=(t // tt,),
        in_specs=[pl.BlockSpec((tt, k), lambda s: (s, 0))] + [pl.BlockSpec((tt, d.shape[1]), lambda s: (s, 0)) for d in dys],
        out_specs=pl.BlockSpec((k, cuts[-1]), lambda s: (0, 0)),
        out_shape=jax.ShapeDtypeStruct((k, cuts[-1]), F32),
        compiler_params=_params("arbitrary"),
    )(x, *dys)


def _mm_tn_gate_up(name, x, dau, *, tt=1024):
    t, k = x.shape
    gi = GATE_UP_INTERLEAVE
    nj = dau.shape[1] // (2 * gi)
    tt = min(tt, t)
    nt = t // tt

    def body(x_ref, dy_ref, g_ref, u_ref):
        step = pl.program_id(1)
        d = lax.dot_general(dy_ref[...], _bf(x_ref[...]), _TN, preferred_element_type=F32)

        @pl.when(step == 0)
        def _():
            g_ref[...] = d[:gi]
            u_ref[...] = d[gi:]

        @pl.when(step > 0)
        def _():
            g_ref[...] += d[:gi]
            u_ref[...] += d[gi:]

    out = pl.BlockSpec((gi, k), lambda j, s: (j, 0))
    return pl.pallas_call(
        body, name=name, grid=(nj, nt),
        in_specs=[pl.BlockSpec((tt, k), lambda j, s: (s, 0)), pl.BlockSpec((tt, 2 * gi), lambda j, s: (s, j))],
        out_specs=[out, out],
        out_shape=[jax.ShapeDtypeStruct((gi * nj, k), F32)] * 2,
        compiler_params=_params("parallel", "arbitrary"),
    )(x, dau)


def _carried(carry, ins, outs, sems, step, total):
    start, forward, finish = carry.phases(ins, outs, sems)
    pl.when(step == 0)(start)
    if forward is not None:
        pl.when(step == (3 * total) // 4)(forward)
    return lambda: pl.when(step == total - 1)(finish)


def _resident(shape):
    return pl.BlockSpec(shape, lambda i: (0,) * len(shape), pipeline_mode=pl.Buffered(1))


def _ffn_fwd(name, x16, res, wgu, wd, g, b, *, tm=512, carry=None):
    t, k = x16.shape
    gi = GATE_UP_INTERLEAVE
    nj, n, ni = wd.shape[0] // gi, wd.shape[1], t // tm
    nc = carry.n if carry is not None else 0

    def body(*refs):
        x_ref, res_ref, wgu_ref, wd_ref, g_ref, b_ref = refs[:6]
        au_ref, hm_ref, y_ref, r_ref, y16_ref = refs[6 + nc:11 + nc]
        if carry is not None:
            finish = _carried(carry, refs[6:6 + nc], refs[11 + nc:11 + 2 * nc], refs[11 + 2 * nc:], pl.program_id(0), ni)
        xv = x_ref[...]
        acc = jnp.zeros((tm, n), F32)
        for j in range(nj):
            au = jnp.dot(xv, wgu_ref[:, 2 * gi * j:2 * gi * (j + 1)], preferred_element_type=F32)
            a, u = au[:, :gi], au[:, gi:]
            au_ref[:, 2 * gi * j:2 * gi * (j + 1)] = _bf(au)
            hm = _bf(a * _sigmoid(a) * u)
            hm_ref[:, gi * j:gi * (j + 1)] = hm
            acc = acc + jnp.dot(hm, wd_ref[gi * j:gi * (j + 1), :], preferred_element_type=F32)
        r = ALPHA * res_ref[...] + 0.5 * acc
        r_ref[...] = r
        y = _layer_norm(r, g_ref[...], b_ref[...])
        y_ref[...] = y
        y16_ref[...] = _bf(y)
        if carry is not None:
            finish()

    row = lambda c: pl.BlockSpec((tm, c), lambda i: (i, 0))
    hbm = pl.BlockSpec(memory_space=pltpu.HBM)
    res_ = pl.pallas_call(
        body, name=name, grid=(ni,),
        in_specs=[row(k), row(n), _resident(wgu.shape), _resident(wd.shape), _resident(g.shape), _resident(b.shape)] + [hbm] * nc,
        out_specs=[row(2 * gi * nj), row(gi * nj), row(n), row(n), row(n)] + [hbm] * nc,
        out_shape=[jax.ShapeDtypeStruct((t, 2 * gi * nj), BF16), jax.ShapeDtypeStruct((t, gi * nj), BF16),
                   jax.ShapeDtypeStruct((t, n), F32), jax.ShapeDtypeStruct((t, n), F32), jax.ShapeDtypeStruct((t, n), BF16)]
        + (carry.out_shape if carry is not None else []),
        scratch_shapes=carry.scratch_shapes if carry is not None else [],
        compiler_params=_params("arbitrary" if carry is not None else "parallel"),
    )(x16, res, wgu, wd, g, b, *(carry.operands if carry is not None else []))
    return tuple(res_[:5]) + ((carry.results(res_[5:]),) if carry is not None else ())


def _ffn_bwd(name, dr16, dr, wdt, au, wgut, *, tm=512, carry=None):
    t, n = dr16.shape
    gi = GATE_UP_INTERLEAVE
    nj, ni = wdt.shape[1] // gi, t // tm
    nc = carry.n if carry is not None else 0

    def body(*refs):
        dr16_ref, dr_ref, wdt_ref, au_ref, wgut_ref = refs[:5]
        dau_ref, dx_ref = refs[5 + nc:7 + nc]
        if carry is not None:
            finish = _carried(carry, refs[5:5 + nc], refs[7 + nc:7 + 2 * nc], refs[7 + 2 * nc:], pl.program_id(0), ni)
        drv = dr16_ref[...]
        acc = jnp.zeros((tm, n), F32)
        for j in range(nj):
            dhm = jnp.dot(drv, wdt_ref[:, gi * j:gi * (j + 1)], preferred_element_type=F32) * 0.5
            au_v = au_ref[:, 2 * gi * j:2 * gi * (j + 1)].astype(F32)
            a, u = au_v[:, :gi], au_v[:, gi:]
            sig = _sigmoid(a)
            silu = a * sig
            dau = jnp.concatenate([_bf(dhm * u * (sig + silu - silu * sig)), _bf(dhm * silu)], axis=1)
            dau_ref[:, 2 * gi * j:2 * gi * (j + 1)] = dau
            acc = acc + jnp.dot(dau, wgut_ref[2 * gi * j:2 * gi * (j + 1), :], preferred_element_type=F32)
        dx_ref[...] = ALPHA * dr_ref[...] + acc
        if carry is not None:
            finish()

    row = lambda c: pl.BlockSpec((tm, c), lambda i: (i, 0))
    hbm = pl.BlockSpec(memory_space=pltpu.HBM)
    res_ = pl.pallas_call(
        body, name=name, grid=(ni,),
        in_specs=[row(n), row(n), _resident(wdt.shape), row(2 * gi * nj), _resident(wgut.shape)] + [hbm] * nc,
        out_specs=[row(2 * gi * nj), row(n)] + [hbm] * nc,
        out_shape=[jax.ShapeDtypeStruct((t, 2 * gi * nj), BF16), jax.ShapeDtypeStruct((t, n), F32)]
        + (carry.out_shape if carry is not None else []),
        scratch_shapes=carry.scratch_shapes if carry is not None else [],
        compiler_params=_params("arbitrary" if carry is not None else "parallel"),
    )(dr16, dr, wdt, au, wgut, *(carry.operands if carry is not None else []))
    return tuple(res_[:2]) + ((carry.results(res_[2:]),) if carry is not None else ())


def _layer_norm(r, g, b):
    mu = jnp.mean(r, axis=-1, keepdims=True)
    var = jnp.mean(jnp.square(r - mu), axis=-1, keepdims=True)
    return (r - mu) * lax.rsqrt(var + LN_EPS) * g + b


def _mm_res_ln(name, a, w, res, g, b, *, scale, tm=256):
    t, k = a.shape
    n = w.shape[1]

    def body(a_ref, w_ref, res_ref, g_ref, b_ref, y_ref, r_ref, y16_ref):
        for rows in _row_chunks(tm):
            r = ALPHA * res_ref[rows, :] + scale * jnp.dot(_bf(a_ref[rows, :]), w_ref[...], preferred_element_type=F32)
            r_ref[rows, :] = r
            y = _layer_norm(r, g_ref[...], b_ref[...])
            y_ref[rows, :] = y
            y16_ref[rows, :] = _bf(y)

    row = lambda c: pl.BlockSpec((tm, c), lambda i: (i, 0))
    const = lambda shape: pl.BlockSpec(shape, lambda i: (0, 0))
    return pl.pallas_call(
        body, name=name, grid=(t // tm,),
        in_specs=[row(k), const((k, n)), row(n), const((1, n)), const((1, n))],
        out_specs=[row(n), row(n), row(n)],
        out_shape=[jax.ShapeDtypeStruct((t, n), F32), jax.ShapeDtypeStruct((t, n), F32), jax.ShapeDtypeStruct((t, n), BF16)],
        compiler_params=_params("parallel"),
    )(a, w, res, g, b)


def _rowwise(name, fn, rows, consts, row_outs, acc_outs=(), tm=512):
    rows = [r if isinstance(r, tuple) else (r, r.shape[1]) for r in rows]
    t = rows[0][0].shape[0]
    tm = min(tm, t)
    assert t % tm == 0
    nr, nc, no, na = len(rows), len(consts), len(row_outs), len(acc_outs)

    def body(*refs):
        vals = [r[...] for r in refs[:nr + nc]]
        outs, accs = fn(*vals)
        for o_ref, o in zip(refs[nr + nc:nr + nc + no], outs):
            o_ref[...] = o.astype(o_ref.dtype)
        if na:
            step = pl.program_id(0)
            acc_refs = refs[nr + nc + no:]

            @pl.when(step == 0)
            def _():
                for a_ref, a in zip(acc_refs, accs):
                    a_ref[...] = a

            @pl.when(step > 0)
            def _():
                for a_ref, a in zip(acc_refs, accs):
                    a_ref[...] += a

    in_specs = [pl.BlockSpec((tm, w), lambda i: (i, 0)) for _, w in rows]
    in_specs += [pl.BlockSpec(c.shape, lambda i, nd=c.ndim: (0,) * nd) for c in consts]
    out_specs = [pl.BlockSpec((tm, c), lambda i: (i, 0)) for c, _ in row_outs]
    out_specs += [pl.BlockSpec(s, lambda i: (0, 0)) for s in acc_outs]
    out_shape = [jax.ShapeDtypeStruct((t, c), dt) for c, dt in row_outs]
    out_shape += [jax.ShapeDtypeStruct(s, F32) for s in acc_outs]
    res = pl.pallas_call(
        body, name=name, grid=(t // tm,), in_specs=in_specs, out_specs=out_specs, out_shape=out_shape,
        compiler_params=_params("arbitrary" if na else "parallel"),
    )(*[r for r, _ in rows], *consts)
    return res


def _ln_bwd(name, r, g, b, dy):
    def fn(r_v, dy_v, g_v, b_v):
        _, vjp = jax.vjp(_layer_norm, r_v, g_v, b_v)
        dr, dg, db = vjp(dy_v)
        return [dr, dr], [dg, db]
    return _rowwise(name, fn, [r, dy], [g, b], [(r.shape[1], F32), (r.shape[1], BF16)], [(1, r.shape[1])] * 2)


def _ln_loss_bwd(name, r, g, b, target):
    def fn(r_v, t_v, g_v, b_v):
        def loss_fn(rr, gg, bb):
            err = jnp.square(_layer_norm(rr, gg, bb) - t_v)
            return 0.5 * jnp.sum(jnp.mean(err, axis=-1, keepdims=True), axis=0, keepdims=True)
        loss, vjp = jax.vjp(loss_fn, r_v, g_v, b_v)
        dr, dg, db = vjp(jnp.ones((1, 1), F32))
        return [dr, dr], [dg, db, jnp.broadcast_to(loss, (1, LANES))]
    return _rowwise(name, fn, [r, target], [g, b], [(r.shape[1], F32), (r.shape[1], BF16)],
                    [(1, r.shape[1])] * 2 + [(1, LANES)])


def _rope_tables(posf, invf, sgn):
    ang = posf * invf
    return jnp.cos(ang), jnp.sin(ang) * sgn


def _rope_apply(tv, cos, sin):
    lane = lax.broadcasted_iota(jnp.int32, cos.shape, 1)
    first = (lane % HEAD_DIM) < (ROPE_DIM // 2)
    outs = []
    for gidx in range(tv.shape[1] // LANES):
        tg = tv[:, LANES * gidx:LANES * (gidx + 1)]
        sw = jnp.where(first, pltpu.roll(tg, LANES - ROPE_DIM // 2, 1), pltpu.roll(tg, ROPE_DIM // 2, 1))
        outs.append(tg * cos + sw * sin)
    return jnp.concatenate(outs, axis=1)


def _proj_in(h16, w_in, posf, invf, sgn, *, tm=512, carry=None):
    t, k = h16.shape
    cuts = [0, D_ATTN, 2 * D_ATTN, 3 * D_ATTN, 3 * D_ATTN + D_SSD, 3 * D_ATTN + D_SSD + D_CONV, w_in.shape[1]]
    nc = carry.n if carry is not None else 0

    def body(*refs):
        h_ref, w_ref, pos_ref, invf_ref, sgn_ref = refs[:5]
        q_ref, k_ref, v_ref, z_ref, xbc_ref, dt_ref, cs_ref = refs[5 + nc:12 + nc]
        if carry is not None:
            finish = _carried(carry, refs[5:5 + nc], refs[12 + nc:12 + 2 * nc], refs[12 + 2 * nc:], pl.program_id(0), t // tm)
        hv = h_ref[...]
        part = lambda a: jnp.dot(hv, w_ref[:, cuts[a]:cuts[a + 1]], preferred_element_type=F32)
        cos, sin = _rope_tables(pos_ref[...], invf_ref[...], sgn_ref[...])
        cs_ref[...] = jnp.concatenate([cos, sin], axis=1)
        q_ref[...] = _bf(_rope_apply(part(0), cos, sin) * (HEAD_DIM ** -0.5))
        k_ref[...] = _bf(_rope_apply(part(1), cos, sin))
        v_ref[...] = _bf(part(2))
        z_ref[...] = part(3)
        xbc_ref[...] = part(4)
        dt_ref[...] = part(5)
        if carry is not None:
            finish()

    row = lambda c: pl.BlockSpec((tm, c), lambda i: (i, 0))
    hbm = pl.BlockSpec(memory_space=pltpu.HBM)
    widths = [D_ATTN, D_ATTN, D_ATTN, D_SSD, D_CONV, LANES, 2 * LANES]
    dtypes = [BF16, BF16, BF16, F32, F32, F32, F32]
    res = pl.pallas_call(
        body, name="proj_in", grid=(t // tm,),
        in_specs=[row(k), _resident(w_in.shape), row(1), _resident(invf.shape), _resident(sgn.shape)] + [hbm] * nc,
        out_specs=[row(c) for c in widths] + [hbm] * nc,
        out_shape=[jax.ShapeDtypeStruct((t, c), dt) for c, dt in zip(widths, dtypes)]
        + (carry.out_shape if carry is not None else []),
        scratch_shapes=carry.scratch_shapes if carry is not None else [],
        compiler_params=_params("arbitrary" if carry is not None else "parallel"),
    )(h16, w_in, posf, invf, sgn, *(carry.operands if carry is not None else []))
    return tuple(res[:7]) + ((carry.results(res[7:]),) if carry is not None else ())


def _rope_bwd(dq, dk, cs):
    def fn(dq_v, dk_v, cs_v):
        cos, sin = cs_v[:, :LANES], -cs_v[:, LANES:]
        gq = _rope_apply(dq_v * (HEAD_DIM ** -0.5), cos, sin)
        gk = _rope_apply(dk_v, cos, sin)
        return [jnp.concatenate([gq, gk], axis=1)], []
    return _rowwise("rope_bwd", fn, [dq, dk, cs], [], [(2 * D_ATTN, BF16)])[0]


def _rms(v, w):
    return v * lax.rsqrt(jnp.mean(v * v, axis=-1, keepdims=True) + RMS_EPS) * w


def _ungroup(yg):
    w = HEADS_PER_GROUP * HEAD_DIM
    return jnp.concatenate([yg[:, GROUP_LANES * g:GROUP_LANES * g + w] for g in range(N_GROUPS)], axis=1)


def _group(xs):
    w = HEADS_PER_GROUP * HEAD_DIM
    parts = []
    for g in range(N_GROUPS):
        parts += [xs[:, w * g:w * (g + 1)], jnp.zeros((xs.shape[0], GROUP_LANES - w), xs.dtype)]
    return jnp.concatenate(parts, axis=1)


def _norms_fn(attn, yg, xs, z, w_attn, w_ssd, dskip):
    a_n = _rms(attn, w_attn)
    y = _ungroup(yg) + dskip * xs
    y_n = _rms(y * (z * jax.nn.sigmoid(z)), w_ssd)
    return jnp.concatenate([a_n, y_n], axis=1)


def _norms_fwd(attn, yg, xbc, z, w_attn, w_ssd, dskip):
    def fn(*v):
        return [_norms_fn(*v)], []
    return _rowwise("norms_fwd", fn, [attn, yg, (xbc, D_SSD), z], [w_attn, w_ssd, dskip], [(D_ATTN + D_SSD, BF16)])[0]


def _norms_bwd(attn, yg, xbc, z, w_attn, w_ssd, dskip, dcat):
    def fn(attn_v, yg_v, xs_v, z_v, dcat_v, wa_v, ws_v, dk_v):
        _, vjp = jax.vjp(_norms_fn, attn_v, yg_v, xs_v, z_v, wa_v, ws_v, dk_v)
        d_attn, d_yg, d_xs, d_z, d_wa, d_ws, d_dk = vjp(dcat_v)
        return [d_attn, d_yg, d_xs, d_z], [d_wa, d_ws, d_dk]
    return _rowwise("norms_bwd", fn, [attn, yg, (xbc, D_SSD), z, dcat], [w_attn, w_ssd, dskip],
                    [(D_ATTN, F32), (N_GROUPS * GROUP_LANES, F32), (D_SSD, F32), (D_SSD, BF16)], [(1, D_SSD)] * 3)


def _spread_sum(v, e):
    h1 = _bf(v)
    r1 = v - h1.astype(F32)
    h2 = _bf(r1)
    h3 = _bf(r1 - h2.astype(F32))
    return sum(jnp.dot(h, e, preferred_element_type=F32) for h in (h1, h2, h3))


@jax.custom_vjp
def _spread(v, e, e_t):
    return _spread_sum(v, e)


def _spread_fwd(v, e, e_t):
    return _spread_sum(v, e), (e, e_t)


def _spread_bwd(saved, g):
    e, e_t = saved
    return _spread_sum(g, e_t), jnp.zeros_like(e), jnp.zeros_like(e_t)


_spread.defvjp(_spread_fwd, _spread_bwd)


def _ssd_prep_fn(xs, dtp, dtb, alog, e_x, e_xt, e_a, e_at):
    dt = jax.nn.softplus(dtp + dtb)
    a = -jnp.exp(alog)
    xdtg = _group(xs) * _spread(dt, e_x, e_xt)
    dag = _spread(dt * a, e_a, e_at)
    return xdtg, dag


def _ssd_prep_fwd(xbc, dtp, dtb, alog, spreaders):
    def fn(xbc_v, dtp_v, dtb_v, alog_v, *e_v):
        xdtg, dag = _ssd_prep_fn(xbc_v[:, :D_SSD], dtp_v, dtb_v, alog_v, *e_v)
        return [xdtg, xbc_v[:, D_SSD:], dag], []
    return _rowwise("ssd_prep_fwd", fn, [xbc, dtp], [dtb, alog, *spreaders],
                    [(N_GROUPS * GROUP_LANES, BF16), (D_CONV - D_SSD, BF16), (N_GROUPS * LANES, F32)])


def _ssd_prep_bwd(xbc, dtp, dtb, alog, spreaders, dxdtg, ddag, dxs_a, db, dc):
    def fn(xs_v, dtp_v, dxdtg_v, ddag_v, dxs_a_v, db_v, dc_v, dtb_v, alog_v, *e_v):
        _, vjp = jax.vjp(lambda a, b, c, d: _ssd_prep_fn(a, b, c, d, *e_v), xs_v, dtp_v, dtb_v, alog_v)
        dxs, ddtp, ddtb, dalog = vjp((dxdtg_v, ddag_v))
        return [jnp.concatenate([dxs + dxs_a_v, db_v, dc_v], axis=1), ddtp], [ddtb, dalog]
    return _rowwise("ssd_prep_bwd", fn, [(xbc, D_SSD), dtp, dxdtg, ddag, dxs_a, db, dc], [dtb, alog, *spreaders],
                    [(D_CONV, F32), (LANES, BF16)], [(1, LANES)] * 2)


def _shift_down(u, d):
    if d == 0:
        return u
    row = lax.broadcasted_iota(jnp.int32, u.shape, 0)
    return jnp.where(row >= d, pltpu.roll(u, d, 0), 0.0)


def _shift_up(u, d):
    if d == 0:
        return u
    s = u.shape[0]
    row = lax.broadcasted_iota(jnp.int32, u.shape, 0)
    return jnp.where(row < s - d, pltpu.roll(u, s - d, 0), 0.0)


def _conv_pre(u, w, b):
    acc = b
    for k in range(CONV_WIDTH):
        acc = acc + w[k:k + 1, :] * _shift_down(u, CONV_WIDTH - 1 - k)
    return acc


def _conv_fwd(u, w, b, *, tc=256):
    nb, s, c = u.shape

    def body(u_ref, w_ref, b_ref, o_ref):
        pre = _conv_pre(u_ref[0], w_ref[...], b_ref[...])
        o_ref[0] = pre * jax.nn.sigmoid(pre)

    return pl.pallas_call(
        body, name="conv_fwd", grid=(c // tc, nb),
        in_specs=[pl.BlockSpec((1, s, tc), lambda j, i: (i, 0, j)), pl.BlockSpec((CONV_WIDTH, tc), lambda j, i: (0, j)),
                  pl.BlockSpec((1, tc), lambda j, i: (0, j))],
        out_specs=pl.BlockSpec((1, s, tc), lambda j, i: (i, 0, j)),
        out_shape=jax.ShapeDtypeStruct((nb, s, c), F32),
        compiler_params=_params("parallel", "parallel"),
    )(u, w, b)


def _conv_bwd(u, w, b, dout, *, tc=256):
    nb, s, c = u.shape

    def body(u_ref, w_ref, b_ref, d_ref, du_ref, dw_ref, db_ref):
        uv, wv = u_ref[0], w_ref[...]
        pre = _conv_pre(uv, wv, b_ref[...])
        sig = jax.nn.sigmoid(pre)
        dpre = d_ref[0] * (sig * (1.0 + pre * (1.0 - sig)))
        du = jnp.zeros_like(uv)
        dws = []
        for k in range(CONV_WIDTH):
            du = du + wv[k:k + 1, :] * _shift_up(dpre, CONV_WIDTH - 1 - k)
            dws.append(jnp.sum(dpre * _shift_down(uv, CONV_WIDTH - 1 - k), axis=0, keepdims=True))
        du_ref[0] = _bf(du)
        dwv = jnp.concatenate(dws + [jnp.zeros((8 - CONV_WIDTH, tc), F32)], axis=0)
        dbv = jnp.sum(dpre, axis=0, keepdims=True)
        first = pl.program_id(1) == 0

        @pl.when(first)
        def _():
            dw_ref[...] = dwv
            db_ref[...] = dbv

        @pl.when(jnp.logical_not(first))
        def _():
            dw_ref[...] += dwv
            db_ref[...] += dbv

    blk = pl.BlockSpec((1, s, tc), lambda j, i: (i, 0, j))
    return pl.pallas_call(
        body, name="conv_bwd", grid=(c // tc, nb),
        in_specs=[blk, pl.BlockSpec((CONV_WIDTH, tc), lambda j, i: (0, j)), pl.BlockSpec((1, tc), lambda j, i: (0, j)), blk],
        out_specs=[blk, pl.BlockSpec((8, tc), lambda j, i: (0, j)), pl.BlockSpec((1, tc), lambda j, i: (0, j))],
        out_shape=[jax.ShapeDtypeStruct((nb, s, c), BF16), jax.ShapeDtypeStruct((8, c), F32), jax.ShapeDtypeStruct((1, c), F32)],
        compiler_params=_params("parallel", "arbitrary"),
    )(u, w, b, dout)


FWD_KEY_BLOCK = 256


def _branch_bias_table(seq, kb):
    ratio = SEQ_BLOCK // kb
    key = np.arange(kb)[None, :, None]
    query = np.arange(SEQ_BLOCK)[None, None, :]
    delta = (np.arange(seq // kb)[:, None, None] - (ratio - 1)) * kb + query - key
    cnt = np.zeros(delta.shape, np.float64)
    for window, dilation in ((128, 1), (512, 4), (2048, 16)):
        cnt += (delta >= 0) & (delta % dilation == 0) & (delta <= window)
    return jnp.asarray(np.where(cnt > 0, np.log(np.maximum(cnt, 1.0)), NEG).astype(np.float32))


HEADS_PER_BLOCK = LANES // HEAD_DIM


def _head_rows(v, h):
    row = lax.broadcasted_iota(jnp.int32, v.shape, 0)
    return jnp.where((row >= HEAD_DIM * h) & (row < HEAD_DIM * (h + 1)), v, jnp.zeros_like(v))


def _attn_fwd(q, k, v, bias):
    nb_, s, _ = q.shape
    ab, kb = SEQ_BLOCK, FWD_KEY_BLOCK
    nblk, nkb, ratio = s // ab, s // kb, ab // kb

    def body(q_ref, k_ref, v_ref, b_ref, o_ref, lse_ref, vt_s):
        i = pl.program_id(2)

        @pl.when(i == 0)
        def _():
            for jb in range(nkb):
                vt_s[jb] = v_ref[0, kb * jb:kb * (jb + 1), :].T

        qt = q_ref[0].T
        qts = [_head_rows(qt, h) for h in range(HEADS_PER_BLOCK)]

        last = ratio * (i + 1) - 1

        def scores(j):
            kj = k_ref[0, pl.ds(pl.multiple_of(j * kb, kb), kb), :]
            return [jnp.dot(kj, qts[h], preferred_element_type=F32) for h in range(HEADS_PER_BLOCK)]

        def step(j, carry):
            ahead = scores(jnp.minimum(j + 1, last))
            lb = b_ref[ratio * i - j + (ratio - 1)]
            out = []
            for h in range(HEADS_PER_BLOCK):
                m, l, acc = carry[3 * h:3 * h + 3]
                st = carry[3 * HEADS_PER_BLOCK + h] + lb
                m_new = jnp.maximum(m, jnp.max(st, axis=0, keepdims=True))
                p = jnp.exp(st - m_new)
                a = jnp.exp(m - m_new)
                l = a * l + jnp.sum(p, axis=0, keepdims=True)
                vt = vt_s[j, HEAD_DIM * h:HEAD_DIM * (h + 1), :]
                acc = a * acc + jnp.dot(vt, _bf(p), preferred_element_type=F32)
                out += [m_new, l, acc]
            return tuple(out) + tuple(ahead)

        init = (jnp.full((1, ab), NEG, F32), jnp.zeros((1, ab), F32), jnp.zeros((HEAD_DIM, ab), F32)) * HEADS_PER_BLOCK
        res = lax.fori_loop(0, ratio * (i + 1), step, init + tuple(scores(0)))
        ot = jnp.concatenate([res[3 * h + 2] / res[3 * h + 1] for h in range(HEADS_PER_BLOCK)], axis=0)
        o_ref[0] = ot.T
        rows = [res[3 * h] + jnp.log(res[3 * h + 1]) for h in range(HEADS_PER_BLOCK)]
        lse_ref[0, 0, 0] = jnp.concatenate(rows + [jnp.zeros((8 - HEADS_PER_BLOCK, ab), F32)], axis=0)

    qblk = pl.BlockSpec((1, ab, LANES), lambda b, hp, i: (b, i, hp))
    full = pl.BlockSpec((1, s, LANES), lambda b, hp, i: (b, 0, hp))
    return pl.pallas_call(
        body, name="attn_fwd", grid=(nb_, D_ATTN // LANES, nblk),
        in_specs=[qblk, full, full, pl.BlockSpec((nkb, kb, ab), lambda b, hp, i: (0, 0, 0))],
        out_specs=[qblk, pl.BlockSpec((1, 1, 1, 8, ab), lambda b, hp, i: (b, hp, i, 0, 0))],
        out_shape=[jax.ShapeDtypeStruct((nb_, s, D_ATTN), F32),
                   jax.ShapeDtypeStruct((nb_, D_ATTN // LANES, nblk, 8, ab), F32)],
        scratch_shapes=[pltpu.VMEM((nkb, LANES, kb), BF16)],
        compiler_params=_params("parallel", "parallel", "arbitrary"),
    )(q, k, v, bias)


def _attn_bwd(q, k, v, o, do, lse, bias):
    nb_, s, _ = q.shape
    ab = SEQ_BLOCK
    nblk = s // ab

    nh = HEADS_PER_BLOCK

    def body(q_ref, k_ref, v_ref, o_ref, do_ref, lse_ref, b_ref, dq_ref, dk_ref, dv_ref,
             qt_s, dot_s, kt_s, dqt_s, do16_s, d_s, dk_acc, dv_acc):
        for jb in range(nblk):
            sl = slice(ab * jb, ab * (jb + 1))
            qt, kt = q_ref[0, sl, :].T, k_ref[0, sl, :].T
            do = do_ref[0, sl, :]
            dot = do.T
            prod = dot * o_ref[0, sl, :].T
            do16_s[sl, :] = _bf(do)
            for h in range(nh):
                qt_s[nh * jb + h] = _head_rows(qt, h)
                kt_s[nh * jb + h] = _head_rows(kt, h)
                dot_s[nh * jb + h] = _head_rows(_bf(dot), h)
            d_s[jb] = jnp.concatenate(
                [jnp.sum(prod[HEAD_DIM * h:HEAD_DIM * (h + 1)], axis=0, keepdims=True) for h in range(nh)]
                + [jnp.zeros((8 - nh, ab), F32)], axis=0)
            dqt_s[jb] = jnp.zeros((LANES, ab), F32)

        def outer(j, carry):
            ks = pl.ds(pl.multiple_of(j * ab, ab), ab)
            kj, vj = k_ref[0, ks, :], v_ref[0, ks, :]
            dk_acc[...] = jnp.zeros_like(dk_acc)
            dv_acc[...] = jnp.zeros_like(dv_acc)

            def inner(i, c2):
                qs = pl.ds(pl.multiple_of(i * ab, ab), ab)
                qi, doi = q_ref[0, qs, :], do16_s[qs, :]
                lb = b_ref[i - j]
                for h in range(nh):
                    st = jnp.dot(kj, qt_s[nh * i + h], preferred_element_type=F32) + lb
                    pt = jnp.exp(st - lse_ref[0, 0, i, h:h + 1, :])
                    dpt = jnp.dot(vj, dot_s[nh * i + h], preferred_element_type=F32)
                    dst16 = _bf(pt * (dpt - d_s[i, h:h + 1, :]))
                    dv_acc[h] += jnp.dot(_bf(pt), doi, preferred_element_type=F32)
                    dk_acc[h] += jnp.dot(dst16, qi, preferred_element_type=F32)
                    dqt_s[i] += jnp.dot(kt_s[nh * j + h], dst16, preferred_element_type=F32)
                return c2

            lax.fori_loop(j, nblk, inner, 0)
            lane = lax.broadcasted_iota(jnp.int32, (ab, LANES), 1)
            dk_ref[0, ks, :] = jnp.where(lane < HEAD_DIM, dk_acc[0], dk_acc[1])
            dv_ref[0, ks, :] = _bf(jnp.where(lane < HEAD_DIM, dv_acc[0], dv_acc[1]))
            return carry

        lax.fori_loop(0, nblk, outer, 0)
        for jb in range(nblk):
            dq_ref[0, ab * jb:ab * (jb + 1), :] = dqt_s[jb].T

    assert nh == 2
    full = pl.BlockSpec((1, s, LANES), lambda b, hp: (b, 0, hp))
    return pl.pallas_call(
        body, name="attn_bwd", grid=(nb_, D_ATTN // LANES),
        in_specs=[full] * 5 + [pl.BlockSpec((1, 1, nblk, 8, ab), lambda b, hp: (b, hp, 0, 0, 0)),
                               pl.BlockSpec((nblk, ab, ab), lambda b, hp: (0, 0, 0))],
        out_specs=[full, full, full],
        out_shape=[jax.ShapeDtypeStruct((nb_, s, D_ATTN), F32), jax.ShapeDtypeStruct((nb_, s, D_ATTN), F32),
                   jax.ShapeDtypeStruct((nb_, s, D_ATTN), BF16)],
        scratch_shapes=[pltpu.VMEM((nh * nblk, LANES, ab), BF16), pltpu.VMEM((nh * nblk, LANES, ab), BF16),
                        pltpu.VMEM((nh * nblk, LANES, ab), BF16), pltpu.VMEM((nblk, LANES, ab), F32),
                        pltpu.VMEM((s, LANES), BF16), pltpu.VMEM((nblk, 8, ab), F32),
                        pltpu.VMEM((nh, ab, LANES), F32), pltpu.VMEM((nh, ab, LANES), F32)],
        compiler_params=_params("parallel", "parallel"),
    )(q, k, v, o, do, lse, bias)


def _cumsum_fwd(dag):
    nb_, s, c = dag.shape
    ab = SEQ_BLOCK

    def body(a_ref, o_ref, ot_ref):
        r = lax.broadcasted_iota(jnp.int32, (ab, ab), 0)
        cc = lax.broadcasted_iota(jnp.int32, (ab, ab), 1)
        tri = (r >= cc).astype(F32)
        carry = jnp.zeros((1, c), F32)
        for i in range(s // ab):
            loc = jnp.dot(tri, a_ref[0, ab * i:ab * (i + 1), :], precision=HIGHEST, preferred_element_type=F32) + carry
            o_ref[0, ab * i:ab * (i + 1), :] = loc
            ot_ref[0, :, ab * i:ab * (i + 1)] = loc.T
            carry = loc[ab - 1:ab, :]

    return pl.pallas_call(
        body, name="ssd_cumsum", grid=(nb_,),
        in_specs=[pl.BlockSpec((1, s, c), lambda b: (b, 0, 0))],
        out_specs=[pl.BlockSpec((1, s, c), lambda b: (b, 0, 0)), pl.BlockSpec((1, c, s), lambda b: (b, 0, 0))],
        out_shape=[jax.ShapeDtypeStruct((nb_, s, c), F32), jax.ShapeDtypeStruct((nb_, c, s), F32)],
        compiler_params=_params("parallel"),
    )(dag)


def _cumsum_bwd(dcol, drow):
    nb_, s, c = dcol.shape
    ab = SEQ_BLOCK

    def body(c_ref, r_ref, o_ref):
        r = lax.broadcasted_iota(jnp.int32, (ab, ab), 0)
        cc = lax.broadcasted_iota(jnp.int32, (ab, ab), 1)
        tri = (r <= cc).astype(F32)
        carry = jnp.zeros((1, c), F32)
        for i in reversed(range(s // ab)):
            rows = r_ref[0, :, ab * i:ab * (i + 1)].T
            parts = []
            for g in range(N_GROUPS):
                parts += [rows[:, 8 * g:8 * (g + 1)], jnp.zeros((ab, LANES - 8), F32)]
            blk = c_ref[0, ab * i:ab * (i + 1), :] + jnp.concatenate(parts, axis=1)
            loc = jnp.dot(tri, blk, precision=HIGHEST, preferred_element_type=F32) + carry
            o_ref[0, ab * i:ab * (i + 1), :] = loc
            carry = loc[0:1, :]

    return pl.pallas_call(
        body, name="ssd_cumsum_bwd", grid=(nb_,),
        in_specs=[pl.BlockSpec((1, s, c), lambda b: (b, 0, 0)), pl.BlockSpec((1, N_GROUPS * 8, s), lambda b: (b, 0, 0))],
        out_specs=pl.BlockSpec((1, s, c), lambda b: (b, 0, 0)),
        out_shape=jax.ShapeDtypeStruct((nb_, s, c), F32),
        compiler_params=_params("parallel"),
    )(dcol, drow)


def _causal_ok(i, j):
    ab = SEQ_BLOCK
    r = lax.broadcasted_iota(jnp.int32, (ab, ab), 0)
    c = lax.broadcasted_iota(jnp.int32, (ab, ab), 1)
    return (r + (i - j) * ab) >= c


def _causal_ok_t(i, j):
    ab = SEQ_BLOCK
    r = lax.broadcasted_iota(jnp.int32, (ab, ab), 0)
    c = lax.broadcasted_iota(jnp.int32, (ab, ab), 1)
    return (c + (i - j) * ab) >= r


def _ssd_chunk(s_in, x, bm_t, cm, cb, acol, arow, a_prev, ok):
    q = x.shape[0]
    decay = jnp.exp(jnp.where(ok, acol - arow, NEG))
    y = jnp.dot(_bf(cb * decay), x, preferred_element_type=F32)
    y = y + jnp.exp(acol - a_prev) * jnp.dot(cm, _bf(s_in), preferred_element_type=F32)
    a_end = acol[q - 1:q, :]
    wx = _bf(jnp.exp(a_end - acol) * x.astype(F32))
    s_out = jnp.exp(a_end - a_prev) * s_in + jnp.dot(bm_t, wx, preferred_element_type=F32)
    return y, s_out


def _ssd_specs(s):
    xblk = pl.BlockSpec((1, s, GROUP_LANES), lambda b, g: (b, 0, g))
    bblk = pl.BlockSpec((1, s, D_STATE), lambda b, g: (b, 0, g))
    cblk = pl.BlockSpec((1, s, D_STATE), lambda b, g: (b, 0, N_GROUPS + g))
    tblk = pl.BlockSpec((1, 8, s), lambda b, g: (b, (LANES // 8) * g, 0))
    return xblk, bblk, cblk, tblk


def _chunk_views(i, j, x_ref, ac_ref, at_ref):
    ab = SEQ_BLOCK
    sl = slice(ab * i, ab * (i + 1))
    hs = slice(HEAD_DIM * j, HEAD_DIM * (j + 1))
    a_prev = jnp.zeros((1, 1), F32) if i == 0 else ac_ref[0, ab * i - 1:ab * i, j:j + 1]
    return sl, hs, ac_ref[0, sl, j:j + 1], at_ref[0, j:j + 1, sl], a_prev


def _ssd_fwd_chunked(xdtg, bc, acum, acum_t):
    nb_, s, _ = xdtg.shape
    ab = SEQ_BLOCK
    hpg = HEADS_PER_GROUP

    def body(x_ref, b_ref, c_ref, ac_ref, at_ref, y_ref):
        ok = _causal_ok(0, 0)
        states = [jnp.zeros((D_STATE, HEAD_DIM), F32) for _ in range(hpg)]
        for i in range(s // ab):
            bm, cm = b_ref[0, ab * i:ab * (i + 1), :], c_ref[0, ab * i:ab * (i + 1), :]
            bm_t = bm.T
            cb = jnp.dot(cm, bm_t, preferred_element_type=F32)
            ys = []
            for j in range(hpg):
                sl, hs, acol, arow, a_prev = _chunk_views(i, j, x_ref, ac_ref, at_ref)
                y, states[j] = _ssd_chunk(states[j], x_ref[0, sl, hs], bm_t, cm, cb, acol, arow, a_prev, ok)
                ys.append(y)
            y_ref[0, sl, :] = jnp.concatenate(ys + [jnp.zeros((ab, GROUP_LANES - hpg * HEAD_DIM), F32)], axis=1)

    xblk, bblk, cblk, tblk = _ssd_specs(s)
    ablk = pl.BlockSpec((1, s, LANES), lambda b, g: (b, 0, g))
    return pl.pallas_call(
        body, name="ssd_fwd", grid=(nb_, N_GROUPS), in_specs=[xblk, bblk, cblk, ablk, tblk], out_specs=xblk,
        out_shape=jax.ShapeDtypeStruct((nb_, s, N_GROUPS * GROUP_LANES), F32),
        compiler_params=_params("parallel", "parallel"),
    )(xdtg, bc, bc, acum, acum_t)


def _ssd_bwd_chunked(xdtg, bc, acum, acum_t, dyg):
    nb_, s, _ = xdtg.shape
    ab = SEQ_BLOCK
    nblk = s // ab
    hpg = HEADS_PER_GROUP

    def body(x_ref, b_ref, c_ref, ac_ref, at_ref, dy_ref, dx_ref, db_ref, dc_ref, dac_ref, dar_ref, s_s):
        ok = _causal_ok(0, 0)
        dx_ref[...] = jnp.zeros_like(dx_ref)
        dac_ref[...] = jnp.zeros_like(dac_ref)
        dar_ref[...] = jnp.zeros_like(dar_ref)
        states = [jnp.zeros((D_STATE, HEAD_DIM), F32) for _ in range(hpg)]
        for i in range(nblk):
            bm_t = b_ref[0, ab * i:ab * (i + 1), :].T
            for j in range(hpg):
                sl, hs, acol, arow, a_prev = _chunk_views(i, j, x_ref, ac_ref, at_ref)
                s_s[hpg * i + j] = states[j]
                if i + 1 < nblk:
                    a_end = acol[ab - 1:ab, :]
                    wx = _bf(jnp.exp(a_end - acol) * x_ref[0, sl, hs].astype(F32))
                    states[j] = jnp.exp(a_end - a_prev) * states[j] + jnp.dot(bm_t, wx, preferred_element_type=F32)
        ok_t = _causal_ok_t(0, 0)
        last_row = lax.broadcasted_iota(jnp.int32, (ab, 1), 0) == ab - 1
        d_state = [jnp.zeros((D_STATE, HEAD_DIM), F32) for _ in range(hpg)]
        pending = [jnp.zeros((1, 1), F32) for _ in range(hpg)]
        total = lambda v: jnp.sum(v, keepdims=True)
        for i in reversed(range(nblk)):
            bm, cm = b_ref[0, ab * i:ab * (i + 1), :], c_ref[0, ab * i:ab * (i + 1), :]
            cm_t = cm.T
            cbt = jnp.dot(bm, cm_t, preferred_element_type=F32)
            dcbt = jnp.zeros((ab, ab), F32)
            d_bm, d_cm = jnp.zeros((ab, D_STATE), F32), jnp.zeros((ab, D_STATE), F32)
            for j in range(hpg):
                sl, hs, acol, arow, a_prev = _chunk_views(i, j, x_ref, ac_ref, at_ref)
                x, dy = x_ref[0, sl, hs], dy_ref[0, sl, hs]
                dy16 = _bf(dy)
                s_in, g_out = s_s[hpg * i + j], d_state[j]
                s16, g16 = _bf(s_in), _bf(g_out)
                decay = jnp.exp(jnp.where(ok_t, arow - acol, NEG))
                gt = cbt * decay
                dgt = lax.dot_general(x, dy16, _NT, preferred_element_type=F32)
                d_x = jnp.dot(_bf(gt), dy16, preferred_element_type=F32)
                dcbt = dcbt + dgt * decay
                mm = dgt * gt
                d_arow = jnp.sum(mm, axis=0, keepdims=True)
                d_acol = -jnp.sum(mm, axis=1, keepdims=True)
                e = jnp.exp(acol - a_prev)
                edy16 = _bf(e * dy)
                d_cm = d_cm + lax.dot_general(edy16, s16, _NT, preferred_element_type=F32)
                d_s = jnp.dot(cm_t, edy16, preferred_element_type=F32)
                de_e = jnp.sum(dy * jnp.dot(cm, s16, preferred_element_type=F32), axis=1, keepdims=True) * e
                a_end = acol[ab - 1:ab, :]
                w = jnp.exp(a_end - acol)
                f = jnp.exp(a_end - a_prev)
                x32 = x.astype(F32)
                bg = jnp.dot(bm, g16, preferred_element_type=F32)
                d_x = d_x + w * bg
                d_bm = d_bm + lax.dot_general(_bf(w * x32), g16, _NT, preferred_element_type=F32)
                dw_w = jnp.sum(bg * x32, axis=1, keepdims=True) * w
                df_f = total(g_out * s_in) * f
                d_end = total(dw_w) + df_f
                d_acol = d_acol + de_e - dw_w + jnp.where(last_row, d_end + pending[j], 0.0)
                pending[j] = -total(de_e) - df_f
                d_state[j] = d_s + f * g_out
                dx_ref[0, sl, hs] = d_x
                dac_ref[0, sl, j:j + 1] = d_acol
                dar_ref[0, j:j + 1, sl] = d_arow
            dcbt16 = _bf(dcbt)
            db_ref[0, ab * i:ab * (i + 1), :] = d_bm + jnp.dot(dcbt16, cm, preferred_element_type=F32)
            dc_ref[0, ab * i:ab * (i + 1), :] = d_cm + lax.dot_general(dcbt16, bm, _TN, preferred_element_type=F32)

    xblk, bblk, cblk, tblk = _ssd_specs(s)
    ablk = pl.BlockSpec((1, s, LANES), lambda b, g: (b, 0, g))
    return pl.pallas_call(
        body, name="ssd_bwd", grid=(nb_, N_GROUPS),
        in_specs=[xblk, bblk, cblk, ablk, tblk, xblk],
        out_specs=[xblk, bblk, bblk, ablk, pl.BlockSpec((1, 8, s), lambda b, g: (b, g, 0))],
        out_shape=[jax.ShapeDtypeStruct((nb_, s, N_GROUPS * GROUP_LANES), F32),
                   jax.ShapeDtypeStruct((nb_, s, N_GROUPS * D_STATE), F32),
                   jax.ShapeDtypeStruct((nb_, s, N_GROUPS * D_STATE), F32),
                   jax.ShapeDtypeStruct((nb_, s, N_GROUPS * LANES), F32),
                   jax.ShapeDtypeStruct((nb_, N_GROUPS * 8, s), F32)],
        scratch_shapes=[pltpu.VMEM((nblk * hpg, D_STATE, HEAD_DIM), F32)],
        compiler_params=_params("parallel", "parallel"),
    )(xdtg, bc, bc, acum, acum_t, dyg)


def _interleave(wg, wu):
    k, f = wg.shape
    gi = GATE_UP_INTERLEAVE
    return jnp.stack([wg.reshape(k, f // gi, gi), wu.reshape(k, f // gi, gi)], axis=2).reshape(k, 2 * f)


def _head_expanders():
    e_x = np.zeros((LANES, N_GROUPS * GROUP_LANES), np.float32)
    e_a = np.zeros((LANES, N_GROUPS * LANES), np.float32)
    for h in range(N_HEADS):
        g, j = divmod(h, HEADS_PER_GROUP)
        e_x[h, GROUP_LANES * g + HEAD_DIM * j:GROUP_LANES * g + HEAD_DIM * (j + 1)] = 1.0
        e_a[h, LANES * g + j] = 1.0
    return [jnp.asarray(m, BF16) for m in (e_x, e_x.T, e_a, e_a.T)]


def _pad_lanes(v, n=LANES):
    return jnp.pad(v, ((0, 0), (0, n - v.shape[1])))


def _local_step(x, positions, target, w, late=None, early_grad_job=None):
    nb, s, d = x.shape
    t = nb * s
    x2 = x.reshape(t, d)
    tgt2 = target.reshape(t, d)
    (job_a, weights_a), (job_b, weights_b) = late if late is not None else ((None, None), (None, None))

    x16 = _bf(x2)
    wgu1 = _interleave(w["ffn1_gate"], w["ffn1_up"])
    ffn1 = _ffn_fwd("ffn1_fwd", x16, x2, wgu1, w["ffn1_down"], w["ln1_g"], w["ln1_b"], carry=job_a)
    au1, hm1, h1, r1, h1_16 = ffn1[:5]
    if job_a is not None:
        w = {**w, **weights_a(ffn1[5])}

    w_in = w["w_in"]
    wqk, wv, wz = w_in[:, :2 * D_ATTN], w_in[:, 2 * D_ATTN:3 * D_ATTN], w_in[:, 3 * D_ATTN:3 * D_ATTN + D_SSD]
    wxbc = w_in[:, 3 * D_ATTN + D_SSD:3 * D_ATTN + D_SSD + D_CONV]
    wdt = _pad_lanes(w_in[:, 3 * D_ATTN + D_SSD + D_CONV:])

    inv_freq = ROPE_THETA ** (-jnp.arange(0, ROPE_DIM, 2, dtype=F32) / ROPE_DIM)
    half = ROPE_DIM // 2
    head_invf = jnp.concatenate([inv_freq, inv_freq, jnp.zeros((HEAD_DIM - ROPE_DIM,), F32)])
    head_sgn = jnp.concatenate([-jnp.ones((half,), F32), jnp.ones((half,), F32), jnp.zeros((HEAD_DIM - ROPE_DIM,), F32)])
    invf = jnp.tile(head_invf, LANES // HEAD_DIM)[None, :]
    sgn = jnp.tile(head_sgn, LANES // HEAD_DIM)[None, :]
    posf = positions.astype(F32).reshape(t, 1)
    bias_fwd, bias_bwd = _branch_bias_table(s, FWD_KEY_BLOCK), _branch_bias_table(s, SEQ_BLOCK)
    spreaders = _head_expanders()
    dtb, alog = _pad_lanes(w["dt_bias"]), _pad_lanes(w["a_log"])
    dskip = jnp.repeat(w["d_skip"], HEAD_DIM, axis=1)

    proj = _proj_in(h1_16, _pad_lanes(w_in, w_in.shape[1] - N_HEADS + LANES), posf, invf, sgn, carry=job_b)
    q16, k16, v16, z, xbc_pre, dtp, cs = proj[:7]
    if job_b is not None:
        w = {**w, **weights_b(proj[7])}
    wgu2 = _interleave(w["ffn2_gate"], w["ffn2_up"])
    to3 =lambda a: a.reshape(nb, s, a.shape[-1])
    attn_o, lse = _attn_fwd(to3(q16), to3(k16), to3(v16), bias_fwd)

    xbc = _conv_fwd(to3(xbc_pre), w["conv_w"], w["conv_b"]).reshape(t, D_CONV)
    xdtg, bc16, dag = _ssd_prep_fwd(xbc, dtp, dtb, alog, spreaders)
    acum, acum_t = _cumsum_fwd(to3(dag))
    yg = _ssd_fwd_chunked(to3(xdtg), to3(bc16), acum, acum_t)

    cat = _norms_fwd(attn_o.reshape(t, D_ATTN), yg.reshape(t, -1), xbc, z, w["attn_norm_w"], w["ssd_norm_w"], dskip)
    h2, r2, h2_16 = _mm_res_ln("w_out_ln2", cat, w["w_out"], h1, w["ln2_g"], w["ln2_b"], scale=1.0)

    au2, hm2, _, r3, _ = _ffn_fwd("ffn2_fwd", h2_16, h2, wgu2, w["ffn2_down"], w["ln3_g"], w["ln3_b"])

    g = {}
    dr3, dr3_16, g["ln3_g"], g["ln3_b"], loss = _ln_loss_bwd("loss_ln3_bwd", r3, w["ln3_g"], w["ln3_b"], tgt2)

    dau2, dh2 = _ffn_bwd("ffn2_bwd", dr3_16, dr3, w["ffn2_down"].T, au2, wgu2.T)
    g["ffn2_down"] = _mm_tn("ffn2_down_dw", hm2, dr3_16, scale=0.5, tk=D_FF // 2, tn=512)
    g["ffn2_gate"], g["ffn2_up"] = _mm_tn_gate_up("ffn2_up_dw", h2_16, dau2)

    dr2, dr2_16, g["ln2_g"], g["ln2_b"] = _ln_bwd("ln2_bwd", r2, w["ln2_g"], w["ln2_b"], dh2)
    dcat = _mm("w_out_dx", [(dr2_16, w["w_out"].T)], tn=768)
    g["w_out"] = _mm_tn("w_out_dw", cat, dr2_16, tk=768, tn=1024)

    d_attn, dyg, dxs_a, dz16, g["attn_norm_w"], g["ssd_norm_w"], ddskip = _norms_bwd(
        attn_o.reshape(t, D_ATTN), yg.reshape(t, -1), xbc, z, w["attn_norm_w"], w["ssd_norm_w"], dskip, dcat)
    g["d_skip"] = ddskip.reshape(N_HEADS, HEAD_DIM).sum(axis=1)[None, :]

    dq, dk, dv16 = _attn_bwd(to3(q16), to3(k16), to3(v16), attn_o, to3(d_attn), lse, bias_bwd)
    dqk16 = _rope_bwd(dq.reshape(t, D_ATTN), dk.reshape(t, D_ATTN), cs)

    dxdtg, dbm, dcm, dacol, darow = _ssd_bwd_chunked(to3(xdtg), to3(bc16), acum, acum_t, to3(dyg))
    ddag = _cumsum_bwd(dacol, darow)
    dxbc, ddtp16, ddtb, dalog = _ssd_prep_bwd(xbc, dtp, dtb, alog, spreaders, dxdtg.reshape(t, -1), ddag.reshape(t, -1),
                                               dxs_a, dbm.reshape(t, -1), dcm.reshape(t, -1))
    g["dt_bias"], g["a_log"] = ddtb[:, :N_HEADS], dalog[:, :N_HEADS]
    dxbc_pre16, dconv_w, g["conv_b"] = _conv_bwd(to3(xbc_pre), w["conv_w"], w["conv_b"], to3(dxbc))
    g["conv_w"] = dconv_w[:CONV_WIDTH]
    dxbc_pre16 = dxbc_pre16.reshape(t, D_CONV)
    dv16 = dv16.reshape(t, D_ATTN)

    dh1 = _mm("w_in_dx", [(dqk16, wqk.T), (dv16, wv.T), (dz16, wz.T), (dxbc_pre16, wxbc.T), (ddtp16, wdt.T)],
              res=dr2, res_scale=ALPHA)
    g["w_in"] = _mm_tn_sections("w_in_dw", h1_16, [dqk16, dv16, dz16, dxbc_pre16, ddtp16])[:, :w_in.shape[1]]

    dr1, dr1_16, g["ln1_g"], g["ln1_b"] = _ln_bwd("ln1_bwd", r1, w["ln1_g"], w["ln1_b"], dh1)
    g["ffn1_down"] = _mm_tn("ffn1_down_dw", hm1, dr1_16, scale=0.5, tk=D_FF // 2, tn=512)
    ffn1b = _ffn_bwd("ffn1_bwd", dr1_16, dr1, w["ffn1_down"].T, au1, wgu1.T,
                     carry=None if early_grad_job is None else early_grad_job(g))
    dau1, dx = ffn1b[:2]
    early = ffn1b[2] if early_grad_job is not None else None
    g["ffn1_gate"], g["ffn1_up"] = _mm_tn_gate_up("ffn1_up_dw", x16, dau1)
    return loss, dx.reshape(nb, s, d), g, early


_HBM = pl.BlockSpec(memory_space=pltpu.HBM)
N_CHIPS = 4
N_DEVICES = 8


def _place():
    return lax.axis_index("x"), lax.axis_index("y"), lax.axis_index("c")


def _other_chips(x, y):
    return [(1 - x, y), (x, 1 - y), (1 - x, 1 - y)]


class _GatherJob:
    def __init__(self, shards):
        assert all((a.shape[0] // 2) % 16 == 0 for a in shards)
        self.n = len(shards)
        self.shapes = [a.shape for a in shards]
        self.operands = [a.reshape(2, a.shape[0] // 2, a.shape[1]) for a in shards]
        self.out_shape = [jax.ShapeDtypeStruct((N_CHIPS,) + a.shape, a.dtype) for a in self.operands]
        pair = pltpu.SemaphoreType.DMA((self.n, N_CHIPS - 1))
        one = pltpu.SemaphoreType.DMA((self.n,))
        self.scratch_shapes = [pair, pair, pair, pair, one, one]

    def results(self, outs):
        return [o.reshape((N_CHIPS,) + s) for o, s in zip(outs, self.shapes)]

    def phases(self, ins, outs, sems):
        n = self.n
        send_sems, recv_sems, fwd_send_sems, fwd_recv_sems, own_send_sems, own_recv_sems = sems
        x, y, c = _place()
        me = 2 * x + y
        peers = _other_chips(x, y)

        def own(t):
            return pltpu.make_async_remote_copy(ins[t], outs[t].at[me], own_send_sems.at[t], own_recv_sems.at[t],
                                                device_id=(x, y, 1 - c), device_id_type=MESH)

        def ici(t, p, src_chip):
            px, py = peers[p]
            return pltpu.make_async_remote_copy(
                ins[t].at[c] if src_chip is None else outs[t].at[src_chip, c],
                outs[t].at[me if src_chip is None else src_chip, c],
                send_sems.at[t, p], recv_sems.at[t, p], device_id=(px, py, c), device_id_type=MESH)

        def d2d(t, p, core):
            px, py = peers[p]
            return pltpu.make_async_remote_copy(
                outs[t].at[2 * px + py, core], outs[t].at[2 * px + py, core],
                fwd_send_sems.at[t, p], fwd_recv_sems.at[t, p], device_id=(x, y, 1 - c), device_id_type=MESH)

        pairs = [(t, p) for t in range(n) for p in range(N_CHIPS - 1)]

        def start():
            for t, p in pairs:
                ici(t, p, None).start()
            for t in range(n):
                own(t).start()

        def forward():
            for t, p in pairs:
                px, py = peers[p]
                ici(t, p, 2 * px + py).wait_recv()
                d2d(t, p, c).start()

        def finish():
            for t, p in pairs:
                d2d(t, p, 1 - c).wait_recv()
            for t in range(n):
                own(t).wait()
            for t, p in pairs:
                ici(t, p, None).wait_send()
                d2d(t, p, c).wait_send()

        return start, forward, finish


class _ExchangeJob:
    def __init__(self, stacks):
        self.n = len(stacks)
        self.operands = list(stacks)
        self.out_shape = [jax.ShapeDtypeStruct(a.shape, a.dtype) for a in stacks]
        pair = pltpu.SemaphoreType.DMA((self.n, N_CHIPS - 1))
        self.scratch_shapes = [pair, pair]

    def results(self, outs):
        return list(outs)

    def phases(self, ins, outs, sems):
        send_sems, recv_sems = sems
        x, y, c = _place()
        me = 2 * x + y
        peers = _other_chips(x, y)
        pairs = [(t, p) for t in range(self.n) for p in range(N_CHIPS - 1)]

        def copy(t, p):
            px, py = peers[p]
            return pltpu.make_async_remote_copy(ins[t].at[2 * px + py], outs[t].at[me], send_sems.at[t, p],
                                                recv_sems.at[t, p], device_id=(px, py, c), device_id_type=MESH)

        def arrival(t, p):
            px, py = peers[p]
            return pltpu.make_async_remote_copy(ins[t].at[me], outs[t].at[2 * px + py], send_sems.at[t, p],
                                                recv_sems.at[t, p], device_id=(px, py, c), device_id_type=MESH)

        def start():
            for t, p in pairs:
                copy(t, p).start()

        def finish():
            for t, p in pairs:
                arrival(t, p).wait_recv()
            for t, p in pairs:
                copy(t, p).wait_send()

        return start, None, finish


def _run_job(job, name):
    n = job.n

    def body(*refs):
        for phase in job.phases(refs[:n], refs[n:2 * n], refs[2 * n:]):
            if phase is not None:
                phase()

    outs = pl.pallas_call(
        body, name=name, in_specs=[_HBM] * n, out_specs=[_HBM] * n,
        out_shape=job.out_shape, scratch_shapes=job.scratch_shapes,
    )(*job.operands)
    return job.results(outs)


def _sibling_halves(stacks, name):
    n = len(stacks)
    halves = [a.shape[1] // 2 for a in stacks]
    split = [a.reshape(a.shape[0], 2, h, a.shape[2]) for a, h in zip(stacks, halves)]

    def body(*refs):
        ins, outs = refs[:n], refs[n:2 * n]
        send_sems, recv_sems = refs[2 * n:]
        x, y, c = _place()
        cps = []
        for t in range(n):
            cp = pltpu.make_async_remote_copy(ins[t].at[:, 1 - c], outs[t], send_sems.at[t], recv_sems.at[t],
                                              device_id=(x, y, 1 - c), device_id_type=MESH)
            cp.start()
            cps.append(cp)
        for cp in cps:
            cp.wait()

    return pl.pallas_call(
        body, name=name,
        in_specs=[_HBM] * n, out_specs=[_HBM] * n,
        out_shape=[jax.ShapeDtypeStruct((a.shape[0], h, a.shape[2]), a.dtype) for a, h in zip(stacks, halves)],
        scratch_shapes=[pltpu.SemaphoreType.DMA((n,)), pltpu.SemaphoreType.DMA((n,))],
    )(*split)


def _sibling_swap(arrs):
    n = len(arrs)

    def body(*refs):
        ins, outs = refs[:n], refs[n:2 * n]
        send_sems, recv_sems = refs[2 * n:]
        x, y, c = _place()
        cps = []
        for t in range(n):
            cp = pltpu.make_async_remote_copy(ins[t], outs[t], send_sems.at[t], recv_sems.at[t],
                                              device_id=(x, y, 1 - c), device_id_type=MESH)
            cp.start()
            cps.append(cp)
        for cp in cps:
            cp.wait()

    return pl.pallas_call(
        body, name="sibling_swap",
        in_specs=[_HBM] * n, out_specs=[_HBM] * n,
        out_shape=[jax.ShapeDtypeStruct(a.shape, a.dtype) for a in arrs],
        scratch_shapes=[pltpu.SemaphoreType.DMA((n,)), pltpu.SemaphoreType.DMA((n,))],
    )(*arrs)


def _half_sum(name, own, other, core):
    k, r, cols = own.shape
    h = r // 2
    tr = next(cand for cand in (128, 176, 64, 32, 16) if h % cand == 0)
    nblk = h // tr

    def body(core_ref, own_ref, other_ref, o_ref):
        o_ref[...] = _bf(own_ref[...] + other_ref[...].astype(F32))

    grid_spec = pltpu.PrefetchScalarGridSpec(
        num_scalar_prefetch=1, grid=(nblk,),
        in_specs=[pl.BlockSpec((k, tr, cols), lambda i, core_ref: (0, i + core_ref[0] * nblk, 0)),
                  pl.BlockSpec((k, tr, cols), lambda i, core_ref: (0, i, 0))],
        out_specs=pl.BlockSpec((k, tr, cols), lambda i, core_ref: (0, i, 0)))
    return pl.pallas_call(
        body, name=name, grid_spec=grid_spec, out_shape=jax.ShapeDtypeStruct((k, h, cols), BF16),
        compiler_params=_params("parallel"),
    )(core.reshape(1).astype(jnp.int32), own, other)


def _small_allreduce(v):
    r = v.shape[0]

    def body(v_ref, tot_ref, slots, send_sems, recv_sems):
        x, y, c = _place()
        me = 4 * x + 2 * y + c
        slots[me] = v_ref[...]
        cps, peers = [], []
        for k in range(1, N_DEVICES):
            px = 1 - x if (k >> 2) & 1 else x
            py = 1 - y if (k >> 1) & 1 else y
            pc = 1 - c if k & 1 else c
            cp = pltpu.make_async_remote_copy(v_ref, slots.at[me], send_sems.at[k - 1], recv_sems.at[k - 1],
                                              device_id=(px, py, pc), device_id_type=MESH)
            cp.start()
            cps.append(cp)
            peers.append((px, py, pc))
        for k, (px, py, pc) in enumerate(peers):
            pltpu.make_async_remote_copy(v_ref, slots.at[4 * px + 2 * py + pc], send_sems.at[k], recv_sems.at[k],
                                         device_id=(px, py, pc), device_id_type=MESH).wait_recv()
        for cp in cps:
            cp.wait_send()
        acc = slots[0]
        for s in range(1, N_DEVICES):
            acc = acc + slots[s]
        tot_ref[...] = acc

    return pl.pallas_call(
        body, name="small_allreduce",
        in_specs=[pl.BlockSpec(memory_space=pltpu.VMEM)], out_specs=pl.BlockSpec(memory_space=pltpu.VMEM),
        out_shape=jax.ShapeDtypeStruct((r, LANES), F32),
        scratch_shapes=[pltpu.VMEM((N_DEVICES, r, LANES), F32), pltpu.SemaphoreType.DMA((N_DEVICES - 1,)),
                        pltpu.SemaphoreType.DMA((N_DEVICES - 1,))],
    )(v)


def _elementwise(name, fn, ins, out_dtypes):
    r, c = ins[0].shape[-2:]
    tr = next((cand for cand in (256, 176, 128, 64, 32, 16) if r % cand == 0), r)
    nin = len(ins)

    def body(*refs):
        outs = fn(*[ref[...] for ref in refs[:nin]])
        for o_ref, o in zip(refs[nin:], outs):
            o_ref[...] = o.astype(o_ref.dtype)

    in_specs = [pl.BlockSpec((tr, c), lambda i: (i, 0)) if a.ndim == 2 else pl.BlockSpec((a.shape[0], tr, c), lambda i: (0, i, 0))
                for a in ins]
    return pl.pallas_call(
        body, name=name, grid=(r // tr,), in_specs=in_specs,
        out_specs=[pl.BlockSpec((tr, c), lambda i: (i, 0)) for _ in out_dtypes],
        out_shape=[jax.ShapeDtypeStruct((r, c), dt) for dt in out_dtypes],
        compiler_params=_params("parallel"),
    )(*ins)


def _row_tile(rows):
    return next((cand for cand in (128, 176, 64, 32, 16) if rows % cand == 0), rows)


def _sum_slots(name, received, own, chip):
    _, r, cols = own.shape
    tr = _row_tile(r)

    def body(chip_ref, own_ref, a_ref, b_ref, c_ref, o_ref):
        o_ref[...] = ((own_ref[0].astype(F32) + a_ref[0].astype(F32)) + b_ref[0].astype(F32)) + c_ref[0].astype(F32)

    def slot(flip):
        return pl.BlockSpec((1, tr, cols), lambda i, chip_ref: (jnp.bitwise_xor(chip_ref[0], flip), i, 0))

    grid_spec = pltpu.PrefetchScalarGridSpec(
        num_scalar_prefetch=1, grid=(r // tr,), in_specs=[slot(0), slot(1), slot(2), slot(3)],
        out_specs=pl.BlockSpec((tr, cols), lambda i, chip_ref: (i, 0)))
    return pl.pallas_call(
        body, name=name, grid_spec=grid_spec, out_shape=jax.ShapeDtypeStruct((r, cols), F32),
        compiler_params=_params("parallel"),
    )(chip.reshape(1).astype(jnp.int32), own, received, received, received)


def _adamw_halves(name, mine, theirs, core, w, m, v):
    h, cols = mine.shape
    tr = _row_tile(h)
    nh = h // tr

    def body(core_ref, mine_ref, theirs_ref, w_ref, m_ref, v_ref, g_ref, d_ref, m2_ref, v2_ref):
        is_mine = (pl.program_id(0) // nh) == core_ref[0]
        g = jnp.where(is_mine, mine_ref[...], theirs_ref[...])
        outs = _adamw_math(g, w_ref[...], m_ref[...], v_ref[...])
        for ref, val in zip((g_ref, d_ref, m2_ref, v2_ref), outs):
            ref[...] = val

    half = pl.BlockSpec((tr, cols), lambda i, core_ref: (i % nh, 0))
    full = pl.BlockSpec((tr, cols), lambda i, core_ref: (i, 0))
    grid_spec = pltpu.PrefetchScalarGridSpec(
        num_scalar_prefetch=1, grid=(2 * nh,), in_specs=[half, half, full, full, full], out_specs=[full] * 4)
    return pl.pallas_call(
        body, name=name, grid_spec=grid_spec, out_shape=[jax.ShapeDtypeStruct((2 * h, cols), F32)] * 4,
        compiler_params=_params("parallel"),
    )(core.reshape(1).astype(jnp.int32), mine, theirs, w, m, v)


def _adamw_math(g, w_v, m_v, v_v):
    m2 = ADAM_B1 * m_v + (1.0 - ADAM_B1) * g
    v2 = ADAM_B2 * v_v + (1.0 - ADAM_B2) * jnp.square(g)
    m_hat = m2 / (1.0 - ADAM_B1 ** ADAM_STEP)
    v_hat = v2 / (1.0 - ADAM_B2 ** ADAM_STEP)
    delta = -ADAM_LR * (m_hat / (jnp.sqrt(v_hat) + ADAM_EPS) + ADAM_WD * w_v)
    return [g, delta, m2, v2]


def _adamw(name, g, w, m, v):
    return _elementwise(name, _adamw_math, [g, w, m, v], [F32] * 4)


_TRANSPOSED = ("ffn1_gate", "ffn1_up", "ffn2_gate", "ffn2_up")
_MATRICES = (("ffn1_gate", 0), ("ffn1_up", 0), ("ffn1_down", 0), ("w_in", 1), ("w_out", 0),
             ("ffn2_gate", 0), ("ffn2_up", 0), ("ffn2_down", 0))


def _block2d(a, name):
    return jnp.swapaxes(a, 1, 2)[0] if name in _TRANSPOSED else a[0]


def _block3d(a, name):
    return jnp.swapaxes(a[None], 1, 2) if name in _TRANSPOSED else a[None]
_VECTORS = ("ln1_g", "ln1_b", "conv_b", "dt_bias", "a_log", "d_skip", "attn_norm_w", "ssd_norm_w",
            "ln2_g", "ln2_b", "ln3_g", "ln3_b")
_WEIGHT_ORDER = ("ln1_g", "ln1_b", "ffn1_gate", "ffn1_up", "ffn1_down", "w_in", "conv_w", "conv_b", "dt_bias", "a_log",
                 "d_skip", "attn_norm_w", "ssd_norm_w", "w_out", "ln2_g", "ln2_b", "ffn2_gate", "ffn2_up", "ffn2_down",
                 "ln3_g", "ln3_b")


def _pack_rows(vectors):
    parts = []
    for vec in vectors:
        flat = vec.reshape(-1)
        parts.append(jnp.pad(flat, (0, (-flat.shape[0]) % LANES)))
    flat = jnp.concatenate(parts)
    flat = jnp.pad(flat, (0, (-flat.shape[0]) % (8 * LANES)))
    return flat.reshape(-1, LANES)


def _unpack_rows(packed, shapes):
    flat = packed.reshape(-1)
    out, off = [], 0
    for shape in shapes:
        size = int(np.prod(shape))
        out.append(flat[off:off + size].reshape(shape))
        off += size + (-size) % LANES
    return out


def _assemble(stack, axis):
    if axis == 0:
        return stack.reshape(-1, stack.shape[2])
    return jnp.concatenate([stack[s] for s in range(N_CHIPS)], axis=1)


def _split(full, axis):
    if axis == 0:
        return full.reshape(N_CHIPS, -1, full.shape[1])
    cols = full.shape[1] // N_CHIPS
    return jnp.stack([full[:, cols * s:cols * (s + 1)] for s in range(N_CHIPS)])


def kernel(x, positions, ln1_g, ln1_b, ffn1_gate, ffn1_up, ffn1_down, w_in, conv_w, conv_b, dt_bias, a_log, d_skip, attn_norm_w, ssd_norm_w, w_out, ln2_g, ln2_b, ffn2_gate, ffn2_up, ffn2_down, ln3_g, ln3_b, loss_target, m_ln1_g, m_ln1_b, m_ffn1_gate, m_ffn1_up, m_ffn1_down, m_w_in, m_conv_w, m_conv_b, m_dt_bias, m_a_log, m_d_skip, m_attn_norm_w, m_ssd_norm_w, m_w_out, m_ln2_g, m_ln2_b, m_ffn2_gate, m_ffn2_up, m_ffn2_down, m_ln3_g, m_ln3_b, v_ln1_g, v_ln1_b, v_ffn1_gate, v_ffn1_up, v_ffn1_down, v_w_in, v_conv_w, v_conv_b, v_dt_bias, v_a_log, v_d_skip, v_attn_norm_w, v_ssd_norm_w, v_w_out, v_ln2_g, v_ln2_b, v_ffn2_gate, v_ffn2_up, v_ffn2_down, v_ln3_g, v_ln3_b):
    given = dict(locals())
    wts = {n: given[n] for n in _WEIGHT_ORDER}
    mom_m = {n: given["m_" + n] for n in _WEIGHT_ORDER}
    mom_v = {n: given["v_" + n] for n in _WEIGHT_ORDER}
    chip = 2 * lax.axis_index("x") + lax.axis_index("y")

    core = lax.axis_index("c")
    groups = [[(n, axis) for n, axis in _MATRICES if n.startswith(prefix)] for prefix in ("ffn1", "w_", "ffn2")]
    own16 = {n: _block2d(wts[n], n).astype(BF16) for n, _ in _MATRICES}

    def full_weights(group, results):
        out = {}
        for (n, axis), st in zip(group, results):
            whole = _assemble(st, axis)
            out[n] = whole.T if n in _TRANSPOSED else whole
        return out

    full = full_weights(groups[0], _run_job(_GatherJob([own16[n] for n, _ in groups[0]]), "gather_ffn1"))
    for n in _VECTORS:
        full[n] = wts[n]
    conv_rows = jnp.pad(wts["conv_w"][0], ((0, 32 - CONV_WIDTH), (0, 0)))

    def mixer_weights(results):
        out = full_weights(groups[1], results)
        out["conv_w"] = _assemble(results[-1], 1)[:CONV_WIDTH]
        return out

    def ffn2_weights(results):
        return full_weights(groups[2], results)

    late = [(_GatherJob([own16[n] for n, _ in groups[1]] + [conv_rows]), mixer_weights),
            (_GatherJob([own16[n] for n, _ in groups[2]]), ffn2_weights)]

    chip_sums = {}

    def core_sums(g, which, tag):
        partials = [_split(g[n], axis) for n, axis in which]
        from_sibling = _sibling_halves([p.astype(BF16) for p in partials], "sibling_halves_" + tag)
        for (n, _), p, o in zip(which, partials, from_sibling):
            chip_sums[n] = _half_sum("core_sum_" + n, p, o, core)
        return _ExchangeJob([chip_sums[n] for n, _ in which])

    last = [(n, axis) for n, axis in _MATRICES if n in ("ffn1_gate", "ffn1_up")]
    early = [(n, axis) for n, axis in _MATRICES if (n, axis) not in last]
    loss, grad_x, g, received_early = _local_step(x, positions, loss_target, full, late,
                                                  lambda g_now: core_sums(g_now, early, "early"))
    received_last = _run_job(core_sums(g, last, "last"), "exchange_last")
    received = dict(zip([n for n, _ in last + early], received_last + received_early))
    half_totals = [_sum_slots("sum_partials_" + n, received[n], chip_sums[n], chip) for n, _ in _MATRICES]
    other_halves = _sibling_swap(half_totals)

    small_shapes = [g[n].shape for n in _VECTORS] + [g["conv_w"].shape, (1,)]
    total = _small_allreduce(_pack_rows([g[n] for n in _VECTORS] + [g["conv_w"], loss[0, :1]]))
    small = _unpack_rows(total, small_shapes)
    loss_out = small[-1].reshape(())

    grads, deltas, new_m, new_v = {}, {}, {}, {}
    for (n, _), mine, theirs in zip(_MATRICES, half_totals, other_halves):
        res = _adamw_halves("adamw_" + n, mine, theirs, core, _block2d(wts[n], n), _block2d(mom_m[n], n), _block2d(mom_v[n], n))
        grads[n], deltas[n], new_m[n], new_v[n] = [_block3d(r, n) for r in res]

    vec_shapes = [wts[n].shape for n in _VECTORS]
    res = _adamw("adamw_vectors", _pack_rows(small[:len(_VECTORS)]), _pack_rows([wts[n] for n in _VECTORS]),
                 _pack_rows([mom_m[n] for n in _VECTORS]), _pack_rows([mom_v[n] for n in _VECTORS]))
    for dst, packed in zip((grads, deltas, new_m, new_v), res):
        for n, val in zip(_VECTORS, _unpack_rows(packed, vec_shapes)):
            dst[n] = val

    cols = conv_w.shape[2]
    g_conv = lax.dynamic_slice_in_dim(small[len(_VECTORS)], chip * cols, cols, axis=1)
    res = _adamw("adamw_conv_w", g_conv, wts["conv_w"][0], mom_m["conv_w"][0], mom_v["conv_w"][0])
    grads["conv_w"], deltas["conv_w"], new_m["conv_w"], new_v["conv_w"] = [r[None] for r in res]

    return (loss_out, grad_x, *[grads[n] for n in _WEIGHT_ORDER], *[deltas[n] for n in _WEIGHT_ORDER],
            *[new_m[n] for n in _WEIGHT_ORDER], *[new_v[n] for n in _WEIGHT_ORDER])
```

```python
import numpy as np
import jax
import jax.numpy as jnp
from jax import lax
from jax.experimental import pallas as pl
from jax.experimental.pallas import tpu as pltpu

F32, BF16 = jnp.float32, jnp.bfloat16

D_MODEL = 1024
D_FF = 2816
N_HEADS = 12
HEAD_DIM = 64
D_ATTN = 768
D_SSD = 768
N_GROUPS = 4
HEADS_PER_GROUP = 3
D_STATE = 128
D_CONV = 1792
CONV_WIDTH = 4
ROPE_DIM = 16
ROPE_THETA = 500000.0
ALPHA = 2.0 ** 0.25
LN_EPS = 1e-5
RMS_EPS = 1e-6
ADAM_LR, ADAM_B1, ADAM_B2, ADAM_EPS, ADAM_WD, ADAM_STEP = 0.001, 0.9, 0.999, 1e-08, 0.01, 10

LANES = 128
GATE_UP_INTERLEAVE = 256
SEQ_BLOCK = 256
GROUP_LANES = 256
VMEM_LIMIT = 56 * 1024 * 1024
NEG = -1e30
MESH = pl.DeviceIdType.MESH
HIGHEST = lax.Precision.HIGHEST

_NT = (((1,), (1,)), ((), ()))
_TN = (((0,), (0,)), ((), ()))


def _params(*sem):
    return pltpu.CompilerParams(dimension_semantics=sem, vmem_limit_bytes=VMEM_LIMIT)


def _bf(v):
    return v.astype(BF16)


EPILOGUE_ROWS = 128


def _row_chunks(tm):
    return [slice(r, min(r + EPILOGUE_ROWS, tm)) for r in range(0, tm, EPILOGUE_ROWS)]


def _sigmoid(v):
    return 0.5 * jnp.tanh(0.5 * v) + 0.5


def _mm(name, pairs, *, scale=1.0, res=None, res_scale=1.0, out_dtype=F32, tm=512, tn=512):
    m, n = pairs[0][0].shape[0], pairs[0][1].shape[1]
    tm, tn = min(tm, m), min(tn, n)
    assert m % tm == 0 and n % tn == 0, (name, m, n, tm, tn)
    npair = len(pairs)

    def body(*refs):
        acc = None
        for a_ref, b_ref in zip(refs[:npair], refs[npair:2 * npair]):
            d = jnp.dot(_bf(a_ref[...]), b_ref[...], preferred_element_type=F32)
            acc = d if acc is None else acc + d
        if scale != 1.0:
            acc = acc * scale
        if res is not None:
            acc = acc + res_scale * refs[2 * npair][...]
        refs[-1][...] = acc.astype(out_dtype)

    in_specs = [pl.BlockSpec((tm, a.shape[1]), lambda i, j: (i, 0)) for a, _ in pairs]
    in_specs += [pl.BlockSpec((b.shape[0], tn), lambda i, j: (0, j)) for _, b in pairs]
    args = [a for a, _ in pairs] + [b for _, b in pairs]
    if res is not None:
        in_specs.append(pl.BlockSpec((tm, tn), lambda i, j: (i, j)))
        args.append(res)
    return pl.pallas_call(
        body, name=name, grid=(m // tm, n // tn), in_specs=in_specs,
        out_specs=pl.BlockSpec((tm, tn), lambda i, j: (i, j)),
        out_shape=jax.ShapeDtypeStruct((m, n), out_dtype),
        compiler_params=_params("parallel", "parallel"),
    )(*args)


def _mm_tn(name, x, dy, *, scale=1.0, tk=512, tn=512, tt=1024):
    t, k = x.shape
    n = dy.shape[1]
    tk, tn, tt = min(tk, k), min(tn, n), min(tt, t)
    assert k % tk == 0 and n % tn == 0 and t % tt == 0, (name, k, n, t)
    nt = t // tt

    def body(x_ref, dy_ref, o_ref):
        step = pl.program_id(2)
        d = lax.dot_general(_bf(x_ref[...]), _bf(dy_ref[...]), _TN, preferred_element_type=F32)

        @pl.when(step == 0)
        def _():
            o_ref[...] = d

        @pl.when(step > 0)
        def _():
            o_ref[...] += d

        if scale != 1.0:
            @pl.when(step == nt - 1)
            def _():
                o_ref[...] = o_ref[...] * scale

    return pl.pallas_call(
        body, name=name, grid=(k // tk, n // tn, nt),
        in_specs=[pl.BlockSpec((tt, tk), lambda i, j, s: (s, i)), pl.BlockSpec((tt, tn), lambda i, j, s: (s, j))],
        out_specs=pl.BlockSpec((tk, tn), lambda i, j, s: (i, j)),
        out_shape=jax.ShapeDtypeStruct((k, n), F32),
        compiler_params=_params("parallel", "parallel", "arbitrary"),
    )(x, dy)


def _mm_tn_sections(name, x, dys, *, tt=512):
    t, k = x.shape
    tt = min(tt, t)
    cuts = np.cumsum([0] + [d.shape[1] for d in dys]).tolist()
    ns = len(dys)

    def body(*refs):
        x_ref, o_ref = refs[0], refs[1 + ns]
        step = pl.program_id(0)
        xt = x_ref[...].T
        parts = [jnp.dot(xt, refs[1 + a][...], preferred_element_type=F32) for a in range(ns)]

        @pl.when(step == 0)
        def _():
            for a in range(ns):
                o_ref[:, cuts[a]:cuts[a + 1]] = parts[a]

        @pl.when(step > 0)
        def _():
            for a in range(ns):
                o_ref[:, cuts[a]:cuts[a + 1]] += parts[a]

    return pl.pallas_call(
        body, name=name, grid=(t // tt,),
        in_specs=[pl.BlockSpec((tt, k), lambda s: (s, 0))] + [pl.BlockSpec((tt, d.shape[1]), lambda s: (s, 0)) for d in dys],
        out_specs=pl.BlockSpec((k, cuts[-1]), lambda s: (0, 0)),
        out_shape=jax.ShapeDtypeStruct((k, cuts[-1]), F32),
        compiler_params=_params("arbitrary"),
    )(x, *dys)


def _mm_tn_gate_up(name, x, dau, *, tt=1024):
    t, k = x.shape
    gi = GATE_UP_INTERLEAVE
    nj = dau.shape[1] // (2 * gi)
    tt = min(tt, t)
    nt = t // tt

    def body(x_ref, dy_ref, g_ref, u_ref):
        step = pl.program_id(1)
        d = lax.dot_general(dy_ref[...], _bf(x_ref[...]), _TN, preferred_element_type=F32)

        @pl.when(step == 0)
        def _():
            g_ref[...] = d[:gi]
            u_ref[...] = d[gi:]

        @pl.when(step > 0)
        def _():
            g_ref[...] += d[:gi]
            u_ref[...] += d[gi:]

    out = pl.BlockSpec((gi, k), lambda j, s: (j, 0))
    return pl.pallas_call(
        body, name=name, grid=(nj, nt),
        in_specs=[pl.BlockSpec((tt, k), lambda j, s: (s, 0)), pl.BlockSpec((tt, 2 * gi), lambda j, s: (s, j))],
        out_specs=[out, out],
        out_shape=[jax.ShapeDtypeStruct((gi * nj, k), F32)] * 2,
        compiler_params=_params("parallel", "arbitrary"),
    )(x, dau)


def _carried(carry, ins, outs, sems, step, total):
    start, forward, finish = carry.phases(ins, outs, sems)
    pl.when(step == 0)(start)
    if forward is not None:
        pl.when(step == (3 * total) // 4)(forward)
    return lambda: pl.when(step == total - 1)(finish)


def _resident(shape):
    return pl.BlockSpec(shape, lambda i: (0,) * len(shape), pipeline_mode=pl.Buffered(1))


def _ffn_fwd(name, x16, res, wgu, wd, g, b, *, tm=512, carry=None):
    t, k = x16.shape
    gi = GATE_UP_INTERLEAVE
    nj, n, ni = wd.shape[0] // gi, wd.shape[1], t // tm
    nc = carry.n if carry is not None else 0

    def body(*refs):
        x_ref, res_ref, wgu_ref, wd_ref, g_ref, b_ref = refs[:6]
        au_ref, hm_ref, y_ref, r_ref, y16_ref = refs[6 + nc:11 + nc]
        if carry is not None:
            finish = _carried(carry, refs[6:6 + nc], refs[11 + nc:11 + 2 * nc], refs[11 + 2 * nc:], pl.program_id(0), ni)
        xv = x_ref[...]
        acc = jnp.zeros((tm, n), F32)
        for j in range(nj):
            au = jnp.dot(xv, wgu_ref[:, 2 * gi * j:2 * gi * (j + 1)], preferred_element_type=F32)
            a, u = au[:, :gi], au[:, gi:]
            au_ref[:, 2 * gi * j:2 * gi * (j + 1)] = _bf(au)
            hm = _bf(a * _sigmoid(a) * u)
            hm_ref[:, gi * j:gi * (j + 1)] = hm
            acc = acc + jnp.dot(hm, wd_ref[gi * j:gi * (j + 1), :], preferred_element_type=F32)
        r = ALPHA * res_ref[...] + 0.5 * acc
        r_ref[...] = r
        y = _layer_norm(r, g_ref[...], b_ref[...])
        y_ref[...] = y
        y16_ref[...] = _bf(y)
        if carry is not None:
            finish()

    row = lambda c: pl.BlockSpec((tm, c), lambda i: (i, 0))
    hbm = pl.BlockSpec(memory_space=pltpu.HBM)
    res_ = pl.pallas_call(
        body, name=name, grid=(ni,),
        in_specs=[row(k), row(n), _resident(wgu.shape), _resident(wd.shape), _resident(g.shape), _resident(b.shape)] + [hbm] * nc,
        out_specs=[row(2 * gi * nj), row(gi * nj), row(n), row(n), row(n)] + [hbm] * nc,
        out_shape=[jax.ShapeDtypeStruct((t, 2 * gi * nj), BF16), jax.ShapeDtypeStruct((t, gi * nj), BF16),
                   jax.ShapeDtypeStruct((t, n), F32), jax.ShapeDtypeStruct((t, n), F32), jax.ShapeDtypeStruct((t, n), BF16)]
        + (carry.out_shape if carry is not None else []),
        scratch_shapes=carry.scratch_shapes if carry is not None else [],
        compiler_params=_params("arbitrary" if carry is not None else "parallel"),
    )(x16, res, wgu, wd, g, b, *(carry.operands if carry is not None else []))
    return tuple(res_[:5]) + ((carry.results(res_[5:]),) if carry is not None else ())


def _ffn_bwd(name, dr16, dr, wdt, au, wgut, *, tm=512, carry=None):
    t, n = dr16.shape
    gi = GATE_UP_INTERLEAVE
    nj, ni = wdt.shape[1] // gi, t // tm
    nc = carry.n if carry is not None else 0

    def body(*refs):
        dr16_ref, dr_ref, wdt_ref, au_ref, wgut_ref = refs[:5]
        dau_ref, dx_ref = refs[5 + nc:7 + nc]
        if carry is not None:
            finish = _carried(carry, refs[5:5 + nc], refs[7 + nc:7 + 2 * nc], refs[7 + 2 * nc:], pl.program_id(0), ni)
        drv = dr16_ref[...]
        acc = jnp.zeros((tm, n), F32)
        for j in range(nj):
            dhm = jnp.dot(drv, wdt_ref[:, gi * j:gi * (j + 1)], preferred_element_type=F32) * 0.5
            au_v = au_ref[:, 2 * gi * j:2 * gi * (j + 1)].astype(F32)
            a, u = au_v[:, :gi], au_v[:, gi:]
            sig = _sigmoid(a)
            silu = a * sig
            dau = jnp.concatenate([_bf(dhm * u * (sig + silu - silu * sig)), _bf(dhm * silu)], axis=1)
            dau_ref[:, 2 * gi * j:2 * gi * (j + 1)] = dau
            acc = acc + jnp.dot(dau, wgut_ref[2 * gi * j:2 * gi * (j + 1), :], preferred_element_type=F32)
        dx_ref[...] = ALPHA * dr_ref[...] + acc
        if carry is not None:
            finish()

    row = lambda c: pl.BlockSpec((tm, c), lambda i: (i, 0))
    hbm = pl.BlockSpec(memory_space=pltpu.HBM)
    res_ = pl.pallas_call(
        body, name=name, grid=(ni,),
        in_specs=[row(n), row(n), _resident(wdt.shape), row(2 * gi * nj), _resident(wgut.shape)] + [hbm] * nc,
        out_specs=[row(2 * gi * nj), row(n)] + [hbm] * nc,
        out_shape=[jax.ShapeDtypeStruct((t, 2 * gi * nj), BF16), jax.ShapeDtypeStruct((t, n), F32)]
        + (carry.out_shape if carry is not None else []),
        scratch_shapes=carry.scratch_shapes if carry is not None else [],
        compiler_params=_params("arbitrary" if carry is not None else "parallel"),
    )(dr16, dr, wdt, au, wgut, *(carry.operands if carry is not None else []))
    return tuple(res_[:2]) + ((carry.results(res_[2:]),) if carry is not None else ())


def _layer_norm(r, g, b):
    mu = jnp.mean(r, axis=-1, keepdims=True)
    var = jnp.mean(jnp.square(r - mu), axis=-1, keepdims=True)
    return (r - mu) * lax.rsqrt(var + LN_EPS) * g + b


def _mm_res_ln(name, a, w, res, g, b, *, scale, tm=256):
    t, k = a.shape
    n = w.shape[1]

    def body(a_ref, w_ref, res_ref, g_ref, b_ref, y_ref, r_ref, y16_ref):
        for rows in _row_chunks(tm):
            r = ALPHA * res_ref[rows, :] + scale * jnp.dot(_bf(a_ref[rows, :]), w_ref[...], preferred_element_type=F32)
            r_ref[rows, :] = r
            y = _layer_norm(r, g_ref[...], b_ref[...])
            y_ref[rows, :] = y
            y16_ref[rows, :] = _bf(y)

    row = lambda c: pl.BlockSpec((tm, c), lambda i: (i, 0))
    const = lambda shape: pl.BlockSpec(shape, lambda i: (0, 0))
    return pl.pallas_call(
        body, name=name, grid=(t // tm,),
        in_specs=[row(k), const((k, n)), row(n), const((1, n)), const((1, n))],
        out_specs=[row(n), row(n), row(n)],
        out_shape=[jax.ShapeDtypeStruct((t, n), F32), jax.ShapeDtypeStruct((t, n), F32), jax.ShapeDtypeStruct((t, n), BF16)],
        compiler_params=_params("parallel"),
    )(a, w, res, g, b)


def _rowwise(name, fn, rows, consts, row_outs, acc_outs=(), tm=512):
    rows = [r if isinstance(r, tuple) else (r, r.shape[1]) for r in rows]
    t = rows[0][0].shape[0]
    tm = min(tm, t)
    assert t % tm == 0
    nr, nc, no, na = len(rows), len(consts), len(row_outs), len(acc_outs)

    def body(*refs):
        vals = [r[...] for r in refs[:nr + nc]]
        outs, accs = fn(*vals)
        for o_ref, o in zip(refs[nr + nc:nr + nc + no], outs):
            o_ref[...] = o.astype(o_ref.dtype)
        if na:
            step = pl.program_id(0)
            acc_refs = refs[nr + nc + no:]

            @pl.when(step == 0)
            def _():
                for a_ref, a in zip(acc_refs, accs):
                    a_ref[...] = a

            @pl.when(step > 0)
            def _():
                for a_ref, a in zip(acc_refs, accs):
                    a_ref[...] += a

    in_specs = [pl.BlockSpec((tm, w), lambda i: (i, 0)) for _, w in rows]
    in_specs += [pl.BlockSpec(c.shape, lambda i, nd=c.ndim: (0,) * nd) for c in consts]
    out_specs = [pl.BlockSpec((tm, c), lambda i: (i, 0)) for c, _ in row_outs]
    out_specs += [pl.BlockSpec(s, lambda i: (0, 0)) for s in acc_outs]
    out_shape = [jax.ShapeDtypeStruct((t, c), dt) for c, dt in row_outs]
    out_shape += [jax.ShapeDtypeStruct(s, F32) for s in acc_outs]
    res = pl.pallas_call(
        body, name=name, grid=(t // tm,), in_specs=in_specs, out_specs=out_specs, out_shape=out_shape,
        compiler_params=_params("arbitrary" if na else "parallel"),
    )(*[r for r, _ in rows], *consts)
    return res


def _ln_bwd(name, r, g, b, dy):
    def fn(r_v, dy_v, g_v, b_v):
        _, vjp = jax.vjp(_layer_norm, r_v, g_v, b_v)
        dr, dg, db = vjp(dy_v)
        return [dr, dr], [dg, db]
    return _rowwise(name, fn, [r, dy], [g, b], [(r.shape[1], F32), (r.shape[1], BF16)], [(1, r.shape[1])] * 2)


def _ln_loss_bwd(name, r, g, b, target):
    def fn(r_v, t_v, g_v, b_v):
        def loss_fn(rr, gg, bb):
            err = jnp.square(_layer_norm(rr, gg, bb) - t_v)
            return 0.5 * jnp.sum(jnp.mean(err, axis=-1, keepdims=True), axis=0, keepdims=True)
        loss, vjp = jax.vjp(loss_fn, r_v, g_v, b_v)
        dr, dg, db = vjp(jnp.ones((1, 1), F32))
        return [dr, dr], [dg, db, jnp.broadcast_to(loss, (1, LANES))]
    return _rowwise(name, fn, [r, target], [g, b], [(r.shape[1], F32), (r.shape[1], BF16)],
                    [(1, r.shape[1])] * 2 + [(1, LANES)])


def _rope_tables(posf, invf, sgn):
    ang = posf * invf
    return jnp.cos(ang), jnp.sin(ang) * sgn


def _rope_apply(tv, cos, sin):
    lane = lax.broadcasted_iota(jnp.int32, cos.shape, 1)
    first = (lane % HEAD_DIM) < (ROPE_DIM // 2)
    outs = []
    for gidx in range(tv.shape[1] // LANES):
        tg = tv[:, LANES * gidx:LANES * (gidx + 1)]
        sw = jnp.where(first, pltpu.roll(tg, LANES - ROPE_DIM // 2, 1), pltpu.roll(tg, ROPE_DIM // 2, 1))
        outs.append(tg * cos + sw * sin)
    return jnp.concatenate(outs, axis=1)


def _proj_in(h16, w_in, posf, invf, sgn, *, tm=512, carry=None):
    t, k = h16.shape
    cuts = [0, D_ATTN, 2 * D_ATTN, 3 * D_ATTN, 3 * D_ATTN + D_SSD, 3 * D_ATTN + D_SSD + D_CONV, w_in.shape[1]]
    nc = carry.n if carry is not None else 0

    def body(*refs):
        h_ref, w_ref, pos_ref, invf_ref, sgn_ref = refs[:5]
        q_ref, k_ref, v_ref, z_ref, xbc_ref, dt_ref, cs_ref = refs[5 + nc:12 + nc]
        if carry is not None:
            finish = _carried(carry, refs[5:5 + nc], refs[12 + nc:12 + 2 * nc], refs[12 + 2 * nc:], pl.program_id(0), t // tm)
        hv = h_ref[...]
        part = lambda a: jnp.dot(hv, w_ref[:, cuts[a]:cuts[a + 1]], preferred_element_type=F32)
        cos, sin = _rope_tables(pos_ref[...], invf_ref[...], sgn_ref[...])
        cs_ref[...] = jnp.concatenate([cos, sin], axis=1)
        q_ref[...] = _bf(_rope_apply(part(0), cos, sin) * (HEAD_DIM ** -0.5))
        k_ref[...] = _bf(_rope_apply(part(1), cos, sin))
        v_ref[...] = _bf(part(2))
        z_ref[...] = part(3)
        xbc_ref[...] = part(4)
        dt_ref[...] = part(5)
        if carry is not None:
            finish()

    row = lambda c: pl.BlockSpec((tm, c), lambda i: (i, 0))
    hbm = pl.BlockSpec(memory_space=pltpu.HBM)
    widths = [D_ATTN, D_ATTN, D_ATTN, D_SSD, D_CONV, LANES, 2 * LANES]
    dtypes = [BF16, BF16, BF16, F32, F32, F32, F32]
    res = pl.pallas_call(
        body, name="proj_in", grid=(t // tm,),
        in_specs=[row(k), _resident(w_in.shape), row(1), _resident(invf.shape), _resident(sgn.shape)] + [hbm] * nc,
        out_specs=[row(c) for c in widths] + [hbm] * nc,
        out_shape=[jax.ShapeDtypeStruct((t, c), dt) for c, dt in zip(widths, dtypes)]
        + (carry.out_shape if carry is not None else []),
        scratch_shapes=carry.scratch_shapes if carry is not None else [],
        compiler_params=_params("arbitrary" if carry is not None else "parallel"),
    )(h16, w_in, posf, invf, sgn, *(carry.operands if carry is not None else []))
    return tuple(res[:7]) + ((carry.results(res[7:]),) if carry is not None else ())


def _rope_bwd(dq, dk, cs):
    def fn(dq_v, dk_v, cs_v):
        cos, sin = cs_v[:, :LANES], -cs_v[:, LANES:]
        gq = _rope_apply(dq_v * (HEAD_DIM ** -0.5), cos, sin)
        gk = _rope_apply(dk_v, cos, sin)
        return [jnp.concatenate([gq, gk], axis=1)], []
    return _rowwise("rope_bwd", fn, [dq, dk, cs], [], [(2 * D_ATTN, BF16)])[0]


def _rms(v, w):
    return v * lax.rsqrt(jnp.mean(v * v, axis=-1, keepdims=True) + RMS_EPS) * w


def _ungroup(yg):
    w = HEADS_PER_GROUP * HEAD_DIM
    return jnp.concatenate([yg[:, GROUP_LANES * g:GROUP_LANES * g + w] for g in range(N_GROUPS)], axis=1)


def _group(xs):
    w = HEADS_PER_GROUP * HEAD_DIM
    parts = []
    for g in range(N_GROUPS):
        parts += [xs[:, w * g:w * (g + 1)], jnp.zeros((xs.shape[0], GROUP_LANES - w), xs.dtype)]
    return jnp.concatenate(parts, axis=1)


def _norms_fn(attn, yg, xs, z, w_attn, w_ssd, dskip):
    a_n = _rms(attn, w_attn)
    y = _ungroup(yg) + dskip * xs
    y_n = _rms(y * (z * _sigmoid(z)), w_ssd)
    return jnp.concatenate([a_n, y_n], axis=1)


def _norms_fwd(attn, yg, xbc, z, w_attn, w_ssd, dskip):
    def fn(*v):
        return [_norms_fn(*v)], []
    return _rowwise("norms_fwd", fn, [attn, yg, (xbc, D_SSD), z], [w_attn, w_ssd, dskip], [(D_ATTN + D_SSD, BF16)])[0]


def _norms_bwd(attn, yg, xbc, z, w_attn, w_ssd, dskip, dcat):
    def fn(attn_v, yg_v, xs_v, z_v, dcat_v, wa_v, ws_v, dk_v):
        _, vjp = jax.vjp(_norms_fn, attn_v, yg_v, xs_v, z_v, wa_v, ws_v, dk_v)
        d_attn, d_yg, d_xs, d_z, d_wa, d_ws, d_dk = vjp(dcat_v)
        return [d_attn, d_yg, d_xs, d_z], [d_wa, d_ws, d_dk]
    return _rowwise("norms_bwd", fn, [attn, yg, (xbc, D_SSD), z, dcat], [w_attn, w_ssd, dskip],
                    [(D_ATTN, F32), (N_GROUPS * GROUP_LANES, F32), (D_SSD, F32), (D_SSD, BF16)], [(1, D_SSD)] * 3)


def _spread_sum(v, e):
    h1 = _bf(v)
    r1 = v - h1.astype(F32)
    h2 = _bf(r1)
    h3 = _bf(r1 - h2.astype(F32))
    return sum(jnp.dot(h, e, preferred_element_type=F32) for h in (h1, h2, h3))


@jax.custom_vjp
def _spread(v, e, e_t):
    return _spread_sum(v, e)


def _spread_fwd(v, e, e_t):
    return _spread_sum(v, e), (e, e_t)


def _spread_bwd(saved, g):
    e, e_t = saved
    return _spread_sum(g, e_t), jnp.zeros_like(e), jnp.zeros_like(e_t)


_spread.defvjp(_spread_fwd, _spread_bwd)


def _ssd_prep_fn(xs, dtp, dtb, alog, e_x, e_xt, e_a, e_at):
    dt = jax.nn.softplus(dtp + dtb)
    a = -jnp.exp(alog)
    xdtg = _group(xs) * _spread(dt, e_x, e_xt)
    dag = _spread(dt * a, e_a, e_at)
    return xdtg, dag


def _ssd_prep_fwd(xbc, dtp, dtb, alog, spreaders):
    def fn(xbc_v, dtp_v, dtb_v, alog_v, *e_v):
        xdtg, dag = _ssd_prep_fn(xbc_v[:, :D_SSD], dtp_v, dtb_v, alog_v, *e_v)
        return [xdtg, xbc_v[:, D_SSD:], dag], []
    return _rowwise("ssd_prep_fwd", fn, [xbc, dtp], [dtb, alog, *spreaders],
                    [(N_GROUPS * GROUP_LANES, BF16), (D_CONV - D_SSD, BF16), (N_GROUPS * LANES, F32)])


def _ssd_prep_bwd(xbc, dtp, dtb, alog, spreaders, dxdtg, ddag, dxs_a, db, dc):
    def fn(xs_v, dtp_v, dxdtg_v, ddag_v, dxs_a_v, db_v, dc_v, dtb_v, alog_v, *e_v):
        _, vjp = jax.vjp(lambda a, b, c, d: _ssd_prep_fn(a, b, c, d, *e_v), xs_v, dtp_v, dtb_v, alog_v)
        dxs, ddtp, ddtb, dalog = vjp((dxdtg_v, ddag_v))
        return [jnp.concatenate([dxs + dxs_a_v, db_v, dc_v], axis=1), ddtp], [ddtb, dalog]
    return _rowwise("ssd_prep_bwd", fn, [(xbc, D_SSD), dtp, dxdtg, ddag, dxs_a, db, dc], [dtb, alog, *spreaders],
                    [(D_CONV, F32), (LANES, BF16)], [(1, LANES)] * 2)


def _shift_down(u, d):
    if d == 0:
        return u
    row = lax.broadcasted_iota(jnp.int32, u.shape, 0)
    return jnp.where(row >= d, pltpu.roll(u, d, 0), 0.0)


def _shift_up(u, d):
    if d == 0:
        return u
    s = u.shape[0]
    row = lax.broadcasted_iota(jnp.int32, u.shape, 0)
    return jnp.where(row < s - d, pltpu.roll(u, s - d, 0), 0.0)


def _conv_pre(u, w, b):
    acc = b
    for k in range(CONV_WIDTH):
        acc = acc + w[k:k + 1, :] * _shift_down(u, CONV_WIDTH - 1 - k)
    return acc


def _conv_fwd(u, w, b, *, tc=256):
    nb, s, c = u.shape

    def body(u_ref, w_ref, b_ref, o_ref):
        pre = _conv_pre(u_ref[0], w_ref[...], b_ref[...])
        o_ref[0] = pre * _sigmoid(pre)

    return pl.pallas_call(
        body, name="conv_fwd", grid=(c // tc, nb),
        in_specs=[pl.BlockSpec((1, s, tc), lambda j, i: (i, 0, j)), pl.BlockSpec((CONV_WIDTH, tc), lambda j, i: (0, j)),
                  pl.BlockSpec((1, tc), lambda j, i: (0, j))],
        out_specs=pl.BlockSpec((1, s, tc), lambda j, i: (i, 0, j)),
        out_shape=jax.ShapeDtypeStruct((nb, s, c), F32),
        compiler_params=_params("parallel", "parallel"),
    )(u, w, b)


def _conv_bwd(u, w, b, dout, *, tc=256):
    nb, s, c = u.shape

    def body(u_ref, w_ref, b_ref, d_ref, du_ref, dw_ref, db_ref):
        uv, wv = u_ref[0], w_ref[...]
        pre = _conv_pre(uv, wv, b_ref[...])
        sig = _sigmoid(pre)
        dpre = d_ref[0] * (sig * (1.0 + pre * (1.0 - sig)))
        du = jnp.zeros_like(uv)
        dws = []
        for k in range(CONV_WIDTH):
            du = du + wv[k:k + 1, :] * _shift_up(dpre, CONV_WIDTH - 1 - k)
            dws.append(jnp.sum(dpre * _shift_down(uv, CONV_WIDTH - 1 - k), axis=0, keepdims=True))
        du_ref[0] = _bf(du)
        dwv = jnp.concatenate(dws + [jnp.zeros((8 - CONV_WIDTH, tc), F32)], axis=0)
        dbv = jnp.sum(dpre, axis=0, keepdims=True)
        first = pl.program_id(1) == 0

        @pl.when(first)
        def _():
            dw_ref[...] = dwv
            db_ref[...] = dbv

        @pl.when(jnp.logical_not(first))
        def _():
            dw_ref[...] += dwv
            db_ref[...] += dbv

    blk = pl.BlockSpec((1, s, tc), lambda j, i: (i, 0, j))
    return pl.pallas_call(
        body, name="conv_bwd", grid=(c // tc, nb),
        in_specs=[blk, pl.BlockSpec((CONV_WIDTH, tc), lambda j, i: (0, j)), pl.BlockSpec((1, tc), lambda j, i: (0, j)), blk],
        out_specs=[blk, pl.BlockSpec((8, tc), lambda j, i: (0, j)), pl.BlockSpec((1, tc), lambda j, i: (0, j))],
        out_shape=[jax.ShapeDtypeStruct((nb, s, c), BF16), jax.ShapeDtypeStruct((8, c), F32), jax.ShapeDtypeStruct((1, c), F32)],
        compiler_params=_params("parallel", "arbitrary"),
    )(u, w, b, dout)


FWD_KEY_BLOCK = 256


def _branch_bias_table(seq, kb):
    ratio = SEQ_BLOCK // kb
    key = np.arange(kb)[None, :, None]
    query = np.arange(SEQ_BLOCK)[None, None, :]
    delta = (np.arange(seq // kb)[:, None, None] - (ratio - 1)) * kb + query - key
    cnt = np.zeros(delta.shape, np.float64)
    for window, dilation in ((128, 1), (512, 4), (2048, 16)):
        cnt += (delta >= 0) & (delta % dilation == 0) & (delta <= window)
    return jnp.asarray(np.where(cnt > 0, np.log(np.maximum(cnt, 1.0)), NEG).astype(np.float32))


HEADS_PER_BLOCK = LANES // HEAD_DIM


def _head_rows(v, h):
    row = lax.broadcasted_iota(jnp.int32, v.shape, 0)
    return jnp.where((row >= HEAD_DIM * h) & (row < HEAD_DIM * (h + 1)), v, jnp.zeros_like(v))


def _attn_fwd(q, k, v, bias):
    nb_, s, _ = q.shape
    ab, kb = SEQ_BLOCK, FWD_KEY_BLOCK
    nblk, nkb, ratio = s // ab, s // kb, ab // kb

    def body(q_ref, k_ref, v_ref, b_ref, o_ref, lse_ref, vt_s):
        i = pl.program_id(2)

        @pl.when(i == 0)
        def _():
            for jb in range(nkb):
                vt_s[jb] = v_ref[0, kb * jb:kb * (jb + 1), :].T

        qt = q_ref[0].T
        qts = [_head_rows(qt, h) for h in range(HEADS_PER_BLOCK)]

        last = ratio * (i + 1) - 1

        def scores(j):
            kj = k_ref[0, pl.ds(pl.multiple_of(j * kb, kb), kb), :]
            return [jnp.dot(kj, qts[h], preferred_element_type=F32) for h in range(HEADS_PER_BLOCK)]

        def step(j, carry):
            ahead = scores(jnp.minimum(j + 1, last))
            lb = b_ref[ratio * i - j + (ratio - 1)]
            out = []
            for h in range(HEADS_PER_BLOCK):
                m, l, acc = carry[3 * h:3 * h + 3]
                st = carry[3 * HEADS_PER_BLOCK + h] + lb
                m_new = jnp.maximum(m, jnp.max(st, axis=0, keepdims=True))
                p = jnp.exp(st - m_new)
                a = jnp.exp(m - m_new)
                l = a * l + jnp.sum(p, axis=0, keepdims=True)
                vt = vt_s[j, HEAD_DIM * h:HEAD_DIM * (h + 1), :]
                acc = a * acc + jnp.dot(vt, _bf(p), preferred_element_type=F32)
                out += [m_new, l, acc]
            return tuple(out) + tuple(ahead)

        init = (jnp.full((1, ab), NEG, F32), jnp.zeros((1, ab), F32), jnp.zeros((HEAD_DIM, ab), F32)) * HEADS_PER_BLOCK
        res = lax.fori_loop(0, ratio * (i + 1), step, init + tuple(scores(0)))
        ot = jnp.concatenate([res[3 * h + 2] / res[3 * h + 1] for h in range(HEADS_PER_BLOCK)], axis=0)
        o_ref[0] = ot.T
        rows = [res[3 * h] + jnp.log(res[3 * h + 1]) for h in range(HEADS_PER_BLOCK)]
        lse_ref[0, 0, 0] = jnp.concatenate(rows + [jnp.zeros((8 - HEADS_PER_BLOCK, ab), F32)], axis=0)

    qblk = pl.BlockSpec((1, ab, LANES), lambda b, hp, i: (b, i, hp))
    full = pl.BlockSpec((1, s, LANES), lambda b, hp, i: (b, 0, hp))
    return pl.pallas_call(
        body, name="attn_fwd", grid=(nb_, D_ATTN // LANES, nblk),
        in_specs=[qblk, full, full, pl.BlockSpec((nkb, kb, ab), lambda b, hp, i: (0, 0, 0))],
        out_specs=[qblk, pl.BlockSpec((1, 1, 1, 8, ab), lambda b, hp, i: (b, hp, i, 0, 0))],
        out_shape=[jax.ShapeDtypeStruct((nb_, s, D_ATTN), F32),
                   jax.ShapeDtypeStruct((nb_, D_ATTN // LANES, nblk, 8, ab), F32)],
        scratch_shapes=[pltpu.VMEM((nkb, LANES, kb), BF16)],
        compiler_params=_params("parallel", "parallel", "arbitrary"),
    )(q, k, v, bias)


def _attn_bwd(q, k, v, o, do, lse, bias):
    nb_, s, _ = q.shape
    ab = SEQ_BLOCK
    nblk = s // ab

    nh = HEADS_PER_BLOCK

    def body(q_ref, k_ref, v_ref, o_ref, do_ref, lse_ref, b_ref, dq_ref, dk_ref, dv_ref,
             qt_s, dot_s, kt_s, dqt_s, do16_s, d_s, dk_acc, dv_acc):
        for jb in range(nblk):
            sl = slice(ab * jb, ab * (jb + 1))
            qt, kt = q_ref[0, sl, :].T, k_ref[0, sl, :].T
            do = do_ref[0, sl, :]
            dot = do.T
            prod = dot * o_ref[0, sl, :].T
            do16_s[sl, :] = _bf(do)
            for h in range(nh):
                qt_s[nh * jb + h] = _head_rows(qt, h)
                kt_s[nh * jb + h] = _head_rows(kt, h)
                dot_s[nh * jb + h] = _head_rows(_bf(dot), h)
            d_s[jb] = jnp.concatenate(
                [jnp.sum(prod[HEAD_DIM * h:HEAD_DIM * (h + 1)], axis=0, keepdims=True) for h in range(nh)]
                + [jnp.zeros((8 - nh, ab), F32)], axis=0)
            dqt_s[jb] = jnp.zeros((LANES, ab), F32)

        def outer(j, carry):
            ks = pl.ds(pl.multiple_of(j * ab, ab), ab)
            kj, vj = k_ref[0, ks, :], v_ref[0, ks, :]
            dk_acc[...] = jnp.zeros_like(dk_acc)
            dv_acc[...] = jnp.zeros_like(dv_acc)

            def inner(i, c2):
                qs = pl.ds(pl.multiple_of(i * ab, ab), ab)
                qi, doi = q_ref[0, qs, :], do16_s[qs, :]
                lb = b_ref[i - j]
                for h in range(nh):
                    st = jnp.dot(kj, qt_s[nh * i + h], preferred_element_type=F32) + lb
                    pt = jnp.exp(st - lse_ref[0, 0, i, h:h + 1, :])
                    dpt = jnp.dot(vj, dot_s[nh * i + h], preferred_element_type=F32)
                    dst16 = _bf(pt * (dpt - d_s[i, h:h + 1, :]))
                    dv_acc[h] += jnp.dot(_bf(pt), doi, preferred_element_type=F32)
                    dk_acc[h] += jnp.dot(dst16, qi, preferred_element_type=F32)
                    dqt_s[i] += jnp.dot(kt_s[nh * j + h], dst16, preferred_element_type=F32)
                return c2

            lax.fori_loop(j, nblk, inner, 0)
            lane = lax.broadcasted_iota(jnp.int32, (ab, LANES), 1)
            dk_ref[0, ks, :] = jnp.where(lane < HEAD_DIM, dk_acc[0], dk_acc[1])
            dv_ref[0, ks, :] = _bf(jnp.where(lane < HEAD_DIM, dv_acc[0], dv_acc[1]))
            return carry

        lax.fori_loop(0, nblk, outer, 0)
        for jb in range(nblk):
            dq_ref[0, ab * jb:ab * (jb + 1), :] = dqt_s[jb].T

    assert nh == 2
    full = pl.BlockSpec((1, s, LANES), lambda b, hp: (b, 0, hp))
    return pl.pallas_call(
        body, name="attn_bwd", grid=(nb_, D_ATTN // LANES),
        in_specs=[full] * 5 + [pl.BlockSpec((1, 1, nblk, 8, ab), lambda b, hp: (b, hp, 0, 0, 0)),
                               pl.BlockSpec((nblk, ab, ab), lambda b, hp: (0, 0, 0))],
        out_specs=[full, full, full],
        out_shape=[jax.ShapeDtypeStruct((nb_, s, D_ATTN), F32), jax.ShapeDtypeStruct((nb_, s, D_ATTN), F32),
                   jax.ShapeDtypeStruct((nb_, s, D_ATTN), BF16)],
        scratch_shapes=[pltpu.VMEM((nh * nblk, LANES, ab), BF16), pltpu.VMEM((nh * nblk, LANES, ab), BF16),
                        pltpu.VMEM((nh * nblk, LANES, ab), BF16), pltpu.VMEM((nblk, LANES, ab), F32),
                        pltpu.VMEM((s, LANES), BF16), pltpu.VMEM((nblk, 8, ab), F32),
                        pltpu.VMEM((nh, ab, LANES), F32), pltpu.VMEM((nh, ab, LANES), F32)],
        compiler_params=_params("parallel", "parallel"),
    )(q, k, v, o, do, lse, bias)


def _cumsum_fwd(dag):
    nb_, s, c = dag.shape
    ab = SEQ_BLOCK

    def body(a_ref, o_ref, ot_ref):
        r = lax.broadcasted_iota(jnp.int32, (ab, ab), 0)
        cc = lax.broadcasted_iota(jnp.int32, (ab, ab), 1)
        tri = (r >= cc).astype(F32)
        carry = jnp.zeros((1, c), F32)
        for i in range(s // ab):
            loc = jnp.dot(tri, a_ref[0, ab * i:ab * (i + 1), :], precision=HIGHEST, preferred_element_type=F32) + carry
            o_ref[0, ab * i:ab * (i + 1), :] = loc
            ot_ref[0, :, ab * i:ab * (i + 1)] = loc.T
            carry = loc[ab - 1:ab, :]

    return pl.pallas_call(
        body, name="ssd_cumsum", grid=(nb_,),
        in_specs=[pl.BlockSpec((1, s, c), lambda b: (b, 0, 0))],
        out_specs=[pl.BlockSpec((1, s, c), lambda b: (b, 0, 0)), pl.BlockSpec((1, c, s), lambda b: (b, 0, 0))],
        out_shape=[jax.ShapeDtypeStruct((nb_, s, c), F32), jax.ShapeDtypeStruct((nb_, c, s), F32)],
        compiler_params=_params("parallel"),
    )(dag)


def _cumsum_bwd(dcol, drow):
    nb_, s, c = dcol.shape
    ab = SEQ_BLOCK

    def body(c_ref, r_ref, o_ref):
        r = lax.broadcasted_iota(jnp.int32, (ab, ab), 0)
        cc = lax.broadcasted_iota(jnp.int32, (ab, ab), 1)
        tri = (r <= cc).astype(F32)
        carry = jnp.zeros((1, c), F32)
        for i in reversed(range(s // ab)):
            rows = r_ref[0, :, ab * i:ab * (i + 1)].T
            parts = []
            for g in range(N_GROUPS):
                parts += [rows[:, 8 * g:8 * (g + 1)], jnp.zeros((ab, LANES - 8), F32)]
            blk = c_ref[0, ab * i:ab * (i + 1), :] + jnp.concatenate(parts, axis=1)
            loc = jnp.dot(tri, blk, precision=HIGHEST, preferred_element_type=F32) + carry
            o_ref[0, ab * i:ab * (i + 1), :] = loc
            carry = loc[0:1, :]

    return pl.pallas_call(
        body, name="ssd_cumsum_bwd", grid=(nb_,),
        in_specs=[pl.BlockSpec((1, s, c), lambda b: (b, 0, 0)), pl.BlockSpec((1, N_GROUPS * 8, s), lambda b: (b, 0, 0))],
        out_specs=pl.BlockSpec((1, s, c), lambda b: (b, 0, 0)),
        out_shape=jax.ShapeDtypeStruct((nb_, s, c), F32),
        compiler_params=_params("parallel"),
    )(dcol, drow)


def _causal_ok(i, j):
    ab = SEQ_BLOCK
    r = lax.broadcasted_iota(jnp.int32, (ab, ab), 0)
    c = lax.broadcasted_iota(jnp.int32, (ab, ab), 1)
    return (r + (i - j) * ab) >= c


def _causal_ok_t(i, j):
    ab = SEQ_BLOCK
    r = lax.broadcasted_iota(jnp.int32, (ab, ab), 0)
    c = lax.broadcasted_iota(jnp.int32, (ab, ab), 1)
    return (c + (i - j) * ab) >= r


def _ssd_chunk(s_in, x, bm_t, cm, cb, acol, arow, a_prev, ok):
    q = x.shape[0]
    decay = jnp.exp(jnp.where(ok, acol - arow, NEG))
    y = jnp.dot(_bf(cb * decay), x, preferred_element_type=F32)
    y = y + jnp.exp(acol - a_prev) * jnp.dot(cm, _bf(s_in), preferred_element_type=F32)
    a_end = acol[q - 1:q, :]
    wx = _bf(jnp.exp(a_end - acol) * x.astype(F32))
    s_out = jnp.exp(a_end - a_prev) * s_in + jnp.dot(bm_t, wx, preferred_element_type=F32)
    return y, s_out


def _ssd_specs(s):
    xblk = pl.BlockSpec((1, s, GROUP_LANES), lambda b, g: (b, 0, g))
    bblk = pl.BlockSpec((1, s, D_STATE), lambda b, g: (b, 0, g))
    cblk = pl.BlockSpec((1, s, D_STATE), lambda b, g: (b, 0, N_GROUPS + g))
    tblk = pl.BlockSpec((1, 8, s), lambda b, g: (b, (LANES // 8) * g, 0))
    return xblk, bblk, cblk, tblk


def _chunk_views(i, j, x_ref, ac_ref, at_ref):
    ab = SEQ_BLOCK
    sl = slice(ab * i, ab * (i + 1))
    hs = slice(HEAD_DIM * j, HEAD_DIM * (j + 1))
    a_prev = jnp.zeros((1, 1), F32) if i == 0 else ac_ref[0, ab * i - 1:ab * i, j:j + 1]
    return sl, hs, ac_ref[0, sl, j:j + 1], at_ref[0, j:j + 1, sl], a_prev


def _ssd_fwd_chunked(xdtg, bc, acum, acum_t):
    nb_, s, _ = xdtg.shape
    ab = SEQ_BLOCK
    hpg = HEADS_PER_GROUP

    def body(x_ref, b_ref, c_ref, ac_ref, at_ref, y_ref):
        ok = _causal_ok(0, 0)
        states = [jnp.zeros((D_STATE, HEAD_DIM), F32) for _ in range(hpg)]
        for i in range(s // ab):
            bm, cm = b_ref[0, ab * i:ab * (i + 1), :], c_ref[0, ab * i:ab * (i + 1), :]
            bm_t = bm.T
            cb = jnp.dot(cm, bm_t, preferred_element_type=F32)
            ys = []
            for j in range(hpg):
                sl, hs, acol, arow, a_prev = _chunk_views(i, j, x_ref, ac_ref, at_ref)
                y, states[j] = _ssd_chunk(states[j], x_ref[0, sl, hs], bm_t, cm, cb, acol, arow, a_prev, ok)
                ys.append(y)
            y_ref[0, sl, :] = jnp.concatenate(ys + [jnp.zeros((ab, GROUP_LANES - hpg * HEAD_DIM), F32)], axis=1)

    xblk, bblk, cblk, tblk = _ssd_specs(s)
    ablk = pl.BlockSpec((1, s, LANES), lambda b, g: (b, 0, g))
    return pl.pallas_call(
        body, name="ssd_fwd", grid=(nb_, N_GROUPS), in_specs=[xblk, bblk, cblk, ablk, tblk], out_specs=xblk,
        out_shape=jax.ShapeDtypeStruct((nb_, s, N_GROUPS * GROUP_LANES), F32),
        compiler_params=_params("parallel", "parallel"),
    )(xdtg, bc, bc, acum, acum_t)


def _ssd_bwd_chunked(xdtg, bc, acum, acum_t, dyg):
    nb_, s, _ = xdtg.shape
    ab = SEQ_BLOCK
    nblk = s // ab
    hpg = HEADS_PER_GROUP

    def body(x_ref, b_ref, c_ref, ac_ref, at_ref, dy_ref, dx_ref, db_ref, dc_ref, dac_ref, dar_ref, s_s):
        ok = _causal_ok(0, 0)
        dx_ref[...] = jnp.zeros_like(dx_ref)
        dac_ref[...] = jnp.zeros_like(dac_ref)
        dar_ref[...] = jnp.zeros_like(dar_ref)
        states = [jnp.zeros((D_STATE, HEAD_DIM), F32) for _ in range(hpg)]
        for i in range(nblk):
            bm_t = b_ref[0, ab * i:ab * (i + 1), :].T
            for j in range(hpg):
                sl, hs, acol, arow, a_prev = _chunk_views(i, j, x_ref, ac_ref, at_ref)
                s_s[hpg * i + j] = states[j]
                if i + 1 < nblk:
                    a_end = acol[ab - 1:ab, :]
                    wx = _bf(jnp.exp(a_end - acol) * x_ref[0, sl, hs].astype(F32))
                    states[j] = jnp.exp(a_end - a_prev) * states[j] + jnp.dot(bm_t, wx, preferred_element_type=F32)
        ok_t = _causal_ok_t(0, 0)
        last_row = lax.broadcasted_iota(jnp.int32, (ab, 1), 0) == ab - 1
        d_state = [jnp.zeros((D_STATE, HEAD_DIM), F32) for _ in range(hpg)]
        pending = [jnp.zeros((1, 1), F32) for _ in range(hpg)]
        total = lambda v: jnp.sum(v, keepdims=True)
        for i in reversed(range(nblk)):
            bm, cm = b_ref[0, ab * i:ab * (i + 1), :], c_ref[0, ab * i:ab * (i + 1), :]
            cm_t = cm.T
            cbt = jnp.dot(bm, cm_t, preferred_element_type=F32)
            dcbt = jnp.zeros((ab, ab), F32)
            d_bm, d_cm = jnp.zeros((ab, D_STATE), F32), jnp.zeros((ab, D_STATE), F32)
            for j in range(hpg):
                sl, hs, acol, arow, a_prev = _chunk_views(i, j, x_ref, ac_ref, at_ref)
                x, dy = x_ref[0, sl, hs], dy_ref[0, sl, hs]
                dy16 = _bf(dy)
                s_in, g_out = s_s[hpg * i + j], d_state[j]
                s16, g16 = _bf(s_in), _bf(g_out)
                decay = jnp.exp(jnp.where(ok_t, arow - acol, NEG))
                gt = cbt * decay
                dgt = lax.dot_general(x, dy16, _NT, preferred_element_type=F32)
                d_x = jnp.dot(_bf(gt), dy16, preferred_element_type=F32)
                dcbt = dcbt + dgt * decay
                mm = dgt * gt
                d_arow = jnp.sum(mm, axis=0, keepdims=True)
                d_acol = -jnp.sum(mm, axis=1, keepdims=True)
                e = jnp.exp(acol - a_prev)
                edy16 = _bf(e * dy)
                d_cm = d_cm + lax.dot_general(edy16, s16, _NT, preferred_element_type=F32)
                d_s = jnp.dot(cm_t, edy16, preferred_element_type=F32)
                de_e = jnp.sum(dy * jnp.dot(cm, s16, preferred_element_type=F32), axis=1, keepdims=True) * e
                a_end = acol[ab - 1:ab, :]
                w = jnp.exp(a_end - acol)
                f = jnp.exp(a_end - a_prev)
                x32 = x.astype(F32)
                bg = jnp.dot(bm, g16, preferred_element_type=F32)
                d_x = d_x + w * bg
                d_bm = d_bm + lax.dot_general(_bf(w * x32), g16, _NT, preferred_element_type=F32)
                dw_w = jnp.sum(bg * x32, axis=1, keepdims=True) * w
                df_f = total(g_out * s_in) * f
                d_end = total(dw_w) + df_f
                d_acol = d_acol + de_e - dw_w + jnp.where(last_row, d_end + pending[j], 0.0)
                pending[j] = -total(de_e) - df_f
                d_state[j] = d_s + f * g_out
                dx_ref[0, sl, hs] = d_x
                dac_ref[0, sl, j:j + 1] = d_acol
                dar_ref[0, j:j + 1, sl] = d_arow
            dcbt16 = _bf(dcbt)
            db_ref[0, ab * i:ab * (i + 1), :] = d_bm + jnp.dot(dcbt16, cm, preferred_element_type=F32)
            dc_ref[0, ab * i:ab * (i + 1), :] = d_cm + lax.dot_general(dcbt16, bm, _TN, preferred_element_type=F32)

    xblk, bblk, cblk, tblk = _ssd_specs(s)
    ablk = pl.BlockSpec((1, s, LANES), lambda b, g: (b, 0, g))
    return pl.pallas_call(
        body, name="ssd_bwd", grid=(nb_, N_GROUPS),
        in_specs=[xblk, bblk, cblk, ablk, tblk, xblk],
        out_specs=[xblk, bblk, bblk, ablk, pl.BlockSpec((1, 8, s), lambda b, g: (b, g, 0))],
        out_shape=[jax.ShapeDtypeStruct((nb_, s, N_GROUPS * GROUP_LANES), F32),
                   jax.ShapeDtypeStruct((nb_, s, N_GROUPS * D_STATE), F32),
                   jax.ShapeDtypeStruct((nb_, s, N_GROUPS * D_STATE), F32),
                   jax.ShapeDtypeStruct((nb_, s, N_GROUPS * LANES), F32),
                   jax.ShapeDtypeStruct((nb_, N_GROUPS * 8, s), F32)],
        scratch_shapes=[pltpu.VMEM((nblk * hpg, D_STATE, HEAD_DIM), F32)],
        compiler_params=_params("parallel", "parallel"),
    )(xdtg, bc, bc, acum, acum_t, dyg)


def _interleave(wg, wu):
    k, f = wg.shape
    gi = GATE_UP_INTERLEAVE
    return jnp.stack([wg.reshape(k, f // gi, gi), wu.reshape(k, f // gi, gi)], axis=2).reshape(k, 2 * f)


def _head_expanders():
    e_x = np.zeros((LANES, N_GROUPS * GROUP_LANES), np.float32)
    e_a = np.zeros((LANES, N_GROUPS * LANES), np.float32)
    for h in range(N_HEADS):
        g, j = divmod(h, HEADS_PER_GROUP)
        e_x[h, GROUP_LANES * g + HEAD_DIM * j:GROUP_LANES * g + HEAD_DIM * (j + 1)] = 1.0
        e_a[h, LANES * g + j] = 1.0
    return [jnp.asarray(m, BF16) for m in (e_x, e_x.T, e_a, e_a.T)]


def _pad_lanes(v, n=LANES):
    return jnp.pad(v, ((0, 0), (0, n - v.shape[1])))


def _local_step(x, positions, target, w, late=None, early_grad_job=None):
    nb, s, d = x.shape
    t = nb * s
    x2 = x.reshape(t, d)
    tgt2 = target.reshape(t, d)
    (job_a, weights_a), (job_b, weights_b) = late if late is not None else ((None, None), (None, None))

    x16 = _bf(x2)
    wgu1 = _interleave(w["ffn1_gate"], w["ffn1_up"])
    ffn1 = _ffn_fwd("ffn1_fwd", x16, x2, wgu1, w["ffn1_down"], w["ln1_g"], w["ln1_b"], carry=job_a)
    au1, hm1, h1, r1, h1_16 = ffn1[:5]
    if job_a is not None:
        w = {**w, **weights_a(ffn1[5])}

    w_in = w["w_in"]
    wqk, wv, wz = w_in[:, :2 * D_ATTN], w_in[:, 2 * D_ATTN:3 * D_ATTN], w_in[:, 3 * D_ATTN:3 * D_ATTN + D_SSD]
    wxbc = w_in[:, 3 * D_ATTN + D_SSD:3 * D_ATTN + D_SSD + D_CONV]
    wdt = _pad_lanes(w_in[:, 3 * D_ATTN + D_SSD + D_CONV:])

    inv_freq = ROPE_THETA ** (-jnp.arange(0, ROPE_DIM, 2, dtype=F32) / ROPE_DIM)
    half = ROPE_DIM // 2
    head_invf = jnp.concatenate([inv_freq, inv_freq, jnp.zeros((HEAD_DIM - ROPE_DIM,), F32)])
    head_sgn = jnp.concatenate([-jnp.ones((half,), F32), jnp.ones((half,), F32), jnp.zeros((HEAD_DIM - ROPE_DIM,), F32)])
    invf = jnp.tile(head_invf, LANES // HEAD_DIM)[None, :]
    sgn = jnp.tile(head_sgn, LANES // HEAD_DIM)[None, :]
    posf = positions.astype(F32).reshape(t, 1)
    bias_fwd, bias_bwd = _branch_bias_table(s, FWD_KEY_BLOCK), _branch_bias_table(s, SEQ_BLOCK)
    spreaders = _head_expanders()
    dtb, alog = _pad_lanes(w["dt_bias"]), _pad_lanes(w["a_log"])
    dskip = jnp.repeat(w["d_skip"], HEAD_DIM, axis=1)

    proj = _proj_in(h1_16, _pad_lanes(w_in, w_in.shape[1] - N_HEADS + LANES), posf, invf, sgn, carry=job_b)
    q16, k16, v16, z, xbc_pre, dtp, cs = proj[:7]
    if job_b is not None:
        w = {**w, **weights_b(proj[7])}
    wgu2 = _interleave(w["ffn2_gate"], w["ffn2_up"])
    to3 =lambda a: a.reshape(nb, s, a.shape[-1])
    attn_o, lse = _attn_fwd(to3(q16), to3(k16), to3(v16), bias_fwd)

    xbc = _conv_fwd(to3(xbc_pre), w["conv_w"], w["conv_b"]).reshape(t, D_CONV)
    xdtg, bc16, dag = _ssd_prep_fwd(xbc, dtp, dtb, alog, spreaders)
    acum, acum_t = _cumsum_fwd(to3(dag))
    yg = _ssd_fwd_chunked(to3(xdtg), to3(bc16), acum, acum_t)

    cat = _norms_fwd(attn_o.reshape(t, D_ATTN), yg.reshape(t, -1), xbc, z, w["attn_norm_w"], w["ssd_norm_w"], dskip)
    h2, r2, h2_16 = _mm_res_ln("w_out_ln2", cat, w["w_out"], h1, w["ln2_g"], w["ln2_b"], scale=1.0)

    au2, hm2, _, r3, _ = _ffn_fwd("ffn2_fwd", h2_16, h2, wgu2, w["ffn2_down"], w["ln3_g"], w["ln3_b"])

    g = {}
    dr3, dr3_16, g["ln3_g"], g["ln3_b"], loss = _ln_loss_bwd("loss_ln3_bwd", r3, w["ln3_g"], w["ln3_b"], tgt2)

    dau2, dh2 = _ffn_bwd("ffn2_bwd", dr3_16, dr3, w["ffn2_down"].T, au2, wgu2.T)
    g["ffn2_down"] = _mm_tn("ffn2_down_dw", hm2, dr3_16, scale=0.5, tk=D_FF // 2, tn=512)
    g["ffn2_gate"], g["ffn2_up"] = _mm_tn_gate_up("ffn2_up_dw", h2_16, dau2)

    dr2, dr2_16, g["ln2_g"], g["ln2_b"] = _ln_bwd("ln2_bwd", r2, w["ln2_g"], w["ln2_b"], dh2)
    dcat = _mm("w_out_dx", [(dr2_16, w["w_out"].T)], tn=768)
    g["w_out"] = _mm_tn("w_out_dw", cat, dr2_16, tk=768, tn=1024)

    d_attn, dyg, dxs_a, dz16, g["attn_norm_w"], g["ssd_norm_w"], ddskip = _norms_bwd(
        attn_o.reshape(t, D_ATTN), yg.reshape(t, -1), xbc, z, w["attn_norm_w"], w["ssd_norm_w"], dskip, dcat)
    g["d_skip"] = ddskip.reshape(N_HEADS, HEAD_DIM).sum(axis=1)[None, :]

    dq, dk, dv16 = _attn_bwd(to3(q16), to3(k16), to3(v16), attn_o, to3(d_attn), lse, bias_bwd)
    dqk16 = _rope_bwd(dq.reshape(t, D_ATTN), dk.reshape(t, D_ATTN), cs)

    dxdtg, dbm, dcm, dacol, darow = _ssd_bwd_chunked(to3(xdtg), to3(bc16), acum, acum_t, to3(dyg))
    ddag = _cumsum_bwd(dacol, darow)
    dxbc, ddtp16, ddtb, dalog = _ssd_prep_bwd(xbc, dtp, dtb, alog, spreaders, dxdtg.reshape(t, -1), ddag.reshape(t, -1),
                                               dxs_a, dbm.reshape(t, -1), dcm.reshape(t, -1))
    g["dt_bias"], g["a_log"] = ddtb[:, :N_HEADS], dalog[:, :N_HEADS]
    dxbc_pre16, dconv_w, g["conv_b"] = _conv_bwd(to3(xbc_pre), w["conv_w"], w["conv_b"], to3(dxbc))
    g["conv_w"] = dconv_w[:CONV_WIDTH]
    dxbc_pre16 = dxbc_pre16.reshape(t, D_CONV)
    dv16 = dv16.reshape(t, D_ATTN)

    dh1 = _mm("w_in_dx", [(dqk16, wqk.T), (dv16, wv.T), (dz16, wz.T), (dxbc_pre16, wxbc.T), (ddtp16, wdt.T)],
              res=dr2, res_scale=ALPHA)
    g["w_in"] = _mm_tn_sections("w_in_dw", h1_16, [dqk16, dv16, dz16, dxbc_pre16, ddtp16])[:, :w_in.shape[1]]

    dr1, dr1_16, g["ln1_g"], g["ln1_b"] = _ln_bwd("ln1_bwd", r1, w["ln1_g"], w["ln1_b"], dh1)
    g["ffn1_down"] = _mm_tn("ffn1_down_dw", hm1, dr1_16, scale=0.5, tk=D_FF // 2, tn=512)
    ffn1b = _ffn_bwd("ffn1_bwd", dr1_16, dr1, w["ffn1_down"].T, au1, wgu1.T,
                     carry=None if early_grad_job is None else early_grad_job(g))
    dau1, dx = ffn1b[:2]
    early = ffn1b[2] if early_grad_job is not None else None
    g["ffn1_gate"], g["ffn1_up"] = _mm_tn_gate_up("ffn1_up_dw", x16, dau1)
    return loss, dx.reshape(nb, s, d), g, early


_HBM = pl.BlockSpec(memory_space=pltpu.HBM)
N_CHIPS = 4
N_DEVICES = 8


def _place():
    return lax.axis_index("x"), lax.axis_index("y"), lax.axis_index("c")


def _other_chips(x, y):
    return [(1 - x, y), (x, 1 - y), (1 - x, 1 - y)]


class _GatherJob:
    def __init__(self, shards):
        assert all((a.shape[0] // 2) % 16 == 0 for a in shards)
        self.n = len(shards)
        self.shapes = [a.shape for a in shards]
        self.operands = [a.reshape(2, a.shape[0] // 2, a.shape[1]) for a in shards]
        self.out_shape = [jax.ShapeDtypeStruct((N_CHIPS,) + a.shape, a.dtype) for a in self.operands]
        pair = pltpu.SemaphoreType.DMA((self.n, N_CHIPS - 1))
        one = pltpu.SemaphoreType.DMA((self.n,))
        self.scratch_shapes = [pair, pair, pair, pair, one, one]

    def results(self, outs):
        return [o.reshape((N_CHIPS,) + s) for o, s in zip(outs, self.shapes)]

    def phases(self, ins, outs, sems):
        n = self.n
        send_sems, recv_sems, fwd_send_sems, fwd_recv_sems, own_send_sems, own_recv_sems = sems
        x, y, c = _place()
        me = 2 * x + y
        peers = _other_chips(x, y)

        def own(t):
            return pltpu.make_async_remote_copy(ins[t], outs[t].at[me], own_send_sems.at[t], own_recv_sems.at[t],
                                                device_id=(x, y, 1 - c), device_id_type=MESH)

        def ici(t, p, src_chip):
            px, py = peers[p]
            return pltpu.make_async_remote_copy(
                ins[t].at[c] if src_chip is None else outs[t].at[src_chip, c],
                outs[t].at[me if src_chip is None else src_chip, c],
                send_sems.at[t, p], recv_sems.at[t, p], device_id=(px, py, c), device_id_type=MESH)

        def d2d(t, p, core):
            px, py = peers[p]
            return pltpu.make_async_remote_copy(
                outs[t].at[2 * px + py, core], outs[t].at[2 * px + py, core],
                fwd_send_sems.at[t, p], fwd_recv_sems.at[t, p], device_id=(x, y, 1 - c), device_id_type=MESH)

        pairs = [(t, p) for t in range(n) for p in range(N_CHIPS - 1)]

        def start():
            for t, p in pairs:
                ici(t, p, None).start()
            for t in range(n):
                own(t).start()

        def forward():
            for t, p in pairs:
                px, py = peers[p]
                ici(t, p, 2 * px + py).wait_recv()
                d2d(t, p, c).start()

        def finish():
            for t, p in pairs:
                d2d(t, p, 1 - c).wait_recv()
            for t in range(n):
                own(t).wait()
            for t, p in pairs:
                ici(t, p, None).wait_send()
                d2d(t, p, c).wait_send()

        return start, forward, finish


class _ExchangeJob:
    def __init__(self, stacks):
        self.n = len(stacks)
        self.operands = list(stacks)
        self.out_shape = [jax.ShapeDtypeStruct(a.shape, a.dtype) for a in stacks]
        pair = pltpu.SemaphoreType.DMA((self.n, N_CHIPS - 1))
        self.scratch_shapes = [pair, pair]

    def results(self, outs):
        return list(outs)

    def phases(self, ins, outs, sems):
        send_sems, recv_sems = sems
        x, y, c = _place()
        me = 2 * x + y
        peers = _other_chips(x, y)
        pairs = [(t, p) for t in range(self.n) for p in range(N_CHIPS - 1)]

        def copy(t, p):
            px, py = peers[p]
            return pltpu.make_async_remote_copy(ins[t].at[2 * px + py], outs[t].at[me], send_sems.at[t, p],
                                                recv_sems.at[t, p], device_id=(px, py, c), device_id_type=MESH)

        def arrival(t, p):
            px, py = peers[p]
            return pltpu.make_async_remote_copy(ins[t].at[me], outs[t].at[2 * px + py], send_sems.at[t, p],
                                                recv_sems.at[t, p], device_id=(px, py, c), device_id_type=MESH)

        def start():
            for t, p in pairs:
                copy(t, p).start()

        def finish():
            for t, p in pairs:
                arrival(t, p).wait_recv()
            for t, p in pairs:
                copy(t, p).wait_send()

        return start, None, finish


def _run_job(job, name):
    n = job.n

    def body(*refs):
        for phase in job.phases(refs[:n], refs[n:2 * n], refs[2 * n:]):
            if phase is not None:
                phase()

    outs = pl.pallas_call(
        body, name=name, in_specs=[_HBM] * n, out_specs=[_HBM] * n,
        out_shape=job.out_shape, scratch_shapes=job.scratch_shapes,
    )(*job.operands)
    return job.results(outs)


def _sibling_halves(stacks, name):
    n = len(stacks)
    halves = [a.shape[1] // 2 for a in stacks]
    split = [a.reshape(a.shape[0], 2, h, a.shape[2]) for a, h in zip(stacks, halves)]

    def body(*refs):
        ins, outs = refs[:n], refs[n:2 * n]
        send_sems, recv_sems = refs[2 * n:]
        x, y, c = _place()
        cps = []
        for t in range(n):
            cp = pltpu.make_async_remote_copy(ins[t].at[:, 1 - c], outs[t], send_sems.at[t], recv_sems.at[t],
                                              device_id=(x, y, 1 - c), device_id_type=MESH)
            cp.start()
            cps.append(cp)
        for cp in cps:
            cp.wait()

    return pl.pallas_call(
        body, name=name,
        in_specs=[_HBM] * n, out_specs=[_HBM] * n,
        out_shape=[jax.ShapeDtypeStruct((a.shape[0], h, a.shape[2]), a.dtype) for a, h in zip(stacks, halves)],
        scratch_shapes=[pltpu.SemaphoreType.DMA((n,)), pltpu.SemaphoreType.DMA((n,))],
    )(*split)


def _sibling_swap(arrs):
    n = len(arrs)

    def body(*refs):
        ins, outs = refs[:n], refs[n:2 * n]
        send_sems, recv_sems = refs[2 * n:]
        x, y, c = _place()
        cps = []
        for t in range(n):
            cp = pltpu.make_async_remote_copy(ins[t], outs[t], send_sems.at[t], recv_sems.at[t],
                                              device_id=(x, y, 1 - c), device_id_type=MESH)
            cp.start()
            cps.append(cp)
        for cp in cps:
            cp.wait()

    return pl.pallas_call(
        body, name="sibling_swap",
        in_specs=[_HBM] * n, out_specs=[_HBM] * n,
        out_shape=[jax.ShapeDtypeStruct(a.shape, a.dtype) for a in arrs],
        scratch_shapes=[pltpu.SemaphoreType.DMA((n,)), pltpu.SemaphoreType.DMA((n,))],
    )(*arrs)


def _half_sum(name, own, other, core):
    k, r, cols = own.shape
    h = r // 2
    tr = next(cand for cand in (128, 176, 64, 32, 16) if h % cand == 0)
    nblk = h // tr

    def body(core_ref, own_ref, other_ref, o_ref):
        o_ref[...] = _bf(own_ref[...] + other_ref[...].astype(F32))

    grid_spec = pltpu.PrefetchScalarGridSpec(
        num_scalar_prefetch=1, grid=(nblk,),
        in_specs=[pl.BlockSpec((k, tr, cols), lambda i, core_ref: (0, i + core_ref[0] * nblk, 0)),
                  pl.BlockSpec((k, tr, cols), lambda i, core_ref: (0, i, 0))],
        out_specs=pl.BlockSpec((k, tr, cols), lambda i, core_ref: (0, i, 0)))
    return pl.pallas_call(
        body, name=name, grid_spec=grid_spec, out_shape=jax.ShapeDtypeStruct((k, h, cols), BF16),
        compiler_params=_params("parallel"),
    )(core.reshape(1).astype(jnp.int32), own, other)


def _small_allreduce(v):
    r = v.shape[0]

    def body(v_ref, tot_ref, slots, send_sems, recv_sems):
        x, y, c = _place()
        me = 4 * x + 2 * y + c
        slots[me] = v_ref[...]
        cps, peers = [], []
        for k in range(1, N_DEVICES):
            px = 1 - x if (k >> 2) & 1 else x
            py = 1 - y if (k >> 1) & 1 else y
            pc = 1 - c if k & 1 else c
            cp = pltpu.make_async_remote_copy(v_ref, slots.at[me], send_sems.at[k - 1], recv_sems.at[k - 1],
                                              device_id=(px, py, pc), device_id_type=MESH)
            cp.start()
            cps.append(cp)
            peers.append((px, py, pc))
        for k, (px, py, pc) in enumerate(peers):
            pltpu.make_async_remote_copy(v_ref, slots.at[4 * px + 2 * py + pc], send_sems.at[k], recv_sems.at[k],
                                         device_id=(px, py, pc), device_id_type=MESH).wait_recv()
        for cp in cps:
            cp.wait_send()
        acc = slots[0]
        for s in range(1, N_DEVICES):
            acc = acc + slots[s]
        tot_ref[...] = acc

    return pl.pallas_call(
        body, name="small_allreduce",
        in_specs=[pl.BlockSpec(memory_space=pltpu.VMEM)], out_specs=pl.BlockSpec(memory_space=pltpu.VMEM),
        out_shape=jax.ShapeDtypeStruct((r, LANES), F32),
        scratch_shapes=[pltpu.VMEM((N_DEVICES, r, LANES), F32), pltpu.SemaphoreType.DMA((N_DEVICES - 1,)),
                        pltpu.SemaphoreType.DMA((N_DEVICES - 1,))],
    )(v)


def _elementwise(name, fn, ins, out_dtypes):
    r, c = ins[0].shape[-2:]
    tr = next((cand for cand in (256, 176, 128, 64, 32, 16) if r % cand == 0), r)
    nin = len(ins)

    def body(*refs):
        outs = fn(*[ref[...] for ref in refs[:nin]])
        for o_ref, o in zip(refs[nin:], outs):
            o_ref[...] = o.astype(o_ref.dtype)

    in_specs = [pl.BlockSpec((tr, c), lambda i: (i, 0)) if a.ndim == 2 else pl.BlockSpec((a.shape[0], tr, c), lambda i: (0, i, 0))
                for a in ins]
    return pl.pallas_call(
        body, name=name, grid=(r // tr,), in_specs=in_specs,
        out_specs=[pl.BlockSpec((tr, c), lambda i: (i, 0)) for _ in out_dtypes],
        out_shape=[jax.ShapeDtypeStruct((r, c), dt) for dt in out_dtypes],
        compiler_params=_params("parallel"),
    )(*ins)


def _row_tile(rows):
    return next((cand for cand in (128, 176, 64, 32, 16) if rows % cand == 0), rows)


def _sum_slots(name, received, own, chip):
    _, r, cols = own.shape
    tr = _row_tile(r)

    def body(chip_ref, own_ref, a_ref, b_ref, c_ref, o_ref):
        o_ref[...] = ((own_ref[0].astype(F32) + a_ref[0].astype(F32)) + b_ref[0].astype(F32)) + c_ref[0].astype(F32)

    def slot(flip):
        return pl.BlockSpec((1, tr, cols), lambda i, chip_ref: (jnp.bitwise_xor(chip_ref[0], flip), i, 0))

    grid_spec = pltpu.PrefetchScalarGridSpec(
        num_scalar_prefetch=1, grid=(r // tr,), in_specs=[slot(0), slot(1), slot(2), slot(3)],
        out_specs=pl.BlockSpec((tr, cols), lambda i, chip_ref: (i, 0)))
    return pl.pallas_call(
        body, name=name, grid_spec=grid_spec, out_shape=jax.ShapeDtypeStruct((r, cols), F32),
        compiler_params=_params("parallel"),
    )(chip.reshape(1).astype(jnp.int32), own, received, received, received)


def _adamw_halves(name, mine, theirs, core, w, m, v):
    h, cols = mine.shape
    tr = _row_tile(h)
    nh = h // tr

    def body(core_ref, mine_ref, theirs_ref, w_ref, m_ref, v_ref, g_ref, d_ref, m2_ref, v2_ref):
        is_mine = (pl.program_id(0) // nh) == core_ref[0]
        g = jnp.where(is_mine, mine_ref[...], theirs_ref[...])
        outs = _adamw_math(g, w_ref[...], m_ref[...], v_ref[...])
        for ref, val in zip((g_ref, d_ref, m2_ref, v2_ref), outs):
            ref[...] = val

    half = pl.BlockSpec((tr, cols), lambda i, core_ref: (i % nh, 0))
    full = pl.BlockSpec((tr, cols), lambda i, core_ref: (i, 0))
    grid_spec = pltpu.PrefetchScalarGridSpec(
        num_scalar_prefetch=1, grid=(2 * nh,), in_specs=[half, half, full, full, full], out_specs=[full] * 4)
    return pl.pallas_call(
        body, name=name, grid_spec=grid_spec, out_shape=[jax.ShapeDtypeStruct((2 * h, cols), F32)] * 4,
        compiler_params=_params("parallel"),
    )(core.reshape(1).astype(jnp.int32), mine, theirs, w, m, v)


def _adamw_math(g, w_v, m_v, v_v):
    m2 = ADAM_B1 * m_v + (1.0 - ADAM_B1) * g
    v2 = ADAM_B2 * v_v + (1.0 - ADAM_B2) * jnp.square(g)
    m_hat = m2 / (1.0 - ADAM_B1 ** ADAM_STEP)
    v_hat = v2 / (1.0 - ADAM_B2 ** ADAM_STEP)
    delta = -ADAM_LR * (m_hat / (jnp.sqrt(v_hat) + ADAM_EPS) + ADAM_WD * w_v)
    return [g, delta, m2, v2]


def _adamw(name, g, w, m, v):
    return _elementwise(name, _adamw_math, [g, w, m, v], [F32] * 4)


_TRANSPOSED = ("ffn1_gate", "ffn1_up", "ffn2_gate", "ffn2_up")
_MATRICES = (("ffn1_gate", 0), ("ffn1_up", 0), ("ffn1_down", 0), ("w_in", 1), ("w_out", 0),
             ("ffn2_gate", 0), ("ffn2_up", 0), ("ffn2_down", 0))


def _block2d(a, name):
    return jnp.swapaxes(a, 1, 2)[0] if name in _TRANSPOSED else a[0]


def _block3d(a, name):
    return jnp.swapaxes(a[None], 1, 2) if name in _TRANSPOSED else a[None]
_VECTORS = ("ln1_g", "ln1_b", "conv_b", "dt_bias", "a_log", "d_skip", "attn_norm_w", "ssd_norm_w",
            "ln2_g", "ln2_b", "ln3_g", "ln3_b")
_WEIGHT_ORDER = ("ln1_g", "ln1_b", "ffn1_gate", "ffn1_up", "ffn1_down", "w_in", "conv_w", "conv_b", "dt_bias", "a_log",
                 "d_skip", "attn_norm_w", "ssd_norm_w", "w_out", "ln2_g", "ln2_b", "ffn2_gate", "ffn2_up", "ffn2_down",
                 "ln3_g", "ln3_b")


def _pack_rows(vectors):
    parts = []
    for vec in vectors:
        flat = vec.reshape(-1)
        parts.append(jnp.pad(flat, (0, (-flat.shape[0]) % LANES)))
    flat = jnp.concatenate(parts)
    flat = jnp.pad(flat, (0, (-flat.shape[0]) % (8 * LANES)))
    return flat.reshape(-1, LANES)


def _unpack_rows(packed, shapes):
    flat = packed.reshape(-1)
    out, off = [], 0
    for shape in shapes:
        size = int(np.prod(shape))
        out.append(flat[off:off + size].reshape(shape))
        off += size + (-size) % LANES
    return out


def _assemble(stack, axis):
    if axis == 0:
        return stack.reshape(-1, stack.shape[2])
    return jnp.concatenate([stack[s] for s in range(N_CHIPS)], axis=1)


def _split(full, axis):
    if axis == 0:
        return full.reshape(N_CHIPS, -1, full.shape[1])
    cols = full.shape[1] // N_CHIPS
    return jnp.stack([full[:, cols * s:cols * (s + 1)] for s in range(N_CHIPS)])


def kernel(x, positions, ln1_g, ln1_b, ffn1_gate, ffn1_up, ffn1_down, w_in, conv_w, conv_b, dt_bias, a_log, d_skip, attn_norm_w, ssd_norm_w, w_out, ln2_g, ln2_b, ffn2_gate, ffn2_up, ffn2_down, ln3_g, ln3_b, loss_target, m_ln1_g, m_ln1_b, m_ffn1_gate, m_ffn1_up, m_ffn1_down, m_w_in, m_conv_w, m_conv_b, m_dt_bias, m_a_log, m_d_skip, m_attn_norm_w, m_ssd_norm_w, m_w_out, m_ln2_g, m_ln2_b, m_ffn2_gate, m_ffn2_up, m_ffn2_down, m_ln3_g, m_ln3_b, v_ln1_g, v_ln1_b, v_ffn1_gate, v_ffn1_up, v_ffn1_down, v_w_in, v_conv_w, v_conv_b, v_dt_bias, v_a_log, v_d_skip, v_attn_norm_w, v_ssd_norm_w, v_w_out, v_ln2_g, v_ln2_b, v_ffn2_gate, v_ffn2_up, v_ffn2_down, v_ln3_g, v_ln3_b):
    given = dict(locals())
    wts = {n: given[n] for n in _WEIGHT_ORDER}
    mom_m = {n: given["m_" + n] for n in _WEIGHT_ORDER}
    mom_v = {n: given["v_" + n] for n in _WEIGHT_ORDER}
    chip = 2 * lax.axis_index("x") + lax.axis_index("y")

    core = lax.axis_index("c")
    groups = [[(n, axis) for n, axis in _MATRICES if n.startswith(prefix)] for prefix in ("ffn1", "w_", "ffn2")]
    own16 = {n: _block2d(wts[n], n).astype(BF16) for n, _ in _MATRICES}

    def full_weights(group, results):
        out = {}
        for (n, axis), st in zip(group, results):
            whole = _assemble(st, axis)
            out[n] = whole.T if n in _TRANSPOSED else whole
        return out

    full = full_weights(groups[0], _run_job(_GatherJob([own16[n] for n, _ in groups[0]]), "gather_ffn1"))
    for n in _VECTORS:
        full[n] = wts[n]
    conv_rows = jnp.pad(wts["conv_w"][0], ((0, 32 - CONV_WIDTH), (0, 0)))

    def mixer_weights(results):
        out = full_weights(groups[1], results)
        out["conv_w"] = _assemble(results[-1], 1)[:CONV_WIDTH]
        return out

    def ffn2_weights(results):
        return full_weights(groups[2], results)

    late = [(_GatherJob([own16[n] for n, _ in groups[1]] + [conv_rows]), mixer_weights),
            (_GatherJob([own16[n] for n, _ in groups[2]]), ffn2_weights)]

    chip_sums = {}

    def core_sums(g, which, tag):
        partials = [_split(g[n], axis) for n, axis in which]
        from_sibling = _sibling_halves([p.astype(BF16) for p in partials], "sibling_halves_" + tag)
        for (n, _), p, o in zip(which, partials, from_sibling):
            chip_sums[n] = _half_sum("core_sum_" + n, p, o, core)
        return _ExchangeJob([chip_sums[n] for n, _ in which])

    last = [(n, axis) for n, axis in _MATRICES if n in ("ffn1_gate", "ffn1_up")]
    early = [(n, axis) for n, axis in _MATRICES if (n, axis) not in last]
    loss, grad_x, g, received_early = _local_step(x, positions, loss_target, full, late,
                                                  lambda g_now: core_sums(g_now, early, "early"))
    received_last = _run_job(core_sums(g, last, "last"), "exchange_last")
    received = dict(zip([n for n, _ in last + early], received_last + received_early))
    half_totals = [_sum_slots("sum_partials_" + n, received[n], chip_sums[n], chip) for n, _ in _MATRICES]
    other_halves = _sibling_swap(half_totals)

    small_shapes = [g[n].shape for n in _VECTORS] + [g["conv_w"].shape, (1,)]
    total = _small_allreduce(_pack_rows([g[n] for n in _VECTORS] + [g["conv_w"], loss[0, :1]]))
    small = _unpack_rows(total, small_shapes)
    loss_out = small[-1].reshape(())

    grads, deltas, new_m, new_v = {}, {}, {}, {}
    for (n, _), mine, theirs in zip(_MATRICES, half_totals, other_halves):
        res = _adamw_halves("adamw_" + n, mine, theirs, core, _block2d(wts[n], n), _block2d(mom_m[n], n), _block2d(mom_v[n], n))
        grads[n], deltas[n], new_m[n], new_v[n] = [_block3d(r, n) for r in res]

    vec_shapes = [wts[n].shape for n in _VECTORS]
    res = _adamw("adamw_vectors", _pack_rows(small[:len(_VECTORS)]), _pack_rows([wts[n] for n in _VECTORS]),
                 _pack_rows([mom_m[n] for n in _VECTORS]), _pack_rows([mom_v[n] for n in _VECTORS]))
    for dst, packed in zip((grads, deltas, new_m, new_v), res):
        for n, val in zip(_VECTORS, _unpack_rows(packed, vec_shapes)):
            dst[n] = val

    cols = conv_w.shape[2]
    g_conv = lax.dynamic_slice_in_dim(small[len(_VECTORS)], chip * cols, cols, axis=1)
    res = _adamw("adamw_conv_w", g_conv, wts["conv_w"][0], mom_m["conv_w"][0], mom_v["conv_w"][0])
    grads["conv_w"], deltas["conv_w"], new_m["conv_w"], new_v["conv_w"] = [r[None] for r in res]

    return (loss_out, grad_x, *[grads[n] for n in _WEIGHT_ORDER], *[deltas[n] for n in _WEIGHT_ORDER],
            *[new_m[n] for n in _WEIGHT_ORDER], *[new_v[n] for n in _WEIGHT_ORDER])
```

```python
import numpy as np
import jax
import jax.numpy as jnp
from jax import lax
from jax.experimental import pallas as pl
from jax.experimental.pallas import tpu as pltpu

F32, BF16 = jnp.float32, jnp.bfloat16

D_MODEL = 1024
D_FF = 2816
N_HEADS = 12
HEAD_DIM = 64
D_ATTN = 768
D_SSD = 768
N_GROUPS = 4
HEADS_PER_GROUP = 3
D_STATE = 128
D_CONV = 1792
CONV_WIDTH = 4
ROPE_DIM = 16
ROPE_THETA = 500000.0
ALPHA = 2.0 ** 0.25
LN_EPS = 1e-5
RMS_EPS = 1e-6
ADAM_LR, ADAM_B1, ADAM_B2, ADAM_EPS, ADAM_WD, ADAM_STEP = 0.001, 0.9, 0.999, 1e-08, 0.01, 10

LANES = 128
GATE_UP_INTERLEAVE = 256
SEQ_BLOCK = 256
GROUP_LANES = 256
VMEM_LIMIT = 56 * 1024 * 1024
NEG = -1e30
MESH = pl.DeviceIdType.MESH
HIGHEST = lax.Precision.HIGHEST

_NT = (((1,), (1,)), ((), ()))
_TN = (((0,), (0,)), ((), ()))


def _params(*sem):
    return pltpu.CompilerParams(dimension_semantics=sem, vmem_limit_bytes=VMEM_LIMIT)


def _bf(v):
    return v.astype(BF16)


EPILOGUE_ROWS = 128


def _row_chunks(tm):
    return [slice(r, min(r + EPILOGUE_ROWS, tm)) for r in range(0, tm, EPILOGUE_ROWS)]


def _sigmoid(v):
    return 0.5 * jnp.tanh(0.5 * v) + 0.5


def _mm(name, pairs, *, scale=1.0, res=None, res_scale=1.0, out_dtype=F32, tm=512, tn=512):
    m, n = pairs[0][0].shape[0], pairs[0][1].shape[1]
    tm, tn = min(tm, m), min(tn, n)
    assert m % tm == 0 and n % tn == 0, (name, m, n, tm, tn)
    npair = len(pairs)

    def body(*refs):
        acc = None
        for a_ref, b_ref in zip(refs[:npair], refs[npair:2 * npair]):
            d = jnp.dot(_bf(a_ref[...]), b_ref[...], preferred_element_type=F32)
            acc = d if acc is None else acc + d
        if scale != 1.0:
            acc = acc * scale
        if res is not None:
            acc = acc + res_scale * refs[2 * npair][...]
        refs[-1][...] = acc.astype(out_dtype)

    in_specs = [pl.BlockSpec((tm, a.shape[1]), lambda i, j: (i, 0)) for a, _ in pairs]
    in_specs += [pl.BlockSpec((b.shape[0], tn), lambda i, j: (0, j)) for _, b in pairs]
    args = [a for a, _ in pairs] + [b for _, b in pairs]
    if res is not None:
        in_specs.append(pl.BlockSpec((tm, tn), lambda i, j: (i, j)))
        args.append(res)
    return pl.pallas_call(
        body, name=name, grid=(m // tm, n // tn), in_specs=in_specs,
        out_specs=pl.BlockSpec((tm, tn), lambda i, j: (i, j)),
        out_shape=jax.ShapeDtypeStruct((m, n), out_dtype),
        compiler_params=_params("parallel", "parallel"),
    )(*args)


def _mm_tn(name, x, dy, *, scale=1.0, tk=512, tn=512, tt=2048):
    t, k = x.shape
    n = dy.shape[1]
    tk, tn, tt = min(tk, k), min(tn, n), min(tt, t)
    assert k % tk == 0 and n % tn == 0 and t % tt == 0, (name, k, n, t)
    nt = t // tt

    def body(x_ref, dy_ref, o_ref):
        step = pl.program_id(2)
        d = lax.dot_general(_bf(x_ref[...]), _bf(dy_ref[...]), _TN, preferred_element_type=F32)

        @pl.when(step == 0)
        def _():
            o_ref[...] = d

        @pl.when(step > 0)
        def _():
            o_ref[...] += d

        if scale != 1.0:
            @pl.when(step == nt - 1)
            def _():
                o_ref[...] = o_ref[...] * scale

    return pl.pallas_call(
        body, name=name, grid=(k // tk, n // tn, nt),
        in_specs=[pl.BlockSpec((tt, tk), lambda i, j, s: (s, i)), pl.BlockSpec((tt, tn), lambda i, j, s: (s, j))],
        out_specs=pl.BlockSpec((tk, tn), lambda i, j, s: (i, j)),
        out_shape=jax.ShapeDtypeStruct((k, n), F32),
        compiler_params=_params("parallel", "parallel", "arbitrary"),
    )(x, dy)


def _mm_tn_sections(name, x, dys, *, tt=512):
    t, k = x.shape
    tt = min(tt, t)
    cuts = np.cumsum([0] + [d.shape[1] for d in dys]).tolist()
    ns = len(dys)

    def body(*refs):
        x_ref, o_ref = refs[0], refs[1 + ns]
        step = pl.program_id(0)
        xt = x_ref[...].T
        parts = [jnp.dot(xt, refs[1 + a][...], preferred_element_type=F32) for a in range(ns)]

        @pl.when(step == 0)
        def _():
            for a in range(ns):
                o_ref[:, cuts[a]:cuts[a + 1]] = parts[a]

        @pl.when(step > 0)
        def _():
            for a in range(ns):
                o_ref[:, cuts[a]:cuts[a + 1]] += parts[a]

    return pl.pallas_call(
        body, name=name, grid=(t // tt,),
        in_specs=[pl.BlockSpec((tt, k), lambda s: (s, 0))] + [pl.BlockSpec((tt, d.shape[1]), lambda s: (s, 0)) for d in dys],
        out_specs=pl.BlockSpec((k, cuts[-1]), lambda s: (0, 0)),
        out_shape=jax.ShapeDtypeStruct((k, cuts[-1]), F32),
        compiler_params=_params("arbitrary"),
    )(x, *dys)


def _mm_tn_gate_up(name, x, dau, *, tt=2048):
    t, k = x.shape
    gi = GATE_UP_INTERLEAVE
    nj = dau.shape[1] // (2 * gi)
    tt = min(tt, t)
    nt = t // tt

    def body(x_ref, dy_ref, g_ref, u_ref):
        step = pl.program_id(1)
        d = lax.dot_general(dy_ref[...], _bf(x_ref[...]), _TN, preferred_element_type=F32)

        @pl.when(step == 0)
        def _():
            g_ref[...] = d[:gi]
            u_ref[...] = d[gi:]

        @pl.when(step > 0)
        def _():
            g_ref[...] += d[:gi]
            u_ref[...] += d[gi:]

    out = pl.BlockSpec((gi, k), lambda j, s: (j, 0))
    return pl.pallas_call(
        body, name=name, grid=(nj, nt),
        in_specs=[pl.BlockSpec((tt, k), lambda j, s: (s, 0)), pl.BlockSpec((tt, 2 * gi), lambda j, s: (s, j))],
        out_specs=[out, out],
        out_shape=[jax.ShapeDtypeStruct((gi * nj, k), F32)] * 2,
        compiler_params=_params("parallel", "arbitrary"),
    )(x, dau)


def _carried(carry, ins, outs, sems, step, total):
    start, forward, finish = carry.phases(ins, outs, sems)
    pl.when(step == 0)(start)
    if forward is not None:
        pl.when(step == (3 * total) // 4)(forward)
    return lambda: pl.when(step == total - 1)(finish)


def _resident(shape):
    return pl.BlockSpec(shape, lambda i: (0,) * len(shape), pipeline_mode=pl.Buffered(1))


def _ffn_fwd(name, x16, res, wgu, wd, g, b, *, tm=512, carry=None):
    t, k = x16.shape
    gi = GATE_UP_INTERLEAVE
    nj, n, ni = wd.shape[0] // gi, wd.shape[1], t // tm
    nc = carry.n if carry is not None else 0

    def body(*refs):
        x_ref, res_ref, wgu_ref, wd_ref, g_ref, b_ref = refs[:6]
        au_ref, hm_ref, y_ref, r_ref, y16_ref = refs[6 + nc:11 + nc]
        if carry is not None:
            finish = _carried(carry, refs[6:6 + nc], refs[11 + nc:11 + 2 * nc], refs[11 + 2 * nc:], pl.program_id(0), ni)
        xv = x_ref[...]
        acc = jnp.zeros((tm, n), F32)
        for j in range(nj):
            au = jnp.dot(xv, wgu_ref[:, 2 * gi * j:2 * gi * (j + 1)], preferred_element_type=F32)
            a, u = au[:, :gi], au[:, gi:]
            au_ref[:, 2 * gi * j:2 * gi * (j + 1)] = _bf(au)
            hm = _bf(a * _sigmoid(a) * u)
            hm_ref[:, gi * j:gi * (j + 1)] = hm
            acc = acc + jnp.dot(hm, wd_ref[gi * j:gi * (j + 1), :], preferred_element_type=F32)
        r = ALPHA * res_ref[...] + 0.5 * acc
        r_ref[...] = r
        y = _layer_norm(r, g_ref[...], b_ref[...])
        y_ref[...] = y
        y16_ref[...] = _bf(y)
        if carry is not None:
            finish()

    row = lambda c: pl.BlockSpec((tm, c), lambda i: (i, 0))
    hbm = pl.BlockSpec(memory_space=pltpu.HBM)
    res_ = pl.pallas_call(
        body, name=name, grid=(ni,),
        in_specs=[row(k), row(n), _resident(wgu.shape), _resident(wd.shape), _resident(g.shape), _resident(b.shape)] + [hbm] * nc,
        out_specs=[row(2 * gi * nj), row(gi * nj), row(n), row(n), row(n)] + [hbm] * nc,
        out_shape=[jax.ShapeDtypeStruct((t, 2 * gi * nj), BF16), jax.ShapeDtypeStruct((t, gi * nj), BF16),
                   jax.ShapeDtypeStruct((t, n), F32), jax.ShapeDtypeStruct((t, n), F32), jax.ShapeDtypeStruct((t, n), BF16)]
        + (carry.out_shape if carry is not None else []),
        scratch_shapes=carry.scratch_shapes if carry is not None else [],
        compiler_params=_params("arbitrary" if carry is not None else "parallel"),
    )(x16, res, wgu, wd, g, b, *(carry.operands if carry is not None else []))
    return tuple(res_[:5]) + ((carry.results(res_[5:]),) if carry is not None else ())


def _ffn_bwd(name, dr16, dr, wdt, au, wgut, *, tm=512, carry=None):
    t, n = dr16.shape
    gi = GATE_UP_INTERLEAVE
    nj, ni = wdt.shape[1] // gi, t // tm
    nc = carry.n if carry is not None else 0

    def body(*refs):
        dr16_ref, dr_ref, wdt_ref, au_ref, wgut_ref = refs[:5]
        dau_ref, dx_ref = refs[5 + nc:7 + nc]
        if carry is not None:
            finish = _carried(carry, refs[5:5 + nc], refs[7 + nc:7 + 2 * nc], refs[7 + 2 * nc:], pl.program_id(0), ni)
        drv = dr16_ref[...]
        acc = jnp.zeros((tm, n), F32)
        for j in range(nj):
            dhm = jnp.dot(drv, wdt_ref[:, gi * j:gi * (j + 1)], preferred_element_type=F32) * 0.5
            au_v = au_ref[:, 2 * gi * j:2 * gi * (j + 1)].astype(F32)
            a, u = au_v[:, :gi], au_v[:, gi:]
            sig = _sigmoid(a)
            silu = a * sig
            dau = jnp.concatenate([_bf(dhm * u * (sig + silu - silu * sig)), _bf(dhm * silu)], axis=1)
            dau_ref[:, 2 * gi * j:2 * gi * (j + 1)] = dau
            acc = acc + jnp.dot(dau, wgut_ref[2 * gi * j:2 * gi * (j + 1), :], preferred_element_type=F32)
        dx_ref[...] = ALPHA * dr_ref[...] + acc
        if carry is not None:
            finish()

    row = lambda c: pl.BlockSpec((tm, c), lambda i: (i, 0))
    hbm = pl.BlockSpec(memory_space=pltpu.HBM)
    res_ = pl.pallas_call(
        body, name=name, grid=(ni,),
        in_specs=[row(n), row(n), _resident(wdt.shape), row(2 * gi * nj), _resident(wgut.shape)] + [hbm] * nc,
        out_specs=[row(2 * gi * nj), row(n)] + [hbm] * nc,
        out_shape=[jax.ShapeDtypeStruct((t, 2 * gi * nj), BF16), jax.ShapeDtypeStruct((t, n), F32)]
        + (carry.out_shape if carry is not None else []),
        scratch_shapes=carry.scratch_shapes if carry is not None else [],
        compiler_params=_params("arbitrary" if carry is not None else "parallel"),
    )(dr16, dr, wdt, au, wgut, *(carry.operands if carry is not None else []))
    return tuple(res_[:2]) + ((carry.results(res_[2:]),) if carry is not None else ())


def _layer_norm(r, g, b):
    mu = jnp.mean(r, axis=-1, keepdims=True)
    var = jnp.mean(jnp.square(r - mu), axis=-1, keepdims=True)
    return (r - mu) * lax.rsqrt(var + LN_EPS) * g + b


def _mm_res_ln(name, a, w, res, g, b, *, scale, tm=256):
    t, k = a.shape
    n = w.shape[1]

    def body(a_ref, w_ref, res_ref, g_ref, b_ref, y_ref, r_ref, y16_ref):
        for rows in _row_chunks(tm):
            r = ALPHA * res_ref[rows, :] + scale * jnp.dot(_bf(a_ref[rows, :]), w_ref[...], preferred_element_type=F32)
            r_ref[rows, :] = r
            y = _layer_norm(r, g_ref[...], b_ref[...])
            y_ref[rows, :] = y
            y16_ref[rows, :] = _bf(y)

    row = lambda c: pl.BlockSpec((tm, c), lambda i: (i, 0))
    const = lambda shape: pl.BlockSpec(shape, lambda i: (0, 0))
    return pl.pallas_call(
        body, name=name, grid=(t // tm,),
        in_specs=[row(k), const((k, n)), row(n), const((1, n)), const((1, n))],
        out_specs=[row(n), row(n), row(n)],
        out_shape=[jax.ShapeDtypeStruct((t, n), F32), jax.ShapeDtypeStruct((t, n), F32), jax.ShapeDtypeStruct((t, n), BF16)],
        compiler_params=_params("parallel"),
    )(a, w, res, g, b)


def _rowwise(name, fn, rows, consts, row_outs, acc_outs=(), tm=512):
    rows = [r if isinstance(r, tuple) else (r, r.shape[1]) for r in rows]
    t = rows[0][0].shape[0]
    tm = min(tm, t)
    assert t % tm == 0
    nr, nc, no, na = len(rows), len(consts), len(row_outs), len(acc_outs)

    def body(*refs):
        vals = [r[...] for r in refs[:nr + nc]]
        outs, accs = fn(*vals)
        for o_ref, o in zip(refs[nr + nc:nr + nc + no], outs):
            o_ref[...] = o.astype(o_ref.dtype)
        if na:
            step = pl.program_id(0)
            acc_refs = refs[nr + nc + no:]

            @pl.when(step == 0)
            def _():
                for a_ref, a in zip(acc_refs, accs):
                    a_ref[...] = a

            @pl.when(step > 0)
            def _():
                for a_ref, a in zip(acc_refs, accs):
                    a_ref[...] += a

    in_specs = [pl.BlockSpec((tm, w), lambda i: (i, 0)) for _, w in rows]
    in_specs += [pl.BlockSpec(c.shape, lambda i, nd=c.ndim: (0,) * nd) for c in consts]
    out_specs = [pl.BlockSpec((tm, c), lambda i: (i, 0)) for c, _ in row_outs]
    out_specs += [pl.BlockSpec(s, lambda i: (0, 0)) for s in acc_outs]
    out_shape = [jax.ShapeDtypeStruct((t, c), dt) for c, dt in row_outs]
    out_shape += [jax.ShapeDtypeStruct(s, F32) for s in acc_outs]
    res = pl.pallas_call(
        body, name=name, grid=(t // tm,), in_specs=in_specs, out_specs=out_specs, out_shape=out_shape,
        compiler_params=_params("arbitrary" if na else "parallel"),
    )(*[r for r, _ in rows], *consts)
    return res


def _ln_bwd(name, r, g, b, dy):
    def fn(r_v, dy_v, g_v, b_v):
        _, vjp = jax.vjp(_layer_norm, r_v, g_v, b_v)
        dr, dg, db = vjp(dy_v)
        return [dr, dr], [dg, db]
    return _rowwise(name, fn, [r, dy], [g, b], [(r.shape[1], F32), (r.shape[1], BF16)], [(1, r.shape[1])] * 2)


def _ln_loss_bwd(name, r, g, b, target):
    def fn(r_v, t_v, g_v, b_v):
        def loss_fn(rr, gg, bb):
            err = jnp.square(_layer_norm(rr, gg, bb) - t_v)
            return 0.5 * jnp.sum(jnp.mean(err, axis=-1, keepdims=True), axis=0, keepdims=True)
        loss, vjp = jax.vjp(loss_fn, r_v, g_v, b_v)
        dr, dg, db = vjp(jnp.ones((1, 1), F32))
        return [dr, dr], [dg, db, jnp.broadcast_to(loss, (1, LANES))]
    return _rowwise(name, fn, [r, target], [g, b], [(r.shape[1], F32), (r.shape[1], BF16)],
                    [(1, r.shape[1])] * 2 + [(1, LANES)])


def _rope_tables(posf, invf, sgn):
    ang = posf * invf
    return jnp.cos(ang), jnp.sin(ang) * sgn


def _rope_apply(tv, cos, sin):
    lane = lax.broadcasted_iota(jnp.int32, cos.shape, 1)
    first = (lane % HEAD_DIM) < (ROPE_DIM // 2)
    outs = []
    for gidx in range(tv.shape[1] // LANES):
        tg = tv[:, LANES * gidx:LANES * (gidx + 1)]
        sw = jnp.where(first, pltpu.roll(tg, LANES - ROPE_DIM // 2, 1), pltpu.roll(tg, ROPE_DIM // 2, 1))
        outs.append(tg * cos + sw * sin)
    return jnp.concatenate(outs, axis=1)


def _proj_in(h16, w_in, posf, invf, sgn, *, tm=512, carry=None):
    t, k = h16.shape
    cuts = [0, D_ATTN, 2 * D_ATTN, 3 * D_ATTN, 3 * D_ATTN + D_SSD, 3 * D_ATTN + D_SSD + D_CONV, w_in.shape[1]]
    nc = carry.n if carry is not None else 0

    def body(*refs):
        h_ref, w_ref, pos_ref, invf_ref, sgn_ref = refs[:5]
        q_ref, k_ref, v_ref, z_ref, xbc_ref, dt_ref, cs_ref = refs[5 + nc:12 + nc]
        if carry is not None:
            finish = _carried(carry, refs[5:5 + nc], refs[12 + nc:12 + 2 * nc], refs[12 + 2 * nc:], pl.program_id(0), t // tm)
        hv = h_ref[...]
        part = lambda a: jnp.dot(hv, w_ref[:, cuts[a]:cuts[a + 1]], preferred_element_type=F32)
        cos, sin = _rope_tables(pos_ref[...], invf_ref[...], sgn_ref[...])
        cs_ref[...] = jnp.concatenate([cos, sin], axis=1)
        q_ref[...] = _bf(_rope_apply(part(0), cos, sin) * (HEAD_DIM ** -0.5))
        k_ref[...] = _bf(_rope_apply(part(1), cos, sin))
        v_ref[...] = _bf(part(2))
        z_ref[...] = part(3)
        xbc_ref[...] = part(4)
        dt_ref[...] = part(5)
        if carry is not None:
            finish()

    row = lambda c: pl.BlockSpec((tm, c), lambda i: (i, 0))
    hbm = pl.BlockSpec(memory_space=pltpu.HBM)
    widths = [D_ATTN, D_ATTN, D_ATTN, D_SSD, D_CONV, LANES, 2 * LANES]
    dtypes = [BF16, BF16, BF16, F32, F32, F32, F32]
    res = pl.pallas_call(
        body, name="proj_in", grid=(t // tm,),
        in_specs=[row(k), _resident(w_in.shape), row(1), _resident(invf.shape), _resident(sgn.shape)] + [hbm] * nc,
        out_specs=[row(c) for c in widths] + [hbm] * nc,
        out_shape=[jax.ShapeDtypeStruct((t, c), dt) for c, dt in zip(widths, dtypes)]
        + (carry.out_shape if carry is not None else []),
        scratch_shapes=carry.scratch_shapes if carry is not None else [],
        compiler_params=_params("arbitrary" if carry is not None else "parallel"),
    )(h16, w_in, posf, invf, sgn, *(carry.operands if carry is not None else []))
    return tuple(res[:7]) + ((carry.results(res[7:]),) if carry is not None else ())


def _rope_bwd(dq, dk, cs):
    def fn(dq_v, dk_v, cs_v):
        cos, sin = cs_v[:, :LANES], -cs_v[:, LANES:]
        gq = _rope_apply(dq_v * (HEAD_DIM ** -0.5), cos, sin)
        gk = _rope_apply(dk_v, cos, sin)
        return [jnp.concatenate([gq, gk], axis=1)], []
    return _rowwise("rope_bwd", fn, [dq, dk, cs], [], [(2 * D_ATTN, BF16)])[0]


def _rms(v, w):
    return v * lax.rsqrt(jnp.mean(v * v, axis=-1, keepdims=True) + RMS_EPS) * w


def _ungroup(yg):
    w = HEADS_PER_GROUP * HEAD_DIM
    return jnp.concatenate([yg[:, GROUP_LANES * g:GROUP_LANES * g + w] for g in range(N_GROUPS)], axis=1)


def _group(xs):
    w = HEADS_PER_GROUP * HEAD_DIM
    parts = []
    for g in range(N_GROUPS):
        parts += [xs[:, w * g:w * (g + 1)], jnp.zeros((xs.shape[0], GROUP_LANES - w), xs.dtype)]
    return jnp.concatenate(parts, axis=1)


def _norms_fn(attn, yg, xs, z, w_attn, w_ssd, dskip):
    a_n = _rms(attn, w_attn)
    y = _ungroup(yg) + dskip * xs
    y_n = _rms(y * (z * _sigmoid(z)), w_ssd)
    return jnp.concatenate([a_n, y_n], axis=1)


def _norms_fwd(attn, yg, xbc, z, w_attn, w_ssd, dskip):
    def fn(*v):
        return [_norms_fn(*v)], []
    return _rowwise("norms_fwd", fn, [attn, yg, (xbc, D_SSD), z], [w_attn, w_ssd, dskip], [(D_ATTN + D_SSD, BF16)])[0]


def _norms_bwd(attn, yg, xbc, z, w_attn, w_ssd, dskip, dcat):
    def fn(attn_v, yg_v, xs_v, z_v, dcat_v, wa_v, ws_v, dk_v):
        _, vjp = jax.vjp(_norms_fn, attn_v, yg_v, xs_v, z_v, wa_v, ws_v, dk_v)
        d_attn, d_yg, d_xs, d_z, d_wa, d_ws, d_dk = vjp(dcat_v)
        return [d_attn, d_yg, d_xs, d_z], [d_wa, d_ws, d_dk]
    return _rowwise("norms_bwd", fn, [attn, yg, (xbc, D_SSD), z, dcat], [w_attn, w_ssd, dskip],
                    [(D_ATTN, F32), (N_GROUPS * GROUP_LANES, F32), (D_SSD, F32), (D_SSD, BF16)], [(1, D_SSD)] * 3)


def _spread_sum(v, e):
    h1 = _bf(v)
    r1 = v - h1.astype(F32)
    h2 = _bf(r1)
    h3 = _bf(r1 - h2.astype(F32))
    return sum(jnp.dot(h, e, preferred_element_type=F32) for h in (h1, h2, h3))


@jax.custom_vjp
def _spread(v, e, e_t):
    return _spread_sum(v, e)


def _spread_fwd(v, e, e_t):
    return _spread_sum(v, e), (e, e_t)


def _spread_bwd(saved, g):
    e, e_t = saved
    return _spread_sum(g, e_t), jnp.zeros_like(e), jnp.zeros_like(e_t)


_spread.defvjp(_spread_fwd, _spread_bwd)


def _ssd_prep_fn(xs, dtp, dtb, alog, e_x, e_xt, e_a, e_at):
    dt = jax.nn.softplus(dtp + dtb)
    a = -jnp.exp(alog)
    xdtg = _group(xs) * _spread(dt, e_x, e_xt)
    dag = _spread(dt * a, e_a, e_at)
    return xdtg, dag


def _ssd_prep_fwd(xbc, dtp, dtb, alog, spreaders):
    def fn(xbc_v, dtp_v, dtb_v, alog_v, *e_v):
        xdtg, dag = _ssd_prep_fn(xbc_v[:, :D_SSD], dtp_v, dtb_v, alog_v, *e_v)
        return [xdtg, xbc_v[:, D_SSD:], dag], []
    return _rowwise("ssd_prep_fwd", fn, [xbc, dtp], [dtb, alog, *spreaders],
                    [(N_GROUPS * GROUP_LANES, BF16), (D_CONV - D_SSD, BF16), (N_GROUPS * LANES, F32)])


def _ssd_prep_bwd(xbc, dtp, dtb, alog, spreaders, dxdtg, ddag, dxs_a, db, dc):
    def fn(xs_v, dtp_v, dxdtg_v, ddag_v, dxs_a_v, db_v, dc_v, dtb_v, alog_v, *e_v):
        _, vjp = jax.vjp(lambda a, b, c, d: _ssd_prep_fn(a, b, c, d, *e_v), xs_v, dtp_v, dtb_v, alog_v)
        dxs, ddtp, ddtb, dalog = vjp((dxdtg_v, ddag_v))
        return [jnp.concatenate([dxs + dxs_a_v, db_v, dc_v], axis=1), ddtp], [ddtb, dalog]
    return _rowwise("ssd_prep_bwd", fn, [(xbc, D_SSD), dtp, dxdtg, ddag, dxs_a, db, dc], [dtb, alog, *spreaders],
                    [(D_CONV, F32), (LANES, BF16)], [(1, LANES)] * 2)


def _shift_down(u, d):
    if d == 0:
        return u
    row = lax.broadcasted_iota(jnp.int32, u.shape, 0)
    return jnp.where(row >= d, pltpu.roll(u, d, 0), 0.0)


def _shift_up(u, d):
    if d == 0:
        return u
    s = u.shape[0]
    row = lax.broadcasted_iota(jnp.int32, u.shape, 0)
    return jnp.where(row < s - d, pltpu.roll(u, s - d, 0), 0.0)


def _conv_pre(u, w, b):
    acc = b
    for k in range(CONV_WIDTH):
        acc = acc + w[k:k + 1, :] * _shift_down(u, CONV_WIDTH - 1 - k)
    return acc


def _conv_fwd(u, w, b, *, tc=256):
    nb, s, c = u.shape

    def body(u_ref, w_ref, b_ref, o_ref):
        pre = _conv_pre(u_ref[0], w_ref[...], b_ref[...])
        o_ref[0] = pre * _sigmoid(pre)

    return pl.pallas_call(
        body, name="conv_fwd", grid=(c // tc, nb),
        in_specs=[pl.BlockSpec((1, s, tc), lambda j, i: (i, 0, j)), pl.BlockSpec((CONV_WIDTH, tc), lambda j, i: (0, j)),
                  pl.BlockSpec((1, tc), lambda j, i: (0, j))],
        out_specs=pl.BlockSpec((1, s, tc), lambda j, i: (i, 0, j)),
        out_shape=jax.ShapeDtypeStruct((nb, s, c), F32),
        compiler_params=_params("parallel", "parallel"),
    )(u, w, b)


def _conv_bwd(u, w, b, dout, *, tc=256):
    nb, s, c = u.shape

    def body(u_ref, w_ref, b_ref, d_ref, du_ref, dw_ref, db_ref):
        uv, wv = u_ref[0], w_ref[...]
        pre = _conv_pre(uv, wv, b_ref[...])
        sig = _sigmoid(pre)
        dpre = d_ref[0] * (sig * (1.0 + pre * (1.0 - sig)))
        du = jnp.zeros_like(uv)
        dws = []
        for k in range(CONV_WIDTH):
            du = du + wv[k:k + 1, :] * _shift_up(dpre, CONV_WIDTH - 1 - k)
            dws.append(jnp.sum(dpre * _shift_down(uv, CONV_WIDTH - 1 - k), axis=0, keepdims=True))
        du_ref[0] = _bf(du)
        dwv = jnp.concatenate(dws + [jnp.zeros((8 - CONV_WIDTH, tc), F32)], axis=0)
        dbv = jnp.sum(dpre, axis=0, keepdims=True)
        first = pl.program_id(1) == 0

        @pl.when(first)
        def _():
            dw_ref[...] = dwv
            db_ref[...] = dbv

        @pl.when(jnp.logical_not(first))
        def _():
            dw_ref[...] += dwv
            db_ref[...] += dbv

    blk = pl.BlockSpec((1, s, tc), lambda j, i: (i, 0, j))
    return pl.pallas_call(
        body, name="conv_bwd", grid=(c // tc, nb),
        in_specs=[blk, pl.BlockSpec((CONV_WIDTH, tc), lambda j, i: (0, j)), pl.BlockSpec((1, tc), lambda j, i: (0, j)), blk],
        out_specs=[blk, pl.BlockSpec((8, tc), lambda j, i: (0, j)), pl.BlockSpec((1, tc), lambda j, i: (0, j))],
        out_shape=[jax.ShapeDtypeStruct((nb, s, c), BF16), jax.ShapeDtypeStruct((8, c), F32), jax.ShapeDtypeStruct((1, c), F32)],
        compiler_params=_params("parallel", "arbitrary"),
    )(u, w, b, dout)


FWD_KEY_BLOCK = 256


def _branch_bias_table(seq, kb):
    ratio = SEQ_BLOCK // kb
    key = np.arange(kb)[None, :, None]
    query = np.arange(SEQ_BLOCK)[None, None, :]
    delta = (np.arange(seq // kb)[:, None, None] - (ratio - 1)) * kb + query - key
    cnt = np.zeros(delta.shape, np.float64)
    for window, dilation in ((128, 1), (512, 4), (2048, 16)):
        cnt += (delta >= 0) & (delta % dilation == 0) & (delta <= window)
    return jnp.asarray(np.where(cnt > 0, np.log(np.maximum(cnt, 1.0)), NEG).astype(np.float32))


HEADS_PER_BLOCK = LANES // HEAD_DIM


def _head_rows(v, h):
    row = lax.broadcasted_iota(jnp.int32, v.shape, 0)
    return jnp.where((row >= HEAD_DIM * h) & (row < HEAD_DIM * (h + 1)), v, jnp.zeros_like(v))


def _attn_fwd(q, k, v, bias):
    nb_, s, _ = q.shape
    ab, kb = SEQ_BLOCK, FWD_KEY_BLOCK
    nblk, nkb, ratio = s // ab, s // kb, ab // kb

    def body(q_ref, k_ref, v_ref, b_ref, o_ref, lse_ref, vt_s):
        i = pl.program_id(2)

        @pl.when(i == 0)
        def _():
            for jb in range(nkb):
                vt_s[jb] = v_ref[0, kb * jb:kb * (jb + 1), :].T

        qt = q_ref[0].T
        qts = [_head_rows(qt, h) for h in range(HEADS_PER_BLOCK)]

        last = ratio * (i + 1) - 1

        def scores(j):
            kj = k_ref[0, pl.ds(pl.multiple_of(j * kb, kb), kb), :]
            return [jnp.dot(kj, qts[h], preferred_element_type=F32) for h in range(HEADS_PER_BLOCK)]

        def step(j, carry):
            ahead = scores(jnp.minimum(j + 1, last))
            lb = b_ref[ratio * i - j + (ratio - 1)]
            out = []
            for h in range(HEADS_PER_BLOCK):
                m, l, acc = carry[3 * h:3 * h + 3]
                st = carry[3 * HEADS_PER_BLOCK + h] + lb
                m_new = jnp.maximum(m, jnp.max(st, axis=0, keepdims=True))
                p = jnp.exp(st - m_new)
                a = jnp.exp(m - m_new)
                l = a * l + jnp.sum(p, axis=0, keepdims=True)
                vt = vt_s[j, HEAD_DIM * h:HEAD_DIM * (h + 1), :]
                acc = a * acc + jnp.dot(vt, _bf(p), preferred_element_type=F32)
                out += [m_new, l, acc]
            return tuple(out) + tuple(ahead)

        init = (jnp.full((1, ab), NEG, F32), jnp.zeros((1, ab), F32), jnp.zeros((HEAD_DIM, ab), F32)) * HEADS_PER_BLOCK
        res = lax.fori_loop(0, ratio * (i + 1), step, init + tuple(scores(0)))
        ot = jnp.concatenate([res[3 * h + 2] / res[3 * h + 1] for h in range(HEADS_PER_BLOCK)], axis=0)
        o_ref[0] = ot.T
        rows = [res[3 * h] + jnp.log(res[3 * h + 1]) for h in range(HEADS_PER_BLOCK)]
        lse_ref[0, 0, 0] = jnp.concatenate(rows + [jnp.zeros((8 - HEADS_PER_BLOCK, ab), F32)], axis=0)

    qblk = pl.BlockSpec((1, ab, LANES), lambda b, hp, i: (b, i, hp))
    full = pl.BlockSpec((1, s, LANES), lambda b, hp, i: (b, 0, hp))
    return pl.pallas_call(
        body, name="attn_fwd", grid=(nb_, D_ATTN // LANES, nblk),
        in_specs=[qblk, full, full, pl.BlockSpec((nkb, kb, ab), lambda b, hp, i: (0, 0, 0))],
        out_specs=[qblk, pl.BlockSpec((1, 1, 1, 8, ab), lambda b, hp, i: (b, hp, i, 0, 0))],
        out_shape=[jax.ShapeDtypeStruct((nb_, s, D_ATTN), F32),
                   jax.ShapeDtypeStruct((nb_, D_ATTN // LANES, nblk, 8, ab), F32)],
        scratch_shapes=[pltpu.VMEM((nkb, LANES, kb), BF16)],
        compiler_params=_params("parallel", "parallel", "arbitrary"),
    )(q, k, v, bias)


def _attn_bwd(q, k, v, o, do, lse, bias):
    nb_, s, _ = q.shape
    ab = SEQ_BLOCK
    nblk = s // ab

    nh = HEADS_PER_BLOCK

    def body(q_ref, k_ref, v_ref, o_ref, do_ref, lse_ref, b_ref, dq_ref, dk_ref, dv_ref,
             qt_s, dot_s, kt_s, dqt_s, do16_s, d_s, dk_acc, dv_acc):
        for jb in range(nblk):
            sl = slice(ab * jb, ab * (jb + 1))
            qt, kt = q_ref[0, sl, :].T, k_ref[0, sl, :].T
            do = do_ref[0, sl, :]
            dot = do.T
            prod = dot * o_ref[0, sl, :].T
            do16_s[sl, :] = _bf(do)
            for h in range(nh):
                qt_s[nh * jb + h] = _head_rows(qt, h)
                kt_s[nh * jb + h] = _head_rows(kt, h)
                dot_s[nh * jb + h] = _head_rows(_bf(dot), h)
            d_s[jb] = jnp.concatenate(
                [jnp.sum(prod[HEAD_DIM * h:HEAD_DIM * (h + 1)], axis=0, keepdims=True) for h in range(nh)]
                + [jnp.zeros((8 - nh, ab), F32)], axis=0)
            dqt_s[jb] = jnp.zeros((LANES, ab), F32)

        def outer(j, carry):
            ks = pl.ds(pl.multiple_of(j * ab, ab), ab)
            kj, vj = k_ref[0, ks, :], v_ref[0, ks, :]
            dk_acc[...] = jnp.zeros_like(dk_acc)
            dv_acc[...] = jnp.zeros_like(dv_acc)

            def inner(i, c2):
                qs = pl.ds(pl.multiple_of(i * ab, ab), ab)
                qi, doi = q_ref[0, qs, :], do16_s[qs, :]
                lb = b_ref[i - j]
                for h in range(nh):
                    st = jnp.dot(kj, qt_s[nh * i + h], preferred_element_type=F32) + lb
                    pt = jnp.exp(st - lse_ref[0, 0, i, h:h + 1, :])
                    dpt = jnp.dot(vj, dot_s[nh * i + h], preferred_element_type=F32)
                    dst16 = _bf(pt * (dpt - d_s[i, h:h + 1, :]))
                    dv_acc[h] += jnp.dot(_bf(pt), doi, preferred_element_type=F32)
                    dk_acc[h] += jnp.dot(dst16, qi, preferred_element_type=F32)
                    dqt_s[i] += jnp.dot(kt_s[nh * j + h], dst16, preferred_element_type=F32)
                return c2

            lax.fori_loop(j, nblk, inner, 0)
            lane = lax.broadcasted_iota(jnp.int32, (ab, LANES), 1)
            dk_ref[0, ks, :] = jnp.where(lane < HEAD_DIM, dk_acc[0], dk_acc[1])
            dv_ref[0, ks, :] = _bf(jnp.where(lane < HEAD_DIM, dv_acc[0], dv_acc[1]))
            return carry

        lax.fori_loop(0, nblk, outer, 0)
        for jb in range(nblk):
            dq_ref[0, ab * jb:ab * (jb + 1), :] = dqt_s[jb].T

    assert nh == 2
    full = pl.BlockSpec((1, s, LANES), lambda b, hp: (b, 0, hp))
    return pl.pallas_call(
        body, name="attn_bwd", grid=(nb_, D_ATTN // LANES),
        in_specs=[full] * 5 + [pl.BlockSpec((1, 1, nblk, 8, ab), lambda b, hp: (b, hp, 0, 0, 0)),
                               pl.BlockSpec((nblk, ab, ab), lambda b, hp: (0, 0, 0))],
        out_specs=[full, full, full],
        out_shape=[jax.ShapeDtypeStruct((nb_, s, D_ATTN), F32), jax.ShapeDtypeStruct((nb_, s, D_ATTN), F32),
                   jax.ShapeDtypeStruct((nb_, s, D_ATTN), BF16)],
        scratch_shapes=[pltpu.VMEM((nh * nblk, LANES, ab), BF16), pltpu.VMEM((nh * nblk, LANES, ab), BF16),
                        pltpu.VMEM((nh * nblk, LANES, ab), BF16), pltpu.VMEM((nblk, LANES, ab), F32),
                        pltpu.VMEM((s, LANES), BF16), pltpu.VMEM((nblk, 8, ab), F32),
                        pltpu.VMEM((nh, ab, LANES), F32), pltpu.VMEM((nh, ab, LANES), F32)],
        compiler_params=_params("parallel", "parallel"),
    )(q, k, v, o, do, lse, bias)


def _cumsum_fwd(dag):
    nb_, s, c = dag.shape
    ab = SEQ_BLOCK

    def body(a_ref, o_ref, ot_ref):
        r = lax.broadcasted_iota(jnp.int32, (ab, ab), 0)
        cc = lax.broadcasted_iota(jnp.int32, (ab, ab), 1)
        tri = (r >= cc).astype(F32)
        carry = jnp.zeros((1, c), F32)
        for i in range(s // ab):
            loc = jnp.dot(tri, a_ref[0, ab * i:ab * (i + 1), :], precision=HIGHEST, preferred_element_type=F32) + carry
            o_ref[0, ab * i:ab * (i + 1), :] = loc
            ot_ref[0, :, ab * i:ab * (i + 1)] = loc.T
            carry = loc[ab - 1:ab, :]

    return pl.pallas_call(
        body, name="ssd_cumsum", grid=(nb_,),
        in_specs=[pl.BlockSpec((1, s, c), lambda b: (b, 0, 0))],
        out_specs=[pl.BlockSpec((1, s, c), lambda b: (b, 0, 0)), pl.BlockSpec((1, c, s), lambda b: (b, 0, 0))],
        out_shape=[jax.ShapeDtypeStruct((nb_, s, c), F32), jax.ShapeDtypeStruct((nb_, c, s), F32)],
        compiler_params=_params("parallel"),
    )(dag)


def _cumsum_bwd(dcol, drow):
    nb_, s, c = dcol.shape
    ab = SEQ_BLOCK

    def body(c_ref, r_ref, o_ref):
        r = lax.broadcasted_iota(jnp.int32, (ab, ab), 0)
        cc = lax.broadcasted_iota(jnp.int32, (ab, ab), 1)
        tri = (r <= cc).astype(F32)
        carry = jnp.zeros((1, c), F32)
        for i in reversed(range(s // ab)):
            rows = r_ref[0, :, ab * i:ab * (i + 1)].T
            parts = []
            for g in range(N_GROUPS):
                parts += [rows[:, 8 * g:8 * (g + 1)], jnp.zeros((ab, LANES - 8), F32)]
            blk = c_ref[0, ab * i:ab * (i + 1), :] + jnp.concatenate(parts, axis=1)
            loc = jnp.dot(tri, blk, precision=HIGHEST, preferred_element_type=F32) + carry
            o_ref[0, ab * i:ab * (i + 1), :] = loc
            carry = loc[0:1, :]

    return pl.pallas_call(
        body, name="ssd_cumsum_bwd", grid=(nb_,),
        in_specs=[pl.BlockSpec((1, s, c), lambda b: (b, 0, 0)), pl.BlockSpec((1, N_GROUPS * 8, s), lambda b: (b, 0, 0))],
        out_specs=pl.BlockSpec((1, s, c), lambda b: (b, 0, 0)),
        out_shape=jax.ShapeDtypeStruct((nb_, s, c), F32),
        compiler_params=_params("parallel"),
    )(dcol, drow)


def _causal_ok(i, j):
    ab = SEQ_BLOCK
    r = lax.broadcasted_iota(jnp.int32, (ab, ab), 0)
    c = lax.broadcasted_iota(jnp.int32, (ab, ab), 1)
    return (r + (i - j) * ab) >= c


def _causal_ok_t(i, j):
    ab = SEQ_BLOCK
    r = lax.broadcasted_iota(jnp.int32, (ab, ab), 0)
    c = lax.broadcasted_iota(jnp.int32, (ab, ab), 1)
    return (c + (i - j) * ab) >= r


def _ssd_chunk(s_in, x, bm_t, cm, cb, acol, arow, a_prev, ok):
    q = x.shape[0]
    decay = jnp.exp(jnp.where(ok, acol - arow, NEG))
    y = jnp.dot(_bf(cb * decay), x, preferred_element_type=F32)
    y = y + jnp.exp(acol - a_prev) * jnp.dot(cm, _bf(s_in), preferred_element_type=F32)
    a_end = acol[q - 1:q, :]
    wx = _bf(jnp.exp(a_end - acol) * x.astype(F32))
    s_out = jnp.exp(a_end - a_prev) * s_in + jnp.dot(bm_t, wx, preferred_element_type=F32)
    return y, s_out


def _ssd_specs(s):
    xblk = pl.BlockSpec((1, s, GROUP_LANES), lambda b, g: (b, 0, g))
    bblk = pl.BlockSpec((1, s, D_STATE), lambda b, g: (b, 0, g))
    cblk = pl.BlockSpec((1, s, D_STATE), lambda b, g: (b, 0, N_GROUPS + g))
    tblk = pl.BlockSpec((1, 8, s), lambda b, g: (b, (LANES // 8) * g, 0))
    return xblk, bblk, cblk, tblk


def _chunk_views(i, j, x_ref, ac_ref, at_ref):
    ab = SEQ_BLOCK
    sl = slice(ab * i, ab * (i + 1))
    hs = slice(HEAD_DIM * j, HEAD_DIM * (j + 1))
    a_prev = jnp.zeros((1, 1), F32) if i == 0 else ac_ref[0, ab * i - 1:ab * i, j:j + 1]
    return sl, hs, ac_ref[0, sl, j:j + 1], at_ref[0, j:j + 1, sl], a_prev


def _ssd_fwd_chunked(xdtg, bc, acum, acum_t):
    nb_, s, _ = xdtg.shape
    ab = SEQ_BLOCK
    hpg = HEADS_PER_GROUP

    def body(x_ref, b_ref, c_ref, ac_ref, at_ref, y_ref):
        ok = _causal_ok(0, 0)
        states = [jnp.zeros((D_STATE, HEAD_DIM), F32) for _ in range(hpg)]
        for i in range(s // ab):
            bm, cm = b_ref[0, ab * i:ab * (i + 1), :], c_ref[0, ab * i:ab * (i + 1), :]
            bm_t = bm.T
            cb = jnp.dot(cm, bm_t, preferred_element_type=F32)
            ys = []
            for j in range(hpg):
                sl, hs, acol, arow, a_prev = _chunk_views(i, j, x_ref, ac_ref, at_ref)
                y, states[j] = _ssd_chunk(states[j], x_ref[0, sl, hs], bm_t, cm, cb, acol, arow, a_prev, ok)
                ys.append(y)
            y_ref[0, sl, :] = jnp.concatenate(ys + [jnp.zeros((ab, GROUP_LANES - hpg * HEAD_DIM), F32)], axis=1)

    xblk, bblk, cblk, tblk = _ssd_specs(s)
    ablk = pl.BlockSpec((1, s, LANES), lambda b, g: (b, 0, g))
    return pl.pallas_call(
        body, name="ssd_fwd", grid=(nb_, N_GROUPS), in_specs=[xblk, bblk, cblk, ablk, tblk], out_specs=xblk,
        out_shape=jax.ShapeDtypeStruct((nb_, s, N_GROUPS * GROUP_LANES), F32),
        compiler_params=_params("parallel", "parallel"),
    )(xdtg, bc, bc, acum, acum_t)


def _ssd_bwd_chunked(xdtg, bc, acum, acum_t, dyg):
    nb_, s, _ = xdtg.shape
    ab = SEQ_BLOCK
    nblk = s // ab
    hpg = HEADS_PER_GROUP

    def body(x_ref, b_ref, c_ref, ac_ref, at_ref, dy_ref, dx_ref, db_ref, dc_ref, dac_ref, dar_ref, s_s):
        ok = _causal_ok(0, 0)
        dx_ref[...] = jnp.zeros_like(dx_ref)
        dac_ref[...] = jnp.zeros_like(dac_ref)
        dar_ref[...] = jnp.zeros_like(dar_ref)
        states = [jnp.zeros((D_STATE, HEAD_DIM), F32) for _ in range(hpg)]
        for i in range(nblk):
            bm_t = b_ref[0, ab * i:ab * (i + 1), :].T
            for j in range(hpg):
                sl, hs, acol, arow, a_prev = _chunk_views(i, j, x_ref, ac_ref, at_ref)
                s_s[hpg * i + j] = states[j]
                if i + 1 < nblk:
                    a_end = acol[ab - 1:ab, :]
                    wx = _bf(jnp.exp(a_end - acol) * x_ref[0, sl, hs].astype(F32))
                    states[j] = jnp.exp(a_end - a_prev) * states[j] + jnp.dot(bm_t, wx, preferred_element_type=F32)
        ok_t = _causal_ok_t(0, 0)
        last_row = lax.broadcasted_iota(jnp.int32, (ab, 1), 0) == ab - 1
        d_state = [jnp.zeros((D_STATE, HEAD_DIM), F32) for _ in range(hpg)]
        pending = [jnp.zeros((1, 1), F32) for _ in range(hpg)]
        total = lambda v: jnp.sum(v, keepdims=True)
        for i in reversed(range(nblk)):
            bm, cm = b_ref[0, ab * i:ab * (i + 1), :], c_ref[0, ab * i:ab * (i + 1), :]
            cm_t = cm.T
            cbt = jnp.dot(bm, cm_t, preferred_element_type=F32)
            dcbt = jnp.zeros((ab, ab), F32)
            d_bm, d_cm = jnp.zeros((ab, D_STATE), F32), jnp.zeros((ab, D_STATE), F32)
            for j in range(hpg):
                sl, hs, acol, arow, a_prev = _chunk_views(i, j, x_ref, ac_ref, at_ref)
                x, dy = x_ref[0, sl, hs], dy_ref[0, sl, hs]
                dy16 = _bf(dy)
                s_in, g_out = s_s[hpg * i + j], d_state[j]
                s16, g16 = _bf(s_in), _bf(g_out)
                decay = jnp.exp(jnp.where(ok_t, arow - acol, NEG))
                gt = cbt * decay
                dgt = lax.dot_general(x, dy16, _NT, preferred_element_type=F32)
                d_x = jnp.dot(_bf(gt), dy16, preferred_element_type=F32)
                dcbt = dcbt + dgt * decay
                mm = dgt * gt
                d_arow = jnp.sum(mm, axis=0, keepdims=True)
                d_acol = -jnp.sum(mm, axis=1, keepdims=True)
                e = jnp.exp(acol - a_prev)
                edy16 = _bf(e * dy)
                d_cm = d_cm + lax.dot_general(edy16, s16, _NT, preferred_element_type=F32)
                d_s = jnp.dot(cm_t, edy16, preferred_element_type=F32)
                de_e = jnp.sum(dy * jnp.dot(cm, s16, preferred_element_type=F32), axis=1, keepdims=True) * e
                a_end = acol[ab - 1:ab, :]
                w = jnp.exp(a_end - acol)
                f = jnp.exp(a_end - a_prev)
                x32 = x.astype(F32)
                bg = jnp.dot(bm, g16, preferred_element_type=F32)
                d_x = d_x + w * bg
                d_bm = d_bm + lax.dot_general(_bf(w * x32), g16, _NT, preferred_element_type=F32)
                dw_w = jnp.sum(bg * x32, axis=1, keepdims=True) * w
                df_f = total(g_out * s_in) * f
                d_end = total(dw_w) + df_f
                d_acol = d_acol + de_e - dw_w + jnp.where(last_row, d_end + pending[j], 0.0)
                pending[j] = -total(de_e) - df_f
                d_state[j] = d_s + f * g_out
                dx_ref[0, sl, hs] = d_x
                dac_ref[0, sl, j:j + 1] = d_acol
                dar_ref[0, j:j + 1, sl] = d_arow
            dcbt16 = _bf(dcbt)
            db_ref[0, ab * i:ab * (i + 1), :] = d_bm + jnp.dot(dcbt16, cm, preferred_element_type=F32)
            dc_ref[0, ab * i:ab * (i + 1), :] = d_cm + lax.dot_general(dcbt16, bm, _TN, preferred_element_type=F32)

    xblk, bblk, cblk, tblk = _ssd_specs(s)
    ablk = pl.BlockSpec((1, s, LANES), lambda b, g: (b, 0, g))
    return pl.pallas_call(
        body, name="ssd_bwd", grid=(nb_, N_GROUPS),
        in_specs=[xblk, bblk, cblk, ablk, tblk, xblk],
        out_specs=[xblk, bblk, bblk, ablk, pl.BlockSpec((1, 8, s), lambda b, g: (b, g, 0))],
        out_shape=[jax.ShapeDtypeStruct((nb_, s, N_GROUPS * GROUP_LANES), F32),
                   jax.ShapeDtypeStruct((nb_, s, N_GROUPS * D_STATE), F32),
                   jax.ShapeDtypeStruct((nb_, s, N_GROUPS * D_STATE), F32),
                   jax.ShapeDtypeStruct((nb_, s, N_GROUPS * LANES), F32),
                   jax.ShapeDtypeStruct((nb_, N_GROUPS * 8, s), F32)],
        scratch_shapes=[pltpu.VMEM((nblk * hpg, D_STATE, HEAD_DIM), F32)],
        compiler_params=_params("parallel", "parallel"),
    )(xdtg, bc, bc, acum, acum_t, dyg)


def _interleave(wg, wu):
    k, f = wg.shape
    gi = GATE_UP_INTERLEAVE
    return jnp.stack([wg.reshape(k, f // gi, gi), wu.reshape(k, f // gi, gi)], axis=2).reshape(k, 2 * f)


def _head_expanders():
    e_x = np.zeros((LANES, N_GROUPS * GROUP_LANES), np.float32)
    e_a = np.zeros((LANES, N_GROUPS * LANES), np.float32)
    for h in range(N_HEADS):
        g, j = divmod(h, HEADS_PER_GROUP)
        e_x[h, GROUP_LANES * g + HEAD_DIM * j:GROUP_LANES * g + HEAD_DIM * (j + 1)] = 1.0
        e_a[h, LANES * g + j] = 1.0
    return [jnp.asarray(m, BF16) for m in (e_x, e_x.T, e_a, e_a.T)]


def _pad_lanes(v, n=LANES):
    return jnp.pad(v, ((0, 0), (0, n - v.shape[1])))


def _local_step(x, positions, target, w, late=None, early_grad_job=None):
    nb, s, d = x.shape
    t = nb * s
    x2 = x.reshape(t, d)
    tgt2 = target.reshape(t, d)
    (job_a, weights_a), (job_b, weights_b) = late if late is not None else ((None, None), (None, None))

    x16 = _bf(x2)
    wgu1 = _interleave(w["ffn1_gate"], w["ffn1_up"])
    ffn1 = _ffn_fwd("ffn1_fwd", x16, x2, wgu1, w["ffn1_down"], w["ln1_g"], w["ln1_b"], carry=job_a)
    au1, hm1, h1, r1, h1_16 = ffn1[:5]
    if job_a is not None:
        w = {**w, **weights_a(ffn1[5])}

    w_in = w["w_in"]
    wqk, wv, wz = w_in[:, :2 * D_ATTN], w_in[:, 2 * D_ATTN:3 * D_ATTN], w_in[:, 3 * D_ATTN:3 * D_ATTN + D_SSD]
    wxbc = w_in[:, 3 * D_ATTN + D_SSD:3 * D_ATTN + D_SSD + D_CONV]
    wdt = _pad_lanes(w_in[:, 3 * D_ATTN + D_SSD + D_CONV:])

    inv_freq = ROPE_THETA ** (-jnp.arange(0, ROPE_DIM, 2, dtype=F32) / ROPE_DIM)
    half = ROPE_DIM // 2
    head_invf = jnp.concatenate([inv_freq, inv_freq, jnp.zeros((HEAD_DIM - ROPE_DIM,), F32)])
    head_sgn = jnp.concatenate([-jnp.ones((half,), F32), jnp.ones((half,), F32), jnp.zeros((HEAD_DIM - ROPE_DIM,), F32)])
    invf = jnp.tile(head_invf, LANES // HEAD_DIM)[None, :]
    sgn = jnp.tile(head_sgn, LANES // HEAD_DIM)[None, :]
    posf = positions.astype(F32).reshape(t, 1)
    bias_fwd, bias_bwd = _branch_bias_table(s, FWD_KEY_BLOCK), _branch_bias_table(s, SEQ_BLOCK)
    spreaders = _head_expanders()
    dtb, alog = _pad_lanes(w["dt_bias"]), _pad_lanes(w["a_log"])
    dskip = jnp.repeat(w["d_skip"], HEAD_DIM, axis=1)

    proj = _proj_in(h1_16, _pad_lanes(w_in, w_in.shape[1] - N_HEADS + LANES), posf, invf, sgn, carry=job_b)
    q16, k16, v16, z, xbc_pre, dtp, cs = proj[:7]
    if job_b is not None:
        w = {**w, **weights_b(proj[7])}
    wgu2 = _interleave(w["ffn2_gate"], w["ffn2_up"])
    to3 =lambda a: a.reshape(nb, s, a.shape[-1])
    attn_o, lse = _attn_fwd(to3(q16), to3(k16), to3(v16), bias_fwd)

    xbc = _conv_fwd(to3(xbc_pre), w["conv_w"], w["conv_b"]).reshape(t, D_CONV)
    xdtg, bc16, dag = _ssd_prep_fwd(xbc, dtp, dtb, alog, spreaders)
    acum, acum_t = _cumsum_fwd(to3(dag))
    yg = _ssd_fwd_chunked(to3(xdtg), to3(bc16), acum, acum_t)

    cat = _norms_fwd(attn_o.reshape(t, D_ATTN), yg.reshape(t, -1), xbc, z, w["attn_norm_w"], w["ssd_norm_w"], dskip)
    h2, r2, h2_16 = _mm_res_ln("w_out_ln2", cat, w["w_out"], h1, w["ln2_g"], w["ln2_b"], scale=1.0)

    au2, hm2, _, r3, _ = _ffn_fwd("ffn2_fwd", h2_16, h2, wgu2, w["ffn2_down"], w["ln3_g"], w["ln3_b"])

    g = {}
    dr3, dr3_16, g["ln3_g"], g["ln3_b"], loss = _ln_loss_bwd("loss_ln3_bwd", r3, w["ln3_g"], w["ln3_b"], tgt2)

    dau2, dh2 = _ffn_bwd("ffn2_bwd", dr3_16, dr3, w["ffn2_down"].T, au2, wgu2.T)
    g["ffn2_down"] = _mm_tn("ffn2_down_dw", hm2, dr3_16, scale=0.5, tk=D_FF // 2, tn=512)
    g["ffn2_gate"], g["ffn2_up"] = _mm_tn_gate_up("ffn2_up_dw", h2_16, dau2)

    dr2, dr2_16, g["ln2_g"], g["ln2_b"] = _ln_bwd("ln2_bwd", r2, w["ln2_g"], w["ln2_b"], dh2)
    dcat = _mm("w_out_dx", [(dr2_16, w["w_out"].T)], tn=768)
    g["w_out"] = _mm_tn("w_out_dw", cat, dr2_16, tk=768, tn=1024)

    d_attn, dyg, dxs_a, dz16, g["attn_norm_w"], g["ssd_norm_w"], ddskip = _norms_bwd(
        attn_o.reshape(t, D_ATTN), yg.reshape(t, -1), xbc, z, w["attn_norm_w"], w["ssd_norm_w"], dskip, dcat)
    g["d_skip"] = ddskip.reshape(N_HEADS, HEAD_DIM).sum(axis=1)[None, :]

    dq, dk, dv16 = _attn_bwd(to3(q16), to3(k16), to3(v16), attn_o, to3(d_attn), lse, bias_bwd)
    dqk16 = _rope_bwd(dq.reshape(t, D_ATTN), dk.reshape(t, D_ATTN), cs)

    dxdtg, dbm, dcm, dacol, darow = _ssd_bwd_chunked(to3(xdtg), to3(bc16), acum, acum_t, to3(dyg))
    ddag = _cumsum_bwd(dacol, darow)
    dxbc, ddtp16, ddtb, dalog = _ssd_prep_bwd(xbc, dtp, dtb, alog, spreaders, dxdtg.reshape(t, -1), ddag.reshape(t, -1),
                                               dxs_a, dbm.reshape(t, -1), dcm.reshape(t, -1))
    g["dt_bias"], g["a_log"] = ddtb[:, :N_HEADS], dalog[:, :N_HEADS]
    dxbc_pre16, dconv_w, g["conv_b"] = _conv_bwd(to3(xbc_pre), w["conv_w"], w["conv_b"], to3(dxbc))
    g["conv_w"] = dconv_w[:CONV_WIDTH]
    dxbc_pre16 = dxbc_pre16.reshape(t, D_CONV)
    dv16 = dv16.reshape(t, D_ATTN)

    dh1 = _mm("w_in_dx", [(dqk16, wqk.T), (dv16, wv.T), (dz16, wz.T), (dxbc_pre16, wxbc.T), (ddtp16, wdt.T)],
              res=dr2, res_scale=ALPHA)
    g["w_in"] = _mm_tn_sections("w_in_dw", h1_16, [dqk16, dv16, dz16, dxbc_pre16, ddtp16])[:, :w_in.shape[1]]

    dr1, dr1_16, g["ln1_g"], g["ln1_b"] = _ln_bwd("ln1_bwd", r1, w["ln1_g"], w["ln1_b"], dh1)
    g["ffn1_down"] = _mm_tn("ffn1_down_dw", hm1, dr1_16, scale=0.5, tk=D_FF // 2, tn=512)
    ffn1b = _ffn_bwd("ffn1_bwd", dr1_16, dr1, w["ffn1_down"].T, au1, wgu1.T,
                     carry=None if early_grad_job is None else early_grad_job(g))
    dau1, dx = ffn1b[:2]
    early = ffn1b[2] if early_grad_job is not None else None
    g["ffn1_gate"], g["ffn1_up"] = _mm_tn_gate_up("ffn1_up_dw", x16, dau1)
    return loss, dx.reshape(nb, s, d), g, early


_HBM = pl.BlockSpec(memory_space=pltpu.HBM)
N_CHIPS = 4
N_DEVICES = 8


def _place():
    return lax.axis_index("x"), lax.axis_index("y"), lax.axis_index("c")


def _other_chips(x, y):
    return [(1 - x, y), (x, 1 - y), (1 - x, 1 - y)]


class _GatherJob:
    def __init__(self, shards):
        assert all((a.shape[0] // 2) % 16 == 0 for a in shards)
        self.n = len(shards)
        self.shapes = [a.shape for a in shards]
        self.operands = [a.reshape(2, a.shape[0] // 2, a.shape[1]) for a in shards]
        self.out_shape = [jax.ShapeDtypeStruct((N_CHIPS,) + a.shape, a.dtype) for a in self.operands]
        pair = pltpu.SemaphoreType.DMA((self.n, N_CHIPS - 1))
        one = pltpu.SemaphoreType.DMA((self.n,))
        self.scratch_shapes = [pair, pair, pair, pair, one, one]

    def results(self, outs):
        return [o.reshape((N_CHIPS,) + s) for o, s in zip(outs, self.shapes)]

    def phases(self, ins, outs, sems):
        n = self.n
        send_sems, recv_sems, fwd_send_sems, fwd_recv_sems, own_send_sems, own_recv_sems = sems
        x, y, c = _place()
        me = 2 * x + y
        peers = _other_chips(x, y)

        def own(t):
            return pltpu.make_async_remote_copy(ins[t], outs[t].at[me], own_send_sems.at[t], own_recv_sems.at[t],
                                                device_id=(x, y, 1 - c), device_id_type=MESH)

        def ici(t, p, src_chip):
            px, py = peers[p]
            return pltpu.make_async_remote_copy(
                ins[t].at[c] if src_chip is None else outs[t].at[src_chip, c],
                outs[t].at[me if src_chip is None else src_chip, c],
                send_sems.at[t, p], recv_sems.at[t, p], device_id=(px, py, c), device_id_type=MESH)

        def d2d(t, p, core):
            px, py = peers[p]
            return pltpu.make_async_remote_copy(
                outs[t].at[2 * px + py, core], outs[t].at[2 * px + py, core],
                fwd_send_sems.at[t, p], fwd_recv_sems.at[t, p], device_id=(x, y, 1 - c), device_id_type=MESH)

        pairs = [(t, p) for t in range(n) for p in range(N_CHIPS - 1)]

        def start():
            for t, p in pairs:
                ici(t, p, None).start()
            for t in range(n):
                own(t).start()

        def forward():
            for t, p in pairs:
                px, py = peers[p]
                ici(t, p, 2 * px + py).wait_recv()
                d2d(t, p, c).start()

        def finish():
            for t, p in pairs:
                d2d(t, p, 1 - c).wait_recv()
            for t in range(n):
                own(t).wait()
            for t, p in pairs:
                ici(t, p, None).wait_send()
                d2d(t, p, c).wait_send()

        return start, forward, finish


class _ExchangeJob:
    def __init__(self, stacks):
        self.n = len(stacks)
        self.operands = list(stacks)
        self.out_shape = [jax.ShapeDtypeStruct(a.shape, a.dtype) for a in stacks]
        pair = pltpu.SemaphoreType.DMA((self.n, N_CHIPS - 1))
        self.scratch_shapes = [pair, pair]

    def results(self, outs):
        return list(outs)

    def phases(self, ins, outs, sems):
        send_sems, recv_sems = sems
        x, y, c = _place()
        me = 2 * x + y
        peers = _other_chips(x, y)
        pairs = [(t, p) for t in range(self.n) for p in range(N_CHIPS - 1)]

        def copy(t, p):
            px, py = peers[p]
            return pltpu.make_async_remote_copy(ins[t].at[2 * px + py], outs[t].at[me], send_sems.at[t, p],
                                                recv_sems.at[t, p], device_id=(px, py, c), device_id_type=MESH)

        def arrival(t, p):
            px, py = peers[p]
            return pltpu.make_async_remote_copy(ins[t].at[me], outs[t].at[2 * px + py], send_sems.at[t, p],
                                                recv_sems.at[t, p], device_id=(px, py, c), device_id_type=MESH)

        def start():
            for t, p in pairs:
                copy(t, p).start()

        def finish():
            for t, p in pairs:
                arrival(t, p).wait_recv()
            for t, p in pairs:
                copy(t, p).wait_send()

        return start, None, finish


def _run_job(job, name):
    n = job.n

    def body(*refs):
        for phase in job.phases(refs[:n], refs[n:2 * n], refs[2 * n:]):
            if phase is not None:
                phase()

    outs = pl.pallas_call(
        body, name=name, in_specs=[_HBM] * n, out_specs=[_HBM] * n,
        out_shape=job.out_shape, scratch_shapes=job.scratch_shapes,
    )(*job.operands)
    return job.results(outs)


def _sibling_halves(stacks, name):
    n = len(stacks)
    halves = [a.shape[1] // 2 for a in stacks]
    split = [a.reshape(a.shape[0], 2, h, a.shape[2]) for a, h in zip(stacks, halves)]

    def body(*refs):
        ins, outs = refs[:n], refs[n:2 * n]
        send_sems, recv_sems = refs[2 * n:]
        x, y, c = _place()
        cps = []
        for t in range(n):
            cp = pltpu.make_async_remote_copy(ins[t].at[:, 1 - c], outs[t], send_sems.at[t], recv_sems.at[t],
                                              device_id=(x, y, 1 - c), device_id_type=MESH)
            cp.start()
            cps.append(cp)
        for cp in cps:
            cp.wait()

    return pl.pallas_call(
        body, name=name,
        in_specs=[_HBM] * n, out_specs=[_HBM] * n,
        out_shape=[jax.ShapeDtypeStruct((a.shape[0], h, a.shape[2]), a.dtype) for a, h in zip(stacks, halves)],
        scratch_shapes=[pltpu.SemaphoreType.DMA((n,)), pltpu.SemaphoreType.DMA((n,))],
    )(*split)


def _sibling_swap(arrs):
    n = len(arrs)

    def body(*refs):
        ins, outs = refs[:n], refs[n:2 * n]
        send_sems, recv_sems = refs[2 * n:]
        x, y, c = _place()
        cps = []
        for t in range(n):
            cp = pltpu.make_async_remote_copy(ins[t], outs[t], send_sems.at[t], recv_sems.at[t],
                                              device_id=(x, y, 1 - c), device_id_type=MESH)
            cp.start()
            cps.append(cp)
        for cp in cps:
            cp.wait()

    return pl.pallas_call(
        body, name="sibling_swap",
        in_specs=[_HBM] * n, out_specs=[_HBM] * n,
        out_shape=[jax.ShapeDtypeStruct(a.shape, a.dtype) for a in arrs],
        scratch_shapes=[pltpu.SemaphoreType.DMA((n,)), pltpu.SemaphoreType.DMA((n,))],
    )(*arrs)


def _half_sum(name, own, other, core):
    k, r, cols = own.shape
    h = r // 2
    tr = next(cand for cand in (128, 176, 64, 32, 16) if h % cand == 0)
    nblk = h // tr

    def body(core_ref, own_ref, other_ref, o_ref):
        o_ref[...] = _bf(own_ref[...] + other_ref[...].astype(F32))

    grid_spec = pltpu.PrefetchScalarGridSpec(
        num_scalar_prefetch=1, grid=(nblk,),
        in_specs=[pl.BlockSpec((k, tr, cols), lambda i, core_ref: (0, i + core_ref[0] * nblk, 0)),
                  pl.BlockSpec((k, tr, cols), lambda i, core_ref: (0, i, 0))],
        out_specs=pl.BlockSpec((k, tr, cols), lambda i, core_ref: (0, i, 0)))
    return pl.pallas_call(
        body, name=name, grid_spec=grid_spec, out_shape=jax.ShapeDtypeStruct((k, h, cols), BF16),
        compiler_params=_params("parallel"),
    )(core.reshape(1).astype(jnp.int32), own, other)


def _small_allreduce(v):
    r = v.shape[0]

    def body(v_ref, tot_ref, slots, send_sems, recv_sems):
        x, y, c = _place()
        me = 4 * x + 2 * y + c
        slots[me] = v_ref[...]
        cps, peers = [], []
        for k in range(1, N_DEVICES):
            px = 1 - x if (k >> 2) & 1 else x
            py = 1 - y if (k >> 1) & 1 else y
            pc = 1 - c if k & 1 else c
            cp = pltpu.make_async_remote_copy(v_ref, slots.at[me], send_sems.at[k - 1], recv_sems.at[k - 1],
                                              device_id=(px, py, pc), device_id_type=MESH)
            cp.start()
            cps.append(cp)
            peers.append((px, py, pc))
        for k, (px, py, pc) in enumerate(peers):
            pltpu.make_async_remote_copy(v_ref, slots.at[4 * px + 2 * py + pc], send_sems.at[k], recv_sems.at[k],
                                         device_id=(px, py, pc), device_id_type=MESH).wait_recv()
        for cp in cps:
            cp.wait_send()
        acc = slots[0]
        for s in range(1, N_DEVICES):
            acc = acc + slots[s]
        tot_ref[...] = acc

    return pl.pallas_call(
        body, name="small_allreduce",
        in_specs=[pl.BlockSpec(memory_space=pltpu.VMEM)], out_specs=pl.BlockSpec(memory_space=pltpu.VMEM),
        out_shape=jax.ShapeDtypeStruct((r, LANES), F32),
        scratch_shapes=[pltpu.VMEM((N_DEVICES, r, LANES), F32), pltpu.SemaphoreType.DMA((N_DEVICES - 1,)),
                        pltpu.SemaphoreType.DMA((N_DEVICES - 1,))],
    )(v)


def _elementwise(name, fn, ins, out_dtypes):
    r, c = ins[0].shape[-2:]
    tr = next((cand for cand in (256, 176, 128, 64, 32, 16) if r % cand == 0), r)
    nin = len(ins)

    def body(*refs):
        outs = fn(*[ref[...] for ref in refs[:nin]])
        for o_ref, o in zip(refs[nin:], outs):
            o_ref[...] = o.astype(o_ref.dtype)

    in_specs = [pl.BlockSpec((tr, c), lambda i: (i, 0)) if a.ndim == 2 else pl.BlockSpec((a.shape[0], tr, c), lambda i: (0, i, 0))
                for a in ins]
    return pl.pallas_call(
        body, name=name, grid=(r // tr,), in_specs=in_specs,
        out_specs=[pl.BlockSpec((tr, c), lambda i: (i, 0)) for _ in out_dtypes],
        out_shape=[jax.ShapeDtypeStruct((r, c), dt) for dt in out_dtypes],
        compiler_params=_params("parallel"),
    )(*ins)


def _row_tile(rows):
    return next((cand for cand in (128, 176, 64, 32, 16) if rows % cand == 0), rows)


def _sum_slots(name, received, own, chip):
    _, r, cols = own.shape
    tr = _row_tile(r)

    def body(chip_ref, own_ref, a_ref, b_ref, c_ref, o_ref):
        o_ref[...] = ((own_ref[0].astype(F32) + a_ref[0].astype(F32)) + b_ref[0].astype(F32)) + c_ref[0].astype(F32)

    def slot(flip):
        return pl.BlockSpec((1, tr, cols), lambda i, chip_ref: (jnp.bitwise_xor(chip_ref[0], flip), i, 0))

    grid_spec = pltpu.PrefetchScalarGridSpec(
        num_scalar_prefetch=1, grid=(r // tr,), in_specs=[slot(0), slot(1), slot(2), slot(3)],
        out_specs=pl.BlockSpec((tr, cols), lambda i, chip_ref: (i, 0)))
    return pl.pallas_call(
        body, name=name, grid_spec=grid_spec, out_shape=jax.ShapeDtypeStruct((r, cols), F32),
        compiler_params=_params("parallel"),
    )(chip.reshape(1).astype(jnp.int32), own, received, received, received)


def _adamw_halves(name, mine, theirs, core, w, m, v):
    h, cols = mine.shape
    tr = _row_tile(h)
    nh = h // tr

    def body(core_ref, mine_ref, theirs_ref, w_ref, m_ref, v_ref, g_ref, d_ref, m2_ref, v2_ref):
        is_mine = (pl.program_id(0) // nh) == core_ref[0]
        g = jnp.where(is_mine, mine_ref[...], theirs_ref[...])
        outs = _adamw_math(g, w_ref[...], m_ref[...], v_ref[...])
        for ref, val in zip((g_ref, d_ref, m2_ref, v2_ref), outs):
            ref[...] = val

    half = pl.BlockSpec((tr, cols), lambda i, core_ref: (i % nh, 0))
    full = pl.BlockSpec((tr, cols), lambda i, core_ref: (i, 0))
    grid_spec = pltpu.PrefetchScalarGridSpec(
        num_scalar_prefetch=1, grid=(2 * nh,), in_specs=[half, half, full, full, full], out_specs=[full] * 4)
    return pl.pallas_call(
        body, name=name, grid_spec=grid_spec, out_shape=[jax.ShapeDtypeStruct((2 * h, cols), F32)] * 4,
        compiler_params=_params("parallel"),
    )(core.reshape(1).astype(jnp.int32), mine, theirs, w, m, v)


def _adamw_math(g, w_v, m_v, v_v):
    m2 = ADAM_B1 * m_v + (1.0 - ADAM_B1) * g
    v2 = ADAM_B2 * v_v + (1.0 - ADAM_B2) * jnp.square(g)
    m_hat = m2 / (1.0 - ADAM_B1 ** ADAM_STEP)
    v_hat = v2 / (1.0 - ADAM_B2 ** ADAM_STEP)
    delta = -ADAM_LR * (m_hat / (jnp.sqrt(v_hat) + ADAM_EPS) + ADAM_WD * w_v)
    return [g, delta, m2, v2]


def _adamw(name, g, w, m, v):
    return _elementwise(name, _adamw_math, [g, w, m, v], [F32] * 4)


_TRANSPOSED = ("ffn1_gate", "ffn1_up", "ffn2_gate", "ffn2_up")
_MATRICES = (("ffn1_gate", 0), ("ffn1_up", 0), ("ffn1_down", 0), ("w_in", 1), ("w_out", 0),
             ("ffn2_gate", 0), ("ffn2_up", 0), ("ffn2_down", 0))


def _block2d(a, name):
    return jnp.swapaxes(a, 1, 2)[0] if name in _TRANSPOSED else a[0]


def _block3d(a, name):
    return jnp.swapaxes(a[None], 1, 2) if name in _TRANSPOSED else a[None]
_VECTORS = ("ln1_g", "ln1_b", "conv_b", "dt_bias", "a_log", "d_skip", "attn_norm_w", "ssd_norm_w",
            "ln2_g", "ln2_b", "ln3_g", "ln3_b")
_WEIGHT_ORDER = ("ln1_g", "ln1_b", "ffn1_gate", "ffn1_up", "ffn1_down", "w_in", "conv_w", "conv_b", "dt_bias", "a_log",
                 "d_skip", "attn_norm_w", "ssd_norm_w", "w_out", "ln2_g", "ln2_b", "ffn2_gate", "ffn2_up", "ffn2_down",
                 "ln3_g", "ln3_b")


def _pack_rows(vectors):
    parts = []
    for vec in vectors:
        flat = vec.reshape(-1)
        parts.append(jnp.pad(flat, (0, (-flat.shape[0]) % LANES)))
    flat = jnp.concatenate(parts)
    flat = jnp.pad(flat, (0, (-flat.shape[0]) % (8 * LANES)))
    return flat.reshape(-1, LANES)


def _unpack_rows(packed, shapes):
    flat = packed.reshape(-1)
    out, off = [], 0
    for shape in shapes:
        size = int(np.prod(shape))
        out.append(flat[off:off + size].reshape(shape))
        off += size + (-size) % LANES
    return out


def _assemble(stack, axis):
    if axis == 0:
        return stack.reshape(-1, stack.shape[2])
    return jnp.concatenate([stack[s] for s in range(N_CHIPS)], axis=1)


def _split(full, axis):
    if axis == 0:
        return full.reshape(N_CHIPS, -1, full.shape[1])
    cols = full.shape[1] // N_CHIPS
    return jnp.stack([full[:, cols * s:cols * (s + 1)] for s in range(N_CHIPS)])


def kernel(x, positions, ln1_g, ln1_b, ffn1_gate, ffn1_up, ffn1_down, w_in, conv_w, conv_b, dt_bias, a_log, d_skip, attn_norm_w, ssd_norm_w, w_out, ln2_g, ln2_b, ffn2_gate, ffn2_up, ffn2_down, ln3_g, ln3_b, loss_target, m_ln1_g, m_ln1_b, m_ffn1_gate, m_ffn1_up, m_ffn1_down, m_w_in, m_conv_w, m_conv_b, m_dt_bias, m_a_log, m_d_skip, m_attn_norm_w, m_ssd_norm_w, m_w_out, m_ln2_g, m_ln2_b, m_ffn2_gate, m_ffn2_up, m_ffn2_down, m_ln3_g, m_ln3_b, v_ln1_g, v_ln1_b, v_ffn1_gate, v_ffn1_up, v_ffn1_down, v_w_in, v_conv_w, v_conv_b, v_dt_bias, v_a_log, v_d_skip, v_attn_norm_w, v_ssd_norm_w, v_w_out, v_ln2_g, v_ln2_b, v_ffn2_gate, v_ffn2_up, v_ffn2_down, v_ln3_g, v_ln3_b):
    given = dict(locals())
    wts = {n: given[n] for n in _WEIGHT_ORDER}
    mom_m = {n: given["m_" + n] for n in _WEIGHT_ORDER}
    mom_v = {n: given["v_" + n] for n in _WEIGHT_ORDER}
    chip = 2 * lax.axis_index("x") + lax.axis_index("y")

    core = lax.axis_index("c")
    groups = [[(n, axis) for n, axis in _MATRICES if n.startswith(prefix)] for prefix in ("ffn1", "w_", "ffn2")]
    own16 = {n: _block2d(wts[n], n).astype(BF16) for n, _ in _MATRICES}

    def full_weights(group, results):
        out = {}
        for (n, axis), st in zip(group, results):
            whole = _assemble(st, axis)
            out[n] = whole.T if n in _TRANSPOSED else whole
        return out

    full = full_weights(groups[0], _run_job(_GatherJob([own16[n] for n, _ in groups[0]]), "gather_ffn1"))
    for n in _VECTORS:
        full[n] = wts[n]
    conv_rows = jnp.pad(wts["conv_w"][0], ((0, 32 - CONV_WIDTH), (0, 0)))

    def mixer_weights(results):
        out = full_weights(groups[1], results)
        out["conv_w"] = _assemble(results[-1], 1)[:CONV_WIDTH]
        return out

    def ffn2_weights(results):
        return full_weights(groups[2], results)

    late = [(_GatherJob([own16[n] for n, _ in groups[1]] + [conv_rows]), mixer_weights),
            (_GatherJob([own16[n] for n, _ in groups[2]]), ffn2_weights)]

    chip_sums = {}

    def core_sums(g, which, tag):
        partials = [_split(g[n], axis) for n, axis in which]
        from_sibling = _sibling_halves([p.astype(BF16) for p in partials], "sibling_halves_" + tag)
        for (n, _), p, o in zip(which, partials, from_sibling):
            chip_sums[n] = _half_sum("core_sum_" + n, p, o, core)
        return _ExchangeJob([chip_sums[n] for n, _ in which])

    last = [(n, axis) for n, axis in _MATRICES if n in ("ffn1_gate", "ffn1_up")]
    early = [(n, axis) for n, axis in _MATRICES if (n, axis) not in last]
    loss, grad_x, g, received_early = _local_step(x, positions, loss_target, full, late,
                                                  lambda g_now: core_sums(g_now, early, "early"))
    received_last = _run_job(core_sums(g, last, "last"), "exchange_last")
    received = dict(zip([n for n, _ in last + early], received_last + received_early))
    half_totals = [_sum_slots("sum_partials_" + n, received[n], chip_sums[n], chip) for n, _ in _MATRICES]
    other_halves = _sibling_swap(half_totals)

    small_shapes = [g[n].shape for n in _VECTORS] + [g["conv_w"].shape, (1,)]
    total = _small_allreduce(_pack_rows([g[n] for n in _VECTORS] + [g["conv_w"], loss[0, :1]]))
    small = _unpack_rows(total, small_shapes)
    loss_out = small[-1].reshape(())

    grads, deltas, new_m, new_v = {}, {}, {}, {}
    for (n, _), mine, theirs in zip(_MATRICES, half_totals, other_halves):
        res = _adamw_halves("adamw_" + n, mine, theirs, core, _block2d(wts[n], n), _block2d(mom_m[n], n), _block2d(mom_v[n], n))
        grads[n], deltas[n], new_m[n], new_v[n] = [_block3d(r, n) for r in res]

    vec_shapes = [wts[n].shape for n in _VECTORS]
    res = _adamw("adamw_vectors", _pack_rows(small[:len(_VECTORS)]), _pack_rows([wts[n] for n in _VECTORS]),
                 _pack_rows([mom_m[n] for n in _VECTORS]), _pack_rows([mom_v[n] for n in _VECTORS]))
    for dst, packed in zip((grads, deltas, new_m, new_v), res):
        for n, val in zip(_VECTORS, _unpack_rows(packed, vec_shapes)):
            dst[n] = val

    cols = conv_w.shape[2]
    g_conv = lax.dynamic_slice_in_dim(small[len(_VECTORS)], chip * cols, cols, axis=1)
    res = _adamw("adamw_conv_w", g_conv, wts["conv_w"][0], mom_m["conv_w"][0], mom_v["conv_w"][0])
    grads["conv_w"], deltas["conv_w"], new_m["conv_w"], new_v["conv_w"] = [r[None] for r in res]

    return (loss_out, grad_x, *[grads[n] for n in _WEIGHT_ORDER], *[deltas[n] for n in _WEIGHT_ORDER],
            *[new_m[n] for n in _WEIGHT_ORDER], *[new_v[n] for n in _WEIGHT_ORDER])
```

```python
import numpy as np
import jax
import jax.numpy as jnp
from jax import lax
from jax.experimental import pallas as pl
from jax.experimental.pallas import tpu as pltpu

F32, BF16 = jnp.float32, jnp.bfloat16

D_MODEL = 1024
D_FF = 2816
N_HEADS = 12
HEAD_DIM = 64
D_ATTN = 768
D_SSD = 768
N_GROUPS = 4
HEADS_PER_GROUP = 3
D_STATE = 128
D_CONV = 1792
CONV_WIDTH = 4
ROPE_DIM = 16
ROPE_THETA = 500000.0
ALPHA = 2.0 ** 0.25
LN_EPS = 1e-5
RMS_EPS = 1e-6
ADAM_LR, ADAM_B1, ADAM_B2, ADAM_EPS, ADAM_WD, ADAM_STEP = 0.001, 0.9, 0.999, 1e-08, 0.01, 10

LANES = 128
GATE_UP_INTERLEAVE = 256
SEQ_BLOCK = 256
GROUP_LANES = 256
VMEM_LIMIT = 56 * 1024 * 1024
NEG = -1e30
MESH = pl.DeviceIdType.MESH
HIGHEST = lax.Precision.HIGHEST

_NT = (((1,), (1,)), ((), ()))
_TN = (((0,), (0,)), ((), ()))


def _params(*sem):
    return pltpu.CompilerParams(dimension_semantics=sem, vmem_limit_bytes=VMEM_LIMIT)


def _bf(v):
    return v.astype(BF16)


EPILOGUE_ROWS = 128


def _row_chunks(tm):
    return [slice(r, min(r + EPILOGUE_ROWS, tm)) for r in range(0, tm, EPILOGUE_ROWS)]


def _sigmoid(v):
    return 0.5 * jnp.tanh(0.5 * v) + 0.5


def _mm(name, pairs, *, scale=1.0, res=None, res_scale=1.0, out_dtype=F32, tm=512, tn=512):
    m, n = pairs[0][0].shape[0], pairs[0][1].shape[1]
    tm, tn = min(tm, m), min(tn, n)
    assert m % tm == 0 and n % tn == 0, (name, m, n, tm, tn)
    npair = len(pairs)

    def body(*refs):
        acc = None
        for a_ref, b_ref in zip(refs[:npair], refs[npair:2 * npair]):
            d = jnp.dot(_bf(a_ref[...]), b_ref[...], preferred_element_type=F32)
            acc = d if acc is None else acc + d
        if scale != 1.0:
            acc = acc * scale
        if res is not None:
            acc = acc + res_scale * refs[2 * npair][...]
        refs[-1][...] = acc.astype(out_dtype)

    in_specs = [pl.BlockSpec((tm, a.shape[1]), lambda i, j: (i, 0)) for a, _ in pairs]
    in_specs += [pl.BlockSpec((b.shape[0], tn), lambda i, j: (0, j)) for _, b in pairs]
    args = [a for a, _ in pairs] + [b for _, b in pairs]
    if res is not None:
        in_specs.append(pl.BlockSpec((tm, tn), lambda i, j: (i, j)))
        args.append(res)
    return pl.pallas_call(
        body, name=name, grid=(m // tm, n // tn), in_specs=in_specs,
        out_specs=pl.BlockSpec((tm, tn), lambda i, j: (i, j)),
        out_shape=jax.ShapeDtypeStruct((m, n), out_dtype),
        compiler_params=_params("parallel", "parallel"),
    )(*args)


def _mm_tn(name, x, dy, *, scale=1.0, tk=512, tn=512, tt=2048):
    t, k = x.shape
    n = dy.shape[1]
    tk, tn, tt = min(tk, k), min(tn, n), min(tt, t)
    assert k % tk == 0 and n % tn == 0 and t % tt == 0, (name, k, n, t)
    nt = t // tt

    def body(x_ref, dy_ref, o_ref):
        step = pl.program_id(2)
        d = lax.dot_general(_bf(x_ref[...]), _bf(dy_ref[...]), _TN, preferred_element_type=F32)

        @pl.when(step == 0)
        def _():
            o_ref[...] = d

        @pl.when(step > 0)
        def _():
            o_ref[...] += d

        if scale != 1.0:
            @pl.when(step == nt - 1)
            def _():
                o_ref[...] = o_ref[...] * scale

    return pl.pallas_call(
        body, name=name, grid=(k // tk, n // tn, nt),
        in_specs=[pl.BlockSpec((tt, tk), lambda i, j, s: (s, i)), pl.BlockSpec((tt, tn), lambda i, j, s: (s, j))],
        out_specs=pl.BlockSpec((tk, tn), lambda i, j, s: (i, j)),
        out_shape=jax.ShapeDtypeStruct((k, n), F32),
        compiler_params=_params("parallel", "parallel", "arbitrary"),
    )(x, dy)


def _mm_tn_sections(name, x, dys, *, tt=512):
    t, k = x.shape
    tt = min(tt, t)
    cuts = np.cumsum([0] + [d.shape[1] for d in dys]).tolist()
    ns = len(dys)

    def body(*refs):
        x_ref, o_ref = refs[0], refs[1 + ns]
        step = pl.program_id(0)
        xt = x_ref[...].T
        parts = [jnp.dot(xt, refs[1 + a][...], preferred_element_type=F32) for a in range(ns)]

        @pl.when(step == 0)
        def _():
            for a in range(ns):
                o_ref[:, cuts[a]:cuts[a + 1]] = parts[a]

        @pl.when(step > 0)
        def _():
            for a in range(ns):
                o_ref[:, cuts[a]:cuts[a + 1]] += parts[a]

    return pl.pallas_call(
        body, name=name, grid=(t // tt,),
        in_specs=[pl.BlockSpec((tt, k), lambda s: (s, 0))] + [pl.BlockSpec((tt, d.shape[1]), lambda s: (s, 0)) for d in dys],
        out_specs=pl.BlockSpec((k, cuts[-1]), lambda s: (0, 0)),
        out_shape=jax.ShapeDtypeStruct((k, cuts[-1]), F32),
        compiler_params=_params("arbitrary"),
    )(x, *dys)


def _mm_tn_gate_up(name, x, dau, *, tt=2048):
    t, k = x.shape
    gi = GATE_UP_INTERLEAVE
    nj = dau.shape[1] // (2 * gi)
    tt = min(tt, t)
    nt = t // tt

    def body(x_ref, dy_ref, g_ref, u_ref):
        step = pl.program_id(1)
        d = lax.dot_general(dy_ref[...], _bf(x_ref[...]), _TN, preferred_element_type=F32)

        @pl.when(step == 0)
        def _():
            g_ref[...] = d[:gi]
            u_ref[...] = d[gi:]

        @pl.when(step > 0)
        def _():
            g_ref[...] += d[:gi]
            u_ref[...] += d[gi:]

    out = pl.BlockSpec((gi, k), lambda j, s: (j, 0))
    return pl.pallas_call(
        body, name=name, grid=(nj, nt),
        in_specs=[pl.BlockSpec((tt, k), lambda j, s: (s, 0)), pl.BlockSpec((tt, 2 * gi), lambda j, s: (s, j))],
        out_specs=[out, out],
        out_shape=[jax.ShapeDtypeStruct((gi * nj, k), F32)] * 2,
        compiler_params=_params("parallel", "arbitrary"),
    )(x, dau)


def _carried(carry, ins, outs, sems, step, total):
    start, forward, finish = carry.phases(ins, outs, sems)
    pl.when(step == 0)(start)
    if forward is not None:
        pl.when(step == (3 * total) // 4)(forward)
    return lambda: pl.when(step == total - 1)(finish)


def _resident(shape):
    return pl.BlockSpec(shape, lambda i: (0,) * len(shape), pipeline_mode=pl.Buffered(1))


def _ffn_fwd(name, x16, res, wgu, wd, g, b, *, tm=512, carry=None):
    t, k = x16.shape
    gi = GATE_UP_INTERLEAVE
    nj, n, ni = wd.shape[0] // gi, wd.shape[1], t // tm
    nc = carry.n if carry is not None else 0

    def body(*refs):
        x_ref, res_ref, wgu_ref, wd_ref, g_ref, b_ref = refs[:6]
        au_ref, hm_ref, y_ref, r_ref, y16_ref = refs[6 + nc:11 + nc]
        if carry is not None:
            finish = _carried(carry, refs[6:6 + nc], refs[11 + nc:11 + 2 * nc], refs[11 + 2 * nc:], pl.program_id(0), ni)
        xv = x_ref[...]
        acc = jnp.zeros((tm, n), F32)
        for j in range(nj):
            au = jnp.dot(xv, wgu_ref[:, 2 * gi * j:2 * gi * (j + 1)], preferred_element_type=F32)
            a, u = au[:, :gi], au[:, gi:]
            au_ref[:, 2 * gi * j:2 * gi * (j + 1)] = _bf(au)
            hm = _bf(a * _sigmoid(a) * u)
            hm_ref[:, gi * j:gi * (j + 1)] = hm
            acc = acc + jnp.dot(hm, wd_ref[gi * j:gi * (j + 1), :], preferred_element_type=F32)
        r = ALPHA * res_ref[...] + 0.5 * acc
        r_ref[...] = r
        y = _layer_norm(r, g_ref[...], b_ref[...])
        y_ref[...] = y
        y16_ref[...] = _bf(y)
        if carry is not None:
            finish()

    row = lambda c: pl.BlockSpec((tm, c), lambda i: (i, 0))
    hbm = pl.BlockSpec(memory_space=pltpu.HBM)
    res_ = pl.pallas_call(
        body, name=name, grid=(ni,),
        in_specs=[row(k), row(n), _resident(wgu.shape), _resident(wd.shape), _resident(g.shape), _resident(b.shape)] + [hbm] * nc,
        out_specs=[row(2 * gi * nj), row(gi * nj), row(n), row(n), row(n)] + [hbm] * nc,
        out_shape=[jax.ShapeDtypeStruct((t, 2 * gi * nj), BF16), jax.ShapeDtypeStruct((t, gi * nj), BF16),
                   jax.ShapeDtypeStruct((t, n), F32), jax.ShapeDtypeStruct((t, n), F32), jax.ShapeDtypeStruct((t, n), BF16)]
        + (carry.out_shape if carry is not None else []),
        scratch_shapes=carry.scratch_shapes if carry is not None else [],
        compiler_params=_params("arbitrary" if carry is not None else "parallel"),
    )(x16, res, wgu, wd, g, b, *(carry.operands if carry is not None else []))
    return tuple(res_[:5]) + ((carry.results(res_[5:]),) if carry is not None else ())


def _ffn_bwd(name, dr16, dr, wdt, au, wgut, *, tm=512, carry=None):
    t, n = dr16.shape
    gi = GATE_UP_INTERLEAVE
    nj, ni = wdt.shape[1] // gi, t // tm
    nc = carry.n if carry is not None else 0

    def body(*refs):
        dr16_ref, dr_ref, wdt_ref, au_ref, wgut_ref = refs[:5]
        dau_ref, dx_ref = refs[5 + nc:7 + nc]
        if carry is not None:
            finish = _carried(carry, refs[5:5 + nc], refs[7 + nc:7 + 2 * nc], refs[7 + 2 * nc:], pl.program_id(0), ni)
        drv = dr16_ref[...]
        acc = jnp.zeros((tm, n), F32)
        for j in range(nj):
            dhm = jnp.dot(drv, wdt_ref[:, gi * j:gi * (j + 1)], preferred_element_type=F32) * 0.5
            au_v = au_ref[:, 2 * gi * j:2 * gi * (j + 1)].astype(F32)
            a, u = au_v[:, :gi], au_v[:, gi:]
            sig = _sigmoid(a)
            silu = a * sig
            dau = jnp.concatenate([_bf(dhm * u * (sig + silu - silu * sig)), _bf(dhm * silu)], axis=1)
            dau_ref[:, 2 * gi * j:2 * gi * (j + 1)] = dau
            acc = acc + jnp.dot(dau, wgut_ref[2 * gi * j:2 * gi * (j + 1), :], preferred_element_type=F32)
        dx_ref[...] = ALPHA * dr_ref[...] + acc
        if carry is not None:
            finish()

    row = lambda c: pl.BlockSpec((tm, c), lambda i: (i, 0))
    hbm = pl.BlockSpec(memory_space=pltpu.HBM)
    res_ = pl.pallas_call(
        body, name=name, grid=(ni,),
        in_specs=[row(n), row(n), _resident(wdt.shape), row(2 * gi * nj), _resident(wgut.shape)] + [hbm] * nc,
        out_specs=[row(2 * gi * nj), row(n)] + [hbm] * nc,
        out_shape=[jax.ShapeDtypeStruct((t, 2 * gi * nj), BF16), jax.ShapeDtypeStruct((t, n), F32)]
        + (carry.out_shape if carry is not None else []),
        scratch_shapes=carry.scratch_shapes if carry is not None else [],
        compiler_params=_params("arbitrary" if carry is not None else "parallel"),
    )(dr16, dr, wdt, au, wgut, *(carry.operands if carry is not None else []))
    return tuple(res_[:2]) + ((carry.results(res_[2:]),) if carry is not None else ())


def _layer_norm(r, g, b):
    mu = jnp.mean(r, axis=-1, keepdims=True)
    var = jnp.mean(jnp.square(r - mu), axis=-1, keepdims=True)
    return (r - mu) * lax.rsqrt(var + LN_EPS) * g + b


def _mm_res_ln(name, a, w, res, g, b, *, scale, tm=512):
    t, k = a.shape
    n = w.shape[1]

    def body(a_ref, w_ref, res_ref, g_ref, b_ref, y_ref, r_ref, y16_ref):
        for rows in _row_chunks(tm):
            r = ALPHA * res_ref[rows, :] + scale * jnp.dot(_bf(a_ref[rows, :]), w_ref[...], preferred_element_type=F32)
            r_ref[rows, :] = r
            y = _layer_norm(r, g_ref[...], b_ref[...])
            y_ref[rows, :] = y
            y16_ref[rows, :] = _bf(y)

    row = lambda c: pl.BlockSpec((tm, c), lambda i: (i, 0))
    const = lambda shape: pl.BlockSpec(shape, lambda i: (0, 0))
    return pl.pallas_call(
        body, name=name, grid=(t // tm,),
        in_specs=[row(k), const((k, n)), row(n), const((1, n)), const((1, n))],
        out_specs=[row(n), row(n), row(n)],
        out_shape=[jax.ShapeDtypeStruct((t, n), F32), jax.ShapeDtypeStruct((t, n), F32), jax.ShapeDtypeStruct((t, n), BF16)],
        compiler_params=_params("parallel"),
    )(a, w, res, g, b)


def _rowwise(name, fn, rows, consts, row_outs, acc_outs=(), tm=512):
    rows = [r if isinstance(r, tuple) else (r, r.shape[1]) for r in rows]
    t = rows[0][0].shape[0]
    tm = min(tm, t)
    assert t % tm == 0
    nr, nc, no, na = len(rows), len(consts), len(row_outs), len(acc_outs)

    def body(*refs):
        vals = [r[...] for r in refs[:nr + nc]]
        outs, accs = fn(*vals)
        for o_ref, o in zip(refs[nr + nc:nr + nc + no], outs):
            o_ref[...] = o.astype(o_ref.dtype)
        if na:
            step = pl.program_id(0)
            acc_refs = refs[nr + nc + no:]

            @pl.when(step == 0)
            def _():
                for a_ref, a in zip(acc_refs, accs):
                    a_ref[...] = a

            @pl.when(step > 0)
            def _():
                for a_ref, a in zip(acc_refs, accs):
                    a_ref[...] += a

    in_specs = [pl.BlockSpec((tm, w), lambda i: (i, 0)) for _, w in rows]
    in_specs += [pl.BlockSpec(c.shape, lambda i, nd=c.ndim: (0,) * nd) for c in consts]
    out_specs = [pl.BlockSpec((tm, c), lambda i: (i, 0)) for c, _ in row_outs]
    out_specs += [pl.BlockSpec(s, lambda i: (0, 0)) for s in acc_outs]
    out_shape = [jax.ShapeDtypeStruct((t, c), dt) for c, dt in row_outs]
    out_shape += [jax.ShapeDtypeStruct(s, F32) for s in acc_outs]
    res = pl.pallas_call(
        body, name=name, grid=(t // tm,), in_specs=in_specs, out_specs=out_specs, out_shape=out_shape,
        compiler_params=_params("arbitrary" if na else "parallel"),
    )(*[r for r, _ in rows], *consts)
    return res


def _ln_bwd(name, r, g, b, dy):
    def fn(r_v, dy_v, g_v, b_v):
        _, vjp = jax.vjp(_layer_norm, r_v, g_v, b_v)
        dr, dg, db = vjp(dy_v)
        return [dr, dr], [dg, db]
    return _rowwise(name, fn, [r, dy], [g, b], [(r.shape[1], F32), (r.shape[1], BF16)], [(1, r.shape[1])] * 2)


def _ln_loss_bwd(name, r, g, b, target):
    def fn(r_v, t_v, g_v, b_v):
        def loss_fn(rr, gg, bb):
            err = jnp.square(_layer_norm(rr, gg, bb) - t_v)
            return 0.5 * jnp.sum(jnp.mean(err, axis=-1, keepdims=True), axis=0, keepdims=True)
        loss, vjp = jax.vjp(loss_fn, r_v, g_v, b_v)
        dr, dg, db = vjp(jnp.ones((1, 1), F32))
        return [dr, dr], [dg, db, jnp.broadcast_to(loss, (1, LANES))]
    return _rowwise(name, fn, [r, target], [g, b], [(r.shape[1], F32), (r.shape[1], BF16)],
                    [(1, r.shape[1])] * 2 + [(1, LANES)])


def _rope_tables(posf, invf, sgn):
    ang = posf * invf
    return jnp.cos(ang), jnp.sin(ang) * sgn


def _rope_apply(tv, cos, sin):
    lane = lax.broadcasted_iota(jnp.int32, cos.shape, 1)
    first = (lane % HEAD_DIM) < (ROPE_DIM // 2)
    outs = []
    for gidx in range(tv.shape[1] // LANES):
        tg = tv[:, LANES * gidx:LANES * (gidx + 1)]
        sw = jnp.where(first, pltpu.roll(tg, LANES - ROPE_DIM // 2, 1), pltpu.roll(tg, ROPE_DIM // 2, 1))
        outs.append(tg * cos + sw * sin)
    return jnp.concatenate(outs, axis=1)


def _proj_in(h16, w_in, posf, invf, sgn, *, tm=512, carry=None):
    t, k = h16.shape
    cuts = [0, D_ATTN, 2 * D_ATTN, 3 * D_ATTN, 3 * D_ATTN + D_SSD, 3 * D_ATTN + D_SSD + D_CONV, w_in.shape[1]]
    nc = carry.n if carry is not None else 0

    def body(*refs):
        h_ref, w_ref, pos_ref, invf_ref, sgn_ref = refs[:5]
        q_ref, k_ref, v_ref, z_ref, xbc_ref, dt_ref, cs_ref = refs[5 + nc:12 + nc]
        if carry is not None:
            finish = _carried(carry, refs[5:5 + nc], refs[12 + nc:12 + 2 * nc], refs[12 + 2 * nc:], pl.program_id(0), t // tm)
        hv = h_ref[...]
        part = lambda a: jnp.dot(hv, w_ref[:, cuts[a]:cuts[a + 1]], preferred_element_type=F32)
        cos, sin = _rope_tables(pos_ref[...], invf_ref[...], sgn_ref[...])
        cs_ref[...] = jnp.concatenate([cos, sin], axis=1)
        q_ref[...] = _bf(_rope_apply(part(0), cos, sin) * (HEAD_DIM ** -0.5))
        k_ref[...] = _bf(_rope_apply(part(1), cos, sin))
        v_ref[...] = _bf(part(2))
        z_ref[...] = part(3)
        xbc_ref[...] = part(4)
        dt_ref[...] = part(5)
        if carry is not None:
            finish()

    row = lambda c: pl.BlockSpec((tm, c), lambda i: (i, 0))
    hbm = pl.BlockSpec(memory_space=pltpu.HBM)
    widths = [D_ATTN, D_ATTN, D_ATTN, D_SSD, D_CONV, LANES, 2 * LANES]
    dtypes = [BF16, BF16, BF16, F32, F32, F32, F32]
    res = pl.pallas_call(
        body, name="proj_in", grid=(t // tm,),
        in_specs=[row(k), _resident(w_in.shape), row(1), _resident(invf.shape), _resident(sgn.shape)] + [hbm] * nc,
        out_specs=[row(c) for c in widths] + [hbm] * nc,
        out_shape=[jax.ShapeDtypeStruct((t, c), dt) for c, dt in zip(widths, dtypes)]
        + (carry.out_shape if carry is not None else []),
        scratch_shapes=carry.scratch_shapes if carry is not None else [],
        compiler_params=_params("arbitrary" if carry is not None else "parallel"),
    )(h16, w_in, posf, invf, sgn, *(carry.operands if carry is not None else []))
    return tuple(res[:7]) + ((carry.results(res[7:]),) if carry is not None else ())


def _rope_bwd(dq, dk, cs):
    def fn(dq_v, dk_v, cs_v):
        cos, sin = cs_v[:, :LANES], -cs_v[:, LANES:]
        gq = _rope_apply(dq_v * (HEAD_DIM ** -0.5), cos, sin)
        gk = _rope_apply(dk_v, cos, sin)
        return [jnp.concatenate([gq, gk], axis=1)], []
    return _rowwise("rope_bwd", fn, [dq, dk, cs], [], [(2 * D_ATTN, BF16)])[0]


def _rms(v, w):
    return v * lax.rsqrt(jnp.mean(v * v, axis=-1, keepdims=True) + RMS_EPS) * w


def _ungroup(yg):
    w = HEADS_PER_GROUP * HEAD_DIM
    return jnp.concatenate([yg[:, GROUP_LANES * g:GROUP_LANES * g + w] for g in range(N_GROUPS)], axis=1)


def _group(xs):
    w = HEADS_PER_GROUP * HEAD_DIM
    parts = []
    for g in range(N_GROUPS):
        parts += [xs[:, w * g:w * (g + 1)], jnp.zeros((xs.shape[0], GROUP_LANES - w), xs.dtype)]
    return jnp.concatenate(parts, axis=1)


def _norms_fn(attn, yg, xs, z, w_attn, w_ssd, dskip):
    a_n = _rms(attn, w_attn)
    y = _ungroup(yg) + dskip * xs
    y_n = _rms(y * (z * _sigmoid(z)), w_ssd)
    return jnp.concatenate([a_n, y_n], axis=1)


def _norms_fwd(attn, yg, xbc, z, w_attn, w_ssd, dskip):
    def fn(*v):
        return [_norms_fn(*v)], []
    return _rowwise("norms_fwd", fn, [attn, yg, (xbc, D_SSD), z], [w_attn, w_ssd, dskip], [(D_ATTN + D_SSD, BF16)])[0]


def _norms_bwd(attn, yg, xbc, z, w_attn, w_ssd, dskip, dcat):
    def fn(attn_v, yg_v, xs_v, z_v, dcat_v, wa_v, ws_v, dk_v):
        _, vjp = jax.vjp(_norms_fn, attn_v, yg_v, xs_v, z_v, wa_v, ws_v, dk_v)
        d_attn, d_yg, d_xs, d_z, d_wa, d_ws, d_dk = vjp(dcat_v)
        return [d_attn, d_yg, d_xs, d_z], [d_wa, d_ws, d_dk]
    return _rowwise("norms_bwd", fn, [attn, yg, (xbc, D_SSD), z, dcat], [w_attn, w_ssd, dskip],
                    [(D_ATTN, F32), (N_GROUPS * GROUP_LANES, F32), (D_SSD, F32), (D_SSD, BF16)], [(1, D_SSD)] * 3)


def _spread_sum(v, e):
    h1 = _bf(v)
    r1 = v - h1.astype(F32)
    h2 = _bf(r1)
    h3 = _bf(r1 - h2.astype(F32))
    return sum(jnp.dot(h, e, preferred_element_type=F32) for h in (h1, h2, h3))


@jax.custom_vjp
def _spread(v, e, e_t):
    return _spread_sum(v, e)


def _spread_fwd(v, e, e_t):
    return _spread_sum(v, e), (e, e_t)


def _spread_bwd(saved, g):
    e, e_t = saved
    return _spread_sum(g, e_t), jnp.zeros_like(e), jnp.zeros_like(e_t)


_spread.defvjp(_spread_fwd, _spread_bwd)


def _ssd_prep_fn(xs, dtp, dtb, alog, e_x, e_xt, e_a, e_at):
    dt = jax.nn.softplus(dtp + dtb)
    a = -jnp.exp(alog)
    xdtg = _group(xs) * _spread(dt, e_x, e_xt)
    dag = _spread(dt * a, e_a, e_at)
    return xdtg, dag


def _ssd_prep_fwd(xbc, dtp, dtb, alog, spreaders):
    def fn(xbc_v, dtp_v, dtb_v, alog_v, *e_v):
        xdtg, dag = _ssd_prep_fn(xbc_v[:, :D_SSD], dtp_v, dtb_v, alog_v, *e_v)
        return [xdtg, xbc_v[:, D_SSD:], dag], []
    return _rowwise("ssd_prep_fwd", fn, [xbc, dtp], [dtb, alog, *spreaders],
                    [(N_GROUPS * GROUP_LANES, BF16), (D_CONV - D_SSD, BF16), (N_GROUPS * LANES, F32)])


def _ssd_prep_bwd(xbc, dtp, dtb, alog, spreaders, dxdtg, ddag, dxs_a, db, dc):
    def fn(xs_v, dtp_v, dxdtg_v, ddag_v, dxs_a_v, db_v, dc_v, dtb_v, alog_v, *e_v):
        _, vjp = jax.vjp(lambda a, b, c, d: _ssd_prep_fn(a, b, c, d, *e_v), xs_v, dtp_v, dtb_v, alog_v)
        dxs, ddtp, ddtb, dalog = vjp((dxdtg_v, ddag_v))
        return [jnp.concatenate([dxs + dxs_a_v, db_v, dc_v], axis=1), ddtp], [ddtb, dalog]
    return _rowwise("ssd_prep_bwd", fn, [(xbc, D_SSD), dtp, dxdtg, ddag, dxs_a, db, dc], [dtb, alog, *spreaders],
                    [(D_CONV, F32), (LANES, BF16)], [(1, LANES)] * 2)


def _shift_down(u, d):
    if d == 0:
        return u
    row = lax.broadcasted_iota(jnp.int32, u.shape, 0)
    return jnp.where(row >= d, pltpu.roll(u, d, 0), 0.0)


def _shift_up(u, d):
    if d == 0:
        return u
    s = u.shape[0]
    row = lax.broadcasted_iota(jnp.int32, u.shape, 0)
    return jnp.where(row < s - d, pltpu.roll(u, s - d, 0), 0.0)


def _conv_pre(u, w, b):
    acc = b
    for k in range(CONV_WIDTH):
        acc = acc + w[k:k + 1, :] * _shift_down(u, CONV_WIDTH - 1 - k)
    return acc


def _conv_fwd(u, w, b, *, tc=256):
    nb, s, c = u.shape

    def body(u_ref, w_ref, b_ref, o_ref):
        pre = _conv_pre(u_ref[0], w_ref[...], b_ref[...])
        o_ref[0] = pre * _sigmoid(pre)

    return pl.pallas_call(
        body, name="conv_fwd", grid=(c // tc, nb),
        in_specs=[pl.BlockSpec((1, s, tc), lambda j, i: (i, 0, j)), pl.BlockSpec((CONV_WIDTH, tc), lambda j, i: (0, j)),
                  pl.BlockSpec((1, tc), lambda j, i: (0, j))],
        out_specs=pl.BlockSpec((1, s, tc), lambda j, i: (i, 0, j)),
        out_shape=jax.ShapeDtypeStruct((nb, s, c), F32),
        compiler_params=_params("parallel", "parallel"),
    )(u, w, b)


def _conv_bwd(u, w, b, dout, *, tc=256):
    nb, s, c = u.shape

    def body(u_ref, w_ref, b_ref, d_ref, du_ref, dw_ref, db_ref):
        uv, wv = u_ref[0], w_ref[...]
        pre = _conv_pre(uv, wv, b_ref[...])
        sig = _sigmoid(pre)
        dpre = d_ref[0] * (sig * (1.0 + pre * (1.0 - sig)))
        du = jnp.zeros_like(uv)
        dws = []
        for k in range(CONV_WIDTH):
            du = du + wv[k:k + 1, :] * _shift_up(dpre, CONV_WIDTH - 1 - k)
            dws.append(jnp.sum(dpre * _shift_down(uv, CONV_WIDTH - 1 - k), axis=0, keepdims=True))
        du_ref[0] = _bf(du)
        dwv = jnp.concatenate(dws + [jnp.zeros((8 - CONV_WIDTH, tc), F32)], axis=0)
        dbv = jnp.sum(dpre, axis=0, keepdims=True)
        first = pl.program_id(1) == 0

        @pl.when(first)
        def _():
            dw_ref[...] = dwv
            db_ref[...] = dbv

        @pl.when(jnp.logical_not(first))
        def _():
            dw_ref[...] += dwv
            db_ref[...] += dbv

    blk = pl.BlockSpec((1, s, tc), lambda j, i: (i, 0, j))
    return pl.pallas_call(
        body, name="conv_bwd", grid=(c // tc, nb),
        in_specs=[blk, pl.BlockSpec((CONV_WIDTH, tc), lambda j, i: (0, j)), pl.BlockSpec((1, tc), lambda j, i: (0, j)), blk],
        out_specs=[blk, pl.BlockSpec((8, tc), lambda j, i: (0, j)), pl.BlockSpec((1, tc), lambda j, i: (0, j))],
        out_shape=[jax.ShapeDtypeStruct((nb, s, c), BF16), jax.ShapeDtypeStruct((8, c), F32), jax.ShapeDtypeStruct((1, c), F32)],
        compiler_params=_params("parallel", "arbitrary"),
    )(u, w, b, dout)


FWD_KEY_BLOCK = 256


def _branch_bias_table(seq, kb):
    ratio = SEQ_BLOCK // kb
    key = np.arange(kb)[None, :, None]
    query = np.arange(SEQ_BLOCK)[None, None, :]
    delta = (np.arange(seq // kb)[:, None, None] - (ratio - 1)) * kb + query - key
    cnt = np.zeros(delta.shape, np.float64)
    for window, dilation in ((128, 1), (512, 4), (2048, 16)):
        cnt += (delta >= 0) & (delta % dilation == 0) & (delta <= window)
    return jnp.asarray(np.where(cnt > 0, np.log(np.maximum(cnt, 1.0)), NEG).astype(np.float32))


HEADS_PER_BLOCK = LANES // HEAD_DIM


def _head_rows(v, h):
    row = lax.broadcasted_iota(jnp.int32, v.shape, 0)
    return jnp.where((row >= HEAD_DIM * h) & (row < HEAD_DIM * (h + 1)), v, jnp.zeros_like(v))


def _attn_fwd(q, k, v, bias):
    nb_, s, _ = q.shape
    ab, kb = SEQ_BLOCK, FWD_KEY_BLOCK
    nblk, nkb, ratio = s // ab, s // kb, ab // kb

    def body(q_ref, k_ref, v_ref, b_ref, o_ref, lse_ref, vt_s):
        i = pl.program_id(2)

        @pl.when(i == 0)
        def _():
            for jb in range(nkb):
                vt_s[jb] = v_ref[0, kb * jb:kb * (jb + 1), :].T

        qt = q_ref[0].T
        qts = [_head_rows(qt, h) for h in range(HEADS_PER_BLOCK)]

        last = ratio * (i + 1) - 1

        def scores(j):
            kj = k_ref[0, pl.ds(pl.multiple_of(j * kb, kb), kb), :]
            return [jnp.dot(kj, qts[h], preferred_element_type=F32) for h in range(HEADS_PER_BLOCK)]

        def step(j, carry):
            ahead = scores(jnp.minimum(j + 1, last))
            lb = b_ref[ratio * i - j + (ratio - 1)]
            out = []
            for h in range(HEADS_PER_BLOCK):
                m, l, acc = carry[3 * h:3 * h + 3]
                st = carry[3 * HEADS_PER_BLOCK + h] + lb
                m_new = jnp.maximum(m, jnp.max(st, axis=0, keepdims=True))
                p = jnp.exp(st - m_new)
                a = jnp.exp(m - m_new)
                l = a * l + jnp.sum(p, axis=0, keepdims=True)
                vt = vt_s[j, HEAD_DIM * h:HEAD_DIM * (h + 1), :]
                acc = a * acc + jnp.dot(vt, _bf(p), preferred_element_type=F32)
                out += [m_new, l, acc]
            return tuple(out) + tuple(ahead)

        init = (jnp.full((1, ab), NEG, F32), jnp.zeros((1, ab), F32), jnp.zeros((HEAD_DIM, ab), F32)) * HEADS_PER_BLOCK
        res = lax.fori_loop(0, ratio * (i + 1), step, init + tuple(scores(0)))
        ot = jnp.concatenate([res[3 * h + 2] / res[3 * h + 1] for h in range(HEADS_PER_BLOCK)], axis=0)
        o_ref[0] = ot.T
        rows = [res[3 * h] + jnp.log(res[3 * h + 1]) for h in range(HEADS_PER_BLOCK)]
        lse_ref[0, 0, 0] = jnp.concatenate(rows + [jnp.zeros((8 - HEADS_PER_BLOCK, ab), F32)], axis=0)

    qblk = pl.BlockSpec((1, ab, LANES), lambda b, hp, i: (b, i, hp))
    full = pl.BlockSpec((1, s, LANES), lambda b, hp, i: (b, 0, hp))
    return pl.pallas_call(
        body, name="attn_fwd", grid=(nb_, D_ATTN // LANES, nblk),
        in_specs=[qblk, full, full, pl.BlockSpec((nkb, kb, ab), lambda b, hp, i: (0, 0, 0))],
        out_specs=[qblk, pl.BlockSpec((1, 1, 1, 8, ab), lambda b, hp, i: (b, hp, i, 0, 0))],
        out_shape=[jax.ShapeDtypeStruct((nb_, s, D_ATTN), F32),
                   jax.ShapeDtypeStruct((nb_, D_ATTN // LANES, nblk, 8, ab), F32)],
        scratch_shapes=[pltpu.VMEM((nkb, LANES, kb), BF16)],
        compiler_params=_params("parallel", "parallel", "arbitrary"),
    )(q, k, v, bias)


def _attn_bwd(q, k, v, o, do, lse, bias):
    nb_, s, _ = q.shape
    ab = SEQ_BLOCK
    nblk = s // ab

    nh = HEADS_PER_BLOCK

    def body(q_ref, k_ref, v_ref, o_ref, do_ref, lse_ref, b_ref, dq_ref, dk_ref, dv_ref,
             qt_s, dot_s, kt_s, dqt_s, do16_s, d_s, dk_acc, dv_acc):
        for jb in range(nblk):
            sl = slice(ab * jb, ab * (jb + 1))
            qt, kt = q_ref[0, sl, :].T, k_ref[0, sl, :].T
            do = do_ref[0, sl, :]
            dot = do.T
            prod = dot * o_ref[0, sl, :].T
            do16_s[sl, :] = _bf(do)
            for h in range(nh):
                qt_s[nh * jb + h] = _head_rows(qt, h)
                kt_s[nh * jb + h] = _head_rows(kt, h)
                dot_s[nh * jb + h] = _head_rows(_bf(dot), h)
            d_s[jb] = jnp.concatenate(
                [jnp.sum(prod[HEAD_DIM * h:HEAD_DIM * (h + 1)], axis=0, keepdims=True) for h in range(nh)]
                + [jnp.zeros((8 - nh, ab), F32)], axis=0)
            dqt_s[jb] = jnp.zeros((LANES, ab), F32)

        def outer(j, carry):
            ks = pl.ds(pl.multiple_of(j * ab, ab), ab)
            kj, vj = k_ref[0, ks, :], v_ref[0, ks, :]
            dk_acc[...] = jnp.zeros_like(dk_acc)
            dv_acc[...] = jnp.zeros_like(dv_acc)

            def inner(i, c2):
                qs = pl.ds(pl.multiple_of(i * ab, ab), ab)
                qi, doi = q_ref[0, qs, :], do16_s[qs, :]
                lb = b_ref[i - j]
                for h in range(nh):
                    st = jnp.dot(kj, qt_s[nh * i + h], preferred_element_type=F32) + lb
                    pt = jnp.exp(st - lse_ref[0, 0, i, h:h + 1, :])
                    dpt = jnp.dot(vj, dot_s[nh * i + h], preferred_element_type=F32)
                    dst16 = _bf(pt * (dpt - d_s[i, h:h + 1, :]))
                    dv_acc[h] += jnp.dot(_bf(pt), doi, preferred_element_type=F32)
                    dk_acc[h] += jnp.dot(dst16, qi, preferred_element_type=F32)
                    dqt_s[i] += jnp.dot(kt_s[nh * j + h], dst16, preferred_element_type=F32)
                return c2

            lax.fori_loop(j, nblk, inner, 0)
            lane = lax.broadcasted_iota(jnp.int32, (ab, LANES), 1)
            dk_ref[0, ks, :] = jnp.where(lane < HEAD_DIM, dk_acc[0], dk_acc[1])
            dv_ref[0, ks, :] = _bf(jnp.where(lane < HEAD_DIM, dv_acc[0], dv_acc[1]))
            return carry

        lax.fori_loop(0, nblk, outer, 0)
        for jb in range(nblk):
            dq_ref[0, ab * jb:ab * (jb + 1), :] = dqt_s[jb].T

    assert nh == 2
    full = pl.BlockSpec((1, s, LANES), lambda b, hp: (b, 0, hp))
    return pl.pallas_call(
        body, name="attn_bwd", grid=(nb_, D_ATTN // LANES),
        in_specs=[full] * 5 + [pl.BlockSpec((1, 1, nblk, 8, ab), lambda b, hp: (b, hp, 0, 0, 0)),
                               pl.BlockSpec((nblk, ab, ab), lambda b, hp: (0, 0, 0))],
        out_specs=[full, full, full],
        out_shape=[jax.ShapeDtypeStruct((nb_, s, D_ATTN), F32), jax.ShapeDtypeStruct((nb_, s, D_ATTN), F32),
                   jax.ShapeDtypeStruct((nb_, s, D_ATTN), BF16)],
        scratch_shapes=[pltpu.VMEM((nh * nblk, LANES, ab), BF16), pltpu.VMEM((nh * nblk, LANES, ab), BF16),
                        pltpu.VMEM((nh * nblk, LANES, ab), BF16), pltpu.VMEM((nblk, LANES, ab), F32),
                        pltpu.VMEM((s, LANES), BF16), pltpu.VMEM((nblk, 8, ab), F32),
                        pltpu.VMEM((nh, ab, LANES), F32), pltpu.VMEM((nh, ab, LANES), F32)],
        compiler_params=_params("parallel", "parallel"),
    )(q, k, v, o, do, lse, bias)


def _cumsum_fwd(dag):
    nb_, s, c = dag.shape
    ab = SEQ_BLOCK

    def body(a_ref, o_ref, ot_ref):
        r = lax.broadcasted_iota(jnp.int32, (ab, ab), 0)
        cc = lax.broadcasted_iota(jnp.int32, (ab, ab), 1)
        tri = (r >= cc).astype(F32)
        carry = jnp.zeros((1, c), F32)
        for i in range(s // ab):
            loc = jnp.dot(tri, a_ref[0, ab * i:ab * (i + 1), :], precision=HIGHEST, preferred_element_type=F32) + carry
            o_ref[0, ab * i:ab * (i + 1), :] = loc
            ot_ref[0, :, ab * i:ab * (i + 1)] = loc.T
            carry = loc[ab - 1:ab, :]

    return pl.pallas_call(
        body, name="ssd_cumsum", grid=(nb_,),
        in_specs=[pl.BlockSpec((1, s, c), lambda b: (b, 0, 0))],
        out_specs=[pl.BlockSpec((1, s, c), lambda b: (b, 0, 0)), pl.BlockSpec((1, c, s), lambda b: (b, 0, 0))],
        out_shape=[jax.ShapeDtypeStruct((nb_, s, c), F32), jax.ShapeDtypeStruct((nb_, c, s), F32)],
        compiler_params=_params("parallel"),
    )(dag)


def _cumsum_bwd(dcol, drow):
    nb_, s, c = dcol.shape
    ab = SEQ_BLOCK

    def body(c_ref, r_ref, o_ref):
        r = lax.broadcasted_iota(jnp.int32, (ab, ab), 0)
        cc = lax.broadcasted_iota(jnp.int32, (ab, ab), 1)
        tri = (r <= cc).astype(F32)
        carry = jnp.zeros((1, c), F32)
        for i in reversed(range(s // ab)):
            rows = r_ref[0, :, ab * i:ab * (i + 1)].T
            parts = []
            for g in range(N_GROUPS):
                parts += [rows[:, 8 * g:8 * (g + 1)], jnp.zeros((ab, LANES - 8), F32)]
            blk = c_ref[0, ab * i:ab * (i + 1), :] + jnp.concatenate(parts, axis=1)
            loc = jnp.dot(tri, blk, precision=HIGHEST, preferred_element_type=F32) + carry
            o_ref[0, ab * i:ab * (i + 1), :] = loc
            carry = loc[0:1, :]

    return pl.pallas_call(
        body, name="ssd_cumsum_bwd", grid=(nb_,),
        in_specs=[pl.BlockSpec((1, s, c), lambda b: (b, 0, 0)), pl.BlockSpec((1, N_GROUPS * 8, s), lambda b: (b, 0, 0))],
        out_specs=pl.BlockSpec((1, s, c), lambda b: (b, 0, 0)),
        out_shape=jax.ShapeDtypeStruct((nb_, s, c), F32),
        compiler_params=_params("parallel"),
    )(dcol, drow)


def _causal_ok(i, j):
    ab = SEQ_BLOCK
    r = lax.broadcasted_iota(jnp.int32, (ab, ab), 0)
    c = lax.broadcasted_iota(jnp.int32, (ab, ab), 1)
    return (r + (i - j) * ab) >= c


def _causal_ok_t(i, j):
    ab = SEQ_BLOCK
    r = lax.broadcasted_iota(jnp.int32, (ab, ab), 0)
    c = lax.broadcasted_iota(jnp.int32, (ab, ab), 1)
    return (c + (i - j) * ab) >= r


def _ssd_chunk(s_in, x, bm_t, cm, cb, acol, arow, a_prev, ok):
    q = x.shape[0]
    decay = jnp.exp(jnp.where(ok, acol - arow, NEG))
    y = jnp.dot(_bf(cb * decay), x, preferred_element_type=F32)
    y = y + jnp.exp(acol - a_prev) * jnp.dot(cm, _bf(s_in), preferred_element_type=F32)
    a_end = acol[q - 1:q, :]
    wx = _bf(jnp.exp(a_end - acol) * x.astype(F32))
    s_out = jnp.exp(a_end - a_prev) * s_in + jnp.dot(bm_t, wx, preferred_element_type=F32)
    return y, s_out


def _ssd_specs(s):
    xblk = pl.BlockSpec((1, s, GROUP_LANES), lambda b, g: (b, 0, g))
    bblk = pl.BlockSpec((1, s, D_STATE), lambda b, g: (b, 0, g))
    cblk = pl.BlockSpec((1, s, D_STATE), lambda b, g: (b, 0, N_GROUPS + g))
    tblk = pl.BlockSpec((1, 8, s), lambda b, g: (b, (LANES // 8) * g, 0))
    return xblk, bblk, cblk, tblk


def _chunk_views(i, j, x_ref, ac_ref, at_ref):
    ab = SEQ_BLOCK
    sl = slice(ab * i, ab * (i + 1))
    hs = slice(HEAD_DIM * j, HEAD_DIM * (j + 1))
    a_prev = jnp.zeros((1, 1), F32) if i == 0 else ac_ref[0, ab * i - 1:ab * i, j:j + 1]
    return sl, hs, ac_ref[0, sl, j:j + 1], at_ref[0, j:j + 1, sl], a_prev


def _ssd_fwd_chunked(xdtg, bc, acum, acum_t):
    nb_, s, _ = xdtg.shape
    ab = SEQ_BLOCK
    hpg = HEADS_PER_GROUP

    def body(x_ref, b_ref, c_ref, ac_ref, at_ref, y_ref):
        ok = _causal_ok(0, 0)
        states = [jnp.zeros((D_STATE, HEAD_DIM), F32) for _ in range(hpg)]
        for i in range(s // ab):
            bm, cm = b_ref[0, ab * i:ab * (i + 1), :], c_ref[0, ab * i:ab * (i + 1), :]
            bm_t = bm.T
            cb = jnp.dot(cm, bm_t, preferred_element_type=F32)
            ys = []
            for j in range(hpg):
                sl, hs, acol, arow, a_prev = _chunk_views(i, j, x_ref, ac_ref, at_ref)
                y, states[j] = _ssd_chunk(states[j], x_ref[0, sl, hs], bm_t, cm, cb, acol, arow, a_prev, ok)
                ys.append(y)
            y_ref[0, sl, :] = jnp.concatenate(ys + [jnp.zeros((ab, GROUP_LANES - hpg * HEAD_DIM), F32)], axis=1)

    xblk, bblk, cblk, tblk = _ssd_specs(s)
    ablk = pl.BlockSpec((1, s, LANES), lambda b, g: (b, 0, g))
    return pl.pallas_call(
        body, name="ssd_fwd", grid=(nb_, N_GROUPS), in_specs=[xblk, bblk, cblk, ablk, tblk], out_specs=xblk,
        out_shape=jax.ShapeDtypeStruct((nb_, s, N_GROUPS * GROUP_LANES), F32),
        compiler_params=_params("parallel", "parallel"),
    )(xdtg, bc, bc, acum, acum_t)


def _ssd_bwd_chunked(xdtg, bc, acum, acum_t, dyg):
    nb_, s, _ = xdtg.shape
    ab = SEQ_BLOCK
    nblk = s // ab
    hpg = HEADS_PER_GROUP

    def body(x_ref, b_ref, c_ref, ac_ref, at_ref, dy_ref, dx_ref, db_ref, dc_ref, dac_ref, dar_ref, s_s):
        ok = _causal_ok(0, 0)
        dx_ref[...] = jnp.zeros_like(dx_ref)
        dac_ref[...] = jnp.zeros_like(dac_ref)
        dar_ref[...] = jnp.zeros_like(dar_ref)
        states = [jnp.zeros((D_STATE, HEAD_DIM), F32) for _ in range(hpg)]
        for i in range(nblk):
            bm_t = b_ref[0, ab * i:ab * (i + 1), :].T
            for j in range(hpg):
                sl, hs, acol, arow, a_prev = _chunk_views(i, j, x_ref, ac_ref, at_ref)
                s_s[hpg * i + j] = states[j]
                if i + 1 < nblk:
                    a_end = acol[ab - 1:ab, :]
                    wx = _bf(jnp.exp(a_end - acol) * x_ref[0, sl, hs].astype(F32))
                    states[j] = jnp.exp(a_end - a_prev) * states[j] + jnp.dot(bm_t, wx, preferred_element_type=F32)
        ok_t = _causal_ok_t(0, 0)
        last_row = lax.broadcasted_iota(jnp.int32, (ab, 1), 0) == ab - 1
        d_state = [jnp.zeros((D_STATE, HEAD_DIM), F32) for _ in range(hpg)]
        pending = [jnp.zeros((1, 1), F32) for _ in range(hpg)]
        total = lambda v: jnp.sum(v, keepdims=True)
        for i in reversed(range(nblk)):
            bm, cm = b_ref[0, ab * i:ab * (i + 1), :], c_ref[0, ab * i:ab * (i + 1), :]
            cm_t = cm.T
            cbt = jnp.dot(bm, cm_t, preferred_element_type=F32)
            dcbt = jnp.zeros((ab, ab), F32)
            d_bm, d_cm = jnp.zeros((ab, D_STATE), F32), jnp.zeros((ab, D_STATE), F32)
            for j in range(hpg):
                sl, hs, acol, arow, a_prev = _chunk_views(i, j, x_ref, ac_ref, at_ref)
                x, dy = x_ref[0, sl, hs], dy_ref[0, sl, hs]
                dy16 = _bf(dy)
                s_in, g_out = s_s[hpg * i + j], d_state[j]
                s16, g16 = _bf(s_in), _bf(g_out)
                decay = jnp.exp(jnp.where(ok_t, arow - acol, NEG))
                gt = cbt * decay
                dgt = lax.dot_general(x, dy16, _NT, preferred_element_type=F32)
                d_x = jnp.dot(_bf(gt), dy16, preferred_element_type=F32)
                dcbt = dcbt + dgt * decay
                mm = dgt * gt
                d_arow = jnp.sum(mm, axis=0, keepdims=True)
                d_acol = -jnp.sum(mm, axis=1, keepdims=True)
                e = jnp.exp(acol - a_prev)
                edy16 = _bf(e * dy)
                d_cm = d_cm + lax.dot_general(edy16, s16, _NT, preferred_element_type=F32)
                d_s = jnp.dot(cm_t, edy16, preferred_element_type=F32)
                de_e = jnp.sum(dy * jnp.dot(cm, s16, preferred_element_type=F32), axis=1, keepdims=True) * e
                a_end = acol[ab - 1:ab, :]
                w = jnp.exp(a_end - acol)
                f = jnp.exp(a_end - a_prev)
                x32 = x.astype(F32)
                bg = jnp.dot(bm, g16, preferred_element_type=F32)
                d_x = d_x + w * bg
                d_bm = d_bm + lax.dot_general(_bf(w * x32), g16, _NT, preferred_element_type=F32)
                dw_w = jnp.sum(bg * x32, axis=1, keepdims=True) * w
                df_f = total(g_out * s_in) * f
                d_end = total(dw_w) + df_f
                d_acol = d_acol + de_e - dw_w + jnp.where(last_row, d_end + pending[j], 0.0)
                pending[j] = -total(de_e) - df_f
                d_state[j] = d_s + f * g_out
                dx_ref[0, sl, hs] = d_x
                dac_ref[0, sl, j:j + 1] = d_acol
                dar_ref[0, j:j + 1, sl] = d_arow
            dcbt16 = _bf(dcbt)
            db_ref[0, ab * i:ab * (i + 1), :] = d_bm + jnp.dot(dcbt16, cm, preferred_element_type=F32)
            dc_ref[0, ab * i:ab * (i + 1), :] = d_cm + lax.dot_general(dcbt16, bm, _TN, preferred_element_type=F32)

    xblk, bblk, cblk, tblk = _ssd_specs(s)
    ablk = pl.BlockSpec((1, s, LANES), lambda b, g: (b, 0, g))
    return pl.pallas_call(
        body, name="ssd_bwd", grid=(nb_, N_GROUPS),
        in_specs=[xblk, bblk, cblk, ablk, tblk, xblk],
        out_specs=[xblk, bblk, bblk, ablk, pl.BlockSpec((1, 8, s), lambda b, g: (b, g, 0))],
        out_shape=[jax.ShapeDtypeStruct((nb_, s, N_GROUPS * GROUP_LANES), F32),
                   jax.ShapeDtypeStruct((nb_, s, N_GROUPS * D_STATE), F32),
                   jax.ShapeDtypeStruct((nb_, s, N_GROUPS * D_STATE), F32),
                   jax.ShapeDtypeStruct((nb_, s, N_GROUPS * LANES), F32),
                   jax.ShapeDtypeStruct((nb_, N_GROUPS * 8, s), F32)],
        scratch_shapes=[pltpu.VMEM((nblk * hpg, D_STATE, HEAD_DIM), F32)],
        compiler_params=_params("parallel", "parallel"),
    )(xdtg, bc, bc, acum, acum_t, dyg)


def _interleave(wg, wu):
    k, f = wg.shape
    gi = GATE_UP_INTERLEAVE
    return jnp.stack([wg.reshape(k, f // gi, gi), wu.reshape(k, f // gi, gi)], axis=2).reshape(k, 2 * f)


def _head_expanders():
    e_x = np.zeros((LANES, N_GROUPS * GROUP_LANES), np.float32)
    e_a = np.zeros((LANES, N_GROUPS * LANES), np.float32)
    for h in range(N_HEADS):
        g, j = divmod(h, HEADS_PER_GROUP)
        e_x[h, GROUP_LANES * g + HEAD_DIM * j:GROUP_LANES * g + HEAD_DIM * (j + 1)] = 1.0
        e_a[h, LANES * g + j] = 1.0
    return [jnp.asarray(m, BF16) for m in (e_x, e_x.T, e_a, e_a.T)]


def _pad_lanes(v, n=LANES):
    return jnp.pad(v, ((0, 0), (0, n - v.shape[1])))


def _local_step(x, positions, target, w, late=None, early_grad_job=None):
    nb, s, d = x.shape
    t = nb * s
    x2 = x.reshape(t, d)
    tgt2 = target.reshape(t, d)
    (job_a, weights_a), (job_b, weights_b) = late if late is not None else ((None, None), (None, None))

    x16 = _bf(x2)
    wgu1 = _interleave(w["ffn1_gate"], w["ffn1_up"])
    ffn1 = _ffn_fwd("ffn1_fwd", x16, x2, wgu1, w["ffn1_down"], w["ln1_g"], w["ln1_b"], carry=job_a)
    au1, hm1, h1, r1, h1_16 = ffn1[:5]
    if job_a is not None:
        w = {**w, **weights_a(ffn1[5])}

    w_in = w["w_in"]
    wqk, wv, wz = w_in[:, :2 * D_ATTN], w_in[:, 2 * D_ATTN:3 * D_ATTN], w_in[:, 3 * D_ATTN:3 * D_ATTN + D_SSD]
    wxbc = w_in[:, 3 * D_ATTN + D_SSD:3 * D_ATTN + D_SSD + D_CONV]
    wdt = _pad_lanes(w_in[:, 3 * D_ATTN + D_SSD + D_CONV:])

    inv_freq = ROPE_THETA ** (-jnp.arange(0, ROPE_DIM, 2, dtype=F32) / ROPE_DIM)
    half = ROPE_DIM // 2
    head_invf = jnp.concatenate([inv_freq, inv_freq, jnp.zeros((HEAD_DIM - ROPE_DIM,), F32)])
    head_sgn = jnp.concatenate([-jnp.ones((half,), F32), jnp.ones((half,), F32), jnp.zeros((HEAD_DIM - ROPE_DIM,), F32)])
    invf = jnp.tile(head_invf, LANES // HEAD_DIM)[None, :]
    sgn = jnp.tile(head_sgn, LANES // HEAD_DIM)[None, :]
    posf = positions.astype(F32).reshape(t, 1)
    bias_fwd, bias_bwd = _branch_bias_table(s, FWD_KEY_BLOCK), _branch_bias_table(s, SEQ_BLOCK)
    spreaders = _head_expanders()
    dtb, alog = _pad_lanes(w["dt_bias"]), _pad_lanes(w["a_log"])
    dskip = jnp.repeat(w["d_skip"], HEAD_DIM, axis=1)

    proj = _proj_in(h1_16, _pad_lanes(w_in, w_in.shape[1] - N_HEADS + LANES), posf, invf, sgn, carry=job_b)
    q16, k16, v16, z, xbc_pre, dtp, cs = proj[:7]
    if job_b is not None:
        w = {**w, **weights_b(proj[7])}
    wgu2 = _interleave(w["ffn2_gate"], w["ffn2_up"])
    to3 =lambda a: a.reshape(nb, s, a.shape[-1])
    attn_o, lse = _attn_fwd(to3(q16), to3(k16), to3(v16), bias_fwd)

    xbc = _conv_fwd(to3(xbc_pre), w["conv_w"], w["conv_b"]).reshape(t, D_CONV)
    xdtg, bc16, dag = _ssd_prep_fwd(xbc, dtp, dtb, alog, spreaders)
    acum, acum_t = _cumsum_fwd(to3(dag))
    yg = _ssd_fwd_chunked(to3(xdtg), to3(bc16), acum, acum_t)

    cat = _norms_fwd(attn_o.reshape(t, D_ATTN), yg.reshape(t, -1), xbc, z, w["attn_norm_w"], w["ssd_norm_w"], dskip)
    h2, r2, h2_16 = _mm_res_ln("w_out_ln2", cat, w["w_out"], h1, w["ln2_g"], w["ln2_b"], scale=1.0)

    au2, hm2, _, r3, _ = _ffn_fwd("ffn2_fwd", h2_16, h2, wgu2, w["ffn2_down"], w["ln3_g"], w["ln3_b"])

    g = {}
    dr3, dr3_16, g["ln3_g"], g["ln3_b"], loss = _ln_loss_bwd("loss_ln3_bwd", r3, w["ln3_g"], w["ln3_b"], tgt2)

    dau2, dh2 = _ffn_bwd("ffn2_bwd", dr3_16, dr3, w["ffn2_down"].T, au2, wgu2.T)
    g["ffn2_down"] = _mm_tn("ffn2_down_dw", hm2, dr3_16, scale=0.5, tk=D_FF // 2, tn=512)
    g["ffn2_gate"], g["ffn2_up"] = _mm_tn_gate_up("ffn2_up_dw", h2_16, dau2)

    dr2, dr2_16, g["ln2_g"], g["ln2_b"] = _ln_bwd("ln2_bwd", r2, w["ln2_g"], w["ln2_b"], dh2)
    dcat = _mm("w_out_dx", [(dr2_16, w["w_out"].T)], tm=1024, tn=768)
    g["w_out"] = _mm_tn("w_out_dw", cat, dr2_16, tk=768, tn=1024)

    d_attn, dyg, dxs_a, dz16, g["attn_norm_w"], g["ssd_norm_w"], ddskip = _norms_bwd(
        attn_o.reshape(t, D_ATTN), yg.reshape(t, -1), xbc, z, w["attn_norm_w"], w["ssd_norm_w"], dskip, dcat)
    g["d_skip"] = ddskip.reshape(N_HEADS, HEAD_DIM).sum(axis=1)[None, :]

    dq, dk, dv16 = _attn_bwd(to3(q16), to3(k16), to3(v16), attn_o, to3(d_attn), lse, bias_bwd)
    dqk16 = _rope_bwd(dq.reshape(t, D_ATTN), dk.reshape(t, D_ATTN), cs)

    dxdtg, dbm, dcm, dacol, darow = _ssd_bwd_chunked(to3(xdtg), to3(bc16), acum, acum_t, to3(dyg))
    ddag = _cumsum_bwd(dacol, darow)
    dxbc, ddtp16, ddtb, dalog = _ssd_prep_bwd(xbc, dtp, dtb, alog, spreaders, dxdtg.reshape(t, -1), ddag.reshape(t, -1),
                                               dxs_a, dbm.reshape(t, -1), dcm.reshape(t, -1))
    g["dt_bias"], g["a_log"] = ddtb[:, :N_HEADS], dalog[:, :N_HEADS]
    dxbc_pre16, dconv_w, g["conv_b"] = _conv_bwd(to3(xbc_pre), w["conv_w"], w["conv_b"], to3(dxbc))
    g["conv_w"] = dconv_w[:CONV_WIDTH]
    dxbc_pre16 = dxbc_pre16.reshape(t, D_CONV)
    dv16 = dv16.reshape(t, D_ATTN)

    dh1 = _mm("w_in_dx", [(dqk16, wqk.T), (dv16, wv.T), (dz16, wz.T), (dxbc_pre16, wxbc.T), (ddtp16, wdt.T)],
              res=dr2, res_scale=ALPHA)
    g["w_in"] = _mm_tn_sections("w_in_dw", h1_16, [dqk16, dv16, dz16, dxbc_pre16, ddtp16])[:, :w_in.shape[1]]

    dr1, dr1_16, g["ln1_g"], g["ln1_b"] = _ln_bwd("ln1_bwd", r1, w["ln1_g"], w["ln1_b"], dh1)
    g["ffn1_down"] = _mm_tn("ffn1_down_dw", hm1, dr1_16, scale=0.5, tk=D_FF // 2, tn=512)
    ffn1b = _ffn_bwd("ffn1_bwd", dr1_16, dr1, w["ffn1_down"].T, au1, wgu1.T,
                     carry=None if early_grad_job is None else early_grad_job(g))
    dau1, dx = ffn1b[:2]
    early = ffn1b[2] if early_grad_job is not None else None
    g["ffn1_gate"], g["ffn1_up"] = _mm_tn_gate_up("ffn1_up_dw", x16, dau1)
    return loss, dx.reshape(nb, s, d), g, early


_HBM = pl.BlockSpec(memory_space=pltpu.HBM)
N_CHIPS = 4
N_DEVICES = 8


def _place():
    return lax.axis_index("x"), lax.axis_index("y"), lax.axis_index("c")


def _other_chips(x, y):
    return [(1 - x, y), (x, 1 - y), (1 - x, 1 - y)]


class _GatherJob:
    def __init__(self, shards):
        assert all((a.shape[0] // 2) % 16 == 0 for a in shards)
        self.n = len(shards)
        self.shapes = [a.shape for a in shards]
        self.operands = [a.reshape(2, a.shape[0] // 2, a.shape[1]) for a in shards]
        self.out_shape = [jax.ShapeDtypeStruct((N_CHIPS,) + a.shape, a.dtype) for a in self.operands]
        pair = pltpu.SemaphoreType.DMA((self.n, N_CHIPS - 1))
        one = pltpu.SemaphoreType.DMA((self.n,))
        self.scratch_shapes = [pair, pair, pair, pair, one, one]

    def results(self, outs):
        return [o.reshape((N_CHIPS,) + s) for o, s in zip(outs, self.shapes)]

    def phases(self, ins, outs, sems):
        n = self.n
        send_sems, recv_sems, fwd_send_sems, fwd_recv_sems, own_send_sems, own_recv_sems = sems
        x, y, c = _place()
        me = 2 * x + y
        peers = _other_chips(x, y)

        def own(t):
            return pltpu.make_async_remote_copy(ins[t], outs[t].at[me], own_send_sems.at[t], own_recv_sems.at[t],
                                                device_id=(x, y, 1 - c), device_id_type=MESH)

        def ici(t, p, src_chip):
            px, py = peers[p]
            return pltpu.make_async_remote_copy(
                ins[t].at[c] if src_chip is None else outs[t].at[src_chip, c],
                outs[t].at[me if src_chip is None else src_chip, c],
                send_sems.at[t, p], recv_sems.at[t, p], device_id=(px, py, c), device_id_type=MESH)

        def d2d(t, p, core):
            px, py = peers[p]
            return pltpu.make_async_remote_copy(
                outs[t].at[2 * px + py, core], outs[t].at[2 * px + py, core],
                fwd_send_sems.at[t, p], fwd_recv_sems.at[t, p], device_id=(x, y, 1 - c), device_id_type=MESH)

        pairs = [(t, p) for t in range(n) for p in range(N_CHIPS - 1)]

        def start():
            for t, p in pairs:
                ici(t, p, None).start()
            for t in range(n):
                own(t).start()

        def forward():
            for t, p in pairs:
                px, py = peers[p]
                ici(t, p, 2 * px + py).wait_recv()
                d2d(t, p, c).start()

        def finish():
            for t, p in pairs:
                d2d(t, p, 1 - c).wait_recv()
            for t in range(n):
                own(t).wait()
            for t, p in pairs:
                ici(t, p, None).wait_send()
                d2d(t, p, c).wait_send()

        return start, forward, finish


class _ExchangeJob:
    def __init__(self, stacks):
        self.n = len(stacks)
        self.operands = list(stacks)
        self.out_shape = [jax.ShapeDtypeStruct(a.shape, a.dtype) for a in stacks]
        pair = pltpu.SemaphoreType.DMA((self.n, N_CHIPS - 1))
        self.scratch_shapes = [pair, pair]

    def results(self, outs):
        return list(outs)

    def phases(self, ins, outs, sems):
        send_sems, recv_sems = sems
        x, y, c = _place()
        me = 2 * x + y
        peers = _other_chips(x, y)
        pairs = [(t, p) for t in range(self.n) for p in range(N_CHIPS - 1)]

        def copy(t, p):
            px, py = peers[p]
            return pltpu.make_async_remote_copy(ins[t].at[2 * px + py], outs[t].at[me], send_sems.at[t, p],
                                                recv_sems.at[t, p], device_id=(px, py, c), device_id_type=MESH)

        def arrival(t, p):
            px, py = peers[p]
            return pltpu.make_async_remote_copy(ins[t].at[me], outs[t].at[2 * px + py], send_sems.at[t, p],
                                                recv_sems.at[t, p], device_id=(px, py, c), device_id_type=MESH)

        def start():
            for t, p in pairs:
                copy(t, p).start()

        def finish():
            for t, p in pairs:
                arrival(t, p).wait_recv()
            for t, p in pairs:
                copy(t, p).wait_send()

        return start, None, finish


def _run_job(job, name):
    n = job.n

    def body(*refs):
        for phase in job.phases(refs[:n], refs[n:2 * n], refs[2 * n:]):
            if phase is not None:
                phase()

    outs = pl.pallas_call(
        body, name=name, in_specs=[_HBM] * n, out_specs=[_HBM] * n,
        out_shape=job.out_shape, scratch_shapes=job.scratch_shapes,
    )(*job.operands)
    return job.results(outs)


def _sibling_halves(stacks, name):
    n = len(stacks)
    halves = [a.shape[1] // 2 for a in stacks]
    split = [a.reshape(a.shape[0], 2, h, a.shape[2]) for a, h in zip(stacks, halves)]

    def body(*refs):
        ins, outs = refs[:n], refs[n:2 * n]
        send_sems, recv_sems = refs[2 * n:]
        x, y, c = _place()
        cps = []
        for t in range(n):
            cp = pltpu.make_async_remote_copy(ins[t].at[:, 1 - c], outs[t], send_sems.at[t], recv_sems.at[t],
                                              device_id=(x, y, 1 - c), device_id_type=MESH)
            cp.start()
            cps.append(cp)
        for cp in cps:
            cp.wait()

    return pl.pallas_call(
        body, name=name,
        in_specs=[_HBM] * n, out_specs=[_HBM] * n,
        out_shape=[jax.ShapeDtypeStruct((a.shape[0], h, a.shape[2]), a.dtype) for a, h in zip(stacks, halves)],
        scratch_shapes=[pltpu.SemaphoreType.DMA((n,)), pltpu.SemaphoreType.DMA((n,))],
    )(*split)


def _sibling_swap(arrs):
    n = len(arrs)

    def body(*refs):
        ins, outs = refs[:n], refs[n:2 * n]
        send_sems, recv_sems = refs[2 * n:]
        x, y, c = _place()
        cps = []
        for t in range(n):
            cp = pltpu.make_async_remote_copy(ins[t], outs[t], send_sems.at[t], recv_sems.at[t],
                                              device_id=(x, y, 1 - c), device_id_type=MESH)
            cp.start()
            cps.append(cp)
        for cp in cps:
            cp.wait()

    return pl.pallas_call(
        body, name="sibling_swap",
        in_specs=[_HBM] * n, out_specs=[_HBM] * n,
        out_shape=[jax.ShapeDtypeStruct(a.shape, a.dtype) for a in arrs],
        scratch_shapes=[pltpu.SemaphoreType.DMA((n,)), pltpu.SemaphoreType.DMA((n,))],
    )(*arrs)


def _half_sum(name, own, other, core):
    k, r, cols = own.shape
    h = r // 2
    tr = next(cand for cand in (128, 176, 64, 32, 16) if h % cand == 0)
    nblk = h // tr

    def body(core_ref, own_ref, other_ref, o_ref):
        o_ref[...] = _bf(own_ref[...] + other_ref[...].astype(F32))

    grid_spec = pltpu.PrefetchScalarGridSpec(
        num_scalar_prefetch=1, grid=(nblk,),
        in_specs=[pl.BlockSpec((k, tr, cols), lambda i, core_ref: (0, i + core_ref[0] * nblk, 0)),
                  pl.BlockSpec((k, tr, cols), lambda i, core_ref: (0, i, 0))],
        out_specs=pl.BlockSpec((k, tr, cols), lambda i, core_ref: (0, i, 0)))
    return pl.pallas_call(
        body, name=name, grid_spec=grid_spec, out_shape=jax.ShapeDtypeStruct((k, h, cols), BF16),
        compiler_params=_params("parallel"),
    )(core.reshape(1).astype(jnp.int32), own, other)


def _small_allreduce(v):
    r = v.shape[0]

    def body(v_ref, tot_ref, slots, send_sems, recv_sems):
        x, y, c = _place()
        me = 4 * x + 2 * y + c
        slots[me] = v_ref[...]
        cps, peers = [], []
        for k in range(1, N_DEVICES):
            px = 1 - x if (k >> 2) & 1 else x
            py = 1 - y if (k >> 1) & 1 else y
            pc = 1 - c if k & 1 else c
            cp = pltpu.make_async_remote_copy(v_ref, slots.at[me], send_sems.at[k - 1], recv_sems.at[k - 1],
                                              device_id=(px, py, pc), device_id_type=MESH)
            cp.start()
            cps.append(cp)
            peers.append((px, py, pc))
        for k, (px, py, pc) in enumerate(peers):
            pltpu.make_async_remote_copy(v_ref, slots.at[4 * px + 2 * py + pc], send_sems.at[k], recv_sems.at[k],
                                         device_id=(px, py, pc), device_id_type=MESH).wait_recv()
        for cp in cps:
            cp.wait_send()
        acc = slots[0]
        for s in range(1, N_DEVICES):
            acc = acc + slots[s]
        tot_ref[...] = acc

    return pl.pallas_call(
        body, name="small_allreduce",
        in_specs=[pl.BlockSpec(memory_space=pltpu.VMEM)], out_specs=pl.BlockSpec(memory_space=pltpu.VMEM),
        out_shape=jax.ShapeDtypeStruct((r, LANES), F32),
        scratch_shapes=[pltpu.VMEM((N_DEVICES, r, LANES), F32), pltpu.SemaphoreType.DMA((N_DEVICES - 1,)),
                        pltpu.SemaphoreType.DMA((N_DEVICES - 1,))],
    )(v)


def _elementwise(name, fn, ins, out_dtypes):
    r, c = ins[0].shape[-2:]
    tr = next((cand for cand in (256, 176, 128, 64, 32, 16) if r % cand == 0), r)
    nin = len(ins)

    def body(*refs):
        outs = fn(*[ref[...] for ref in refs[:nin]])
        for o_ref, o in zip(refs[nin:], outs):
            o_ref[...] = o.astype(o_ref.dtype)

    in_specs = [pl.BlockSpec((tr, c), lambda i: (i, 0)) if a.ndim == 2 else pl.BlockSpec((a.shape[0], tr, c), lambda i: (0, i, 0))
                for a in ins]
    return pl.pallas_call(
        body, name=name, grid=(r // tr,), in_specs=in_specs,
        out_specs=[pl.BlockSpec((tr, c), lambda i: (i, 0)) for _ in out_dtypes],
        out_shape=[jax.ShapeDtypeStruct((r, c), dt) for dt in out_dtypes],
        compiler_params=_params("parallel"),
    )(*ins)


def _row_tile(rows):
    return next((cand for cand in (128, 176, 64, 32, 16) if rows % cand == 0), rows)


def _sum_slots(name, received, own, chip):
    _, r, cols = own.shape
    tr = _row_tile(r)

    def body(chip_ref, own_ref, a_ref, b_ref, c_ref, o_ref):
        o_ref[...] = ((own_ref[0].astype(F32) + a_ref[0].astype(F32)) + b_ref[0].astype(F32)) + c_ref[0].astype(F32)

    def slot(flip):
        return pl.BlockSpec((1, tr, cols), lambda i, chip_ref: (jnp.bitwise_xor(chip_ref[0], flip), i, 0))

    grid_spec = pltpu.PrefetchScalarGridSpec(
        num_scalar_prefetch=1, grid=(r // tr,), in_specs=[slot(0), slot(1), slot(2), slot(3)],
        out_specs=pl.BlockSpec((tr, cols), lambda i, chip_ref: (i, 0)))
    return pl.pallas_call(
        body, name=name, grid_spec=grid_spec, out_shape=jax.ShapeDtypeStruct((r, cols), F32),
        compiler_params=_params("parallel"),
    )(chip.reshape(1).astype(jnp.int32), own, received, received, received)


def _adamw_halves(name, mine, theirs, core, w, m, v):
    h, cols = mine.shape
    tr = _row_tile(h)
    nh = h // tr

    def body(core_ref, mine_ref, theirs_ref, w_ref, m_ref, v_ref, g_ref, d_ref, m2_ref, v2_ref):
        is_mine = (pl.program_id(0) // nh) == core_ref[0]
        g = jnp.where(is_mine, mine_ref[...], theirs_ref[...])
        outs = _adamw_math(g, w_ref[...], m_ref[...], v_ref[...])
        for ref, val in zip((g_ref, d_ref, m2_ref, v2_ref), outs):
            ref[...] = val

    half = pl.BlockSpec((tr, cols), lambda i, core_ref: (i % nh, 0))
    full = pl.BlockSpec((tr, cols), lambda i, core_ref: (i, 0))
    grid_spec = pltpu.PrefetchScalarGridSpec(
        num_scalar_prefetch=1, grid=(2 * nh,), in_specs=[half, half, full, full, full], out_specs=[full] * 4)
    return pl.pallas_call(
        body, name=name, grid_spec=grid_spec, out_shape=[jax.ShapeDtypeStruct((2 * h, cols), F32)] * 4,
        compiler_params=_params("parallel"),
    )(core.reshape(1).astype(jnp.int32), mine, theirs, w, m, v)


def _adamw_math(g, w_v, m_v, v_v):
    m2 = ADAM_B1 * m_v + (1.0 - ADAM_B1) * g
    v2 = ADAM_B2 * v_v + (1.0 - ADAM_B2) * jnp.square(g)
    m_hat = m2 / (1.0 - ADAM_B1 ** ADAM_STEP)
    v_hat = v2 / (1.0 - ADAM_B2 ** ADAM_STEP)
    delta = -ADAM_LR * (m_hat / (jnp.sqrt(v_hat) + ADAM_EPS) + ADAM_WD * w_v)
    return [g, delta, m2, v2]


def _adamw(name, g, w, m, v):
    return _elementwise(name, _adamw_math, [g, w, m, v], [F32] * 4)


_TRANSPOSED = ("ffn1_gate", "ffn1_up", "ffn2_gate", "ffn2_up")
_MATRICES = (("ffn1_gate", 0), ("ffn1_up", 0), ("ffn1_down", 0), ("w_in", 1), ("w_out", 0),
             ("ffn2_gate", 0), ("ffn2_up", 0), ("ffn2_down", 0))


def _block2d(a, name):
    return jnp.swapaxes(a, 1, 2)[0] if name in _TRANSPOSED else a[0]


def _block3d(a, name):
    return jnp.swapaxes(a[None], 1, 2) if name in _TRANSPOSED else a[None]
_VECTORS = ("ln1_g", "ln1_b", "conv_b", "dt_bias", "a_log", "d_skip", "attn_norm_w", "ssd_norm_w",
            "ln2_g", "ln2_b", "ln3_g", "ln3_b")
_WEIGHT_ORDER = ("ln1_g", "ln1_b", "ffn1_gate", "ffn1_up", "ffn1_down", "w_in", "conv_w", "conv_b", "dt_bias", "a_log",
                 "d_skip", "attn_norm_w", "ssd_norm_w", "w_out", "ln2_g", "ln2_b", "ffn2_gate", "ffn2_up", "ffn2_down",
                 "ln3_g", "ln3_b")


def _pack_rows(vectors):
    parts = []
    for vec in vectors:
        flat = vec.reshape(-1)
        parts.append(jnp.pad(flat, (0, (-flat.shape[0]) % LANES)))
    flat = jnp.concatenate(parts)
    flat = jnp.pad(flat, (0, (-flat.shape[0]) % (8 * LANES)))
    return flat.reshape(-1, LANES)


def _unpack_rows(packed, shapes):
    flat = packed.reshape(-1)
    out, off = [], 0
    for shape in shapes:
        size = int(np.prod(shape))
        out.append(flat[off:off + size].reshape(shape))
        off += size + (-size) % LANES
    return out


def _assemble(stack, axis):
    if axis == 0:
        return stack.reshape(-1, stack.shape[2])
    return jnp.concatenate([stack[s] for s in range(N_CHIPS)], axis=1)


def _split(full, axis):
    if axis == 0:
        return full.reshape(N_CHIPS, -1, full.shape[1])
    cols = full.shape[1] // N_CHIPS
    return jnp.stack([full[:, cols * s:cols * (s + 1)] for s in range(N_CHIPS)])


def kernel(x, positions, ln1_g, ln1_b, ffn1_gate, ffn1_up, ffn1_down, w_in, conv_w, conv_b, dt_bias, a_log, d_skip, attn_norm_w, ssd_norm_w, w_out, ln2_g, ln2_b, ffn2_gate, ffn2_up, ffn2_down, ln3_g, ln3_b, loss_target, m_ln1_g, m_ln1_b, m_ffn1_gate, m_ffn1_up, m_ffn1_down, m_w_in, m_conv_w, m_conv_b, m_dt_bias, m_a_log, m_d_skip, m_attn_norm_w, m_ssd_norm_w, m_w_out, m_ln2_g, m_ln2_b, m_ffn2_gate, m_ffn2_up, m_ffn2_down, m_ln3_g, m_ln3_b, v_ln1_g, v_ln1_b, v_ffn1_gate, v_ffn1_up, v_ffn1_down, v_w_in, v_conv_w, v_conv_b, v_dt_bias, v_a_log, v_d_skip, v_attn_norm_w, v_ssd_norm_w, v_w_out, v_ln2_g, v_ln2_b, v_ffn2_gate, v_ffn2_up, v_ffn2_down, v_ln3_g, v_ln3_b):
    given = dict(locals())
    wts = {n: given[n] for n in _WEIGHT_ORDER}
    mom_m = {n: given["m_" + n] for n in _WEIGHT_ORDER}
    mom_v = {n: given["v_" + n] for n in _WEIGHT_ORDER}
    chip = 2 * lax.axis_index("x") + lax.axis_index("y")

    core = lax.axis_index("c")
    groups = [[(n, axis) for n, axis in _MATRICES if n.startswith(prefix)] for prefix in ("ffn1", "w_", "ffn2")]
    own16 = {n: _block2d(wts[n], n).astype(BF16) for n, _ in _MATRICES}

    def full_weights(group, results):
        out = {}
        for (n, axis), st in zip(group, results):
            whole = _assemble(st, axis)
            out[n] = whole.T if n in _TRANSPOSED else whole
        return out

    full = full_weights(groups[0], _run_job(_GatherJob([own16[n] for n, _ in groups[0]]), "gather_ffn1"))
    for n in _VECTORS:
        full[n] = wts[n]
    conv_rows = jnp.pad(wts["conv_w"][0], ((0, 32 - CONV_WIDTH), (0, 0)))

    def mixer_weights(results):
        out = full_weights(groups[1], results)
        out["conv_w"] = _assemble(results[-1], 1)[:CONV_WIDTH]
        return out

    def ffn2_weights(results):
        return full_weights(groups[2], results)

    late = [(_GatherJob([own16[n] for n, _ in groups[1]] + [conv_rows]), mixer_weights),
            (_GatherJob([own16[n] for n, _ in groups[2]]), ffn2_weights)]

    chip_sums = {}

    def core_sums(g, which, tag):
        partials = [_split(g[n], axis) for n, axis in which]
        from_sibling = _sibling_halves([p.astype(BF16) for p in partials], "sibling_halves_" + tag)
        for (n, _), p, o in zip(which, partials, from_sibling):
            chip_sums[n] = _half_sum("core_sum_" + n, p, o, core)
        return _ExchangeJob([chip_sums[n] for n, _ in which])

    last = [(n, axis) for n, axis in _MATRICES if n in ("ffn1_gate", "ffn1_up")]
    early = [(n, axis) for n, axis in _MATRICES if (n, axis) not in last]
    loss, grad_x, g, received_early = _local_step(x, positions, loss_target, full, late,
                                                  lambda g_now: core_sums(g_now, early, "early"))
    received_last = _run_job(core_sums(g, last, "last"), "exchange_last")
    received = dict(zip([n for n, _ in last + early], received_last + received_early))
    half_totals = [_sum_slots("sum_partials_" + n, received[n], chip_sums[n], chip) for n, _ in _MATRICES]
    other_halves = _sibling_swap(half_totals)

    small_shapes = [g[n].shape for n in _VECTORS] + [g["conv_w"].shape, (1,)]
    total = _small_allreduce(_pack_rows([g[n] for n in _VECTORS] + [g["conv_w"], loss[0, :1]]))
    small = _unpack_rows(total, small_shapes)
    loss_out = small[-1].reshape(())

    grads, deltas, new_m, new_v = {}, {}, {}, {}
    for (n, _), mine, theirs in zip(_MATRICES, half_totals, other_halves):
        res = _adamw_halves("adamw_" + n, mine, theirs, core, _block2d(wts[n], n), _block2d(mom_m[n], n), _block2d(mom_v[n], n))
        grads[n], deltas[n], new_m[n], new_v[n] = [_block3d(r, n) for r in res]

    vec_shapes = [wts[n].shape for n in _VECTORS]
    res = _adamw("adamw_vectors", _pack_rows(small[:len(_VECTORS)]), _pack_rows([wts[n] for n in _VECTORS]),
                 _pack_rows([mom_m[n] for n in _VECTORS]), _pack_rows([mom_v[n] for n in _VECTORS]))
    for dst, packed in zip((grads, deltas, new_m, new_v), res):
        for n, val in zip(_VECTORS, _unpack_rows(packed, vec_shapes)):
            dst[n] = val

    cols = conv_w.shape[2]
    g_conv = lax.dynamic_slice_in_dim(small[len(_VECTORS)], chip * cols, cols, axis=1)
    res = _adamw("adamw_conv_w", g_conv, wts["conv_w"][0], mom_m["conv_w"][0], mom_v["conv_w"][0])
    grads["conv_w"], deltas["conv_w"], new_m["conv_w"], new_v["conv_w"] = [r[None] for r in res]

    return (loss_out, grad_x, *[grads[n] for n in _WEIGHT_ORDER], *[deltas[n] for n in _WEIGHT_ORDER],
            *[new_m[n] for n in _WEIGHT_ORDER], *[new_v[n] for n in _WEIGHT_ORDER])
```

```python
import numpy as np
import jax
import jax.numpy as jnp
from jax import lax
from jax.experimental import pallas as pl
from jax.experimental.pallas import tpu as pltpu

F32, BF16 = jnp.float32, jnp.bfloat16

D_MODEL = 1024
D_FF = 2816
N_HEADS = 12
HEAD_DIM = 64
D_ATTN = 768
D_SSD = 768
N_GROUPS = 4
HEADS_PER_GROUP = 3
D_STATE = 128
D_CONV = 1792
CONV_WIDTH = 4
ROPE_DIM = 16
ROPE_THETA = 500000.0
ALPHA = 2.0 ** 0.25
LN_EPS = 1e-5
RMS_EPS = 1e-6
ADAM_LR, ADAM_B1, ADAM_B2, ADAM_EPS, ADAM_WD, ADAM_STEP = 0.001, 0.9, 0.999, 1e-08, 0.01, 10

LANES = 128
GATE_UP_INTERLEAVE = 256
SEQ_BLOCK = 256
GROUP_LANES = 256
VMEM_LIMIT = 56 * 1024 * 1024
NEG = -1e30
MESH = pl.DeviceIdType.MESH
HIGHEST = lax.Precision.HIGHEST

_NT = (((1,), (1,)), ((), ()))
_TN = (((0,), (0,)), ((), ()))


def _params(*sem):
    return pltpu.CompilerParams(dimension_semantics=sem, vmem_limit_bytes=VMEM_LIMIT)


def _bf(v):
    return v.astype(BF16)


EPILOGUE_ROWS = 128


def _row_chunks(tm):
    return [slice(r, min(r + EPILOGUE_ROWS, tm)) for r in range(0, tm, EPILOGUE_ROWS)]


def _sigmoid(v):
    return 0.5 * jnp.tanh(0.5 * v) + 0.5


def _mm(name, pairs, *, scale=1.0, res=None, res_scale=1.0, out_dtype=F32, tm=512, tn=512):
    m, n = pairs[0][0].shape[0], pairs[0][1].shape[1]
    tm, tn = min(tm, m), min(tn, n)
    assert m % tm == 0 and n % tn == 0, (name, m, n, tm, tn)
    npair = len(pairs)

    def body(*refs):
        acc = None
        for a_ref, b_ref in zip(refs[:npair], refs[npair:2 * npair]):
            d = jnp.dot(_bf(a_ref[...]), b_ref[...], preferred_element_type=F32)
            acc = d if acc is None else acc + d
        if scale != 1.0:
            acc = acc * scale
        if res is not None:
            acc = acc + res_scale * refs[2 * npair][...]
        refs[-1][...] = acc.astype(out_dtype)

    in_specs = [pl.BlockSpec((tm, a.shape[1]), lambda i, j: (i, 0)) for a, _ in pairs]
    in_specs += [pl.BlockSpec((b.shape[0], tn), lambda i, j: (0, j)) for _, b in pairs]
    args = [a for a, _ in pairs] + [b for _, b in pairs]
    if res is not None:
        in_specs.append(pl.BlockSpec((tm, tn), lambda i, j: (i, j)))
        args.append(res)
    return pl.pallas_call(
        body, name=name, grid=(m // tm, n // tn), in_specs=in_specs,
        out_specs=pl.BlockSpec((tm, tn), lambda i, j: (i, j)),
        out_shape=jax.ShapeDtypeStruct((m, n), out_dtype),
        compiler_params=_params("parallel", "parallel"),
    )(*args)


def _mm_tn(name, x, dy, *, scale=1.0, tk=512, tn=512, tt=2048):
    t, k = x.shape
    n = dy.shape[1]
    tk, tn, tt = min(tk, k), min(tn, n), min(tt, t)
    assert k % tk == 0 and n % tn == 0 and t % tt == 0, (name, k, n, t)
    nt = t // tt

    def body(x_ref, dy_ref, o_ref):
        step = pl.program_id(2)
        d = lax.dot_general(_bf(x_ref[...]), _bf(dy_ref[...]), _TN, preferred_element_type=F32)

        @pl.when(step == 0)
        def _():
            o_ref[...] = d

        @pl.when(step > 0)
        def _():
            o_ref[...] += d

        if scale != 1.0:
            @pl.when(step == nt - 1)
            def _():
                o_ref[...] = o_ref[...] * scale

    return pl.pallas_call(
        body, name=name, grid=(k // tk, n // tn, nt),
        in_specs=[pl.BlockSpec((tt, tk), lambda i, j, s: (s, i)), pl.BlockSpec((tt, tn), lambda i, j, s: (s, j))],
        out_specs=pl.BlockSpec((tk, tn), lambda i, j, s: (i, j)),
        out_shape=jax.ShapeDtypeStruct((k, n), F32),
        compiler_params=_params("parallel", "parallel", "arbitrary"),
    )(x, dy)


def _mm_tn_sections(name, x, dys, *, tt=512):
    t, k = x.shape
    tt = min(tt, t)
    cuts = np.cumsum([0] + [d.shape[1] for d in dys]).tolist()
    ns = len(dys)

    def body(*refs):
        x_ref, o_ref = refs[0], refs[1 + ns]
        step = pl.program_id(0)
        xt = x_ref[...].T
        parts = [jnp.dot(xt, refs[1 + a][...], preferred_element_type=F32) for a in range(ns)]

        @pl.when(step == 0)
        def _():
            for a in range(ns):
                o_ref[:, cuts[a]:cuts[a + 1]] = parts[a]

        @pl.when(step > 0)
        def _():
            for a in range(ns):
                o_ref[:, cuts[a]:cuts[a + 1]] += parts[a]

    return pl.pallas_call(
        body, name=name, grid=(t // tt,),
        in_specs=[pl.BlockSpec((tt, k), lambda s: (s, 0))] + [pl.BlockSpec((tt, d.shape[1]), lambda s: (s, 0)) for d in dys],
        out_specs=pl.BlockSpec((k, cuts[-1]), lambda s: (0, 0)),
        out_shape=jax.ShapeDtypeStruct((k, cuts[-1]), F32),
        compiler_params=_params("arbitrary"),
    )(x, *dys)


def _mm_tn_gate_up(name, x, dau, *, tt=2048):
    t, k = x.shape
    gi = GATE_UP_INTERLEAVE
    nj = dau.shape[1] // (2 * gi)
    tt = min(tt, t)
    nt = t // tt

    def body(x_ref, dy_ref, g_ref, u_ref):
        step = pl.program_id(1)
        d = lax.dot_general(dy_ref[...], _bf(x_ref[...]), _TN, preferred_element_type=F32)

        @pl.when(step == 0)
        def _():
            g_ref[...] = d[:gi]
            u_ref[...] = d[gi:]

        @pl.when(step > 0)
        def _():
            g_ref[...] += d[:gi]
            u_ref[...] += d[gi:]

    out = pl.BlockSpec((gi, k), lambda j, s: (j, 0))
    return pl.pallas_call(
        body, name=name, grid=(nj, nt),
        in_specs=[pl.BlockSpec((tt, k), lambda j, s: (s, 0)), pl.BlockSpec((tt, 2 * gi), lambda j, s: (s, j))],
        out_specs=[out, out],
        out_shape=[jax.ShapeDtypeStruct((gi * nj, k), F32)] * 2,
        compiler_params=_params("parallel", "arbitrary"),
    )(x, dau)


def _carried(carry, ins, outs, sems, step, total):
    start, forward, finish = carry.phases(ins, outs, sems)
    pl.when(step == 0)(start)
    if forward is not None:
        pl.when(step == (3 * total) // 4)(forward)
    return lambda: pl.when(step == total - 1)(finish)


def _resident(shape):
    return pl.BlockSpec(shape, lambda i: (0,) * len(shape), pipeline_mode=pl.Buffered(1))


def _ffn_fwd(name, x16, res, wgu, wd, g, b, *, tm=512, carry=None):
    t, k = x16.shape
    gi = GATE_UP_INTERLEAVE
    nj, n, ni = wd.shape[0] // gi, wd.shape[1], t // tm
    nc = carry.n if carry is not None else 0

    def body(*refs):
        x_ref, res_ref, wgu_ref, wd_ref, g_ref, b_ref = refs[:6]
        au_ref, hm_ref, y_ref, r_ref, y16_ref = refs[6 + nc:11 + nc]
        if carry is not None:
            finish = _carried(carry, refs[6:6 + nc], refs[11 + nc:11 + 2 * nc], refs[11 + 2 * nc:], pl.program_id(0), ni)
        xv = x_ref[...]
        acc = jnp.zeros((tm, n), F32)
        for j in range(nj):
            au = jnp.dot(xv, wgu_ref[:, 2 * gi * j:2 * gi * (j + 1)], preferred_element_type=F32)
            a, u = au[:, :gi], au[:, gi:]
            au_ref[:, 2 * gi * j:2 * gi * (j + 1)] = _bf(au)
            hm = _bf(a * _sigmoid(a) * u)
            hm_ref[:, gi * j:gi * (j + 1)] = hm
            acc = acc + jnp.dot(hm, wd_ref[gi * j:gi * (j + 1), :], preferred_element_type=F32)
        r = ALPHA * res_ref[...] + 0.5 * acc
        r_ref[...] = r
        y = _layer_norm(r, g_ref[...], b_ref[...])
        y_ref[...] = y
        y16_ref[...] = _bf(y)
        if carry is not None:
            finish()

    row = lambda c: pl.BlockSpec((tm, c), lambda i: (i, 0))
    hbm = pl.BlockSpec(memory_space=pltpu.HBM)
    res_ = pl.pallas_call(
        body, name=name, grid=(ni,),
        in_specs=[row(k), row(n), _resident(wgu.shape), _resident(wd.shape), _resident(g.shape), _resident(b.shape)] + [hbm] * nc,
        out_specs=[row(2 * gi * nj), row(gi * nj), row(n), row(n), row(n)] + [hbm] * nc,
        out_shape=[jax.ShapeDtypeStruct((t, 2 * gi * nj), BF16), jax.ShapeDtypeStruct((t, gi * nj), BF16),
                   jax.ShapeDtypeStruct((t, n), F32), jax.ShapeDtypeStruct((t, n), F32), jax.ShapeDtypeStruct((t, n), BF16)]
        + (carry.out_shape if carry is not None else []),
        scratch_shapes=carry.scratch_shapes if carry is not None else [],
        compiler_params=_params("arbitrary" if carry is not None else "parallel"),
    )(x16, res, wgu, wd, g, b, *(carry.operands if carry is not None else []))
    return tuple(res_[:5]) + ((carry.results(res_[5:]),) if carry is not None else ())


def _ffn_bwd(name, dr16, dr, wdt, au, wgut, *, tm=512, carry=None):
    t, n = dr16.shape
    gi = GATE_UP_INTERLEAVE
    nj, ni = wdt.shape[1] // gi, t // tm
    nc = carry.n if carry is not None else 0

    def body(*refs):
        dr16_ref, dr_ref, wdt_ref, au_ref, wgut_ref = refs[:5]
        dau_ref, dx_ref = refs[5 + nc:7 + nc]
        if carry is not None:
            finish = _carried(carry, refs[5:5 + nc], refs[7 + nc:7 + 2 * nc], refs[7 + 2 * nc:], pl.program_id(0), ni)
        drv = dr16_ref[...]
        acc = jnp.zeros((tm, n), F32)
        for j in range(nj):
            dhm = jnp.dot(drv, wdt_ref[:, gi * j:gi * (j + 1)], preferred_element_type=F32)
            au_v = au_ref[:, 2 * gi * j:2 * gi * (j + 1)].astype(F32)
            a, u = au_v[:, :gi], au_v[:, gi:]
            sig = _sigmoid(a)
            silu = a * sig
            dau = jnp.concatenate([_bf(dhm * u * (sig + silu - silu * sig)), _bf(dhm * silu)], axis=1)
            dau_ref[:, 2 * gi * j:2 * gi * (j + 1)] = dau
            acc = acc + jnp.dot(dau, wgut_ref[2 * gi * j:2 * gi * (j + 1), :], preferred_element_type=F32)
        dx_ref[...] = ALPHA * dr_ref[...] + acc
        if carry is not None:
            finish()

    row = lambda c: pl.BlockSpec((tm, c), lambda i: (i, 0))
    hbm = pl.BlockSpec(memory_space=pltpu.HBM)
    res_ = pl.pallas_call(
        body, name=name, grid=(ni,),
        in_specs=[row(n), row(n), _resident(wdt.shape), row(2 * gi * nj), _resident(wgut.shape)] + [hbm] * nc,
        out_specs=[row(2 * gi * nj), row(n)] + [hbm] * nc,
        out_shape=[jax.ShapeDtypeStruct((t, 2 * gi * nj), BF16), jax.ShapeDtypeStruct((t, n), F32)]
        + (carry.out_shape if carry is not None else []),
        scratch_shapes=carry.scratch_shapes if carry is not None else [],
        compiler_params=_params("arbitrary" if carry is not None else "parallel"),
    )(dr16, dr, wdt, au, wgut, *(carry.operands if carry is not None else []))
    return tuple(res_[:2]) + ((carry.results(res_[2:]),) if carry is not None else ())


def _layer_norm(r, g, b):
    mu = jnp.mean(r, axis=-1, keepdims=True)
    var = jnp.mean(jnp.square(r - mu), axis=-1, keepdims=True)
    return (r - mu) * lax.rsqrt(var + LN_EPS) * g + b


def _mm_res_ln(name, a, w, res, g, b, *, scale, tm=512):
    t, k = a.shape
    n = w.shape[1]

    def body(a_ref, w_ref, res_ref, g_ref, b_ref, y_ref, r_ref, y16_ref):
        for rows in _row_chunks(tm):
            r = ALPHA * res_ref[rows, :] + scale * jnp.dot(_bf(a_ref[rows, :]), w_ref[...], preferred_element_type=F32)
            r_ref[rows, :] = r
            y = _layer_norm(r, g_ref[...], b_ref[...])
            y_ref[rows, :] = y
            y16_ref[rows, :] = _bf(y)

    row = lambda c: pl.BlockSpec((tm, c), lambda i: (i, 0))
    const = lambda shape: pl.BlockSpec(shape, lambda i: (0, 0))
    return pl.pallas_call(
        body, name=name, grid=(t // tm,),
        in_specs=[row(k), const((k, n)), row(n), const((1, n)), const((1, n))],
        out_specs=[row(n), row(n), row(n)],
        out_shape=[jax.ShapeDtypeStruct((t, n), F32), jax.ShapeDtypeStruct((t, n), F32), jax.ShapeDtypeStruct((t, n), BF16)],
        compiler_params=_params("parallel"),
    )(a, w, res, g, b)


def _rowwise(name, fn, rows, consts, row_outs, acc_outs=(), tm=512):
    rows = [r if isinstance(r, tuple) else (r, r.shape[1]) for r in rows]
    t = rows[0][0].shape[0]
    tm = min(tm, t)
    assert t % tm == 0
    nr, nc, no, na = len(rows), len(consts), len(row_outs), len(acc_outs)

    def body(*refs):
        vals = [r[...] for r in refs[:nr + nc]]
        outs, accs = fn(*vals)
        for o_ref, o in zip(refs[nr + nc:nr + nc + no], outs):
            o_ref[...] = o.astype(o_ref.dtype)
        if na:
            step = pl.program_id(0)
            acc_refs = refs[nr + nc + no:]

            @pl.when(step == 0)
            def _():
                for a_ref, a in zip(acc_refs, accs):
                    a_ref[...] = a

            @pl.when(step > 0)
            def _():
                for a_ref, a in zip(acc_refs, accs):
                    a_ref[...] += a

    in_specs = [pl.BlockSpec((tm, w), lambda i: (i, 0)) for _, w in rows]
    in_specs += [pl.BlockSpec(c.shape, lambda i, nd=c.ndim: (0,) * nd) for c in consts]
    out_specs = [pl.BlockSpec((tm, c), lambda i: (i, 0)) for c, _ in row_outs]
    out_specs += [pl.BlockSpec(s, lambda i: (0, 0)) for s in acc_outs]
    out_shape = [jax.ShapeDtypeStruct((t, c), dt) for c, dt in row_outs]
    out_shape += [jax.ShapeDtypeStruct(s, F32) for s in acc_outs]
    res = pl.pallas_call(
        body, name=name, grid=(t // tm,), in_specs=in_specs, out_specs=out_specs, out_shape=out_shape,
        compiler_params=_params("arbitrary" if na else "parallel"),
    )(*[r for r, _ in rows], *consts)
    return res


def _ln_bwd(name, r, g, b, dy):
    def fn(r_v, dy_v, g_v, b_v):
        _, vjp = jax.vjp(_layer_norm, r_v, g_v, b_v)
        dr, dg, db = vjp(dy_v)
        return [dr, dr], [dg, db]
    return _rowwise(name, fn, [r, dy], [g, b], [(r.shape[1], F32), (r.shape[1], BF16)], [(1, r.shape[1])] * 2)


def _ln_loss_bwd(name, r, g, b, target):
    def fn(r_v, t_v, g_v, b_v):
        def loss_fn(rr, gg, bb):
            err = jnp.square(_layer_norm(rr, gg, bb) - t_v)
            return 0.5 * jnp.sum(jnp.mean(err, axis=-1, keepdims=True), axis=0, keepdims=True)
        loss, vjp = jax.vjp(loss_fn, r_v, g_v, b_v)
        dr, dg, db = vjp(jnp.ones((1, 1), F32))
        return [dr, dr], [dg, db, jnp.broadcast_to(loss, (1, LANES))]
    return _rowwise(name, fn, [r, target], [g, b], [(r.shape[1], F32), (r.shape[1], BF16)],
                    [(1, r.shape[1])] * 2 + [(1, LANES)])


def _rope_tables(posf, invf, sgn):
    ang = posf * invf
    return jnp.cos(ang), jnp.sin(ang) * sgn


def _rope_apply(tv, cos, sin):
    lane = lax.broadcasted_iota(jnp.int32, cos.shape, 1)
    first = (lane % HEAD_DIM) < (ROPE_DIM // 2)
    outs = []
    for gidx in range(tv.shape[1] // LANES):
        tg = tv[:, LANES * gidx:LANES * (gidx + 1)]
        sw = jnp.where(first, pltpu.roll(tg, LANES - ROPE_DIM // 2, 1), pltpu.roll(tg, ROPE_DIM // 2, 1))
        outs.append(tg * cos + sw * sin)
    return jnp.concatenate(outs, axis=1)


def _proj_in(h16, w_in, posf, invf, sgn, *, tm=512, carry=None):
    t, k = h16.shape
    cuts = [0, D_ATTN, 2 * D_ATTN, 3 * D_ATTN, 3 * D_ATTN + D_SSD, 3 * D_ATTN + D_SSD + D_CONV, w_in.shape[1]]
    nc = carry.n if carry is not None else 0

    def body(*refs):
        h_ref, w_ref, pos_ref, invf_ref, sgn_ref = refs[:5]
        q_ref, k_ref, v_ref, z_ref, xbc_ref, dt_ref, cs_ref = refs[5 + nc:12 + nc]
        if carry is not None:
            finish = _carried(carry, refs[5:5 + nc], refs[12 + nc:12 + 2 * nc], refs[12 + 2 * nc:], pl.program_id(0), t // tm)
        hv = h_ref[...]
        part = lambda a: jnp.dot(hv, w_ref[:, cuts[a]:cuts[a + 1]], preferred_element_type=F32)
        cos, sin = _rope_tables(pos_ref[...], invf_ref[...], sgn_ref[...])
        cs_ref[...] = jnp.concatenate([cos, sin], axis=1)
        q_ref[...] = _bf(_rope_apply(part(0), cos, sin) * (HEAD_DIM ** -0.5))
        k_ref[...] = _bf(_rope_apply(part(1), cos, sin))
        v_ref[...] = _bf(part(2))
        z_ref[...] = part(3)
        xbc_ref[...] = part(4)
        dt_ref[...] = part(5)
        if carry is not None:
            finish()

    row = lambda c: pl.BlockSpec((tm, c), lambda i: (i, 0))
    hbm = pl.BlockSpec(memory_space=pltpu.HBM)
    widths = [D_ATTN, D_ATTN, D_ATTN, D_SSD, D_CONV, LANES, 2 * LANES]
    dtypes = [BF16, BF16, BF16, F32, F32, F32, F32]
    res = pl.pallas_call(
        body, name="proj_in", grid=(t // tm,),
        in_specs=[row(k), _resident(w_in.shape), row(1), _resident(invf.shape), _resident(sgn.shape)] + [hbm] * nc,
        out_specs=[row(c) for c in widths] + [hbm] * nc,
        out_shape=[jax.ShapeDtypeStruct((t, c), dt) for c, dt in zip(widths, dtypes)]
        + (carry.out_shape if carry is not None else []),
        scratch_shapes=carry.scratch_shapes if carry is not None else [],
        compiler_params=_params("arbitrary" if carry is not None else "parallel"),
    )(h16, w_in, posf, invf, sgn, *(carry.operands if carry is not None else []))
    return tuple(res[:7]) + ((carry.results(res[7:]),) if carry is not None else ())


def _rope_bwd(dq, dk, cs):
    def fn(dq_v, dk_v, cs_v):
        cos, sin = cs_v[:, :LANES], -cs_v[:, LANES:]
        gq = _rope_apply(dq_v * (HEAD_DIM ** -0.5), cos, sin)
        gk = _rope_apply(dk_v, cos, sin)
        return [jnp.concatenate([gq, gk], axis=1)], []
    return _rowwise("rope_bwd", fn, [dq, dk, cs], [], [(2 * D_ATTN, BF16)])[0]


def _rms(v, w):
    return v * lax.rsqrt(jnp.mean(v * v, axis=-1, keepdims=True) + RMS_EPS) * w


def _ungroup(yg):
    w = HEADS_PER_GROUP * HEAD_DIM
    return jnp.concatenate([yg[:, GROUP_LANES * g:GROUP_LANES * g + w] for g in range(N_GROUPS)], axis=1)


def _group(xs):
    w = HEADS_PER_GROUP * HEAD_DIM
    parts = []
    for g in range(N_GROUPS):
        parts += [xs[:, w * g:w * (g + 1)], jnp.zeros((xs.shape[0], GROUP_LANES - w), xs.dtype)]
    return jnp.concatenate(parts, axis=1)


def _norms_fn(attn, yg, xs, z, w_attn, w_ssd, dskip):
    a_n = _rms(attn, w_attn)
    y = _ungroup(yg) + dskip * xs
    y_n = _rms(y * (z * _sigmoid(z)), w_ssd)
    return jnp.concatenate([a_n, y_n], axis=1)


def _norms_fwd(attn, yg, xbc, z, w_attn, w_ssd, dskip):
    def fn(*v):
        return [_norms_fn(*v)], []
    return _rowwise("norms_fwd", fn, [attn, yg, (xbc, D_SSD), z], [w_attn, w_ssd, dskip], [(D_ATTN + D_SSD, BF16)])[0]


def _norms_bwd(attn, yg, xbc, z, w_attn, w_ssd, dskip, dcat):
    def fn(attn_v, yg_v, xs_v, z_v, dcat_v, wa_v, ws_v, dk_v):
        _, vjp = jax.vjp(_norms_fn, attn_v, yg_v, xs_v, z_v, wa_v, ws_v, dk_v)
        d_attn, d_yg, d_xs, d_z, d_wa, d_ws, d_dk = vjp(dcat_v)
        return [d_attn, d_yg, d_xs, d_z], [d_wa, d_ws, d_dk]
    return _rowwise("norms_bwd", fn, [attn, yg, (xbc, D_SSD), z, dcat], [w_attn, w_ssd, dskip],
                    [(D_ATTN, F32), (N_GROUPS * GROUP_LANES, F32), (D_SSD, F32), (D_SSD, BF16)], [(1, D_SSD)] * 3)


def _spread_sum(v, e):
    h1 = _bf(v)
    r1 = v - h1.astype(F32)
    h2 = _bf(r1)
    h3 = _bf(r1 - h2.astype(F32))
    return sum(jnp.dot(h, e, preferred_element_type=F32) for h in (h1, h2, h3))


@jax.custom_vjp
def _spread(v, e, e_t):
    return _spread_sum(v, e)


def _spread_fwd(v, e, e_t):
    return _spread_sum(v, e), (e, e_t)


def _spread_bwd(saved, g):
    e, e_t = saved
    return _spread_sum(g, e_t), jnp.zeros_like(e), jnp.zeros_like(e_t)


_spread.defvjp(_spread_fwd, _spread_bwd)


def _ssd_prep_fn(xs, dtp, dtb, alog, e_x, e_xt, e_a, e_at):
    dt = jax.nn.softplus(dtp + dtb)
    a = -jnp.exp(alog)
    xdtg = _group(xs) * _spread(dt, e_x, e_xt)
    dag = _spread(dt * a, e_a, e_at)
    return xdtg, dag


def _ssd_prep_fwd(xbc, dtp, dtb, alog, spreaders):
    def fn(xbc_v, dtp_v, dtb_v, alog_v, *e_v):
        xdtg, dag = _ssd_prep_fn(xbc_v[:, :D_SSD], dtp_v, dtb_v, alog_v, *e_v)
        return [xdtg, xbc_v[:, D_SSD:], dag], []
    return _rowwise("ssd_prep_fwd", fn, [xbc, dtp], [dtb, alog, *spreaders],
                    [(N_GROUPS * GROUP_LANES, BF16), (D_CONV - D_SSD, BF16), (N_GROUPS * LANES, F32)])


def _ssd_prep_bwd(xbc, dtp, dtb, alog, spreaders, dxdtg, ddag, dxs_a, db, dc):
    def fn(xs_v, dtp_v, dxdtg_v, ddag_v, dxs_a_v, db_v, dc_v, dtb_v, alog_v, *e_v):
        _, vjp = jax.vjp(lambda a, b, c, d: _ssd_prep_fn(a, b, c, d, *e_v), xs_v, dtp_v, dtb_v, alog_v)
        dxs, ddtp, ddtb, dalog = vjp((dxdtg_v, ddag_v))
        return [jnp.concatenate([dxs + dxs_a_v, db_v, dc_v], axis=1), ddtp], [ddtb, dalog]
    return _rowwise("ssd_prep_bwd", fn, [(xbc, D_SSD), dtp, dxdtg, ddag, dxs_a, db, dc], [dtb, alog, *spreaders],
                    [(D_CONV, F32), (LANES, BF16)], [(1, LANES)] * 2)


def _shift_down(u, d):
    if d == 0:
        return u
    row = lax.broadcasted_iota(jnp.int32, u.shape, 0)
    return jnp.where(row >= d, pltpu.roll(u, d, 0), 0.0)


def _shift_up(u, d):
    if d == 0:
        return u
    s = u.shape[0]
    row = lax.broadcasted_iota(jnp.int32, u.shape, 0)
    return jnp.where(row < s - d, pltpu.roll(u, s - d, 0), 0.0)


def _conv_pre(u, w, b):
    acc = b
    for k in range(CONV_WIDTH):
        acc = acc + w[k:k + 1, :] * _shift_down(u, CONV_WIDTH - 1 - k)
    return acc


def _conv_fwd(u, w, b, *, tc=256):
    nb, s, c = u.shape

    def body(u_ref, w_ref, b_ref, o_ref):
        pre = _conv_pre(u_ref[0], w_ref[...], b_ref[...])
        o_ref[0] = pre * _sigmoid(pre)

    return pl.pallas_call(
        body, name="conv_fwd", grid=(c // tc, nb),
        in_specs=[pl.BlockSpec((1, s, tc), lambda j, i: (i, 0, j)), pl.BlockSpec((CONV_WIDTH, tc), lambda j, i: (0, j)),
                  pl.BlockSpec((1, tc), lambda j, i: (0, j))],
        out_specs=pl.BlockSpec((1, s, tc), lambda j, i: (i, 0, j)),
        out_shape=jax.ShapeDtypeStruct((nb, s, c), F32),
        compiler_params=_params("parallel", "parallel"),
    )(u, w, b)


def _conv_bwd(u, w, b, dout, *, tc=256):
    nb, s, c = u.shape

    def body(u_ref, w_ref, b_ref, d_ref, du_ref, dw_ref, db_ref):
        uv, wv = u_ref[0], w_ref[...]
        pre = _conv_pre(uv, wv, b_ref[...])
        sig = _sigmoid(pre)
        dpre = d_ref[0] * (sig * (1.0 + pre * (1.0 - sig)))
        du = jnp.zeros_like(uv)
        dws = []
        for k in range(CONV_WIDTH):
            du = du + wv[k:k + 1, :] * _shift_up(dpre, CONV_WIDTH - 1 - k)
            dws.append(jnp.sum(dpre * _shift_down(uv, CONV_WIDTH - 1 - k), axis=0, keepdims=True))
        du_ref[0] = _bf(du)
        dwv = jnp.concatenate(dws + [jnp.zeros((8 - CONV_WIDTH, tc), F32)], axis=0)
        dbv = jnp.sum(dpre, axis=0, keepdims=True)
        first = pl.program_id(1) == 0

        @pl.when(first)
        def _():
            dw_ref[...] = dwv
            db_ref[...] = dbv

        @pl.when(jnp.logical_not(first))
        def _():
            dw_ref[...] += dwv
            db_ref[...] += dbv

    blk = pl.BlockSpec((1, s, tc), lambda j, i: (i, 0, j))
    return pl.pallas_call(
        body, name="conv_bwd", grid=(c // tc, nb),
        in_specs=[blk, pl.BlockSpec((CONV_WIDTH, tc), lambda j, i: (0, j)), pl.BlockSpec((1, tc), lambda j, i: (0, j)), blk],
        out_specs=[blk, pl.BlockSpec((8, tc), lambda j, i: (0, j)), pl.BlockSpec((1, tc), lambda j, i: (0, j))],
        out_shape=[jax.ShapeDtypeStruct((nb, s, c), BF16), jax.ShapeDtypeStruct((8, c), F32), jax.ShapeDtypeStruct((1, c), F32)],
        compiler_params=_params("parallel", "arbitrary"),
    )(u, w, b, dout)


FWD_KEY_BLOCK = 256


def _branch_bias_table(seq, kb):
    ratio = SEQ_BLOCK // kb
    key = np.arange(kb)[None, :, None]
    query = np.arange(SEQ_BLOCK)[None, None, :]
    delta = (np.arange(seq // kb)[:, None, None] - (ratio - 1)) * kb + query - key
    cnt = np.zeros(delta.shape, np.float64)
    for window, dilation in ((128, 1), (512, 4), (2048, 16)):
        cnt += (delta >= 0) & (delta % dilation == 0) & (delta <= window)
    return jnp.asarray(np.where(cnt > 0, np.log(np.maximum(cnt, 1.0)), NEG).astype(np.float32))


HEADS_PER_BLOCK = LANES // HEAD_DIM


def _head_rows(v, h):
    row = lax.broadcasted_iota(jnp.int32, v.shape, 0)
    return jnp.where((row >= HEAD_DIM * h) & (row < HEAD_DIM * (h + 1)), v, jnp.zeros_like(v))


def _attn_fwd(q, k, v, bias):
    nb_, s, _ = q.shape
    ab, kb = SEQ_BLOCK, FWD_KEY_BLOCK
    nblk, nkb, ratio = s // ab, s // kb, ab // kb

    def body(q_ref, k_ref, v_ref, b_ref, o_ref, lse_ref, vt_s):
        i = pl.program_id(2)

        @pl.when(i == 0)
        def _():
            for jb in range(nkb):
                vt_s[jb] = v_ref[0, kb * jb:kb * (jb + 1), :].T

        qt = q_ref[0].T
        qts = [_head_rows(qt, h) for h in range(HEADS_PER_BLOCK)]

        last = ratio * (i + 1) - 1

        def scores(j):
            kj = k_ref[0, pl.ds(pl.multiple_of(j * kb, kb), kb), :]
            return [jnp.dot(kj, qts[h], preferred_element_type=F32) for h in range(HEADS_PER_BLOCK)]

        def step(j, carry):
            ahead = scores(jnp.minimum(j + 1, last))
            lb = b_ref[ratio * i - j + (ratio - 1)]
            out = []
            for h in range(HEADS_PER_BLOCK):
                m, l, acc = carry[3 * h:3 * h + 3]
                st = carry[3 * HEADS_PER_BLOCK + h] + lb
                m_new = jnp.maximum(m, jnp.max(st, axis=0, keepdims=True))
                p = jnp.exp(st - m_new)
                a = jnp.exp(m - m_new)
                l = a * l + jnp.sum(p, axis=0, keepdims=True)
                vt = vt_s[j, HEAD_DIM * h:HEAD_DIM * (h + 1), :]
                acc = a * acc + jnp.dot(vt, _bf(p), preferred_element_type=F32)
                out += [m_new, l, acc]
            return tuple(out) + tuple(ahead)

        init = (jnp.full((1, ab), NEG, F32), jnp.zeros((1, ab), F32), jnp.zeros((HEAD_DIM, ab), F32)) * HEADS_PER_BLOCK
        res = lax.fori_loop(0, ratio * (i + 1), step, init + tuple(scores(0)))
        ot = jnp.concatenate([res[3 * h + 2] / res[3 * h + 1] for h in range(HEADS_PER_BLOCK)], axis=0)
        o_ref[0] = ot.T
        rows = [res[3 * h] + jnp.log(res[3 * h + 1]) for h in range(HEADS_PER_BLOCK)]
        lse_ref[0, 0, 0] = jnp.concatenate(rows + [jnp.zeros((8 - HEADS_PER_BLOCK, ab), F32)], axis=0)

    qblk = pl.BlockSpec((1, ab, LANES), lambda b, hp, i: (b, i, hp))
    full = pl.BlockSpec((1, s, LANES), lambda b, hp, i: (b, 0, hp))
    return pl.pallas_call(
        body, name="attn_fwd", grid=(nb_, D_ATTN // LANES, nblk),
        in_specs=[qblk, full, full, pl.BlockSpec((nkb, kb, ab), lambda b, hp, i: (0, 0, 0))],
        out_specs=[qblk, pl.BlockSpec((1, 1, 1, 8, ab), lambda b, hp, i: (b, hp, i, 0, 0))],
        out_shape=[jax.ShapeDtypeStruct((nb_, s, D_ATTN), F32),
                   jax.ShapeDtypeStruct((nb_, D_ATTN // LANES, nblk, 8, ab), F32)],
        scratch_shapes=[pltpu.VMEM((nkb, LANES, kb), BF16)],
        compiler_params=_params("parallel", "parallel", "arbitrary"),
    )(q, k, v, bias)


def _attn_bwd(q, k, v, o, do, lse, bias):
    nb_, s, _ = q.shape
    ab = SEQ_BLOCK
    nblk = s // ab

    nh = HEADS_PER_BLOCK

    def body(q_ref, k_ref, v_ref, o_ref, do_ref, lse_ref, b_ref, dq_ref, dk_ref, dv_ref,
             qt_s, dot_s, kt_s, dqt_s, do16_s, d_s, dk_acc, dv_acc):
        for jb in range(nblk):
            sl = slice(ab * jb, ab * (jb + 1))
            qt, kt = q_ref[0, sl, :].T, k_ref[0, sl, :].T
            do = do_ref[0, sl, :]
            dot = do.T
            prod = dot * o_ref[0, sl, :].T
            do16_s[sl, :] = _bf(do)
            for h in range(nh):
                qt_s[nh * jb + h] = _head_rows(qt, h)
                kt_s[nh * jb + h] = _head_rows(kt, h)
                dot_s[nh * jb + h] = _head_rows(_bf(dot), h)
            d_s[jb] = jnp.concatenate(
                [jnp.sum(prod[HEAD_DIM * h:HEAD_DIM * (h + 1)], axis=0, keepdims=True) for h in range(nh)]
                + [jnp.zeros((8 - nh, ab), F32)], axis=0)
            dqt_s[jb] = jnp.zeros((LANES, ab), F32)

        def outer(j, carry):
            ks = pl.ds(pl.multiple_of(j * ab, ab), ab)
            kj, vj = k_ref[0, ks, :], v_ref[0, ks, :]
            dk_acc[...] = jnp.zeros_like(dk_acc)
            dv_acc[...] = jnp.zeros_like(dv_acc)

            def inner(i, c2):
                qs = pl.ds(pl.multiple_of(i * ab, ab), ab)
                qi, doi = q_ref[0, qs, :], do16_s[qs, :]
                lb = b_ref[i - j]
                for h in range(nh):
                    st = jnp.dot(kj, qt_s[nh * i + h], preferred_element_type=F32) + lb
                    pt = jnp.exp(st - lse_ref[0, 0, i, h:h + 1, :])
                    dpt = jnp.dot(vj, dot_s[nh * i + h], preferred_element_type=F32)
                    dst16 = _bf(pt * (dpt - d_s[i, h:h + 1, :]))
                    dv_acc[h] += jnp.dot(_bf(pt), doi, preferred_element_type=F32)
                    dk_acc[h] += jnp.dot(dst16, qi, preferred_element_type=F32)
                    dqt_s[i] += jnp.dot(kt_s[nh * j + h], dst16, preferred_element_type=F32)
                return c2

            lax.fori_loop(j, nblk, inner, 0)
            lane = lax.broadcasted_iota(jnp.int32, (ab, LANES), 1)
            dk_ref[0, ks, :] = jnp.where(lane < HEAD_DIM, dk_acc[0], dk_acc[1])
            dv_ref[0, ks, :] = _bf(jnp.where(lane < HEAD_DIM, dv_acc[0], dv_acc[1]))
            return carry

        lax.fori_loop(0, nblk, outer, 0)
        for jb in range(nblk):
            dq_ref[0, ab * jb:ab * (jb + 1), :] = dqt_s[jb].T

    assert nh == 2
    full = pl.BlockSpec((1, s, LANES), lambda b, hp: (b, 0, hp))
    return pl.pallas_call(
        body, name="attn_bwd", grid=(nb_, D_ATTN // LANES),
        in_specs=[full] * 5 + [pl.BlockSpec((1, 1, nblk, 8, ab), lambda b, hp: (b, hp, 0, 0, 0)),
                               pl.BlockSpec((nblk, ab, ab), lambda b, hp: (0, 0, 0))],
        out_specs=[full, full, full],
        out_shape=[jax.ShapeDtypeStruct((nb_, s, D_ATTN), F32), jax.ShapeDtypeStruct((nb_, s, D_ATTN), F32),
                   jax.ShapeDtypeStruct((nb_, s, D_ATTN), BF16)],
        scratch_shapes=[pltpu.VMEM((nh * nblk, LANES, ab), BF16), pltpu.VMEM((nh * nblk, LANES, ab), BF16),
                        pltpu.VMEM((nh * nblk, LANES, ab), BF16), pltpu.VMEM((nblk, LANES, ab), F32),
                        pltpu.VMEM((s, LANES), BF16), pltpu.VMEM((nblk, 8, ab), F32),
                        pltpu.VMEM((nh, ab, LANES), F32), pltpu.VMEM((nh, ab, LANES), F32)],
        compiler_params=_params("parallel", "parallel"),
    )(q, k, v, o, do, lse, bias)


def _cumsum_fwd(dag):
    nb_, s, c = dag.shape
    ab = SEQ_BLOCK

    def body(a_ref, o_ref, ot_ref):
        r = lax.broadcasted_iota(jnp.int32, (ab, ab), 0)
        cc = lax.broadcasted_iota(jnp.int32, (ab, ab), 1)
        tri = (r >= cc).astype(F32)
        carry = jnp.zeros((1, c), F32)
        for i in range(s // ab):
            loc = jnp.dot(tri, a_ref[0, ab * i:ab * (i + 1), :], precision=HIGHEST, preferred_element_type=F32) + carry
            o_ref[0, ab * i:ab * (i + 1), :] = loc
            ot_ref[0, :, ab * i:ab * (i + 1)] = loc.T
            carry = loc[ab - 1:ab, :]

    return pl.pallas_call(
        body, name="ssd_cumsum", grid=(nb_,),
        in_specs=[pl.BlockSpec((1, s, c), lambda b: (b, 0, 0))],
        out_specs=[pl.BlockSpec((1, s, c), lambda b: (b, 0, 0)), pl.BlockSpec((1, c, s), lambda b: (b, 0, 0))],
        out_shape=[jax.ShapeDtypeStruct((nb_, s, c), F32), jax.ShapeDtypeStruct((nb_, c, s), F32)],
        compiler_params=_params("parallel"),
    )(dag)


def _cumsum_bwd(dcol, drow):
    nb_, s, c = dcol.shape
    ab = SEQ_BLOCK

    def body(c_ref, r_ref, o_ref):
        r = lax.broadcasted_iota(jnp.int32, (ab, ab), 0)
        cc = lax.broadcasted_iota(jnp.int32, (ab, ab), 1)
        tri = (r <= cc).astype(F32)
        carry = jnp.zeros((1, c), F32)
        for i in reversed(range(s // ab)):
            rows = r_ref[0, :, ab * i:ab * (i + 1)].T
            parts = []
            for g in range(N_GROUPS):
                parts += [rows[:, 8 * g:8 * (g + 1)], jnp.zeros((ab, LANES - 8), F32)]
            blk = c_ref[0, ab * i:ab * (i + 1), :] + jnp.concatenate(parts, axis=1)
            loc = jnp.dot(tri, blk, precision=HIGHEST, preferred_element_type=F32) + carry
            o_ref[0, ab * i:ab * (i + 1), :] = loc
            carry = loc[0:1, :]

    return pl.pallas_call(
        body, name="ssd_cumsum_bwd", grid=(nb_,),
        in_specs=[pl.BlockSpec((1, s, c), lambda b: (b, 0, 0)), pl.BlockSpec((1, N_GROUPS * 8, s), lambda b: (b, 0, 0))],
        out_specs=pl.BlockSpec((1, s, c), lambda b: (b, 0, 0)),
        out_shape=jax.ShapeDtypeStruct((nb_, s, c), F32),
        compiler_params=_params("parallel"),
    )(dcol, drow)


def _causal_ok(i, j):
    ab = SEQ_BLOCK
    r = lax.broadcasted_iota(jnp.int32, (ab, ab), 0)
    c = lax.broadcasted_iota(jnp.int32, (ab, ab), 1)
    return (r + (i - j) * ab) >= c


def _causal_ok_t(i, j):
    ab = SEQ_BLOCK
    r = lax.broadcasted_iota(jnp.int32, (ab, ab), 0)
    c = lax.broadcasted_iota(jnp.int32, (ab, ab), 1)
    return (c + (i - j) * ab) >= r


def _ssd_chunk(s_in, x, bm_t, cm, cb, acol, arow, a_prev, ok):
    q = x.shape[0]
    decay = jnp.exp(jnp.where(ok, acol - arow, NEG))
    y = jnp.dot(_bf(cb * decay), x, preferred_element_type=F32)
    y = y + jnp.exp(acol - a_prev) * jnp.dot(cm, _bf(s_in), preferred_element_type=F32)
    a_end = acol[q - 1:q, :]
    wx = _bf(jnp.exp(a_end - acol) * x.astype(F32))
    s_out = jnp.exp(a_end - a_prev) * s_in + jnp.dot(bm_t, wx, preferred_element_type=F32)
    return y, s_out


def _ssd_specs(s):
    xblk = pl.BlockSpec((1, s, GROUP_LANES), lambda b, g: (b, 0, g))
    bblk = pl.BlockSpec((1, s, D_STATE), lambda b, g: (b, 0, g))
    cblk = pl.BlockSpec((1, s, D_STATE), lambda b, g: (b, 0, N_GROUPS + g))
    tblk = pl.BlockSpec((1, 8, s), lambda b, g: (b, (LANES // 8) * g, 0))
    return xblk, bblk, cblk, tblk


def _chunk_views(i, j, x_ref, ac_ref, at_ref):
    ab = SEQ_BLOCK
    sl = slice(ab * i, ab * (i + 1))
    hs = slice(HEAD_DIM * j, HEAD_DIM * (j + 1))
    a_prev = jnp.zeros((1, 1), F32) if i == 0 else ac_ref[0, ab * i - 1:ab * i, j:j + 1]
    return sl, hs, ac_ref[0, sl, j:j + 1], at_ref[0, j:j + 1, sl], a_prev


def _ssd_fwd_chunked(xdtg, bc, acum, acum_t):
    nb_, s, _ = xdtg.shape
    ab = SEQ_BLOCK
    hpg = HEADS_PER_GROUP

    def body(x_ref, b_ref, c_ref, ac_ref, at_ref, y_ref):
        ok = _causal_ok(0, 0)
        states = [jnp.zeros((D_STATE, HEAD_DIM), F32) for _ in range(hpg)]
        for i in range(s // ab):
            bm, cm = b_ref[0, ab * i:ab * (i + 1), :], c_ref[0, ab * i:ab * (i + 1), :]
            bm_t = bm.T
            cb = jnp.dot(cm, bm_t, preferred_element_type=F32)
            ys = []
            for j in range(hpg):
                sl, hs, acol, arow, a_prev = _chunk_views(i, j, x_ref, ac_ref, at_ref)
                y, states[j] = _ssd_chunk(states[j], x_ref[0, sl, hs], bm_t, cm, cb, acol, arow, a_prev, ok)
                ys.append(y)
            y_ref[0, sl, :] = jnp.concatenate(ys + [jnp.zeros((ab, GROUP_LANES - hpg * HEAD_DIM), F32)], axis=1)

    xblk, bblk, cblk, tblk = _ssd_specs(s)
    ablk = pl.BlockSpec((1, s, LANES), lambda b, g: (b, 0, g))
    return pl.pallas_call(
        body, name="ssd_fwd", grid=(nb_, N_GROUPS), in_specs=[xblk, bblk, cblk, ablk, tblk], out_specs=xblk,
        out_shape=jax.ShapeDtypeStruct((nb_, s, N_GROUPS * GROUP_LANES), F32),
        compiler_params=_params("parallel", "parallel"),
    )(xdtg, bc, bc, acum, acum_t)


def _ssd_bwd_chunked(xdtg, bc, acum, acum_t, dyg):
    nb_, s, _ = xdtg.shape
    ab = SEQ_BLOCK
    nblk = s // ab
    hpg = HEADS_PER_GROUP

    def body(x_ref, b_ref, c_ref, ac_ref, at_ref, dy_ref, dx_ref, db_ref, dc_ref, dac_ref, dar_ref, s_s):
        ok = _causal_ok(0, 0)
        dx_ref[...] = jnp.zeros_like(dx_ref)
        dac_ref[...] = jnp.zeros_like(dac_ref)
        dar_ref[...] = jnp.zeros_like(dar_ref)
        states = [jnp.zeros((D_STATE, HEAD_DIM), F32) for _ in range(hpg)]
        for i in range(nblk):
            bm_t = b_ref[0, ab * i:ab * (i + 1), :].T
            for j in range(hpg):
                sl, hs, acol, arow, a_prev = _chunk_views(i, j, x_ref, ac_ref, at_ref)
                s_s[hpg * i + j] = states[j]
                if i + 1 < nblk:
                    a_end = acol[ab - 1:ab, :]
                    wx = _bf(jnp.exp(a_end - acol) * x_ref[0, sl, hs].astype(F32))
                    states[j] = jnp.exp(a_end - a_prev) * states[j] + jnp.dot(bm_t, wx, preferred_element_type=F32)
        ok_t = _causal_ok_t(0, 0)
        last_row = lax.broadcasted_iota(jnp.int32, (ab, 1), 0) == ab - 1
        d_state = [jnp.zeros((D_STATE, HEAD_DIM), F32) for _ in range(hpg)]
        pending = [jnp.zeros((1, 1), F32) for _ in range(hpg)]
        total = lambda v: jnp.sum(v, keepdims=True)
        for i in reversed(range(nblk)):
            bm, cm = b_ref[0, ab * i:ab * (i + 1), :], c_ref[0, ab * i:ab * (i + 1), :]
            cm_t = cm.T
            cbt = jnp.dot(bm, cm_t, preferred_element_type=F32)
            dcbt = jnp.zeros((ab, ab), F32)
            d_bm, d_cm = jnp.zeros((ab, D_STATE), F32), jnp.zeros((ab, D_STATE), F32)
            for j in range(hpg):
                sl, hs, acol, arow, a_prev = _chunk_views(i, j, x_ref, ac_ref, at_ref)
                x, dy = x_ref[0, sl, hs], dy_ref[0, sl, hs]
                dy16 = _bf(dy)
                s_in, g_out = s_s[hpg * i + j], d_state[j]
                s16, g16 = _bf(s_in), _bf(g_out)
                decay = jnp.exp(jnp.where(ok_t, arow - acol, NEG))
                gt = cbt * decay
                dgt = lax.dot_general(x, dy16, _NT, preferred_element_type=F32)
                d_x = jnp.dot(_bf(gt), dy16, preferred_element_type=F32)
                dcbt = dcbt + dgt * decay
                mm = dgt * gt
                d_arow = jnp.sum(mm, axis=0, keepdims=True)
                d_acol = -jnp.sum(mm, axis=1, keepdims=True)
                e = jnp.exp(acol - a_prev)
                edy16 = _bf(e * dy)
                d_cm = d_cm + lax.dot_general(edy16, s16, _NT, preferred_element_type=F32)
                d_s = jnp.dot(cm_t, edy16, preferred_element_type=F32)
                de_e = jnp.sum(dy * jnp.dot(cm, s16, preferred_element_type=F32), axis=1, keepdims=True) * e
                a_end = acol[ab - 1:ab, :]
                w = jnp.exp(a_end - acol)
                f = jnp.exp(a_end - a_prev)
                x32 = x.astype(F32)
                bg = jnp.dot(bm, g16, preferred_element_type=F32)
                d_x = d_x + w * bg
                d_bm = d_bm + lax.dot_general(_bf(w * x32), g16, _NT, preferred_element_type=F32)
                dw_w = jnp.sum(bg * x32, axis=1, keepdims=True) * w
                df_f = total(g_out * s_in) * f
                d_end = total(dw_w) + df_f
                d_acol = d_acol + de_e - dw_w + jnp.where(last_row, d_end + pending[j], 0.0)
                pending[j] = -total(de_e) - df_f
                d_state[j] = d_s + f * g_out
                dx_ref[0, sl, hs] = d_x
                dac_ref[0, sl, j:j + 1] = d_acol
                dar_ref[0, j:j + 1, sl] = d_arow
            dcbt16 = _bf(dcbt)
            db_ref[0, ab * i:ab * (i + 1), :] = d_bm + jnp.dot(dcbt16, cm, preferred_element_type=F32)
            dc_ref[0, ab * i:ab * (i + 1), :] = d_cm + lax.dot_general(dcbt16, bm, _TN, preferred_element_type=F32)

    xblk, bblk, cblk, tblk = _ssd_specs(s)
    ablk = pl.BlockSpec((1, s, LANES), lambda b, g: (b, 0, g))
    return pl.pallas_call(
        body, name="ssd_bwd", grid=(nb_, N_GROUPS),
        in_specs=[xblk, bblk, cblk, ablk, tblk, xblk],
        out_specs=[xblk, bblk, bblk, ablk, pl.BlockSpec((1, 8, s), lambda b, g: (b, g, 0))],
        out_shape=[jax.ShapeDtypeStruct((nb_, s, N_GROUPS * GROUP_LANES), F32),
                   jax.ShapeDtypeStruct((nb_, s, N_GROUPS * D_STATE), F32),
                   jax.ShapeDtypeStruct((nb_, s, N_GROUPS * D_STATE), F32),
                   jax.ShapeDtypeStruct((nb_, s, N_GROUPS * LANES), F32),
                   jax.ShapeDtypeStruct((nb_, N_GROUPS * 8, s), F32)],
        scratch_shapes=[pltpu.VMEM((nblk * hpg, D_STATE, HEAD_DIM), F32)],
        compiler_params=_params("parallel", "parallel"),
    )(xdtg, bc, bc, acum, acum_t, dyg)


def _interleave(wg, wu):
    k, f = wg.shape
    gi = GATE_UP_INTERLEAVE
    return jnp.stack([wg.reshape(k, f // gi, gi), wu.reshape(k, f // gi, gi)], axis=2).reshape(k, 2 * f)


def _head_expanders():
    e_x = np.zeros((LANES, N_GROUPS * GROUP_LANES), np.float32)
    e_a = np.zeros((LANES, N_GROUPS * LANES), np.float32)
    for h in range(N_HEADS):
        g, j = divmod(h, HEADS_PER_GROUP)
        e_x[h, GROUP_LANES * g + HEAD_DIM * j:GROUP_LANES * g + HEAD_DIM * (j + 1)] = 1.0
        e_a[h, LANES * g + j] = 1.0
    return [jnp.asarray(m, BF16) for m in (e_x, e_x.T, e_a, e_a.T)]


def _pad_lanes(v, n=LANES):
    return jnp.pad(v, ((0, 0), (0, n - v.shape[1])))


def _local_step(x, positions, target, w, late=None, early_grad_job=None):
    nb, s, d = x.shape
    t = nb * s
    x2 = x.reshape(t, d)
    tgt2 = target.reshape(t, d)
    (job_a, weights_a), (job_b, weights_b) = late if late is not None else ((None, None), (None, None))

    x16 = _bf(x2)
    wgu1 = _interleave(w["ffn1_gate"], w["ffn1_up"])
    ffn1 = _ffn_fwd("ffn1_fwd", x16, x2, wgu1, w["ffn1_down"], w["ln1_g"], w["ln1_b"], carry=job_a)
    au1, hm1, h1, r1, h1_16 = ffn1[:5]
    if job_a is not None:
        w = {**w, **weights_a(ffn1[5])}

    w_in = w["w_in"]
    wqk, wv, wz = w_in[:, :2 * D_ATTN], w_in[:, 2 * D_ATTN:3 * D_ATTN], w_in[:, 3 * D_ATTN:3 * D_ATTN + D_SSD]
    wxbc = w_in[:, 3 * D_ATTN + D_SSD:3 * D_ATTN + D_SSD + D_CONV]
    wdt = _pad_lanes(w_in[:, 3 * D_ATTN + D_SSD + D_CONV:])

    inv_freq = ROPE_THETA ** (-jnp.arange(0, ROPE_DIM, 2, dtype=F32) / ROPE_DIM)
    half = ROPE_DIM // 2
    head_invf = jnp.concatenate([inv_freq, inv_freq, jnp.zeros((HEAD_DIM - ROPE_DIM,), F32)])
    head_sgn = jnp.concatenate([-jnp.ones((half,), F32), jnp.ones((half,), F32), jnp.zeros((HEAD_DIM - ROPE_DIM,), F32)])
    invf = jnp.tile(head_invf, LANES // HEAD_DIM)[None, :]
    sgn = jnp.tile(head_sgn, LANES // HEAD_DIM)[None, :]
    posf = positions.astype(F32).reshape(t, 1)
    bias_fwd, bias_bwd = _branch_bias_table(s, FWD_KEY_BLOCK), _branch_bias_table(s, SEQ_BLOCK)
    spreaders = _head_expanders()
    dtb, alog = _pad_lanes(w["dt_bias"]), _pad_lanes(w["a_log"])
    dskip = jnp.repeat(w["d_skip"], HEAD_DIM, axis=1)

    proj = _proj_in(h1_16, _pad_lanes(w_in, w_in.shape[1] - N_HEADS + LANES), posf, invf, sgn, carry=job_b)
    q16, k16, v16, z, xbc_pre, dtp, cs = proj[:7]
    if job_b is not None:
        w = {**w, **weights_b(proj[7])}
    wgu2 = _interleave(w["ffn2_gate"], w["ffn2_up"])
    to3 =lambda a: a.reshape(nb, s, a.shape[-1])
    attn_o, lse = _attn_fwd(to3(q16), to3(k16), to3(v16), bias_fwd)

    xbc = _conv_fwd(to3(xbc_pre), w["conv_w"], w["conv_b"]).reshape(t, D_CONV)
    xdtg, bc16, dag = _ssd_prep_fwd(xbc, dtp, dtb, alog, spreaders)
    acum, acum_t = _cumsum_fwd(to3(dag))
    yg = _ssd_fwd_chunked(to3(xdtg), to3(bc16), acum, acum_t)

    cat = _norms_fwd(attn_o.reshape(t, D_ATTN), yg.reshape(t, -1), xbc, z, w["attn_norm_w"], w["ssd_norm_w"], dskip)
    h2, r2, h2_16 = _mm_res_ln("w_out_ln2", cat, w["w_out"], h1, w["ln2_g"], w["ln2_b"], scale=1.0)

    au2, hm2, _, r3, _ = _ffn_fwd("ffn2_fwd", h2_16, h2, wgu2, w["ffn2_down"], w["ln3_g"], w["ln3_b"])

    g = {}
    dr3, dr3_16, g["ln3_g"], g["ln3_b"], loss = _ln_loss_bwd("loss_ln3_bwd", r3, w["ln3_g"], w["ln3_b"], tgt2)

    dau2, dh2 = _ffn_bwd("ffn2_bwd", dr3_16, dr3, w["ffn2_down"].T * 0.5, au2, wgu2.T)
    g["ffn2_down"] = _mm_tn("ffn2_down_dw", hm2, dr3_16, scale=0.5, tk=D_FF // 2, tn=512)
    g["ffn2_gate"], g["ffn2_up"] = _mm_tn_gate_up("ffn2_up_dw", h2_16, dau2)

    dr2, dr2_16, g["ln2_g"], g["ln2_b"] = _ln_bwd("ln2_bwd", r2, w["ln2_g"], w["ln2_b"], dh2)
    dcat = _mm("w_out_dx", [(dr2_16, w["w_out"].T)], tm=1024, tn=768)
    g["w_out"] = _mm_tn("w_out_dw", cat, dr2_16, tk=768, tn=1024)

    d_attn, dyg, dxs_a, dz16, g["attn_norm_w"], g["ssd_norm_w"], ddskip = _norms_bwd(
        attn_o.reshape(t, D_ATTN), yg.reshape(t, -1), xbc, z, w["attn_norm_w"], w["ssd_norm_w"], dskip, dcat)
    g["d_skip"] = ddskip.reshape(N_HEADS, HEAD_DIM).sum(axis=1)[None, :]

    dq, dk, dv16 = _attn_bwd(to3(q16), to3(k16), to3(v16), attn_o, to3(d_attn), lse, bias_bwd)
    dqk16 = _rope_bwd(dq.reshape(t, D_ATTN), dk.reshape(t, D_ATTN), cs)

    dxdtg, dbm, dcm, dacol, darow = _ssd_bwd_chunked(to3(xdtg), to3(bc16), acum, acum_t, to3(dyg))
    ddag = _cumsum_bwd(dacol, darow)
    dxbc, ddtp16, ddtb, dalog = _ssd_prep_bwd(xbc, dtp, dtb, alog, spreaders, dxdtg.reshape(t, -1), ddag.reshape(t, -1),
                                               dxs_a, dbm.reshape(t, -1), dcm.reshape(t, -1))
    g["dt_bias"], g["a_log"] = ddtb[:, :N_HEADS], dalog[:, :N_HEADS]
    dxbc_pre16, dconv_w, g["conv_b"] = _conv_bwd(to3(xbc_pre), w["conv_w"], w["conv_b"], to3(dxbc))
    g["conv_w"] = dconv_w[:CONV_WIDTH]
    dxbc_pre16 = dxbc_pre16.reshape(t, D_CONV)
    dv16 = dv16.reshape(t, D_ATTN)

    dh1 = _mm("w_in_dx", [(dqk16, wqk.T), (dv16, wv.T), (dz16, wz.T), (dxbc_pre16, wxbc.T), (ddtp16, wdt.T)],
              res=dr2, res_scale=ALPHA, tm=1024)
    g["w_in"] = _mm_tn_sections("w_in_dw", h1_16, [dqk16, dv16, dz16, dxbc_pre16, ddtp16])[:, :w_in.shape[1]]

    dr1, dr1_16, g["ln1_g"], g["ln1_b"] = _ln_bwd("ln1_bwd", r1, w["ln1_g"], w["ln1_b"], dh1)
    g["ffn1_down"] = _mm_tn("ffn1_down_dw", hm1, dr1_16, scale=0.5, tk=D_FF // 2, tn=512)
    ffn1b = _ffn_bwd("ffn1_bwd", dr1_16, dr1, w["ffn1_down"].T * 0.5, au1, wgu1.T,
                     carry=None if early_grad_job is None else early_grad_job(g))
    dau1, dx = ffn1b[:2]
    early = ffn1b[2] if early_grad_job is not None else None
    g["ffn1_gate"], g["ffn1_up"] = _mm_tn_gate_up("ffn1_up_dw", x16, dau1)
    return loss, dx.reshape(nb, s, d), g, early


_HBM = pl.BlockSpec(memory_space=pltpu.HBM)
N_CHIPS = 4
N_DEVICES = 8


def _place():
    return lax.axis_index("x"), lax.axis_index("y"), lax.axis_index("c")


def _other_chips(x, y):
    return [(1 - x, y), (x, 1 - y), (1 - x, 1 - y)]


class _GatherJob:
    def __init__(self, shards):
        assert all((a.shape[0] // 2) % 16 == 0 for a in shards)
        self.n = len(shards)
        self.shapes = [a.shape for a in shards]
        self.operands = [a.reshape(2, a.shape[0] // 2, a.shape[1]) for a in shards]
        self.out_shape = [jax.ShapeDtypeStruct((N_CHIPS,) + a.shape, a.dtype) for a in self.operands]
        pair = pltpu.SemaphoreType.DMA((self.n, N_CHIPS - 1))
        one = pltpu.SemaphoreType.DMA((self.n,))
        self.scratch_shapes = [pair, pair, pair, pair, one, one]

    def results(self, outs):
        return [o.reshape((N_CHIPS,) + s) for o, s in zip(outs, self.shapes)]

    def phases(self, ins, outs, sems):
        n = self.n
        send_sems, recv_sems, fwd_send_sems, fwd_recv_sems, own_send_sems, own_recv_sems = sems
        x, y, c = _place()
        me = 2 * x + y
        peers = _other_chips(x, y)

        def own(t):
            return pltpu.make_async_remote_copy(ins[t], outs[t].at[me], own_send_sems.at[t], own_recv_sems.at[t],
                                                device_id=(x, y, 1 - c), device_id_type=MESH)

        def ici(t, p, src_chip):
            px, py = peers[p]
            return pltpu.make_async_remote_copy(
                ins[t].at[c] if src_chip is None else outs[t].at[src_chip, c],
                outs[t].at[me if src_chip is None else src_chip, c],
                send_sems.at[t, p], recv_sems.at[t, p], device_id=(px, py, c), device_id_type=MESH)

        def d2d(t, p, core):
            px, py = peers[p]
            return pltpu.make_async_remote_copy(
                outs[t].at[2 * px + py, core], outs[t].at[2 * px + py, core],
                fwd_send_sems.at[t, p], fwd_recv_sems.at[t, p], device_id=(x, y, 1 - c), device_id_type=MESH)

        pairs = [(t, p) for t in range(n) for p in range(N_CHIPS - 1)]

        def start():
            for t, p in pairs:
                ici(t, p, None).start()
            for t in range(n):
                own(t).start()

        def forward():
            for t, p in pairs:
                px, py = peers[p]
                ici(t, p, 2 * px + py).wait_recv()
                d2d(t, p, c).start()

        def finish():
            for t, p in pairs:
                d2d(t, p, 1 - c).wait_recv()
            for t in range(n):
                own(t).wait()
            for t, p in pairs:
                ici(t, p, None).wait_send()
                d2d(t, p, c).wait_send()

        return start, forward, finish


class _ExchangeJob:
    def __init__(self, stacks):
        self.n = len(stacks)
        self.operands = list(stacks)
        self.out_shape = [jax.ShapeDtypeStruct(a.shape, a.dtype) for a in stacks]
        pair = pltpu.SemaphoreType.DMA((self.n, N_CHIPS - 1))
        self.scratch_shapes = [pair, pair]

    def results(self, outs):
        return list(outs)

    def phases(self, ins, outs, sems):
        send_sems, recv_sems = sems
        x, y, c = _place()
        me = 2 * x + y
        peers = _other_chips(x, y)
        pairs = [(t, p) for t in range(self.n) for p in range(N_CHIPS - 1)]

        def copy(t, p):
            px, py = peers[p]
            return pltpu.make_async_remote_copy(ins[t].at[2 * px + py], outs[t].at[me], send_sems.at[t, p],
                                                recv_sems.at[t, p], device_id=(px, py, c), device_id_type=MESH)

        def arrival(t, p):
            px, py = peers[p]
            return pltpu.make_async_remote_copy(ins[t].at[me], outs[t].at[2 * px + py], send_sems.at[t, p],
                                                recv_sems.at[t, p], device_id=(px, py, c), device_id_type=MESH)

        def start():
            for t, p in pairs:
                copy(t, p).start()

        def finish():
            for t, p in pairs:
                arrival(t, p).wait_recv()
            for t, p in pairs:
                copy(t, p).wait_send()

        return start, None, finish


def _run_job(job, name):
    n = job.n

    def body(*refs):
        for phase in job.phases(refs[:n], refs[n:2 * n], refs[2 * n:]):
            if phase is not None:
                phase()

    outs = pl.pallas_call(
        body, name=name, in_specs=[_HBM] * n, out_specs=[_HBM] * n,
        out_shape=job.out_shape, scratch_shapes=job.scratch_shapes,
    )(*job.operands)
    return job.results(outs)


def _sibling_halves(stacks, name):
    n = len(stacks)
    halves = [a.shape[1] // 2 for a in stacks]
    split = [a.reshape(a.shape[0], 2, h, a.shape[2]) for a, h in zip(stacks, halves)]

    def body(*refs):
        ins, outs = refs[:n], refs[n:2 * n]
        send_sems, recv_sems = refs[2 * n:]
        x, y, c = _place()
        cps = []
        for t in range(n):
            cp = pltpu.make_async_remote_copy(ins[t].at[:, 1 - c], outs[t], send_sems.at[t], recv_sems.at[t],
                                              device_id=(x, y, 1 - c), device_id_type=MESH)
            cp.start()
            cps.append(cp)
        for cp in cps:
            cp.wait()

    return pl.pallas_call(
        body, name=name,
        in_specs=[_HBM] * n, out_specs=[_HBM] * n,
        out_shape=[jax.ShapeDtypeStruct((a.shape[0], h, a.shape[2]), a.dtype) for a, h in zip(stacks, halves)],
        scratch_shapes=[pltpu.SemaphoreType.DMA((n,)), pltpu.SemaphoreType.DMA((n,))],
    )(*split)


def _sibling_swap(arrs):
    n = len(arrs)

    def body(*refs):
        ins, outs = refs[:n], refs[n:2 * n]
        send_sems, recv_sems = refs[2 * n:]
        x, y, c = _place()
        cps = []
        for t in range(n):
            cp = pltpu.make_async_remote_copy(ins[t], outs[t], send_sems.at[t], recv_sems.at[t],
                                              device_id=(x, y, 1 - c), device_id_type=MESH)
            cp.start()
            cps.append(cp)
        for cp in cps:
            cp.wait()

    return pl.pallas_call(
        body, name="sibling_swap",
        in_specs=[_HBM] * n, out_specs=[_HBM] * n,
        out_shape=[jax.ShapeDtypeStruct(a.shape, a.dtype) for a in arrs],
        scratch_shapes=[pltpu.SemaphoreType.DMA((n,)), pltpu.SemaphoreType.DMA((n,))],
    )(*arrs)


def _half_sum(name, own, other, core):
    k, r, cols = own.shape
    h = r // 2
    tr = next(cand for cand in (128, 176, 64, 32, 16) if h % cand == 0)
    nblk = h // tr

    def body(core_ref, own_ref, other_ref, o_ref):
        o_ref[...] = _bf(own_ref[...] + other_ref[...].astype(F32))

    grid_spec = pltpu.PrefetchScalarGridSpec(
        num_scalar_prefetch=1, grid=(nblk,),
        in_specs=[pl.BlockSpec((k, tr, cols), lambda i, core_ref: (0, i + core_ref[0] * nblk, 0)),
                  pl.BlockSpec((k, tr, cols), lambda i, core_ref: (0, i, 0))],
        out_specs=pl.BlockSpec((k, tr, cols), lambda i, core_ref: (0, i, 0)))
    return pl.pallas_call(
        body, name=name, grid_spec=grid_spec, out_shape=jax.ShapeDtypeStruct((k, h, cols), BF16),
        compiler_params=_params("parallel"),
    )(core.reshape(1).astype(jnp.int32), own, other)


def _small_allreduce(v):
    r = v.shape[0]

    def body(v_ref, tot_ref, slots, send_sems, recv_sems):
        x, y, c = _place()
        me = 4 * x + 2 * y + c
        slots[me] = v_ref[...]
        cps, peers = [], []
        for k in range(1, N_DEVICES):
            px = 1 - x if (k >> 2) & 1 else x
            py = 1 - y if (k >> 1) & 1 else y
            pc = 1 - c if k & 1 else c
            cp = pltpu.make_async_remote_copy(v_ref, slots.at[me], send_sems.at[k - 1], recv_sems.at[k - 1],
                                              device_id=(px, py, pc), device_id_type=MESH)
            cp.start()
            cps.append(cp)
            peers.append((px, py, pc))
        for k, (px, py, pc) in enumerate(peers):
            pltpu.make_async_remote_copy(v_ref, slots.at[4 * px + 2 * py + pc], send_sems.at[k], recv_sems.at[k],
                                         device_id=(px, py, pc), device_id_type=MESH).wait_recv()
        for cp in cps:
            cp.wait_send()
        acc = slots[0]
        for s in range(1, N_DEVICES):
            acc = acc + slots[s]
        tot_ref[...] = acc

    return pl.pallas_call(
        body, name="small_allreduce",
        in_specs=[pl.BlockSpec(memory_space=pltpu.VMEM)], out_specs=pl.BlockSpec(memory_space=pltpu.VMEM),
        out_shape=jax.ShapeDtypeStruct((r, LANES), F32),
        scratch_shapes=[pltpu.VMEM((N_DEVICES, r, LANES), F32), pltpu.SemaphoreType.DMA((N_DEVICES - 1,)),
                        pltpu.SemaphoreType.DMA((N_DEVICES - 1,))],
    )(v)


def _elementwise(name, fn, ins, out_dtypes):
    r, c = ins[0].shape[-2:]
    tr = next((cand for cand in (256, 176, 128, 64, 32, 16) if r % cand == 0), r)
    nin = len(ins)

    def body(*refs):
        outs = fn(*[ref[...] for ref in refs[:nin]])
        for o_ref, o in zip(refs[nin:], outs):
            o_ref[...] = o.astype(o_ref.dtype)

    in_specs = [pl.BlockSpec((tr, c), lambda i: (i, 0)) if a.ndim == 2 else pl.BlockSpec((a.shape[0], tr, c), lambda i: (0, i, 0))
                for a in ins]
    return pl.pallas_call(
        body, name=name, grid=(r // tr,), in_specs=in_specs,
        out_specs=[pl.BlockSpec((tr, c), lambda i: (i, 0)) for _ in out_dtypes],
        out_shape=[jax.ShapeDtypeStruct((r, c), dt) for dt in out_dtypes],
        compiler_params=_params("parallel"),
    )(*ins)


def _row_tile(rows):
    return next((cand for cand in (128, 176, 64, 32, 16) if rows % cand == 0), rows)


def _sum_slots(name, received, own, chip):
    _, r, cols = own.shape
    tr = _row_tile(r)

    def body(chip_ref, own_ref, a_ref, b_ref, c_ref, o_ref):
        o_ref[...] = ((own_ref[0].astype(F32) + a_ref[0].astype(F32)) + b_ref[0].astype(F32)) + c_ref[0].astype(F32)

    def slot(flip):
        return pl.BlockSpec((1, tr, cols), lambda i, chip_ref: (jnp.bitwise_xor(chip_ref[0], flip), i, 0))

    grid_spec = pltpu.PrefetchScalarGridSpec(
        num_scalar_prefetch=1, grid=(r // tr,), in_specs=[slot(0), slot(1), slot(2), slot(3)],
        out_specs=pl.BlockSpec((tr, cols), lambda i, chip_ref: (i, 0)))
    return pl.pallas_call(
        body, name=name, grid_spec=grid_spec, out_shape=jax.ShapeDtypeStruct((r, cols), F32),
        compiler_params=_params("parallel"),
    )(chip.reshape(1).astype(jnp.int32), own, received, received, received)


def _adamw_halves(name, mine, theirs, core, w, m, v):
    h, cols = mine.shape
    tr = _row_tile(h)
    nh = h // tr

    def body(core_ref, mine_ref, theirs_ref, w_ref, m_ref, v_ref, g_ref, d_ref, m2_ref, v2_ref):
        is_mine = (pl.program_id(0) // nh) == core_ref[0]
        g = jnp.where(is_mine, mine_ref[...], theirs_ref[...])
        outs = _adamw_math(g, w_ref[...], m_ref[...], v_ref[...])
        for ref, val in zip((g_ref, d_ref, m2_ref, v2_ref), outs):
            ref[...] = val

    half = pl.BlockSpec((tr, cols), lambda i, core_ref: (i % nh, 0))
    full = pl.BlockSpec((tr, cols), lambda i, core_ref: (i, 0))
    grid_spec = pltpu.PrefetchScalarGridSpec(
        num_scalar_prefetch=1, grid=(2 * nh,), in_specs=[half, half, full, full, full], out_specs=[full] * 4)
    return pl.pallas_call(
        body, name=name, grid_spec=grid_spec, out_shape=[jax.ShapeDtypeStruct((2 * h, cols), F32)] * 4,
        compiler_params=_params("parallel"),
    )(core.reshape(1).astype(jnp.int32), mine, theirs, w, m, v)


def _adamw_math(g, w_v, m_v, v_v):
    m2 = ADAM_B1 * m_v + (1.0 - ADAM_B1) * g
    v2 = ADAM_B2 * v_v + (1.0 - ADAM_B2) * jnp.square(g)
    m_hat = m2 / (1.0 - ADAM_B1 ** ADAM_STEP)
    v_hat = v2 / (1.0 - ADAM_B2 ** ADAM_STEP)
    delta = -ADAM_LR * (m_hat / (jnp.sqrt(v_hat) + ADAM_EPS) + ADAM_WD * w_v)
    return [g, delta, m2, v2]


def _adamw(name, g, w, m, v):
    return _elementwise(name, _adamw_math, [g, w, m, v], [F32] * 4)


_TRANSPOSED = ("ffn1_gate", "ffn1_up", "ffn2_gate", "ffn2_up")
_MATRICES = (("ffn1_gate", 0), ("ffn1_up", 0), ("ffn1_down", 0), ("w_in", 1), ("w_out", 0),
             ("ffn2_gate", 0), ("ffn2_up", 0), ("ffn2_down", 0))


def _block2d(a, name):
    return jnp.swapaxes(a, 1, 2)[0] if name in _TRANSPOSED else a[0]


def _block3d(a, name):
    return jnp.swapaxes(a[None], 1, 2) if name in _TRANSPOSED else a[None]
_VECTORS = ("ln1_g", "ln1_b", "conv_b", "dt_bias", "a_log", "d_skip", "attn_norm_w", "ssd_norm_w",
            "ln2_g", "ln2_b", "ln3_g", "ln3_b")
_WEIGHT_ORDER = ("ln1_g", "ln1_b", "ffn1_gate", "ffn1_up", "ffn1_down", "w_in", "conv_w", "conv_b", "dt_bias", "a_log",
                 "d_skip", "attn_norm_w", "ssd_norm_w", "w_out", "ln2_g", "ln2_b", "ffn2_gate", "ffn2_up", "ffn2_down",
                 "ln3_g", "ln3_b")


def _pack_rows(vectors):
    parts = []
    for vec in vectors:
        flat = vec.reshape(-1)
        parts.append(jnp.pad(flat, (0, (-flat.shape[0]) % LANES)))
    flat = jnp.concatenate(parts)
    flat = jnp.pad(flat, (0, (-flat.shape[0]) % (8 * LANES)))
    return flat.reshape(-1, LANES)


def _unpack_rows(packed, shapes):
    flat = packed.reshape(-1)
    out, off = [], 0
    for shape in shapes:
        size = int(np.prod(shape))
        out.append(flat[off:off + size].reshape(shape))
        off += size + (-size) % LANES
    return out


def _assemble(stack, axis):
    if axis == 0:
        return stack.reshape(-1, stack.shape[2])
    return jnp.concatenate([stack[s] for s in range(N_CHIPS)], axis=1)


def _split(full, axis):
    if axis == 0:
        return full.reshape(N_CHIPS, -1, full.shape[1])
    cols = full.shape[1] // N_CHIPS
    return jnp.stack([full[:, cols * s:cols * (s + 1)] for s in range(N_CHIPS)])


def kernel(x, positions, ln1_g, ln1_b, ffn1_gate, ffn1_up, ffn1_down, w_in, conv_w, conv_b, dt_bias, a_log, d_skip, attn_norm_w, ssd_norm_w, w_out, ln2_g, ln2_b, ffn2_gate, ffn2_up, ffn2_down, ln3_g, ln3_b, loss_target, m_ln1_g, m_ln1_b, m_ffn1_gate, m_ffn1_up, m_ffn1_down, m_w_in, m_conv_w, m_conv_b, m_dt_bias, m_a_log, m_d_skip, m_attn_norm_w, m_ssd_norm_w, m_w_out, m_ln2_g, m_ln2_b, m_ffn2_gate, m_ffn2_up, m_ffn2_down, m_ln3_g, m_ln3_b, v_ln1_g, v_ln1_b, v_ffn1_gate, v_ffn1_up, v_ffn1_down, v_w_in, v_conv_w, v_conv_b, v_dt_bias, v_a_log, v_d_skip, v_attn_norm_w, v_ssd_norm_w, v_w_out, v_ln2_g, v_ln2_b, v_ffn2_gate, v_ffn2_up, v_ffn2_down, v_ln3_g, v_ln3_b):
    given = dict(locals())
    wts = {n: given[n] for n in _WEIGHT_ORDER}
    mom_m = {n: given["m_" + n] for n in _WEIGHT_ORDER}
    mom_v = {n: given["v_" + n] for n in _WEIGHT_ORDER}
    chip = 2 * lax.axis_index("x") + lax.axis_index("y")

    core = lax.axis_index("c")
    groups = [[(n, axis) for n, axis in _MATRICES if n.startswith(prefix)] for prefix in ("ffn1", "w_", "ffn2")]
    own16 = {n: _block2d(wts[n], n).astype(BF16) for n, _ in _MATRICES}

    def full_weights(group, results):
        out = {}
        for (n, axis), st in zip(group, results):
            whole = _assemble(st, axis)
            out[n] = whole.T if n in _TRANSPOSED else whole
        return out

    full = full_weights(groups[0], _run_job(_GatherJob([own16[n] for n, _ in groups[0]]), "gather_ffn1"))
    for n in _VECTORS:
        full[n] = wts[n]
    conv_rows = jnp.pad(wts["conv_w"][0], ((0, 32 - CONV_WIDTH), (0, 0)))

    def mixer_weights(results):
        out = full_weights(groups[1], results)
        out["conv_w"] = _assemble(results[-1], 1)[:CONV_WIDTH]
        return out

    def ffn2_weights(results):
        return full_weights(groups[2], results)

    late = [(_GatherJob([own16[n] for n, _ in groups[1]] + [conv_rows]), mixer_weights),
            (_GatherJob([own16[n] for n, _ in groups[2]]), ffn2_weights)]

    chip_sums = {}

    def core_sums(g, which, tag):
        partials = [_split(g[n], axis) for n, axis in which]
        from_sibling = _sibling_halves([p.astype(BF16) for p in partials], "sibling_halves_" + tag)
        for (n, _), p, o in zip(which, partials, from_sibling):
            chip_sums[n] = _half_sum("core_sum_" + n, p, o, core)
        return _ExchangeJob([chip_sums[n] for n, _ in which])

    last = [(n, axis) for n, axis in _MATRICES if n in ("ffn1_gate", "ffn1_up")]
    early = [(n, axis) for n, axis in _MATRICES if (n, axis) not in last]
    loss, grad_x, g, received_early = _local_step(x, positions, loss_target, full, late,
                                                  lambda g_now: core_sums(g_now, early, "early"))
    received_last = _run_job(core_sums(g, last, "last"), "exchange_last")
    received = dict(zip([n for n, _ in last + early], received_last + received_early))
    half_totals = [_sum_slots("sum_partials_" + n, received[n], chip_sums[n], chip) for n, _ in _MATRICES]
    other_halves = _sibling_swap(half_totals)

    small_shapes = [g[n].shape for n in _VECTORS] + [g["conv_w"].shape, (1,)]
    total = _small_allreduce(_pack_rows([g[n] for n in _VECTORS] + [g["conv_w"], loss[0, :1]]))
    small = _unpack_rows(total, small_shapes)
    loss_out = small[-1].reshape(())

    grads, deltas, new_m, new_v = {}, {}, {}, {}
    for (n, _), mine, theirs in zip(_MATRICES, half_totals, other_halves):
        res = _adamw_halves("adamw_" + n, mine, theirs, core, _block2d(wts[n], n), _block2d(mom_m[n], n), _block2d(mom_v[n], n))
        grads[n], deltas[n], new_m[n], new_v[n] = [_block3d(r, n) for r in res]

    vec_shapes = [wts[n].shape for n in _VECTORS]
    res = _adamw("adamw_vectors", _pack_rows(small[:len(_VECTORS)]), _pack_rows([wts[n] for n in _VECTORS]),
                 _pack_rows([mom_m[n] for n in _VECTORS]), _pack_rows([mom_v[n] for n in _VECTORS]))
    for dst, packed in zip((grads, deltas, new_m, new_v), res):
        for n, val in zip(_VECTORS, _unpack_rows(packed, vec_shapes)):
            dst[n] = val

    cols = conv_w.shape[2]
    g_conv = lax.dynamic_slice_in_dim(small[len(_VECTORS)], chip * cols, cols, axis=1)
    res = _adamw("adamw_conv_w", g_conv, wts["conv_w"][0], mom_m["conv_w"][0], mom_v["conv_w"][0])
    grads["conv_w"], deltas["conv_w"], new_m["conv_w"], new_v["conv_w"] = [r[None] for r in res]

    return (loss_out, grad_x, *[grads[n] for n in _WEIGHT_ORDER], *[deltas[n] for n in _WEIGHT_ORDER],
            *[new_m[n] for n in _WEIGHT_ORDER], *[new_v[n] for n in _WEIGHT_ORDER])
```

```python
import numpy as np
import jax
import jax.numpy as jnp
from jax import lax
from jax.experimental import pallas as pl
from jax.experimental.pallas import tpu as pltpu

F32, BF16 = jnp.float32, jnp.bfloat16

D_MODEL = 1024
D_FF = 2816
N_HEADS = 12
HEAD_DIM = 64
D_ATTN = 768
D_SSD = 768
N_GROUPS = 4
HEADS_PER_GROUP = 3
D_STATE = 128
D_CONV = 1792
CONV_WIDTH = 4
ROPE_DIM = 16
ROPE_THETA = 500000.0
ALPHA = 2.0 ** 0.25
LN_EPS = 1e-5
RMS_EPS = 1e-6
ADAM_LR, ADAM_B1, ADAM_B2, ADAM_EPS, ADAM_WD, ADAM_STEP = 0.001, 0.9, 0.999, 1e-08, 0.01, 10

LANES = 128
GATE_UP_INTERLEAVE = 256
SEQ_BLOCK = 256
HEAD_LANES = 128
GROUP_LANES = 3 * HEAD_LANES
VMEM_LIMIT = 56 * 1024 * 1024
NEG = -1e30
MESH = pl.DeviceIdType.MESH
HIGHEST = lax.Precision.HIGHEST

_NT = (((1,), (1,)), ((), ()))
_TN = (((0,), (0,)), ((), ()))


def _params(*sem):
    return pltpu.CompilerParams(dimension_semantics=sem, vmem_limit_bytes=VMEM_LIMIT)


def _bf(v):
    return v.astype(BF16)


EPILOGUE_ROWS = 128


def _row_chunks(tm):
    return [slice(r, min(r + EPILOGUE_ROWS, tm)) for r in range(0, tm, EPILOGUE_ROWS)]


def _sigmoid(v):
    return 0.5 * jnp.tanh(0.5 * v) + 0.5


def _mm(name, pairs, *, scale=1.0, res=None, res_scale=1.0, out_dtype=F32, tm=512, tn=512):
    m, n = pairs[0][0].shape[0], pairs[0][1].shape[1]
    tm, tn = min(tm, m), min(tn, n)
    assert m % tm == 0 and n % tn == 0, (name, m, n, tm, tn)
    npair = len(pairs)

    def body(*refs):
        acc = None
        for a_ref, b_ref in zip(refs[:npair], refs[npair:2 * npair]):
            d = jnp.dot(_bf(a_ref[...]), b_ref[...], preferred_element_type=F32)
            acc = d if acc is None else acc + d
        if scale != 1.0:
            acc = acc * scale
        if res is not None:
            acc = acc + res_scale * refs[2 * npair][...]
        refs[-1][...] = acc.astype(out_dtype)

    in_specs = [pl.BlockSpec((tm, a.shape[1]), lambda i, j: (i, 0)) for a, _ in pairs]
    in_specs += [pl.BlockSpec((b.shape[0], tn), lambda i, j: (0, j)) for _, b in pairs]
    args = [a for a, _ in pairs] + [b for _, b in pairs]
    if res is not None:
        in_specs.append(pl.BlockSpec((tm, tn), lambda i, j: (i, j)))
        args.append(res)
    return pl.pallas_call(
        body, name=name, grid=(m // tm, n // tn), in_specs=in_specs,
        out_specs=pl.BlockSpec((tm, tn), lambda i, j: (i, j)),
        out_shape=jax.ShapeDtypeStruct((m, n), out_dtype),
        compiler_params=_params("parallel", "parallel"),
    )(*args)


def _mm_tn(name, x, dy, *, scale=1.0, tk=512, tn=512, tt=2048):
    t, k = x.shape
    n = dy.shape[1]
    tk, tn, tt = min(tk, k), min(tn, n), min(tt, t)
    assert k % tk == 0 and n % tn == 0 and t % tt == 0, (name, k, n, t)
    nt = t // tt

    def body(x_ref, dy_ref, o_ref):
        step = pl.program_id(2)
        d = lax.dot_general(_bf(x_ref[...]), _bf(dy_ref[...]), _TN, preferred_element_type=F32)

        @pl.when(step == 0)
        def _():
            o_ref[...] = d

        @pl.when(step > 0)
        def _():
            o_ref[...] += d

        if scale != 1.0:
            @pl.when(step == nt - 1)
            def _():
                o_ref[...] = o_ref[...] * scale

    return pl.pallas_call(
        body, name=name, grid=(k // tk, n // tn, nt),
        in_specs=[pl.BlockSpec((tt, tk), lambda i, j, s: (s, i)), pl.BlockSpec((tt, tn), lambda i, j, s: (s, j))],
        out_specs=pl.BlockSpec((tk, tn), lambda i, j, s: (i, j)),
        out_shape=jax.ShapeDtypeStruct((k, n), F32),
        compiler_params=_params("parallel", "parallel", "arbitrary"),
    )(x, dy)


def _mm_tn_sections(name, x, dys, *, tt=512):
    t, k = x.shape
    tt = min(tt, t)
    cuts = np.cumsum([0] + [d.shape[1] for d in dys]).tolist()
    ns = len(dys)

    def body(*refs):
        x_ref, o_ref = refs[0], refs[1 + ns]
        step = pl.program_id(0)
        xt = x_ref[...].T
        parts = [jnp.dot(xt, refs[1 + a][...], preferred_element_type=F32) for a in range(ns)]

        @pl.when(step == 0)
        def _():
            for a in range(ns):
                o_ref[:, cuts[a]:cuts[a + 1]] = parts[a]

        @pl.when(step > 0)
        def _():
            for a in range(ns):
                o_ref[:, cuts[a]:cuts[a + 1]] += parts[a]

    return pl.pallas_call(
        body, name=name, grid=(t // tt,),
        in_specs=[pl.BlockSpec((tt, k), lambda s: (s, 0))] + [pl.BlockSpec((tt, d.shape[1]), lambda s: (s, 0)) for d in dys],
        out_specs=pl.BlockSpec((k, cuts[-1]), lambda s: (0, 0)),
        out_shape=jax.ShapeDtypeStruct((k, cuts[-1]), F32),
        compiler_params=_params("arbitrary"),
    )(x, *dys)


def _mm_tn_gate_up(name, x, dau, *, tt=2048):
    t, k = x.shape
    gi = GATE_UP_INTERLEAVE
    nj = dau.shape[1] // (2 * gi)
    tt = min(tt, t)
    nt = t // tt

    def body(x_ref, dy_ref, g_ref, u_ref):
        step = pl.program_id(1)
        d = lax.dot_general(dy_ref[...], _bf(x_ref[...]), _TN, preferred_element_type=F32)

        @pl.when(step == 0)
        def _():
            g_ref[...] = d[:gi]
            u_ref[...] = d[gi:]

        @pl.when(step > 0)
        def _():
            g_ref[...] += d[:gi]
            u_ref[...] += d[gi:]

    out = pl.BlockSpec((gi, k), lambda j, s: (j, 0))
    return pl.pallas_call(
        body, name=name, grid=(nj, nt),
        in_specs=[pl.BlockSpec((tt, k), lambda j, s: (s, 0)), pl.BlockSpec((tt, 2 * gi), lambda j, s: (s, j))],
        out_specs=[out, out],
        out_shape=[jax.ShapeDtypeStruct((gi * nj, k), F32)] * 2,
        compiler_params=_params("parallel", "arbitrary"),
    )(x, dau)


def _carried(carry, ins, outs, sems, step, total):
    start, forward, finish = carry.phases(ins, outs, sems)
    pl.when(step == 0)(start)
    if forward is not None:
        pl.when(step == (3 * total) // 4)(forward)
    return lambda: pl.when(step == total - 1)(finish)


def _resident(shape):
    return pl.BlockSpec(shape, lambda i: (0,) * len(shape), pipeline_mode=pl.Buffered(1))


def _ffn_fwd(name, x16, res, wgu, wd, g, b, *, tm=512, carry=None):
    t, k = x16.shape
    gi = GATE_UP_INTERLEAVE
    nj, n, ni = wd.shape[0] // gi, wd.shape[1], t // tm
    nc = carry.n if carry is not None else 0

    def body(*refs):
        x_ref, res_ref, wgu_ref, wd_ref, g_ref, b_ref = refs[:6]
        au_ref, hm_ref, y_ref, r_ref, y16_ref = refs[6 + nc:11 + nc]
        if carry is not None:
            finish = _carried(carry, refs[6:6 + nc], refs[11 + nc:11 + 2 * nc], refs[11 + 2 * nc:], pl.program_id(0), ni)
        xv = x_ref[...]
        acc = jnp.zeros((tm, n), F32)
        for j in range(nj):
            au = jnp.dot(xv, wgu_ref[:, 2 * gi * j:2 * gi * (j + 1)], preferred_element_type=F32)
            a, u = au[:, :gi], au[:, gi:]
            au_ref[:, 2 * gi * j:2 * gi * (j + 1)] = _bf(au)
            hm = _bf(a * _sigmoid(a) * u)
            hm_ref[:, gi * j:gi * (j + 1)] = hm
            acc = acc + jnp.dot(hm, wd_ref[gi * j:gi * (j + 1), :], preferred_element_type=F32)
        r = ALPHA * res_ref[...] + 0.5 * acc
        r_ref[...] = r
        y = _layer_norm(r, g_ref[...], b_ref[...])
        y_ref[...] = y
        y16_ref[...] = _bf(y)
        if carry is not None:
            finish()

    row = lambda c: pl.BlockSpec((tm, c), lambda i: (i, 0))
    hbm = pl.BlockSpec(memory_space=pltpu.HBM)
    res_ = pl.pallas_call(
        body, name=name, grid=(ni,),
        in_specs=[row(k), row(n), _resident(wgu.shape), _resident(wd.shape), _resident(g.shape), _resident(b.shape)] + [hbm] * nc,
        out_specs=[row(2 * gi * nj), row(gi * nj), row(n), row(n), row(n)] + [hbm] * nc,
        out_shape=[jax.ShapeDtypeStruct((t, 2 * gi * nj), BF16), jax.ShapeDtypeStruct((t, gi * nj), BF16),
                   jax.ShapeDtypeStruct((t, n), F32), jax.ShapeDtypeStruct((t, n), F32), jax.ShapeDtypeStruct((t, n), BF16)]
        + (carry.out_shape if carry is not None else []),
        scratch_shapes=carry.scratch_shapes if carry is not None else [],
        compiler_params=_params("arbitrary" if carry is not None else "parallel"),
    )(x16, res, wgu, wd, g, b, *(carry.operands if carry is not None else []))
    return tuple(res_[:5]) + ((carry.results(res_[5:]),) if carry is not None else ())


def _ffn_bwd(name, dr16, dr, wdt, au, wgut, *, tm=512, carry=None):
    t, n = dr16.shape
    gi = GATE_UP_INTERLEAVE
    nj, ni = wdt.shape[1] // gi, t // tm
    nc = carry.n if carry is not None else 0

    def body(*refs):
        dr16_ref, dr_ref, wdt_ref, au_ref, wgut_ref = refs[:5]
        dau_ref, dx_ref = refs[5 + nc:7 + nc]
        if carry is not None:
            finish = _carried(carry, refs[5:5 + nc], refs[7 + nc:7 + 2 * nc], refs[7 + 2 * nc:], pl.program_id(0), ni)
        drv = dr16_ref[...]
        acc = jnp.zeros((tm, n), F32)
        for j in range(nj):
            dhm = jnp.dot(drv, wdt_ref[:, gi * j:gi * (j + 1)], preferred_element_type=F32)
            au_v = au_ref[:, 2 * gi * j:2 * gi * (j + 1)].astype(F32)
            a, u = au_v[:, :gi], au_v[:, gi:]
            sig = _sigmoid(a)
            silu = a * sig
            dau = jnp.concatenate([_bf(dhm * u * (sig + silu - silu * sig)), _bf(dhm * silu)], axis=1)
            dau_ref[:, 2 * gi * j:2 * gi * (j + 1)] = dau
            acc = acc + jnp.dot(dau, wgut_ref[2 * gi * j:2 * gi * (j + 1), :], preferred_element_type=F32)
        dx_ref[...] = ALPHA * dr_ref[...] + acc
        if carry is not None:
            finish()

    row = lambda c: pl.BlockSpec((tm, c), lambda i: (i, 0))
    hbm = pl.BlockSpec(memory_space=pltpu.HBM)
    res_ = pl.pallas_call(
        body, name=name, grid=(ni,),
        in_specs=[row(n), row(n), _resident(wdt.shape), row(2 * gi * nj), _resident(wgut.shape)] + [hbm] * nc,
        out_specs=[row(2 * gi * nj), row(n)] + [hbm] * nc,
        out_shape=[jax.ShapeDtypeStruct((t, 2 * gi * nj), BF16), jax.ShapeDtypeStruct((t, n), F32)]
        + (carry.out_shape if carry is not None else []),
        scratch_shapes=carry.scratch_shapes if carry is not None else [],
        compiler_params=_params("arbitrary" if carry is not None else "parallel"),
    )(dr16, dr, wdt, au, wgut, *(carry.operands if carry is not None else []))
    return tuple(res_[:2]) + ((carry.results(res_[2:]),) if carry is not None else ())


def _layer_norm(r, g, b):
    mu = jnp.mean(r, axis=-1, keepdims=True)
    var = jnp.mean(jnp.square(r - mu), axis=-1, keepdims=True)
    return (r - mu) * lax.rsqrt(var + LN_EPS) * g + b


def _mm_res_ln(name, a, w, res, g, b, *, scale, tm=512):
    t, k = a.shape
    n = w.shape[1]

    def body(a_ref, w_ref, res_ref, g_ref, b_ref, y_ref, r_ref, y16_ref):
        for rows in _row_chunks(tm):
            r = ALPHA * res_ref[rows, :] + scale * jnp.dot(_bf(a_ref[rows, :]), w_ref[...], preferred_element_type=F32)
            r_ref[rows, :] = r
            y = _layer_norm(r, g_ref[...], b_ref[...])
            y_ref[rows, :] = y
            y16_ref[rows, :] = _bf(y)

    row = lambda c: pl.BlockSpec((tm, c), lambda i: (i, 0))
    const = lambda shape: pl.BlockSpec(shape, lambda i: (0, 0))
    return pl.pallas_call(
        body, name=name, grid=(t // tm,),
        in_specs=[row(k), const((k, n)), row(n), const((1, n)), const((1, n))],
        out_specs=[row(n), row(n), row(n)],
        out_shape=[jax.ShapeDtypeStruct((t, n), F32), jax.ShapeDtypeStruct((t, n), F32), jax.ShapeDtypeStruct((t, n), BF16)],
        compiler_params=_params("parallel"),
    )(a, w, res, g, b)


def _rowwise(name, fn, rows, consts, row_outs, acc_outs=(), tm=512):
    rows = [r if isinstance(r, tuple) else (r, r.shape[1]) for r in rows]
    t = rows[0][0].shape[0]
    tm = min(tm, t)
    assert t % tm == 0
    nr, nc, no, na = len(rows), len(consts), len(row_outs), len(acc_outs)

    def body(*refs):
        vals = [r[...] for r in refs[:nr + nc]]
        outs, accs = fn(*vals)
        for o_ref, o in zip(refs[nr + nc:nr + nc + no], outs):
            o_ref[...] = o.astype(o_ref.dtype)
        if na:
            step = pl.program_id(0)
            acc_refs = refs[nr + nc + no:]

            @pl.when(step == 0)
            def _():
                for a_ref, a in zip(acc_refs, accs):
                    a_ref[...] = a

            @pl.when(step > 0)
            def _():
                for a_ref, a in zip(acc_refs, accs):
                    a_ref[...] += a

    in_specs = [pl.BlockSpec((tm, w), lambda i: (i, 0)) for _, w in rows]
    in_specs += [pl.BlockSpec(c.shape, lambda i, nd=c.ndim: (0,) * nd) for c in consts]
    out_specs = [pl.BlockSpec((tm, c), lambda i: (i, 0)) for c, _ in row_outs]
    out_specs += [pl.BlockSpec(s, lambda i: (0, 0)) for s in acc_outs]
    out_shape = [jax.ShapeDtypeStruct((t, c), dt) for c, dt in row_outs]
    out_shape += [jax.ShapeDtypeStruct(s, F32) for s in acc_outs]
    res = pl.pallas_call(
        body, name=name, grid=(t // tm,), in_specs=in_specs, out_specs=out_specs, out_shape=out_shape,
        compiler_params=_params("arbitrary" if na else "parallel"),
    )(*[r for r, _ in rows], *consts)
    return res


def _ln_bwd(name, r, g, b, dy):
    def fn(r_v, dy_v, g_v, b_v):
        _, vjp = jax.vjp(_layer_norm, r_v, g_v, b_v)
        dr, dg, db = vjp(dy_v)
        return [dr, dr], [dg, db]
    return _rowwise(name, fn, [r, dy], [g, b], [(r.shape[1], F32), (r.shape[1], BF16)], [(1, r.shape[1])] * 2)


def _ln_loss_bwd(name, r, g, b, target):
    def fn(r_v, t_v, g_v, b_v):
        def loss_fn(rr, gg, bb):
            err = jnp.square(_layer_norm(rr, gg, bb) - t_v)
            return 0.5 * jnp.sum(jnp.mean(err, axis=-1, keepdims=True), axis=0, keepdims=True)
        loss, vjp = jax.vjp(loss_fn, r_v, g_v, b_v)
        dr, dg, db = vjp(jnp.ones((1, 1), F32))
        return [dr, dr], [dg, db, jnp.broadcast_to(loss, (1, LANES))]
    return _rowwise(name, fn, [r, target], [g, b], [(r.shape[1], F32), (r.shape[1], BF16)],
                    [(1, r.shape[1])] * 2 + [(1, LANES)])


def _rope_tables(posf, invf, sgn):
    ang = posf * invf
    return jnp.cos(ang), jnp.sin(ang) * sgn


def _rope_apply(tv, cos, sin):
    lane = lax.broadcasted_iota(jnp.int32, cos.shape, 1)
    first = (lane % HEAD_DIM) < (ROPE_DIM // 2)
    outs = []
    for gidx in range(tv.shape[1] // LANES):
        tg = tv[:, LANES * gidx:LANES * (gidx + 1)]
        sw = jnp.where(first, pltpu.roll(tg, LANES - ROPE_DIM // 2, 1), pltpu.roll(tg, ROPE_DIM // 2, 1))
        outs.append(tg * cos + sw * sin)
    return jnp.concatenate(outs, axis=1)


def _proj_in(h16, w_in, posf, invf, sgn, *, tm=512, carry=None):
    t, k = h16.shape
    cuts = [0, D_ATTN, 2 * D_ATTN, 3 * D_ATTN, 3 * D_ATTN + D_SSD, 3 * D_ATTN + D_SSD + D_CONV, w_in.shape[1]]
    nc = carry.n if carry is not None else 0

    def body(*refs):
        h_ref, w_ref, pos_ref, invf_ref, sgn_ref = refs[:5]
        q_ref, k_ref, v_ref, z_ref, xbc_ref, dt_ref, cs_ref = refs[5 + nc:12 + nc]
        if carry is not None:
            finish = _carried(carry, refs[5:5 + nc], refs[12 + nc:12 + 2 * nc], refs[12 + 2 * nc:], pl.program_id(0), t // tm)
        hv = h_ref[...]
        part = lambda a: jnp.dot(hv, w_ref[:, cuts[a]:cuts[a + 1]], preferred_element_type=F32)
        cos, sin = _rope_tables(pos_ref[...], invf_ref[...], sgn_ref[...])
        cs_ref[...] = jnp.concatenate([cos, sin], axis=1)
        q_ref[...] = _bf(_rope_apply(part(0), cos, sin) * (HEAD_DIM ** -0.5))
        k_ref[...] = _bf(_rope_apply(part(1), cos, sin))
        v_ref[...] = _bf(part(2))
        z_ref[...] = part(3)
        xbc_ref[...] = part(4)
        dt_ref[...] = part(5)
        if carry is not None:
            finish()

    row = lambda c: pl.BlockSpec((tm, c), lambda i: (i, 0))
    hbm = pl.BlockSpec(memory_space=pltpu.HBM)
    widths = [D_ATTN, D_ATTN, D_ATTN, D_SSD, D_CONV, LANES, 2 * LANES]
    dtypes = [BF16, BF16, BF16, F32, F32, F32, F32]
    res = pl.pallas_call(
        body, name="proj_in", grid=(t // tm,),
        in_specs=[row(k), _resident(w_in.shape), row(1), _resident(invf.shape), _resident(sgn.shape)] + [hbm] * nc,
        out_specs=[row(c) for c in widths] + [hbm] * nc,
        out_shape=[jax.ShapeDtypeStruct((t, c), dt) for c, dt in zip(widths, dtypes)]
        + (carry.out_shape if carry is not None else []),
        scratch_shapes=carry.scratch_shapes if carry is not None else [],
        compiler_params=_params("arbitrary" if carry is not None else "parallel"),
    )(h16, w_in, posf, invf, sgn, *(carry.operands if carry is not None else []))
    return tuple(res[:7]) + ((carry.results(res[7:]),) if carry is not None else ())


def _rope_bwd(dq, dk, cs):
    def fn(dq_v, dk_v, cs_v):
        cos, sin = cs_v[:, :LANES], -cs_v[:, LANES:]
        gq = _rope_apply(dq_v * (HEAD_DIM ** -0.5), cos, sin)
        gk = _rope_apply(dk_v, cos, sin)
        return [jnp.concatenate([gq, gk], axis=1)], []
    return _rowwise("rope_bwd", fn, [dq, dk, cs], [], [(2 * D_ATTN, BF16)])[0]


def _rms(v, w):
    return v * lax.rsqrt(jnp.mean(v * v, axis=-1, keepdims=True) + RMS_EPS) * w


def _ungroup(yg):
    return jnp.concatenate([yg[:, HEAD_LANES * h:HEAD_LANES * h + HEAD_DIM] for h in range(N_HEADS)], axis=1)


def _group(xs):
    parts = []
    for h in range(N_HEADS):
        parts += [xs[:, HEAD_DIM * h:HEAD_DIM * (h + 1)], jnp.zeros((xs.shape[0], HEAD_LANES - HEAD_DIM), xs.dtype)]
    return jnp.concatenate(parts, axis=1)


def _norms_fn(attn, yg, xs, z, w_attn, w_ssd, dskip):
    a_n = _rms(attn, w_attn)
    y = _ungroup(yg) + dskip * xs
    y_n = _rms(y * (z * _sigmoid(z)), w_ssd)
    return jnp.concatenate([a_n, y_n], axis=1)


def _norms_fwd(attn, yg, xbc, z, w_attn, w_ssd, dskip):
    def fn(*v):
        return [_norms_fn(*v)], []
    return _rowwise("norms_fwd", fn, [attn, yg, (xbc, D_SSD), z], [w_attn, w_ssd, dskip], [(D_ATTN + D_SSD, BF16)])[0]


def _norms_bwd(attn, yg, xbc, z, w_attn, w_ssd, dskip, dcat):
    def fn(attn_v, yg_v, xs_v, z_v, dcat_v, wa_v, ws_v, dk_v):
        _, vjp = jax.vjp(_norms_fn, attn_v, yg_v, xs_v, z_v, wa_v, ws_v, dk_v)
        d_attn, d_yg, d_xs, d_z, d_wa, d_ws, d_dk = vjp(dcat_v)
        return [d_attn, d_yg, d_xs, d_z], [d_wa, d_ws, d_dk]
    return _rowwise("norms_bwd", fn, [attn, yg, (xbc, D_SSD), z, dcat], [w_attn, w_ssd, dskip],
                    [(D_ATTN, F32), (N_GROUPS * GROUP_LANES, F32), (D_SSD, F32), (D_SSD, BF16)], [(1, D_SSD)] * 3)


def _spread_sum(v, e):
    h1 = _bf(v)
    r1 = v - h1.astype(F32)
    h2 = _bf(r1)
    h3 = _bf(r1 - h2.astype(F32))
    return sum(jnp.dot(h, e, preferred_element_type=F32) for h in (h1, h2, h3))


@jax.custom_vjp
def _spread(v, e, e_t):
    return _spread_sum(v, e)


def _spread_fwd(v, e, e_t):
    return _spread_sum(v, e), (e, e_t)


def _spread_bwd(saved, g):
    e, e_t = saved
    return _spread_sum(g, e_t), jnp.zeros_like(e), jnp.zeros_like(e_t)


_spread.defvjp(_spread_fwd, _spread_bwd)


def _ssd_prep_fn(xs, dtp, dtb, alog, e_x, e_xt, e_a, e_at):
    dt = jax.nn.softplus(dtp + dtb)
    a = -jnp.exp(alog)
    xdtg = _group(xs) * _spread(dt, e_x, e_xt)
    dag = _spread(dt * a, e_a, e_at)
    return xdtg, dag


def _ssd_prep_fwd(xbc, dtp, dtb, alog, spreaders):
    def fn(xbc_v, dtp_v, dtb_v, alog_v, *e_v):
        xdtg, dag = _ssd_prep_fn(xbc_v[:, :D_SSD], dtp_v, dtb_v, alog_v, *e_v)
        return [xdtg, xbc_v[:, D_SSD:], dag], []
    return _rowwise("ssd_prep_fwd", fn, [xbc, dtp], [dtb, alog, *spreaders],
                    [(N_GROUPS * GROUP_LANES, BF16), (D_CONV - D_SSD, BF16), (N_GROUPS * LANES, F32)])


def _ssd_prep_bwd(xbc, dtp, dtb, alog, spreaders, dxdtg, ddag, dxs_a, db, dc):
    def fn(xs_v, dtp_v, dxdtg_v, ddag_v, dxs_a_v, db_v, dc_v, dtb_v, alog_v, *e_v):
        _, vjp = jax.vjp(lambda a, b, c, d: _ssd_prep_fn(a, b, c, d, *e_v), xs_v, dtp_v, dtb_v, alog_v)
        dxs, ddtp, ddtb, dalog = vjp((dxdtg_v, ddag_v))
        return [jnp.concatenate([dxs + dxs_a_v, db_v, dc_v], axis=1), ddtp], [ddtb, dalog]
    return _rowwise("ssd_prep_bwd", fn, [(xbc, D_SSD), dtp, dxdtg, ddag, dxs_a, db, dc], [dtb, alog, *spreaders],
                    [(D_CONV, F32), (LANES, BF16)], [(1, LANES)] * 2)


def _shift_down(u, d):
    if d == 0:
        return u
    row = lax.broadcasted_iota(jnp.int32, u.shape, 0)
    return jnp.where(row >= d, pltpu.roll(u, d, 0), 0.0)


def _shift_up(u, d):
    if d == 0:
        return u
    s = u.shape[0]
    row = lax.broadcasted_iota(jnp.int32, u.shape, 0)
    return jnp.where(row < s - d, pltpu.roll(u, s - d, 0), 0.0)


def _conv_pre(u, w, b):
    acc = b
    for k in range(CONV_WIDTH):
        acc = acc + w[k:k + 1, :] * _shift_down(u, CONV_WIDTH - 1 - k)
    return acc


def _conv_fwd(u, w, b, *, tc=256):
    nb, s, c = u.shape

    def body(u_ref, w_ref, b_ref, o_ref):
        pre = _conv_pre(u_ref[0], w_ref[...], b_ref[...])
        o_ref[0] = pre * _sigmoid(pre)

    return pl.pallas_call(
        body, name="conv_fwd", grid=(c // tc, nb),
        in_specs=[pl.BlockSpec((1, s, tc), lambda j, i: (i, 0, j)), pl.BlockSpec((CONV_WIDTH, tc), lambda j, i: (0, j)),
                  pl.BlockSpec((1, tc), lambda j, i: (0, j))],
        out_specs=pl.BlockSpec((1, s, tc), lambda j, i: (i, 0, j)),
        out_shape=jax.ShapeDtypeStruct((nb, s, c), F32),
        compiler_params=_params("parallel", "parallel"),
    )(u, w, b)


def _conv_bwd(u, w, b, dout, *, tc=256):
    nb, s, c = u.shape

    def body(u_ref, w_ref, b_ref, d_ref, du_ref, dw_ref, db_ref):
        uv, wv = u_ref[0], w_ref[...]
        pre = _conv_pre(uv, wv, b_ref[...])
        sig = _sigmoid(pre)
        dpre = d_ref[0] * (sig * (1.0 + pre * (1.0 - sig)))
        du = jnp.zeros_like(uv)
        dws = []
        for k in range(CONV_WIDTH):
            du = du + wv[k:k + 1, :] * _shift_up(dpre, CONV_WIDTH - 1 - k)
            dws.append(jnp.sum(dpre * _shift_down(uv, CONV_WIDTH - 1 - k), axis=0, keepdims=True))
        du_ref[0] = _bf(du)
        dwv = jnp.concatenate(dws + [jnp.zeros((8 - CONV_WIDTH, tc), F32)], axis=0)
        dbv = jnp.sum(dpre, axis=0, keepdims=True)
        first = pl.program_id(1) == 0

        @pl.when(first)
        def _():
            dw_ref[...] = dwv
            db_ref[...] = dbv

        @pl.when(jnp.logical_not(first))
        def _():
            dw_ref[...] += dwv
            db_ref[...] += dbv

    blk = pl.BlockSpec((1, s, tc), lambda j, i: (i, 0, j))
    return pl.pallas_call(
        body, name="conv_bwd", grid=(c // tc, nb),
        in_specs=[blk, pl.BlockSpec((CONV_WIDTH, tc), lambda j, i: (0, j)), pl.BlockSpec((1, tc), lambda j, i: (0, j)), blk],
        out_specs=[blk, pl.BlockSpec((8, tc), lambda j, i: (0, j)), pl.BlockSpec((1, tc), lambda j, i: (0, j))],
        out_shape=[jax.ShapeDtypeStruct((nb, s, c), BF16), jax.ShapeDtypeStruct((8, c), F32), jax.ShapeDtypeStruct((1, c), F32)],
        compiler_params=_params("parallel", "arbitrary"),
    )(u, w, b, dout)


FWD_KEY_BLOCK = 256


def _branch_bias_table(seq, kb):
    ratio = SEQ_BLOCK // kb
    key = np.arange(kb)[None, :, None]
    query = np.arange(SEQ_BLOCK)[None, None, :]
    delta = (np.arange(seq // kb)[:, None, None] - (ratio - 1)) * kb + query - key
    cnt = np.zeros(delta.shape, np.float64)
    for window, dilation in ((128, 1), (512, 4), (2048, 16)):
        cnt += (delta >= 0) & (delta % dilation == 0) & (delta <= window)
    return jnp.asarray(np.where(cnt > 0, np.log(np.maximum(cnt, 1.0)), NEG).astype(np.float32))


HEADS_PER_BLOCK = LANES // HEAD_DIM


def _head_rows(v, h):
    row = lax.broadcasted_iota(jnp.int32, v.shape, 0)
    return jnp.where((row >= HEAD_DIM * h) & (row < HEAD_DIM * (h + 1)), v, jnp.zeros_like(v))


def _attn_fwd(q, k, v, bias):
    nb_, s, _ = q.shape
    ab, kb = SEQ_BLOCK, FWD_KEY_BLOCK
    nblk, nkb, ratio = s // ab, s // kb, ab // kb

    def body(q_ref, k_ref, v_ref, b_ref, o_ref, lse_ref, vt_s):
        i = pl.program_id(2)

        @pl.when(i == 0)
        def _():
            for jb in range(nkb):
                vt_s[jb] = v_ref[0, kb * jb:kb * (jb + 1), :].T

        qt = q_ref[0].T
        qts = [_head_rows(qt, h) for h in range(HEADS_PER_BLOCK)]

        last = ratio * (i + 1) - 1

        def scores(j):
            kj = k_ref[0, pl.ds(pl.multiple_of(j * kb, kb), kb), :]
            return [jnp.dot(kj, qts[h], preferred_element_type=F32) for h in range(HEADS_PER_BLOCK)]

        def step(j, carry):
            ahead = scores(jnp.minimum(j + 1, last))
            lb = b_ref[ratio * i - j + (ratio - 1)]
            out = []
            for h in range(HEADS_PER_BLOCK):
                m, l, acc = carry[3 * h:3 * h + 3]
                st = carry[3 * HEADS_PER_BLOCK + h] + lb
                m_new = jnp.maximum(m, jnp.max(st, axis=0, keepdims=True))
                p = jnp.exp(st - m_new)
                a = jnp.exp(m - m_new)
                l = a * l + jnp.sum(p, axis=0, keepdims=True)
                vt = vt_s[j, HEAD_DIM * h:HEAD_DIM * (h + 1), :]
                acc = a * acc + jnp.dot(vt, _bf(p), preferred_element_type=F32)
                out += [m_new, l, acc]
            return tuple(out) + tuple(ahead)

        init = (jnp.full((1, ab), NEG, F32), jnp.zeros((1, ab), F32), jnp.zeros((HEAD_DIM, ab), F32)) * HEADS_PER_BLOCK
        res = lax.fori_loop(0, ratio * (i + 1), step, init + tuple(scores(0)))
        ot = jnp.concatenate([res[3 * h + 2] / res[3 * h + 1] for h in range(HEADS_PER_BLOCK)], axis=0)
        o_ref[0] = ot.T
        rows = [res[3 * h] + jnp.log(res[3 * h + 1]) for h in range(HEADS_PER_BLOCK)]
        lse_ref[0, 0, 0] = jnp.concatenate(rows + [jnp.zeros((8 - HEADS_PER_BLOCK, ab), F32)], axis=0)

    qblk = pl.BlockSpec((1, ab, LANES), lambda b, hp, i: (b, i, hp))
    full = pl.BlockSpec((1, s, LANES), lambda b, hp, i: (b, 0, hp))
    return pl.pallas_call(
        body, name="attn_fwd", grid=(nb_, D_ATTN // LANES, nblk),
        in_specs=[qblk, full, full, pl.BlockSpec((nkb, kb, ab), lambda b, hp, i: (0, 0, 0))],
        out_specs=[qblk, pl.BlockSpec((1, 1, 1, 8, ab), lambda b, hp, i: (b, hp, i, 0, 0))],
        out_shape=[jax.ShapeDtypeStruct((nb_, s, D_ATTN), F32),
                   jax.ShapeDtypeStruct((nb_, D_ATTN // LANES, nblk, 8, ab), F32)],
        scratch_shapes=[pltpu.VMEM((nkb, LANES, kb), BF16)],
        compiler_params=_params("parallel", "parallel", "arbitrary"),
    )(q, k, v, bias)


def _attn_bwd(q, k, v, o, do, lse, bias):
    nb_, s, _ = q.shape
    ab = SEQ_BLOCK
    nblk = s // ab

    nh = HEADS_PER_BLOCK

    def body(q_ref, k_ref, v_ref, o_ref, do_ref, lse_ref, b_ref, dq_ref, dk_ref, dv_ref,
             qt_s, dot_s, kt_s, dqt_s, do16_s, d_s, dk_acc, dv_acc):
        for jb in range(nblk):
            sl = slice(ab * jb, ab * (jb + 1))
            qt, kt = q_ref[0, sl, :].T, k_ref[0, sl, :].T
            do = do_ref[0, sl, :]
            dot = do.T
            prod = dot * o_ref[0, sl, :].T
            do16_s[sl, :] = _bf(do)
            for h in range(nh):
                qt_s[nh * jb + h] = _head_rows(qt, h)
                kt_s[nh * jb + h] = _head_rows(kt, h)
                dot_s[nh * jb + h] = _head_rows(_bf(dot), h)
            d_s[jb] = jnp.concatenate(
                [jnp.sum(prod[HEAD_DIM * h:HEAD_DIM * (h + 1)], axis=0, keepdims=True) for h in range(nh)]
                + [jnp.zeros((8 - nh, ab), F32)], axis=0)
            dqt_s[jb] = jnp.zeros((LANES, ab), F32)

        def outer(j, carry):
            ks = pl.ds(pl.multiple_of(j * ab, ab), ab)
            kj, vj = k_ref[0, ks, :], v_ref[0, ks, :]
            dk_acc[...] = jnp.zeros_like(dk_acc)
            dv_acc[...] = jnp.zeros_like(dv_acc)

            def inner(i, c2):
                qs = pl.ds(pl.multiple_of(i * ab, ab), ab)
                qi, doi = q_ref[0, qs, :], do16_s[qs, :]
                lb = b_ref[i - j]
                for h in range(nh):
                    st = jnp.dot(kj, qt_s[nh * i + h], preferred_element_type=F32) + lb
                    pt = jnp.exp(st - lse_ref[0, 0, i, h:h + 1, :])
                    dpt = jnp.dot(vj, dot_s[nh * i + h], preferred_element_type=F32)
                    dst16 = _bf(pt * (dpt - d_s[i, h:h + 1, :]))
                    dv_acc[h] += jnp.dot(_bf(pt), doi, preferred_element_type=F32)
                    dk_acc[h] += jnp.dot(dst16, qi, preferred_element_type=F32)
                    dqt_s[i] += jnp.dot(kt_s[nh * j + h], dst16, preferred_element_type=F32)
                return c2

            lax.fori_loop(j, nblk, inner, 0)
            lane = lax.broadcasted_iota(jnp.int32, (ab, LANES), 1)
            dk_ref[0, ks, :] = jnp.where(lane < HEAD_DIM, dk_acc[0], dk_acc[1])
            dv_ref[0, ks, :] = _bf(jnp.where(lane < HEAD_DIM, dv_acc[0], dv_acc[1]))
            return carry

        lax.fori_loop(0, nblk, outer, 0)
        for jb in range(nblk):
            dq_ref[0, ab * jb:ab * (jb + 1), :] = dqt_s[jb].T

    assert nh == 2
    full = pl.BlockSpec((1, s, LANES), lambda b, hp: (b, 0, hp))
    return pl.pallas_call(
        body, name="attn_bwd", grid=(nb_, D_ATTN // LANES),
        in_specs=[full] * 5 + [pl.BlockSpec((1, 1, nblk, 8, ab), lambda b, hp: (b, hp, 0, 0, 0)),
                               pl.BlockSpec((nblk, ab, ab), lambda b, hp: (0, 0, 0))],
        out_specs=[full, full, full],
        out_shape=[jax.ShapeDtypeStruct((nb_, s, D_ATTN), F32), jax.ShapeDtypeStruct((nb_, s, D_ATTN), F32),
                   jax.ShapeDtypeStruct((nb_, s, D_ATTN), BF16)],
        scratch_shapes=[pltpu.VMEM((nh * nblk, LANES, ab), BF16), pltpu.VMEM((nh * nblk, LANES, ab), BF16),
                        pltpu.VMEM((nh * nblk, LANES, ab), BF16), pltpu.VMEM((nblk, LANES, ab), F32),
                        pltpu.VMEM((s, LANES), BF16), pltpu.VMEM((nblk, 8, ab), F32),
                        pltpu.VMEM((nh, ab, LANES), F32), pltpu.VMEM((nh, ab, LANES), F32)],
        compiler_params=_params("parallel", "parallel"),
    )(q, k, v, o, do, lse, bias)


def _cumsum_fwd(dag):
    nb_, s, c = dag.shape
    ab = SEQ_BLOCK

    def body(a_ref, o_ref, ot_ref):
        r = lax.broadcasted_iota(jnp.int32, (ab, ab), 0)
        cc = lax.broadcasted_iota(jnp.int32, (ab, ab), 1)
        tri = (r >= cc).astype(F32)
        carry = jnp.zeros((1, c), F32)
        for i in range(s // ab):
            loc = jnp.dot(tri, a_ref[0, ab * i:ab * (i + 1), :], precision=HIGHEST, preferred_element_type=F32) + carry
            o_ref[0, ab * i:ab * (i + 1), :] = loc
            ot_ref[0, :, ab * i:ab * (i + 1)] = loc.T
            carry = loc[ab - 1:ab, :]

    return pl.pallas_call(
        body, name="ssd_cumsum", grid=(nb_,),
        in_specs=[pl.BlockSpec((1, s, c), lambda b: (b, 0, 0))],
        out_specs=[pl.BlockSpec((1, s, c), lambda b: (b, 0, 0)), pl.BlockSpec((1, c, s), lambda b: (b, 0, 0))],
        out_shape=[jax.ShapeDtypeStruct((nb_, s, c), F32), jax.ShapeDtypeStruct((nb_, c, s), F32)],
        compiler_params=_params("parallel"),
    )(dag)


def _cumsum_bwd(dcol, drow):
    nb_, s, c = dcol.shape
    ab = SEQ_BLOCK

    def body(c_ref, r_ref, o_ref):
        r = lax.broadcasted_iota(jnp.int32, (ab, ab), 0)
        cc = lax.broadcasted_iota(jnp.int32, (ab, ab), 1)
        tri = (r <= cc).astype(F32)
        carry = jnp.zeros((1, c), F32)
        for i in reversed(range(s // ab)):
            rows = r_ref[0, :, ab * i:ab * (i + 1)].T
            parts = []
            for g in range(N_GROUPS):
                parts += [rows[:, 8 * g:8 * (g + 1)], jnp.zeros((ab, LANES - 8), F32)]
            blk = c_ref[0, ab * i:ab * (i + 1), :] + jnp.concatenate(parts, axis=1)
            loc = jnp.dot(tri, blk, precision=HIGHEST, preferred_element_type=F32) + carry
            o_ref[0, ab * i:ab * (i + 1), :] = loc
            carry = loc[0:1, :]

    return pl.pallas_call(
        body, name="ssd_cumsum_bwd", grid=(nb_,),
        in_specs=[pl.BlockSpec((1, s, c), lambda b: (b, 0, 0)), pl.BlockSpec((1, N_GROUPS * 8, s), lambda b: (b, 0, 0))],
        out_specs=pl.BlockSpec((1, s, c), lambda b: (b, 0, 0)),
        out_shape=jax.ShapeDtypeStruct((nb_, s, c), F32),
        compiler_params=_params("parallel"),
    )(dcol, drow)


def _causal_ok(i, j):
    ab = SEQ_BLOCK
    r = lax.broadcasted_iota(jnp.int32, (ab, ab), 0)
    c = lax.broadcasted_iota(jnp.int32, (ab, ab), 1)
    return (r + (i - j) * ab) >= c


def _causal_ok_t(i, j):
    ab = SEQ_BLOCK
    r = lax.broadcasted_iota(jnp.int32, (ab, ab), 0)
    c = lax.broadcasted_iota(jnp.int32, (ab, ab), 1)
    return (c + (i - j) * ab) >= r


def _ssd_chunk(s_in, x, bm_t, cm, cb, acol, arow, a_prev, ok):
    q = x.shape[0]
    decay = jnp.exp(jnp.where(ok, acol - arow, NEG))
    y = jnp.dot(_bf(cb * decay), x, preferred_element_type=F32)
    y = y + jnp.exp(acol - a_prev) * jnp.dot(cm, _bf(s_in), preferred_element_type=F32)
    a_end = acol[q - 1:q, :]
    wx = _bf(jnp.exp(a_end - acol) * x.astype(F32))
    s_out = jnp.exp(a_end - a_prev) * s_in + jnp.dot(bm_t, wx, preferred_element_type=F32)
    return y, s_out


def _ssd_specs(s):
    xblk = pl.BlockSpec((1, s, GROUP_LANES), lambda b, g: (b, 0, g))
    bblk = pl.BlockSpec((1, s, D_STATE), lambda b, g: (b, 0, g))
    cblk = pl.BlockSpec((1, s, D_STATE), lambda b, g: (b, 0, N_GROUPS + g))
    tblk = pl.BlockSpec((1, 8, s), lambda b, g: (b, (LANES // 8) * g, 0))
    return xblk, bblk, cblk, tblk


def _chunk_views(i, j, x_ref, ac_ref, at_ref):
    ab = SEQ_BLOCK
    sl = slice(ab * i, ab * (i + 1))
    hs = slice(HEAD_LANES * j, HEAD_LANES * (j + 1))
    a_prev = jnp.zeros((1, 1), F32) if i == 0 else ac_ref[0, ab * i - 1:ab * i, j:j + 1]
    return sl, hs, ac_ref[0, sl, j:j + 1], at_ref[0, j:j + 1, sl], a_prev


def _ssd_fwd_chunked(xdtg, bc, acum, acum_t):
    nb_, s, _ = xdtg.shape
    ab = SEQ_BLOCK
    hpg = HEADS_PER_GROUP

    def body(x_ref, b_ref, c_ref, ac_ref, at_ref, y_ref):
        ok = _causal_ok(0, 0)
        states = [jnp.zeros((D_STATE, HEAD_LANES), F32) for _ in range(hpg)]
        for i in range(s // ab):
            bm, cm = b_ref[0, ab * i:ab * (i + 1), :], c_ref[0, ab * i:ab * (i + 1), :]
            bm_t = bm.T
            cb = jnp.dot(cm, bm_t, preferred_element_type=F32)
            ys = []
            for j in range(hpg):
                sl, hs, acol, arow, a_prev = _chunk_views(i, j, x_ref, ac_ref, at_ref)
                y, states[j] = _ssd_chunk(states[j], x_ref[0, sl, hs], bm_t, cm, cb, acol, arow, a_prev, ok)
                ys.append(y)
            y_ref[0, sl, :] = jnp.concatenate(ys, axis=1)

    xblk, bblk, cblk, tblk = _ssd_specs(s)
    ablk = pl.BlockSpec((1, s, LANES), lambda b, g: (b, 0, g))
    return pl.pallas_call(
        body, name="ssd_fwd", grid=(nb_, N_GROUPS), in_specs=[xblk, bblk, cblk, ablk, tblk], out_specs=xblk,
        out_shape=jax.ShapeDtypeStruct((nb_, s, N_GROUPS * GROUP_LANES), F32),
        compiler_params=_params("parallel", "parallel"),
    )(xdtg, bc, bc, acum, acum_t)


def _ssd_bwd_chunked(xdtg, bc, acum, acum_t, dyg):
    nb_, s, _ = xdtg.shape
    ab = SEQ_BLOCK
    nblk = s // ab
    hpg = HEADS_PER_GROUP

    def body(x_ref, b_ref, c_ref, ac_ref, at_ref, dy_ref, dx_ref, db_ref, dc_ref, dac_ref, dar_ref, s_s):
        ok = _causal_ok(0, 0)
        dx_ref[...] = jnp.zeros_like(dx_ref)
        dac_ref[...] = jnp.zeros_like(dac_ref)
        dar_ref[...] = jnp.zeros_like(dar_ref)
        states = [jnp.zeros((D_STATE, HEAD_LANES), F32) for _ in range(hpg)]
        for i in range(nblk):
            bm_t = b_ref[0, ab * i:ab * (i + 1), :].T
            for j in range(hpg):
                sl, hs, acol, arow, a_prev = _chunk_views(i, j, x_ref, ac_ref, at_ref)
                s_s[hpg * i + j] = states[j]
                if i + 1 < nblk:
                    a_end = acol[ab - 1:ab, :]
                    wx = _bf(jnp.exp(a_end - acol) * x_ref[0, sl, hs].astype(F32))
                    states[j] = jnp.exp(a_end - a_prev) * states[j] + jnp.dot(bm_t, wx, preferred_element_type=F32)
        ok_t = _causal_ok_t(0, 0)
        last_row = lax.broadcasted_iota(jnp.int32, (ab, 1), 0) == ab - 1
        d_state = [jnp.zeros((D_STATE, HEAD_LANES), F32) for _ in range(hpg)]
        pending = [jnp.zeros((1, 1), F32) for _ in range(hpg)]
        total = lambda v: jnp.sum(v, keepdims=True)
        for i in reversed(range(nblk)):
            bm, cm = b_ref[0, ab * i:ab * (i + 1), :], c_ref[0, ab * i:ab * (i + 1), :]
            cm_t = cm.T
            cbt = jnp.dot(bm, cm_t, preferred_element_type=F32)
            dcbt = jnp.zeros((ab, ab), F32)
            d_bm, d_cm = jnp.zeros((ab, D_STATE), F32), jnp.zeros((ab, D_STATE), F32)
            for j in range(hpg):
                sl, hs, acol, arow, a_prev = _chunk_views(i, j, x_ref, ac_ref, at_ref)
                x, dy = x_ref[0, sl, hs], dy_ref[0, sl, hs]
                dy16 = _bf(dy)
                s_in, g_out = s_s[hpg * i + j], d_state[j]
                s16, g16 = _bf(s_in), _bf(g_out)
                decay = jnp.exp(jnp.where(ok_t, arow - acol, NEG))
                gt = cbt * decay
                dgt = lax.dot_general(x, dy16, _NT, preferred_element_type=F32)
                d_x = jnp.dot(_bf(gt), dy16, preferred_element_type=F32)
                dcbt = dcbt + dgt * decay
                mm = dgt * gt
                d_arow = jnp.sum(mm, axis=0, keepdims=True)
                d_acol = -jnp.sum(mm, axis=1, keepdims=True)
                e = jnp.exp(acol - a_prev)
                edy16 = _bf(e * dy)
                d_cm = d_cm + lax.dot_general(edy16, s16, _NT, preferred_element_type=F32)
                d_s = jnp.dot(cm_t, edy16, preferred_element_type=F32)
                de_e = jnp.sum(dy * jnp.dot(cm, s16, preferred_element_type=F32), axis=1, keepdims=True) * e
                a_end = acol[ab - 1:ab, :]
                w = jnp.exp(a_end - acol)
                f = jnp.exp(a_end - a_prev)
                x32 = x.astype(F32)
                bg = jnp.dot(bm, g16, preferred_element_type=F32)
                d_x = d_x + w * bg
                d_bm = d_bm + lax.dot_general(_bf(w * x32), g16, _NT, preferred_element_type=F32)
                dw_w = jnp.sum(bg * x32, axis=1, keepdims=True) * w
                df_f = total(g_out * s_in) * f
                d_end = total(dw_w) + df_f
                d_acol = d_acol + de_e - dw_w + jnp.where(last_row, d_end + pending[j], 0.0)
                pending[j] = -total(de_e) - df_f
                d_state[j] = d_s + f * g_out
                dx_ref[0, sl, hs] = d_x
                dac_ref[0, sl, j:j + 1] = d_acol
                dar_ref[0, j:j + 1, sl] = d_arow
            dcbt16 = _bf(dcbt)
            db_ref[0, ab * i:ab * (i + 1), :] = d_bm + jnp.dot(dcbt16, cm, preferred_element_type=F32)
            dc_ref[0, ab * i:ab * (i + 1), :] = d_cm + lax.dot_general(dcbt16, bm, _TN, preferred_element_type=F32)

    xblk, bblk, cblk, tblk = _ssd_specs(s)
    ablk = pl.BlockSpec((1, s, LANES), lambda b, g: (b, 0, g))
    return pl.pallas_call(
        body, name="ssd_bwd", grid=(nb_, N_GROUPS),
        in_specs=[xblk, bblk, cblk, ablk, tblk, xblk],
        out_specs=[xblk, bblk, bblk, ablk, pl.BlockSpec((1, 8, s), lambda b, g: (b, g, 0))],
        out_shape=[jax.ShapeDtypeStruct((nb_, s, N_GROUPS * GROUP_LANES), F32),
                   jax.ShapeDtypeStruct((nb_, s, N_GROUPS * D_STATE), F32),
                   jax.ShapeDtypeStruct((nb_, s, N_GROUPS * D_STATE), F32),
                   jax.ShapeDtypeStruct((nb_, s, N_GROUPS * LANES), F32),
                   jax.ShapeDtypeStruct((nb_, N_GROUPS * 8, s), F32)],
        scratch_shapes=[pltpu.VMEM((nblk * hpg, D_STATE, HEAD_LANES), F32)],
        compiler_params=_params("parallel", "parallel"),
    )(xdtg, bc, bc, acum, acum_t, dyg)


def _interleave(wg, wu):
    k, f = wg.shape
    gi = GATE_UP_INTERLEAVE
    return jnp.stack([wg.reshape(k, f // gi, gi), wu.reshape(k, f // gi, gi)], axis=2).reshape(k, 2 * f)


def _head_expanders():
    e_x = np.zeros((LANES, N_GROUPS * GROUP_LANES), np.float32)
    e_a = np.zeros((LANES, N_GROUPS * LANES), np.float32)
    for h in range(N_HEADS):
        g, j = divmod(h, HEADS_PER_GROUP)
        e_x[h, HEAD_LANES * h:HEAD_LANES * h + HEAD_DIM] = 1.0
        e_a[h, LANES * g + j] = 1.0
    return [jnp.asarray(m, BF16) for m in (e_x, e_x.T, e_a, e_a.T)]


def _pad_lanes(v, n=LANES):
    return jnp.pad(v, ((0, 0), (0, n - v.shape[1])))


def _local_step(x, positions, target, w, late=None, early_grad_job=None):
    nb, s, d = x.shape
    t = nb * s
    x2 = x.reshape(t, d)
    tgt2 = target.reshape(t, d)
    (job_a, weights_a), (job_b, weights_b) = late if late is not None else ((None, None), (None, None))

    x16 = _bf(x2)
    wgu1 = _interleave(w["ffn1_gate"], w["ffn1_up"])
    ffn1 = _ffn_fwd("ffn1_fwd", x16, x2, wgu1, w["ffn1_down"], w["ln1_g"], w["ln1_b"], carry=job_a)
    au1, hm1, h1, r1, h1_16 = ffn1[:5]
    if job_a is not None:
        w = {**w, **weights_a(ffn1[5])}

    w_in = w["w_in"]
    wqk, wv, wz = w_in[:, :2 * D_ATTN], w_in[:, 2 * D_ATTN:3 * D_ATTN], w_in[:, 3 * D_ATTN:3 * D_ATTN + D_SSD]
    wxbc = w_in[:, 3 * D_ATTN + D_SSD:3 * D_ATTN + D_SSD + D_CONV]
    wdt = _pad_lanes(w_in[:, 3 * D_ATTN + D_SSD + D_CONV:])

    inv_freq = ROPE_THETA ** (-jnp.arange(0, ROPE_DIM, 2, dtype=F32) / ROPE_DIM)
    half = ROPE_DIM // 2
    head_invf = jnp.concatenate([inv_freq, inv_freq, jnp.zeros((HEAD_DIM - ROPE_DIM,), F32)])
    head_sgn = jnp.concatenate([-jnp.ones((half,), F32), jnp.ones((half,), F32), jnp.zeros((HEAD_DIM - ROPE_DIM,), F32)])
    invf = jnp.tile(head_invf, LANES // HEAD_DIM)[None, :]
    sgn = jnp.tile(head_sgn, LANES // HEAD_DIM)[None, :]
    posf = positions.astype(F32).reshape(t, 1)
    bias_fwd, bias_bwd = _branch_bias_table(s, FWD_KEY_BLOCK), _branch_bias_table(s, SEQ_BLOCK)
    spreaders = _head_expanders()
    dtb, alog = _pad_lanes(w["dt_bias"]), _pad_lanes(w["a_log"])
    dskip = jnp.repeat(w["d_skip"], HEAD_DIM, axis=1)

    proj = _proj_in(h1_16, _pad_lanes(w_in, w_in.shape[1] - N_HEADS + LANES), posf, invf, sgn, carry=job_b)
    q16, k16, v16, z, xbc_pre, dtp, cs = proj[:7]
    if job_b is not None:
        w = {**w, **weights_b(proj[7])}
    wgu2 = _interleave(w["ffn2_gate"], w["ffn2_up"])
    to3 =lambda a: a.reshape(nb, s, a.shape[-1])
    attn_o, lse = _attn_fwd(to3(q16), to3(k16), to3(v16), bias_fwd)

    xbc = _conv_fwd(to3(xbc_pre), w["conv_w"], w["conv_b"]).reshape(t, D_CONV)
    xdtg, bc16, dag = _ssd_prep_fwd(xbc, dtp, dtb, alog, spreaders)
    acum, acum_t = _cumsum_fwd(to3(dag))
    yg = _ssd_fwd_chunked(to3(xdtg), to3(bc16), acum, acum_t)

    cat = _norms_fwd(attn_o.reshape(t, D_ATTN), yg.reshape(t, -1), xbc, z, w["attn_norm_w"], w["ssd_norm_w"], dskip)
    h2, r2, h2_16 = _mm_res_ln("w_out_ln2", cat, w["w_out"], h1, w["ln2_g"], w["ln2_b"], scale=1.0)

    au2, hm2, _, r3, _ = _ffn_fwd("ffn2_fwd", h2_16, h2, wgu2, w["ffn2_down"], w["ln3_g"], w["ln3_b"])

    g = {}
    dr3, dr3_16, g["ln3_g"], g["ln3_b"], loss = _ln_loss_bwd("loss_ln3_bwd", r3, w["ln3_g"], w["ln3_b"], tgt2)

    dau2, dh2 = _ffn_bwd("ffn2_bwd", dr3_16, dr3, w["ffn2_down"].T * 0.5, au2, wgu2.T)
    g["ffn2_down"] = _mm_tn("ffn2_down_dw", hm2, dr3_16, scale=0.5, tk=D_FF // 2, tn=512)
    g["ffn2_gate"], g["ffn2_up"] = _mm_tn_gate_up("ffn2_up_dw", h2_16, dau2)

    dr2, dr2_16, g["ln2_g"], g["ln2_b"] = _ln_bwd("ln2_bwd", r2, w["ln2_g"], w["ln2_b"], dh2)
    dcat = _mm("w_out_dx", [(dr2_16, w["w_out"].T)], tm=1024, tn=768)
    g["w_out"] = _mm_tn("w_out_dw", cat, dr2_16, tk=768, tn=1024)

    d_attn, dyg, dxs_a, dz16, g["attn_norm_w"], g["ssd_norm_w"], ddskip = _norms_bwd(
        attn_o.reshape(t, D_ATTN), yg.reshape(t, -1), xbc, z, w["attn_norm_w"], w["ssd_norm_w"], dskip, dcat)
    g["d_skip"] = ddskip.reshape(N_HEADS, HEAD_DIM).sum(axis=1)[None, :]

    dq, dk, dv16 = _attn_bwd(to3(q16), to3(k16), to3(v16), attn_o, to3(d_attn), lse, bias_bwd)
    dqk16 = _rope_bwd(dq.reshape(t, D_ATTN), dk.reshape(t, D_ATTN), cs)

    dxdtg, dbm, dcm, dacol, darow = _ssd_bwd_chunked(to3(xdtg), to3(bc16), acum, acum_t, to3(dyg))
    ddag = _cumsum_bwd(dacol, darow)
    dxbc, ddtp16, ddtb, dalog = _ssd_prep_bwd(xbc, dtp, dtb, alog, spreaders, dxdtg.reshape(t, -1), ddag.reshape(t, -1),
                                               dxs_a, dbm.reshape(t, -1), dcm.reshape(t, -1))
    g["dt_bias"], g["a_log"] = ddtb[:, :N_HEADS], dalog[:, :N_HEADS]
    dxbc_pre16, dconv_w, g["conv_b"] = _conv_bwd(to3(xbc_pre), w["conv_w"], w["conv_b"], to3(dxbc))
    g["conv_w"] = dconv_w[:CONV_WIDTH]
    dxbc_pre16 = dxbc_pre16.reshape(t, D_CONV)
    dv16 = dv16.reshape(t, D_ATTN)

    dh1 = _mm("w_in_dx", [(dqk16, wqk.T), (dv16, wv.T), (dz16, wz.T), (dxbc_pre16, wxbc.T), (ddtp16, wdt.T)],
              res=dr2, res_scale=ALPHA, tm=1024)
    g["w_in"] = _mm_tn_sections("w_in_dw", h1_16, [dqk16, dv16, dz16, dxbc_pre16, ddtp16])[:, :w_in.shape[1]]

    dr1, dr1_16, g["ln1_g"], g["ln1_b"] = _ln_bwd("ln1_bwd", r1, w["ln1_g"], w["ln1_b"], dh1)
    g["ffn1_down"] = _mm_tn("ffn1_down_dw", hm1, dr1_16, scale=0.5, tk=D_FF // 2, tn=512)
    ffn1b = _ffn_bwd("ffn1_bwd", dr1_16, dr1, w["ffn1_down"].T * 0.5, au1, wgu1.T,
                     carry=None if early_grad_job is None else early_grad_job(g))
    dau1, dx = ffn1b[:2]
    early = ffn1b[2] if early_grad_job is not None else None
    g["ffn1_gate"], g["ffn1_up"] = _mm_tn_gate_up("ffn1_up_dw", x16, dau1)
    return loss, dx.reshape(nb, s, d), g, early


_HBM = pl.BlockSpec(memory_space=pltpu.HBM)
N_CHIPS = 4
N_DEVICES = 8


def _place():
    return lax.axis_index("x"), lax.axis_index("y"), lax.axis_index("c")


def _other_chips(x, y):
    return [(1 - x, y), (x, 1 - y), (1 - x, 1 - y)]


class _GatherJob:
    def __init__(self, shards):
        assert all((a.shape[0] // 2) % 16 == 0 for a in shards)
        self.n = len(shards)
        self.shapes = [a.shape for a in shards]
        self.operands = [a.reshape(2, a.shape[0] // 2, a.shape[1]) for a in shards]
        self.out_shape = [jax.ShapeDtypeStruct((N_CHIPS,) + a.shape, a.dtype) for a in self.operands]
        pair = pltpu.SemaphoreType.DMA((self.n, N_CHIPS - 1))
        one = pltpu.SemaphoreType.DMA((self.n,))
        self.scratch_shapes = [pair, pair, pair, pair, one, one]

    def results(self, outs):
        return [o.reshape((N_CHIPS,) + s) for o, s in zip(outs, self.shapes)]

    def phases(self, ins, outs, sems):
        n = self.n
        send_sems, recv_sems, fwd_send_sems, fwd_recv_sems, own_send_sems, own_recv_sems = sems
        x, y, c = _place()
        me = 2 * x + y
        peers = _other_chips(x, y)

        def own(t):
            return pltpu.make_async_remote_copy(ins[t], outs[t].at[me], own_send_sems.at[t], own_recv_sems.at[t],
                                                device_id=(x, y, 1 - c), device_id_type=MESH)

        def ici(t, p, src_chip):
            px, py = peers[p]
            return pltpu.make_async_remote_copy(
                ins[t].at[c] if src_chip is None else outs[t].at[src_chip, c],
                outs[t].at[me if src_chip is None else src_chip, c],
                send_sems.at[t, p], recv_sems.at[t, p], device_id=(px, py, c), device_id_type=MESH)

        def d2d(t, p, core):
            px, py = peers[p]
            return pltpu.make_async_remote_copy(
                outs[t].at[2 * px + py, core], outs[t].at[2 * px + py, core],
                fwd_send_sems.at[t, p], fwd_recv_sems.at[t, p], device_id=(x, y, 1 - c), device_id_type=MESH)

        pairs = [(t, p) for t in range(n) for p in range(N_CHIPS - 1)]

        def start():
            for t, p in pairs:
                ici(t, p, None).start()
            for t in range(n):
                own(t).start()

        def forward():
            for t, p in pairs:
                px, py = peers[p]
                ici(t, p, 2 * px + py).wait_recv()
                d2d(t, p, c).start()

        def finish():
            for t, p in pairs:
                d2d(t, p, 1 - c).wait_recv()
            for t in range(n):
                own(t).wait()
            for t, p in pairs:
                ici(t, p, None).wait_send()
                d2d(t, p, c).wait_send()

        return start, forward, finish


class _ExchangeJob:
    def __init__(self, stacks):
        self.n = len(stacks)
        self.operands = list(stacks)
        self.out_shape = [jax.ShapeDtypeStruct(a.shape, a.dtype) for a in stacks]
        pair = pltpu.SemaphoreType.DMA((self.n, N_CHIPS - 1))
        self.scratch_shapes = [pair, pair]

    def results(self, outs):
        return list(outs)

    def phases(self, ins, outs, sems):
        send_sems, recv_sems = sems
        x, y, c = _place()
        me = 2 * x + y
        peers = _other_chips(x, y)
        pairs = [(t, p) for t in range(self.n) for p in range(N_CHIPS - 1)]

        def copy(t, p):
            px, py = peers[p]
            return pltpu.make_async_remote_copy(ins[t].at[2 * px + py], outs[t].at[me], send_sems.at[t, p],
                                                recv_sems.at[t, p], device_id=(px, py, c), device_id_type=MESH)

        def arrival(t, p):
            px, py = peers[p]
            return pltpu.make_async_remote_copy(ins[t].at[me], outs[t].at[2 * px + py], send_sems.at[t, p],
                                                recv_sems.at[t, p], device_id=(px, py, c), device_id_type=MESH)

        def start():
            for t, p in pairs:
                copy(t, p).start()

        def finish():
            for t, p in pairs:
                arrival(t, p).wait_recv()
            for t, p in pairs:
                copy(t, p).wait_send()

        return start, None, finish


def _run_job(job, name):
    n = job.n

    def body(*refs):
        for phase in job.phases(refs[:n], refs[n:2 * n], refs[2 * n:]):
            if phase is not None:
                phase()

    outs = pl.pallas_call(
        body, name=name, in_specs=[_HBM] * n, out_specs=[_HBM] * n,
        out_shape=job.out_shape, scratch_shapes=job.scratch_shapes,
    )(*job.operands)
    return job.results(outs)


def _sibling_halves(stacks, name):
    n = len(stacks)
    halves = [a.shape[1] // 2 for a in stacks]
    split = [a.reshape(a.shape[0], 2, h, a.shape[2]) for a, h in zip(stacks, halves)]

    def body(*refs):
        ins, outs = refs[:n], refs[n:2 * n]
        send_sems, recv_sems = refs[2 * n:]
        x, y, c = _place()
        cps = []
        for t in range(n):
            cp = pltpu.make_async_remote_copy(ins[t].at[:, 1 - c], outs[t], send_sems.at[t], recv_sems.at[t],
                                              device_id=(x, y, 1 - c), device_id_type=MESH)
            cp.start()
            cps.append(cp)
        for cp in cps:
            cp.wait()

    return pl.pallas_call(
        body, name=name,
        in_specs=[_HBM] * n, out_specs=[_HBM] * n,
        out_shape=[jax.ShapeDtypeStruct((a.shape[0], h, a.shape[2]), a.dtype) for a, h in zip(stacks, halves)],
        scratch_shapes=[pltpu.SemaphoreType.DMA((n,)), pltpu.SemaphoreType.DMA((n,))],
    )(*split)


def _sibling_swap(arrs):
    n = len(arrs)

    def body(*refs):
        ins, outs = refs[:n], refs[n:2 * n]
        send_sems, recv_sems = refs[2 * n:]
        x, y, c = _place()
        cps = []
        for t in range(n):
            cp = pltpu.make_async_remote_copy(ins[t], outs[t], send_sems.at[t], recv_sems.at[t],
                                              device_id=(x, y, 1 - c), device_id_type=MESH)
            cp.start()
            cps.append(cp)
        for cp in cps:
            cp.wait()

    return pl.pallas_call(
        body, name="sibling_swap",
        in_specs=[_HBM] * n, out_specs=[_HBM] * n,
        out_shape=[jax.ShapeDtypeStruct(a.shape, a.dtype) for a in arrs],
        scratch_shapes=[pltpu.SemaphoreType.DMA((n,)), pltpu.SemaphoreType.DMA((n,))],
    )(*arrs)


def _half_sum(name, own, other, core):
    k, r, cols = own.shape
    h = r // 2
    tr = next(cand for cand in (128, 176, 64, 32, 16) if h % cand == 0)
    nblk = h // tr

    def body(core_ref, own_ref, other_ref, o_ref):
        o_ref[...] = _bf(own_ref[...] + other_ref[...].astype(F32))

    grid_spec = pltpu.PrefetchScalarGridSpec(
        num_scalar_prefetch=1, grid=(nblk,),
        in_specs=[pl.BlockSpec((k, tr, cols), lambda i, core_ref: (0, i + core_ref[0] * nblk, 0)),
                  pl.BlockSpec((k, tr, cols), lambda i, core_ref: (0, i, 0))],
        out_specs=pl.BlockSpec((k, tr, cols), lambda i, core_ref: (0, i, 0)))
    return pl.pallas_call(
        body, name=name, grid_spec=grid_spec, out_shape=jax.ShapeDtypeStruct((k, h, cols), BF16),
        compiler_params=_params("parallel"),
    )(core.reshape(1).astype(jnp.int32), own, other)


def _small_allreduce(v):
    r = v.shape[0]

    def body(v_ref, tot_ref, slots, send_sems, recv_sems):
        x, y, c = _place()
        me = 4 * x + 2 * y + c
        slots[me] = v_ref[...]
        cps, peers = [], []
        for k in range(1, N_DEVICES):
            px = 1 - x if (k >> 2) & 1 else x
            py = 1 - y if (k >> 1) & 1 else y
            pc = 1 - c if k & 1 else c
            cp = pltpu.make_async_remote_copy(v_ref, slots.at[me], send_sems.at[k - 1], recv_sems.at[k - 1],
                                              device_id=(px, py, pc), device_id_type=MESH)
            cp.start()
            cps.append(cp)
            peers.append((px, py, pc))
        for k, (px, py, pc) in enumerate(peers):
            pltpu.make_async_remote_copy(v_ref, slots.at[4 * px + 2 * py + pc], send_sems.at[k], recv_sems.at[k],
                                         device_id=(px, py, pc), device_id_type=MESH).wait_recv()
        for cp in cps:
            cp.wait_send()
        acc = slots[0]
        for s in range(1, N_DEVICES):
            acc = acc + slots[s]
        tot_ref[...] = acc

    return pl.pallas_call(
        body, name="small_allreduce",
        in_specs=[pl.BlockSpec(memory_space=pltpu.VMEM)], out_specs=pl.BlockSpec(memory_space=pltpu.VMEM),
        out_shape=jax.ShapeDtypeStruct((r, LANES), F32),
        scratch_shapes=[pltpu.VMEM((N_DEVICES, r, LANES), F32), pltpu.SemaphoreType.DMA((N_DEVICES - 1,)),
                        pltpu.SemaphoreType.DMA((N_DEVICES - 1,))],
    )(v)


def _elementwise(name, fn, ins, out_dtypes):
    r, c = ins[0].shape[-2:]
    tr = next((cand for cand in (256, 176, 128, 64, 32, 16) if r % cand == 0), r)
    nin = len(ins)

    def body(*refs):
        outs = fn(*[ref[...] for ref in refs[:nin]])
        for o_ref, o in zip(refs[nin:], outs):
            o_ref[...] = o.astype(o_ref.dtype)

    in_specs = [pl.BlockSpec((tr, c), lambda i: (i, 0)) if a.ndim == 2 else pl.BlockSpec((a.shape[0], tr, c), lambda i: (0, i, 0))
                for a in ins]
    return pl.pallas_call(
        body, name=name, grid=(r // tr,), in_specs=in_specs,
        out_specs=[pl.BlockSpec((tr, c), lambda i: (i, 0)) for _ in out_dtypes],
        out_shape=[jax.ShapeDtypeStruct((r, c), dt) for dt in out_dtypes],
        compiler_params=_params("parallel"),
    )(*ins)


def _row_tile(rows):
    return next((cand for cand in (128, 176, 64, 32, 16) if rows % cand == 0), rows)


def _sum_slots(name, received, own, chip):
    _, r, cols = own.shape
    tr = _row_tile(r)

    def body(chip_ref, own_ref, a_ref, b_ref, c_ref, o_ref):
        o_ref[...] = ((own_ref[0].astype(F32) + a_ref[0].astype(F32)) + b_ref[0].astype(F32)) + c_ref[0].astype(F32)

    def slot(flip):
        return pl.BlockSpec((1, tr, cols), lambda i, chip_ref: (jnp.bitwise_xor(chip_ref[0], flip), i, 0))

    grid_spec = pltpu.PrefetchScalarGridSpec(
        num_scalar_prefetch=1, grid=(r // tr,), in_specs=[slot(0), slot(1), slot(2), slot(3)],
        out_specs=pl.BlockSpec((tr, cols), lambda i, chip_ref: (i, 0)))
    return pl.pallas_call(
        body, name=name, grid_spec=grid_spec, out_shape=jax.ShapeDtypeStruct((r, cols), F32),
        compiler_params=_params("parallel"),
    )(chip.reshape(1).astype(jnp.int32), own, received, received, received)


def _adamw_halves(name, mine, theirs, core, w, m, v):
    h, cols = mine.shape
    tr = _row_tile(h)
    nh = h // tr

    def body(core_ref, mine_ref, theirs_ref, w_ref, m_ref, v_ref, g_ref, d_ref, m2_ref, v2_ref):
        is_mine = (pl.program_id(0) // nh) == core_ref[0]
        g = jnp.where(is_mine, mine_ref[...], theirs_ref[...])
        outs = _adamw_math(g, w_ref[...], m_ref[...], v_ref[...])
        for ref, val in zip((g_ref, d_ref, m2_ref, v2_ref), outs):
            ref[...] = val

    half = pl.BlockSpec((tr, cols), lambda i, core_ref: (i % nh, 0))
    full = pl.BlockSpec((tr, cols), lambda i, core_ref: (i, 0))
    grid_spec = pltpu.PrefetchScalarGridSpec(
        num_scalar_prefetch=1, grid=(2 * nh,), in_specs=[half, half, full, full, full], out_specs=[full] * 4)
    return pl.pallas_call(
        body, name=name, grid_spec=grid_spec, out_shape=[jax.ShapeDtypeStruct((2 * h, cols), F32)] * 4,
        compiler_params=_params("parallel"),
    )(core.reshape(1).astype(jnp.int32), mine, theirs, w, m, v)


def _adamw_math(g, w_v, m_v, v_v):
    m2 = ADAM_B1 * m_v + (1.0 - ADAM_B1) * g
    v2 = ADAM_B2 * v_v + (1.0 - ADAM_B2) * jnp.square(g)
    m_hat = m2 / (1.0 - ADAM_B1 ** ADAM_STEP)
    v_hat = v2 / (1.0 - ADAM_B2 ** ADAM_STEP)
    delta = -ADAM_LR * (m_hat / (jnp.sqrt(v_hat) + ADAM_EPS) + ADAM_WD * w_v)
    return [g, delta, m2, v2]


def _adamw(name, g, w, m, v):
    return _elementwise(name, _adamw_math, [g, w, m, v], [F32] * 4)


_TRANSPOSED = ("ffn1_gate", "ffn1_up", "ffn2_gate", "ffn2_up")
_MATRICES = (("ffn1_gate", 0), ("ffn1_up", 0), ("ffn1_down", 0), ("w_in", 1), ("w_out", 0),
             ("ffn2_gate", 0), ("ffn2_up", 0), ("ffn2_down", 0))


def _block2d(a, name):
    return jnp.swapaxes(a, 1, 2)[0] if name in _TRANSPOSED else a[0]


def _block3d(a, name):
    return jnp.swapaxes(a[None], 1, 2) if name in _TRANSPOSED else a[None]
_VECTORS = ("ln1_g", "ln1_b", "conv_b", "dt_bias", "a_log", "d_skip", "attn_norm_w", "ssd_norm_w",
            "ln2_g", "ln2_b", "ln3_g", "ln3_b")
_WEIGHT_ORDER = ("ln1_g", "ln1_b", "ffn1_gate", "ffn1_up", "ffn1_down", "w_in", "conv_w", "conv_b", "dt_bias", "a_log",
                 "d_skip", "attn_norm_w", "ssd_norm_w", "w_out", "ln2_g", "ln2_b", "ffn2_gate", "ffn2_up", "ffn2_down",
                 "ln3_g", "ln3_b")


def _pack_rows(vectors):
    parts = []
    for vec in vectors:
        flat = vec.reshape(-1)
        parts.append(jnp.pad(flat, (0, (-flat.shape[0]) % LANES)))
    flat = jnp.concatenate(parts)
    flat = jnp.pad(flat, (0, (-flat.shape[0]) % (8 * LANES)))
    return flat.reshape(-1, LANES)


def _unpack_rows(packed, shapes):
    flat = packed.reshape(-1)
    out, off = [], 0
    for shape in shapes:
        size = int(np.prod(shape))
        out.append(flat[off:off + size].reshape(shape))
        off += size + (-size) % LANES
    return out


def _assemble(stack, axis):
    if axis == 0:
        return stack.reshape(-1, stack.shape[2])
    return jnp.concatenate([stack[s] for s in range(N_CHIPS)], axis=1)


def _split(full, axis):
    if axis == 0:
        return full.reshape(N_CHIPS, -1, full.shape[1])
    cols = full.shape[1] // N_CHIPS
    return jnp.stack([full[:, cols * s:cols * (s + 1)] for s in range(N_CHIPS)])


def kernel(x, positions, ln1_g, ln1_b, ffn1_gate, ffn1_up, ffn1_down, w_in, conv_w, conv_b, dt_bias, a_log, d_skip, attn_norm_w, ssd_norm_w, w_out, ln2_g, ln2_b, ffn2_gate, ffn2_up, ffn2_down, ln3_g, ln3_b, loss_target, m_ln1_g, m_ln1_b, m_ffn1_gate, m_ffn1_up, m_ffn1_down, m_w_in, m_conv_w, m_conv_b, m_dt_bias, m_a_log, m_d_skip, m_attn_norm_w, m_ssd_norm_w, m_w_out, m_ln2_g, m_ln2_b, m_ffn2_gate, m_ffn2_up, m_ffn2_down, m_ln3_g, m_ln3_b, v_ln1_g, v_ln1_b, v_ffn1_gate, v_ffn1_up, v_ffn1_down, v_w_in, v_conv_w, v_conv_b, v_dt_bias, v_a_log, v_d_skip, v_attn_norm_w, v_ssd_norm_w, v_w_out, v_ln2_g, v_ln2_b, v_ffn2_gate, v_ffn2_up, v_ffn2_down, v_ln3_g, v_ln3_b):
    given = dict(locals())
    wts = {n: given[n] for n in _WEIGHT_ORDER}
    mom_m = {n: given["m_" + n] for n in _WEIGHT_ORDER}
    mom_v = {n: given["v_" + n] for n in _WEIGHT_ORDER}
    chip = 2 * lax.axis_index("x") + lax.axis_index("y")

    core = lax.axis_index("c")
    groups = [[(n, axis) for n, axis in _MATRICES if n.startswith(prefix)] for prefix in ("ffn1", "w_", "ffn2")]
    own16 = {n: _block2d(wts[n], n).astype(BF16) for n, _ in _MATRICES}

    def full_weights(group, results):
        out = {}
        for (n, axis), st in zip(group, results):
            whole = _assemble(st, axis)
            out[n] = whole.T if n in _TRANSPOSED else whole
        return out

    full = full_weights(groups[0], _run_job(_GatherJob([own16[n] for n, _ in groups[0]]), "gather_ffn1"))
    for n in _VECTORS:
        full[n] = wts[n]
    conv_rows = jnp.pad(wts["conv_w"][0], ((0, 32 - CONV_WIDTH), (0, 0)))

    def mixer_weights(results):
        out = full_weights(groups[1], results)
        out["conv_w"] = _assemble(results[-1], 1)[:CONV_WIDTH]
        return out

    def ffn2_weights(results):
        return full_weights(groups[2], results)

    late = [(_GatherJob([own16[n] for n, _ in groups[1]] + [conv_rows]), mixer_weights),
            (_GatherJob([own16[n] for n, _ in groups[2]]), ffn2_weights)]

    chip_sums = {}

    def core_sums(g, which, tag):
        partials = [_split(g[n], axis) for n, axis in which]
        from_sibling = _sibling_halves([p.astype(BF16) for p in partials], "sibling_halves_" + tag)
        for (n, _), p, o in zip(which, partials, from_sibling):
            chip_sums[n] = _half_sum("core_sum_" + n, p, o, core)
        return _ExchangeJob([chip_sums[n] for n, _ in which])

    last = [(n, axis) for n, axis in _MATRICES if n in ("ffn1_gate", "ffn1_up")]
    early = [(n, axis) for n, axis in _MATRICES if (n, axis) not in last]
    loss, grad_x, g, received_early = _local_step(x, positions, loss_target, full, late,
                                                  lambda g_now: core_sums(g_now, early, "early"))
    received_last = _run_job(core_sums(g, last, "last"), "exchange_last")
    received = dict(zip([n for n, _ in last + early], received_last + received_early))
    half_totals = [_sum_slots("sum_partials_" + n, received[n], chip_sums[n], chip) for n, _ in _MATRICES]
    other_halves = _sibling_swap(half_totals)

    small_shapes = [g[n].shape for n in _VECTORS] + [g["conv_w"].shape, (1,)]
    total = _small_allreduce(_pack_rows([g[n] for n in _VECTORS] + [g["conv_w"], loss[0, :1]]))
    small = _unpack_rows(total, small_shapes)
    loss_out = small[-1].reshape(())

    grads, deltas, new_m, new_v = {}, {}, {}, {}
    for (n, _), mine, theirs in zip(_MATRICES, half_totals, other_halves):
        res = _adamw_halves("adamw_" + n, mine, theirs, core, _block2d(wts[n], n), _block2d(mom_m[n], n), _block2d(mom_v[n], n))
        grads[n], deltas[n], new_m[n], new_v[n] = [_block3d(r, n) for r in res]

    vec_shapes = [wts[n].shape for n in _VECTORS]
    res = _adamw("adamw_vectors", _pack_rows(small[:len(_VECTORS)]), _pack_rows([wts[n] for n in _VECTORS]),
                 _pack_rows([mom_m[n] for n in _VECTORS]), _pack_rows([mom_v[n] for n in _VECTORS]))
    for dst, packed in zip((grads, deltas, new_m, new_v), res):
        for n, val in zip(_VECTORS, _unpack_rows(packed, vec_shapes)):
            dst[n] = val

    cols = conv_w.shape[2]
    g_conv = lax.dynamic_slice_in_dim(small[len(_VECTORS)], chip * cols, cols, axis=1)
    res = _adamw("adamw_conv_w", g_conv, wts["conv_w"][0], mom_m["conv_w"][0], mom_v["conv_w"][0])
    grads["conv_w"], deltas["conv_w"], new_m["conv_w"], new_v["conv_w"] = [r[None] for r in res]

    return (loss_out, grad_x, *[grads[n] for n in _WEIGHT_ORDER], *[deltas[n] for n in _WEIGHT_ORDER],
            *[new_m[n] for n in _WEIGHT_ORDER], *[new_v[n] for n in _WEIGHT_ORDER])
```

```python
import numpy as np
import jax
import jax.numpy as jnp
from jax import lax
from jax.experimental import pallas as pl
from jax.experimental.pallas import tpu as pltpu

F32, BF16 = jnp.float32, jnp.bfloat16

D_MODEL = 1024
D_FF = 2816
N_HEADS = 12
HEAD_DIM = 64
D_ATTN = 768
D_SSD = 768
N_GROUPS = 4
HEADS_PER_GROUP = 3
D_STATE = 128
D_CONV = 1792
CONV_WIDTH = 4
ROPE_DIM = 16
ROPE_THETA = 500000.0
ALPHA = 2.0 ** 0.25
LN_EPS = 1e-5
RMS_EPS = 1e-6
ADAM_LR, ADAM_B1, ADAM_B2, ADAM_EPS, ADAM_WD, ADAM_STEP = 0.001, 0.9, 0.999, 1e-08, 0.01, 10

LANES = 128
GATE_UP_INTERLEAVE = 256
SEQ_BLOCK = 256
HEAD_LANES = 128
GROUP_LANES = 3 * HEAD_LANES
VMEM_LIMIT = 56 * 1024 * 1024
NEG = -1e30
MESH = pl.DeviceIdType.MESH
HIGHEST = lax.Precision.HIGHEST

_NT = (((1,), (1,)), ((), ()))
_TN = (((0,), (0,)), ((), ()))


def _params(*sem):
    return pltpu.CompilerParams(dimension_semantics=sem, vmem_limit_bytes=VMEM_LIMIT)


def _bf(v):
    return v.astype(BF16)


EPILOGUE_ROWS = 128


def _row_chunks(tm):
    return [slice(r, min(r + EPILOGUE_ROWS, tm)) for r in range(0, tm, EPILOGUE_ROWS)]


def _sigmoid(v):
    return 0.5 * jnp.tanh(0.5 * v) + 0.5


def _mm(name, pairs, *, scale=1.0, res=None, res_scale=1.0, out_dtype=F32, tm=512, tn=512):
    m, n = pairs[0][0].shape[0], pairs[0][1].shape[1]
    tm, tn = min(tm, m), min(tn, n)
    assert m % tm == 0 and n % tn == 0, (name, m, n, tm, tn)
    npair = len(pairs)

    def body(*refs):
        acc = None
        for a_ref, b_ref in zip(refs[:npair], refs[npair:2 * npair]):
            d = jnp.dot(_bf(a_ref[...]), b_ref[...], preferred_element_type=F32)
            acc = d if acc is None else acc + d
        if scale != 1.0:
            acc = acc * scale
        if res is not None:
            acc = acc + res_scale * refs[2 * npair][...]
        refs[-1][...] = acc.astype(out_dtype)

    in_specs = [pl.BlockSpec((tm, a.shape[1]), lambda i, j: (i, 0)) for a, _ in pairs]
    in_specs += [pl.BlockSpec((b.shape[0], tn), lambda i, j: (0, j)) for _, b in pairs]
    args = [a for a, _ in pairs] + [b for _, b in pairs]
    if res is not None:
        in_specs.append(pl.BlockSpec((tm, tn), lambda i, j: (i, j)))
        args.append(res)
    return pl.pallas_call(
        body, name=name, grid=(m // tm, n // tn), in_specs=in_specs,
        out_specs=pl.BlockSpec((tm, tn), lambda i, j: (i, j)),
        out_shape=jax.ShapeDtypeStruct((m, n), out_dtype),
        compiler_params=_params("parallel", "parallel"),
    )(*args)


def _mm_tn(name, x, dy, *, scale=1.0, tk=512, tn=512, tt=2048):
    t, k = x.shape
    n = dy.shape[1]
    tk, tn, tt = min(tk, k), min(tn, n), min(tt, t)
    assert k % tk == 0 and n % tn == 0 and t % tt == 0, (name, k, n, t)
    nt = t // tt

    def body(x_ref, dy_ref, o_ref):
        step = pl.program_id(2)
        d = lax.dot_general(_bf(x_ref[...]), _bf(dy_ref[...]), _TN, preferred_element_type=F32)

        @pl.when(step == 0)
        def _():
            o_ref[...] = d

        @pl.when(step > 0)
        def _():
            o_ref[...] += d

        if scale != 1.0:
            @pl.when(step == nt - 1)
            def _():
                o_ref[...] = o_ref[...] * scale

    return pl.pallas_call(
        body, name=name, grid=(k // tk, n // tn, nt),
        in_specs=[pl.BlockSpec((tt, tk), lambda i, j, s: (s, i)), pl.BlockSpec((tt, tn), lambda i, j, s: (s, j))],
        out_specs=pl.BlockSpec((tk, tn), lambda i, j, s: (i, j)),
        out_shape=jax.ShapeDtypeStruct((k, n), F32),
        compiler_params=_params("parallel", "parallel", "arbitrary"),
    )(x, dy)


def _mm_tn_sections(name, x, dys, *, tt=512):
    t, k = x.shape
    tt = min(tt, t)
    cuts = np.cumsum([0] + [d.shape[1] for d in dys]).tolist()
    ns = len(dys)

    def body(*refs):
        x_ref, o_ref = refs[0], refs[1 + ns]
        step = pl.program_id(0)
        xt = x_ref[...].T
        parts = [jnp.dot(xt, refs[1 + a][...], preferred_element_type=F32) for a in range(ns)]

        @pl.when(step == 0)
        def _():
            for a in range(ns):
                o_ref[:, cuts[a]:cuts[a + 1]] = parts[a]

        @pl.when(step > 0)
        def _():
            for a in range(ns):
                o_ref[:, cuts[a]:cuts[a + 1]] += parts[a]

    return pl.pallas_call(
        body, name=name, grid=(t // tt,),
        in_specs=[pl.BlockSpec((tt, k), lambda s: (s, 0))] + [pl.BlockSpec((tt, d.shape[1]), lambda s: (s, 0)) for d in dys],
        out_specs=pl.BlockSpec((k, cuts[-1]), lambda s: (0, 0)),
        out_shape=jax.ShapeDtypeStruct((k, cuts[-1]), F32),
        compiler_params=_params("arbitrary"),
    )(x, *dys)


def _mm_tn_gate_up(name, x, dau, *, tt=2048):
    t, k = x.shape
    gi = GATE_UP_INTERLEAVE
    nj = dau.shape[1] // (2 * gi)
    tt = min(tt, t)
    nt = t // tt

    def body(x_ref, dy_ref, g_ref, u_ref):
        step = pl.program_id(1)
        d = lax.dot_general(dy_ref[...], _bf(x_ref[...]), _TN, preferred_element_type=F32)

        @pl.when(step == 0)
        def _():
            g_ref[...] = d[:gi]
            u_ref[...] = d[gi:]

        @pl.when(step > 0)
        def _():
            g_ref[...] += d[:gi]
            u_ref[...] += d[gi:]

    out = pl.BlockSpec((gi, k), lambda j, s: (j, 0))
    return pl.pallas_call(
        body, name=name, grid=(nj, nt),
        in_specs=[pl.BlockSpec((tt, k), lambda j, s: (s, 0)), pl.BlockSpec((tt, 2 * gi), lambda j, s: (s, j))],
        out_specs=[out, out],
        out_shape=[jax.ShapeDtypeStruct((gi * nj, k), F32)] * 2,
        compiler_params=_params("parallel", "arbitrary"),
    )(x, dau)


def _carried(carry, ins, outs, sems, step, total):
    start, forward, finish = carry.phases(ins, outs, sems)
    pl.when(step == 0)(start)
    if forward is not None:
        pl.when(step == (3 * total) // 4)(forward)
    return lambda: pl.when(step == total - 1)(finish)


def _resident(shape):
    return pl.BlockSpec(shape, lambda i: (0,) * len(shape), pipeline_mode=pl.Buffered(1))


def _ffn_fwd(name, x16, res, wgu, wd, g, b, *, tm=512, carry=None):
    t, k = x16.shape
    gi = GATE_UP_INTERLEAVE
    nj, n, ni = wd.shape[0] // gi, wd.shape[1], t // tm
    nc = carry.n if carry is not None else 0

    def body(*refs):
        x_ref, res_ref, wgu_ref, wd_ref, g_ref, b_ref = refs[:6]
        au_ref, hm_ref, y_ref, r_ref, y16_ref = refs[6 + nc:11 + nc]
        if carry is not None:
            finish = _carried(carry, refs[6:6 + nc], refs[11 + nc:11 + 2 * nc], refs[11 + 2 * nc:], pl.program_id(0), ni)
        xv = x_ref[...]
        acc = jnp.zeros((tm, n), F32)
        for j in range(nj):
            au = jnp.dot(xv, wgu_ref[:, 2 * gi * j:2 * gi * (j + 1)], preferred_element_type=F32)
            a, u = au[:, :gi], au[:, gi:]
            au_ref[:, 2 * gi * j:2 * gi * (j + 1)] = _bf(au)
            hm = _bf(a * _sigmoid(a) * u)
            hm_ref[:, gi * j:gi * (j + 1)] = hm
            acc = acc + jnp.dot(hm, wd_ref[gi * j:gi * (j + 1), :], preferred_element_type=F32)
        r = ALPHA * res_ref[...] + 0.5 * acc
        r_ref[...] = r
        y = _layer_norm(r, g_ref[...], b_ref[...])
        y_ref[...] = y
        y16_ref[...] = _bf(y)
        if carry is not None:
            finish()

    row = lambda c: pl.BlockSpec((tm, c), lambda i: (i, 0))
    hbm = pl.BlockSpec(memory_space=pltpu.HBM)
    res_ = pl.pallas_call(
        body, name=name, grid=(ni,),
        in_specs=[row(k), row(n), _resident(wgu.shape), _resident(wd.shape), _resident(g.shape), _resident(b.shape)] + [hbm] * nc,
        out_specs=[row(2 * gi * nj), row(gi * nj), row(n), row(n), row(n)] + [hbm] * nc,
        out_shape=[jax.ShapeDtypeStruct((t, 2 * gi * nj), BF16), jax.ShapeDtypeStruct((t, gi * nj), BF16),
                   jax.ShapeDtypeStruct((t, n), F32), jax.ShapeDtypeStruct((t, n), F32), jax.ShapeDtypeStruct((t, n), BF16)]
        + (carry.out_shape if carry is not None else []),
        scratch_shapes=carry.scratch_shapes if carry is not None else [],
        compiler_params=_params("arbitrary" if carry is not None else "parallel"),
    )(x16, res, wgu, wd, g, b, *(carry.operands if carry is not None else []))
    return tuple(res_[:5]) + ((carry.results(res_[5:]),) if carry is not None else ())


def _ffn_bwd(name, dr16, dr, wdt, au, wgut, *, tm=512, carry=None):
    t, n = dr16.shape
    gi = GATE_UP_INTERLEAVE
    nj, ni = wdt.shape[1] // gi, t // tm
    nc = carry.n if carry is not None else 0

    def body(*refs):
        dr16_ref, dr_ref, wdt_ref, au_ref, wgut_ref = refs[:5]
        dau_ref, dx_ref = refs[5 + nc:7 + nc]
        if carry is not None:
            finish = _carried(carry, refs[5:5 + nc], refs[7 + nc:7 + 2 * nc], refs[7 + 2 * nc:], pl.program_id(0), ni)
        drv = dr16_ref[...]
        acc = jnp.zeros((tm, n), F32)
        for j in range(nj):
            dhm = jnp.dot(drv, wdt_ref[:, gi * j:gi * (j + 1)], preferred_element_type=F32)
            au_v = au_ref[:, 2 * gi * j:2 * gi * (j + 1)].astype(F32)
            a, u = au_v[:, :gi], au_v[:, gi:]
            sig = _sigmoid(a)
            silu = a * sig
            dau = jnp.concatenate([_bf(dhm * u * (sig + silu - silu * sig)), _bf(dhm * silu)], axis=1)
            dau_ref[:, 2 * gi * j:2 * gi * (j + 1)] = dau
            acc = acc + jnp.dot(dau, wgut_ref[2 * gi * j:2 * gi * (j + 1), :], preferred_element_type=F32)
        dx_ref[...] = ALPHA * dr_ref[...] + acc
        if carry is not None:
            finish()

    row = lambda c: pl.BlockSpec((tm, c), lambda i: (i, 0))
    hbm = pl.BlockSpec(memory_space=pltpu.HBM)
    res_ = pl.pallas_call(
        body, name=name, grid=(ni,),
        in_specs=[row(n), row(n), _resident(wdt.shape), row(2 * gi * nj), _resident(wgut.shape)] + [hbm] * nc,
        out_specs=[row(2 * gi * nj), row(n)] + [hbm] * nc,
        out_shape=[jax.ShapeDtypeStruct((t, 2 * gi * nj), BF16), jax.ShapeDtypeStruct((t, n), F32)]
        + (carry.out_shape if carry is not None else []),
        scratch_shapes=carry.scratch_shapes if carry is not None else [],
        compiler_params=_params("arbitrary" if carry is not None else "parallel"),
    )(dr16, dr, wdt, au, wgut, *(carry.operands if carry is not None else []))
    return tuple(res_[:2]) + ((carry.results(res_[2:]),) if carry is not None else ())


def _layer_norm(r, g, b):
    mu = jnp.mean(r, axis=-1, keepdims=True)
    var = jnp.mean(jnp.square(r - mu), axis=-1, keepdims=True)
    return (r - mu) * lax.rsqrt(var + LN_EPS) * g + b


def _mm_res_ln(name, a, w, res, g, b, *, scale, tm=512):
    t, k = a.shape
    n = w.shape[1]

    def body(a_ref, w_ref, res_ref, g_ref, b_ref, y_ref, r_ref, y16_ref):
        for rows in _row_chunks(tm):
            r = ALPHA * res_ref[rows, :] + scale * jnp.dot(_bf(a_ref[rows, :]), w_ref[...], preferred_element_type=F32)
            r_ref[rows, :] = r
            y = _layer_norm(r, g_ref[...], b_ref[...])
            y_ref[rows, :] = y
            y16_ref[rows, :] = _bf(y)

    row = lambda c: pl.BlockSpec((tm, c), lambda i: (i, 0))
    const = lambda shape: pl.BlockSpec(shape, lambda i: (0, 0))
    return pl.pallas_call(
        body, name=name, grid=(t // tm,),
        in_specs=[row(k), const((k, n)), row(n), const((1, n)), const((1, n))],
        out_specs=[row(n), row(n), row(n)],
        out_shape=[jax.ShapeDtypeStruct((t, n), F32), jax.ShapeDtypeStruct((t, n), F32), jax.ShapeDtypeStruct((t, n), BF16)],
        compiler_params=_params("parallel"),
    )(a, w, res, g, b)


def _rowwise(name, fn, rows, consts, row_outs, acc_outs=(), tm=512):
    rows = [r if isinstance(r, tuple) else (r, r.shape[1]) for r in rows]
    t = rows[0][0].shape[0]
    tm = min(tm, t)
    assert t % tm == 0
    nr, nc, no, na = len(rows), len(consts), len(row_outs), len(acc_outs)

    def body(*refs):
        vals = [r[...] for r in refs[:nr + nc]]
        outs, accs = fn(*vals)
        for o_ref, o in zip(refs[nr + nc:nr + nc + no], outs):
            o_ref[...] = o.astype(o_ref.dtype)
        if na:
            step = pl.program_id(0)
            acc_refs = refs[nr + nc + no:]

            @pl.when(step == 0)
            def _():
                for a_ref, a in zip(acc_refs, accs):
                    a_ref[...] = a

            @pl.when(step > 0)
            def _():
                for a_ref, a in zip(acc_refs, accs):
                    a_ref[...] += a

    in_specs = [pl.BlockSpec((tm, w), lambda i: (i, 0)) for _, w in rows]
    in_specs += [pl.BlockSpec(c.shape, lambda i, nd=c.ndim: (0,) * nd) for c in consts]
    out_specs = [pl.BlockSpec((tm, c), lambda i: (i, 0)) for c, _ in row_outs]
    out_specs += [pl.BlockSpec(s, lambda i: (0, 0)) for s in acc_outs]
    out_shape = [jax.ShapeDtypeStruct((t, c), dt) for c, dt in row_outs]
    out_shape += [jax.ShapeDtypeStruct(s, F32) for s in acc_outs]
    res = pl.pallas_call(
        body, name=name, grid=(t // tm,), in_specs=in_specs, out_specs=out_specs, out_shape=out_shape,
        compiler_params=_params("arbitrary" if na else "parallel"),
    )(*[r for r, _ in rows], *consts)
    return res


def _ln_bwd(name, r, g, b, dy):
    def fn(r_v, dy_v, g_v, b_v):
        _, vjp = jax.vjp(_layer_norm, r_v, g_v, b_v)
        dr, dg, db = vjp(dy_v)
        return [dr, dr], [dg, db]
    return _rowwise(name, fn, [r, dy], [g, b], [(r.shape[1], F32), (r.shape[1], BF16)], [(1, r.shape[1])] * 2)


def _ln_loss_bwd(name, r, g, b, target):
    def fn(r_v, t_v, g_v, b_v):
        def loss_fn(rr, gg, bb):
            err = jnp.square(_layer_norm(rr, gg, bb) - t_v)
            return 0.5 * jnp.sum(jnp.mean(err, axis=-1, keepdims=True), axis=0, keepdims=True)
        loss, vjp = jax.vjp(loss_fn, r_v, g_v, b_v)
        dr, dg, db = vjp(jnp.ones((1, 1), F32))
        return [dr, dr], [dg, db, jnp.broadcast_to(loss, (1, LANES))]
    return _rowwise(name, fn, [r, target], [g, b], [(r.shape[1], F32), (r.shape[1], BF16)],
                    [(1, r.shape[1])] * 2 + [(1, LANES)])


def _rope_tables(posf, invf, sgn):
    ang = posf * invf
    return jnp.cos(ang), jnp.sin(ang) * sgn


def _rope_apply(tv, cos, sin):
    lane = lax.broadcasted_iota(jnp.int32, cos.shape, 1)
    first = (lane % HEAD_DIM) < (ROPE_DIM // 2)
    outs = []
    for gidx in range(tv.shape[1] // LANES):
        tg = tv[:, LANES * gidx:LANES * (gidx + 1)]
        sw = jnp.where(first, pltpu.roll(tg, LANES - ROPE_DIM // 2, 1), pltpu.roll(tg, ROPE_DIM // 2, 1))
        outs.append(tg * cos + sw * sin)
    return jnp.concatenate(outs, axis=1)


def _proj_in(h16, w_in, posf, invf, sgn, *, tm=512, carry=None):
    t, k = h16.shape
    cuts = [0, D_ATTN, 2 * D_ATTN, 3 * D_ATTN, 3 * D_ATTN + D_SSD, 3 * D_ATTN + D_SSD + D_CONV, w_in.shape[1]]
    nc = carry.n if carry is not None else 0

    def body(*refs):
        h_ref, w_ref, pos_ref, invf_ref, sgn_ref = refs[:5]
        q_ref, k_ref, v_ref, z_ref, xbc_ref, dt_ref, cs_ref = refs[5 + nc:12 + nc]
        if carry is not None:
            finish = _carried(carry, refs[5:5 + nc], refs[12 + nc:12 + 2 * nc], refs[12 + 2 * nc:], pl.program_id(0), t // tm)
        hv = h_ref[...]
        part = lambda a: jnp.dot(hv, w_ref[:, cuts[a]:cuts[a + 1]], preferred_element_type=F32)
        cos, sin = _rope_tables(pos_ref[...], invf_ref[...], sgn_ref[...])
        cs_ref[...] = jnp.concatenate([cos, sin], axis=1)
        q_ref[...] = _bf(_rope_apply(part(0), cos, sin) * (HEAD_DIM ** -0.5))
        k_ref[...] = _bf(_rope_apply(part(1), cos, sin))
        v_ref[...] = _bf(part(2))
        z_ref[...] = part(3)
        xbc_ref[...] = part(4)
        dt_ref[...] = part(5)
        if carry is not None:
            finish()

    row = lambda c: pl.BlockSpec((tm, c), lambda i: (i, 0))
    hbm = pl.BlockSpec(memory_space=pltpu.HBM)
    widths = [D_ATTN, D_ATTN, D_ATTN, D_SSD, D_CONV, LANES, 2 * LANES]
    dtypes = [BF16, BF16, BF16, F32, F32, F32, F32]
    res = pl.pallas_call(
        body, name="proj_in", grid=(t // tm,),
        in_specs=[row(k), _resident(w_in.shape), row(1), _resident(invf.shape), _resident(sgn.shape)] + [hbm] * nc,
        out_specs=[row(c) for c in widths] + [hbm] * nc,
        out_shape=[jax.ShapeDtypeStruct((t, c), dt) for c, dt in zip(widths, dtypes)]
        + (carry.out_shape if carry is not None else []),
        scratch_shapes=carry.scratch_shapes if carry is not None else [],
        compiler_params=_params("arbitrary" if carry is not None else "parallel"),
    )(h16, w_in, posf, invf, sgn, *(carry.operands if carry is not None else []))
    return tuple(res[:7]) + ((carry.results(res[7:]),) if carry is not None else ())


def _rope_bwd(dq, dk, cs):
    def fn(dq_v, dk_v, cs_v):
        cos, sin = cs_v[:, :LANES], -cs_v[:, LANES:]
        gq = _rope_apply(dq_v * (HEAD_DIM ** -0.5), cos, sin)
        gk = _rope_apply(dk_v, cos, sin)
        return [jnp.concatenate([gq, gk], axis=1)], []
    return _rowwise("rope_bwd", fn, [dq, dk, cs], [], [(2 * D_ATTN, BF16)])[0]


def _rms(v, w):
    return v * lax.rsqrt(jnp.mean(v * v, axis=-1, keepdims=True) + RMS_EPS) * w


def _ungroup(yg):
    return jnp.concatenate([yg[:, HEAD_LANES * h:HEAD_LANES * h + HEAD_DIM] for h in range(N_HEADS)], axis=1)


def _group(xs):
    parts = []
    for h in range(N_HEADS):
        parts += [xs[:, HEAD_DIM * h:HEAD_DIM * (h + 1)], jnp.zeros((xs.shape[0], HEAD_LANES - HEAD_DIM), xs.dtype)]
    return jnp.concatenate(parts, axis=1)


def _norms_fn(attn, yg, xs, z, w_attn, w_ssd, dskip):
    a_n = _rms(attn, w_attn)
    y = _ungroup(yg) + dskip * xs
    y_n = _rms(y * (z * _sigmoid(z)), w_ssd)
    return jnp.concatenate([a_n, y_n], axis=1)


def _norms_fwd(attn, yg, xbc, z, w_attn, w_ssd, dskip):
    def fn(*v):
        return [_norms_fn(*v)], []
    return _rowwise("norms_fwd", fn, [attn, yg, (xbc, D_SSD), z], [w_attn, w_ssd, dskip], [(D_ATTN + D_SSD, BF16)])[0]


def _norms_bwd(attn, yg, xbc, z, w_attn, w_ssd, dskip, dcat):
    def fn(attn_v, yg_v, xs_v, z_v, dcat_v, wa_v, ws_v, dk_v):
        _, vjp = jax.vjp(_norms_fn, attn_v, yg_v, xs_v, z_v, wa_v, ws_v, dk_v)
        d_attn, d_yg, d_xs, d_z, d_wa, d_ws, d_dk = vjp(dcat_v)
        return [d_attn, d_yg, d_xs, d_z], [d_wa, d_ws, d_dk]
    return _rowwise("norms_bwd", fn, [attn, yg, (xbc, D_SSD), z, dcat], [w_attn, w_ssd, dskip],
                    [(D_ATTN, F32), (N_GROUPS * GROUP_LANES, F32), (D_SSD, F32), (D_SSD, BF16)], [(1, D_SSD)] * 3)


def _spread_sum(v, e):
    h1 = _bf(v)
    r1 = v - h1.astype(F32)
    h2 = _bf(r1)
    h3 = _bf(r1 - h2.astype(F32))
    return sum(jnp.dot(h, e, preferred_element_type=F32) for h in (h1, h2, h3))


@jax.custom_vjp
def _spread(v, e, e_t):
    return _spread_sum(v, e)


def _spread_fwd(v, e, e_t):
    return _spread_sum(v, e), (e, e_t)


def _spread_bwd(saved, g):
    e, e_t = saved
    return _spread_sum(g, e_t), jnp.zeros_like(e), jnp.zeros_like(e_t)


_spread.defvjp(_spread_fwd, _spread_bwd)


def _ssd_prep_fn(xs, dtp, dtb, alog, e_x, e_xt, e_a, e_at):
    dt = jax.nn.softplus(dtp + dtb)
    a = -jnp.exp(alog)
    xdtg = _group(xs) * _spread(dt, e_x, e_xt)
    dag = _spread(dt * a, e_a, e_at)
    return xdtg, dag


def _ssd_prep_fwd(xbc, dtp, dtb, alog, spreaders):
    def fn(xbc_v, dtp_v, dtb_v, alog_v, *e_v):
        xdtg, dag = _ssd_prep_fn(xbc_v[:, :D_SSD], dtp_v, dtb_v, alog_v, *e_v)
        return [xdtg, xbc_v[:, D_SSD:], dag], []
    return _rowwise("ssd_prep_fwd", fn, [xbc, dtp], [dtb, alog, *spreaders],
                    [(N_GROUPS * GROUP_LANES, BF16), (D_CONV - D_SSD, BF16), (N_GROUPS * LANES, F32)])


def _ssd_prep_bwd(xbc, dtp, dtb, alog, spreaders, dxdtg, ddag, dxs_a, db, dc):
    def fn(xs_v, dtp_v, dxdtg_v, ddag_v, dxs_a_v, db_v, dc_v, dtb_v, alog_v, *e_v):
        _, vjp = jax.vjp(lambda a, b, c, d: _ssd_prep_fn(a, b, c, d, *e_v), xs_v, dtp_v, dtb_v, alog_v)
        dxs, ddtp, ddtb, dalog = vjp((dxdtg_v, ddag_v))
        return [jnp.concatenate([dxs + dxs_a_v, db_v, dc_v], axis=1), ddtp], [ddtb, dalog]
    return _rowwise("ssd_prep_bwd", fn, [(xbc, D_SSD), dtp, dxdtg, ddag, dxs_a, db, dc], [dtb, alog, *spreaders],
                    [(D_CONV, F32), (LANES, BF16)], [(1, LANES)] * 2)


def _shift_down(u, d):
    if d == 0:
        return u
    row = lax.broadcasted_iota(jnp.int32, u.shape, 0)
    return jnp.where(row >= d, pltpu.roll(u, d, 0), 0.0)


def _shift_up(u, d):
    if d == 0:
        return u
    s = u.shape[0]
    row = lax.broadcasted_iota(jnp.int32, u.shape, 0)
    return jnp.where(row < s - d, pltpu.roll(u, s - d, 0), 0.0)


def _conv_pre(u, w, b):
    acc = b
    for k in range(CONV_WIDTH):
        acc = acc + w[k:k + 1, :] * _shift_down(u, CONV_WIDTH - 1 - k)
    return acc


def _conv_fwd(u, w, b, *, tc=256):
    nb, s, c = u.shape

    def body(u_ref, w_ref, b_ref, o_ref):
        pre = _conv_pre(u_ref[0], w_ref[...], b_ref[...])
        o_ref[0] = pre * _sigmoid(pre)

    return pl.pallas_call(
        body, name="conv_fwd", grid=(c // tc, nb),
        in_specs=[pl.BlockSpec((1, s, tc), lambda j, i: (i, 0, j)), pl.BlockSpec((CONV_WIDTH, tc), lambda j, i: (0, j)),
                  pl.BlockSpec((1, tc), lambda j, i: (0, j))],
        out_specs=pl.BlockSpec((1, s, tc), lambda j, i: (i, 0, j)),
        out_shape=jax.ShapeDtypeStruct((nb, s, c), F32),
        compiler_params=_params("parallel", "parallel"),
    )(u, w, b)


def _conv_bwd(u, w, b, dout, *, tc=256):
    nb, s, c = u.shape

    def body(u_ref, w_ref, b_ref, d_ref, du_ref, dw_ref, db_ref):
        uv, wv = u_ref[0], w_ref[...]
        pre = _conv_pre(uv, wv, b_ref[...])
        sig = _sigmoid(pre)
        dpre = d_ref[0] * (sig * (1.0 + pre * (1.0 - sig)))
        du = jnp.zeros_like(uv)
        dws = []
        for k in range(CONV_WIDTH):
            du = du + wv[k:k + 1, :] * _shift_up(dpre, CONV_WIDTH - 1 - k)
            dws.append(jnp.sum(dpre * _shift_down(uv, CONV_WIDTH - 1 - k), axis=0, keepdims=True))
        du_ref[0] = _bf(du)
        dwv = jnp.concatenate(dws + [jnp.zeros((8 - CONV_WIDTH, tc), F32)], axis=0)
        dbv = jnp.sum(dpre, axis=0, keepdims=True)
        first = pl.program_id(1) == 0

        @pl.when(first)
        def _():
            dw_ref[...] = dwv
            db_ref[...] = dbv

        @pl.when(jnp.logical_not(first))
        def _():
            dw_ref[...] += dwv
            db_ref[...] += dbv

    blk = pl.BlockSpec((1, s, tc), lambda j, i: (i, 0, j))
    return pl.pallas_call(
        body, name="conv_bwd", grid=(c // tc, nb),
        in_specs=[blk, pl.BlockSpec((CONV_WIDTH, tc), lambda j, i: (0, j)), pl.BlockSpec((1, tc), lambda j, i: (0, j)), blk],
        out_specs=[blk, pl.BlockSpec((8, tc), lambda j, i: (0, j)), pl.BlockSpec((1, tc), lambda j, i: (0, j))],
        out_shape=[jax.ShapeDtypeStruct((nb, s, c), BF16), jax.ShapeDtypeStruct((8, c), F32), jax.ShapeDtypeStruct((1, c), F32)],
        compiler_params=_params("parallel", "arbitrary"),
    )(u, w, b, dout)


FWD_KEY_BLOCK = 256


def _branch_bias_table(seq, kb):
    ratio = SEQ_BLOCK // kb
    key = np.arange(kb)[None, :, None]
    query = np.arange(SEQ_BLOCK)[None, None, :]
    delta = (np.arange(seq // kb)[:, None, None] - (ratio - 1)) * kb + query - key
    cnt = np.zeros(delta.shape, np.float64)
    for window, dilation in ((128, 1), (512, 4), (2048, 16)):
        cnt += (delta >= 0) & (delta % dilation == 0) & (delta <= window)
    return jnp.asarray(np.where(cnt > 0, np.log(np.maximum(cnt, 1.0)), NEG).astype(np.float32))


HEADS_PER_BLOCK = LANES // HEAD_DIM


def _head_rows(v, h):
    row = lax.broadcasted_iota(jnp.int32, v.shape, 0)
    return jnp.where((row >= HEAD_DIM * h) & (row < HEAD_DIM * (h + 1)), v, jnp.zeros_like(v))


def _attn_fwd(q, k, v, bias):
    nb_, s, _ = q.shape
    ab, kb = SEQ_BLOCK, FWD_KEY_BLOCK
    nblk, nkb, ratio = s // ab, s // kb, ab // kb

    def body(q_ref, k_ref, v_ref, b_ref, o_ref, lse_ref, vt_s):
        i = pl.program_id(2)

        @pl.when(i == 0)
        def _():
            for jb in range(nkb):
                vt_s[jb] = v_ref[0, kb * jb:kb * (jb + 1), :].T

        qt = q_ref[0].T
        qts = [_head_rows(qt, h) for h in range(HEADS_PER_BLOCK)]

        last = ratio * (i + 1) - 1

        def scores(j):
            kj = k_ref[0, pl.ds(pl.multiple_of(j * kb, kb), kb), :]
            return [jnp.dot(kj, qts[h], preferred_element_type=F32) for h in range(HEADS_PER_BLOCK)]

        def step(j, carry):
            ahead = scores(jnp.minimum(j + 1, last))
            lb = b_ref[ratio * i - j + (ratio - 1)]
            out = []
            for h in range(HEADS_PER_BLOCK):
                m, l, acc = carry[3 * h:3 * h + 3]
                st = carry[3 * HEADS_PER_BLOCK + h] + lb
                m_new = jnp.maximum(m, jnp.max(st, axis=0, keepdims=True))
                p = jnp.exp(st - m_new)
                a = jnp.exp(m - m_new)
                l = a * l + jnp.sum(p, axis=0, keepdims=True)
                vt = vt_s[j, HEAD_DIM * h:HEAD_DIM * (h + 1), :]
                acc = a * acc + jnp.dot(vt, _bf(p), preferred_element_type=F32)
                out += [m_new, l, acc]
            return tuple(out) + tuple(ahead)

        init = (jnp.full((1, ab), NEG, F32), jnp.zeros((1, ab), F32), jnp.zeros((HEAD_DIM, ab), F32)) * HEADS_PER_BLOCK
        res = lax.fori_loop(0, ratio * (i + 1), step, init + tuple(scores(0)))
        ot = jnp.concatenate([res[3 * h + 2] / res[3 * h + 1] for h in range(HEADS_PER_BLOCK)], axis=0)
        o_ref[0] = ot.T
        rows = [res[3 * h] + jnp.log(res[3 * h + 1]) for h in range(HEADS_PER_BLOCK)]
        lse_ref[0, 0, 0] = jnp.concatenate(rows + [jnp.zeros((8 - HEADS_PER_BLOCK, ab), F32)], axis=0)

    qblk = pl.BlockSpec((1, ab, LANES), lambda b, hp, i: (b, i, hp))
    full = pl.BlockSpec((1, s, LANES), lambda b, hp, i: (b, 0, hp))
    return pl.pallas_call(
        body, name="attn_fwd", grid=(nb_, D_ATTN // LANES, nblk),
        in_specs=[qblk, full, full, pl.BlockSpec((nkb, kb, ab), lambda b, hp, i: (0, 0, 0))],
        out_specs=[qblk, pl.BlockSpec((1, 1, 1, 8, ab), lambda b, hp, i: (b, hp, i, 0, 0))],
        out_shape=[jax.ShapeDtypeStruct((nb_, s, D_ATTN), F32),
                   jax.ShapeDtypeStruct((nb_, D_ATTN // LANES, nblk, 8, ab), F32)],
        scratch_shapes=[pltpu.VMEM((nkb, LANES, kb), BF16)],
        compiler_params=_params("parallel", "parallel", "arbitrary"),
    )(q, k, v, bias)


def _attn_bwd(q, k, v, o, do, lse, bias):
    nb_, s, _ = q.shape
    ab = SEQ_BLOCK
    nblk = s // ab

    nh = HEADS_PER_BLOCK

    def body(q_ref, k_ref, v_ref, o_ref, do_ref, lse_ref, b_ref, dq_ref, dk_ref, dv_ref,
             qt_s, dot_s, kt_s, dqt_s, do16_s, d_s, dk_acc, dv_acc):
        for jb in range(nblk):
            sl = slice(ab * jb, ab * (jb + 1))
            qt, kt = q_ref[0, sl, :].T, k_ref[0, sl, :].T
            do = do_ref[0, sl, :]
            dot = do.T
            prod = dot * o_ref[0, sl, :].T
            do16_s[sl, :] = _bf(do)
            for h in range(nh):
                qt_s[nh * jb + h] = _head_rows(qt, h)
                kt_s[nh * jb + h] = _head_rows(kt, h)
                dot_s[nh * jb + h] = _head_rows(_bf(dot), h)
            d_s[jb] = jnp.concatenate(
                [jnp.sum(prod[HEAD_DIM * h:HEAD_DIM * (h + 1)], axis=0, keepdims=True) for h in range(nh)]
                + [jnp.zeros((8 - nh, ab), F32)], axis=0)
            dqt_s[jb] = jnp.zeros((LANES, ab), F32)

        def outer(j, carry):
            ks = pl.ds(pl.multiple_of(j * ab, ab), ab)
            kj, vj = k_ref[0, ks, :], v_ref[0, ks, :]
            dk_acc[...] = jnp.zeros_like(dk_acc)
            dv_acc[...] = jnp.zeros_like(dv_acc)

            def inner(i, c2):
                qs = pl.ds(pl.multiple_of(i * ab, ab), ab)
                qi, doi = q_ref[0, qs, :], do16_s[qs, :]
                lb = b_ref[i - j]
                for h in range(nh):
                    st = jnp.dot(kj, qt_s[nh * i + h], preferred_element_type=F32) + lb
                    pt = jnp.exp(st - lse_ref[0, 0, i, h:h + 1, :])
                    dpt = jnp.dot(vj, dot_s[nh * i + h], preferred_element_type=F32)
                    dst16 = _bf(pt * (dpt - d_s[i, h:h + 1, :]))
                    dv_acc[h] += jnp.dot(_bf(pt), doi, preferred_element_type=F32)
                    dk_acc[h] += jnp.dot(dst16, qi, preferred_element_type=F32)
                    dqt_s[i] += jnp.dot(kt_s[nh * j + h], dst16, preferred_element_type=F32)
                return c2

            lax.fori_loop(j, nblk, inner, 0)
            lane = lax.broadcasted_iota(jnp.int32, (ab, LANES), 1)
            dk_ref[0, ks, :] = jnp.where(lane < HEAD_DIM, dk_acc[0], dk_acc[1])
            dv_ref[0, ks, :] = _bf(jnp.where(lane < HEAD_DIM, dv_acc[0], dv_acc[1]))
            return carry

        lax.fori_loop(0, nblk, outer, 0)
        for jb in range(nblk):
            dq_ref[0, ab * jb:ab * (jb + 1), :] = dqt_s[jb].T

    assert nh == 2
    full = pl.BlockSpec((1, s, LANES), lambda b, hp: (b, 0, hp))
    return pl.pallas_call(
        body, name="attn_bwd", grid=(nb_, D_ATTN // LANES),
        in_specs=[full] * 5 + [pl.BlockSpec((1, 1, nblk, 8, ab), lambda b, hp: (b, hp, 0, 0, 0)),
                               pl.BlockSpec((nblk, ab, ab), lambda b, hp: (0, 0, 0))],
        out_specs=[full, full, full],
        out_shape=[jax.ShapeDtypeStruct((nb_, s, D_ATTN), F32), jax.ShapeDtypeStruct((nb_, s, D_ATTN), F32),
                   jax.ShapeDtypeStruct((nb_, s, D_ATTN), BF16)],
        scratch_shapes=[pltpu.VMEM((nh * nblk, LANES, ab), BF16), pltpu.VMEM((nh * nblk, LANES, ab), BF16),
                        pltpu.VMEM((nh * nblk, LANES, ab), BF16), pltpu.VMEM((nblk, LANES, ab), F32),
                        pltpu.VMEM((s, LANES), BF16), pltpu.VMEM((nblk, 8, ab), F32),
                        pltpu.VMEM((nh, ab, LANES), F32), pltpu.VMEM((nh, ab, LANES), F32)],
        compiler_params=_params("parallel", "parallel"),
    )(q, k, v, o, do, lse, bias)


def _cumsum_fwd(dag):
    nb_, s, c = dag.shape
    ab = SEQ_BLOCK

    def body(a_ref, o_ref, ot_ref):
        r = lax.broadcasted_iota(jnp.int32, (ab, ab), 0)
        cc = lax.broadcasted_iota(jnp.int32, (ab, ab), 1)
        tri = (r >= cc).astype(F32)
        carry = jnp.zeros((1, c), F32)
        for i in range(s // ab):
            loc = jnp.dot(tri, a_ref[0, ab * i:ab * (i + 1), :], precision=HIGHEST, preferred_element_type=F32) + carry
            o_ref[0, ab * i:ab * (i + 1), :] = loc
            ot_ref[0, :, ab * i:ab * (i + 1)] = loc.T
            carry = loc[ab - 1:ab, :]

    return pl.pallas_call(
        body, name="ssd_cumsum", grid=(nb_,),
        in_specs=[pl.BlockSpec((1, s, c), lambda b: (b, 0, 0))],
        out_specs=[pl.BlockSpec((1, s, c), lambda b: (b, 0, 0)), pl.BlockSpec((1, c, s), lambda b: (b, 0, 0))],
        out_shape=[jax.ShapeDtypeStruct((nb_, s, c), F32), jax.ShapeDtypeStruct((nb_, c, s), F32)],
        compiler_params=_params("parallel"),
    )(dag)


def _cumsum_bwd(dcol, drow):
    nb_, s, c = dcol.shape
    ab = SEQ_BLOCK

    def body(c_ref, r_ref, o_ref):
        r = lax.broadcasted_iota(jnp.int32, (ab, ab), 0)
        cc = lax.broadcasted_iota(jnp.int32, (ab, ab), 1)
        tri = (r <= cc).astype(F32)
        carry = jnp.zeros((1, c), F32)
        for i in reversed(range(s // ab)):
            rows = r_ref[0, :, ab * i:ab * (i + 1)].T
            parts = []
            for g in range(N_GROUPS):
                parts += [rows[:, 8 * g:8 * (g + 1)], jnp.zeros((ab, LANES - 8), F32)]
            blk = c_ref[0, ab * i:ab * (i + 1), :] + jnp.concatenate(parts, axis=1)
            loc = jnp.dot(tri, blk, precision=HIGHEST, preferred_element_type=F32) + carry
            o_ref[0, ab * i:ab * (i + 1), :] = loc
            carry = loc[0:1, :]

    return pl.pallas_call(
        body, name="ssd_cumsum_bwd", grid=(nb_,),
        in_specs=[pl.BlockSpec((1, s, c), lambda b: (b, 0, 0)), pl.BlockSpec((1, N_GROUPS * 8, s), lambda b: (b, 0, 0))],
        out_specs=pl.BlockSpec((1, s, c), lambda b: (b, 0, 0)),
        out_shape=jax.ShapeDtypeStruct((nb_, s, c), F32),
        compiler_params=_params("parallel"),
    )(dcol, drow)


def _causal_ok(i, j):
    ab = SEQ_BLOCK
    r = lax.broadcasted_iota(jnp.int32, (ab, ab), 0)
    c = lax.broadcasted_iota(jnp.int32, (ab, ab), 1)
    return (r + (i - j) * ab) >= c


def _causal_ok_t(i, j):
    ab = SEQ_BLOCK
    r = lax.broadcasted_iota(jnp.int32, (ab, ab), 0)
    c = lax.broadcasted_iota(jnp.int32, (ab, ab), 1)
    return (c + (i - j) * ab) >= r


def _ssd_chunk(s_in, x, bm_t, cm, cb, acol, arow, a_prev, ok):
    q = x.shape[0]
    decay = jnp.exp(jnp.where(ok, acol - arow, NEG))
    y = jnp.dot(_bf(cb * decay), x, preferred_element_type=F32)
    y = y + jnp.exp(acol - a_prev) * jnp.dot(cm, _bf(s_in), preferred_element_type=F32)
    a_end = acol[q - 1:q, :]
    wx = _bf(jnp.exp(a_end - acol) * x.astype(F32))
    s_out = jnp.exp(a_end - a_prev) * s_in + jnp.dot(bm_t, wx, preferred_element_type=F32)
    return y, s_out


def _ssd_specs(s):
    xblk = pl.BlockSpec((1, s, GROUP_LANES), lambda b, g: (b, 0, g))
    bblk = pl.BlockSpec((1, s, D_STATE), lambda b, g: (b, 0, g))
    cblk = pl.BlockSpec((1, s, D_STATE), lambda b, g: (b, 0, N_GROUPS + g))
    tblk = pl.BlockSpec((1, 8, s), lambda b, g: (b, (LANES // 8) * g, 0))
    return xblk, bblk, cblk, tblk


def _chunk_views(i, j, x_ref, ac_ref, at_ref):
    ab = SEQ_BLOCK
    sl = slice(ab * i, ab * (i + 1))
    hs = slice(HEAD_LANES * j, HEAD_LANES * (j + 1))
    a_prev = jnp.zeros((1, 1), F32) if i == 0 else ac_ref[0, ab * i - 1:ab * i, j:j + 1]
    return sl, hs, ac_ref[0, sl, j:j + 1], at_ref[0, j:j + 1, sl], a_prev


def _ssd_fwd_chunked(xdtg, bc, acum, acum_t):
    nb_, s, _ = xdtg.shape
    ab = SEQ_BLOCK
    hpg = HEADS_PER_GROUP

    def body(x_ref, b_ref, c_ref, ac_ref, at_ref, y_ref):
        ok = _causal_ok(0, 0)
        states = [jnp.zeros((D_STATE, HEAD_LANES), F32) for _ in range(hpg)]
        for i in range(s // ab):
            bm, cm = b_ref[0, ab * i:ab * (i + 1), :], c_ref[0, ab * i:ab * (i + 1), :]
            bm_t = bm.T
            cb = jnp.dot(cm, bm_t, preferred_element_type=F32)
            ys = []
            for j in range(hpg):
                sl, hs, acol, arow, a_prev = _chunk_views(i, j, x_ref, ac_ref, at_ref)
                y, states[j] = _ssd_chunk(states[j], x_ref[0, sl, hs], bm_t, cm, cb, acol, arow, a_prev, ok)
                ys.append(y)
            y_ref[0, sl, :] = jnp.concatenate(ys, axis=1)

    xblk, bblk, cblk, tblk = _ssd_specs(s)
    ablk = pl.BlockSpec((1, s, LANES), lambda b, g: (b, 0, g))
    return pl.pallas_call(
        body, name="ssd_fwd", grid=(nb_, N_GROUPS), in_specs=[xblk, bblk, cblk, ablk, tblk], out_specs=xblk,
        out_shape=jax.ShapeDtypeStruct((nb_, s, N_GROUPS * GROUP_LANES), F32),
        compiler_params=_params("parallel", "parallel"),
    )(xdtg, bc, bc, acum, acum_t)


def _ssd_bwd_chunked(xdtg, bc, acum, acum_t, dyg):
    nb_, s, _ = xdtg.shape
    ab = SEQ_BLOCK
    nblk = s // ab
    hpg = HEADS_PER_GROUP

    def body(x_ref, b_ref, c_ref, ac_ref, at_ref, dy_ref, dx_ref, db_ref, dc_ref, dac_ref, dar_ref, s_s):
        dac_ref[...] = jnp.zeros_like(dac_ref)
        dar_ref[...] = jnp.zeros_like(dar_ref)
        states = [jnp.zeros((D_STATE, HEAD_LANES), F32) for _ in range(hpg)]
        for i in range(nblk):
            bm_t = b_ref[0, ab * i:ab * (i + 1), :].T
            for j in range(hpg):
                sl, hs, acol, arow, a_prev = _chunk_views(i, j, x_ref, ac_ref, at_ref)
                s_s[hpg * i + j] = states[j]
                if i + 1 < nblk:
                    a_end = acol[ab - 1:ab, :]
                    wx = _bf(jnp.exp(a_end - acol) * x_ref[0, sl, hs].astype(F32))
                    states[j] = jnp.exp(a_end - a_prev) * states[j] + jnp.dot(bm_t, wx, preferred_element_type=F32)
        ok_t = _causal_ok_t(0, 0)
        last_row = lax.broadcasted_iota(jnp.int32, (ab, 1), 0) == ab - 1
        d_state = [jnp.zeros((D_STATE, HEAD_LANES), F32) for _ in range(hpg)]
        pending = [jnp.zeros((1, 1), F32) for _ in range(hpg)]
        total = lambda v: jnp.sum(v, keepdims=True)
        for i in reversed(range(nblk)):
            bm, cm = b_ref[0, ab * i:ab * (i + 1), :], c_ref[0, ab * i:ab * (i + 1), :]
            cm_t = cm.T
            cbt = jnp.dot(bm, cm_t, preferred_element_type=F32)
            dcbt = jnp.zeros((ab, ab), F32)
            d_bm, d_cm = jnp.zeros((ab, D_STATE), F32), jnp.zeros((ab, D_STATE), F32)
            for j in range(hpg):
                sl, hs, acol, arow, a_prev = _chunk_views(i, j, x_ref, ac_ref, at_ref)
                x, dy = x_ref[0, sl, hs], dy_ref[0, sl, hs]
                dy16 = _bf(dy)
                s_in, g_out = s_s[hpg * i + j], d_state[j]
                s16, g16 = _bf(s_in), _bf(g_out)
                decay = jnp.exp(jnp.where(ok_t, arow - acol, NEG))
                gt = cbt * decay
                dgt = lax.dot_general(x, dy16, _NT, preferred_element_type=F32)
                d_x = jnp.dot(_bf(gt), dy16, preferred_element_type=F32)
                dcbt = dcbt + dgt * decay
                mm = dgt * gt
                d_arow = jnp.sum(mm, axis=0, keepdims=True)
                d_acol = -jnp.sum(mm, axis=1, keepdims=True)
                e = jnp.exp(acol - a_prev)
                edy16 = _bf(e * dy)
                d_cm = d_cm + lax.dot_general(edy16, s16, _NT, preferred_element_type=F32)
                d_s = jnp.dot(cm_t, edy16, preferred_element_type=F32)
                de_e = jnp.sum(dy * jnp.dot(cm, s16, preferred_element_type=F32), axis=1, keepdims=True) * e
                a_end = acol[ab - 1:ab, :]
                w = jnp.exp(a_end - acol)
                f = jnp.exp(a_end - a_prev)
                x32 = x.astype(F32)
                bg = jnp.dot(bm, g16, preferred_element_type=F32)
                d_x = d_x + w * bg
                d_bm = d_bm + lax.dot_general(_bf(w * x32), g16, _NT, preferred_element_type=F32)
                dw_w = jnp.sum(bg * x32, axis=1, keepdims=True) * w
                df_f = total(g_out * s_in) * f
                d_end = total(dw_w) + df_f
                d_acol = d_acol + de_e - dw_w + jnp.where(last_row, d_end + pending[j], 0.0)
                pending[j] = -total(de_e) - df_f
                d_state[j] = d_s + f * g_out
                dx_ref[0, sl, hs] = d_x
                dac_ref[0, sl, j:j + 1] = d_acol
                dar_ref[0, j:j + 1, sl] = d_arow
            dcbt16 = _bf(dcbt)
            db_ref[0, ab * i:ab * (i + 1), :] = d_bm + jnp.dot(dcbt16, cm, preferred_element_type=F32)
            dc_ref[0, ab * i:ab * (i + 1), :] = d_cm + lax.dot_general(dcbt16, bm, _TN, preferred_element_type=F32)

    xblk, bblk, cblk, tblk = _ssd_specs(s)
    ablk = pl.BlockSpec((1, s, LANES), lambda b, g: (b, 0, g))
    return pl.pallas_call(
        body, name="ssd_bwd", grid=(nb_, N_GROUPS),
        in_specs=[xblk, bblk, cblk, ablk, tblk, xblk],
        out_specs=[xblk, bblk, bblk, ablk, pl.BlockSpec((1, 8, s), lambda b, g: (b, g, 0))],
        out_shape=[jax.ShapeDtypeStruct((nb_, s, N_GROUPS * GROUP_LANES), F32),
                   jax.ShapeDtypeStruct((nb_, s, N_GROUPS * D_STATE), F32),
                   jax.ShapeDtypeStruct((nb_, s, N_GROUPS * D_STATE), F32),
                   jax.ShapeDtypeStruct((nb_, s, N_GROUPS * LANES), F32),
                   jax.ShapeDtypeStruct((nb_, N_GROUPS * 8, s), F32)],
        scratch_shapes=[pltpu.VMEM((nblk * hpg, D_STATE, HEAD_LANES), F32)],
        compiler_params=_params("parallel", "parallel"),
    )(xdtg, bc, bc, acum, acum_t, dyg)


def _interleave(wg, wu):
    k, f = wg.shape
    gi = GATE_UP_INTERLEAVE
    return jnp.stack([wg.reshape(k, f // gi, gi), wu.reshape(k, f // gi, gi)], axis=2).reshape(k, 2 * f)


def _head_expanders():
    e_x = np.zeros((LANES, N_GROUPS * GROUP_LANES), np.float32)
    e_a = np.zeros((LANES, N_GROUPS * LANES), np.float32)
    for h in range(N_HEADS):
        g, j = divmod(h, HEADS_PER_GROUP)
        e_x[h, HEAD_LANES * h:HEAD_LANES * h + HEAD_DIM] = 1.0
        e_a[h, LANES * g + j] = 1.0
    return [jnp.asarray(m, BF16) for m in (e_x, e_x.T, e_a, e_a.T)]


def _pad_lanes(v, n=LANES):
    return jnp.pad(v, ((0, 0), (0, n - v.shape[1])))


def _local_step(x, positions, target, w, late=None, early_grad_job=None):
    nb, s, d = x.shape
    t = nb * s
    x2 = x.reshape(t, d)
    tgt2 = target.reshape(t, d)
    (job_a, weights_a), (job_b, weights_b) = late if late is not None else ((None, None), (None, None))

    x16 = _bf(x2)
    wgu1 = _interleave(w["ffn1_gate"], w["ffn1_up"])
    ffn1 = _ffn_fwd("ffn1_fwd", x16, x2, wgu1, w["ffn1_down"], w["ln1_g"], w["ln1_b"], carry=job_a)
    au1, hm1, h1, r1, h1_16 = ffn1[:5]
    if job_a is not None:
        w = {**w, **weights_a(ffn1[5])}

    w_in = w["w_in"]
    wqk, wv, wz = w_in[:, :2 * D_ATTN], w_in[:, 2 * D_ATTN:3 * D_ATTN], w_in[:, 3 * D_ATTN:3 * D_ATTN + D_SSD]
    wxbc = w_in[:, 3 * D_ATTN + D_SSD:3 * D_ATTN + D_SSD + D_CONV]
    wdt = _pad_lanes(w_in[:, 3 * D_ATTN + D_SSD + D_CONV:])

    inv_freq = ROPE_THETA ** (-jnp.arange(0, ROPE_DIM, 2, dtype=F32) / ROPE_DIM)
    half = ROPE_DIM // 2
    head_invf = jnp.concatenate([inv_freq, inv_freq, jnp.zeros((HEAD_DIM - ROPE_DIM,), F32)])
    head_sgn = jnp.concatenate([-jnp.ones((half,), F32), jnp.ones((half,), F32), jnp.zeros((HEAD_DIM - ROPE_DIM,), F32)])
    invf = jnp.tile(head_invf, LANES // HEAD_DIM)[None, :]
    sgn = jnp.tile(head_sgn, LANES // HEAD_DIM)[None, :]
    posf = positions.astype(F32).reshape(t, 1)
    bias_fwd, bias_bwd = _branch_bias_table(s, FWD_KEY_BLOCK), _branch_bias_table(s, SEQ_BLOCK)
    spreaders = _head_expanders()
    dtb, alog = _pad_lanes(w["dt_bias"]), _pad_lanes(w["a_log"])
    dskip = jnp.repeat(w["d_skip"], HEAD_DIM, axis=1)

    proj = _proj_in(h1_16, _pad_lanes(w_in, w_in.shape[1] - N_HEADS + LANES), posf, invf, sgn, carry=job_b)
    q16, k16, v16, z, xbc_pre, dtp, cs = proj[:7]
    if job_b is not None:
        w = {**w, **weights_b(proj[7])}
    wgu2 = _interleave(w["ffn2_gate"], w["ffn2_up"])
    to3 =lambda a: a.reshape(nb, s, a.shape[-1])
    attn_o, lse = _attn_fwd(to3(q16), to3(k16), to3(v16), bias_fwd)

    xbc = _conv_fwd(to3(xbc_pre), w["conv_w"], w["conv_b"]).reshape(t, D_CONV)
    xdtg, bc16, dag = _ssd_prep_fwd(xbc, dtp, dtb, alog, spreaders)
    acum, acum_t = _cumsum_fwd(to3(dag))
    yg = _ssd_fwd_chunked(to3(xdtg), to3(bc16), acum, acum_t)

    cat = _norms_fwd(attn_o.reshape(t, D_ATTN), yg.reshape(t, -1), xbc, z, w["attn_norm_w"], w["ssd_norm_w"], dskip)
    h2, r2, h2_16 = _mm_res_ln("w_out_ln2", cat, w["w_out"], h1, w["ln2_g"], w["ln2_b"], scale=1.0)

    au2, hm2, _, r3, _ = _ffn_fwd("ffn2_fwd", h2_16, h2, wgu2, w["ffn2_down"], w["ln3_g"], w["ln3_b"])

    g = {}
    dr3, dr3_16, g["ln3_g"], g["ln3_b"], loss = _ln_loss_bwd("loss_ln3_bwd", r3, w["ln3_g"], w["ln3_b"], tgt2)

    dau2, dh2 = _ffn_bwd("ffn2_bwd", dr3_16, dr3, w["ffn2_down"].T * 0.5, au2, wgu2.T)
    g["ffn2_down"] = _mm_tn("ffn2_down_dw", hm2, dr3_16, scale=0.5, tk=D_FF // 2, tn=512)
    g["ffn2_gate"], g["ffn2_up"] = _mm_tn_gate_up("ffn2_up_dw", h2_16, dau2)

    dr2, dr2_16, g["ln2_g"], g["ln2_b"] = _ln_bwd("ln2_bwd", r2, w["ln2_g"], w["ln2_b"], dh2)
    dcat = _mm("w_out_dx", [(dr2_16, w["w_out"].T)], tm=1024, tn=768)
    g["w_out"] = _mm_tn("w_out_dw", cat, dr2_16, tk=768, tn=1024)

    d_attn, dyg, dxs_a, dz16, g["attn_norm_w"], g["ssd_norm_w"], ddskip = _norms_bwd(
        attn_o.reshape(t, D_ATTN), yg.reshape(t, -1), xbc, z, w["attn_norm_w"], w["ssd_norm_w"], dskip, dcat)
    g["d_skip"] = ddskip.reshape(N_HEADS, HEAD_DIM).sum(axis=1)[None, :]

    dq, dk, dv16 = _attn_bwd(to3(q16), to3(k16), to3(v16), attn_o, to3(d_attn), lse, bias_bwd)
    dqk16 = _rope_bwd(dq.reshape(t, D_ATTN), dk.reshape(t, D_ATTN), cs)

    dxdtg, dbm, dcm, dacol, darow = _ssd_bwd_chunked(to3(xdtg), to3(bc16), acum, acum_t, to3(dyg))
    ddag = _cumsum_bwd(dacol, darow)
    dxbc, ddtp16, ddtb, dalog = _ssd_prep_bwd(xbc, dtp, dtb, alog, spreaders, dxdtg.reshape(t, -1), ddag.reshape(t, -1),
                                               dxs_a, dbm.reshape(t, -1), dcm.reshape(t, -1))
    g["dt_bias"], g["a_log"] = ddtb[:, :N_HEADS], dalog[:, :N_HEADS]
    dxbc_pre16, dconv_w, g["conv_b"] = _conv_bwd(to3(xbc_pre), w["conv_w"], w["conv_b"], to3(dxbc))
    g["conv_w"] = dconv_w[:CONV_WIDTH]
    dxbc_pre16 = dxbc_pre16.reshape(t, D_CONV)
    dv16 = dv16.reshape(t, D_ATTN)

    dh1 = _mm("w_in_dx", [(dqk16, wqk.T), (dv16, wv.T), (dz16, wz.T), (dxbc_pre16, wxbc.T), (ddtp16, wdt.T)],
              res=dr2, res_scale=ALPHA, tm=1024)
    g["w_in"] = _mm_tn_sections("w_in_dw", h1_16, [dqk16, dv16, dz16, dxbc_pre16, ddtp16])[:, :w_in.shape[1]]

    dr1, dr1_16, g["ln1_g"], g["ln1_b"] = _ln_bwd("ln1_bwd", r1, w["ln1_g"], w["ln1_b"], dh1)
    g["ffn1_down"] = _mm_tn("ffn1_down_dw", hm1, dr1_16, scale=0.5, tk=D_FF // 2, tn=512)
    ffn1b = _ffn_bwd("ffn1_bwd", dr1_16, dr1, w["ffn1_down"].T * 0.5, au1, wgu1.T,
                     carry=None if early_grad_job is None else early_grad_job(g))
    dau1, dx = ffn1b[:2]
    early = ffn1b[2] if early_grad_job is not None else None
    g["ffn1_gate"], g["ffn1_up"] = _mm_tn_gate_up("ffn1_up_dw", x16, dau1)
    return loss, dx.reshape(nb, s, d), g, early


_HBM = pl.BlockSpec(memory_space=pltpu.HBM)
N_CHIPS = 4
N_DEVICES = 8


def _place():
    return lax.axis_index("x"), lax.axis_index("y"), lax.axis_index("c")


def _other_chips(x, y):
    return [(1 - x, y), (x, 1 - y), (1 - x, 1 - y)]


class _GatherJob:
    def __init__(self, shards):
        assert all((a.shape[0] // 2) % 16 == 0 for a in shards)
        self.n = len(shards)
        self.shapes = [a.shape for a in shards]
        self.operands = [a.reshape(2, a.shape[0] // 2, a.shape[1]) for a in shards]
        self.out_shape = [jax.ShapeDtypeStruct((N_CHIPS,) + a.shape, a.dtype) for a in self.operands]
        pair = pltpu.SemaphoreType.DMA((self.n, N_CHIPS - 1))
        one = pltpu.SemaphoreType.DMA((self.n,))
        self.scratch_shapes = [pair, pair, pair, pair, one, one]

    def results(self, outs):
        return [o.reshape((N_CHIPS,) + s) for o, s in zip(outs, self.shapes)]

    def phases(self, ins, outs, sems):
        n = self.n
        send_sems, recv_sems, fwd_send_sems, fwd_recv_sems, own_send_sems, own_recv_sems = sems
        x, y, c = _place()
        me = 2 * x + y
        peers = _other_chips(x, y)

        def own(t):
            return pltpu.make_async_remote_copy(ins[t], outs[t].at[me], own_send_sems.at[t], own_recv_sems.at[t],
                                                device_id=(x, y, 1 - c), device_id_type=MESH)

        def ici(t, p, src_chip):
            px, py = peers[p]
            return pltpu.make_async_remote_copy(
                ins[t].at[c] if src_chip is None else outs[t].at[src_chip, c],
                outs[t].at[me if src_chip is None else src_chip, c],
                send_sems.at[t, p], recv_sems.at[t, p], device_id=(px, py, c), device_id_type=MESH)

        def d2d(t, p, core):
            px, py = peers[p]
            return pltpu.make_async_remote_copy(
                outs[t].at[2 * px + py, core], outs[t].at[2 * px + py, core],
                fwd_send_sems.at[t, p], fwd_recv_sems.at[t, p], device_id=(x, y, 1 - c), device_id_type=MESH)

        pairs = [(t, p) for t in range(n) for p in range(N_CHIPS - 1)]

        def start():
            for t, p in pairs:
                ici(t, p, None).start()
            for t in range(n):
                own(t).start()

        def forward():
            for t, p in pairs:
                px, py = peers[p]
                ici(t, p, 2 * px + py).wait_recv()
                d2d(t, p, c).start()

        def finish():
            for t, p in pairs:
                d2d(t, p, 1 - c).wait_recv()
            for t in range(n):
                own(t).wait()
            for t, p in pairs:
                ici(t, p, None).wait_send()
                d2d(t, p, c).wait_send()

        return start, forward, finish


class _ExchangeJob:
    def __init__(self, stacks):
        self.n = len(stacks)
        self.operands = list(stacks)
        self.out_shape = [jax.ShapeDtypeStruct(a.shape, a.dtype) for a in stacks]
        pair = pltpu.SemaphoreType.DMA((self.n, N_CHIPS - 1))
        self.scratch_shapes = [pair, pair]

    def results(self, outs):
        return list(outs)

    def phases(self, ins, outs, sems):
        send_sems, recv_sems = sems
        x, y, c = _place()
        me = 2 * x + y
        peers = _other_chips(x, y)
        pairs = [(t, p) for t in range(self.n) for p in range(N_CHIPS - 1)]

        def copy(t, p):
            px, py = peers[p]
            return pltpu.make_async_remote_copy(ins[t].at[2 * px + py], outs[t].at[me], send_sems.at[t, p],
                                                recv_sems.at[t, p], device_id=(px, py, c), device_id_type=MESH)

        def arrival(t, p):
            px, py = peers[p]
            return pltpu.make_async_remote_copy(ins[t].at[me], outs[t].at[2 * px + py], send_sems.at[t, p],
                                                recv_sems.at[t, p], device_id=(px, py, c), device_id_type=MESH)

        def start():
            for t, p in pairs:
                copy(t, p).start()

        def finish():
            for t, p in pairs:
                arrival(t, p).wait_recv()
            for t, p in pairs:
                copy(t, p).wait_send()

        return start, None, finish


def _run_job(job, name):
    n = job.n

    def body(*refs):
        for phase in job.phases(refs[:n], refs[n:2 * n], refs[2 * n:]):
            if phase is not None:
                phase()

    outs = pl.pallas_call(
        body, name=name, in_specs=[_HBM] * n, out_specs=[_HBM] * n,
        out_shape=job.out_shape, scratch_shapes=job.scratch_shapes,
    )(*job.operands)
    return job.results(outs)


def _sibling_halves(stacks, name):
    n = len(stacks)
    halves = [a.shape[1] // 2 for a in stacks]
    split = [a.reshape(a.shape[0], 2, h, a.shape[2]) for a, h in zip(stacks, halves)]

    def body(*refs):
        ins, outs = refs[:n], refs[n:2 * n]
        send_sems, recv_sems = refs[2 * n:]
        x, y, c = _place()
        cps = []
        for t in range(n):
            cp = pltpu.make_async_remote_copy(ins[t].at[:, 1 - c], outs[t], send_sems.at[t], recv_sems.at[t],
                                              device_id=(x, y, 1 - c), device_id_type=MESH)
            cp.start()
            cps.append(cp)
        for cp in cps:
            cp.wait()

    return pl.pallas_call(
        body, name=name,
        in_specs=[_HBM] * n, out_specs=[_HBM] * n,
        out_shape=[jax.ShapeDtypeStruct((a.shape[0], h, a.shape[2]), a.dtype) for a, h in zip(stacks, halves)],
        scratch_shapes=[pltpu.SemaphoreType.DMA((n,)), pltpu.SemaphoreType.DMA((n,))],
    )(*split)


def _sibling_swap(arrs):
    n = len(arrs)

    def body(*refs):
        ins, outs = refs[:n], refs[n:2 * n]
        send_sems, recv_sems = refs[2 * n:]
        x, y, c = _place()
        cps = []
        for t in range(n):
            cp = pltpu.make_async_remote_copy(ins[t], outs[t], send_sems.at[t], recv_sems.at[t],
                                              device_id=(x, y, 1 - c), device_id_type=MESH)
            cp.start()
            cps.append(cp)
        for cp in cps:
            cp.wait()

    return pl.pallas_call(
        body, name="sibling_swap",
        in_specs=[_HBM] * n, out_specs=[_HBM] * n,
        out_shape=[jax.ShapeDtypeStruct(a.shape, a.dtype) for a in arrs],
        scratch_shapes=[pltpu.SemaphoreType.DMA((n,)), pltpu.SemaphoreType.DMA((n,))],
    )(*arrs)


def _half_sum(name, own, other, core):
    k, r, cols = own.shape
    h = r // 2
    tr = next(cand for cand in (128, 176, 64, 32, 16) if h % cand == 0)
    nblk = h // tr

    def body(core_ref, own_ref, other_ref, o_ref):
        o_ref[...] = _bf(own_ref[...] + other_ref[...].astype(F32))

    grid_spec = pltpu.PrefetchScalarGridSpec(
        num_scalar_prefetch=1, grid=(nblk,),
        in_specs=[pl.BlockSpec((k, tr, cols), lambda i, core_ref: (0, i + core_ref[0] * nblk, 0)),
                  pl.BlockSpec((k, tr, cols), lambda i, core_ref: (0, i, 0))],
        out_specs=pl.BlockSpec((k, tr, cols), lambda i, core_ref: (0, i, 0)))
    return pl.pallas_call(
        body, name=name, grid_spec=grid_spec, out_shape=jax.ShapeDtypeStruct((k, h, cols), BF16),
        compiler_params=_params("parallel"),
    )(core.reshape(1).astype(jnp.int32), own, other)


def _small_allreduce(v):
    r = v.shape[0]

    def body(v_ref, tot_ref, slots, send_sems, recv_sems):
        x, y, c = _place()
        me = 4 * x + 2 * y + c
        slots[me] = v_ref[...]
        cps, peers = [], []
        for k in range(1, N_DEVICES):
            px = 1 - x if (k >> 2) & 1 else x
            py = 1 - y if (k >> 1) & 1 else y
            pc = 1 - c if k & 1 else c
            cp = pltpu.make_async_remote_copy(v_ref, slots.at[me], send_sems.at[k - 1], recv_sems.at[k - 1],
                                              device_id=(px, py, pc), device_id_type=MESH)
            cp.start()
            cps.append(cp)
            peers.append((px, py, pc))
        for k, (px, py, pc) in enumerate(peers):
            pltpu.make_async_remote_copy(v_ref, slots.at[4 * px + 2 * py + pc], send_sems.at[k], recv_sems.at[k],
                                         device_id=(px, py, pc), device_id_type=MESH).wait_recv()
        for cp in cps:
            cp.wait_send()
        acc = slots[0]
        for s in range(1, N_DEVICES):
            acc = acc + slots[s]
        tot_ref[...] = acc

    return pl.pallas_call(
        body, name="small_allreduce",
        in_specs=[pl.BlockSpec(memory_space=pltpu.VMEM)], out_specs=pl.BlockSpec(memory_space=pltpu.VMEM),
        out_shape=jax.ShapeDtypeStruct((r, LANES), F32),
        scratch_shapes=[pltpu.VMEM((N_DEVICES, r, LANES), F32), pltpu.SemaphoreType.DMA((N_DEVICES - 1,)),
                        pltpu.SemaphoreType.DMA((N_DEVICES - 1,))],
    )(v)


def _elementwise(name, fn, ins, out_dtypes):
    r, c = ins[0].shape[-2:]
    tr = next((cand for cand in (256, 176, 128, 64, 32, 16) if r % cand == 0), r)
    nin = len(ins)

    def body(*refs):
        outs = fn(*[ref[...] for ref in refs[:nin]])
        for o_ref, o in zip(refs[nin:], outs):
            o_ref[...] = o.astype(o_ref.dtype)

    in_specs = [pl.BlockSpec((tr, c), lambda i: (i, 0)) if a.ndim == 2 else pl.BlockSpec((a.shape[0], tr, c), lambda i: (0, i, 0))
                for a in ins]
    return pl.pallas_call(
        body, name=name, grid=(r // tr,), in_specs=in_specs,
        out_specs=[pl.BlockSpec((tr, c), lambda i: (i, 0)) for _ in out_dtypes],
        out_shape=[jax.ShapeDtypeStruct((r, c), dt) for dt in out_dtypes],
        compiler_params=_params("parallel"),
    )(*ins)


def _row_tile(rows):
    return next((cand for cand in (128, 176, 64, 32, 16) if rows % cand == 0), rows)


def _sum_slots(name, received, own, chip):
    _, r, cols = own.shape
    tr = _row_tile(r)

    def body(chip_ref, own_ref, a_ref, b_ref, c_ref, o_ref):
        o_ref[...] = ((own_ref[0].astype(F32) + a_ref[0].astype(F32)) + b_ref[0].astype(F32)) + c_ref[0].astype(F32)

    def slot(flip):
        return pl.BlockSpec((1, tr, cols), lambda i, chip_ref: (jnp.bitwise_xor(chip_ref[0], flip), i, 0))

    grid_spec = pltpu.PrefetchScalarGridSpec(
        num_scalar_prefetch=1, grid=(r // tr,), in_specs=[slot(0), slot(1), slot(2), slot(3)],
        out_specs=pl.BlockSpec((tr, cols), lambda i, chip_ref: (i, 0)))
    return pl.pallas_call(
        body, name=name, grid_spec=grid_spec, out_shape=jax.ShapeDtypeStruct((r, cols), F32),
        compiler_params=_params("parallel"),
    )(chip.reshape(1).astype(jnp.int32), own, received, received, received)


def _adamw_halves(name, mine, theirs, core, w, m, v):
    h, cols = mine.shape
    tr = _row_tile(h)
    nh = h // tr

    def body(core_ref, mine_ref, theirs_ref, w_ref, m_ref, v_ref, g_ref, d_ref, m2_ref, v2_ref):
        is_mine = (pl.program_id(0) // nh) == core_ref[0]
        g = jnp.where(is_mine, mine_ref[...], theirs_ref[...])
        outs = _adamw_math(g, w_ref[...], m_ref[...], v_ref[...])
        for ref, val in zip((g_ref, d_ref, m2_ref, v2_ref), outs):
            ref[...] = val

    half = pl.BlockSpec((tr, cols), lambda i, core_ref: (i % nh, 0))
    full = pl.BlockSpec((tr, cols), lambda i, core_ref: (i, 0))
    grid_spec = pltpu.PrefetchScalarGridSpec(
        num_scalar_prefetch=1, grid=(2 * nh,), in_specs=[half, half, full, full, full], out_specs=[full] * 4)
    return pl.pallas_call(
        body, name=name, grid_spec=grid_spec, out_shape=[jax.ShapeDtypeStruct((2 * h, cols), F32)] * 4,
        compiler_params=_params("parallel"),
    )(core.reshape(1).astype(jnp.int32), mine, theirs, w, m, v)


def _adamw_math(g, w_v, m_v, v_v):
    m2 = ADAM_B1 * m_v + (1.0 - ADAM_B1) * g
    v2 = ADAM_B2 * v_v + (1.0 - ADAM_B2) * jnp.square(g)
    m_hat = m2 / (1.0 - ADAM_B1 ** ADAM_STEP)
    v_hat = v2 / (1.0 - ADAM_B2 ** ADAM_STEP)
    delta = -ADAM_LR * (m_hat / (jnp.sqrt(v_hat) + ADAM_EPS) + ADAM_WD * w_v)
    return [g, delta, m2, v2]


def _adamw(name, g, w, m, v):
    return _elementwise(name, _adamw_math, [g, w, m, v], [F32] * 4)


_TRANSPOSED = ("ffn1_gate", "ffn1_up", "ffn2_gate", "ffn2_up")
_MATRICES = (("ffn1_gate", 0), ("ffn1_up", 0), ("ffn1_down", 0), ("w_in", 1), ("w_out", 0),
             ("ffn2_gate", 0), ("ffn2_up", 0), ("ffn2_down", 0))


def _block2d(a, name):
    return jnp.swapaxes(a, 1, 2)[0] if name in _TRANSPOSED else a[0]


def _block3d(a, name):
    return jnp.swapaxes(a[None], 1, 2) if name in _TRANSPOSED else a[None]
_VECTORS = ("ln1_g", "ln1_b", "conv_b", "dt_bias", "a_log", "d_skip", "attn_norm_w", "ssd_norm_w",
            "ln2_g", "ln2_b", "ln3_g", "ln3_b")
_WEIGHT_ORDER = ("ln1_g", "ln1_b", "ffn1_gate", "ffn1_up", "ffn1_down", "w_in", "conv_w", "conv_b", "dt_bias", "a_log",
                 "d_skip", "attn_norm_w", "ssd_norm_w", "w_out", "ln2_g", "ln2_b", "ffn2_gate", "ffn2_up", "ffn2_down",
                 "ln3_g", "ln3_b")


def _pack_rows(vectors):
    parts = []
    for vec in vectors:
        flat = vec.reshape(-1)
        parts.append(jnp.pad(flat, (0, (-flat.shape[0]) % LANES)))
    flat = jnp.concatenate(parts)
    flat = jnp.pad(flat, (0, (-flat.shape[0]) % (8 * LANES)))
    return flat.reshape(-1, LANES)


def _unpack_rows(packed, shapes):
    flat = packed.reshape(-1)
    out, off = [], 0
    for shape in shapes:
        size = int(np.prod(shape))
        out.append(flat[off:off + size].reshape(shape))
        off += size + (-size) % LANES
    return out


def _assemble(stack, axis):
    if axis == 0:
        return stack.reshape(-1, stack.shape[2])
    return jnp.concatenate([stack[s] for s in range(N_CHIPS)], axis=1)


def _split(full, axis):
    if axis == 0:
        return full.reshape(N_CHIPS, -1, full.shape[1])
    cols = full.shape[1] // N_CHIPS
    return jnp.stack([full[:, cols * s:cols * (s + 1)] for s in range(N_CHIPS)])


def kernel(x, positions, ln1_g, ln1_b, ffn1_gate, ffn1_up, ffn1_down, w_in, conv_w, conv_b, dt_bias, a_log, d_skip, attn_norm_w, ssd_norm_w, w_out, ln2_g, ln2_b, ffn2_gate, ffn2_up, ffn2_down, ln3_g, ln3_b, loss_target, m_ln1_g, m_ln1_b, m_ffn1_gate, m_ffn1_up, m_ffn1_down, m_w_in, m_conv_w, m_conv_b, m_dt_bias, m_a_log, m_d_skip, m_attn_norm_w, m_ssd_norm_w, m_w_out, m_ln2_g, m_ln2_b, m_ffn2_gate, m_ffn2_up, m_ffn2_down, m_ln3_g, m_ln3_b, v_ln1_g, v_ln1_b, v_ffn1_gate, v_ffn1_up, v_ffn1_down, v_w_in, v_conv_w, v_conv_b, v_dt_bias, v_a_log, v_d_skip, v_attn_norm_w, v_ssd_norm_w, v_w_out, v_ln2_g, v_ln2_b, v_ffn2_gate, v_ffn2_up, v_ffn2_down, v_ln3_g, v_ln3_b):
    given = dict(locals())
    wts = {n: given[n] for n in _WEIGHT_ORDER}
    mom_m = {n: given["m_" + n] for n in _WEIGHT_ORDER}
    mom_v = {n: given["v_" + n] for n in _WEIGHT_ORDER}
    chip = 2 * lax.axis_index("x") + lax.axis_index("y")

    core = lax.axis_index("c")
    groups = [[(n, axis) for n, axis in _MATRICES if n.startswith(prefix)] for prefix in ("ffn1", "w_", "ffn2")]
    own16 = {n: _block2d(wts[n], n).astype(BF16) for n, _ in _MATRICES}

    def full_weights(group, results):
        out = {}
        for (n, axis), st in zip(group, results):
            whole = _assemble(st, axis)
            out[n] = whole.T if n in _TRANSPOSED else whole
        return out

    full = full_weights(groups[0], _run_job(_GatherJob([own16[n] for n, _ in groups[0]]), "gather_ffn1"))
    for n in _VECTORS:
        full[n] = wts[n]
    conv_rows = jnp.pad(wts["conv_w"][0], ((0, 32 - CONV_WIDTH), (0, 0)))

    def mixer_weights(results):
        out = full_weights(groups[1], results)
        out["conv_w"] = _assemble(results[-1], 1)[:CONV_WIDTH]
        return out

    def ffn2_weights(results):
        return full_weights(groups[2], results)

    late = [(_GatherJob([own16[n] for n, _ in groups[1]] + [conv_rows]), mixer_weights),
            (_GatherJob([own16[n] for n, _ in groups[2]]), ffn2_weights)]

    chip_sums = {}

    def core_sums(g, which, tag):
        partials = [_split(g[n], axis) for n, axis in which]
        from_sibling = _sibling_halves([p.astype(BF16) for p in partials], "sibling_halves_" + tag)
        for (n, _), p, o in zip(which, partials, from_sibling):
            chip_sums[n] = _half_sum("core_sum_" + n, p, o, core)
        return _ExchangeJob([chip_sums[n] for n, _ in which])

    last = [(n, axis) for n, axis in _MATRICES if n in ("ffn1_gate", "ffn1_up")]
    early = [(n, axis) for n, axis in _MATRICES if (n, axis) not in last]
    loss, grad_x, g, received_early = _local_step(x, positions, loss_target, full, late,
                                                  lambda g_now: core_sums(g_now, early, "early"))
    received_last = _run_job(core_sums(g, last, "last"), "exchange_last")
    received = dict(zip([n for n, _ in last + early], received_last + received_early))
    half_totals = [_sum_slots("sum_partials_" + n, received[n], chip_sums[n], chip) for n, _ in _MATRICES]
    other_halves = _sibling_swap(half_totals)

    small_shapes = [g[n].shape for n in _VECTORS] + [g["conv_w"].shape, (1,)]
    total = _small_allreduce(_pack_rows([g[n] for n in _VECTORS] + [g["conv_w"], loss[0, :1]]))
    small = _unpack_rows(total, small_shapes)
    loss_out = small[-1].reshape(())

    grads, deltas, new_m, new_v = {}, {}, {}, {}
    for (n, _), mine, theirs in zip(_MATRICES, half_totals, other_halves):
        res = _adamw_halves("adamw_" + n, mine, theirs, core, _block2d(wts[n], n), _block2d(mom_m[n], n), _block2d(mom_v[n], n))
        grads[n], deltas[n], new_m[n], new_v[n] = [_block3d(r, n) for r in res]

    vec_shapes = [wts[n].shape for n in _VECTORS]
    res = _adamw("adamw_vectors", _pack_rows(small[:len(_VECTORS)]), _pack_rows([wts[n] for n in _VECTORS]),
                 _pack_rows([mom_m[n] for n in _VECTORS]), _pack_rows([mom_v[n] for n in _VECTORS]))
    for dst, packed in zip((grads, deltas, new_m, new_v), res):
        for n, val in zip(_VECTORS, _unpack_rows(packed, vec_shapes)):
            dst[n] = val

    cols = conv_w.shape[2]
    g_conv = lax.dynamic_slice_in_dim(small[len(_VECTORS)], chip * cols, cols, axis=1)
    res = _adamw("adamw_conv_w", g_conv, wts["conv_w"][0], mom_m["conv_w"][0], mom_v["conv_w"][0])
    grads["conv_w"], deltas["conv_w"], new_m["conv_w"], new_v["conv_w"] = [r[None] for r in res]

    return (loss_out, grad_x, *[grads[n] for n in _WEIGHT_ORDER], *[deltas[n] for n in _WEIGHT_ORDER],
            *[new_m[n] for n in _WEIGHT_ORDER], *[new_v[n] for n in _WEIGHT_ORDER])
```
